```python
import jax, jax.numpy as jnp
from jax import lax
import numpy as np

D_MODEL = 2048
BATCH = 8
SEQ = 4096
DEPTH = 1

A_HEAD_DIM = 128
A_WIDTH = D_MODEL // 2
A_HEADS = A_WIDTH // A_HEAD_DIM
A_CHUNK = 64
B_HEAD_DIM = 64
B_WIDTH = D_MODEL // 2
B_Q_HEADS = B_WIDTH // B_HEAD_DIM
B_GROUP = 4
B_KV_HEADS = B_Q_HEADS // B_GROUP
B_KV_WIDTH = B_KV_HEADS * B_HEAD_DIM
WINDOW = 128
BLOCK = 128
MLP_HIDDEN = 4 * D_MODEL
N_MOD = 6
EPS = 1e-6

SPLIT_SIZES = (A_WIDTH, A_WIDTH, A_WIDTH, A_WIDTH,
               B_WIDTH, B_KV_WIDTH, B_KV_WIDTH,
               D_MODEL, D_MODEL)
IN_WIDTH = 4 * A_WIDTH + B_WIDTH + 2 * B_KV_WIDTH + 2 * D_MODEL

kernel_name = "hybrid_hgrn2_swa_sink_gated_block"


def split_columns(t):
    idx, acc = [], 0
    for s in SPLIT_SIZES[:-1]:
        acc += s
        idx.append(acc)
    return jnp.split(t, idx, axis=-1)


def rms_norm(x, gain):
    xf = x.astype(jnp.float32)
    y = xf * lax.rsqrt(jnp.mean(xf * xf, axis=-1, keepdims=True) + EPS)
    return (y * gain.astype(jnp.float32)).astype(x.dtype)


def head_rms(t, gain):
    return t * lax.rsqrt(jnp.mean(t * t, axis=-1, keepdims=True) + EPS) * gain.astype(jnp.float32)


def hgrn2_mixer(q, f_logit, i, g, lb, o_gain):
    f32 = jnp.float32
    bsz, seq, _ = q.shape
    H, K, C = A_HEADS, A_HEAD_DIM, A_CHUNK
    n = seq // C
    lbf = lb.astype(f32)
    f = lbf + (1.0 - lbf) * jax.nn.sigmoid(f_logit.astype(f32))
    log_f = jnp.log(f)
    k = 1.0 - f
    qf = jax.nn.silu(q.astype(f32))

    def to_chunks(t):
        return t.reshape(bsz, n, C, H, K).transpose(0, 3, 1, 2, 4)

    qc, kc, vc, lfc = (to_chunks(t) for t in (qf, k, i.astype(f32), log_f))
    b = jnp.cumsum(lfc, axis=3)
    b_mid = b[:, :, :, C // 2 - 1:C // 2, :]
    b_last = b[:, :, :, C - 1:C, :]
    q_dec = qc * jnp.exp(b - b_mid)
    k_dec = kc * jnp.exp(b_mid - b)
    causal = jnp.tril(jnp.ones((C, C), dtype=bool))
    scores = jnp.where(causal, jnp.einsum('bhntk,bhnsk->bhnts', q_dec, k_dec), 0.0)
    o_intra = jnp.einsum('bhnts,bhnsv->bhntv', scores, vc)
    d_state = jnp.einsum('bhnsk,bhnsv->bhnkv', kc * jnp.exp(b_last - b), vc)
    chunk_decay = jnp.exp(b_last[:, :, :, 0, :])

    def step(state, inp):
        ds, dec = inp
        return dec[..., None] * state + ds, state

    s0 = jnp.zeros((bsz, H, K, K), f32)
    _, s_prev = lax.scan(step, s0, (jnp.moveaxis(d_state, 2, 0), jnp.moveaxis(chunk_decay, 2, 0)))
    s_prev = jnp.moveaxis(s_prev, 0, 2)
    o_inter = jnp.einsum('bhntk,bhnkv->bhntv', qc * jnp.exp(b), s_prev)
    o = (o_intra + o_inter).transpose(0, 2, 3, 1, 4).reshape(bsz, seq, H, K)
    o = head_rms(o, o_gain.reshape(H, K)).reshape(bsz, seq, A_WIDTH)
    o = o * jax.nn.silu(g.astype(f32))
    return o.astype(q.dtype)


def swa_sink_attention(q, k, v, q_gain, k_gain, sinks):
    f32 = jnp.float32
    bsz, seq, _ = q.shape
    nb = seq // BLOCK
    qh = head_rms(q.astype(f32).reshape(bsz, seq, B_KV_HEADS, B_GROUP, B_HEAD_DIM), q_gain)
    kh = head_rms(k.astype(f32).reshape(bsz, seq, B_KV_HEADS, B_HEAD_DIM), k_gain)
    vh = v.astype(f32).reshape(bsz, seq, B_KV_HEADS, B_HEAD_DIM)
    qb = qh.reshape(bsz, nb, BLOCK, B_KV_HEADS, B_GROUP, B_HEAD_DIM)
    kb = kh.reshape(bsz, nb, BLOCK, B_KV_HEADS, B_HEAD_DIM)
    vb = vh.reshape(bsz, nb, BLOCK, B_KV_HEADS, B_HEAD_DIM)

    def with_prev(t):
        prev = jnp.concatenate([jnp.zeros_like(t[:, :1]), t[:, :-1]], axis=1)
        return jnp.concatenate([prev, t], axis=2)

    kw, vw = with_prev(kb), with_prev(vb)
    scale = B_HEAD_DIM ** -0.5
    scores = jnp.einsum('bnqhgd,bnkhd->bnhgqk', qb, kw) * scale
    qi = jnp.arange(BLOCK)[:, None] + BLOCK
    ki = jnp.arange(2 * BLOCK)[None, :]
    rel = qi - ki
    band = (rel >= 0) & (rel < WINDOW)
    has_key = (jnp.arange(nb) > 0)[:, None, None] | (ki >= BLOCK)[None]
    mask = band[None] & has_key
    scores = jnp.where(mask[None, :, None, None], scores, -jnp.inf)
    sink = jnp.broadcast_to(sinks.astype(f32).reshape(B_KV_HEADS, B_GROUP)[None, None, :, :, None, None],
                            scores.shape[:-1] + (1,))
    probs = jax.nn.softmax(jnp.concatenate([scores, sink], axis=-1), axis=-1)[..., :-1]
    out = jnp.einsum('bnhgqk,bnkhd->bnqhgd', probs, vw)
    return out.reshape(bsz, seq, B_WIDTH).astype(q.dtype)


def _fwd_setup_inputs(seed: int = 0) -> dict:
    key = jax.random.key(seed)
    ks = jax.random.split(key, 20)
    f32 = jnp.float32

    def w(k, shape, fan_in):
        return jax.random.normal(k, shape, f32) * (fan_in ** -0.5)

    return {
        "x": jax.random.normal(ks[0], (BATCH, SEQ, D_MODEL), f32),
        "c": jax.random.normal(ks[1], (BATCH, D_MODEL), f32),
        "w_ada": w(ks[2], (DEPTH, D_MODEL, N_MOD * D_MODEL), D_MODEL),
        "b_ada": 0.02 * jax.random.normal(ks[3], (DEPTH, N_MOD * D_MODEL), f32),
        "norm1_gain": 1.0 + 0.02 * jax.random.normal(ks[4], (DEPTH, D_MODEL), f32),
        "w_in": w(ks[5], (DEPTH, D_MODEL, IN_WIDTH), D_MODEL),
        "lb_logits": 0.5 * jax.random.normal(ks[6], (DEPTH + 1, A_WIDTH), f32),
        "hgrn_o_gain": 1.0 + 0.02 * jax.random.normal(ks[7], (DEPTH, A_WIDTH), f32),
        "q_norm_gain": 1.0 + 0.02 * jax.random.normal(ks[8], (DEPTH, B_HEAD_DIM), f32),
        "k_norm_gain": 1.0 + 0.02 * jax.random.normal(ks[9], (DEPTH, B_HEAD_DIM), f32),
        "sinks": 0.5 * jax.random.normal(ks[10], (DEPTH, B_Q_HEADS), f32),
        "w_branch_a": w(ks[11], (DEPTH, A_WIDTH, D_MODEL), A_WIDTH),
        "w_branch_b": w(ks[12], (DEPTH, B_WIDTH, D_MODEL), B_WIDTH),
        "w_out": w(ks[13], (DEPTH, D_MODEL, D_MODEL), D_MODEL),
        "norm2_gain": 1.0 + 0.02 * jax.random.normal(ks[14], (DEPTH, D_MODEL), f32),
        "w_mlp_in": w(ks[15], (DEPTH, D_MODEL, MLP_HIDDEN), D_MODEL),
        "w_mlp_out": w(ks[16], (DEPTH, MLP_HIDDEN, D_MODEL), MLP_HIDDEN),
    }


def _fwd_reference(x, c, w_ada, b_ada, norm1_gain, w_in, lb_logits, hgrn_o_gain, q_norm_gain,
              k_norm_gain, sinks, w_branch_a, w_branch_b, w_out, norm2_gain, w_mlp_in, w_mlp_out):
    lb_all = jnp.cumsum(jax.nn.softmax(lb_logits.astype(jnp.float32), axis=0), axis=0)
    for l in range(DEPTH):
        mod = jax.nn.silu(c) @ w_ada[l] + b_ada[l]
        sh1, sc1, gt1, sh2, sc2, gt2 = (m[:, None, :] for m in jnp.split(mod, N_MOD, axis=-1))
        h = rms_norm(x, norm1_gain[l]) * (1.0 + sc1) + sh1
        qa, fa, ia, ga, qb, kb, vb, gate_a, gate_b = split_columns(h @ w_in[l])
        ya = hgrn2_mixer(qa, fa, ia, ga, lb_all[l], hgrn_o_gain[l]) @ w_branch_a[l]
        yb = swa_sink_attention(qb, kb, vb, q_norm_gain[l], k_norm_gain[l], sinks[l]) @ w_branch_b[l]
        merged = jax.nn.sigmoid(gate_a) * ya + jax.nn.sigmoid(gate_b) * yb
        x = x + gt1 * (merged @ w_out[l])
        h2 = rms_norm(x, norm2_gain[l]) * (1.0 + sc2) + sh2
        x = x + gt2 * (jnp.square(jax.nn.relu(h2 @ w_mlp_in[l])) @ w_mlp_out[l])
    return x


import jax as _jax
import jax.numpy as _jnp

TWIN_FORMAT = 'train_step'
FWD_PARAMS = ['x', 'c', 'w_ada', 'b_ada', 'norm1_gain', 'w_in', 'lb_logits', 'hgrn_o_gain', 'q_norm_gain', 'k_norm_gain', 'sinks', 'w_branch_a', 'w_branch_b', 'w_out', 'norm2_gain', 'w_mlp_in', 'w_mlp_out']
TWIN_WEIGHTS = ['w_ada', 'b_ada', 'norm1_gain', 'w_in', 'lb_logits', 'hgrn_o_gain', 'q_norm_gain', 'k_norm_gain', 'sinks', 'w_branch_a', 'w_branch_b', 'w_out', 'norm2_gain', 'w_mlp_in', 'w_mlp_out']
TWIN_DIFF_INPUT = 'x'
TWIN_INPUTS = ['x', 'c', 'w_ada', 'b_ada', 'norm1_gain', 'w_in', 'lb_logits', 'hgrn_o_gain', 'q_norm_gain', 'k_norm_gain', 'sinks', 'w_branch_a', 'w_branch_b', 'w_out', 'norm2_gain', 'w_mlp_in', 'w_mlp_out', 'loss_target', 'm_w_ada', 'm_b_ada', 'm_norm1_gain', 'm_w_in', 'm_lb_logits', 'm_hgrn_o_gain', 'm_q_norm_gain', 'm_k_norm_gain', 'm_sinks', 'm_w_branch_a', 'm_w_branch_b', 'm_w_out', 'm_norm2_gain', 'm_w_mlp_in', 'm_w_mlp_out', 'v_w_ada', 'v_b_ada', 'v_norm1_gain', 'v_w_in', 'v_lb_logits', 'v_hgrn_o_gain', 'v_q_norm_gain', 'v_k_norm_gain', 'v_sinks', 'v_w_branch_a', 'v_w_branch_b', 'v_w_out', 'v_norm2_gain', 'v_w_mlp_in', 'v_w_mlp_out']
TWIN_OUTPUTS = ['loss', 'grad_x', 'grad_w_ada', 'grad_b_ada', 'grad_norm1_gain', 'grad_w_in', 'grad_lb_logits', 'grad_hgrn_o_gain', 'grad_q_norm_gain', 'grad_k_norm_gain', 'grad_sinks', 'grad_w_branch_a', 'grad_w_branch_b', 'grad_w_out', 'grad_norm2_gain', 'grad_w_mlp_in', 'grad_w_mlp_out', 'delta_w_ada', 'delta_b_ada', 'delta_norm1_gain', 'delta_w_in', 'delta_lb_logits', 'delta_hgrn_o_gain', 'delta_q_norm_gain', 'delta_k_norm_gain', 'delta_sinks', 'delta_w_branch_a', 'delta_w_branch_b', 'delta_w_out', 'delta_norm2_gain', 'delta_w_mlp_in', 'delta_w_mlp_out', 'new_m_w_ada', 'new_m_b_ada', 'new_m_norm1_gain', 'new_m_w_in', 'new_m_lb_logits', 'new_m_hgrn_o_gain', 'new_m_q_norm_gain', 'new_m_k_norm_gain', 'new_m_sinks', 'new_m_w_branch_a', 'new_m_w_branch_b', 'new_m_w_out', 'new_m_norm2_gain', 'new_m_w_mlp_in', 'new_m_w_mlp_out', 'new_v_w_ada', 'new_v_b_ada', 'new_v_norm1_gain', 'new_v_w_in', 'new_v_lb_logits', 'new_v_hgrn_o_gain', 'new_v_q_norm_gain', 'new_v_k_norm_gain', 'new_v_sinks', 'new_v_w_branch_a', 'new_v_w_branch_b', 'new_v_w_out', 'new_v_norm2_gain', 'new_v_w_mlp_in', 'new_v_w_mlp_out']
TWIN_LEAF_KINDS = {'loss': 'loss', 'grad_x': 'grad_x', 'grad_w_ada': 'grad_w', 'grad_b_ada': 'grad_w', 'grad_norm1_gain': 'grad_w', 'grad_w_in': 'grad_w', 'grad_lb_logits': 'grad_w', 'grad_hgrn_o_gain': 'grad_w', 'grad_q_norm_gain': 'grad_w', 'grad_k_norm_gain': 'grad_w', 'grad_sinks': 'grad_w', 'grad_w_branch_a': 'grad_w', 'grad_w_branch_b': 'grad_w', 'grad_w_out': 'grad_w', 'grad_norm2_gain': 'grad_w', 'grad_w_mlp_in': 'grad_w', 'grad_w_mlp_out': 'grad_w', 'delta_w_ada': 'delta_w', 'delta_b_ada': 'delta_w', 'delta_norm1_gain': 'delta_w', 'delta_w_in': 'delta_w', 'delta_lb_logits': 'delta_w', 'delta_hgrn_o_gain': 'delta_w', 'delta_q_norm_gain': 'delta_w', 'delta_k_norm_gain': 'delta_w', 'delta_sinks': 'delta_w', 'delta_w_branch_a': 'delta_w', 'delta_w_branch_b': 'delta_w', 'delta_w_out': 'delta_w', 'delta_norm2_gain': 'delta_w', 'delta_w_mlp_in': 'delta_w', 'delta_w_mlp_out': 'delta_w', 'new_m_w_ada': 'new_m', 'new_m_b_ada': 'new_m', 'new_m_norm1_gain': 'new_m', 'new_m_w_in': 'new_m', 'new_m_lb_logits': 'new_m', 'new_m_hgrn_o_gain': 'new_m', 'new_m_q_norm_gain': 'new_m', 'new_m_k_norm_gain': 'new_m', 'new_m_sinks': 'new_m', 'new_m_w_branch_a': 'new_m', 'new_m_w_branch_b': 'new_m', 'new_m_w_out': 'new_m', 'new_m_norm2_gain': 'new_m', 'new_m_w_mlp_in': 'new_m', 'new_m_w_mlp_out': 'new_m', 'new_v_w_ada': 'new_v', 'new_v_b_ada': 'new_v', 'new_v_norm1_gain': 'new_v', 'new_v_w_in': 'new_v', 'new_v_lb_logits': 'new_v', 'new_v_hgrn_o_gain': 'new_v', 'new_v_q_norm_gain': 'new_v', 'new_v_k_norm_gain': 'new_v', 'new_v_sinks': 'new_v', 'new_v_w_branch_a': 'new_v', 'new_v_w_branch_b': 'new_v', 'new_v_w_out': 'new_v', 'new_v_norm2_gain': 'new_v', 'new_v_w_mlp_in': 'new_v', 'new_v_w_mlp_out': 'new_v'}


def _forward(args):
    return _fwd_reference(*[args[k] for k in FWD_PARAMS])


def _output_shape():
    def fwd():
        inp = _fwd_setup_inputs(0)
        return _fwd_reference(*[inp[k] for k in FWD_PARAMS])
    out = _jax.eval_shape(fwd)
    return out.shape, out.dtype

N_MICROBATCH = 1
ADAM_LR = 0.001
ADAM_B1 = 0.9
ADAM_B2 = 0.999
ADAM_EPS = 1e-08
ADAM_WD = 0.01
ADAM_STEP = 10
PER_EXAMPLE_BATCH_AXIS = {'x': 0, 'c': 0, 'loss_target': 0}
SHARED_INPUTS = []
_WEIGHT_DTYPES = {'w_ada': _jnp.float32, 'b_ada': _jnp.float32, 'norm1_gain': _jnp.float32, 'w_in': _jnp.float32, 'lb_logits': _jnp.float32, 'hgrn_o_gain': _jnp.float32, 'q_norm_gain': _jnp.float32, 'k_norm_gain': _jnp.float32, 'sinks': _jnp.float32, 'w_branch_a': _jnp.float32, 'w_branch_b': _jnp.float32, 'w_out': _jnp.float32, 'norm2_gain': _jnp.float32, 'w_mlp_in': _jnp.float32, 'w_mlp_out': _jnp.float32}
MOMENT_SCALE = {'w_ada': 6.820097e+00, 'b_ada': 1.778998e+01, 'norm1_gain': 1.442550e+00, 'w_in': 1.492683e+00, 'lb_logits': 2.260412e-02, 'hgrn_o_gain': 3.534554e+00, 'q_norm_gain': 1.049345e+00, 'k_norm_gain': 1.051136e+00, 'sinks': 5.519988e-01, 'w_branch_a': 1.470387e+00, 'w_branch_b': 2.528922e+00, 'w_out': 2.898611e+00, 'norm2_gain': 4.618428e+01, 'w_mlp_in': 2.807984e+00, 'w_mlp_out': 6.552319e+00}


def _to_microbatches(a, axis):
    t = _jnp.moveaxis(a, axis, 0)
    t = t.reshape((N_MICROBATCH, t.shape[0] // N_MICROBATCH) + t.shape[1:])
    return _jnp.moveaxis(t, 1, axis + 1)


def setup_inputs(seed: int = 0) -> dict:
    inp = _fwd_setup_inputs(seed)
    key = _jax.random.fold_in(_jax.random.key(seed), 7919)
    shape, _ = _output_shape()
    out = dict(inp)
    out["loss_target"] = _jax.random.normal(_jax.random.fold_in(key, 0), shape, _jnp.float32)
    for i, name in enumerate(TWIN_WEIGHTS):
        w = inp[name].astype(_jnp.float32)
        if MOMENT_SCALE is None:
            s = _jnp.sqrt(_jnp.mean(_jnp.square(w)) + 1e-30)
        else:
            s = MOMENT_SCALE[name]
        km, kv = _jax.random.split(_jax.random.fold_in(key, i + 1))
        out[name] = w
        out["m_" + name] = s * _jax.random.normal(km, w.shape, _jnp.float32)
        out["v_" + name] = (s * s) * _jax.random.uniform(kv, w.shape, _jnp.float32, 0.5, 1.5)
    if N_MICROBATCH > 1:
        for name, axis in PER_EXAMPLE_BATCH_AXIS.items():
            out[name] = _to_microbatches(out[name], axis)
    return {'x': out['x'], 'c': out['c'], 'w_ada': out['w_ada'], 'b_ada': out['b_ada'], 'norm1_gain': out['norm1_gain'], 'w_in': out['w_in'], 'lb_logits': out['lb_logits'], 'hgrn_o_gain': out['hgrn_o_gain'], 'q_norm_gain': out['q_norm_gain'], 'k_norm_gain': out['k_norm_gain'], 'sinks': out['sinks'], 'w_branch_a': out['w_branch_a'], 'w_branch_b': out['w_branch_b'], 'w_out': out['w_out'], 'norm2_gain': out['norm2_gain'], 'w_mlp_in': out['w_mlp_in'], 'w_mlp_out': out['w_mlp_out'], 'loss_target': out['loss_target'], 'm_w_ada': out['m_w_ada'], 'm_b_ada': out['m_b_ada'], 'm_norm1_gain': out['m_norm1_gain'], 'm_w_in': out['m_w_in'], 'm_lb_logits': out['m_lb_logits'], 'm_hgrn_o_gain': out['m_hgrn_o_gain'], 'm_q_norm_gain': out['m_q_norm_gain'], 'm_k_norm_gain': out['m_k_norm_gain'], 'm_sinks': out['m_sinks'], 'm_w_branch_a': out['m_w_branch_a'], 'm_w_branch_b': out['m_w_branch_b'], 'm_w_out': out['m_w_out'], 'm_norm2_gain': out['m_norm2_gain'], 'm_w_mlp_in': out['m_w_mlp_in'], 'm_w_mlp_out': out['m_w_mlp_out'], 'v_w_ada': out['v_w_ada'], 'v_b_ada': out['v_b_ada'], 'v_norm1_gain': out['v_norm1_gain'], 'v_w_in': out['v_w_in'], 'v_lb_logits': out['v_lb_logits'], 'v_hgrn_o_gain': out['v_hgrn_o_gain'], 'v_q_norm_gain': out['v_q_norm_gain'], 'v_k_norm_gain': out['v_k_norm_gain'], 'v_sinks': out['v_sinks'], 'v_w_branch_a': out['v_w_branch_a'], 'v_w_branch_b': out['v_w_branch_b'], 'v_w_out': out['v_w_out'], 'v_norm2_gain': out['v_norm2_gain'], 'v_w_mlp_in': out['v_w_mlp_in'], 'v_w_mlp_out': out['v_w_mlp_out']}


def _loss(weights, diff, rest, loss_target):
    with _jax.named_scope("forward"):
        args = {**rest, TWIN_DIFF_INPUT: diff, **{k: w.astype(_WEIGHT_DTYPES[k]) for k, w in weights.items()}}
        y = _forward(args)
    with _jax.named_scope("loss_head"):
        err = _jnp.square(y.astype(_jnp.float32) - loss_target)
        return 0.5 * _jnp.sum(_jnp.mean(err, axis=-1)) if err.ndim else 0.5 * err


def _adamw(w, g, m, v):
    m = ADAM_B1 * m + (1.0 - ADAM_B1) * g
    v = ADAM_B2 * v + (1.0 - ADAM_B2) * _jnp.square(g)
    m_hat = m / (1.0 - ADAM_B1 ** ADAM_STEP)
    v_hat = v / (1.0 - ADAM_B2 ** ADAM_STEP)
    delta = -ADAM_LR * (m_hat / (_jnp.sqrt(v_hat) + ADAM_EPS) + ADAM_WD * w)
    return delta, m, v


def reference(x, c, w_ada, b_ada, norm1_gain, w_in, lb_logits, hgrn_o_gain, q_norm_gain, k_norm_gain, sinks, w_branch_a, w_branch_b, w_out, norm2_gain, w_mlp_in, w_mlp_out, loss_target, m_w_ada, m_b_ada, m_norm1_gain, m_w_in, m_lb_logits, m_hgrn_o_gain, m_q_norm_gain, m_k_norm_gain, m_sinks, m_w_branch_a, m_w_branch_b, m_w_out, m_norm2_gain, m_w_mlp_in, m_w_mlp_out, v_w_ada, v_b_ada, v_norm1_gain, v_w_in, v_lb_logits, v_hgrn_o_gain, v_q_norm_gain, v_k_norm_gain, v_sinks, v_w_branch_a, v_w_branch_b, v_w_out, v_norm2_gain, v_w_mlp_in, v_w_mlp_out):
    given = dict(x=x, c=c, w_ada=w_ada, b_ada=b_ada, norm1_gain=norm1_gain, w_in=w_in, lb_logits=lb_logits, hgrn_o_gain=hgrn_o_gain, q_norm_gain=q_norm_gain, k_norm_gain=k_norm_gain, sinks=sinks, w_branch_a=w_branch_a, w_branch_b=w_branch_b, w_out=w_out, norm2_gain=norm2_gain, w_mlp_in=w_mlp_in, w_mlp_out=w_mlp_out, loss_target=loss_target, m_w_ada=m_w_ada, m_b_ada=m_b_ada, m_norm1_gain=m_norm1_gain, m_w_in=m_w_in, m_lb_logits=m_lb_logits, m_hgrn_o_gain=m_hgrn_o_gain, m_q_norm_gain=m_q_norm_gain, m_k_norm_gain=m_k_norm_gain, m_sinks=m_sinks, m_w_branch_a=m_w_branch_a, m_w_branch_b=m_w_branch_b, m_w_out=m_w_out, m_norm2_gain=m_norm2_gain, m_w_mlp_in=m_w_mlp_in, m_w_mlp_out=m_w_mlp_out, v_w_ada=v_w_ada, v_b_ada=v_b_ada, v_norm1_gain=v_norm1_gain, v_w_in=v_w_in, v_lb_logits=v_lb_logits, v_hgrn_o_gain=v_hgrn_o_gain, v_q_norm_gain=v_q_norm_gain, v_k_norm_gain=v_k_norm_gain, v_sinks=v_sinks, v_w_branch_a=v_w_branch_a, v_w_branch_b=v_w_branch_b, v_w_out=v_w_out, v_norm2_gain=v_norm2_gain, v_w_mlp_in=v_w_mlp_in, v_w_mlp_out=v_w_mlp_out)
    weights = {n: given[n] for n in TWIN_WEIGHTS}
    shared = {n: given[n] for n in SHARED_INPUTS}
    per_example = {n: given[n] for n in ['x', 'c']}
    grad_fn = _jax.value_and_grad(_loss, argnums=(0, 1))

    def one_microbatch(ex, loss_target):
        ex = dict(ex)
        diff = ex.pop(TWIN_DIFF_INPUT)
        return grad_fn(weights, diff, {**shared, **ex}, loss_target)

    if N_MICROBATCH == 1:
        loss, (grad_w, grad_x) = one_microbatch(per_example, given["loss_target"])
    else:
        def body(carry, xs):
            loss_sum, grad_sum = carry
            l_k, (gw_k, gx_k) = one_microbatch(xs[0], xs[1])
            with _jax.named_scope("update"):
                return (loss_sum + l_k, _jax.tree.map(_jnp.add, grad_sum, gw_k)), gx_k

        init = (_jnp.zeros((), _jnp.float32), _jax.tree.map(_jnp.zeros_like, weights))
        (loss, grad_w), grad_x = _jax.lax.scan(body, init, (per_example, given["loss_target"]))
    with _jax.named_scope("update"):
        delta_w, new_m, new_v = {}, {}, {}
        for n in TWIN_WEIGHTS:
            delta_w[n], new_m[n], new_v[n] = _adamw(weights[n], grad_w[n], given["m_" + n], given["v_" + n])
    return (loss, grad_x, *[grad_w[n] for n in TWIN_WEIGHTS], *[delta_w[n] for n in TWIN_WEIGHTS],
            *[new_m[n] for n in TWIN_WEIGHTS], *[new_v[n] for n in TWIN_WEIGHTS])
```

```python
import functools

import jax
import jax.numpy as jnp
from jax import lax
from jax.experimental import pallas as pl
from jax.experimental.pallas import tpu as pltpu

F32 = jnp.float32
BF16 = jnp.bfloat16
HIGHEST = lax.Precision.HIGHEST
MESH = pl.DeviceIdType.MESH

D_MODEL = 2048
A_WIDTH = 1024
A_HEADS = 8
A_HEAD_DIM = 128
A_CHUNK = 64
B_WIDTH = 1024
B_HEAD_DIM = 64
B_GROUP = 4
B_KV_HEADS = 4
B_KV_WIDTH = 256
BLOCK = 128
MLP_HIDDEN = 8192
IN_WIDTH = 9728
N_MOD = 6
EPS = 1e-6
N_CHIPS = 4
N_DEV = 8

OFF_QA, OFF_FA, OFF_IA, OFF_GA = 0, 1024, 2048, 3072
OFF_QB, OFF_KB, OFF_VB = 4096, 5120, 5376
OFF_GATE_A, OFF_GATE_B = 5632, 7680

ADAM_LR = 0.001
ADAM_B1 = 0.9
ADAM_B2 = 0.999
ADAM_EPS = 1e-08
ADAM_WD = 0.01
ADAM_STEP = 10

VMEM_LIMIT_V7X = 48 * 1024 * 1024
NEG_BIG = -1e30


def _pcall(body, **kw):
    return pl.pallas_call(body, **kw)


def _params(sem=None, vmem=VMEM_LIMIT_V7X):
    return pltpu.CompilerParams(dimension_semantics=sem, vmem_limit_bytes=vmem)


def _sig(x):
    return 1.0 / (1.0 + jnp.exp(-x))


def _nn(a, b):
    return lax.dot_general(a.astype(BF16), b.astype(BF16), (((1,), (0,)), ((), ())), preferred_element_type=F32)


def _nt(a, b):
    return lax.dot_general(a.astype(BF16), b.astype(BF16), (((1,), (1,)), ((), ())), preferred_element_type=F32)


def _tn(a, b):
    return lax.dot_general(a.astype(BF16), b.astype(BF16), (((0,), (0,)), ((), ())), preferred_element_type=F32)


def _mm(a, b, *, name, ta=False, tb=False, bm=1024, bn=1024, bk=2048, out_dtypes=(F32,), epi=None, extras=()):
    if ta:
        K, M = a.shape
    else:
        M, K = a.shape
    if tb:
        N, K2 = b.shape
    else:
        K2, N = b.shape
    bm, bn, bk = min(bm, M), min(bn, N), min(bk, K)
    assert K == K2 and M % bm == 0 and N % bn == 0 and K % bk == 0, (name, a.shape, b.shape)
    nk = K // bk
    a_spec = pl.BlockSpec((bk, bm), lambda i, j, k: (k, i)) if ta else pl.BlockSpec((bm, bk), lambda i, j, k: (i, k))
    b_spec = pl.BlockSpec((bn, bk), lambda i, j, k: (j, k)) if tb else pl.BlockSpec((bk, bn), lambda i, j, k: (k, j))
    t_spec = pl.BlockSpec((bm, bn), lambda i, j, k: (i, j))
    dims = (((0 if ta else 1,), (1 if tb else 0,)), ((), ()))
    n_e, n_o = len(extras), len(out_dtypes)

    def body(*refs):
        a_ref, b_ref = refs[0], refs[1]
        e_refs = refs[2:2 + n_e]
        o_refs = refs[2 + n_e:2 + n_e + n_o]

        def finish(acc):
            outs = (acc,) if epi is None else epi(acc, *[e[...] for e in e_refs])
            for o_ref, o in zip(o_refs, outs):
                o_ref[...] = o.astype(o_ref.dtype)

        part = lax.dot_general(a_ref[...].astype(BF16), b_ref[...].astype(BF16), dims, preferred_element_type=F32)
        if nk == 1:
            finish(part)
        else:
            acc_ref = refs[-1]
            k = pl.program_id(2)

            @pl.when(k == 0)
            def _():
                acc_ref[...] = part

            @pl.when(k > 0)
            def _():
                acc_ref[...] += part

            @pl.when(k == nk - 1)
            def _():
                finish(acc_ref[...])

    out = _pcall(
        body, name=name, grid=(M // bm, N // bn, nk),
        in_specs=[a_spec, b_spec] + [t_spec] * n_e,
        out_specs=[t_spec] * n_o,
        out_shape=[jax.ShapeDtypeStruct((M, N), dt) for dt in out_dtypes],
        scratch_shapes=[pltpu.VMEM((bm, bn), F32)] if nk > 1 else [],
        compiler_params=_params(("parallel", "parallel", "arbitrary")),
    )(a, b, *extras)
    return out[0] if n_o == 1 else out


def _ada_fwd(c_all, w_ada, b_cols):
    n = w_ada.shape[1]
    bn = 512

    def body(c_ref, w_ref, b_ref, o_ref):
        cv = c_ref[...]
        sc = cv * _sig(cv)
        o_ref[...] = jnp.dot(sc, w_ref[...], precision=HIGHEST, preferred_element_type=F32) + b_ref[...]

    return _pcall(
        body, name="ada_fwd", grid=(n // bn,),
        in_specs=[pl.BlockSpec((N_DEV, D_MODEL), lambda j: (0, 0)), pl.BlockSpec((D_MODEL, bn), lambda j: (0, j)),
                  pl.BlockSpec((1, bn), lambda j: (0, j))],
        out_specs=pl.BlockSpec((N_DEV, bn), lambda j: (0, j)),
        out_shape=jax.ShapeDtypeStruct((N_DEV, n), F32),
        compiler_params=_params(("parallel",)),
    )(c_all, w_ada, b_cols)


ROWS_EW = 256


def _rms_fwd_math(x, gain, scale, shift):
    rstd = lax.rsqrt(jnp.mean(x * x, axis=-1, keepdims=True) + EPS)
    xhat = x * rstd
    n = xhat * gain
    return n * (1.0 + scale) + shift, xhat, n, rstd


def _rms_bwd_math(dh, xhat, n, rstd, gain, scale):
    dn = dh * (1.0 + scale)
    dxhat = dn * gain
    dx = rstd * (dxhat - xhat * jnp.mean(dxhat * xhat, axis=-1, keepdims=True))
    d_scale = jnp.sum(dh * n, axis=0, keepdims=True)
    d_shift = jnp.sum(dh, axis=0, keepdims=True)
    d_gain = jnp.sum(dn * xhat, axis=0, keepdims=True)
    return dx, d_scale, d_shift, d_gain


def _row_spec(w=D_MODEL, br=ROWS_EW):
    return pl.BlockSpec((br, w), lambda i: (i, 0))


def _vec_spec(r=8, w=D_MODEL):
    return pl.BlockSpec((r, w), lambda i: (0, 0))


def _norm1_fwd(x, gain, mod8):
    T = x.shape[0]

    def body(x_ref, g_ref, m_ref, h_ref):
        h, _, _, _ = _rms_fwd_math(x_ref[...], g_ref[...], m_ref[1:2, :], m_ref[0:1, :])
        h_ref[...] = h.astype(BF16)

    return _pcall(
        body, name="norm1_fwd", grid=(T // ROWS_EW,),
        in_specs=[_row_spec(), _vec_spec(1), _vec_spec()],
        out_specs=_row_spec(), out_shape=jax.ShapeDtypeStruct((T, D_MODEL), BF16),
        compiler_params=_params(("parallel",)),
    )(x, gain, mod8)


def _res_norm2_fwd(x, mo, gain, mod8):
    T = x.shape[0]

    def body(x_ref, mo_ref, g_ref, m_ref, x1_ref, h_ref):
        x1 = x_ref[...] + m_ref[2:3, :] * mo_ref[...]
        x1_ref[...] = x1
        h, _, _, _ = _rms_fwd_math(x1, g_ref[...], m_ref[4:5, :], m_ref[3:4, :])
        h_ref[...] = h.astype(BF16)

    return _pcall(
        body, name="res_norm2_fwd", grid=(T // ROWS_EW,),
        in_specs=[_row_spec(), _row_spec(), _vec_spec(1), _vec_spec()],
        out_specs=[_row_spec(), _row_spec()],
        out_shape=[jax.ShapeDtypeStruct((T, D_MODEL), F32), jax.ShapeDtypeStruct((T, D_MODEL), BF16)],
        compiler_params=_params(("parallel",)),
    )(x, mo, gain, mod8)


def _loss_bwd(x1, mlp, target, mod8):
    T = x1.shape[0]

    def body(x1_ref, mlp_ref, t_ref, m_ref, dy_ref, dmlp_ref, st_ref):
        i = pl.program_id(0)
        gate = m_ref[5:6, :]
        mlp_v = mlp_ref[...]
        err = x1_ref[...] + gate * mlp_v - t_ref[...]
        dy = err * (1.0 / D_MODEL)
        dy_ref[...] = dy
        dmlp_ref[...] = (dy * gate).astype(BF16)

        @pl.when(i == 0)
        def _():
            st_ref[...] = jnp.zeros_like(st_ref)

        st_ref[0:1, :] += jnp.sum(err * err, axis=0, keepdims=True)
        st_ref[1:2, :] += jnp.sum(dy * mlp_v, axis=0, keepdims=True)

    return _pcall(
        body, name="loss_bwd", grid=(T // ROWS_EW,),
        in_specs=[_row_spec(), _row_spec(), _row_spec(), _vec_spec()],
        out_specs=[_row_spec(), _row_spec(), _vec_spec()],
        out_shape=[jax.ShapeDtypeStruct((T, D_MODEL), F32), jax.ShapeDtypeStruct((T, D_MODEL), BF16),
                   jax.ShapeDtypeStruct((8, D_MODEL), F32)],
        compiler_params=_params(("arbitrary",)),
    )(x1, mlp, target, mod8)


def _norm2_bwd(dh2, x1, dy, mo, gain, mod8):
    T = x1.shape[0]

    def body(dh_ref, x1_ref, dy_ref, mo_ref, g_ref, m_ref, dx1_ref, dmo_ref, st_ref):
        i = pl.program_id(0)
        gain_v, scale = g_ref[...], m_ref[4:5, :]
        _, xhat, n, rstd = _rms_fwd_math(x1_ref[...], gain_v, scale, m_ref[3:4, :])
        dx, d_scale, d_shift, d_gain = _rms_bwd_math(dh_ref[...], xhat, n, rstd, gain_v, scale)
        dx1 = dy_ref[...] + dx
        dx1_ref[...] = dx1
        dmo_ref[...] = (dx1 * m_ref[2:3, :]).astype(BF16)

        @pl.when(i == 0)
        def _():
            st_ref[...] = jnp.zeros_like(st_ref)

        st_ref[0:1, :] += d_scale
        st_ref[1:2, :] += d_shift
        st_ref[2:3, :] += d_gain
        st_ref[3:4, :] += jnp.sum(dx1 * mo_ref[...], axis=0, keepdims=True)

    return _pcall(
        body, name="norm2_bwd", grid=(T // ROWS_EW,),
        in_specs=[_row_spec(), _row_spec(), _row_spec(), _row_spec(), _vec_spec(1), _vec_spec()],
        out_specs=[_row_spec(), _row_spec(), _vec_spec()],
        out_shape=[jax.ShapeDtypeStruct((T, D_MODEL), F32), jax.ShapeDtypeStruct((T, D_MODEL), BF16),
                   jax.ShapeDtypeStruct((8, D_MODEL), F32)],
        compiler_params=_params(("arbitrary",)),
    )(dh2, x1, dy, mo, gain, mod8)


def _norm1_bwd(dh, x, dx1, gain, mod8):
    T = x.shape[0]

    def body(dh_ref, x_ref, dx1_ref, g_ref, m_ref, dx_ref, st_ref):
        i = pl.program_id(0)
        gain_v, scale = g_ref[...], m_ref[1:2, :]
        _, xhat, n, rstd = _rms_fwd_math(x_ref[...], gain_v, scale, m_ref[0:1, :])
        dx, d_scale, d_shift, d_gain = _rms_bwd_math(dh_ref[...], xhat, n, rstd, gain_v, scale)
        dx_ref[...] = dx1_ref[...] + dx

        @pl.when(i == 0)
        def _():
            st_ref[...] = jnp.zeros_like(st_ref)

        st_ref[0:1, :] += d_scale
        st_ref[1:2, :] += d_shift
        st_ref[2:3, :] += d_gain

    return _pcall(
        body, name="norm1_bwd", grid=(T // ROWS_EW,),
        in_specs=[_row_spec(), _row_spec(), _row_spec(), _vec_spec(1), _vec_spec()],
        out_specs=[_row_spec(), _vec_spec()],
        out_shape=[jax.ShapeDtypeStruct((T, D_MODEL), F32), jax.ShapeDtypeStruct((8, D_MODEL), F32)],
        compiler_params=_params(("arbitrary",)),
    )(dh, x, dx1, gain, mod8)


MERGE_BC = 512


def _merge_specs():
    ga = pl.BlockSpec((ROWS_EW, MERGE_BC), lambda i, j: (i, OFF_GATE_A // MERGE_BC + j))
    gb = pl.BlockSpec((ROWS_EW, MERGE_BC), lambda i, j: (i, OFF_GATE_B // MERGE_BC + j))
    t = pl.BlockSpec((ROWS_EW, MERGE_BC), lambda i, j: (i, j))
    return ga, gb, t


def _merge_fwd(proj, ya, yb):
    T = proj.shape[0]
    ga, gb, t = _merge_specs()

    def body(ga_ref, gb_ref, ya_ref, yb_ref, o_ref):
        o_ref[...] = (_sig(ga_ref[...]) * ya_ref[...] + _sig(gb_ref[...]) * yb_ref[...]).astype(BF16)

    return _pcall(
        body, name="merge_fwd", grid=(T // ROWS_EW, D_MODEL // MERGE_BC),
        in_specs=[ga, gb, t, t], out_specs=t, out_shape=jax.ShapeDtypeStruct((T, D_MODEL), BF16),
        compiler_params=_params(("parallel", "parallel")),
    )(proj, proj, ya, yb)


def _merge_bwd(proj, ya, yb, dmerged):
    T = proj.shape[0]
    ga, gb, t = _merge_specs()

    def body(ga_ref, gb_ref, ya_ref, yb_ref, dm_ref, dya_ref, dyb_ref, dga_ref, dgb_ref):
        dm = dm_ref[...]
        sa, sb = _sig(ga_ref[...]), _sig(gb_ref[...])
        dya_ref[...] = (dm * sa).astype(BF16)
        dyb_ref[...] = (dm * sb).astype(BF16)
        dga_ref[...] = (dm * ya_ref[...] * sa * (1.0 - sa)).astype(BF16)
        dgb_ref[...] = (dm * yb_ref[...] * sb * (1.0 - sb)).astype(BF16)

    sh = jax.ShapeDtypeStruct((T, D_MODEL), BF16)
    return _pcall(
        body, name="merge_bwd", grid=(T // ROWS_EW, D_MODEL // MERGE_BC),
        in_specs=[ga, gb, t, t, t], out_specs=[t, t, t, t], out_shape=[sh, sh, sh, sh],
        compiler_params=_params(("parallel", "parallel")),
    )(proj, proj, ya, yb, dmerged)


def _hgrn_rows(T):
    return 512 if T >= 1024 else 128


def _lower_bound(lbl):
    e = jnp.exp(lbl - jnp.max(lbl, axis=0, keepdims=True))
    return e[0:1, :] / (e[0:1, :] + e[1:2, :])


def _hgrn_chunk_fwd(q, fl, v, lb, st):
    C = A_CHUNK
    sg = _sig(fl)
    f = lb + (1.0 - lb) * sg
    lf = jnp.log(f)
    k = 1.0 - f
    sq = _sig(q)
    qf = q * sq
    row = lax.broadcasted_iota(jnp.int32, (C, C), 0)
    col = lax.broadcasted_iota(jnp.int32, (C, C), 1)
    causal = row >= col
    b = jnp.dot(causal.astype(F32), lf, precision=HIGHEST, preferred_element_type=F32)
    bm = b[C // 2 - 1:C // 2, :]
    bl = b[C - 1:C, :]
    e_q, e_k = jnp.exp(b - bm), jnp.exp(bm - b)
    e_b, e_l = jnp.exp(b), jnp.exp(bl - b)
    qd, kd = qf * e_q, k * e_k
    qe, ke = qf * e_b, k * e_l
    att = jnp.where(causal, _nt(qd, kd), 0.0)
    o = _nn(att, v) + _nt(qe, st)
    dec = jnp.exp(bl)
    st_next = st * dec + _tn(v, ke)
    return dict(sg=sg, f=f, k=k, sq=sq, qf=qf, causal=causal, e_q=e_q, e_k=e_k, e_b=e_b, e_l=e_l, qd=qd, kd=kd,
                qe=qe, ke=ke, att=att, o=o, dec=dec, st_next=st_next)


def _hgrn_fwd(proj, lb_logits, o_gain):
    T = proj.shape[0]
    BR = _hgrn_rows(T)
    cps = BR // A_CHUNK
    K = A_HEAD_DIM

    def col(off):
        return pl.BlockSpec((BR, K), lambda h, cb: (cb, off // K + h))

    def body(q_ref, f_ref, i_ref, g_ref, lbl_ref, og_ref, o_ref, s_ref, st):
        @pl.when(pl.program_id(1) == 0)
        def _():
            st[...] = jnp.zeros_like(st)

        lb = _lower_bound(lbl_ref[...])
        gain = og_ref[...]
        for ci in range(cps):
            r = slice(ci * A_CHUNK, (ci + 1) * A_CHUNK)
            s_prev = st[...]
            s_ref[0, ci] = s_prev
            c = _hgrn_chunk_fwd(q_ref[r, :], f_ref[r, :], i_ref[r, :], lb, s_prev)
            st[...] = c["st_next"]
            o = c["o"]
            on = o * lax.rsqrt(jnp.mean(o * o, axis=-1, keepdims=True) + EPS)
            g = g_ref[r, :]
            o_ref[r, :] = (on * gain * (g * _sig(g))).astype(BF16)

    return _pcall(
        body, name="hgrn_fwd", grid=(A_HEADS, T // BR),
        in_specs=[col(OFF_QA), col(OFF_FA), col(OFF_IA), col(OFF_GA),
                  pl.BlockSpec((2, K), lambda h, cb: (0, h)), pl.BlockSpec((1, K), lambda h, cb: (0, h))],
        out_specs=[pl.BlockSpec((BR, K), lambda h, cb: (cb, h)),
                   pl.BlockSpec((1, cps, K, K), lambda h, cb: (h, cb, 0, 0))],
        out_shape=[jax.ShapeDtypeStruct((T, A_WIDTH), BF16),
                   jax.ShapeDtypeStruct((A_HEADS, T // A_CHUNK, K, K), F32)],
        scratch_shapes=[pltpu.VMEM((K, K), F32)],
        compiler_params=_params(("parallel", "arbitrary")),
    )(proj, proj, proj, proj, lb_logits, o_gain)


def _hgrn_bwd(proj, lb_logits, o_gain, states, do):
    T = proj.shape[0]
    BR = _hgrn_rows(T)
    cps = BR // A_CHUNK
    ncb = T // BR
    K, C = A_HEAD_DIM, A_CHUNK

    def col(off):
        return pl.BlockSpec((BR, K), lambda h, cb: (ncb - 1 - cb, off // K + h))

    def body(q_ref, f_ref, i_ref, g_ref, lbl_ref, og_ref, s_ref, do_ref,
             dq_ref, df_ref, di_ref, dg_ref, dlb_ref, dog_ref, dst):
        @pl.when(pl.program_id(1) == 0)
        def _():
            dst[...] = jnp.zeros_like(dst)
            dlb_ref[...] = jnp.zeros_like(dlb_ref)
            dog_ref[...] = jnp.zeros_like(dog_ref)

        lb = _lower_bound(lbl_ref[...])
        gain = og_ref[...]
        row = lax.broadcasted_iota(jnp.int32, (C, K), 0)
        for ci in reversed(range(cps)):
            r = slice(ci * C, (ci + 1) * C)
            st = s_ref[0, ci]
            v = i_ref[r, :]
            q = q_ref[r, :]
            c = _hgrn_chunk_fwd(q, f_ref[r, :], v, lb, st)
            dst_next = dst[...]
            o = c["o"]
            rn = lax.rsqrt(jnp.mean(o * o, axis=-1, keepdims=True) + EPS)
            on = o * rn
            g = g_ref[r, :]
            sgg = _sig(g)
            dy = do_ref[r, :]
            d_ong = dy * (g * sgg)
            dg_ref[r, :] = (dy * (on * gain) * (sgg * (1.0 + g * (1.0 - sgg)))).astype(BF16)
            dog_ref[0:1, :] += jnp.sum(d_ong * on, axis=0, keepdims=True)
            d_on = d_ong * gain
            d_o = rn * (d_on - on * jnp.mean(d_on * on, axis=-1, keepdims=True))
            datt = jnp.where(c["causal"], _nt(d_o, v), 0.0)
            dv = _tn(c["att"], d_o) + _nt(c["ke"], dst_next)
            dqd = _nn(datt, c["kd"])
            dkd = _tn(datt, c["qd"])
            dqe = _nn(d_o, st)
            dke = _nn(v, dst_next)
            dst[...] = dst_next * c["dec"] + _tn(d_o, c["qe"])
            d_dec = jnp.sum(dst_next * st, axis=0, keepdims=True)
            t_q, t_k = dqd * c["qd"], dkd * c["kd"]
            t_e, t_l = dqe * c["qe"], dke * c["ke"]
            db = t_q - t_k + t_e - t_l
            dbm = jnp.sum(t_k - t_q, axis=0, keepdims=True)
            dbl = jnp.sum(t_l, axis=0, keepdims=True) + d_dec * c["dec"]
            db = db + jnp.where(row == C // 2 - 1, dbm, 0.0) + jnp.where(row == C - 1, dbl, 0.0)
            upper = jnp.logical_not(c["causal"]) | (lax.broadcasted_iota(jnp.int32, (C, C), 0)
                                                    == lax.broadcasted_iota(jnp.int32, (C, C), 1))
            dlf = jnp.dot(upper.astype(F32), db, precision=HIGHEST, preferred_element_type=F32)
            dqf = dqd * c["e_q"] + dqe * c["e_b"]
            sq = c["sq"]
            dq_ref[r, :] = (dqf * (sq * (1.0 + q * (1.0 - sq)))).astype(BF16)
            dk = dkd * c["e_k"] + dke * c["e_l"]
            df = dlf / c["f"] - dk
            sg = c["sg"]
            df_ref[r, :] = (df * (1.0 - lb) * sg * (1.0 - sg)).astype(BF16)
            dlb_ref[0:1, :] += jnp.sum(df * (1.0 - sg), axis=0, keepdims=True)
            di_ref[r, :] = dv.astype(BF16)

    ocol = pl.BlockSpec((BR, K), lambda h, cb: (ncb - 1 - cb, h))
    vec =pl.BlockSpec((8, K), lambda h, cb: (0, h))
    return _pcall(
        body, name="hgrn_bwd", grid=(A_HEADS, ncb),
        in_specs=[col(OFF_QA), col(OFF_FA), col(OFF_IA), col(OFF_GA),
                  pl.BlockSpec((2, K), lambda h, cb: (0, h)), pl.BlockSpec((1, K), lambda h, cb: (0, h)),
                  pl.BlockSpec((1, cps, K, K), lambda h, cb: (h, ncb - 1 - cb, 0, 0)),
                  pl.BlockSpec((BR, K), lambda h, cb: (ncb - 1 - cb, h))],
        out_specs=[ocol, ocol, ocol, ocol, vec, vec],
        out_shape=[jax.ShapeDtypeStruct((T, A_WIDTH), BF16)] * 4 + [jax.ShapeDtypeStruct((8, A_WIDTH), F32)] * 2,
        scratch_shapes=[pltpu.VMEM((K, K), F32)],
        compiler_params=_params(("parallel", "arbitrary")),
    )(proj, proj, proj, proj, lb_logits, o_gain, states, do)


def _head_norm(x):
    r = lax.rsqrt(jnp.mean(x * x, axis=-1, keepdims=True) + EPS)
    return x * r, r


def _head_norm_bwd(dy, xn, r, gain):
    dxn = dy * gain
    return r * (dxn - xn * jnp.mean(dxn * xn, axis=-1, keepdims=True)), jnp.sum(dy * xn, axis=0, keepdims=True)


def _swa_mask(has_prev):
    rows = B_GROUP * BLOCK
    r = lax.broadcasted_iota(jnp.int32, (rows, 2 * BLOCK), 0) % BLOCK
    c = lax.broadcasted_iota(jnp.int32, (rows, 2 * BLOCK), 1)
    rel = r + BLOCK - c
    return (rel >= 0) & (rel < BLOCK) & ((c >= BLOCK) | has_prev)


def _swa_head_fwd(j, q_ref, kp_ref, kc_ref, vp_ref, vc_ref, qg, kg, sk_ref, mask):
    hs = slice(j * B_HEAD_DIM, (j + 1) * B_HEAD_DIM)
    kcat = jnp.concatenate([kp_ref[:, hs], kc_ref[:, hs]], axis=0)
    vcat = jnp.concatenate([vp_ref[:, hs], vc_ref[:, hs]], axis=0)
    qs = jnp.concatenate([q_ref[:, pl.ds((j * B_GROUP + g) * B_HEAD_DIM, B_HEAD_DIM)] for g in range(B_GROUP)], axis=0)
    kn, kr = _head_norm(kcat)
    qn, qr = _head_norm(qs)
    kh, qh = kn * kg, qn * qg
    s = jnp.where(mask, _nt(qh, kh) * (B_HEAD_DIM ** -0.5), NEG_BIG)
    sink = jnp.concatenate(
        [jnp.broadcast_to(sk_ref[0:1, pl.ds(j * B_GROUP + g, 1)], (BLOCK, 1)) for g in range(B_GROUP)], axis=0)
    m = jnp.maximum(jnp.max(s, axis=-1, keepdims=True), sink)
    p = jnp.exp(s - m)
    e_sink = jnp.exp(sink - m)
    inv = 1.0 / (jnp.sum(p, axis=-1, keepdims=True) + e_sink)
    prob = p * inv
    return dict(vcat=vcat, kn=kn, kr=kr, qn=qn, qr=qr, kh=kh, qh=qh, prob=prob, p_sink=e_sink * inv)


def _swa_in_specs(nb, last):
    def qi(n):
        return jnp.minimum(n, last)

    q = pl.BlockSpec((BLOCK, B_WIDTH), lambda n: (qi(n), OFF_QB // B_WIDTH))
    kc = pl.BlockSpec((BLOCK, B_KV_WIDTH), lambda n: (qi(n), OFF_KB // B_KV_WIDTH))
    kp = pl.BlockSpec((BLOCK, B_KV_WIDTH), lambda n: (jnp.maximum(qi(n) - 1, 0), OFF_KB // B_KV_WIDTH))
    vc = pl.BlockSpec((BLOCK, B_KV_WIDTH), lambda n: (qi(n), OFF_VB // B_KV_WIDTH))
    vp = pl.BlockSpec((BLOCK, B_KV_WIDTH), lambda n: (jnp.maximum(qi(n) - 1, 0), OFF_VB // B_KV_WIDTH))
    small = [pl.BlockSpec((1, B_HEAD_DIM), lambda n: (0, 0)), pl.BlockSpec((1, B_HEAD_DIM), lambda n: (0, 0)),
             pl.BlockSpec((1, B_GROUP * B_KV_HEADS), lambda n: (0, 0))]
    return [q, kp, kc, vp, vc] + small


def _swa_fwd(proj, q_gain, k_gain, sinks):
    T = proj.shape[0]
    nb = T // BLOCK

    def body(q_ref, kp_ref, kc_ref, vp_ref, vc_ref, qg_ref, kg_ref, sk_ref, o_ref):
        mask = _swa_mask(pl.program_id(0) > 0)
        for j in range(B_KV_HEADS):
            c = _swa_head_fwd(j, q_ref, kp_ref, kc_ref, vp_ref, vc_ref, qg_ref[...], kg_ref[...], sk_ref, mask)
            o = _nn(c["prob"], c["vcat"])
            for g in range(B_GROUP):
                o_ref[:, pl.ds((j * B_GROUP + g) * B_HEAD_DIM, B_HEAD_DIM)] = o[g * BLOCK:(g + 1) * BLOCK].astype(BF16)

    return _pcall(
        body, name="swa_fwd", grid=(nb,),
        in_specs=_swa_in_specs(nb, nb - 1),
        out_specs=pl.BlockSpec((BLOCK, B_WIDTH), lambda n: (n, 0)),
        out_shape=jax.ShapeDtypeStruct((T, B_WIDTH), BF16),
        compiler_params=_params(("parallel",)),
    )(proj, proj, proj, proj, proj, q_gain, k_gain, sinks)


def _swa_bwd(proj, q_gain, k_gain, sinks, do):
    T = proj.shape[0]
    nb = T // BLOCK
    scale = B_HEAD_DIM ** -0.5

    def body(q_ref, kp_ref, kc_ref, vp_ref, vc_ref, qg_ref, kg_ref, sk_ref, do_ref,
             dq_ref, dkv_ref, sm_ref, ck, cv):
        n = pl.program_id(0)

        @pl.when(n == 0)
        def _():
            ck[...] = jnp.zeros_like(ck)
            cv[...] = jnp.zeros_like(cv)
            sm_ref[...] = jnp.zeros_like(sm_ref)

        @pl.when(n < nb)
        def _():
            mask = _swa_mask(n > 0)
            qg, kg = qg_ref[...], kg_ref[...]
            lane = lax.broadcasted_iota(jnp.int32, (1, BLOCK), 1)
            for j in range(B_KV_HEADS):
                hs = slice(j * B_HEAD_DIM, (j + 1) * B_HEAD_DIM)
                vs = slice(B_KV_WIDTH + j * B_HEAD_DIM, B_KV_WIDTH + (j + 1) * B_HEAD_DIM)
                c = _swa_head_fwd(j, q_ref, kp_ref, kc_ref, vp_ref, vc_ref, qg, kg, sk_ref, mask)
                d_out = jnp.concatenate(
                    [do_ref[:, pl.ds((j * B_GROUP + g) * B_HEAD_DIM, B_HEAD_DIM)] for g in range(B_GROUP)], axis=0)
                prob = c["prob"]
                out = _nn(prob, c["vcat"])
                delta = jnp.sum(d_out * out, axis=-1, keepdims=True)
                ds = prob * (_nt(d_out, c["vcat"]) - delta)
                d_sink = -c["p_sink"] * delta
                dqh = _nn(ds, c["kh"]) * scale
                dkh = _tn(ds, c["qh"]) * scale
                dv = _tn(prob, d_out)
                dq, dqg = _head_norm_bwd(dqh, c["qn"], c["qr"], qg)
                dk, dkg = _head_norm_bwd(dkh, c["kn"], c["kr"], kg)
                sm_ref[0:1, 0:B_HEAD_DIM] += dqg
                sm_ref[1:2, 0:B_HEAD_DIM] += dkg
                for g in range(B_GROUP):
                    dq_ref[:, pl.ds((j * B_GROUP + g) * B_HEAD_DIM, B_HEAD_DIM)] = dq[g * BLOCK:(g + 1) * BLOCK].astype(BF16)
                    tot = jnp.sum(d_sink[g * BLOCK:(g + 1) * BLOCK], axis=0, keepdims=True)
                    sm_ref[2:3, :] += jnp.where(lane == j * B_GROUP + g, tot, 0.0)
                dkv_ref[:, hs] = (ck[:, hs] + dk[0:BLOCK]).astype(BF16)
                dkv_ref[:, vs] = (cv[:, hs] + dv[0:BLOCK]).astype(BF16)
                ck[:, hs] = dk[BLOCK:2 * BLOCK]
                cv[:, hs] = dv[BLOCK:2 * BLOCK]

        @pl.when(n == nb)
        def _():
            dkv_ref[:, 0:B_KV_WIDTH] = ck[...].astype(BF16)
            dkv_ref[:, B_KV_WIDTH:2 * B_KV_WIDTH] = cv[...].astype(BF16)

    return _pcall(
        body, name="swa_bwd", grid=(nb + 1,),
        in_specs=_swa_in_specs(nb, nb - 1) + [pl.BlockSpec((BLOCK, B_WIDTH), lambda n: (jnp.minimum(n, nb - 1), 0))],
        out_specs=[pl.BlockSpec((BLOCK, B_WIDTH), lambda n: (jnp.minimum(n, nb - 1), 0)),
                   pl.BlockSpec((BLOCK, 2 * B_KV_WIDTH), lambda n: (jnp.maximum(n - 1, 0), 0)),
                   pl.BlockSpec((8, BLOCK), lambda n: (0, 0))],
        out_shape=[jax.ShapeDtypeStruct((T, B_WIDTH), BF16), jax.ShapeDtypeStruct((T, 2 * B_KV_WIDTH), BF16),
                   jax.ShapeDtypeStruct((8, BLOCK), F32)],
        scratch_shapes=[pltpu.VMEM((BLOCK, B_KV_WIDTH), F32), pltpu.VMEM((BLOCK, B_KV_WIDTH), F32)],
        compiler_params=_params(("arbitrary",)),
    )(proj, proj, proj, proj, proj, q_gain, k_gain, sinks, do)


def _local_step(x, target, mod8, norm1_gain, norm2_gain, lb_logits, o_gain, q_gain, k_gain, sinks,
                w_in, w_a, w_b, w_out, w_mi, w_mo):
    relu2 = lambda u: (u, jnp.square(jnp.maximum(u, 0.0)))
    h = _norm1_fwd(x, norm1_gain, mod8)
    proj = _mm(h, w_in, name="mm_proj", bn=512)
    o_a, states = _hgrn_fwd(proj, lb_logits, o_gain)
    o_b = _swa_fwd(proj, q_gain, k_gain, sinks)
    ya = _mm(o_a, w_a, name="mm_branch_a")
    yb = _mm(o_b, w_b, name="mm_branch_b")
    merged = _merge_fwd(proj, ya, yb)
    mo = _mm(merged, w_out, name="mm_out")
    x1, h2 = _res_norm2_fwd(x, mo, norm2_gain, mod8)
    u, act = _mm(h2, w_mi, name="mm_mlp_in", out_dtypes=(F32, BF16), epi=relu2)
    mlp = _mm(act, w_mo, name="mm_mlp_out")
    dy, dmlp, st_loss = _loss_bwd(x1, mlp, target, mod8)
    du = _mm(dmlp, w_mo, name="mm_d_act", tb=True, out_dtypes=(BF16,), extras=(u,),
             epi=lambda acc, uu: (acc * (2.0 * jnp.maximum(uu, 0.0)),))
    g_mo = _mm(act, dmlp, name="mm_g_mlp_out", ta=True)
    dh2 = _mm(du, w_mi, name="mm_d_h2", tb=True)
    g_mi = _mm(h2, du, name="mm_g_mlp_in", ta=True)
    dx1, dmo, st_n2 = _norm2_bwd(dh2, x1, dy, mo, norm2_gain, mod8)
    dmerged = _mm(dmo, w_out, name="mm_d_merged", tb=True)
    g_out = _mm(merged, dmo, name="mm_g_out", ta=True)
    dya, dyb, dga, dgb = _merge_bwd(proj, ya, yb, dmerged)
    do_a = _mm(dya, w_a, name="mm_d_oa", tb=True)
    g_a = _mm(o_a, dya, name="mm_g_branch_a", ta=True)
    do_b = _mm(dyb, w_b, name="mm_d_ob", tb=True)
    g_b = _mm(o_b, dyb, name="mm_g_branch_b", ta=True)
    dqa, dfa, dia, dgga, d_lb, d_og = _hgrn_bwd(proj, lb_logits, o_gain, states, do_a)
    dqb, dkvb, st_swa = _swa_bwd(proj, q_gain, k_gain, sinks, do_b)
    dproj = jnp.concatenate([dqa, dfa, dia, dgga, dqb, dkvb, dga, dgb], axis=1)
    dh = _mm(dproj, w_in, name="mm_d_h", tb=True, bk=2432)
    g_in = _mm(h, dproj, name="mm_g_in", ta=True, bn=512)
    grad_x, st_n1 = _norm1_bwd(dh, x, dx1, norm1_gain, mod8)
    stats = dict(loss=st_loss, n2=st_n2, n1=st_n1, d_lb=d_lb, d_og=d_og, swa=st_swa)
    return grad_x, (g_in, g_a, g_b, g_out, g_mi, g_mo), stats


def _ew_rows(rows, cols):
    br = 8
    while br * 2 <= rows and br * 2 * cols * 4 <= (1 << 20) and rows % (br * 2) == 0:
        br *= 2
    return br


def _cast_bf16(w, name):
    R, C = w.shape
    br = _ew_rows(R, C)
    spec = pl.BlockSpec((br, C), lambda i: (i, 0))

    def body(w_ref, o_ref):
        o_ref[...] = w_ref[...].astype(BF16)

    return _pcall(body, name=name, grid=(R // br,), in_specs=[spec], out_specs=spec,
                  out_shape=jax.ShapeDtypeStruct((R, C), BF16), compiler_params=_params(("parallel",)))(w)


def _adamw_math(w, g, m, v):
    m = ADAM_B1 * m + (1.0 - ADAM_B1) * g
    v = ADAM_B2 * v + (1.0 - ADAM_B2) * (g * g)
    m_hat = m / (1.0 - ADAM_B1 ** ADAM_STEP)
    v_hat = v / (1.0 - ADAM_B2 ** ADAM_STEP)
    delta = -ADAM_LR * (m_hat / (jnp.sqrt(v_hat) + ADAM_EPS) + ADAM_WD * w)
    return delta, m, v


def _adamw(w, g, m, v, name):
    R, C = w.shape
    br = _ew_rows(R, C)
    spec = pl.BlockSpec((br, C), lambda i: (i, 0))

    def body(w_ref, g_ref, m_ref, v_ref, d_ref, nm_ref, nv_ref):
        d_ref[...], nm_ref[...], nv_ref[...] = _adamw_math(w_ref[...], g_ref[...], m_ref[...], v_ref[...])

    sh = jax.ShapeDtypeStruct((R, C), F32)
    return _pcall(body, name=name, grid=(R // br,), in_specs=[spec] * 4, out_specs=[spec] * 3, out_shape=[sh] * 3,
                  compiler_params=_params(("parallel",)))(w, g, m, v)


def _ada_grad_adamw(c_t, dmod, w, m, v):
    R, C = w.shape
    br = _ew_rows(R, C)
    spec = pl.BlockSpec((br, C), lambda i: (i, 0))

    def body(c_ref, dm_ref, w_ref, m_ref, v_ref, g_ref, d_ref, nm_ref, nv_ref):
        cv = c_ref[...]
        sc = cv * _sig(cv)
        g = sc[:, 0:1] * dm_ref[0:1, :]
        for b in range(1, N_DEV):
            g = g + sc[:, b:b + 1] * dm_ref[b:b + 1, :]
        g_ref[...] = g
        d_ref[...], nm_ref[...], nv_ref[...] = _adamw_math(w_ref[...], g, m_ref[...], v_ref[...])

    sh = jax.ShapeDtypeStruct((R, C), F32)
    return _pcall(
        body, name="ada_grad_adamw", grid=(R // br,),
        in_specs=[pl.BlockSpec((br, N_DEV), lambda i: (i, 0)), pl.BlockSpec((N_DEV, C), lambda i: (0, 0)), spec, spec, spec],
        out_specs=[spec] * 4, out_shape=[sh] * 4, compiler_params=_params(("parallel",)))(c_t, dmod, w, m, v)


SMALL_ROWS = 16


def _small_sum(small_all, lb_logits):
    def body(s_ref, lbl_ref, o_ref):
        acc = s_ref[0:SMALL_ROWS, :]
        for d in range(1, N_DEV):
            acc = acc + s_ref[d * SMALL_ROWS:(d + 1) * SMALL_ROWS, :]
        o_ref[...] = acc
        z = lbl_ref[...]
        e = jnp.exp(z - jnp.max(z, axis=0, keepdims=True))
        p0 = e[0:1, :] / (e[0:1, :] + e[1:2, :])
        dz = acc[8:9, 0:A_WIDTH] * p0 * (1.0 - p0)
        o_ref[8:9, 0:A_WIDTH] = dz
        o_ref[10:11, 0:A_WIDTH] = -dz

    return _pcall(body, name="small_sum", out_shape=jax.ShapeDtypeStruct((SMALL_ROWS, D_MODEL), F32),
                  in_specs=[pl.BlockSpec(memory_space=pltpu.VMEM)] * 2, out_specs=pl.BlockSpec(memory_space=pltpu.VMEM),
                  compiler_params=_params())(small_all, lb_logits)


RELATIONS = ((1, 0), (0, 1), (1, 1))
ANY = pl.BlockSpec(memory_space=pl.ANY)


def _place():
    x, y, c = lax.axis_index("x"), lax.axis_index("y"), lax.axis_index("c")
    return x, y, c


def _allgather_small(x_shard, name):
    m_per, n = x_shard.shape

    def body(x_ref, out_ref, send_sems, recv_sems, local_sem):
        x, y, c = _place()
        me, sibling = (x, y, c), (x, y, 1 - c)
        chips = [(1 - x, y), (x, 1 - y), (1 - x, 1 - y)]

        def rows(px, py, pc):
            return out_ref.at[pl.ds((4 * px + 2 * py + pc) * m_per, m_per), :]

        def copy(k, block, to, src=None):
            return pltpu.make_async_remote_copy(
                src_ref=rows(*block) if src is None else src, dst_ref=rows(*block),
                send_sem=send_sems.at[k], recv_sem=recv_sems.at[k], device_id=to, device_id_type=MESH)

        mine = pltpu.make_async_copy(x_ref, rows(*me), local_sem)
        mine.start()
        first = [copy(0, me, sibling, src=x_ref)]
        first += [copy(1 + j, me, (*chip, c), src=x_ref) for j, chip in enumerate(chips)]
        for cp in first:
            cp.start()
        passed = [copy(4 + j, (*chip, c), sibling) for j, chip in enumerate(chips)]
        for j, chip in enumerate(chips):
            copy(1 + j, (*chip, c), me).wait_recv()
            passed[j].start()
        copy(0, sibling, me).wait_recv()
        for j, chip in enumerate(chips):
            copy(4 + j, (*chip, 1 - c), me).wait_recv()
        for cp in first + passed:
            cp.wait_send()
        mine.wait()

    return _pcall(
        body, name=name, out_shape=jax.ShapeDtypeStruct((N_DEV * m_per, n), x_shard.dtype),
        in_specs=[pl.BlockSpec(memory_space=pltpu.VMEM)], out_specs=pl.BlockSpec(memory_space=pltpu.VMEM),
        scratch_shapes=[pltpu.SemaphoreType.DMA((7,)), pltpu.SemaphoreType.DMA((7,)), pltpu.SemaphoreType.DMA],
        compiler_params=_params(),
    )(x_shard)


W_SHAPES = ((D_MODEL, IN_WIDTH, True), (A_WIDTH, D_MODEL, True), (B_WIDTH, D_MODEL, True),
            (D_MODEL, D_MODEL, False), (D_MODEL, MLP_HIDDEN, True), (MLP_HIDDEN, D_MODEL, False))
N_W = len(W_SHAPES)


def _shard_shape(w):
    R, C, by_col = W_SHAPES[w]
    return (R, C // N_CHIPS) if by_col else (R // N_CHIPS, C)


def _half_shape(w):
    sr, sc = _shard_shape(w)
    return sr // 2, sc


def _region(full_ref, w, chip, half):
    sr, sc = _shard_shape(w)
    by_col = W_SHAPES[w][2]
    r0, c0 = (0, chip * sc) if by_col else (chip * sr, 0)
    if half is None:
        return full_ref.at[pl.ds(r0, sr), pl.ds(c0, sc)]
    return full_ref.at[pl.ds(r0 + half * (sr // 2), sr // 2), pl.ds(c0, sc)]


def _gather_weights(shards):
    def body(*refs):
        s_refs, f_refs = refs[:N_W], refs[N_W:2 * N_W]
        send, recv, p_send, p_recv, l_sem = refs[2 * N_W:]
        x, y, c = _place()
        me = 4 * x + 2 * y + c
        for d in range(N_DEV):
            @pl.when(me == d)
            def _(d=d):
                chip, dc = d >> 1, d & 1
                local = [pltpu.make_async_copy(s_refs[w], _region(f_refs[w], w, chip, None), l_sem.at[w])
                         for w in range(N_W)]
                for cp in local:
                    cp.start()
                sends = []
                for w in range(N_W):
                    hr = _half_shape(w)[0]
                    for k, (rx, ry) in enumerate(RELATIONS):
                        sends.append(pltpu.make_async_remote_copy(
                            src_ref=s_refs[w].at[pl.ds(dc * hr, hr), :], dst_ref=_region(f_refs[w], w, chip, dc),
                            send_sem=send.at[w * 3 + k], recv_sem=recv.at[w * 3 + k],
                            device_id=(x ^ rx, y ^ ry, c), device_id_type=MESH))
                        sends[-1].start()
                for w in range(N_W):
                    for k, (rx, ry) in enumerate(RELATIONS):
                        got = _region(f_refs[w], w, chip ^ (2 * rx + ry), dc)
                        pltpu.make_async_remote_copy(
                            src_ref=got, dst_ref=got, send_sem=send.at[w * 3 + k], recv_sem=recv.at[w * 3 + k],
                            device_id=(x, y, c), device_id_type=MESH).wait_recv()
                        sends.append(pltpu.make_async_remote_copy(
                            src_ref=got, dst_ref=got, send_sem=p_send.at[w * 3 + k], recv_sem=p_recv.at[w * 3 + k],
                            device_id=(x, y, 1 - c), device_id_type=MESH))
                        sends[-1].start()
                for w in range(N_W):
                    for k, (rx, ry) in enumerate(RELATIONS):
                        got = _region(f_refs[w], w, chip ^ (2 * rx + ry), 1 - dc)
                        pltpu.make_async_remote_copy(
                            src_ref=got, dst_ref=got, send_sem=p_send.at[w * 3 + k], recv_sem=p_recv.at[w * 3 + k],
                            device_id=(x, y, c), device_id_type=MESH).wait_recv()
                for cp in sends:
                    cp.wait_send()
                for cp in local:
                    cp.wait()

    n_sem = 3 * N_W
    return _pcall(
        body, name="gather_weights",
        out_shape=[jax.ShapeDtypeStruct(W_SHAPES[w][:2], BF16) for w in range(N_W)],
        in_specs=[ANY] * N_W, out_specs=[ANY] * N_W,
        scratch_shapes=[pltpu.SemaphoreType.DMA((n_sem,)) for _ in range(4)] + [pltpu.SemaphoreType.DMA((N_W,))],
        compiler_params=_params(),
    )(*shards)


def _grad_view(g, w):
    R, C, by_col = W_SHAPES[w]
    return g.reshape(1, 2, R // 2, C) if by_col else g.reshape(N_CHIPS, 2, R // N_CHIPS // 2, C)


def _sibling_exchange(g4s):
    pieces = [(w, p) for w in range(N_W) for p in range(g4s[w].shape[0])]

    def body(*refs):
        g_refs, r_refs = refs[:N_W], refs[N_W:2 * N_W]
        send, recv = refs[2 * N_W:]
        x, y, c = _place()
        for dc in range(2):
            @pl.when(c == dc)
            def _(dc=dc):
                cps = [pltpu.make_async_remote_copy(
                    src_ref=g_refs[w].at[p, 1 - dc], dst_ref=r_refs[w].at[p], send_sem=send.at[i], recv_sem=recv.at[i],
                    device_id=(x, y, 1 - c), device_id_type=MESH) for i, (w, p) in enumerate(pieces)]
                for cp in cps:
                    cp.start()
                for cp in cps:
                    cp.wait()

    return _pcall(
        body, name="sibling_exchange",
        out_shape=[jax.ShapeDtypeStruct((g.shape[0],) + g.shape[2:], F32) for g in g4s],
        in_specs=[ANY] * N_W, out_specs=[ANY] * N_W,
        scratch_shapes=[pltpu.SemaphoreType.DMA((len(pieces),)), pltpu.SemaphoreType.DMA((len(pieces),))],
        compiler_params=_params(),
    )(*g4s)


def _pair_sum(g4, other, c_arr, name):
    P, _, hr, C = g4.shape
    br = _ew_rows(hr, C)

    def body(c_ref, g_ref, o_ref, p_ref):
        p_ref[...] = (g_ref[...] + o_ref[...]).astype(BF16)

    return _pcall(
        body, name=name,
        grid_spec=pltpu.PrefetchScalarGridSpec(
            num_scalar_prefetch=1, grid=(P, hr // br),
            in_specs=[pl.BlockSpec((None, None, br, C), lambda p, i, c_ref: (p, c_ref[0], i, 0)),
                      pl.BlockSpec((None, br, C), lambda p, i, c_ref: (p, i, 0))],
            out_specs=pl.BlockSpec((None, br, C), lambda p, i, c_ref: (p, i, 0))),
        out_shape=jax.ShapeDtypeStruct((P, hr, C), BF16),
        compiler_params=_params(("parallel", "parallel")),
    )(c_arr, g4, other)


def _chip_exchange(ps):
    def src_of(p_ref, w, chip):
        sr, sc = _shard_shape(w)
        return p_ref.at[0, :, pl.ds(chip * sc, sc)] if W_SHAPES[w][2] else p_ref.at[chip]

    def body(*refs):
        p_refs, s_refs = refs[:N_W], refs[N_W:2 * N_W]
        send, recv, l_sem = refs[2 * N_W:]
        x, y, c = _place()
        chip_t = 2 * x + y
        for chip in range(N_CHIPS):
            @pl.when(chip_t == chip)
            def _(chip=chip):
                local = [pltpu.make_async_copy(src_of(p_refs[w], w, chip), s_refs[w].at[3], l_sem.at[w])
                         for w in range(N_W)]
                for cp in local:
                    cp.start()
                cps = []
                for w in range(N_W):
                    for k, (rx, ry) in enumerate(RELATIONS):
                        cps.append(pltpu.make_async_remote_copy(
                            src_ref=src_of(p_refs[w], w, chip ^ (2 * rx + ry)), dst_ref=s_refs[w].at[k],
                            send_sem=send.at[w * 3 + k], recv_sem=recv.at[w * 3 + k],
                            device_id=(x ^ rx, y ^ ry, c), device_id_type=MESH))
                        cps[-1].start()
                for cp in cps:
                    cp.wait()
                for cp in local:
                    cp.wait()

    return _pcall(
        body, name="chip_exchange",
        out_shape=[jax.ShapeDtypeStruct((4,) + _half_shape(w), BF16) for w in range(N_W)],
        in_specs=[ANY] * N_W, out_specs=[ANY] * N_W,
        scratch_shapes=[pltpu.SemaphoreType.DMA((3 * N_W,)), pltpu.SemaphoreType.DMA((3 * N_W,)),
                        pltpu.SemaphoreType.DMA((N_W,))],
        compiler_params=_params(),
    )(*ps)


def _sum_slots(slots, name):
    _, hr, C = slots.shape
    br = _ew_rows(hr, C)

    def body(s_ref, o_ref):
        acc = s_ref[0].astype(F32)
        for k in range(1, 4):
            acc = acc + s_ref[k].astype(F32)
        o_ref[...] = acc

    return _pcall(
        body, name=name, grid=(hr // br,),
        in_specs=[pl.BlockSpec((4, br, C), lambda i: (0, i, 0))], out_specs=pl.BlockSpec((br, C), lambda i: (i, 0)),
        out_shape=jax.ShapeDtypeStruct((hr, C), F32), compiler_params=_params(("parallel",)),
    )(slots)


def _sibling_share(halves):
    def body(*refs):
        h_refs, o_refs = refs[:N_W], refs[N_W:2 * N_W]
        send, recv, l_sem = refs[2 * N_W:]
        x, y, c = _place()
        for dc in range(2):
            @pl.when(c == dc)
            def _(dc=dc):
                local = [pltpu.make_async_copy(h_refs[w], o_refs[w].at[dc], l_sem.at[w]) for w in range(N_W)]
                for cp in local:
                    cp.start()
                cps = [pltpu.make_async_remote_copy(
                    src_ref=h_refs[w], dst_ref=o_refs[w].at[dc], send_sem=send.at[w], recv_sem=recv.at[w],
                    device_id=(x, y, 1 - c), device_id_type=MESH) for w in range(N_W)]
                for cp in cps:
                    cp.start()
                for w in range(N_W):
                    pltpu.make_async_remote_copy(
                        src_ref=h_refs[w], dst_ref=o_refs[w].at[1 - dc], send_sem=send.at[w], recv_sem=recv.at[w],
                        device_id=(x, y, c), device_id_type=MESH).wait_recv()
                for cp in cps:
                    cp.wait_send()
                for cp in local:
                    cp.wait()

    return _pcall(
        body, name="sibling_share",
        out_shape=[jax.ShapeDtypeStruct((2,) + _half_shape(w), F32) for w in range(N_W)],
        in_specs=[ANY] * N_W, out_specs=[ANY] * N_W,
        scratch_shapes=[pltpu.SemaphoreType.DMA((N_W,)), pltpu.SemaphoreType.DMA((N_W,)), pltpu.SemaphoreType.DMA((N_W,))],
        compiler_params=_params(),
    )(*halves)


def _pad_lanes(v, width=D_MODEL):
    return jnp.pad(v, ((0, 0), (0, width - v.shape[1])))


def _pack_small(b_ada, norm1, norm2, lb, o_gain, q_gain, k_gain, sinks):
    rows = [b_ada.reshape(N_MOD, D_MODEL), norm1, norm2, jnp.concatenate([lb[0:1], o_gain], axis=1),
            _pad_lanes(jnp.concatenate([q_gain, k_gain, sinks], axis=1)), _pad_lanes(lb[1:2]),
            jnp.zeros((SMALL_ROWS - 11, D_MODEL), F32)]
    return jnp.concatenate(rows, axis=0)


def _unpack_small(p):
    return (p[0:6].reshape(1, N_MOD * D_MODEL), p[6:7], p[7:8],
            jnp.concatenate([p[8:9, 0:A_WIDTH], p[10:11, 0:A_WIDTH]], axis=0), p[8:9, A_WIDTH:],
            p[9:10, 0:64], p[9:10, 64:128], p[9:10, 128:144])


def kernel(x, c, w_ada, b_ada, norm1_gain, w_in, lb_logits, hgrn_o_gain, q_norm_gain, k_norm_gain, sinks, w_branch_a, w_branch_b, w_out, norm2_gain, w_mlp_in, w_mlp_out, loss_target, m_w_ada, m_b_ada, m_norm1_gain, m_w_in, m_lb_logits, m_hgrn_o_gain, m_q_norm_gain, m_k_norm_gain, m_sinks, m_w_branch_a, m_w_branch_b, m_w_out, m_norm2_gain, m_w_mlp_in, m_w_mlp_out, v_w_ada, v_b_ada, v_norm1_gain, v_w_in, v_lb_logits, v_hgrn_o_gain, v_q_norm_gain, v_k_norm_gain, v_sinks, v_w_branch_a, v_w_branch_b, v_w_out, v_norm2_gain, v_w_mlp_in, v_w_mlp_out):
    xi, yi, ci = _place()
    chip = 2 * xi + yi
    me = 4 * xi + 2 * yi + ci
    ada_cols = w_ada.shape[2]

    c_all = _allgather_small(jnp.broadcast_to(c, (8, D_MODEL)), "gather_c").reshape(N_DEV, 8, D_MODEL)[:, 0]
    b_cols = lax.dynamic_slice(b_ada, (0, chip * ada_cols), (1, ada_cols))
    mod_part = _ada_fwd(c_all, w_ada[0], b_cols)
    mod_all = _allgather_small(mod_part, "gather_mod").reshape(N_CHIPS, 2, N_DEV, ada_cols)[:, 0]
    mod_mine = lax.dynamic_index_in_dim(mod_all, me, axis=1, keepdims=False).reshape(N_MOD, D_MODEL)
    mod8 = jnp.concatenate([mod_mine, jnp.zeros((2, D_MODEL), F32)], axis=0)

    shards = (w_in[0], w_branch_a[0], w_branch_b[0], w_out[0], w_mlp_in[0], w_mlp_out[0])
    fulls = _gather_weights([_cast_bf16(s, f"cast_w{i}") for i, s in enumerate(shards)])

    grad_x, grads, st = _local_step(x[0], loss_target[0], mod8, norm1_gain, norm2_gain, lb_logits, hgrn_o_gain,
                                    q_norm_gain, k_norm_gain, sinks, *fulls)
    loss = lax.psum(0.5 * jnp.sum(st["loss"][0]) / D_MODEL, ("x", "y", "c"))

    g4s = [_grad_view(g, w) for w, g in enumerate(grads)]
    others = _sibling_exchange(g4s)
    c_arr = ci.astype(jnp.int32).reshape(1)
    pair = [_pair_sum(g4s[w], others[w], c_arr, f"pair_sum{w}") for w in range(N_W)]
    slots = _chip_exchange(pair)
    halves = [_sum_slots(slots[w], f"sum_slots{w}") for w in range(N_W)]
    g_sh = [r.reshape(_shard_shape(w)) for w, r in enumerate(_sibling_share(halves))]
    moments = ((m_w_in, v_w_in), (m_w_branch_a, v_w_branch_a), (m_w_branch_b, v_w_branch_b), (m_w_out, v_w_out),
               (m_w_mlp_in, v_w_mlp_in), (m_w_mlp_out, v_w_mlp_out))
    big = [(g_sh[w],) + tuple(_adamw(shards[w], g_sh[w], moments[w][0][0], moments[w][1][0], f"adamw{w}"))
           for w in range(N_W)]

    swa = st["swa"]
    small = jnp.concatenate([
        st["n1"][1:2], st["n1"][0:1], st["n2"][3:4], st["n2"][1:2], st["n2"][0:1], st["loss"][1:2],
        st["n1"][2:3], st["n2"][2:3], jnp.concatenate([st["d_lb"][0:1], st["d_og"][0:1]], axis=1),
        _pad_lanes(jnp.concatenate([swa[0:1, 0:64], swa[1:2, 0:64], swa[2:3, 0:16]], axis=1)),
        jnp.zeros((SMALL_ROWS - 10, D_MODEL), F32)], axis=0)
    small_all = _allgather_small(small, "gather_small")
    g_small = _small_sum(small_all, lb_logits)
    small_w = (b_ada, norm1_gain, norm2_gain, lb_logits, hgrn_o_gain, q_norm_gain, k_norm_gain, sinks)
    small_m = (m_b_ada, m_norm1_gain, m_norm2_gain, m_lb_logits, m_hgrn_o_gain, m_q_norm_gain, m_k_norm_gain, m_sinks)
    small_v = (v_b_ada, v_norm1_gain, v_norm2_gain, v_lb_logits, v_hgrn_o_gain, v_q_norm_gain, v_k_norm_gain, v_sinks)
    sm = [_unpack_small(t) for t in
          (g_small,) + tuple(_adamw(_pack_small(*small_w), g_small, _pack_small(*small_m), _pack_small(*small_v),
                                    "adamw_small"))]
    g_b, g_n1, g_n2, g_lb, g_og, g_qg, g_kg, g_sk = ([t[i] for t in sm] for i in range(8))

    dmod_all = small_all.reshape(N_DEV, SMALL_ROWS, D_MODEL)[:, 0:N_MOD].reshape(N_DEV, N_MOD * D_MODEL)
    dmod_cols = lax.dynamic_slice(dmod_all, (0, chip * ada_cols), (N_DEV, ada_cols))
    ada = _ada_grad_adamw(c_all.T, dmod_cols, w_ada[0], m_w_ada[0], v_w_ada[0])

    def ordered(k):
        lead = lambda a: a[None]
        return (lead(ada[k]), g_b[k], g_n1[k], lead(big[0][k]), g_lb[k], g_og[k], g_qg[k], g_kg[k], g_sk[k],
                lead(big[1][k]), lead(big[2][k]), lead(big[3][k]), g_n2[k], lead(big[4][k]), lead(big[5][k]))

    return (loss, grad_x[None]) + ordered(0) + ordered(1) + ordered(2) + ordered(3)
```

```python
import functools

import jax
import jax.numpy as jnp
from jax import lax
from jax.experimental import pallas as pl
from jax.experimental.pallas import tpu as pltpu

F32 = jnp.float32
BF16 = jnp.bfloat16
HIGHEST = lax.Precision.HIGHEST
MESH = pl.DeviceIdType.MESH

D_MODEL = 2048
A_WIDTH = 1024
A_HEADS = 8
A_HEAD_DIM = 128
A_CHUNK = 64
B_WIDTH = 1024
B_HEAD_DIM = 64
B_GROUP = 4
B_KV_HEADS = 4
B_KV_WIDTH = 256
BLOCK = 128
MLP_HIDDEN = 8192
IN_WIDTH = 9728
N_MOD = 6
EPS = 1e-6
N_CHIPS = 4
N_DEV = 8

OFF_QA, OFF_FA, OFF_IA, OFF_GA = 0, 1024, 2048, 3072
OFF_QB, OFF_KB, OFF_VB = 4096, 5120, 5376
OFF_GATE_A, OFF_GATE_B = 5632, 7680

ADAM_LR = 0.001
ADAM_B1 = 0.9
ADAM_B2 = 0.999
ADAM_EPS = 1e-08
ADAM_WD = 0.01
ADAM_STEP = 10

VMEM_LIMIT_V7X = 48 * 1024 * 1024
NEG_BIG = -1e30


def _pcall(body, **kw):
    return pl.pallas_call(body, **kw)


def _params(sem=None, vmem=VMEM_LIMIT_V7X):
    return pltpu.CompilerParams(dimension_semantics=sem, vmem_limit_bytes=vmem)


def _sig(x):
    return 1.0 / (1.0 + jnp.exp(-x))


def _nn(a, b):
    return lax.dot_general(a.astype(BF16), b.astype(BF16), (((1,), (0,)), ((), ())), preferred_element_type=F32)


def _nt(a, b):
    return lax.dot_general(a.astype(BF16), b.astype(BF16), (((1,), (1,)), ((), ())), preferred_element_type=F32)


def _tn(a, b):
    return lax.dot_general(a.astype(BF16), b.astype(BF16), (((0,), (0,)), ((), ())), preferred_element_type=F32)


def _mm(a, b, *, name, ta=False, tb=False, bm=1024, bn=1024, bk=2048, out_dtypes=(F32,), epi=None, extras=()):
    if ta:
        K, M = a.shape
    else:
        M, K = a.shape
    if tb:
        N, K2 = b.shape
    else:
        K2, N = b.shape
    bm, bn, bk = min(bm, M), min(bn, N), min(bk, K)
    assert K == K2 and M % bm == 0 and N % bn == 0 and K % bk == 0, (name, a.shape, b.shape)
    nk = K // bk
    a_spec = pl.BlockSpec((bk, bm), lambda i, j, k: (k, i)) if ta else pl.BlockSpec((bm, bk), lambda i, j, k: (i, k))
    b_spec = pl.BlockSpec((bn, bk), lambda i, j, k: (j, k)) if tb else pl.BlockSpec((bk, bn), lambda i, j, k: (k, j))
    t_spec = pl.BlockSpec((bm, bn), lambda i, j, k: (i, j))
    dims = (((0 if ta else 1,), (1 if tb else 0,)), ((), ()))
    n_e, n_o = len(extras), len(out_dtypes)

    def body(*refs):
        a_ref, b_ref = refs[0], refs[1]
        e_refs = refs[2:2 + n_e]
        o_refs = refs[2 + n_e:2 + n_e + n_o]

        def finish(acc):
            outs = (acc,) if epi is None else epi(acc, *[e[...] for e in e_refs])
            for o_ref, o in zip(o_refs, outs):
                o_ref[...] = o.astype(o_ref.dtype)

        part = lax.dot_general(a_ref[...].astype(BF16), b_ref[...].astype(BF16), dims, preferred_element_type=F32)
        if nk == 1:
            finish(part)
        else:
            acc_ref = refs[-1]
            k = pl.program_id(2)

            @pl.when(k == 0)
            def _():
                acc_ref[...] = part

            @pl.when(k > 0)
            def _():
                acc_ref[...] += part

            @pl.when(k == nk - 1)
            def _():
                finish(acc_ref[...])

    out = _pcall(
        body, name=name, grid=(M // bm, N // bn, nk),
        in_specs=[a_spec, b_spec] + [t_spec] * n_e,
        out_specs=[t_spec] * n_o,
        out_shape=[jax.ShapeDtypeStruct((M, N), dt) for dt in out_dtypes],
        scratch_shapes=[pltpu.VMEM((bm, bn), F32)] if nk > 1 else [],
        compiler_params=_params(("parallel", "parallel", "arbitrary")),
    )(a, b, *extras)
    return out[0] if n_o == 1 else out


def _ada_fwd(c_all, w_ada, b_cols):
    n = w_ada.shape[1]
    bn = 512

    def body(c_ref, w_ref, b_ref, o_ref):
        cv = c_ref[...]
        sc = cv * _sig(cv)
        o_ref[...] = jnp.dot(sc, w_ref[...], precision=HIGHEST, preferred_element_type=F32) + b_ref[...]

    return _pcall(
        body, name="ada_fwd", grid=(n // bn,),
        in_specs=[pl.BlockSpec((N_DEV, D_MODEL), lambda j: (0, 0)), pl.BlockSpec((D_MODEL, bn), lambda j: (0, j)),
                  pl.BlockSpec((1, bn), lambda j: (0, j))],
        out_specs=pl.BlockSpec((N_DEV, bn), lambda j: (0, j)),
        out_shape=jax.ShapeDtypeStruct((N_DEV, n), F32),
        compiler_params=_params(("parallel",)),
    )(c_all, w_ada, b_cols)


ROWS_EW = 256


def _rms_fwd_math(x, gain, scale, shift):
    rstd = lax.rsqrt(jnp.mean(x * x, axis=-1, keepdims=True) + EPS)
    xhat = x * rstd
    n = xhat * gain
    return n * (1.0 + scale) + shift, xhat, n, rstd


def _rms_bwd_math(dh, xhat, n, rstd, gain, scale):
    dn = dh * (1.0 + scale)
    dxhat = dn * gain
    dx = rstd * (dxhat - xhat * jnp.mean(dxhat * xhat, axis=-1, keepdims=True))
    d_scale = jnp.sum(dh * n, axis=0, keepdims=True)
    d_shift = jnp.sum(dh, axis=0, keepdims=True)
    d_gain = jnp.sum(dn * xhat, axis=0, keepdims=True)
    return dx, d_scale, d_shift, d_gain


def _row_spec(w=D_MODEL, br=ROWS_EW):
    return pl.BlockSpec((br, w), lambda i: (i, 0))


def _vec_spec(r=8, w=D_MODEL):
    return pl.BlockSpec((r, w), lambda i: (0, 0))


def _norm1_fwd(x, gain, mod8):
    T = x.shape[0]

    def body(x_ref, g_ref, m_ref, h_ref):
        h, _, _, _ = _rms_fwd_math(x_ref[...], g_ref[...], m_ref[1:2, :], m_ref[0:1, :])
        h_ref[...] = h.astype(BF16)

    return _pcall(
        body, name="norm1_fwd", grid=(T // ROWS_EW,),
        in_specs=[_row_spec(), _vec_spec(1), _vec_spec()],
        out_specs=_row_spec(), out_shape=jax.ShapeDtypeStruct((T, D_MODEL), BF16),
        compiler_params=_params(("parallel",)),
    )(x, gain, mod8)


def _res_norm2_fwd(x, mo, gain, mod8):
    T = x.shape[0]

    def body(x_ref, mo_ref, g_ref, m_ref, x1_ref, h_ref):
        x1 = x_ref[...] + m_ref[2:3, :] * mo_ref[...]
        x1_ref[...] = x1
        h, _, _, _ = _rms_fwd_math(x1, g_ref[...], m_ref[4:5, :], m_ref[3:4, :])
        h_ref[...] = h.astype(BF16)

    return _pcall(
        body, name="res_norm2_fwd", grid=(T // ROWS_EW,),
        in_specs=[_row_spec(), _row_spec(), _vec_spec(1), _vec_spec()],
        out_specs=[_row_spec(), _row_spec()],
        out_shape=[jax.ShapeDtypeStruct((T, D_MODEL), F32), jax.ShapeDtypeStruct((T, D_MODEL), BF16)],
        compiler_params=_params(("parallel",)),
    )(x, mo, gain, mod8)


def _loss_bwd(x1, mlp, target, mod8):
    T = x1.shape[0]

    def body(x1_ref, mlp_ref, t_ref, m_ref, dy_ref, dmlp_ref, st_ref):
        i = pl.program_id(0)
        gate = m_ref[5:6, :]
        mlp_v = mlp_ref[...]
        err = x1_ref[...] + gate * mlp_v - t_ref[...]
        dy = err * (1.0 / D_MODEL)
        dy_ref[...] = dy
        dmlp_ref[...] = (dy * gate).astype(BF16)

        @pl.when(i == 0)
        def _():
            st_ref[...] = jnp.zeros_like(st_ref)

        st_ref[0:1, :] += jnp.sum(err * err, axis=0, keepdims=True)
        st_ref[1:2, :] += jnp.sum(dy * mlp_v, axis=0, keepdims=True)

    return _pcall(
        body, name="loss_bwd", grid=(T // ROWS_EW,),
        in_specs=[_row_spec(), _row_spec(), _row_spec(), _vec_spec()],
        out_specs=[_row_spec(), _row_spec(), _vec_spec()],
        out_shape=[jax.ShapeDtypeStruct((T, D_MODEL), F32), jax.ShapeDtypeStruct((T, D_MODEL), BF16),
                   jax.ShapeDtypeStruct((8, D_MODEL), F32)],
        compiler_params=_params(("arbitrary",)),
    )(x1, mlp, target, mod8)


def _norm2_bwd(dh2, x1, dy, mo, gain, mod8):
    T = x1.shape[0]

    def body(dh_ref, x1_ref, dy_ref, mo_ref, g_ref, m_ref, dx1_ref, dmo_ref, st_ref):
        i = pl.program_id(0)
        gain_v, scale = g_ref[...], m_ref[4:5, :]
        _, xhat, n, rstd = _rms_fwd_math(x1_ref[...], gain_v, scale, m_ref[3:4, :])
        dx, d_scale, d_shift, d_gain = _rms_bwd_math(dh_ref[...], xhat, n, rstd, gain_v, scale)
        dx1 = dy_ref[...] + dx
        dx1_ref[...] = dx1
        dmo_ref[...] = (dx1 * m_ref[2:3, :]).astype(BF16)

        @pl.when(i == 0)
        def _():
            st_ref[...] = jnp.zeros_like(st_ref)

        st_ref[0:1, :] += d_scale
        st_ref[1:2, :] += d_shift
        st_ref[2:3, :] += d_gain
        st_ref[3:4, :] += jnp.sum(dx1 * mo_ref[...], axis=0, keepdims=True)

    return _pcall(
        body, name="norm2_bwd", grid=(T // ROWS_EW,),
        in_specs=[_row_spec(), _row_spec(), _row_spec(), _row_spec(), _vec_spec(1), _vec_spec()],
        out_specs=[_row_spec(), _row_spec(), _vec_spec()],
        out_shape=[jax.ShapeDtypeStruct((T, D_MODEL), F32), jax.ShapeDtypeStruct((T, D_MODEL), BF16),
                   jax.ShapeDtypeStruct((8, D_MODEL), F32)],
        compiler_params=_params(("arbitrary",)),
    )(dh2, x1, dy, mo, gain, mod8)


def _norm1_bwd(dh, x, dx1, gain, mod8):
    T = x.shape[0]

    def body(dh_ref, x_ref, dx1_ref, g_ref, m_ref, dx_ref, st_ref):
        i = pl.program_id(0)
        gain_v, scale = g_ref[...], m_ref[1:2, :]
        _, xhat, n, rstd = _rms_fwd_math(x_ref[...], gain_v, scale, m_ref[0:1, :])
        dx, d_scale, d_shift, d_gain = _rms_bwd_math(dh_ref[...], xhat, n, rstd, gain_v, scale)
        dx_ref[...] = dx1_ref[...] + dx

        @pl.when(i == 0)
        def _():
            st_ref[...] = jnp.zeros_like(st_ref)

        st_ref[0:1, :] += d_scale
        st_ref[1:2, :] += d_shift
        st_ref[2:3, :] += d_gain

    return _pcall(
        body, name="norm1_bwd", grid=(T // ROWS_EW,),
        in_specs=[_row_spec(), _row_spec(), _row_spec(), _vec_spec(1), _vec_spec()],
        out_specs=[_row_spec(), _vec_spec()],
        out_shape=[jax.ShapeDtypeStruct((T, D_MODEL), F32), jax.ShapeDtypeStruct((8, D_MODEL), F32)],
        compiler_params=_params(("arbitrary",)),
    )(dh, x, dx1, gain, mod8)


MERGE_BC = 512


def _merge_specs():
    ga = pl.BlockSpec((ROWS_EW, MERGE_BC), lambda i, j: (i, OFF_GATE_A // MERGE_BC + j))
    gb = pl.BlockSpec((ROWS_EW, MERGE_BC), lambda i, j: (i, OFF_GATE_B // MERGE_BC + j))
    t = pl.BlockSpec((ROWS_EW, MERGE_BC), lambda i, j: (i, j))
    return ga, gb, t


def _merge_fwd(proj, ya, yb):
    T = proj.shape[0]
    ga, gb, t = _merge_specs()

    def body(ga_ref, gb_ref, ya_ref, yb_ref, o_ref):
        o_ref[...] = (_sig(ga_ref[...]) * ya_ref[...] + _sig(gb_ref[...]) * yb_ref[...]).astype(BF16)

    return _pcall(
        body, name="merge_fwd", grid=(T // ROWS_EW, D_MODEL // MERGE_BC),
        in_specs=[ga, gb, t, t], out_specs=t, out_shape=jax.ShapeDtypeStruct((T, D_MODEL), BF16),
        compiler_params=_params(("parallel", "parallel")),
    )(proj, proj, ya, yb)


def _merge_bwd(proj, ya, yb, dmerged):
    T = proj.shape[0]
    ga, gb, t = _merge_specs()

    def body(ga_ref, gb_ref, ya_ref, yb_ref, dm_ref, dya_ref, dyb_ref, dga_ref, dgb_ref):
        dm = dm_ref[...]
        sa, sb = _sig(ga_ref[...]), _sig(gb_ref[...])
        dya_ref[...] = (dm * sa).astype(BF16)
        dyb_ref[...] = (dm * sb).astype(BF16)
        dga_ref[...] = (dm * ya_ref[...] * sa * (1.0 - sa)).astype(BF16)
        dgb_ref[...] = (dm * yb_ref[...] * sb * (1.0 - sb)).astype(BF16)

    sh = jax.ShapeDtypeStruct((T, D_MODEL), BF16)
    return _pcall(
        body, name="merge_bwd", grid=(T // ROWS_EW, D_MODEL // MERGE_BC),
        in_specs=[ga, gb, t, t, t], out_specs=[t, t, t, t], out_shape=[sh, sh, sh, sh],
        compiler_params=_params(("parallel", "parallel")),
    )(proj, proj, ya, yb, dmerged)


def _hgrn_rows(T):
    return 512 if T >= 1024 else 128


def _lower_bound(lbl):
    e = jnp.exp(lbl - jnp.max(lbl, axis=0, keepdims=True))
    return e[0:1, :] / (e[0:1, :] + e[1:2, :])


def _hgrn_chunk_fwd(q, fl, v, lb, st):
    C = A_CHUNK
    sg = _sig(fl)
    f = lb + (1.0 - lb) * sg
    lf = jnp.log(f)
    k = 1.0 - f
    sq = _sig(q)
    qf = q * sq
    row = lax.broadcasted_iota(jnp.int32, (C, C), 0)
    col = lax.broadcasted_iota(jnp.int32, (C, C), 1)
    causal = row >= col
    b = jnp.dot(causal.astype(F32), lf, precision=HIGHEST, preferred_element_type=F32)
    bm = b[C // 2 - 1:C // 2, :]
    bl = b[C - 1:C, :]
    e_q, e_k = jnp.exp(b - bm), jnp.exp(bm - b)
    e_b, e_l = jnp.exp(b), jnp.exp(bl - b)
    qd, kd = qf * e_q, k * e_k
    qe, ke = qf * e_b, k * e_l
    att = jnp.where(causal, _nt(qd, kd), 0.0)
    o = _nn(att, v) + _nt(qe, st)
    dec = jnp.exp(bl)
    st_next = st * dec + _tn(v, ke)
    return dict(sg=sg, f=f, k=k, sq=sq, qf=qf, causal=causal, e_q=e_q, e_k=e_k, e_b=e_b, e_l=e_l, qd=qd, kd=kd,
                qe=qe, ke=ke, att=att, o=o, dec=dec, st_next=st_next)


def _hgrn_fwd(proj, lb_logits, o_gain):
    T = proj.shape[0]
    BR = _hgrn_rows(T)
    cps = BR // A_CHUNK
    K = A_HEAD_DIM

    def col(off):
        return pl.BlockSpec((BR, K), lambda h, cb: (cb, off // K + h))

    def body(q_ref, f_ref, i_ref, g_ref, lbl_ref, og_ref, o_ref, s_ref, st):
        @pl.when(pl.program_id(1) == 0)
        def _():
            st[...] = jnp.zeros_like(st)

        lb = _lower_bound(lbl_ref[...])
        gain = og_ref[...]
        for ci in range(cps):
            r = slice(ci * A_CHUNK, (ci + 1) * A_CHUNK)
            s_prev = st[...]
            s_ref[0, ci] = s_prev
            c = _hgrn_chunk_fwd(q_ref[r, :], f_ref[r, :], i_ref[r, :], lb, s_prev)
            st[...] = c["st_next"]
            o = c["o"]
            on = o * lax.rsqrt(jnp.mean(o * o, axis=-1, keepdims=True) + EPS)
            g = g_ref[r, :]
            o_ref[r, :] = (on * gain * (g * _sig(g))).astype(BF16)

    return _pcall(
        body, name="hgrn_fwd", grid=(A_HEADS, T // BR),
        in_specs=[col(OFF_QA), col(OFF_FA), col(OFF_IA), col(OFF_GA),
                  pl.BlockSpec((2, K), lambda h, cb: (0, h)), pl.BlockSpec((1, K), lambda h, cb: (0, h))],
        out_specs=[pl.BlockSpec((BR, K), lambda h, cb: (cb, h)),
                   pl.BlockSpec((1, cps, K, K), lambda h, cb: (h, cb, 0, 0))],
        out_shape=[jax.ShapeDtypeStruct((T, A_WIDTH), BF16),
                   jax.ShapeDtypeStruct((A_HEADS, T // A_CHUNK, K, K), F32)],
        scratch_shapes=[pltpu.VMEM((K, K), F32)],
        compiler_params=_params(("parallel", "arbitrary")),
    )(proj, proj, proj, proj, lb_logits, o_gain)


def _hgrn_bwd(proj, lb_logits, o_gain, states, do):
    T = proj.shape[0]
    BR = _hgrn_rows(T)
    cps = BR // A_CHUNK
    ncb = T // BR
    K, C = A_HEAD_DIM, A_CHUNK

    def col(off):
        return pl.BlockSpec((BR, K), lambda h, cb: (ncb - 1 - cb, off // K + h))

    def body(q_ref, f_ref, i_ref, g_ref, lbl_ref, og_ref, s_ref, do_ref,
             dq_ref, df_ref, di_ref, dg_ref, dlb_ref, dog_ref, dst):
        @pl.when(pl.program_id(1) == 0)
        def _():
            dst[...] = jnp.zeros_like(dst)
            dlb_ref[...] = jnp.zeros_like(dlb_ref)
            dog_ref[...] = jnp.zeros_like(dog_ref)

        lb = _lower_bound(lbl_ref[...])
        gain = og_ref[...]
        row = lax.broadcasted_iota(jnp.int32, (C, K), 0)
        for ci in reversed(range(cps)):
            r = slice(ci * C, (ci + 1) * C)
            st = s_ref[0, ci]
            v = i_ref[r, :]
            q = q_ref[r, :]
            c = _hgrn_chunk_fwd(q, f_ref[r, :], v, lb, st)
            dst_next = dst[...]
            o = c["o"]
            rn = lax.rsqrt(jnp.mean(o * o, axis=-1, keepdims=True) + EPS)
            on = o * rn
            g = g_ref[r, :]
            sgg = _sig(g)
            dy = do_ref[r, :]
            d_ong = dy * (g * sgg)
            dg_ref[r, :] = (dy * (on * gain) * (sgg * (1.0 + g * (1.0 - sgg)))).astype(BF16)
            dog_ref[0:1, :] += jnp.sum(d_ong * on, axis=0, keepdims=True)
            d_on = d_ong * gain
            d_o = rn * (d_on - on * jnp.mean(d_on * on, axis=-1, keepdims=True))
            datt = jnp.where(c["causal"], _nt(d_o, v), 0.0)
            dv = _tn(c["att"], d_o) + _nt(c["ke"], dst_next)
            dqd = _nn(datt, c["kd"])
            dkd = _tn(datt, c["qd"])
            dqe = _nn(d_o, st)
            dke = _nn(v, dst_next)
            dst[...] = dst_next * c["dec"] + _tn(d_o, c["qe"])
            d_dec = jnp.sum(dst_next * st, axis=0, keepdims=True)
            t_q, t_k = dqd * c["qd"], dkd * c["kd"]
            t_e, t_l = dqe * c["qe"], dke * c["ke"]
            db = t_q - t_k + t_e - t_l
            dbm = jnp.sum(t_k - t_q, axis=0, keepdims=True)
            dbl = jnp.sum(t_l, axis=0, keepdims=True) + d_dec * c["dec"]
            db = db + jnp.where(row == C // 2 - 1, dbm, 0.0) + jnp.where(row == C - 1, dbl, 0.0)
            upper = jnp.logical_not(c["causal"]) | (lax.broadcasted_iota(jnp.int32, (C, C), 0)
                                                    == lax.broadcasted_iota(jnp.int32, (C, C), 1))
            dlf = jnp.dot(upper.astype(F32), db, precision=HIGHEST, preferred_element_type=F32)
            dqf = dqd * c["e_q"] + dqe * c["e_b"]
            sq = c["sq"]
            dq_ref[r, :] = (dqf * (sq * (1.0 + q * (1.0 - sq)))).astype(BF16)
            dk = dkd * c["e_k"] + dke * c["e_l"]
            df = dlf / c["f"] - dk
            sg = c["sg"]
            df_ref[r, :] = (df * (1.0 - lb) * sg * (1.0 - sg)).astype(BF16)
            dlb_ref[0:1, :] += jnp.sum(df * (1.0 - sg), axis=0, keepdims=True)
            di_ref[r, :] = dv.astype(BF16)

    ocol = pl.BlockSpec((BR, K), lambda h, cb: (ncb - 1 - cb, h))
    vec =pl.BlockSpec((8, K), lambda h, cb: (0, h))
    return _pcall(
        body, name="hgrn_bwd", grid=(A_HEADS, ncb),
        in_specs=[col(OFF_QA), col(OFF_FA), col(OFF_IA), col(OFF_GA),
                  pl.BlockSpec((2, K), lambda h, cb: (0, h)), pl.BlockSpec((1, K), lambda h, cb: (0, h)),
                  pl.BlockSpec((1, cps, K, K), lambda h, cb: (h, ncb - 1 - cb, 0, 0)),
                  pl.BlockSpec((BR, K), lambda h, cb: (ncb - 1 - cb, h))],
        out_specs=[ocol, ocol, ocol, ocol, vec, vec],
        out_shape=[jax.ShapeDtypeStruct((T, A_WIDTH), BF16)] * 4 + [jax.ShapeDtypeStruct((8, A_WIDTH), F32)] * 2,
        scratch_shapes=[pltpu.VMEM((K, K), F32)],
        compiler_params=_params(("parallel", "arbitrary")),
    )(proj, proj, proj, proj, lb_logits, o_gain, states, do)


def _head_norm(x):
    r = lax.rsqrt(jnp.mean(x * x, axis=-1, keepdims=True) + EPS)
    return x * r, r


def _head_norm_bwd(dy, xn, r, gain):
    dxn = dy * gain
    return r * (dxn - xn * jnp.mean(dxn * xn, axis=-1, keepdims=True)), jnp.sum(dy * xn, axis=0, keepdims=True)


def _swa_mask(has_prev):
    rows = B_GROUP * BLOCK
    r = lax.broadcasted_iota(jnp.int32, (rows, 2 * BLOCK), 0) % BLOCK
    c = lax.broadcasted_iota(jnp.int32, (rows, 2 * BLOCK), 1)
    rel = r + BLOCK - c
    return (rel >= 0) & (rel < BLOCK) & ((c >= BLOCK) | has_prev)


def _swa_head_fwd(j, q_ref, kp_ref, kc_ref, vp_ref, vc_ref, qg, kg, sk_ref, mask):
    hs = slice(j * B_HEAD_DIM, (j + 1) * B_HEAD_DIM)
    kcat = jnp.concatenate([kp_ref[:, hs], kc_ref[:, hs]], axis=0)
    vcat = jnp.concatenate([vp_ref[:, hs], vc_ref[:, hs]], axis=0)
    qs = jnp.concatenate([q_ref[:, pl.ds((j * B_GROUP + g) * B_HEAD_DIM, B_HEAD_DIM)] for g in range(B_GROUP)], axis=0)
    kn, kr = _head_norm(kcat)
    qn, qr = _head_norm(qs)
    kh, qh = kn * kg, qn * qg
    s = jnp.where(mask, _nt(qh, kh) * (B_HEAD_DIM ** -0.5), NEG_BIG)
    sink = jnp.concatenate(
        [jnp.broadcast_to(sk_ref[0:1, pl.ds(j * B_GROUP + g, 1)], (BLOCK, 1)) for g in range(B_GROUP)], axis=0)
    m = jnp.maximum(jnp.max(s, axis=-1, keepdims=True), sink)
    p = jnp.exp(s - m)
    e_sink = jnp.exp(sink - m)
    inv = 1.0 / (jnp.sum(p, axis=-1, keepdims=True) + e_sink)
    prob = p * inv
    return dict(vcat=vcat, kn=kn, kr=kr, qn=qn, qr=qr, kh=kh, qh=qh, prob=prob, p_sink=e_sink * inv)


def _swa_in_specs(nb, last):
    def qi(n):
        return jnp.minimum(n, last)

    q = pl.BlockSpec((BLOCK, B_WIDTH), lambda n: (qi(n), OFF_QB // B_WIDTH))
    kc = pl.BlockSpec((BLOCK, B_KV_WIDTH), lambda n: (qi(n), OFF_KB // B_KV_WIDTH))
    kp = pl.BlockSpec((BLOCK, B_KV_WIDTH), lambda n: (jnp.maximum(qi(n) - 1, 0), OFF_KB // B_KV_WIDTH))
    vc = pl.BlockSpec((BLOCK, B_KV_WIDTH), lambda n: (qi(n), OFF_VB // B_KV_WIDTH))
    vp = pl.BlockSpec((BLOCK, B_KV_WIDTH), lambda n: (jnp.maximum(qi(n) - 1, 0), OFF_VB // B_KV_WIDTH))
    small = [pl.BlockSpec((1, B_HEAD_DIM), lambda n: (0, 0)), pl.BlockSpec((1, B_HEAD_DIM), lambda n: (0, 0)),
             pl.BlockSpec((1, B_GROUP * B_KV_HEADS), lambda n: (0, 0))]
    return [q, kp, kc, vp, vc] + small


def _swa_fwd(proj, q_gain, k_gain, sinks):
    T = proj.shape[0]
    nb = T // BLOCK

    def body(q_ref, kp_ref, kc_ref, vp_ref, vc_ref, qg_ref, kg_ref, sk_ref, o_ref):
        mask = _swa_mask(pl.program_id(0) > 0)
        for j in range(B_KV_HEADS):
            c = _swa_head_fwd(j, q_ref, kp_ref, kc_ref, vp_ref, vc_ref, qg_ref[...], kg_ref[...], sk_ref, mask)
            o = _nn(c["prob"], c["vcat"])
            for g in range(B_GROUP):
                o_ref[:, pl.ds((j * B_GROUP + g) * B_HEAD_DIM, B_HEAD_DIM)] = o[g * BLOCK:(g + 1) * BLOCK].astype(BF16)

    return _pcall(
        body, name="swa_fwd", grid=(nb,),
        in_specs=_swa_in_specs(nb, nb - 1),
        out_specs=pl.BlockSpec((BLOCK, B_WIDTH), lambda n: (n, 0)),
        out_shape=jax.ShapeDtypeStruct((T, B_WIDTH), BF16),
        compiler_params=_params(("parallel",)),
    )(proj, proj, proj, proj, proj, q_gain, k_gain, sinks)


def _swa_bwd(proj, q_gain, k_gain, sinks, do):
    T = proj.shape[0]
    nb = T // BLOCK
    scale = B_HEAD_DIM ** -0.5

    def body(q_ref, kp_ref, kc_ref, vp_ref, vc_ref, qg_ref, kg_ref, sk_ref, do_ref,
             dq_ref, dkv_ref, sm_ref, ck, cv):
        n = pl.program_id(0)

        @pl.when(n == 0)
        def _():
            ck[...] = jnp.zeros_like(ck)
            cv[...] = jnp.zeros_like(cv)
            sm_ref[...] = jnp.zeros_like(sm_ref)

        @pl.when(n < nb)
        def _():
            mask = _swa_mask(n > 0)
            qg, kg = qg_ref[...], kg_ref[...]
            lane = lax.broadcasted_iota(jnp.int32, (1, BLOCK), 1)
            for j in range(B_KV_HEADS):
                hs = slice(j * B_HEAD_DIM, (j + 1) * B_HEAD_DIM)
                vs = slice(B_KV_WIDTH + j * B_HEAD_DIM, B_KV_WIDTH + (j + 1) * B_HEAD_DIM)
                c = _swa_head_fwd(j, q_ref, kp_ref, kc_ref, vp_ref, vc_ref, qg, kg, sk_ref, mask)
                d_out = jnp.concatenate(
                    [do_ref[:, pl.ds((j * B_GROUP + g) * B_HEAD_DIM, B_HEAD_DIM)] for g in range(B_GROUP)], axis=0)
                prob = c["prob"]
                out = _nn(prob, c["vcat"])
                delta = jnp.sum(d_out * out, axis=-1, keepdims=True)
                ds = prob * (_nt(d_out, c["vcat"]) - delta)
                d_sink = -c["p_sink"] * delta
                dqh = _nn(ds, c["kh"]) * scale
                dkh = _tn(ds, c["qh"]) * scale
                dv = _tn(prob, d_out)
                dq, dqg = _head_norm_bwd(dqh, c["qn"], c["qr"], qg)
                dk, dkg = _head_norm_bwd(dkh, c["kn"], c["kr"], kg)
                sm_ref[0:1, 0:B_HEAD_DIM] += dqg
                sm_ref[1:2, 0:B_HEAD_DIM] += dkg
                for g in range(B_GROUP):
                    dq_ref[:, pl.ds((j * B_GROUP + g) * B_HEAD_DIM, B_HEAD_DIM)] = dq[g * BLOCK:(g + 1) * BLOCK].astype(BF16)
                    tot = jnp.sum(d_sink[g * BLOCK:(g + 1) * BLOCK], axis=0, keepdims=True)
                    sm_ref[2:3, :] += jnp.where(lane == j * B_GROUP + g, tot, 0.0)
                dkv_ref[:, hs] = (ck[:, hs] + dk[0:BLOCK]).astype(BF16)
                dkv_ref[:, vs] = (cv[:, hs] + dv[0:BLOCK]).astype(BF16)
                ck[:, hs] = dk[BLOCK:2 * BLOCK]
                cv[:, hs] = dv[BLOCK:2 * BLOCK]

        @pl.when(n == nb)
        def _():
            dkv_ref[:, 0:B_KV_WIDTH] = ck[...].astype(BF16)
            dkv_ref[:, B_KV_WIDTH:2 * B_KV_WIDTH] = cv[...].astype(BF16)

    return _pcall(
        body, name="swa_bwd", grid=(nb + 1,),
        in_specs=_swa_in_specs(nb, nb - 1) + [pl.BlockSpec((BLOCK, B_WIDTH), lambda n: (jnp.minimum(n, nb - 1), 0))],
        out_specs=[pl.BlockSpec((BLOCK, B_WIDTH), lambda n: (jnp.minimum(n, nb - 1), 0)),
                   pl.BlockSpec((BLOCK, 2 * B_KV_WIDTH), lambda n: (jnp.maximum(n - 1, 0), 0)),
                   pl.BlockSpec((8, BLOCK), lambda n: (0, 0))],
        out_shape=[jax.ShapeDtypeStruct((T, B_WIDTH), BF16), jax.ShapeDtypeStruct((T, 2 * B_KV_WIDTH), BF16),
                   jax.ShapeDtypeStruct((8, BLOCK), F32)],
        scratch_shapes=[pltpu.VMEM((BLOCK, B_KV_WIDTH), F32), pltpu.VMEM((BLOCK, B_KV_WIDTH), F32)],
        compiler_params=_params(("arbitrary",)),
    )(proj, proj, proj, proj, proj, q_gain, k_gain, sinks, do)


def _local_step(x, target, mod8, norm1_gain, norm2_gain, lb_logits, o_gain, q_gain, k_gain, sinks,
                w_in, w_a, w_b, w_out, w_mi, w_mo):
    relu2 = lambda u: (u, jnp.square(jnp.maximum(u, 0.0)))
    h = _norm1_fwd(x, norm1_gain, mod8)
    proj = _mm(h, w_in, name="mm_proj", bn=512)
    o_a, states = _hgrn_fwd(proj, lb_logits, o_gain)
    o_b = _swa_fwd(proj, q_gain, k_gain, sinks)
    ya = _mm(o_a, w_a, name="mm_branch_a")
    yb = _mm(o_b, w_b, name="mm_branch_b")
    merged = _merge_fwd(proj, ya, yb)
    mo = _mm(merged, w_out, name="mm_out")
    x1, h2 = _res_norm2_fwd(x, mo, norm2_gain, mod8)
    u, act = _mm(h2, w_mi, name="mm_mlp_in", out_dtypes=(F32, BF16), epi=relu2)
    mlp = _mm(act, w_mo, name="mm_mlp_out")
    dy, dmlp, st_loss = _loss_bwd(x1, mlp, target, mod8)
    du = _mm(dmlp, w_mo, name="mm_d_act", tb=True, out_dtypes=(BF16,), extras=(u,),
             epi=lambda acc, uu: (acc * (2.0 * jnp.maximum(uu, 0.0)),))
    g_mo = _mm(act, dmlp, name="mm_g_mlp_out", ta=True)
    dh2 = _mm(du, w_mi, name="mm_d_h2", tb=True)
    g_mi = _mm(h2, du, name="mm_g_mlp_in", ta=True)
    dx1, dmo, st_n2 = _norm2_bwd(dh2, x1, dy, mo, norm2_gain, mod8)
    dmerged = _mm(dmo, w_out, name="mm_d_merged", tb=True)
    g_out = _mm(merged, dmo, name="mm_g_out", ta=True)
    dya, dyb, dga, dgb = _merge_bwd(proj, ya, yb, dmerged)
    do_a = _mm(dya, w_a, name="mm_d_oa", tb=True)
    g_a = _mm(o_a, dya, name="mm_g_branch_a", ta=True)
    do_b = _mm(dyb, w_b, name="mm_d_ob", tb=True)
    g_b = _mm(o_b, dyb, name="mm_g_branch_b", ta=True)
    dqa, dfa, dia, dgga, d_lb, d_og = _hgrn_bwd(proj, lb_logits, o_gain, states, do_a)
    dqb, dkvb, st_swa = _swa_bwd(proj, q_gain, k_gain, sinks, do_b)
    dproj = jnp.concatenate([dqa, dfa, dia, dgga, dqb, dkvb, dga, dgb], axis=1)
    dh = _mm(dproj, w_in, name="mm_d_h", tb=True, bk=2432)
    g_in = _mm(h, dproj, name="mm_g_in", ta=True, bn=512)
    grad_x, st_n1 = _norm1_bwd(dh, x, dx1, norm1_gain, mod8)
    stats = dict(loss=st_loss, n2=st_n2, n1=st_n1, d_lb=d_lb, d_og=d_og, swa=st_swa)
    return grad_x, (g_in, g_a, g_b, g_out, g_mi, g_mo), stats


def _ew_rows(rows, cols):
    br = 8
    while br * 2 <= rows and br * 2 * cols * 4 <= (1 << 20) and rows % (br * 2) == 0:
        br *= 2
    return br


def _cast_into_full(shard, w, chip_arr, name):
    sr, sc = shard.shape
    R, C, by_col = W_SHAPES[w]
    br = _ew_rows(sr, sc)
    nb = sr // br
    out_map = (lambda i, chip: (i, chip[0])) if by_col else (lambda i, chip: (chip[0] * nb + i, 0))

    def body(chip_ref, w_ref, o_ref):
        o_ref[...] = w_ref[...].astype(BF16)

    return _pcall(
        body, name=name,
        grid_spec=pltpu.PrefetchScalarGridSpec(
            num_scalar_prefetch=1, grid=(nb,),
            in_specs=[pl.BlockSpec((br, sc), lambda i, chip: (i, 0))],
            out_specs=pl.BlockSpec((br, sc), out_map)),
        out_shape=jax.ShapeDtypeStruct((R, C), BF16), compiler_params=_params(("parallel",)))(chip_arr, shard)


def _adamw_math(w, g, m, v):
    m = ADAM_B1 * m + (1.0 - ADAM_B1) * g
    v = ADAM_B2 * v + (1.0 - ADAM_B2) * (g * g)
    m_hat = m / (1.0 - ADAM_B1 ** ADAM_STEP)
    v_hat = v / (1.0 - ADAM_B2 ** ADAM_STEP)
    delta = -ADAM_LR * (m_hat / (jnp.sqrt(v_hat) + ADAM_EPS) + ADAM_WD * w)
    return delta, m, v


def _adamw(w, g, m, v, name):
    R, C = w.shape
    br = _ew_rows(R, C)
    spec = pl.BlockSpec((br, C), lambda i: (i, 0))

    def body(w_ref, g_ref, m_ref, v_ref, d_ref, nm_ref, nv_ref):
        d_ref[...], nm_ref[...], nv_ref[...] = _adamw_math(w_ref[...], g_ref[...], m_ref[...], v_ref[...])

    sh = jax.ShapeDtypeStruct((R, C), F32)
    return _pcall(body, name=name, grid=(R // br,), in_specs=[spec] * 4, out_specs=[spec] * 3, out_shape=[sh] * 3,
                  compiler_params=_params(("parallel",)))(w, g, m, v)


def _adamw_halves(w, own, other, m, v, c_arr, name):
    R, C = w.shape
    hr = R // 2
    br = _ew_rows(hr, C)
    nb = hr // br
    full = pl.BlockSpec((br, C), lambda h, i, c_ref: (h * nb + i, 0))
    half = pl.BlockSpec((br, C), lambda h, i, c_ref: (i, 0))

    def body(c_ref, w_ref, own_ref, oth_ref, m_ref, v_ref, g_ref, d_ref, nm_ref, nv_ref):
        g = jnp.where(pl.program_id(0) == c_ref[0], own_ref[...], oth_ref[...])
        g_ref[...] = g
        d_ref[...], nm_ref[...], nv_ref[...] = _adamw_math(w_ref[...], g, m_ref[...], v_ref[...])

    sh = jax.ShapeDtypeStruct((R, C), F32)
    return _pcall(
        body, name=name,
        grid_spec=pltpu.PrefetchScalarGridSpec(
            num_scalar_prefetch=1, grid=(2, nb), in_specs=[full, half, half, full, full], out_specs=[full] * 4),
        out_shape=[sh] * 4, compiler_params=_params(("parallel", "parallel")))(c_arr, w, own, other, m, v)


def _ada_grad_adamw(c_t, dmod, w, m, v):
    R, C = w.shape
    br = _ew_rows(R, C)
    spec = pl.BlockSpec((br, C), lambda i: (i, 0))

    def body(c_ref, dm_ref, w_ref, m_ref, v_ref, g_ref, d_ref, nm_ref, nv_ref):
        cv = c_ref[...]
        sc = cv * _sig(cv)
        g = sc[:, 0:1] * dm_ref[0:1, :]
        for b in range(1, N_DEV):
            g = g + sc[:, b:b + 1] * dm_ref[b:b + 1, :]
        g_ref[...] = g
        d_ref[...], nm_ref[...], nv_ref[...] = _adamw_math(w_ref[...], g, m_ref[...], v_ref[...])

    sh = jax.ShapeDtypeStruct((R, C), F32)
    return _pcall(
        body, name="ada_grad_adamw", grid=(R // br,),
        in_specs=[pl.BlockSpec((br, N_DEV), lambda i: (i, 0)), pl.BlockSpec((N_DEV, C), lambda i: (0, 0)), spec, spec, spec],
        out_specs=[spec] * 4, out_shape=[sh] * 4, compiler_params=_params(("parallel",)))(c_t, dmod, w, m, v)


SMALL_ROWS = 16


def _small_sum(small_all, lb_logits):
    def body(s_ref, lbl_ref, o_ref):
        acc = s_ref[0:SMALL_ROWS, :]
        for d in range(1, N_DEV):
            acc = acc + s_ref[d * SMALL_ROWS:(d + 1) * SMALL_ROWS, :]
        o_ref[...] = acc
        z = lbl_ref[...]
        e = jnp.exp(z - jnp.max(z, axis=0, keepdims=True))
        p0 = e[0:1, :] / (e[0:1, :] + e[1:2, :])
        dz = acc[8:9, 0:A_WIDTH] * p0 * (1.0 - p0)
        o_ref[8:9, 0:A_WIDTH] = dz
        o_ref[10:11, 0:A_WIDTH] = -dz

    return _pcall(body, name="small_sum", out_shape=jax.ShapeDtypeStruct((SMALL_ROWS, D_MODEL), F32),
                  in_specs=[pl.BlockSpec(memory_space=pltpu.VMEM)] * 2, out_specs=pl.BlockSpec(memory_space=pltpu.VMEM),
                  compiler_params=_params())(small_all, lb_logits)


RELATIONS = ((1, 0), (0, 1), (1, 1))
ANY = pl.BlockSpec(memory_space=pl.ANY)


def _place():
    x, y, c = lax.axis_index("x"), lax.axis_index("y"), lax.axis_index("c")
    return x, y, c


def _allgather_small(x_shard, name):
    m_per, n = x_shard.shape

    def body(x_ref, out_ref, send_sems, recv_sems, local_sem):
        x, y, c = _place()
        me, sibling = (x, y, c), (x, y, 1 - c)
        chips = [(1 - x, y), (x, 1 - y), (1 - x, 1 - y)]

        def rows(px, py, pc):
            return out_ref.at[pl.ds((4 * px + 2 * py + pc) * m_per, m_per), :]

        def copy(k, block, to, src=None):
            return pltpu.make_async_remote_copy(
                src_ref=rows(*block) if src is None else src, dst_ref=rows(*block),
                send_sem=send_sems.at[k], recv_sem=recv_sems.at[k], device_id=to, device_id_type=MESH)

        mine = pltpu.make_async_copy(x_ref, rows(*me), local_sem)
        mine.start()
        first = [copy(0, me, sibling, src=x_ref)]
        first += [copy(1 + j, me, (*chip, c), src=x_ref) for j, chip in enumerate(chips)]
        for cp in first:
            cp.start()
        passed = [copy(4 + j, (*chip, c), sibling) for j, chip in enumerate(chips)]
        for j, chip in enumerate(chips):
            copy(1 + j, (*chip, c), me).wait_recv()
            passed[j].start()
        copy(0, sibling, me).wait_recv()
        for j, chip in enumerate(chips):
            copy(4 + j, (*chip, 1 - c), me).wait_recv()
        for cp in first + passed:
            cp.wait_send()
        mine.wait()

    return _pcall(
        body, name=name, out_shape=jax.ShapeDtypeStruct((N_DEV * m_per, n), x_shard.dtype),
        in_specs=[pl.BlockSpec(memory_space=pltpu.VMEM)], out_specs=pl.BlockSpec(memory_space=pltpu.VMEM),
        scratch_shapes=[pltpu.SemaphoreType.DMA((7,)), pltpu.SemaphoreType.DMA((7,)), pltpu.SemaphoreType.DMA],
        compiler_params=_params(),
    )(x_shard)


W_SHAPES = ((D_MODEL, IN_WIDTH, True), (A_WIDTH, D_MODEL, True), (B_WIDTH, D_MODEL, True),
            (D_MODEL, D_MODEL, False), (D_MODEL, MLP_HIDDEN, True), (MLP_HIDDEN, D_MODEL, False))
N_W = len(W_SHAPES)


def _shard_shape(w):
    R, C, by_col = W_SHAPES[w]
    return (R, C // N_CHIPS) if by_col else (R // N_CHIPS, C)


def _half_shape(w):
    sr, sc = _shard_shape(w)
    return sr // 2, sc


def _region(full_ref, w, chip, half):
    sr, sc = _shard_shape(w)
    by_col = W_SHAPES[w][2]
    r0, c0 = (0, chip * sc) if by_col else (chip * sr, 0)
    if half is None:
        return full_ref.at[pl.ds(r0, sr), pl.ds(c0, sc)]
    return full_ref.at[pl.ds(r0 + half * (sr // 2), sr // 2), pl.ds(c0, sc)]


def _gather_weights(partials):
    def body(*refs):
        i_refs, f_refs = refs[:N_W], refs[N_W:2 * N_W]
        send, recv, p_send, p_recv = refs[2 * N_W:]
        x, y, c = _place()
        me = 4 * x + 2 * y + c
        for d in range(N_DEV):
            @pl.when(me == d)
            def _(d=d):
                chip, dc = d >> 1, d & 1
                sends = []
                for w in range(N_W):
                    for k, (rx, ry) in enumerate(RELATIONS):
                        sends.append(pltpu.make_async_remote_copy(
                            src_ref=_region(i_refs[w], w, chip, dc), dst_ref=_region(f_refs[w], w, chip, dc),
                            send_sem=send.at[w * 3 + k], recv_sem=recv.at[w * 3 + k],
                            device_id=(x ^ rx, y ^ ry, c), device_id_type=MESH))
                        sends[-1].start()
                for w in range(N_W):
                    for k, (rx, ry) in enumerate(RELATIONS):
                        got = _region(f_refs[w], w, chip ^ (2 * rx + ry), dc)
                        pltpu.make_async_remote_copy(
                            src_ref=got, dst_ref=got, send_sem=send.at[w * 3 + k], recv_sem=recv.at[w * 3 + k],
                            device_id=(x, y, c), device_id_type=MESH).wait_recv()
                        sends.append(pltpu.make_async_remote_copy(
                            src_ref=got, dst_ref=got, send_sem=p_send.at[w * 3 + k], recv_sem=p_recv.at[w * 3 + k],
                            device_id=(x, y, 1 - c), device_id_type=MESH))
                        sends[-1].start()
                for w in range(N_W):
                    for k, (rx, ry) in enumerate(RELATIONS):
                        got = _region(f_refs[w], w, chip ^ (2 * rx + ry), 1 - dc)
                        pltpu.make_async_remote_copy(
                            src_ref=got, dst_ref=got, send_sem=p_send.at[w * 3 + k], recv_sem=p_recv.at[w * 3 + k],
                            device_id=(x, y, c), device_id_type=MESH).wait_recv()
                for cp in sends:
                    cp.wait_send()

    n_sem = 3 * N_W
    return _pcall(
        body, name="gather_weights",
        out_shape=[jax.ShapeDtypeStruct(W_SHAPES[w][:2], BF16) for w in range(N_W)],
        in_specs=[ANY] * N_W, out_specs=[ANY] * N_W, input_output_aliases={w: w for w in range(N_W)},
        scratch_shapes=[pltpu.SemaphoreType.DMA((n_sem,)) for _ in range(4)],
        compiler_params=_params(),
    )(*partials)


def _grad_view(g, w):
    R, C, by_col = W_SHAPES[w]
    return g.reshape(1, 2, R // 2, C) if by_col else g.reshape(N_CHIPS, 2, R // N_CHIPS // 2, C)


def _sibling_exchange(g4s):
    pieces = [(w, p) for w in range(N_W) for p in range(g4s[w].shape[0])]

    def body(*refs):
        g_refs, r_refs = refs[:N_W], refs[N_W:2 * N_W]
        send, recv = refs[2 * N_W:]
        x, y, c = _place()
        for dc in range(2):
            @pl.when(c == dc)
            def _(dc=dc):
                cps = [pltpu.make_async_remote_copy(
                    src_ref=g_refs[w].at[p, 1 - dc], dst_ref=r_refs[w].at[p], send_sem=send.at[i], recv_sem=recv.at[i],
                    device_id=(x, y, 1 - c), device_id_type=MESH) for i, (w, p) in enumerate(pieces)]
                for cp in cps:
                    cp.start()
                for cp in cps:
                    cp.wait()

    return _pcall(
        body, name="sibling_exchange",
        out_shape=[jax.ShapeDtypeStruct((g.shape[0],) + g.shape[2:], F32) for g in g4s],
        in_specs=[ANY] * N_W, out_specs=[ANY] * N_W,
        scratch_shapes=[pltpu.SemaphoreType.DMA((len(pieces),)), pltpu.SemaphoreType.DMA((len(pieces),))],
        compiler_params=_params(),
    )(*g4s)


def _pair_sum(g4, other, c_arr, name):
    P, _, hr, C = g4.shape
    br = _ew_rows(hr, C)

    def body(c_ref, g_ref, o_ref, p_ref):
        p_ref[...] = (g_ref[...] + o_ref[...]).astype(BF16)

    return _pcall(
        body, name=name,
        grid_spec=pltpu.PrefetchScalarGridSpec(
            num_scalar_prefetch=1, grid=(P, hr // br),
            in_specs=[pl.BlockSpec((None, None, br, C), lambda p, i, c_ref: (p, c_ref[0], i, 0)),
                      pl.BlockSpec((None, br, C), lambda p, i, c_ref: (p, i, 0))],
            out_specs=pl.BlockSpec((None, br, C), lambda p, i, c_ref: (p, i, 0))),
        out_shape=jax.ShapeDtypeStruct((P, hr, C), BF16),
        compiler_params=_params(("parallel", "parallel")),
    )(c_arr, g4, other)


def _pair_part(p_ref, w, chip):
    sr, sc = _shard_shape(w)
    return p_ref.at[0, :, pl.ds(chip * sc, sc)] if W_SHAPES[w][2] else p_ref.at[chip]


def _chip_exchange(ps):
    def body(*refs):
        p_refs, s_refs = refs[:N_W], refs[N_W:2 * N_W]
        send, recv = refs[2 * N_W:]
        x, y, c = _place()
        chip_t = 2 * x + y
        for chip in range(N_CHIPS):
            @pl.when(chip_t == chip)
            def _(chip=chip):
                cps = []
                for w in range(N_W):
                    for k, (rx, ry) in enumerate(RELATIONS):
                        cps.append(pltpu.make_async_remote_copy(
                            src_ref=_pair_part(p_refs[w], w, chip ^ (2 * rx + ry)), dst_ref=s_refs[w].at[k],
                            send_sem=send.at[w * 3 + k], recv_sem=recv.at[w * 3 + k],
                            device_id=(x ^ rx, y ^ ry, c), device_id_type=MESH))
                        cps[-1].start()
                for cp in cps:
                    cp.wait()

    return _pcall(
        body, name="chip_exchange",
        out_shape=[jax.ShapeDtypeStruct((3,) + _half_shape(w), BF16) for w in range(N_W)],
        in_specs=[ANY] * N_W, out_specs=[ANY] * N_W,
        scratch_shapes=[pltpu.SemaphoreType.DMA((3 * N_W,)), pltpu.SemaphoreType.DMA((3 * N_W,))],
        compiler_params=_params(),
    )(*ps)


def _sum_slots(pair, slots, w, chip_arr, name):
    _, hr, C = slots.shape
    br = _ew_rows(hr, C)
    own_map = (lambda i, chip: (0, i, chip[0])) if W_SHAPES[w][2] else (lambda i, chip: (chip[0], i, 0))

    def body(chip_ref, p_ref, s_ref, o_ref):
        acc = p_ref[...].astype(F32)
        for k in range(3):
            acc = acc + s_ref[k].astype(F32)
        o_ref[...] = acc

    return _pcall(
        body, name=name,
        grid_spec=pltpu.PrefetchScalarGridSpec(
            num_scalar_prefetch=1, grid=(hr // br,),
            in_specs=[pl.BlockSpec((None, br, C), own_map), pl.BlockSpec((3, br, C), lambda i, chip: (0, i, 0))],
            out_specs=pl.BlockSpec((br, C), lambda i, chip: (i, 0))),
        out_shape=jax.ShapeDtypeStruct((hr, C), F32), compiler_params=_params(("parallel",)),
    )(chip_arr, pair, slots)


def _sibling_share(halves):
    def body(*refs):
        h_refs, o_refs = refs[:N_W], refs[N_W:2 * N_W]
        send, recv = refs[2 * N_W:]
        x, y, c = _place()
        cps = [pltpu.make_async_remote_copy(
            src_ref=h_refs[w], dst_ref=o_refs[w], send_sem=send.at[w], recv_sem=recv.at[w],
            device_id=(x, y, 1 - c), device_id_type=MESH) for w in range(N_W)]
        for cp in cps:
            cp.start()
        for cp in cps:
            cp.wait()

    return _pcall(
        body, name="sibling_share",
        out_shape=[jax.ShapeDtypeStruct(_half_shape(w), F32) for w in range(N_W)],
        in_specs=[ANY] * N_W, out_specs=[ANY] * N_W,
        scratch_shapes=[pltpu.SemaphoreType.DMA((N_W,)), pltpu.SemaphoreType.DMA((N_W,))],
        compiler_params=_params(),
    )(*halves)


def _pad_lanes(v, width=D_MODEL):
    return jnp.pad(v, ((0, 0), (0, width - v.shape[1])))


def _pack_small(b_ada, norm1, norm2, lb, o_gain, q_gain, k_gain, sinks):
    rows = [b_ada.reshape(N_MOD, D_MODEL), norm1, norm2, jnp.concatenate([lb[0:1], o_gain], axis=1),
            _pad_lanes(jnp.concatenate([q_gain, k_gain, sinks], axis=1)), _pad_lanes(lb[1:2]),
            jnp.zeros((SMALL_ROWS - 11, D_MODEL), F32)]
    return jnp.concatenate(rows, axis=0)


def _unpack_small(p):
    return (p[0:6].reshape(1, N_MOD * D_MODEL), p[6:7], p[7:8],
            jnp.concatenate([p[8:9, 0:A_WIDTH], p[10:11, 0:A_WIDTH]], axis=0), p[8:9, A_WIDTH:],
            p[9:10, 0:64], p[9:10, 64:128], p[9:10, 128:144])


def kernel(x, c, w_ada, b_ada, norm1_gain, w_in, lb_logits, hgrn_o_gain, q_norm_gain, k_norm_gain, sinks, w_branch_a, w_branch_b, w_out, norm2_gain, w_mlp_in, w_mlp_out, loss_target, m_w_ada, m_b_ada, m_norm1_gain, m_w_in, m_lb_logits, m_hgrn_o_gain, m_q_norm_gain, m_k_norm_gain, m_sinks, m_w_branch_a, m_w_branch_b, m_w_out, m_norm2_gain, m_w_mlp_in, m_w_mlp_out, v_w_ada, v_b_ada, v_norm1_gain, v_w_in, v_lb_logits, v_hgrn_o_gain, v_q_norm_gain, v_k_norm_gain, v_sinks, v_w_branch_a, v_w_branch_b, v_w_out, v_norm2_gain, v_w_mlp_in, v_w_mlp_out):
    xi, yi, ci = _place()
    chip = 2 * xi + yi
    me = 4 * xi + 2 * yi + ci
    ada_cols = w_ada.shape[2]

    c_all = _allgather_small(jnp.broadcast_to(c, (8, D_MODEL)), "gather_c").reshape(N_DEV, 8, D_MODEL)[:, 0]
    b_cols = lax.dynamic_slice(b_ada, (0, chip * ada_cols), (1, ada_cols))
    mod_part = _ada_fwd(c_all, w_ada[0], b_cols)
    mod_all = _allgather_small(mod_part, "gather_mod").reshape(N_CHIPS, 2, N_DEV, ada_cols)[:, 0]
    mod_mine = lax.dynamic_index_in_dim(mod_all, me, axis=1, keepdims=False).reshape(N_MOD, D_MODEL)
    mod8 = jnp.concatenate([mod_mine, jnp.zeros((2, D_MODEL), F32)], axis=0)

    shards = (w_in[0], w_branch_a[0], w_branch_b[0], w_out[0], w_mlp_in[0], w_mlp_out[0])
    chip_arr = chip.astype(jnp.int32).reshape(1)
    c_arr = ci.astype(jnp.int32).reshape(1)
    fulls = _gather_weights([_cast_into_full(s, w, chip_arr, f"cast_w{w}") for w, s in enumerate(shards)])

    grad_x, grads, st = _local_step(x[0], loss_target[0], mod8, norm1_gain, norm2_gain, lb_logits, hgrn_o_gain,
                                    q_norm_gain, k_norm_gain, sinks, *fulls)
    loss = lax.psum(0.5 * jnp.sum(st["loss"][0]) / D_MODEL, ("x", "y", "c"))

    g4s = [_grad_view(g, w) for w, g in enumerate(grads)]
    others = _sibling_exchange(g4s)
    pair = [_pair_sum(g4s[w], others[w], c_arr, f"pair_sum{w}") for w in range(N_W)]
    slots = _chip_exchange(pair)
    halves = [_sum_slots(pair[w], slots[w], w, chip_arr, f"sum_slots{w}") for w in range(N_W)]
    theirs = _sibling_share(halves)
    moments = ((m_w_in, v_w_in), (m_w_branch_a, v_w_branch_a), (m_w_branch_b, v_w_branch_b), (m_w_out, v_w_out),
               (m_w_mlp_in, v_w_mlp_in), (m_w_mlp_out, v_w_mlp_out))
    big = [_adamw_halves(shards[w], halves[w], theirs[w], moments[w][0][0], moments[w][1][0], c_arr, f"adamw{w}")
           for w in range(N_W)]

    swa = st["swa"]
    small = jnp.concatenate([
        st["n1"][1:2], st["n1"][0:1], st["n2"][3:4], st["n2"][1:2], st["n2"][0:1], st["loss"][1:2],
        st["n1"][2:3], st["n2"][2:3], jnp.concatenate([st["d_lb"][0:1], st["d_og"][0:1]], axis=1),
        _pad_lanes(jnp.concatenate([swa[0:1, 0:64], swa[1:2, 0:64], swa[2:3, 0:16]], axis=1)),
        jnp.zeros((SMALL_ROWS - 10, D_MODEL), F32)], axis=0)
    small_all = _allgather_small(small, "gather_small")
    g_small = _small_sum(small_all, lb_logits)
    small_w = (b_ada, norm1_gain, norm2_gain, lb_logits, hgrn_o_gain, q_norm_gain, k_norm_gain, sinks)
    small_m = (m_b_ada, m_norm1_gain, m_norm2_gain, m_lb_logits, m_hgrn_o_gain, m_q_norm_gain, m_k_norm_gain, m_sinks)
    small_v = (v_b_ada, v_norm1_gain, v_norm2_gain, v_lb_logits, v_hgrn_o_gain, v_q_norm_gain, v_k_norm_gain, v_sinks)
    sm = [_unpack_small(t) for t in
          (g_small,) + tuple(_adamw(_pack_small(*small_w), g_small, _pack_small(*small_m), _pack_small(*small_v),
                                    "adamw_small"))]
    g_b, g_n1, g_n2, g_lb, g_og, g_qg, g_kg, g_sk = ([t[i] for t in sm] for i in range(8))

    dmod_all = small_all.reshape(N_DEV, SMALL_ROWS, D_MODEL)[:, 0:N_MOD].reshape(N_DEV, N_MOD * D_MODEL)
    dmod_cols = lax.dynamic_slice(dmod_all, (0, chip * ada_cols), (N_DEV, ada_cols))
    ada = _ada_grad_adamw(c_all.T, dmod_cols, w_ada[0], m_w_ada[0], v_w_ada[0])

    def ordered(k):
        lead = lambda a: a[None]
        return (lead(ada[k]), g_b[k], g_n1[k], lead(big[0][k]), g_lb[k], g_og[k], g_qg[k], g_kg[k], g_sk[k],
                lead(big[1][k]), lead(big[2][k]), lead(big[3][k]), g_n2[k], lead(big[4][k]), lead(big[5][k]))

    return (loss, grad_x[None]) + ordered(0) + ordered(1) + ordered(2) + ordered(3)
```

```python
import functools

import jax
import jax.numpy as jnp
from jax import lax
from jax.experimental import pallas as pl
from jax.experimental.pallas import tpu as pltpu

F32 = jnp.float32
BF16 = jnp.bfloat16
HIGHEST = lax.Precision.HIGHEST
MESH = pl.DeviceIdType.MESH

D_MODEL = 2048
A_WIDTH = 1024
A_HEADS = 8
A_HEAD_DIM = 128
A_CHUNK = 64
B_WIDTH = 1024
B_HEAD_DIM = 64
B_GROUP = 4
B_KV_HEADS = 4
B_KV_WIDTH = 256
BLOCK = 128
MLP_HIDDEN = 8192
IN_WIDTH = 9728
N_MOD = 6
EPS = 1e-6
N_CHIPS = 4
N_DEV = 8

OFF_QA, OFF_FA, OFF_IA, OFF_GA = 0, 1024, 2048, 3072
OFF_QB, OFF_KB, OFF_VB = 4096, 5120, 5376
OFF_GATE_A, OFF_GATE_B = 5632, 7680

ADAM_LR = 0.001
ADAM_B1 = 0.9
ADAM_B2 = 0.999
ADAM_EPS = 1e-08
ADAM_WD = 0.01
ADAM_STEP = 10

VMEM_LIMIT_V7X = 48 * 1024 * 1024
NEG_BIG = -1e30


def _params(sem=None, vmem=VMEM_LIMIT_V7X):
    return pltpu.CompilerParams(dimension_semantics=sem, vmem_limit_bytes=vmem)


class _Plan:
    def __init__(self, ins, outs, sems, stages, aliases=None):
        self.ins, self.outs, self.sems, self.stages, self.aliases = ins, outs, sems, stages, aliases or {}


def _join(a, b):
    assert len(a.stages) == 2 and len(b.stages) == 2
    ni, no, ns = len(a.ins), len(a.outs), len(a.sems)

    def stage(k):
        def run(pi, po, ps):
            a.stages[k](pi[:ni], po[:no], ps[:ns])
            b.stages[k](pi[ni:], po[no:], ps[ns:])
        return run

    aliases = dict(a.aliases)
    aliases.update({ni + i: no + o for i, o in b.aliases.items()})
    return _Plan(a.ins + b.ins, a.outs + b.outs, a.sems + b.sems, [stage(0), stage(1)], aliases)


def _pcall(body, plan=None, **kw):
    if plan is None:
        return pl.pallas_call(body, **kw)
    grid = kw["grid"]
    single = not isinstance(kw["out_specs"], (list, tuple))
    in_specs = list(kw["in_specs"])
    out_specs = [kw["out_specs"]] if single else list(kw["out_specs"])
    out_shape = [kw["out_shape"]] if single else list(kw["out_shape"])
    scratch = list(kw.get("scratch_shapes", ()))
    n_in, n_out, n_scr = len(in_specs), len(out_specs), len(scratch)
    n_pi, n_po = len(plan.ins), len(plan.outs)
    total = 1
    for g in grid:
        total *= g
    n_st = len(plan.stages)

    def wrapped(*refs):
        o0 = n_in + n_pi
        s0 = o0 + n_out + n_po
        pi, po, ps = refs[n_in:o0], refs[o0 + n_out:s0], refs[s0 + n_scr:]
        lin = 0
        for d, g in enumerate(grid):
            lin = lin * g + pl.program_id(d)
        for si in range(n_st - 1):
            @pl.when(lin == (si * (total - 1)) // (n_st - 1))
            def _(si=si):
                plan.stages[si](pi, po, ps)
        body(*refs[:n_in], *refs[o0:o0 + n_out], *refs[s0:s0 + n_scr])

        @pl.when(lin == total - 1)
        def _():
            plan.stages[-1](pi, po, ps)

    any_spec = pl.BlockSpec(memory_space=pl.ANY)
    call = pl.pallas_call(
        wrapped, name=kw["name"], grid=grid, in_specs=in_specs + [any_spec] * n_pi,
        out_specs=out_specs + [any_spec] * n_po, out_shape=out_shape + list(plan.outs),
        scratch_shapes=scratch + list(plan.sems),
        input_output_aliases={n_in + i: n_out + o for i, o in plan.aliases.items()},
        compiler_params=_params(("arbitrary",) * len(grid)))

    def run(*args):
        res = call(*args, *plan.ins)
        outs = list(res[:n_out])
        return (outs[0] if single else outs), list(res[n_out:])

    return run


def _run_plan(plan, name):
    return _pcall(lambda: None, plan=plan, name=name, grid=(1,), in_specs=[], out_specs=[], out_shape=[])()[1]


def _sig(x):
    return 1.0 / (1.0 + jnp.exp(-x))


def _nn(a, b):
    return lax.dot_general(a.astype(BF16), b.astype(BF16), (((1,), (0,)), ((), ())), preferred_element_type=F32)


def _nt(a, b):
    return lax.dot_general(a.astype(BF16), b.astype(BF16), (((1,), (1,)), ((), ())), preferred_element_type=F32)


def _tn(a, b):
    return lax.dot_general(a.astype(BF16), b.astype(BF16), (((0,), (0,)), ((), ())), preferred_element_type=F32)


def _mm(a, b, *, name, ta=False, tb=False, bm=1024, bn=1024, bk=2048, out_dtypes=(F32,), epi=None, extras=(), plan=None):
    if ta:
        K, M = a.shape
    else:
        M, K = a.shape
    if tb:
        N, K2 = b.shape
    else:
        K2, N = b.shape
    bm, bn, bk = min(bm, M), min(bn, N), min(bk, K)
    assert K == K2 and M % bm == 0 and N % bn == 0 and K % bk == 0, (name, a.shape, b.shape)
    nk = K // bk
    a_spec = pl.BlockSpec((bk, bm), lambda i, j, k: (k, i)) if ta else pl.BlockSpec((bm, bk), lambda i, j, k: (i, k))
    b_spec = pl.BlockSpec((bn, bk), lambda i, j, k: (j, k)) if tb else pl.BlockSpec((bk, bn), lambda i, j, k: (k, j))
    t_spec = pl.BlockSpec((bm, bn), lambda i, j, k: (i, j))
    dims = (((0 if ta else 1,), (1 if tb else 0,)), ((), ()))
    n_e, n_o = len(extras), len(out_dtypes)

    def body(*refs):
        a_ref, b_ref = refs[0], refs[1]
        e_refs = refs[2:2 + n_e]
        o_refs = refs[2 + n_e:2 + n_e + n_o]

        def finish(acc):
            outs = (acc,) if epi is None else epi(acc, *[e[...] for e in e_refs])
            for o_ref, o in zip(o_refs, outs):
                o_ref[...] = o.astype(o_ref.dtype)

        part = lax.dot_general(a_ref[...].astype(BF16), b_ref[...].astype(BF16), dims, preferred_element_type=F32)
        if nk == 1:
            finish(part)
        else:
            acc_ref = refs[-1]
            k = pl.program_id(2)

            @pl.when(k == 0)
            def _():
                acc_ref[...] = part

            @pl.when(k > 0)
            def _():
                acc_ref[...] += part

            @pl.when(k == nk - 1)
            def _():
                finish(acc_ref[...])

    out = _pcall(
        body, plan=plan, name=name, grid=(M // bm, N // bn, nk),
        in_specs=[a_spec, b_spec] + [t_spec] * n_e,
        out_specs=[t_spec] * n_o,
        out_shape=[jax.ShapeDtypeStruct((M, N), dt) for dt in out_dtypes],
        scratch_shapes=[pltpu.VMEM((bm, bn), F32)] if nk > 1 else [],
        compiler_params=_params(("parallel", "parallel", "arbitrary")),
    )(a, b, *extras)
    if plan is not None:
        return (out[0][0] if n_o == 1 else out[0]), out[1]
    return out[0] if n_o == 1 else out


def _ada_fwd(c_all, w_ada, b_cols):
    n = w_ada.shape[1]
    bn = 512

    def body(c_ref, w_ref, b_ref, o_ref):
        cv = c_ref[...]
        sc = cv * _sig(cv)
        o_ref[...] = jnp.dot(sc, w_ref[...], precision=HIGHEST, preferred_element_type=F32) + b_ref[...]

    return _pcall(
        body, name="ada_fwd", grid=(n // bn,),
        in_specs=[pl.BlockSpec((N_DEV, D_MODEL), lambda j: (0, 0)), pl.BlockSpec((D_MODEL, bn), lambda j: (0, j)),
                  pl.BlockSpec((1, bn), lambda j: (0, j))],
        out_specs=pl.BlockSpec((N_DEV, bn), lambda j: (0, j)),
        out_shape=jax.ShapeDtypeStruct((N_DEV, n), F32),
        compiler_params=_params(("parallel",)),
    )(c_all, w_ada, b_cols)


ROWS_EW = 256


def _rms_fwd_math(x, gain, scale, shift):
    rstd = lax.rsqrt(jnp.mean(x * x, axis=-1, keepdims=True) + EPS)
    xhat = x * rstd
    n = xhat * gain
    return n * (1.0 + scale) + shift, xhat, n, rstd


def _rms_bwd_math(dh, xhat, n, rstd, gain, scale):
    dn = dh * (1.0 + scale)
    dxhat = dn * gain
    dx = rstd * (dxhat - xhat * jnp.mean(dxhat * xhat, axis=-1, keepdims=True))
    d_scale = jnp.sum(dh * n, axis=0, keepdims=True)
    d_shift = jnp.sum(dh, axis=0, keepdims=True)
    d_gain = jnp.sum(dn * xhat, axis=0, keepdims=True)
    return dx, d_scale, d_shift, d_gain


def _row_spec(w=D_MODEL, br=ROWS_EW):
    return pl.BlockSpec((br, w), lambda i: (i, 0))


def _vec_spec(r=8, w=D_MODEL):
    return pl.BlockSpec((r, w), lambda i: (0, 0))


def _norm1_fwd(x, gain, mod8):
    T = x.shape[0]

    def body(x_ref, g_ref, m_ref, h_ref):
        h, _, _, _ = _rms_fwd_math(x_ref[...], g_ref[...], m_ref[1:2, :], m_ref[0:1, :])
        h_ref[...] = h.astype(BF16)

    return _pcall(
        body, name="norm1_fwd", grid=(T // ROWS_EW,),
        in_specs=[_row_spec(), _vec_spec(1), _vec_spec()],
        out_specs=_row_spec(), out_shape=jax.ShapeDtypeStruct((T, D_MODEL), BF16),
        compiler_params=_params(("parallel",)),
    )(x, gain, mod8)


def _res_norm2_fwd(x, mo, gain, mod8):
    T = x.shape[0]

    def body(x_ref, mo_ref, g_ref, m_ref, x1_ref, h_ref):
        x1 = x_ref[...] + m_ref[2:3, :] * mo_ref[...]
        x1_ref[...] = x1
        h, _, _, _ = _rms_fwd_math(x1, g_ref[...], m_ref[4:5, :], m_ref[3:4, :])
        h_ref[...] = h.astype(BF16)

    return _pcall(
        body, name="res_norm2_fwd", grid=(T // ROWS_EW,),
        in_specs=[_row_spec(), _row_spec(), _vec_spec(1), _vec_spec()],
        out_specs=[_row_spec(), _row_spec()],
        out_shape=[jax.ShapeDtypeStruct((T, D_MODEL), F32), jax.ShapeDtypeStruct((T, D_MODEL), BF16)],
        compiler_params=_params(("parallel",)),
    )(x, mo, gain, mod8)


def _loss_bwd(x1, mlp, target, mod8):
    T = x1.shape[0]

    def body(x1_ref, mlp_ref, t_ref, m_ref, dy_ref, dmlp_ref, st_ref):
        i = pl.program_id(0)
        gate = m_ref[5:6, :]
        mlp_v = mlp_ref[...]
        err = x1_ref[...] + gate * mlp_v - t_ref[...]
        dy = err * (1.0 / D_MODEL)
        dy_ref[...] = dy
        dmlp_ref[...] = (dy * gate).astype(BF16)

        @pl.when(i == 0)
        def _():
            st_ref[...] = jnp.zeros_like(st_ref)

        st_ref[0:1, :] += jnp.sum(err * err, axis=0, keepdims=True)
        st_ref[1:2, :] += jnp.sum(dy * mlp_v, axis=0, keepdims=True)

    return _pcall(
        body, name="loss_bwd", grid=(T // ROWS_EW,),
        in_specs=[_row_spec(), _row_spec(), _row_spec(), _vec_spec()],
        out_specs=[_row_spec(), _row_spec(), _vec_spec()],
        out_shape=[jax.ShapeDtypeStruct((T, D_MODEL), F32), jax.ShapeDtypeStruct((T, D_MODEL), BF16),
                   jax.ShapeDtypeStruct((8, D_MODEL), F32)],
        compiler_params=_params(("arbitrary",)),
    )(x1, mlp, target, mod8)


def _norm2_bwd(dh2, x1, dy, mo, gain, mod8):
    T = x1.shape[0]

    def body(dh_ref, x1_ref, dy_ref, mo_ref, g_ref, m_ref, dx1_ref, dmo_ref, st_ref):
        i = pl.program_id(0)
        gain_v, scale = g_ref[...], m_ref[4:5, :]
        _, xhat, n, rstd = _rms_fwd_math(x1_ref[...], gain_v, scale, m_ref[3:4, :])
        dx, d_scale, d_shift, d_gain = _rms_bwd_math(dh_ref[...], xhat, n, rstd, gain_v, scale)
        dx1 = dy_ref[...] + dx
        dx1_ref[...] = dx1
        dmo_ref[...] = (dx1 * m_ref[2:3, :]).astype(BF16)

        @pl.when(i == 0)
        def _():
            st_ref[...] = jnp.zeros_like(st_ref)

        st_ref[0:1, :] += d_scale
        st_ref[1:2, :] += d_shift
        st_ref[2:3, :] += d_gain
        st_ref[3:4, :] += jnp.sum(dx1 * mo_ref[...], axis=0, keepdims=True)

    return _pcall(
        body, name="norm2_bwd", grid=(T // ROWS_EW,),
        in_specs=[_row_spec(), _row_spec(), _row_spec(), _row_spec(), _vec_spec(1), _vec_spec()],
        out_specs=[_row_spec(), _row_spec(), _vec_spec()],
        out_shape=[jax.ShapeDtypeStruct((T, D_MODEL), F32), jax.ShapeDtypeStruct((T, D_MODEL), BF16),
                   jax.ShapeDtypeStruct((8, D_MODEL), F32)],
        compiler_params=_params(("arbitrary",)),
    )(dh2, x1, dy, mo, gain, mod8)


def _norm1_bwd(dh, x, dx1, gain, mod8):
    T = x.shape[0]

    def body(dh_ref, x_ref, dx1_ref, g_ref, m_ref, dx_ref, st_ref):
        i = pl.program_id(0)
        gain_v, scale = g_ref[...], m_ref[1:2, :]
        _, xhat, n, rstd = _rms_fwd_math(x_ref[...], gain_v, scale, m_ref[0:1, :])
        dx, d_scale, d_shift, d_gain = _rms_bwd_math(dh_ref[...], xhat, n, rstd, gain_v, scale)
        dx_ref[...] = dx1_ref[...] + dx

        @pl.when(i == 0)
        def _():
            st_ref[...] = jnp.zeros_like(st_ref)

        st_ref[0:1, :] += d_scale
        st_ref[1:2, :] += d_shift
        st_ref[2:3, :] += d_gain

    return _pcall(
        body, name="norm1_bwd", grid=(T // ROWS_EW,),
        in_specs=[_row_spec(), _row_spec(), _row_spec(), _vec_spec(1), _vec_spec()],
        out_specs=[_row_spec(), _vec_spec()],
        out_shape=[jax.ShapeDtypeStruct((T, D_MODEL), F32), jax.ShapeDtypeStruct((8, D_MODEL), F32)],
        compiler_params=_params(("arbitrary",)),
    )(dh, x, dx1, gain, mod8)


MERGE_BC = 512


def _merge_specs():
    ga = pl.BlockSpec((ROWS_EW, MERGE_BC), lambda i, j: (i, OFF_GATE_A // MERGE_BC + j))
    gb = pl.BlockSpec((ROWS_EW, MERGE_BC), lambda i, j: (i, OFF_GATE_B // MERGE_BC + j))
    t = pl.BlockSpec((ROWS_EW, MERGE_BC), lambda i, j: (i, j))
    return ga, gb, t


def _merge_fwd(proj, ya, yb):
    T = proj.shape[0]
    ga, gb, t = _merge_specs()

    def body(ga_ref, gb_ref, ya_ref, yb_ref, o_ref):
        o_ref[...] = (_sig(ga_ref[...]) * ya_ref[...] + _sig(gb_ref[...]) * yb_ref[...]).astype(BF16)

    return _pcall(
        body, name="merge_fwd", grid=(T // ROWS_EW, D_MODEL // MERGE_BC),
        in_specs=[ga, gb, t, t], out_specs=t, out_shape=jax.ShapeDtypeStruct((T, D_MODEL), BF16),
        compiler_params=_params(("parallel", "parallel")),
    )(proj, proj, ya, yb)


def _merge_bwd(proj, ya, yb, dmerged):
    T = proj.shape[0]
    ga, gb, t = _merge_specs()

    def body(ga_ref, gb_ref, ya_ref, yb_ref, dm_ref, dya_ref, dyb_ref, dga_ref, dgb_ref):
        dm = dm_ref[...]
        sa, sb = _sig(ga_ref[...]), _sig(gb_ref[...])
        dya_ref[...] = (dm * sa).astype(BF16)
        dyb_ref[...] = (dm * sb).astype(BF16)
        dga_ref[...] = (dm * ya_ref[...] * sa * (1.0 - sa)).astype(BF16)
        dgb_ref[...] = (dm * yb_ref[...] * sb * (1.0 - sb)).astype(BF16)

    sh = jax.ShapeDtypeStruct((T, D_MODEL), BF16)
    return _pcall(
        body, name="merge_bwd", grid=(T // ROWS_EW, D_MODEL // MERGE_BC),
        in_specs=[ga, gb, t, t, t], out_specs=[t, t, t, t], out_shape=[sh, sh, sh, sh],
        compiler_params=_params(("parallel", "parallel")),
    )(proj, proj, ya, yb, dmerged)


def _hgrn_rows(T):
    return 512 if T >= 1024 else 128


def _lower_bound(lbl):
    e = jnp.exp(lbl - jnp.max(lbl, axis=0, keepdims=True))
    return e[0:1, :] / (e[0:1, :] + e[1:2, :])


def _hgrn_chunk_fwd(q, fl, v, lb, st):
    C = A_CHUNK
    sg = _sig(fl)
    f = lb + (1.0 - lb) * sg
    lf = jnp.log(f)
    k = 1.0 - f
    sq = _sig(q)
    qf = q * sq
    row = lax.broadcasted_iota(jnp.int32, (C, C), 0)
    col = lax.broadcasted_iota(jnp.int32, (C, C), 1)
    causal = row >= col
    b = jnp.dot(causal.astype(F32), lf, precision=HIGHEST, preferred_element_type=F32)
    bm = b[C // 2 - 1:C // 2, :]
    bl = b[C - 1:C, :]
    e_q, e_k = jnp.exp(b - bm), jnp.exp(bm - b)
    e_b, e_l = jnp.exp(b), jnp.exp(bl - b)
    qd, kd = qf * e_q, k * e_k
    qe, ke = qf * e_b, k * e_l
    att = jnp.where(causal, _nt(qd, kd), 0.0)
    o = _nn(att, v) + _nt(qe, st)
    dec = jnp.exp(bl)
    st_next = st * dec + _tn(v, ke)
    return dict(sg=sg, f=f, k=k, sq=sq, qf=qf, causal=causal, e_q=e_q, e_k=e_k, e_b=e_b, e_l=e_l, qd=qd, kd=kd,
                qe=qe, ke=ke, att=att, o=o, dec=dec, st_next=st_next)


def _hgrn_fwd(proj, lb_logits, o_gain, plan=None):
    T = proj.shape[0]
    BR = _hgrn_rows(T)
    cps = BR // A_CHUNK
    K = A_HEAD_DIM

    def col(off):
        return pl.BlockSpec((BR, K), lambda h, cb: (cb, off // K + h))

    def body(q_ref, f_ref, i_ref, g_ref, lbl_ref, og_ref, o_ref, s_ref, st):
        @pl.when(pl.program_id(1) == 0)
        def _():
            st[...] = jnp.zeros_like(st)

        lb = _lower_bound(lbl_ref[...])
        gain = og_ref[...]
        for ci in range(cps):
            r = slice(ci * A_CHUNK, (ci + 1) * A_CHUNK)
            s_prev = st[...]
            s_ref[0, ci] = s_prev
            c = _hgrn_chunk_fwd(q_ref[r, :], f_ref[r, :], i_ref[r, :], lb, s_prev)
            st[...] = c["st_next"]
            o = c["o"]
            on = o * lax.rsqrt(jnp.mean(o * o, axis=-1, keepdims=True) + EPS)
            g = g_ref[r, :]
            o_ref[r, :] = (on * gain * (g * _sig(g))).astype(BF16)

    return _pcall(
        body, plan=plan, name="hgrn_fwd", grid=(A_HEADS, T // BR),
        in_specs=[col(OFF_QA), col(OFF_FA), col(OFF_IA), col(OFF_GA),
                  pl.BlockSpec((2, K), lambda h, cb: (0, h)), pl.BlockSpec((1, K), lambda h, cb: (0, h))],
        out_specs=[pl.BlockSpec((BR, K), lambda h, cb: (cb, h)),
                   pl.BlockSpec((1, cps, K, K), lambda h, cb: (h, cb, 0, 0))],
        out_shape=[jax.ShapeDtypeStruct((T, A_WIDTH), BF16),
                   jax.ShapeDtypeStruct((A_HEADS, T // A_CHUNK, K, K), F32)],
        scratch_shapes=[pltpu.VMEM((K, K), F32)],
        compiler_params=_params(("parallel", "arbitrary")),
    )(proj, proj, proj, proj, lb_logits, o_gain)


def _hgrn_bwd(proj, lb_logits, o_gain, states, do, plan=None):
    T = proj.shape[0]
    BR = _hgrn_rows(T)
    cps = BR // A_CHUNK
    ncb = T // BR
    K, C = A_HEAD_DIM, A_CHUNK

    def col(off):
        return pl.BlockSpec((BR, K), lambda h, cb: (ncb - 1 - cb, off // K + h))

    def body(q_ref, f_ref, i_ref, g_ref, lbl_ref, og_ref, s_ref, do_ref,
             dq_ref, df_ref, di_ref, dg_ref, dlb_ref, dog_ref, dst):
        @pl.when(pl.program_id(1) == 0)
        def _():
            dst[...] = jnp.zeros_like(dst)
            dlb_ref[...] = jnp.zeros_like(dlb_ref)
            dog_ref[...] = jnp.zeros_like(dog_ref)

        lb = _lower_bound(lbl_ref[...])
        gain = og_ref[...]
        row = lax.broadcasted_iota(jnp.int32, (C, K), 0)
        for ci in reversed(range(cps)):
            r = slice(ci * C, (ci + 1) * C)
            st = s_ref[0, ci]
            v = i_ref[r, :]
            q = q_ref[r, :]
            c = _hgrn_chunk_fwd(q, f_ref[r, :], v, lb, st)
            dst_next = dst[...]
            o = c["o"]
            rn = lax.rsqrt(jnp.mean(o * o, axis=-1, keepdims=True) + EPS)
            on = o * rn
            g = g_ref[r, :]
            sgg = _sig(g)
            dy = do_ref[r, :]
            d_ong = dy * (g * sgg)
            dg_ref[r, :] = (dy * (on * gain) * (sgg * (1.0 + g * (1.0 - sgg)))).astype(BF16)
            dog_ref[0:1, :] += jnp.sum(d_ong * on, axis=0, keepdims=True)
            d_on = d_ong * gain
            d_o = rn * (d_on - on * jnp.mean(d_on * on, axis=-1, keepdims=True))
            datt = jnp.where(c["causal"], _nt(d_o, v), 0.0)
            dv = _tn(c["att"], d_o) + _nt(c["ke"], dst_next)
            dqd = _nn(datt, c["kd"])
            dkd = _tn(datt, c["qd"])
            dqe = _nn(d_o, st)
            dke = _nn(v, dst_next)
            dst[...] = dst_next * c["dec"] + _tn(d_o, c["qe"])
            d_dec = jnp.sum(dst_next * st, axis=0, keepdims=True)
            t_q, t_k = dqd * c["qd"], dkd * c["kd"]
            t_e, t_l = dqe * c["qe"], dke * c["ke"]
            db = t_q - t_k + t_e - t_l
            dbm = jnp.sum(t_k - t_q, axis=0, keepdims=True)
            dbl = jnp.sum(t_l, axis=0, keepdims=True) + d_dec * c["dec"]
            db = db + jnp.where(row == C // 2 - 1, dbm, 0.0) + jnp.where(row == C - 1, dbl, 0.0)
            upper = jnp.logical_not(c["causal"]) | (lax.broadcasted_iota(jnp.int32, (C, C), 0)
                                                    == lax.broadcasted_iota(jnp.int32, (C, C), 1))
            dlf = jnp.dot(upper.astype(F32), db, precision=HIGHEST, preferred_element_type=F32)
            dqf = dqd * c["e_q"] + dqe * c["e_b"]
            sq = c["sq"]
            dq_ref[r, :] = (dqf * (sq * (1.0 + q * (1.0 - sq)))).astype(BF16)
            dk = dkd * c["e_k"] + dke * c["e_l"]
            df = dlf / c["f"] - dk
            sg = c["sg"]
            df_ref[r, :] = (df * (1.0 - lb) * sg * (1.0 - sg)).astype(BF16)
            dlb_ref[0:1, :] += jnp.sum(df * (1.0 - sg), axis=0, keepdims=True)
            di_ref[r, :] = dv.astype(BF16)

    ocol = pl.BlockSpec((BR, K), lambda h, cb: (ncb - 1 - cb, h))
    vec =pl.BlockSpec((8, K), lambda h, cb: (0, h))
    return _pcall(
        body, plan=plan, name="hgrn_bwd", grid=(A_HEADS, ncb),
        in_specs=[col(OFF_QA), col(OFF_FA), col(OFF_IA), col(OFF_GA),
                  pl.BlockSpec((2, K), lambda h, cb: (0, h)), pl.BlockSpec((1, K), lambda h, cb: (0, h)),
                  pl.BlockSpec((1, cps, K, K), lambda h, cb: (h, ncb - 1 - cb, 0, 0)),
                  pl.BlockSpec((BR, K), lambda h, cb: (ncb - 1 - cb, h))],
        out_specs=[ocol, ocol, ocol, ocol, vec, vec],
        out_shape=[jax.ShapeDtypeStruct((T, A_WIDTH), BF16)] * 4 + [jax.ShapeDtypeStruct((8, A_WIDTH), F32)] * 2,
        scratch_shapes=[pltpu.VMEM((K, K), F32)],
        compiler_params=_params(("parallel", "arbitrary")),
    )(proj, proj, proj, proj, lb_logits, o_gain, states, do)


def _head_norm(x):
    r = lax.rsqrt(jnp.mean(x * x, axis=-1, keepdims=True) + EPS)
    return x * r, r


def _head_norm_bwd(dy, xn, r, gain):
    dxn = dy * gain
    return r * (dxn - xn * jnp.mean(dxn * xn, axis=-1, keepdims=True)), jnp.sum(dy * xn, axis=0, keepdims=True)


def _swa_mask(has_prev):
    rows = B_GROUP * BLOCK
    r = lax.broadcasted_iota(jnp.int32, (rows, 2 * BLOCK), 0) % BLOCK
    c = lax.broadcasted_iota(jnp.int32, (rows, 2 * BLOCK), 1)
    rel = r + BLOCK - c
    return (rel >= 0) & (rel < BLOCK) & ((c >= BLOCK) | has_prev)


def _swa_head_fwd(j, q_ref, kp_ref, kc_ref, vp_ref, vc_ref, qg, kg, sk_ref, mask):
    hs = slice(j * B_HEAD_DIM, (j + 1) * B_HEAD_DIM)
    kcat = jnp.concatenate([kp_ref[:, hs], kc_ref[:, hs]], axis=0)
    vcat = jnp.concatenate([vp_ref[:, hs], vc_ref[:, hs]], axis=0)
    qs = jnp.concatenate([q_ref[:, pl.ds((j * B_GROUP + g) * B_HEAD_DIM, B_HEAD_DIM)] for g in range(B_GROUP)], axis=0)
    kn, kr = _head_norm(kcat)
    qn, qr = _head_norm(qs)
    kh, qh = kn * kg, qn * qg
    s = jnp.where(mask, _nt(qh, kh) * (B_HEAD_DIM ** -0.5), NEG_BIG)
    sink = jnp.concatenate(
        [jnp.broadcast_to(sk_ref[0:1, pl.ds(j * B_GROUP + g, 1)], (BLOCK, 1)) for g in range(B_GROUP)], axis=0)
    m = jnp.maximum(jnp.max(s, axis=-1, keepdims=True), sink)
    p = jnp.exp(s - m)
    e_sink = jnp.exp(sink - m)
    inv = 1.0 / (jnp.sum(p, axis=-1, keepdims=True) + e_sink)
    prob = p * inv
    return dict(vcat=vcat, kn=kn, kr=kr, qn=qn, qr=qr, kh=kh, qh=qh, prob=prob, p_sink=e_sink * inv)


def _swa_in_specs(nb, last):
    def qi(n):
        return jnp.minimum(n, last)

    q = pl.BlockSpec((BLOCK, B_WIDTH), lambda n: (qi(n), OFF_QB // B_WIDTH))
    kc = pl.BlockSpec((BLOCK, B_KV_WIDTH), lambda n: (qi(n), OFF_KB // B_KV_WIDTH))
    kp = pl.BlockSpec((BLOCK, B_KV_WIDTH), lambda n: (jnp.maximum(qi(n) - 1, 0), OFF_KB // B_KV_WIDTH))
    vc = pl.BlockSpec((BLOCK, B_KV_WIDTH), lambda n: (qi(n), OFF_VB // B_KV_WIDTH))
    vp = pl.BlockSpec((BLOCK, B_KV_WIDTH), lambda n: (jnp.maximum(qi(n) - 1, 0), OFF_VB // B_KV_WIDTH))
    small = [pl.BlockSpec((1, B_HEAD_DIM), lambda n: (0, 0)), pl.BlockSpec((1, B_HEAD_DIM), lambda n: (0, 0)),
             pl.BlockSpec((1, B_GROUP * B_KV_HEADS), lambda n: (0, 0))]
    return [q, kp, kc, vp, vc] + small


def _swa_fwd(proj, q_gain, k_gain, sinks, plan=None):
    T = proj.shape[0]
    nb = T // BLOCK

    def body(q_ref, kp_ref, kc_ref, vp_ref, vc_ref, qg_ref, kg_ref, sk_ref, o_ref):
        mask = _swa_mask(pl.program_id(0) > 0)
        for j in range(B_KV_HEADS):
            c = _swa_head_fwd(j, q_ref, kp_ref, kc_ref, vp_ref, vc_ref, qg_ref[...], kg_ref[...], sk_ref, mask)
            o = _nn(c["prob"], c["vcat"])
            for g in range(B_GROUP):
                o_ref[:, pl.ds((j * B_GROUP + g) * B_HEAD_DIM, B_HEAD_DIM)] = o[g * BLOCK:(g + 1) * BLOCK].astype(BF16)

    return _pcall(
        body, plan=plan, name="swa_fwd", grid=(nb,),
        in_specs=_swa_in_specs(nb, nb - 1),
        out_specs=pl.BlockSpec((BLOCK, B_WIDTH), lambda n: (n, 0)),
        out_shape=jax.ShapeDtypeStruct((T, B_WIDTH), BF16),
        compiler_params=_params(("parallel",)),
    )(proj, proj, proj, proj, proj, q_gain, k_gain, sinks)


def _swa_bwd(proj, q_gain, k_gain, sinks, do, plan=None):
    T = proj.shape[0]
    nb = T // BLOCK
    scale = B_HEAD_DIM ** -0.5

    def body(q_ref, kp_ref, kc_ref, vp_ref, vc_ref, qg_ref, kg_ref, sk_ref, do_ref,
             dq_ref, dkv_ref, sm_ref, ck, cv):
        n = pl.program_id(0)

        @pl.when(n == 0)
        def _():
            ck[...] = jnp.zeros_like(ck)
            cv[...] = jnp.zeros_like(cv)
            sm_ref[...] = jnp.zeros_like(sm_ref)

        @pl.when(n < nb)
        def _():
            mask = _swa_mask(n > 0)
            qg, kg = qg_ref[...], kg_ref[...]
            lane = lax.broadcasted_iota(jnp.int32, (1, BLOCK), 1)
            for j in range(B_KV_HEADS):
                hs = slice(j * B_HEAD_DIM, (j + 1) * B_HEAD_DIM)
                vs = slice(B_KV_WIDTH + j * B_HEAD_DIM, B_KV_WIDTH + (j + 1) * B_HEAD_DIM)
                c = _swa_head_fwd(j, q_ref, kp_ref, kc_ref, vp_ref, vc_ref, qg, kg, sk_ref, mask)
                d_out = jnp.concatenate(
                    [do_ref[:, pl.ds((j * B_GROUP + g) * B_HEAD_DIM, B_HEAD_DIM)] for g in range(B_GROUP)], axis=0)
                prob = c["prob"]
                out = _nn(prob, c["vcat"])
                delta = jnp.sum(d_out * out, axis=-1, keepdims=True)
                ds = prob * (_nt(d_out, c["vcat"]) - delta)
                d_sink = -c["p_sink"] * delta
                dqh = _nn(ds, c["kh"]) * scale
                dkh = _tn(ds, c["qh"]) * scale
                dv = _tn(prob, d_out)
                dq, dqg = _head_norm_bwd(dqh, c["qn"], c["qr"], qg)
                dk, dkg = _head_norm_bwd(dkh, c["kn"], c["kr"], kg)
                sm_ref[0:1, 0:B_HEAD_DIM] += dqg
                sm_ref[1:2, 0:B_HEAD_DIM] += dkg
                for g in range(B_GROUP):
                    dq_ref[:, pl.ds((j * B_GROUP + g) * B_HEAD_DIM, B_HEAD_DIM)] = dq[g * BLOCK:(g + 1) * BLOCK].astype(BF16)
                    tot = jnp.sum(d_sink[g * BLOCK:(g + 1) * BLOCK], axis=0, keepdims=True)
                    sm_ref[2:3, :] += jnp.where(lane == j * B_GROUP + g, tot, 0.0)
                dkv_ref[:, hs] = (ck[:, hs] + dk[0:BLOCK]).astype(BF16)
                dkv_ref[:, vs] = (cv[:, hs] + dv[0:BLOCK]).astype(BF16)
                ck[:, hs] = dk[BLOCK:2 * BLOCK]
                cv[:, hs] = dv[BLOCK:2 * BLOCK]

        @pl.when(n == nb)
        def _():
            dkv_ref[:, 0:B_KV_WIDTH] = ck[...].astype(BF16)
            dkv_ref[:, B_KV_WIDTH:2 * B_KV_WIDTH] = cv[...].astype(BF16)

    return _pcall(
        body, plan=plan, name="swa_bwd", grid=(nb + 1,),
        in_specs=_swa_in_specs(nb, nb - 1) + [pl.BlockSpec((BLOCK, B_WIDTH), lambda n: (jnp.minimum(n, nb - 1), 0))],
        out_specs=[pl.BlockSpec((BLOCK, B_WIDTH), lambda n: (jnp.minimum(n, nb - 1), 0)),
                   pl.BlockSpec((BLOCK, 2 * B_KV_WIDTH), lambda n: (jnp.maximum(n - 1, 0), 0)),
                   pl.BlockSpec((8, BLOCK), lambda n: (0, 0))],
        out_shape=[jax.ShapeDtypeStruct((T, B_WIDTH), BF16), jax.ShapeDtypeStruct((T, 2 * B_KV_WIDTH), BF16),
                   jax.ShapeDtypeStruct((8, BLOCK), F32)],
        scratch_shapes=[pltpu.VMEM((BLOCK, B_KV_WIDTH), F32), pltpu.VMEM((BLOCK, B_KV_WIDTH), F32)],
        compiler_params=_params(("arbitrary",)),
    )(proj, proj, proj, proj, proj, q_gain, k_gain, sinks, do)


W_IN, W_A, W_B, W_OUT, W_MI, W_MO = range(6)


def _local_step(x, target, mod8, norm1_gain, norm2_gain, lb_logits, o_gain, q_gain, k_gain, sinks, parts, c_arr, chip_arr):
    relu2 = lambda u: (u, jnp.square(jnp.maximum(u, 0.0)))
    pair, half = {}, {}

    def exchange(ws, grads):
        return _sibling_exchange_plan([_grad_view(g, w) for w, g in zip(ws, grads)])

    def pair_sums(ws, grads, others):
        for w, g, o in zip(ws, grads, others):
            pair[w] = _pair_sum(_grad_view(g, w), o, c_arr, f"pair_sum{w}")

    def sum_slots(ws, slots):
        for w, s in zip(ws, slots):
            half[w] = _sum_slots(pair[w], s, w, chip_arr, f"sum_slots{w}")

    (w_in,) = _run_plan(_gather_plan({W_IN: parts[W_IN]}), "gather_w_in")
    h = _norm1_fwd(x, norm1_gain, mod8)
    proj, (w_a, w_b, w_out) = _mm(h, w_in, name="mm_proj", bn=512,
                                  plan=_gather_plan({w: parts[w] for w in (W_A, W_B, W_OUT)}))
    (o_a, states), (w_mi,) = _hgrn_fwd(proj, lb_logits, o_gain, plan=_gather_plan({W_MI: parts[W_MI]}))
    o_b, (w_mo,) = _swa_fwd(proj, q_gain, k_gain, sinks, plan=_gather_plan({W_MO: parts[W_MO]}))
    ya = _mm(o_a, w_a, name="mm_branch_a")
    yb = _mm(o_b, w_b, name="mm_branch_b")
    merged = _merge_fwd(proj, ya, yb)
    mo = _mm(merged, w_out, name="mm_out")
    x1, h2 = _res_norm2_fwd(x, mo, norm2_gain, mod8)
    u, act = _mm(h2, w_mi, name="mm_mlp_in", out_dtypes=(F32, BF16), epi=relu2)
    mlp = _mm(act, w_mo, name="mm_mlp_out")
    dy, dmlp, st_loss = _loss_bwd(x1, mlp, target, mod8)
    g_mo = _mm(act, dmlp, name="mm_g_mlp_out", ta=True)
    du, others = _mm(dmlp, w_mo, name="mm_d_act", tb=True, out_dtypes=(BF16,), extras=(u,),
                     epi=lambda acc, uu: (acc * (2.0 * jnp.maximum(uu, 0.0)),), plan=exchange([W_MO], [g_mo]))
    pair_sums([W_MO], [g_mo], others)
    g_mi, slots_mo = _mm(h2, du, name="mm_g_mlp_in", ta=True, plan=_chip_exchange_plan({W_MO: pair[W_MO]}))
    dh2, others = _mm(du, w_mi, name="mm_d_h2", tb=True, plan=exchange([W_MI], [g_mi]))
    pair_sums([W_MI], [g_mi], others)
    sum_slots([W_MO], slots_mo)
    dx1, dmo, st_n2 = _norm2_bwd(dh2, x1, dy, mo, norm2_gain, mod8)
    dmerged = _mm(dmo, w_out, name="mm_d_merged", tb=True)
    g_out = _mm(merged, dmo, name="mm_g_out", ta=True)
    dya, dyb, dga, dgb = _merge_bwd(proj, ya, yb, dmerged)
    do_a = _mm(dya, w_a, name="mm_d_oa", tb=True)
    g_a = _mm(o_a, dya, name="mm_g_branch_a", ta=True)
    do_b = _mm(dyb, w_b, name="mm_d_ob", tb=True)
    g_b = _mm(o_b, dyb, name="mm_g_branch_b", ta=True)
    mid = [W_A, W_B, W_OUT]
    (dqa, dfa, dia, dgga, d_lb, d_og), res = _hgrn_bwd(
        proj, lb_logits, o_gain, states, do_a,
        plan=_join(_chip_exchange_plan({W_MI: pair[W_MI]}), exchange(mid, [g_a, g_b, g_out])))
    sum_slots([W_MI], res[:1])
    pair_sums(mid, [g_a, g_b, g_out], res[1:])
    (dqb, dkvb, st_swa), slots_mid = _swa_bwd(proj, q_gain, k_gain, sinks, do_b,
                                              plan=_chip_exchange_plan({w: pair[w] for w in mid}))
    sum_slots(mid, slots_mid)
    dproj = jnp.concatenate([dqa, dfa, dia, dgga, dqb, dkvb, dga, dgb], axis=1)
    g_in = _mm(h, dproj, name="mm_g_in", ta=True, bn=512)
    done = [W_A, W_B, W_OUT, W_MI, W_MO]
    dh, res = _mm(dproj, w_in, name="mm_d_h", tb=True, bk=2432,
                  plan=_join(exchange([W_IN], [g_in]), _sibling_share_plan([half[w] for w in done])))
    pair_sums([W_IN], [g_in], res[:1])
    theirs = dict(zip(done, res[1:]))
    grad_x, st_n1 = _norm1_bwd(dh, x, dx1, norm1_gain, mod8)
    sum_slots([W_IN], _run_plan(_chip_exchange_plan({W_IN: pair[W_IN]}), "chip_exchange_w_in"))
    (theirs[W_IN],) = _run_plan(_sibling_share_plan([half[W_IN]]), "sibling_share_w_in")
    stats = dict(loss=st_loss, n2=st_n2, n1=st_n1, d_lb=d_lb, d_og=d_og, swa=st_swa)
    return grad_x, [half[w] for w in range(N_W)], [theirs[w] for w in range(N_W)], stats


def _ew_rows(rows, cols):
    br = 8
    while br * 2 <= rows and br * 2 * cols * 4 <= (1 << 20) and rows % (br * 2) == 0:
        br *= 2
    return br


def _cast_into_full(shard, w, chip_arr, name):
    sr, sc = shard.shape
    R, C, by_col = W_SHAPES[w]
    br = _ew_rows(sr, sc)
    nb = sr // br
    out_map = (lambda i, chip: (i, chip[0])) if by_col else (lambda i, chip: (chip[0] * nb + i, 0))

    def body(chip_ref, w_ref, o_ref):
        o_ref[...] = w_ref[...].astype(BF16)

    return _pcall(
        body, name=name,
        grid_spec=pltpu.PrefetchScalarGridSpec(
            num_scalar_prefetch=1, grid=(nb,),
            in_specs=[pl.BlockSpec((br, sc), lambda i, chip: (i, 0))],
            out_specs=pl.BlockSpec((br, sc), out_map)),
        out_shape=jax.ShapeDtypeStruct((R, C), BF16), compiler_params=_params(("parallel",)))(chip_arr, shard)


def _adamw_math(w, g, m, v):
    m = ADAM_B1 * m + (1.0 - ADAM_B1) * g
    v = ADAM_B2 * v + (1.0 - ADAM_B2) * (g * g)
    m_hat = m / (1.0 - ADAM_B1 ** ADAM_STEP)
    v_hat = v / (1.0 - ADAM_B2 ** ADAM_STEP)
    delta = -ADAM_LR * (m_hat / (jnp.sqrt(v_hat) + ADAM_EPS) + ADAM_WD * w)
    return delta, m, v


def _adamw(w, g, m, v, name):
    R, C = w.shape
    br = _ew_rows(R, C)
    spec = pl.BlockSpec((br, C), lambda i: (i, 0))

    def body(w_ref, g_ref, m_ref, v_ref, d_ref, nm_ref, nv_ref):
        d_ref[...], nm_ref[...], nv_ref[...] = _adamw_math(w_ref[...], g_ref[...], m_ref[...], v_ref[...])

    sh = jax.ShapeDtypeStruct((R, C), F32)
    return _pcall(body, name=name, grid=(R // br,), in_specs=[spec] * 4, out_specs=[spec] * 3, out_shape=[sh] * 3,
                  compiler_params=_params(("parallel",)))(w, g, m, v)


def _adamw_halves(w, own, other, m, v, c_arr, name):
    R, C = w.shape
    hr = R // 2
    br = _ew_rows(hr, C)
    nb = hr // br
    full = pl.BlockSpec((br, C), lambda h, i, c_ref: (h * nb + i, 0))
    half = pl.BlockSpec((br, C), lambda h, i, c_ref: (i, 0))

    def body(c_ref, w_ref, own_ref, oth_ref, m_ref, v_ref, g_ref, d_ref, nm_ref, nv_ref):
        g = jnp.where(pl.program_id(0) == c_ref[0], own_ref[...], oth_ref[...])
        g_ref[...] = g
        d_ref[...], nm_ref[...], nv_ref[...] = _adamw_math(w_ref[...], g, m_ref[...], v_ref[...])

    sh = jax.ShapeDtypeStruct((R, C), F32)
    return _pcall(
        body, name=name,
        grid_spec=pltpu.PrefetchScalarGridSpec(
            num_scalar_prefetch=1, grid=(2, nb), in_specs=[full, half, half, full, full], out_specs=[full] * 4),
        out_shape=[sh] * 4, compiler_params=_params(("parallel", "parallel")))(c_arr, w, own, other, m, v)


def _ada_grad_adamw(c_t, dmod, w, m, v):
    R, C = w.shape
    br = _ew_rows(R, C)
    spec = pl.BlockSpec((br, C), lambda i: (i, 0))

    def body(c_ref, dm_ref, w_ref, m_ref, v_ref, g_ref, d_ref, nm_ref, nv_ref):
        cv = c_ref[...]
        sc = cv * _sig(cv)
        g = sc[:, 0:1] * dm_ref[0:1, :]
        for b in range(1, N_DEV):
            g = g + sc[:, b:b + 1] * dm_ref[b:b + 1, :]
        g_ref[...] = g
        d_ref[...], nm_ref[...], nv_ref[...] = _adamw_math(w_ref[...], g, m_ref[...], v_ref[...])

    sh = jax.ShapeDtypeStruct((R, C), F32)
    return _pcall(
        body, name="ada_grad_adamw", grid=(R // br,),
        in_specs=[pl.BlockSpec((br, N_DEV), lambda i: (i, 0)), pl.BlockSpec((N_DEV, C), lambda i: (0, 0)), spec, spec, spec],
        out_specs=[spec] * 4, out_shape=[sh] * 4, compiler_params=_params(("parallel",)))(c_t, dmod, w, m, v)


SMALL_ROWS = 16


def _small_sum(small_all, lb_logits):
    def body(s_ref, lbl_ref, o_ref):
        acc = s_ref[0:SMALL_ROWS, :]
        for d in range(1, N_DEV):
            acc = acc + s_ref[d * SMALL_ROWS:(d + 1) * SMALL_ROWS, :]
        o_ref[...] = acc
        z = lbl_ref[...]
        e = jnp.exp(z - jnp.max(z, axis=0, keepdims=True))
        p0 = e[0:1, :] / (e[0:1, :] + e[1:2, :])
        dz = acc[8:9, 0:A_WIDTH] * p0 * (1.0 - p0)
        o_ref[8:9, 0:A_WIDTH] = dz
        o_ref[10:11, 0:A_WIDTH] = -dz

    return _pcall(body, name="small_sum", out_shape=jax.ShapeDtypeStruct((SMALL_ROWS, D_MODEL), F32),
                  in_specs=[pl.BlockSpec(memory_space=pltpu.VMEM)] * 2, out_specs=pl.BlockSpec(memory_space=pltpu.VMEM),
                  compiler_params=_params())(small_all, lb_logits)


RELATIONS = ((1, 0), (0, 1), (1, 1))
ANY = pl.BlockSpec(memory_space=pl.ANY)


def _place():
    x, y, c = lax.axis_index("x"), lax.axis_index("y"), lax.axis_index("c")
    return x, y, c


def _allgather_small(x_shard, name):
    m_per, n = x_shard.shape

    def body(x_ref, out_ref, send_sems, recv_sems, local_sem):
        x, y, c = _place()
        me, sibling = (x, y, c), (x, y, 1 - c)
        chips = [(1 - x, y), (x, 1 - y), (1 - x, 1 - y)]

        def rows(px, py, pc):
            return out_ref.at[pl.ds((4 * px + 2 * py + pc) * m_per, m_per), :]

        def copy(k, block, to, src=None):
            return pltpu.make_async_remote_copy(
                src_ref=rows(*block) if src is None else src, dst_ref=rows(*block),
                send_sem=send_sems.at[k], recv_sem=recv_sems.at[k], device_id=to, device_id_type=MESH)

        mine = pltpu.make_async_copy(x_ref, rows(*me), local_sem)
        mine.start()
        first = [copy(0, me, sibling, src=x_ref)]
        first += [copy(1 + j, me, (*chip, c), src=x_ref) for j, chip in enumerate(chips)]
        for cp in first:
            cp.start()
        passed = [copy(4 + j, (*chip, c), sibling) for j, chip in enumerate(chips)]
        for j, chip in enumerate(chips):
            copy(1 + j, (*chip, c), me).wait_recv()
            passed[j].start()
        copy(0, sibling, me).wait_recv()
        for j, chip in enumerate(chips):
            copy(4 + j, (*chip, 1 - c), me).wait_recv()
        for cp in first + passed:
            cp.wait_send()
        mine.wait()

    return _pcall(
        body, name=name, out_shape=jax.ShapeDtypeStruct((N_DEV * m_per, n), x_shard.dtype),
        in_specs=[pl.BlockSpec(memory_space=pltpu.VMEM)], out_specs=pl.BlockSpec(memory_space=pltpu.VMEM),
        scratch_shapes=[pltpu.SemaphoreType.DMA((7,)), pltpu.SemaphoreType.DMA((7,)), pltpu.SemaphoreType.DMA],
        compiler_params=_params(),
    )(x_shard)


W_SHAPES = ((D_MODEL, IN_WIDTH, True), (A_WIDTH, D_MODEL, True), (B_WIDTH, D_MODEL, True),
            (D_MODEL, D_MODEL, False), (D_MODEL, MLP_HIDDEN, True), (MLP_HIDDEN, D_MODEL, False))
N_W = len(W_SHAPES)


def _shard_shape(w):
    R, C, by_col = W_SHAPES[w]
    return (R, C // N_CHIPS) if by_col else (R // N_CHIPS, C)


def _half_shape(w):
    sr, sc = _shard_shape(w)
    return sr // 2, sc


def _region(full_ref, w, chip, half):
    sr, sc = _shard_shape(w)
    by_col = W_SHAPES[w][2]
    r0, c0 = (0, chip * sc) if by_col else (chip * sr, 0)
    if half is None:
        return full_ref.at[pl.ds(r0, sr), pl.ds(c0, sc)]
    return full_ref.at[pl.ds(r0 + half * (sr // 2), sr // 2), pl.ds(c0, sc)]


def _on_device(fn):
    x, y, c = _place()
    me = 4 * x + 2 * y + c
    for d in range(N_DEV):
        @pl.when(me == d)
        def _(d=d):
            fn(x, y, c, d)


def _gather_plan(partials):
    ws = sorted(partials)
    pairs = [(i, w, k) for i, w in enumerate(ws) for k in range(3)]

    def first(pi, po, ps, x, y, c, d, i, w, k):
        chip, dc = d >> 1, d & 1
        rx, ry = RELATIONS[k]
        return pltpu.make_async_remote_copy(
            src_ref=_region(pi[i], w, chip, dc), dst_ref=_region(po[i], w, chip, dc),
            send_sem=ps[0].at[i * 3 + k], recv_sem=ps[1].at[i * 3 + k],
            device_id=(x ^ rx, y ^ ry, c), device_id_type=MESH)

    def landed(po, ps, x, y, c, d, i, w, k, half, to_sibling):
        rx, ry = RELATIONS[k]
        got = _region(po[i], w, (d >> 1) ^ (2 * rx + ry), half)
        s = 2 if to_sibling else 0
        return pltpu.make_async_remote_copy(
            src_ref=got, dst_ref=got, send_sem=ps[s].at[i * 3 + k], recv_sem=ps[s + 1].at[i * 3 + k],
            device_id=(x, y, 1 - c), device_id_type=MESH)

    def send(pi, po, ps):
        def run(x, y, c, d):
            for i, w, k in pairs:
                first(pi, po, ps, x, y, c, d, i, w, k).start()
        _on_device(run)

    def pass_on(pi, po, ps):
        def run(x, y, c, d):
            for i, w, k in pairs:
                landed(po, ps, x, y, c, d, i, w, k, d & 1, False).wait_recv()
                landed(po, ps, x, y, c, d, i, w, k, d & 1, True).start()
        _on_device(run)

    def finish(pi, po, ps):
        def run(x, y, c, d):
            for i, w, k in pairs:
                landed(po, ps, x, y, c, d, i, w, k, 1 - (d & 1), True).wait_recv()
            for i, w, k in pairs:
                first(pi, po, ps, x, y, c, d, i, w, k).wait_send()
                landed(po, ps, x, y, c, d, i, w, k, d & 1, True).wait_send()
        _on_device(run)

    return _Plan([partials[w] for w in ws], [jax.ShapeDtypeStruct(W_SHAPES[w][:2], BF16) for w in ws],
                 [pltpu.SemaphoreType.DMA((3 * len(ws),)) for _ in range(4)], [send, pass_on, finish],
                 {i: i for i in range(len(ws))})


def _grad_view(g, w):
    R, C, by_col = W_SHAPES[w]
    return g.reshape(1, 2, R // 2, C) if by_col else g.reshape(N_CHIPS, 2, R // N_CHIPS // 2, C)


def _start_wait_plan(ins, outs, n_copies, copies):
    def start(pi, po, ps):
        for cp in copies(pi, po, ps):
            cp.start()

    def finish(pi, po, ps):
        for cp in copies(pi, po, ps):
            cp.wait()

    return _Plan(ins, outs, [pltpu.SemaphoreType.DMA((n_copies,)), pltpu.SemaphoreType.DMA((n_copies,))], [start, finish])


def _sibling_exchange_plan(g4s):
    pieces = [(i, p) for i, g in enumerate(g4s) for p in range(g.shape[0])]

    def copies(pi, po, ps):
        x, y, c = _place()
        return [pltpu.make_async_remote_copy(
            src_ref=pi[i].at[p, 1 - c], dst_ref=po[i].at[p], send_sem=ps[0].at[n], recv_sem=ps[1].at[n],
            device_id=(x, y, 1 - c), device_id_type=MESH) for n, (i, p) in enumerate(pieces)]

    return _start_wait_plan(list(g4s), [jax.ShapeDtypeStruct((g.shape[0],) + g.shape[2:], F32) for g in g4s],
                            len(pieces), copies)


def _pair_sum(g4, other, c_arr, name):
    P, _, hr, C = g4.shape
    br = _ew_rows(hr, C)

    def body(c_ref, g_ref, o_ref, p_ref):
        p_ref[...] = (g_ref[...] + o_ref[...]).astype(BF16)

    return _pcall(
        body, name=name,
        grid_spec=pltpu.PrefetchScalarGridSpec(
            num_scalar_prefetch=1, grid=(P, hr // br),
            in_specs=[pl.BlockSpec((None, None, br, C), lambda p, i, c_ref: (p, c_ref[0], i, 0)),
                      pl.BlockSpec((None, br, C), lambda p, i, c_ref: (p, i, 0))],
            out_specs=pl.BlockSpec((None, br, C), lambda p, i, c_ref: (p, i, 0))),
        out_shape=jax.ShapeDtypeStruct((P, hr, C), BF16),
        compiler_params=_params(("parallel", "parallel")),
    )(c_arr, g4, other)


def _pair_part(p_ref, w, chip):
    sr, sc = _shard_shape(w)
    return p_ref.at[0, :, pl.ds(chip * sc, sc)] if W_SHAPES[w][2] else p_ref.at[chip]


def _chip_exchange_plan(pairs):
    ws = sorted(pairs)

    def stage(wait):
        def run(pi, po, ps):
            def on(x, y, c, d):
                for i, w in enumerate(ws):
                    for k, (rx, ry) in enumerate(RELATIONS):
                        cp = pltpu.make_async_remote_copy(
                            src_ref=_pair_part(pi[i], w, (d >> 1) ^ (2 * rx + ry)), dst_ref=po[i].at[k],
                            send_sem=ps[0].at[i * 3 + k], recv_sem=ps[1].at[i * 3 + k],
                            device_id=(x ^ rx, y ^ ry, c), device_id_type=MESH)
                        if wait:
                            cp.wait()
                        else:
                            cp.start()
            _on_device(on)
        return run

    return _Plan([pairs[w] for w in ws], [jax.ShapeDtypeStruct((3,) + _half_shape(w), BF16) for w in ws],
                 [pltpu.SemaphoreType.DMA((3 * len(ws),)), pltpu.SemaphoreType.DMA((3 * len(ws),))],
                 [stage(False), stage(True)])


def _sum_slots(pair, slots, w, chip_arr, name):
    _, hr, C = slots.shape
    br = _ew_rows(hr, C)
    own_map = (lambda i, chip: (0, i, chip[0])) if W_SHAPES[w][2] else (lambda i, chip: (chip[0], i, 0))

    def body(chip_ref, p_ref, s_ref, o_ref):
        acc = p_ref[...].astype(F32)
        for k in range(3):
            acc = acc + s_ref[k].astype(F32)
        o_ref[...] = acc

    return _pcall(
        body, name=name,
        grid_spec=pltpu.PrefetchScalarGridSpec(
            num_scalar_prefetch=1, grid=(hr // br,),
            in_specs=[pl.BlockSpec((None, br, C), own_map), pl.BlockSpec((3, br, C), lambda i, chip: (0, i, 0))],
            out_specs=pl.BlockSpec((br, C), lambda i, chip: (i, 0))),
        out_shape=jax.ShapeDtypeStruct((hr, C), F32), compiler_params=_params(("parallel",)),
    )(chip_arr, pair, slots)


def _sibling_share_plan(halves):
    def copies(pi, po, ps):
        x, y, c = _place()
        return [pltpu.make_async_remote_copy(
            src_ref=pi[i], dst_ref=po[i], send_sem=ps[0].at[i], recv_sem=ps[1].at[i],
            device_id=(x, y, 1 - c), device_id_type=MESH) for i in range(len(halves))]

    return _start_wait_plan(list(halves), [jax.ShapeDtypeStruct(h.shape, F32) for h in halves], len(halves), copies)


def _pad_lanes(v, width=D_MODEL):
    return jnp.pad(v, ((0, 0), (0, width - v.shape[1])))


def _pack_small(b_ada, norm1, norm2, lb, o_gain, q_gain, k_gain, sinks):
    rows = [b_ada.reshape(N_MOD, D_MODEL), norm1, norm2, jnp.concatenate([lb[0:1], o_gain], axis=1),
            _pad_lanes(jnp.concatenate([q_gain, k_gain, sinks], axis=1)), _pad_lanes(lb[1:2]),
            jnp.zeros((SMALL_ROWS - 11, D_MODEL), F32)]
    return jnp.concatenate(rows, axis=0)


def _unpack_small(p):
    return (p[0:6].reshape(1, N_MOD * D_MODEL), p[6:7], p[7:8],
            jnp.concatenate([p[8:9, 0:A_WIDTH], p[10:11, 0:A_WIDTH]], axis=0), p[8:9, A_WIDTH:],
            p[9:10, 0:64], p[9:10, 64:128], p[9:10, 128:144])


def kernel(x, c, w_ada, b_ada, norm1_gain, w_in, lb_logits, hgrn_o_gain, q_norm_gain, k_norm_gain, sinks, w_branch_a, w_branch_b, w_out, norm2_gain, w_mlp_in, w_mlp_out, loss_target, m_w_ada, m_b_ada, m_norm1_gain, m_w_in, m_lb_logits, m_hgrn_o_gain, m_q_norm_gain, m_k_norm_gain, m_sinks, m_w_branch_a, m_w_branch_b, m_w_out, m_norm2_gain, m_w_mlp_in, m_w_mlp_out, v_w_ada, v_b_ada, v_norm1_gain, v_w_in, v_lb_logits, v_hgrn_o_gain, v_q_norm_gain, v_k_norm_gain, v_sinks, v_w_branch_a, v_w_branch_b, v_w_out, v_norm2_gain, v_w_mlp_in, v_w_mlp_out):
    xi, yi, ci = _place()
    chip = 2 * xi + yi
    me = 4 * xi + 2 * yi + ci
    ada_cols = w_ada.shape[2]

    c_all = _allgather_small(jnp.broadcast_to(c, (8, D_MODEL)), "gather_c").reshape(N_DEV, 8, D_MODEL)[:, 0]
    b_cols = lax.dynamic_slice(b_ada, (0, chip * ada_cols), (1, ada_cols))
    mod_part = _ada_fwd(c_all, w_ada[0], b_cols)
    mod_all = _allgather_small(mod_part, "gather_mod").reshape(N_CHIPS, 2, N_DEV, ada_cols)[:, 0]
    mod_mine = lax.dynamic_index_in_dim(mod_all, me, axis=1, keepdims=False).reshape(N_MOD, D_MODEL)
    mod8 = jnp.concatenate([mod_mine, jnp.zeros((2, D_MODEL), F32)], axis=0)

    shards = (w_in[0], w_branch_a[0], w_branch_b[0], w_out[0], w_mlp_in[0], w_mlp_out[0])
    chip_arr = chip.astype(jnp.int32).reshape(1)
    c_arr = ci.astype(jnp.int32).reshape(1)
    parts = [_cast_into_full(s, w, chip_arr, f"cast_w{w}") for w, s in enumerate(shards)]

    grad_x, halves, theirs, st = _local_step(x[0], loss_target[0], mod8, norm1_gain, norm2_gain, lb_logits, hgrn_o_gain,
                                             q_norm_gain, k_norm_gain, sinks, parts, c_arr, chip_arr)
    loss = lax.psum(0.5 * jnp.sum(st["loss"][0]) / D_MODEL, ("x", "y", "c"))
    moments =((m_w_in, v_w_in), (m_w_branch_a, v_w_branch_a), (m_w_branch_b, v_w_branch_b), (m_w_out, v_w_out),
               (m_w_mlp_in, v_w_mlp_in), (m_w_mlp_out, v_w_mlp_out))
    big = [_adamw_halves(shards[w], halves[w], theirs[w], moments[w][0][0], moments[w][1][0], c_arr, f"adamw{w}")
           for w in range(N_W)]

    swa = st["swa"]
    small = jnp.concatenate([
        st["n1"][1:2], st["n1"][0:1], st["n2"][3:4], st["n2"][1:2], st["n2"][0:1], st["loss"][1:2],
        st["n1"][2:3], st["n2"][2:3], jnp.concatenate([st["d_lb"][0:1], st["d_og"][0:1]], axis=1),
        _pad_lanes(jnp.concatenate([swa[0:1, 0:64], swa[1:2, 0:64], swa[2:3, 0:16]], axis=1)),
        jnp.zeros((SMALL_ROWS - 10, D_MODEL), F32)], axis=0)
    small_all = _allgather_small(small, "gather_small")
    g_small = _small_sum(small_all, lb_logits)
    small_w = (b_ada, norm1_gain, norm2_gain, lb_logits, hgrn_o_gain, q_norm_gain, k_norm_gain, sinks)
    small_m = (m_b_ada, m_norm1_gain, m_norm2_gain, m_lb_logits, m_hgrn_o_gain, m_q_norm_gain, m_k_norm_gain, m_sinks)
    small_v = (v_b_ada, v_norm1_gain, v_norm2_gain, v_lb_logits, v_hgrn_o_gain, v_q_norm_gain, v_k_norm_gain, v_sinks)
    sm = [_unpack_small(t) for t in
          (g_small,) + tuple(_adamw(_pack_small(*small_w), g_small, _pack_small(*small_m), _pack_small(*small_v),
                                    "adamw_small"))]
    g_b, g_n1, g_n2, g_lb, g_og, g_qg, g_kg, g_sk = ([t[i] for t in sm] for i in range(8))

    dmod_all = small_all.reshape(N_DEV, SMALL_ROWS, D_MODEL)[:, 0:N_MOD].reshape(N_DEV, N_MOD * D_MODEL)
    dmod_cols = lax.dynamic_slice(dmod_all, (0, chip * ada_cols), (N_DEV, ada_cols))
    ada = _ada_grad_adamw(c_all.T, dmod_cols, w_ada[0], m_w_ada[0], v_w_ada[0])

    def ordered(k):
        lead = lambda a: a[None]
        return (lead(ada[k]), g_b[k], g_n1[k], lead(big[0][k]), g_lb[k], g_og[k], g_qg[k], g_kg[k], g_sk[k],
                lead(big[1][k]), lead(big[2][k]), lead(big[3][k]), g_n2[k], lead(big[4][k]), lead(big[5][k]))

    return (loss, grad_x[None]) + ordered(0) + ordered(1) + ordered(2) + ordered(3)
```

```python
import functools

import jax
import jax.numpy as jnp
from jax import lax
from jax.experimental import pallas as pl
from jax.experimental.pallas import tpu as pltpu

F32 = jnp.float32
BF16 = jnp.bfloat16
HIGHEST = lax.Precision.HIGHEST
MESH = pl.DeviceIdType.MESH

D_MODEL = 2048
A_WIDTH = 1024
A_HEADS = 8
A_HEAD_DIM = 128
A_CHUNK = 64
B_WIDTH = 1024
B_HEAD_DIM = 64
B_GROUP = 4
B_KV_HEADS = 4
B_KV_WIDTH = 256
BLOCK = 128
MLP_HIDDEN = 8192
IN_WIDTH = 9728
N_MOD = 6
EPS = 1e-6
N_CHIPS = 4
N_DEV = 8

OFF_QA, OFF_FA, OFF_IA, OFF_GA = 0, 1024, 2048, 3072
OFF_QB, OFF_KB, OFF_VB = 4096, 5120, 5376
OFF_GATE_A, OFF_GATE_B = 5632, 7680

ADAM_LR = 0.001
ADAM_B1 = 0.9
ADAM_B2 = 0.999
ADAM_EPS = 1e-08
ADAM_WD = 0.01
ADAM_STEP = 10

VMEM_LIMIT_V7X = 48 * 1024 * 1024
NEG_BIG = -1e30


def _params(sem=None, vmem=VMEM_LIMIT_V7X):
    return pltpu.CompilerParams(dimension_semantics=sem, vmem_limit_bytes=vmem)


class _Plan:
    def __init__(self, ins, outs, sems, stages, aliases=None, mid_at=()):
        self.ins, self.outs, self.sems, self.stages, self.aliases = ins, outs, sems, stages, aliases or {}
        self.mid_at = tuple(mid_at)
        assert len(self.mid_at) == len(stages) - 2


def _join(a, b):
    assert len(a.stages) == 2 and len(b.stages) == 2
    ni, no, ns = len(a.ins), len(a.outs), len(a.sems)

    def stage(k):
        def run(pi, po, ps):
            a.stages[k](pi[:ni], po[:no], ps[:ns])
            b.stages[k](pi[ni:], po[no:], ps[ns:])
        return run

    aliases = dict(a.aliases)
    aliases.update({ni + i: no + o for i, o in b.aliases.items()})
    return _Plan(a.ins + b.ins, a.outs + b.outs, a.sems + b.sems, [stage(0), stage(1)], aliases)


def _pcall(body, plan=None, **kw):
    if plan is None:
        return pl.pallas_call(body, **kw)
    grid = kw["grid"]
    single = not isinstance(kw["out_specs"], (list, tuple))
    in_specs = list(kw["in_specs"])
    out_specs = [kw["out_specs"]] if single else list(kw["out_specs"])
    out_shape = [kw["out_shape"]] if single else list(kw["out_shape"])
    scratch = list(kw.get("scratch_shapes", ()))
    n_in, n_out, n_scr = len(in_specs), len(out_specs), len(scratch)
    n_pi, n_po = len(plan.ins), len(plan.outs)
    total = 1
    for g in grid:
        total *= g
    n_st = len(plan.stages)

    def wrapped(*refs):
        o0 = n_in + n_pi
        s0 = o0 + n_out + n_po
        pi, po, ps = refs[n_in:o0], refs[o0 + n_out:s0], refs[s0 + n_scr:]
        lin = 0
        for d, g in enumerate(grid):
            lin = lin * g + pl.program_id(d)
        for si, frac in enumerate((0.0,) + plan.mid_at):
            @pl.when(lin == int(frac * (total - 1)))
            def _(si=si):
                plan.stages[si](pi, po, ps)
        body(*refs[:n_in], *refs[o0:o0 + n_out], *refs[s0:s0 + n_scr])

        @pl.when(lin == total - 1)
        def _():
            plan.stages[-1](pi, po, ps)

    any_spec = pl.BlockSpec(memory_space=pl.ANY)
    call = pl.pallas_call(
        wrapped, name=kw["name"], grid=grid, in_specs=in_specs + [any_spec] * n_pi,
        out_specs=out_specs + [any_spec] * n_po, out_shape=out_shape + list(plan.outs),
        scratch_shapes=scratch + list(plan.sems),
        input_output_aliases={n_in + i: n_out + o for i, o in plan.aliases.items()},
        compiler_params=_params(("arbitrary",) * len(grid)))

    def run(*args):
        res = call(*args, *plan.ins)
        outs = list(res[:n_out])
        return (outs[0] if single else outs), list(res[n_out:])

    return run


def _run_plan(plan, name):
    return _pcall(lambda: None, plan=plan, name=name, grid=(1,), in_specs=[], out_specs=[], out_shape=[])()[1]


def _sig(x):
    return 1.0 / (1.0 + jnp.exp(-x))


def _nn(a, b):
    return lax.dot_general(a.astype(BF16), b.astype(BF16), (((1,), (0,)), ((), ())), preferred_element_type=F32)


def _nt(a, b):
    return lax.dot_general(a.astype(BF16), b.astype(BF16), (((1,), (1,)), ((), ())), preferred_element_type=F32)


def _tn(a, b):
    return lax.dot_general(a.astype(BF16), b.astype(BF16), (((0,), (0,)), ((), ())), preferred_element_type=F32)


def _mm(a, b, *, name, ta=False, tb=False, bm=1024, bn=1024, bk=2048, out_dtypes=(F32,), epi=None, extras=(), plan=None):
    if ta:
        K, M = a.shape
    else:
        M, K = a.shape
    if tb:
        N, K2 = b.shape
    else:
        K2, N = b.shape
    bm, bn, bk = min(bm, M), min(bn, N), min(bk, K)
    assert K == K2 and M % bm == 0 and N % bn == 0 and K % bk == 0, (name, a.shape, b.shape)
    nk = K // bk
    a_spec = pl.BlockSpec((bk, bm), lambda i, j, k: (k, i)) if ta else pl.BlockSpec((bm, bk), lambda i, j, k: (i, k))
    b_spec = pl.BlockSpec((bn, bk), lambda i, j, k: (j, k)) if tb else pl.BlockSpec((bk, bn), lambda i, j, k: (k, j))
    t_spec = pl.BlockSpec((bm, bn), lambda i, j, k: (i, j))
    dims = (((0 if ta else 1,), (1 if tb else 0,)), ((), ()))
    n_e, n_o = len(extras), len(out_dtypes)

    def body(*refs):
        a_ref, b_ref = refs[0], refs[1]
        e_refs = refs[2:2 + n_e]
        o_refs = refs[2 + n_e:2 + n_e + n_o]

        def finish(acc):
            outs = (acc,) if epi is None else epi(acc, *[e[...] for e in e_refs])
            for o_ref, o in zip(o_refs, outs):
                o_ref[...] = o.astype(o_ref.dtype)

        part = lax.dot_general(a_ref[...].astype(BF16), b_ref[...].astype(BF16), dims, preferred_element_type=F32)
        if nk == 1:
            finish(part)
        else:
            acc_ref = refs[-1]
            k = pl.program_id(2)

            @pl.when(k == 0)
            def _():
                acc_ref[...] = part

            @pl.when(k > 0)
            def _():
                acc_ref[...] += part

            @pl.when(k == nk - 1)
            def _():
                finish(acc_ref[...])

    out = _pcall(
        body, plan=plan, name=name, grid=(M // bm, N // bn, nk),
        in_specs=[a_spec, b_spec] + [t_spec] * n_e,
        out_specs=[t_spec] * n_o,
        out_shape=[jax.ShapeDtypeStruct((M, N), dt) for dt in out_dtypes],
        scratch_shapes=[pltpu.VMEM((bm, bn), F32)] if nk > 1 else [],
        compiler_params=_params(("parallel", "parallel", "arbitrary")),
    )(a, b, *extras)
    if plan is not None:
        return (out[0][0] if n_o == 1 else out[0]), out[1]
    return out[0] if n_o == 1 else out


def _ada_fwd(c_all, w_ada, b_cols):
    n = w_ada.shape[1]
    bn = 512

    def body(c_ref, w_ref, b_ref, o_ref):
        cv = c_ref[...]
        sc = cv * _sig(cv)
        o_ref[...] = jnp.dot(sc, w_ref[...], precision=HIGHEST, preferred_element_type=F32) + b_ref[...]

    return _pcall(
        body, name="ada_fwd", grid=(n // bn,),
        in_specs=[pl.BlockSpec((N_DEV, D_MODEL), lambda j: (0, 0)), pl.BlockSpec((D_MODEL, bn), lambda j: (0, j)),
                  pl.BlockSpec((1, bn), lambda j: (0, j))],
        out_specs=pl.BlockSpec((N_DEV, bn), lambda j: (0, j)),
        out_shape=jax.ShapeDtypeStruct((N_DEV, n), F32),
        compiler_params=_params(("parallel",)),
    )(c_all, w_ada, b_cols)


ROWS_EW = 256


def _rms_fwd_math(x, gain, scale, shift):
    rstd = lax.rsqrt(jnp.mean(x * x, axis=-1, keepdims=True) + EPS)
    xhat = x * rstd
    n = xhat * gain
    return n * (1.0 + scale) + shift, xhat, n, rstd


def _rms_bwd_math(dh, xhat, n, rstd, gain, scale):
    dn = dh * (1.0 + scale)
    dxhat = dn * gain
    dx = rstd * (dxhat - xhat * jnp.mean(dxhat * xhat, axis=-1, keepdims=True))
    d_scale = jnp.sum(dh * n, axis=0, keepdims=True)
    d_shift = jnp.sum(dh, axis=0, keepdims=True)
    d_gain = jnp.sum(dn * xhat, axis=0, keepdims=True)
    return dx, d_scale, d_shift, d_gain


def _row_spec(w=D_MODEL, br=ROWS_EW):
    return pl.BlockSpec((br, w), lambda i: (i, 0))


def _vec_spec(r=8, w=D_MODEL):
    return pl.BlockSpec((r, w), lambda i: (0, 0))


def _norm1_fwd(x, gain, mod8):
    T = x.shape[0]

    def body(x_ref, g_ref, m_ref, h_ref):
        h, _, _, _ = _rms_fwd_math(x_ref[...], g_ref[...], m_ref[1:2, :], m_ref[0:1, :])
        h_ref[...] = h.astype(BF16)

    return _pcall(
        body, name="norm1_fwd", grid=(T // ROWS_EW,),
        in_specs=[_row_spec(), _vec_spec(1), _vec_spec()],
        out_specs=_row_spec(), out_shape=jax.ShapeDtypeStruct((T, D_MODEL), BF16),
        compiler_params=_params(("parallel",)),
    )(x, gain, mod8)


def _res_norm2_fwd(x, mo, gain, mod8):
    T = x.shape[0]

    def body(x_ref, mo_ref, g_ref, m_ref, x1_ref, h_ref):
        x1 = x_ref[...] + m_ref[2:3, :] * mo_ref[...]
        x1_ref[...] = x1
        h, _, _, _ = _rms_fwd_math(x1, g_ref[...], m_ref[4:5, :], m_ref[3:4, :])
        h_ref[...] = h.astype(BF16)

    return _pcall(
        body, name="res_norm2_fwd", grid=(T // ROWS_EW,),
        in_specs=[_row_spec(), _row_spec(), _vec_spec(1), _vec_spec()],
        out_specs=[_row_spec(), _row_spec()],
        out_shape=[jax.ShapeDtypeStruct((T, D_MODEL), F32), jax.ShapeDtypeStruct((T, D_MODEL), BF16)],
        compiler_params=_params(("parallel",)),
    )(x, mo, gain, mod8)


def _loss_bwd(x1, mlp, target, mod8):
    T = x1.shape[0]

    def body(x1_ref, mlp_ref, t_ref, m_ref, dy_ref, dmlp_ref, st_ref):
        i = pl.program_id(0)
        gate = m_ref[5:6, :]
        mlp_v = mlp_ref[...]
        err = x1_ref[...] + gate * mlp_v - t_ref[...]
        dy = err * (1.0 / D_MODEL)
        dy_ref[...] = dy
        dmlp_ref[...] = (dy * gate).astype(BF16)

        @pl.when(i == 0)
        def _():
            st_ref[...] = jnp.zeros_like(st_ref)

        st_ref[0:1, :] += jnp.sum(err * err, axis=0, keepdims=True)
        st_ref[1:2, :] += jnp.sum(dy * mlp_v, axis=0, keepdims=True)

    return _pcall(
        body, name="loss_bwd", grid=(T // ROWS_EW,),
        in_specs=[_row_spec(), _row_spec(), _row_spec(), _vec_spec()],
        out_specs=[_row_spec(), _row_spec(), _vec_spec()],
        out_shape=[jax.ShapeDtypeStruct((T, D_MODEL), F32), jax.ShapeDtypeStruct((T, D_MODEL), BF16),
                   jax.ShapeDtypeStruct((8, D_MODEL), F32)],
        compiler_params=_params(("arbitrary",)),
    )(x1, mlp, target, mod8)


def _norm2_bwd(dh2, x1, dy, mo, gain, mod8):
    T = x1.shape[0]

    def body(dh_ref, x1_ref, dy_ref, mo_ref, g_ref, m_ref, dx1_ref, dmo_ref, st_ref):
        i = pl.program_id(0)
        gain_v, scale = g_ref[...], m_ref[4:5, :]
        _, xhat, n, rstd = _rms_fwd_math(x1_ref[...], gain_v, scale, m_ref[3:4, :])
        dx, d_scale, d_shift, d_gain = _rms_bwd_math(dh_ref[...], xhat, n, rstd, gain_v, scale)
        dx1 = dy_ref[...] + dx
        dx1_ref[...] = dx1
        dmo_ref[...] = (dx1 * m_ref[2:3, :]).astype(BF16)

        @pl.when(i == 0)
        def _():
            st_ref[...] = jnp.zeros_like(st_ref)

        st_ref[0:1, :] += d_scale
        st_ref[1:2, :] += d_shift
        st_ref[2:3, :] += d_gain
        st_ref[3:4, :] += jnp.sum(dx1 * mo_ref[...], axis=0, keepdims=True)

    return _pcall(
        body, name="norm2_bwd", grid=(T // ROWS_EW,),
        in_specs=[_row_spec(), _row_spec(), _row_spec(), _row_spec(), _vec_spec(1), _vec_spec()],
        out_specs=[_row_spec(), _row_spec(), _vec_spec()],
        out_shape=[jax.ShapeDtypeStruct((T, D_MODEL), F32), jax.ShapeDtypeStruct((T, D_MODEL), BF16),
                   jax.ShapeDtypeStruct((8, D_MODEL), F32)],
        compiler_params=_params(("arbitrary",)),
    )(dh2, x1, dy, mo, gain, mod8)


def _norm1_bwd(dh, x, dx1, gain, mod8):
    T = x.shape[0]

    def body(dh_ref, x_ref, dx1_ref, g_ref, m_ref, dx_ref, st_ref):
        i = pl.program_id(0)
        gain_v, scale = g_ref[...], m_ref[1:2, :]
        _, xhat, n, rstd = _rms_fwd_math(x_ref[...], gain_v, scale, m_ref[0:1, :])
        dx, d_scale, d_shift, d_gain = _rms_bwd_math(dh_ref[...], xhat, n, rstd, gain_v, scale)
        dx_ref[...] = dx1_ref[...] + dx

        @pl.when(i == 0)
        def _():
            st_ref[...] = jnp.zeros_like(st_ref)

        st_ref[0:1, :] += d_scale
        st_ref[1:2, :] += d_shift
        st_ref[2:3, :] += d_gain

    return _pcall(
        body, name="norm1_bwd", grid=(T // ROWS_EW,),
        in_specs=[_row_spec(), _row_spec(), _row_spec(), _vec_spec(1), _vec_spec()],
        out_specs=[_row_spec(), _vec_spec()],
        out_shape=[jax.ShapeDtypeStruct((T, D_MODEL), F32), jax.ShapeDtypeStruct((8, D_MODEL), F32)],
        compiler_params=_params(("arbitrary",)),
    )(dh, x, dx1, gain, mod8)


MERGE_BC = 512


def _merge_specs():
    ga = pl.BlockSpec((ROWS_EW, MERGE_BC), lambda i, j: (i, OFF_GATE_A // MERGE_BC + j))
    gb = pl.BlockSpec((ROWS_EW, MERGE_BC), lambda i, j: (i, OFF_GATE_B // MERGE_BC + j))
    t = pl.BlockSpec((ROWS_EW, MERGE_BC), lambda i, j: (i, j))
    return ga, gb, t


def _merge_fwd(proj, ya, yb):
    T = proj.shape[0]
    ga, gb, t = _merge_specs()

    def body(ga_ref, gb_ref, ya_ref, yb_ref, o_ref):
        o_ref[...] = (_sig(ga_ref[...]) * ya_ref[...] + _sig(gb_ref[...]) * yb_ref[...]).astype(BF16)

    return _pcall(
        body, name="merge_fwd", grid=(T // ROWS_EW, D_MODEL // MERGE_BC),
        in_specs=[ga, gb, t, t], out_specs=t, out_shape=jax.ShapeDtypeStruct((T, D_MODEL), BF16),
        compiler_params=_params(("parallel", "parallel")),
    )(proj, proj, ya, yb)


def _merge_bwd(proj, ya, yb, dmerged):
    T = proj.shape[0]
    ga, gb, t = _merge_specs()

    def body(ga_ref, gb_ref, ya_ref, yb_ref, dm_ref, dya_ref, dyb_ref, dga_ref, dgb_ref):
        dm = dm_ref[...]
        sa, sb = _sig(ga_ref[...]), _sig(gb_ref[...])
        dya_ref[...] = (dm * sa).astype(BF16)
        dyb_ref[...] = (dm * sb).astype(BF16)
        dga_ref[...] = (dm * ya_ref[...] * sa * (1.0 - sa)).astype(BF16)
        dgb_ref[...] = (dm * yb_ref[...] * sb * (1.0 - sb)).astype(BF16)

    sh = jax.ShapeDtypeStruct((T, D_MODEL), BF16)
    return _pcall(
        body, name="merge_bwd", grid=(T // ROWS_EW, D_MODEL // MERGE_BC),
        in_specs=[ga, gb, t, t, t], out_specs=[t, t, t, t], out_shape=[sh, sh, sh, sh],
        compiler_params=_params(("parallel", "parallel")),
    )(proj, proj, ya, yb, dmerged)


def _hgrn_rows(T):
    return 512 if T >= 1024 else 128


def _lower_bound(lbl):
    e = jnp.exp(lbl - jnp.max(lbl, axis=0, keepdims=True))
    return e[0:1, :] / (e[0:1, :] + e[1:2, :])


def _chunk_sum_matrix(rows, backward):
    shift = A_CHUNK.bit_length() - 1
    r = lax.broadcasted_iota(jnp.int32, (rows, rows), 0)
    c = lax.broadcasted_iota(jnp.int32, (rows, rows), 1)
    same = jnp.right_shift(r, shift) == jnp.right_shift(c, shift)
    return (same & ((r <= c) if backward else (r >= c))).astype(BF16)


def _chunk_sums(m, x):
    n = x.shape[1]
    hi = x.astype(BF16)
    rest = x - hi.astype(F32)
    mid = rest.astype(BF16)
    lo = (rest - mid.astype(F32)).astype(BF16)
    y = jnp.dot(m, jnp.concatenate([hi, mid, lo], axis=1), preferred_element_type=F32)
    return y[:, 0:n] + y[:, n:2 * n] + y[:, 2 * n:3 * n]


def _hgrn_block_pre(q, fl, lb, m_fwd):
    sg = _sig(fl)
    f = lb + (1.0 - lb) * sg
    sq = _sig(q)
    return dict(sg=sg, f=f, k=1.0 - f, sq=sq, qf=q * sq, b=_chunk_sums(m_fwd, jnp.log(f)))


def _hgrn_chunk_fwd(pre, r, v, st):
    C = A_CHUNK
    qf, k, b = pre["qf"][r], pre["k"][r], pre["b"][r]
    causal = lax.broadcasted_iota(jnp.int32, (C, C), 0) >= lax.broadcasted_iota(jnp.int32, (C, C), 1)
    bm = b[C // 2 - 1:C // 2, :]
    bl = b[C - 1:C, :]
    e_q, e_k = jnp.exp(b - bm), jnp.exp(bm - b)
    e_b, e_l = jnp.exp(b), jnp.exp(bl - b)
    qd, kd = qf * e_q, k * e_k
    qe, ke = qf * e_b, k * e_l
    att = jnp.where(causal, _nt(qd, kd), 0.0)
    o = _nn(att, v) + _nt(qe, st)
    dec = jnp.exp(bl)
    st_next = st * dec + _tn(v, ke)
    return dict(causal=causal, e_q=e_q, e_k=e_k, e_b=e_b, e_l=e_l, qd=qd, kd=kd,
                qe=qe, ke=ke, att=att, o=o, dec=dec, st_next=st_next)


HGRN_HEADS_PER_STEP = 4


def _hgrn_fwd(proj, lb_logits, o_gain, plan=None):
    T = proj.shape[0]
    BR = _hgrn_rows(T)
    cps = BR // A_CHUNK
    K, NH = A_HEAD_DIM, HGRN_HEADS_PER_STEP
    W = NH * K

    def col(off):
        return pl.BlockSpec((BR, W), lambda h, cb: (cb, off // W + h))

    def body(q_ref, f_ref, i_ref, g_ref, lbl_ref, og_ref, o_ref, s_ref, st):
        @pl.when(pl.program_id(1) == 0)
        def _():
            st[...] = jnp.zeros_like(st)

        lb_all = _lower_bound(lbl_ref[...])
        m_fwd = _chunk_sum_matrix(BR, False)
        pre = [_hgrn_block_pre(q_ref[:, n * K:(n + 1) * K], f_ref[:, n * K:(n + 1) * K], lb_all[:, n * K:(n + 1) * K], m_fwd)
               for n in range(NH)]
        state = [st[n] for n in range(NH)]
        for ci in range(cps):
            r = slice(ci * A_CHUNK, (ci + 1) * A_CHUNK)
            for n in range(NH):
                hs = slice(n * K, (n + 1) * K)
                s_ref[n, ci] = state[n]
                c = _hgrn_chunk_fwd(pre[n], r, i_ref[r, hs], state[n])
                state[n] = c["st_next"]
                o = c["o"]
                on = o * lax.rsqrt(jnp.mean(o * o, axis=-1, keepdims=True) + EPS)
                g = g_ref[r, hs]
                o_ref[r, hs] = (on * og_ref[:, hs] * (g * _sig(g))).astype(BF16)
        for n in range(NH):
            st[n] = state[n]

    return _pcall(
        body, plan=plan, name="hgrn_fwd", grid=(A_HEADS // NH, T // BR),
        in_specs=[col(OFF_QA), col(OFF_FA), col(OFF_IA), col(OFF_GA),
                  pl.BlockSpec((2, W), lambda h, cb: (0, h)), pl.BlockSpec((1, W), lambda h, cb: (0, h))],
        out_specs=[pl.BlockSpec((BR, W), lambda h, cb: (cb, h)),
                   pl.BlockSpec((NH, cps, K, K), lambda h, cb: (h, cb, 0, 0))],
        out_shape=[jax.ShapeDtypeStruct((T, A_WIDTH), BF16),
                   jax.ShapeDtypeStruct((A_HEADS, T // A_CHUNK, K, K), F32)],
        scratch_shapes=[pltpu.VMEM((NH, K, K), F32)],
        compiler_params=_params(("parallel", "arbitrary")),
    )(proj, proj, proj, proj, lb_logits, o_gain)


def _hgrn_bwd(proj, lb_logits, o_gain, states, do, plan=None):
    T = proj.shape[0]
    BR = _hgrn_rows(T)
    cps = BR // A_CHUNK
    ncb = T // BR
    K, C, NH = A_HEAD_DIM, A_CHUNK, HGRN_HEADS_PER_STEP
    W = NH * K

    def col(off):
        return pl.BlockSpec((BR, W), lambda h, cb: (ncb - 1 - cb, off // W + h))

    def body(q_ref, f_ref, i_ref, g_ref, lbl_ref, og_ref, s_ref, do_ref,
             dq_ref, df_ref, di_ref, dg_ref, dlb_ref, dog_ref, dst):
        @pl.when(pl.program_id(1) == 0)
        def _():
            dst[...] = jnp.zeros_like(dst)
            dlb_ref[...] = jnp.zeros_like(dlb_ref)
            dog_ref[...] = jnp.zeros_like(dog_ref)

        lb_all = _lower_bound(lbl_ref[...])
        row = lax.broadcasted_iota(jnp.int32, (C, K), 0)
        m_fwd, m_bwd = _chunk_sum_matrix(BR, False), _chunk_sum_matrix(BR, True)
        pre = [_hgrn_block_pre(q_ref[:, n * K:(n + 1) * K], f_ref[:, n * K:(n + 1) * K], lb_all[:, n * K:(n + 1) * K], m_fwd)
               for n in range(NH)]
        d_state = [dst[n] for n in range(NH)]
        d_og = [jnp.zeros((1, K), F32) for _ in range(NH)]
        db_of = [[None] * cps for _ in range(NH)]
        dk_of = [[None] * cps for _ in range(NH)]
        for ci in reversed(range(cps)):
            r = slice(ci * C, (ci + 1) * C)
            for n in range(NH):
                hs = slice(n * K, (n + 1) * K)
                gain = og_ref[:, hs]
                st = s_ref[n, ci]
                v = i_ref[r, hs]
                q = q_ref[r, hs]
                c = _hgrn_chunk_fwd(pre[n], r, v, st)
                dst_next = d_state[n]
                o = c["o"]
                rn = lax.rsqrt(jnp.mean(o * o, axis=-1, keepdims=True) + EPS)
                on = o * rn
                g = g_ref[r, hs]
                sgg = _sig(g)
                dy = do_ref[r, hs]
                d_ong = dy * (g * sgg)
                dg_ref[r, hs] = (dy * (on * gain) * (sgg * (1.0 + g * (1.0 - sgg)))).astype(BF16)
                d_og[n] = d_og[n] + jnp.sum(d_ong * on, axis=0, keepdims=True)
                d_on = d_ong * gain
                d_o = rn * (d_on - on * jnp.mean(d_on * on, axis=-1, keepdims=True))
                datt = jnp.where(c["causal"], _nt(d_o, v), 0.0)
                dv = _tn(c["att"], d_o) + _nt(c["ke"], dst_next)
                dqd = _nn(datt, c["kd"])
                dkd = _tn(datt, c["qd"])
                dqe = _nn(d_o, st)
                dke = _nn(v, dst_next)
                d_state[n] = dst_next * c["dec"] + _tn(d_o, c["qe"])
                d_dec = jnp.sum(dst_next * st, axis=0, keepdims=True)
                t_q, t_k = dqd * c["qd"], dkd * c["kd"]
                t_e, t_l = dqe * c["qe"], dke * c["ke"]
                db = t_q - t_k + t_e - t_l
                dbm = jnp.sum(t_k - t_q, axis=0, keepdims=True)
                dbl = jnp.sum(t_l, axis=0, keepdims=True) + d_dec * c["dec"]
                db_of[n][ci] = db + jnp.where(row == C // 2 - 1, dbm, 0.0) + jnp.where(row == C - 1, dbl, 0.0)
                dqf = dqd * c["e_q"] + dqe * c["e_b"]
                sq = pre[n]["sq"][r]
                dq_ref[r, hs] = (dqf * (sq * (1.0 + q * (1.0 - sq)))).astype(BF16)
                dk_of[n][ci] = dkd * c["e_k"] + dke * c["e_l"]
                di_ref[r, hs] = dv.astype(BF16)
        for n in range(NH):
            hs = slice(n * K, (n + 1) * K)
            dst[n] = d_state[n]
            dog_ref[0:1, hs] += d_og[n]
            lb, sg = lb_all[:, hs], pre[n]["sg"]
            dlf = _chunk_sums(m_bwd, jnp.concatenate(db_of[n], axis=0))
            df = dlf / pre[n]["f"] - jnp.concatenate(dk_of[n], axis=0)
            df_ref[:, hs] = (df * (1.0 - lb) * sg * (1.0 - sg)).astype(BF16)
            dlb_ref[0:1, hs] += jnp.sum(df * (1.0 - sg), axis=0, keepdims=True)

    ocol = pl.BlockSpec((BR, W), lambda h, cb: (ncb - 1 - cb, h))
    vec = pl.BlockSpec((8, W), lambda h, cb: (0, h))
    return _pcall(
        body, plan=plan, name="hgrn_bwd", grid=(A_HEADS // NH, ncb),
        in_specs=[col(OFF_QA), col(OFF_FA), col(OFF_IA), col(OFF_GA),
                  pl.BlockSpec((2, W), lambda h, cb: (0, h)), pl.BlockSpec((1, W), lambda h, cb: (0, h)),
                  pl.BlockSpec((NH, cps, K, K), lambda h, cb: (h, ncb - 1 - cb, 0, 0)),
                  pl.BlockSpec((BR, W), lambda h, cb: (ncb - 1 - cb, h))],
        out_specs=[ocol, ocol, ocol, ocol, vec, vec],
        out_shape=[jax.ShapeDtypeStruct((T, A_WIDTH), BF16)] * 4 + [jax.ShapeDtypeStruct((8, A_WIDTH), F32)] * 2,
        scratch_shapes=[pltpu.VMEM((NH, K, K), F32)],
        compiler_params=_params(("parallel", "arbitrary")),
    )(proj, proj, proj, proj, lb_logits, o_gain, states, do)


def _head_norm(x):
    r = lax.rsqrt(jnp.mean(x * x, axis=-1, keepdims=True) + EPS)
    return x * r, r


def _head_norm_bwd(dy, xn, r, gain):
    dxn = dy * gain
    return r * (dxn - xn * jnp.mean(dxn * xn, axis=-1, keepdims=True)), jnp.sum(dy * xn, axis=0, keepdims=True)


def _swa_mask(has_prev):
    rows = B_GROUP * BLOCK
    r = lax.broadcasted_iota(jnp.int32, (rows, 2 * BLOCK), 0) % BLOCK
    c = lax.broadcasted_iota(jnp.int32, (rows, 2 * BLOCK), 1)
    rel = r + BLOCK - c
    return (rel >= 0) & (rel < BLOCK) & ((c >= BLOCK) | has_prev)


def _swa_head_fwd(j, q_ref, kp_ref, kc_ref, vp_ref, vc_ref, qg, kg, sk_ref, mask):
    hs = slice(j * B_HEAD_DIM, (j + 1) * B_HEAD_DIM)
    kcat = jnp.concatenate([kp_ref[:, hs], kc_ref[:, hs]], axis=0)
    vcat = jnp.concatenate([vp_ref[:, hs], vc_ref[:, hs]], axis=0)
    qs = jnp.concatenate([q_ref[:, pl.ds((j * B_GROUP + g) * B_HEAD_DIM, B_HEAD_DIM)] for g in range(B_GROUP)], axis=0)
    kn, kr = _head_norm(kcat)
    qn, qr = _head_norm(qs)
    kh, qh = kn * kg, qn * qg
    s = jnp.where(mask, _nt(qh, kh) * (B_HEAD_DIM ** -0.5), NEG_BIG)
    sink = jnp.concatenate(
        [jnp.broadcast_to(sk_ref[0:1, pl.ds(j * B_GROUP + g, 1)], (BLOCK, 1)) for g in range(B_GROUP)], axis=0)
    m = jnp.maximum(jnp.max(s, axis=-1, keepdims=True), sink)
    p = jnp.exp(s - m)
    e_sink = jnp.exp(sink - m)
    inv = 1.0 / (jnp.sum(p, axis=-1, keepdims=True) + e_sink)
    prob = p * inv
    return dict(vcat=vcat, kn=kn, kr=kr, qn=qn, qr=qr, kh=kh, qh=qh, prob=prob, p_sink=e_sink * inv)


def _swa_in_specs(nb, last):
    def qi(n):
        return jnp.minimum(n, last)

    q = pl.BlockSpec((BLOCK, B_WIDTH), lambda n: (qi(n), OFF_QB // B_WIDTH))
    kc = pl.BlockSpec((BLOCK, B_KV_WIDTH), lambda n: (qi(n), OFF_KB // B_KV_WIDTH))
    kp = pl.BlockSpec((BLOCK, B_KV_WIDTH), lambda n: (jnp.maximum(qi(n) - 1, 0), OFF_KB // B_KV_WIDTH))
    vc = pl.BlockSpec((BLOCK, B_KV_WIDTH), lambda n: (qi(n), OFF_VB // B_KV_WIDTH))
    vp = pl.BlockSpec((BLOCK, B_KV_WIDTH), lambda n: (jnp.maximum(qi(n) - 1, 0), OFF_VB // B_KV_WIDTH))
    small = [pl.BlockSpec((1, B_HEAD_DIM), lambda n: (0, 0)), pl.BlockSpec((1, B_HEAD_DIM), lambda n: (0, 0)),
             pl.BlockSpec((1, B_GROUP * B_KV_HEADS), lambda n: (0, 0))]
    return [q, kp, kc, vp, vc] + small


def _swa_fwd(proj, q_gain, k_gain, sinks, plan=None):
    T = proj.shape[0]
    nb = T // BLOCK

    def body(q_ref, kp_ref, kc_ref, vp_ref, vc_ref, qg_ref, kg_ref, sk_ref, o_ref):
        mask = _swa_mask(pl.program_id(0) > 0)
        for j in range(B_KV_HEADS):
            c = _swa_head_fwd(j, q_ref, kp_ref, kc_ref, vp_ref, vc_ref, qg_ref[...], kg_ref[...], sk_ref, mask)
            o = _nn(c["prob"], c["vcat"])
            for g in range(B_GROUP):
                o_ref[:, pl.ds((j * B_GROUP + g) * B_HEAD_DIM, B_HEAD_DIM)] = o[g * BLOCK:(g + 1) * BLOCK].astype(BF16)

    return _pcall(
        body, plan=plan, name="swa_fwd", grid=(nb,),
        in_specs=_swa_in_specs(nb, nb - 1),
        out_specs=pl.BlockSpec((BLOCK, B_WIDTH), lambda n: (n, 0)),
        out_shape=jax.ShapeDtypeStruct((T, B_WIDTH), BF16),
        compiler_params=_params(("parallel",)),
    )(proj, proj, proj, proj, proj, q_gain, k_gain, sinks)


def _swa_bwd(proj, q_gain, k_gain, sinks, do, plan=None):
    T = proj.shape[0]
    nb = T // BLOCK
    scale = B_HEAD_DIM ** -0.5

    def body(q_ref, kp_ref, kc_ref, vp_ref, vc_ref, qg_ref, kg_ref, sk_ref, do_ref,
             dq_ref, dkv_ref, sm_ref, ck, cv):
        n = pl.program_id(0)

        @pl.when(n == 0)
        def _():
            ck[...] = jnp.zeros_like(ck)
            cv[...] = jnp.zeros_like(cv)
            sm_ref[...] = jnp.zeros_like(sm_ref)

        @pl.when(n < nb)
        def _():
            mask = _swa_mask(n > 0)
            qg, kg = qg_ref[...], kg_ref[...]
            lane = lax.broadcasted_iota(jnp.int32, (1, BLOCK), 1)
            for j in range(B_KV_HEADS):
                hs = slice(j * B_HEAD_DIM, (j + 1) * B_HEAD_DIM)
                vs = slice(B_KV_WIDTH + j * B_HEAD_DIM, B_KV_WIDTH + (j + 1) * B_HEAD_DIM)
                c = _swa_head_fwd(j, q_ref, kp_ref, kc_ref, vp_ref, vc_ref, qg, kg, sk_ref, mask)
                d_out = jnp.concatenate(
                    [do_ref[:, pl.ds((j * B_GROUP + g) * B_HEAD_DIM, B_HEAD_DIM)] for g in range(B_GROUP)], axis=0)
                prob = c["prob"]
                out = _nn(prob, c["vcat"])
                delta = jnp.sum(d_out * out, axis=-1, keepdims=True)
                ds = prob * (_nt(d_out, c["vcat"]) - delta)
                d_sink = -c["p_sink"] * delta
                dqh = _nn(ds, c["kh"]) * scale
                dkh = _tn(ds, c["qh"]) * scale
                dv = _tn(prob, d_out)
                dq, dqg = _head_norm_bwd(dqh, c["qn"], c["qr"], qg)
                dk, dkg = _head_norm_bwd(dkh, c["kn"], c["kr"], kg)
                sm_ref[0:1, 0:B_HEAD_DIM] += dqg
                sm_ref[1:2, 0:B_HEAD_DIM] += dkg
                for g in range(B_GROUP):
                    dq_ref[:, pl.ds((j * B_GROUP + g) * B_HEAD_DIM, B_HEAD_DIM)] = dq[g * BLOCK:(g + 1) * BLOCK].astype(BF16)
                    tot = jnp.sum(d_sink[g * BLOCK:(g + 1) * BLOCK], axis=0, keepdims=True)
                    sm_ref[2:3, :] += jnp.where(lane == j * B_GROUP + g, tot, 0.0)
                dkv_ref[:, hs] = (ck[:, hs] + dk[0:BLOCK]).astype(BF16)
                dkv_ref[:, vs] = (cv[:, hs] + dv[0:BLOCK]).astype(BF16)
                ck[:, hs] = dk[BLOCK:2 * BLOCK]
                cv[:, hs] = dv[BLOCK:2 * BLOCK]

        @pl.when(n == nb)
        def _():
            dkv_ref[:, 0:B_KV_WIDTH] = ck[...].astype(BF16)
            dkv_ref[:, B_KV_WIDTH:2 * B_KV_WIDTH] = cv[...].astype(BF16)

    return _pcall(
        body, plan=plan, name="swa_bwd", grid=(nb + 1,),
        in_specs=_swa_in_specs(nb, nb - 1) + [pl.BlockSpec((BLOCK, B_WIDTH), lambda n: (jnp.minimum(n, nb - 1), 0))],
        out_specs=[pl.BlockSpec((BLOCK, B_WIDTH), lambda n: (jnp.minimum(n, nb - 1), 0)),
                   pl.BlockSpec((BLOCK, 2 * B_KV_WIDTH), lambda n: (jnp.maximum(n - 1, 0), 0)),
                   pl.BlockSpec((8, BLOCK), lambda n: (0, 0))],
        out_shape=[jax.ShapeDtypeStruct((T, B_WIDTH), BF16), jax.ShapeDtypeStruct((T, 2 * B_KV_WIDTH), BF16),
                   jax.ShapeDtypeStruct((8, BLOCK), F32)],
        scratch_shapes=[pltpu.VMEM((BLOCK, B_KV_WIDTH), F32), pltpu.VMEM((BLOCK, B_KV_WIDTH), F32)],
        compiler_params=_params(("arbitrary",)),
    )(proj, proj, proj, proj, proj, q_gain, k_gain, sinks, do)


W_IN, W_A, W_B, W_OUT, W_MI, W_MO = range(6)


def _local_step(x, target, mod8, norm1_gain, norm2_gain, lb_logits, o_gain, q_gain, k_gain, sinks, parts, c_arr, chip_arr):
    relu2 = lambda u: (u, jnp.square(jnp.maximum(u, 0.0)))
    pair, half = {}, {}

    def exchange(ws, grads):
        return _sibling_exchange_plan([_grad_view(g, w) for w, g in zip(ws, grads)])

    def pair_sums(ws, grads, others):
        for w, g, o in zip(ws, grads, others):
            pair[w] = _pair_sum(_grad_view(g, w), o, c_arr, f"pair_sum{w}")

    def sum_slots(ws, slots):
        for w, s in zip(ws, slots):
            half[w] = _sum_slots(pair[w], s, w, chip_arr, f"sum_slots{w}")

    (w_in,) = _run_plan(_gather_plan({W_IN: parts[W_IN]}), "gather_w_in")
    h = _norm1_fwd(x, norm1_gain, mod8)
    proj, (w_a, w_b, w_out) = _mm(h, w_in, name="mm_proj", bn=512,
                                  plan=_gather_plan({w: parts[w] for w in (W_A, W_B, W_OUT)}))
    (o_a, states), (w_mi,) = _hgrn_fwd(proj, lb_logits, o_gain, plan=_gather_plan({W_MI: parts[W_MI]}, pass_at=0.9))
    o_b, (w_mo,) = _swa_fwd(proj, q_gain, k_gain, sinks, plan=_gather_plan({W_MO: parts[W_MO]}, pass_at=0.9))
    ya = _mm(o_a, w_a, name="mm_branch_a")
    yb = _mm(o_b, w_b, name="mm_branch_b")
    merged = _merge_fwd(proj, ya, yb)
    mo = _mm(merged, w_out, name="mm_out")
    x1, h2 = _res_norm2_fwd(x, mo, norm2_gain, mod8)
    u, act = _mm(h2, w_mi, name="mm_mlp_in", out_dtypes=(F32, BF16), epi=relu2)
    mlp = _mm(act, w_mo, name="mm_mlp_out")
    dy, dmlp, st_loss = _loss_bwd(x1, mlp, target, mod8)
    g_mo = _mm(act, dmlp, name="mm_g_mlp_out", ta=True)
    du, others = _mm(dmlp, w_mo, name="mm_d_act", tb=True, out_dtypes=(BF16,), extras=(u,),
                     epi=lambda acc, uu: (acc * (2.0 * jnp.maximum(uu, 0.0)),), plan=exchange([W_MO], [g_mo]))
    pair_sums([W_MO], [g_mo], others)
    g_mi, slots_mo = _mm(h2, du, name="mm_g_mlp_in", ta=True, plan=_chip_exchange_plan({W_MO: pair[W_MO]}))
    dh2, others = _mm(du, w_mi, name="mm_d_h2", tb=True, plan=exchange([W_MI], [g_mi]))
    pair_sums([W_MI], [g_mi], others)
    sum_slots([W_MO], slots_mo)
    dx1, dmo, st_n2 = _norm2_bwd(dh2, x1, dy, mo, norm2_gain, mod8)
    dmerged = _mm(dmo, w_out, name="mm_d_merged", tb=True)
    g_out = _mm(merged, dmo, name="mm_g_out", ta=True)
    dya, dyb, dga, dgb = _merge_bwd(proj, ya, yb, dmerged)
    do_a = _mm(dya, w_a, name="mm_d_oa", tb=True)
    g_a = _mm(o_a, dya, name="mm_g_branch_a", ta=True)
    do_b = _mm(dyb, w_b, name="mm_d_ob", tb=True)
    g_b = _mm(o_b, dyb, name="mm_g_branch_b", ta=True)
    mid = [W_A, W_B, W_OUT]
    (dqa, dfa, dia, dgga, d_lb, d_og), res = _hgrn_bwd(
        proj, lb_logits, o_gain, states, do_a,
        plan=_join(_chip_exchange_plan({W_MI: pair[W_MI]}), exchange(mid, [g_a, g_b, g_out])))
    sum_slots([W_MI], res[:1])
    pair_sums(mid, [g_a, g_b, g_out], res[1:])
    (dqb, dkvb, st_swa), slots_mid = _swa_bwd(proj, q_gain, k_gain, sinks, do_b,
                                              plan=_chip_exchange_plan({w: pair[w] for w in mid}))
    sum_slots(mid, slots_mid)
    dproj = jnp.concatenate([dqa, dfa, dia, dgga, dqb, dkvb, dga, dgb], axis=1)
    g_in = _mm(h, dproj, name="mm_g_in", ta=True, bn=512)
    done = [W_A, W_B, W_OUT, W_MI, W_MO]
    dh, res = _mm(dproj, w_in, name="mm_d_h", tb=True, bk=2432,
                  plan=_join(exchange([W_IN], [g_in]), _sibling_share_plan([half[w] for w in done])))
    pair_sums([W_IN], [g_in], res[:1])
    theirs = dict(zip(done, res[1:]))
    grad_x, st_n1 = _norm1_bwd(dh, x, dx1, norm1_gain, mod8)
    sum_slots([W_IN], _run_plan(_chip_exchange_plan({W_IN: pair[W_IN]}), "chip_exchange_w_in"))
    (theirs[W_IN],) = _run_plan(_sibling_share_plan([half[W_IN]]), "sibling_share_w_in")
    stats = dict(loss=st_loss, n2=st_n2, n1=st_n1, d_lb=d_lb, d_og=d_og, swa=st_swa)
    return grad_x, [half[w] for w in range(N_W)], [theirs[w] for w in range(N_W)], stats


def _ew_rows(rows, cols):
    br = 8
    while br * 2 <= rows and br * 2 * cols * 4 <= (1 << 20) and rows % (br * 2) == 0:
        br *= 2
    return br


def _cast_into_full(shard, w, chip_arr, name):
    sr, sc = shard.shape
    R, C, by_col = W_SHAPES[w]
    br = _ew_rows(sr, sc)
    nb = sr // br
    out_map = (lambda i, chip: (i, chip[0])) if by_col else (lambda i, chip: (chip[0] * nb + i, 0))

    def body(chip_ref, w_ref, o_ref):
        o_ref[...] = w_ref[...].astype(BF16)

    return _pcall(
        body, name=name,
        grid_spec=pltpu.PrefetchScalarGridSpec(
            num_scalar_prefetch=1, grid=(nb,),
            in_specs=[pl.BlockSpec((br, sc), lambda i, chip: (i, 0))],
            out_specs=pl.BlockSpec((br, sc), out_map)),
        out_shape=jax.ShapeDtypeStruct((R, C), BF16), compiler_params=_params(("parallel",)))(chip_arr, shard)


def _adamw_math(w, g, m, v):
    m = ADAM_B1 * m + (1.0 - ADAM_B1) * g
    v = ADAM_B2 * v + (1.0 - ADAM_B2) * (g * g)
    m_hat = m / (1.0 - ADAM_B1 ** ADAM_STEP)
    v_hat = v / (1.0 - ADAM_B2 ** ADAM_STEP)
    delta = -ADAM_LR * (m_hat / (jnp.sqrt(v_hat) + ADAM_EPS) + ADAM_WD * w)
    return delta, m, v


def _adamw(w, g, m, v, name):
    R, C = w.shape
    br = _ew_rows(R, C)
    spec = pl.BlockSpec((br, C), lambda i: (i, 0))

    def body(w_ref, g_ref, m_ref, v_ref, d_ref, nm_ref, nv_ref):
        d_ref[...], nm_ref[...], nv_ref[...] = _adamw_math(w_ref[...], g_ref[...], m_ref[...], v_ref[...])

    sh = jax.ShapeDtypeStruct((R, C), F32)
    return _pcall(body, name=name, grid=(R // br,), in_specs=[spec] * 4, out_specs=[spec] * 3, out_shape=[sh] * 3,
                  compiler_params=_params(("parallel",)))(w, g, m, v)


def _adamw_halves(w, own, other, m, v, c_arr, name):
    R, C = w.shape
    hr = R // 2
    br = _ew_rows(hr, C)
    nb = hr // br
    full = pl.BlockSpec((br, C), lambda h, i, c_ref: (h * nb + i, 0))
    half = pl.BlockSpec((br, C), lambda h, i, c_ref: (i, 0))

    def body(c_ref, w_ref, own_ref, oth_ref, m_ref, v_ref, g_ref, d_ref, nm_ref, nv_ref):
        g = jnp.where(pl.program_id(0) == c_ref[0], own_ref[...], oth_ref[...])
        g_ref[...] = g
        d_ref[...], nm_ref[...], nv_ref[...] = _adamw_math(w_ref[...], g, m_ref[...], v_ref[...])

    sh = jax.ShapeDtypeStruct((R, C), F32)
    return _pcall(
        body, name=name,
        grid_spec=pltpu.PrefetchScalarGridSpec(
            num_scalar_prefetch=1, grid=(2, nb), in_specs=[full, half, half, full, full], out_specs=[full] * 4),
        out_shape=[sh] * 4, compiler_params=_params(("parallel", "parallel")))(c_arr, w, own, other, m, v)


def _ada_grad_adamw(c_t, dmod, w, m, v):
    R, C = w.shape
    br = _ew_rows(R, C)
    spec = pl.BlockSpec((br, C), lambda i: (i, 0))

    def body(c_ref, dm_ref, w_ref, m_ref, v_ref, g_ref, d_ref, nm_ref, nv_ref):
        cv = c_ref[...]
        sc = cv * _sig(cv)
        g = sc[:, 0:1] * dm_ref[0:1, :]
        for b in range(1, N_DEV):
            g = g + sc[:, b:b + 1] * dm_ref[b:b + 1, :]
        g_ref[...] = g
        d_ref[...], nm_ref[...], nv_ref[...] = _adamw_math(w_ref[...], g, m_ref[...], v_ref[...])

    sh = jax.ShapeDtypeStruct((R, C), F32)
    return _pcall(
        body, name="ada_grad_adamw", grid=(R // br,),
        in_specs=[pl.BlockSpec((br, N_DEV), lambda i: (i, 0)), pl.BlockSpec((N_DEV, C), lambda i: (0, 0)), spec, spec, spec],
        out_specs=[spec] * 4, out_shape=[sh] * 4, compiler_params=_params(("parallel",)))(c_t, dmod, w, m, v)


SMALL_ROWS = 16


def _small_sum(small_all, lb_logits):
    def body(s_ref, lbl_ref, o_ref):
        acc = s_ref[0:SMALL_ROWS, :]
        for d in range(1, N_DEV):
            acc = acc + s_ref[d * SMALL_ROWS:(d + 1) * SMALL_ROWS, :]
        o_ref[...] = acc
        z = lbl_ref[...]
        e = jnp.exp(z - jnp.max(z, axis=0, keepdims=True))
        p0 = e[0:1, :] / (e[0:1, :] + e[1:2, :])
        dz = acc[8:9, 0:A_WIDTH] * p0 * (1.0 - p0)
        o_ref[8:9, 0:A_WIDTH] = dz
        o_ref[10:11, 0:A_WIDTH] = -dz

    return _pcall(body, name="small_sum", out_shape=jax.ShapeDtypeStruct((SMALL_ROWS, D_MODEL), F32),
                  in_specs=[pl.BlockSpec(memory_space=pltpu.VMEM)] * 2, out_specs=pl.BlockSpec(memory_space=pltpu.VMEM),
                  compiler_params=_params())(small_all, lb_logits)


RELATIONS = ((1, 0), (0, 1), (1, 1))
ANY = pl.BlockSpec(memory_space=pl.ANY)


def _place():
    x, y, c = lax.axis_index("x"), lax.axis_index("y"), lax.axis_index("c")
    return x, y, c


def _allgather_small(x_shard, name):
    m_per, n = x_shard.shape

    def body(x_ref, out_ref, send_sems, recv_sems, local_sem):
        x, y, c = _place()
        me, sibling = (x, y, c), (x, y, 1 - c)
        chips = [(1 - x, y), (x, 1 - y), (1 - x, 1 - y)]

        def rows(px, py, pc):
            return out_ref.at[pl.ds((4 * px + 2 * py + pc) * m_per, m_per), :]

        def copy(k, block, to, src=None):
            return pltpu.make_async_remote_copy(
                src_ref=rows(*block) if src is None else src, dst_ref=rows(*block),
                send_sem=send_sems.at[k], recv_sem=recv_sems.at[k], device_id=to, device_id_type=MESH)

        mine = pltpu.make_async_copy(x_ref, rows(*me), local_sem)
        mine.start()
        first = [copy(0, me, sibling, src=x_ref)]
        first += [copy(1 + j, me, (*chip, c), src=x_ref) for j, chip in enumerate(chips)]
        for cp in first:
            cp.start()
        passed = [copy(4 + j, (*chip, c), sibling) for j, chip in enumerate(chips)]
        for j, chip in enumerate(chips):
            copy(1 + j, (*chip, c), me).wait_recv()
            passed[j].start()
        copy(0, sibling, me).wait_recv()
        for j, chip in enumerate(chips):
            copy(4 + j, (*chip, 1 - c), me).wait_recv()
        for cp in first + passed:
            cp.wait_send()
        mine.wait()

    return _pcall(
        body, name=name, out_shape=jax.ShapeDtypeStruct((N_DEV * m_per, n), x_shard.dtype),
        in_specs=[pl.BlockSpec(memory_space=pltpu.VMEM)], out_specs=pl.BlockSpec(memory_space=pltpu.VMEM),
        scratch_shapes=[pltpu.SemaphoreType.DMA((7,)), pltpu.SemaphoreType.DMA((7,)), pltpu.SemaphoreType.DMA],
        compiler_params=_params(),
    )(x_shard)


W_SHAPES = ((D_MODEL, IN_WIDTH, True), (A_WIDTH, D_MODEL, True), (B_WIDTH, D_MODEL, True),
            (D_MODEL, D_MODEL, False), (D_MODEL, MLP_HIDDEN, True), (MLP_HIDDEN, D_MODEL, False))
N_W = len(W_SHAPES)


def _shard_shape(w):
    R, C, by_col = W_SHAPES[w]
    return (R, C // N_CHIPS) if by_col else (R // N_CHIPS, C)


def _half_shape(w):
    sr, sc = _shard_shape(w)
    return sr // 2, sc


def _region(full_ref, w, chip, half):
    sr, sc = _shard_shape(w)
    by_col = W_SHAPES[w][2]
    r0, c0 = (0, chip * sc) if by_col else (chip * sr, 0)
    if half is None:
        return full_ref.at[pl.ds(r0, sr), pl.ds(c0, sc)]
    return full_ref.at[pl.ds(r0 + half * (sr // 2), sr // 2), pl.ds(c0, sc)]


def _on_device(fn):
    x, y, c = _place()
    me = 4 * x + 2 * y + c
    for d in range(N_DEV):
        @pl.when(me == d)
        def _(d=d):
            fn(x, y, c, d)


def _gather_plan(partials, pass_at=0.5):
    ws = sorted(partials)
    pairs = [(i, w, k) for i, w in enumerate(ws) for k in range(3)]

    def first(pi, po, ps, x, y, c, d, i, w, k):
        chip, dc = d >> 1, d & 1
        rx, ry = RELATIONS[k]
        return pltpu.make_async_remote_copy(
            src_ref=_region(pi[i], w, chip, dc), dst_ref=_region(po[i], w, chip, dc),
            send_sem=ps[0].at[i * 3 + k], recv_sem=ps[1].at[i * 3 + k],
            device_id=(x ^ rx, y ^ ry, c), device_id_type=MESH)

    def landed(po, ps, x, y, c, d, i, w, k, half, to_sibling):
        rx, ry = RELATIONS[k]
        got = _region(po[i], w, (d >> 1) ^ (2 * rx + ry), half)
        s = 2 if to_sibling else 0
        return pltpu.make_async_remote_copy(
            src_ref=got, dst_ref=got, send_sem=ps[s].at[i * 3 + k], recv_sem=ps[s + 1].at[i * 3 + k],
            device_id=(x, y, 1 - c), device_id_type=MESH)

    def send(pi, po, ps):
        def run(x, y, c, d):
            for i, w, k in pairs:
                first(pi, po, ps, x, y, c, d, i, w, k).start()
        _on_device(run)

    def pass_on(pi, po, ps):
        def run(x, y, c, d):
            for i, w, k in pairs:
                landed(po, ps, x, y, c, d, i, w, k, d & 1, False).wait_recv()
                landed(po, ps, x, y, c, d, i, w, k, d & 1, True).start()
        _on_device(run)

    def finish(pi, po, ps):
        def run(x, y, c, d):
            for i, w, k in pairs:
                landed(po, ps, x, y, c, d, i, w, k, 1 - (d & 1), True).wait_recv()
            for i, w, k in pairs:
                first(pi, po, ps, x, y, c, d, i, w, k).wait_send()
                landed(po, ps, x, y, c, d, i, w, k, d & 1, True).wait_send()
        _on_device(run)

    return _Plan([partials[w] for w in ws], [jax.ShapeDtypeStruct(W_SHAPES[w][:2], BF16) for w in ws],
                 [pltpu.SemaphoreType.DMA((3 * len(ws),)) for _ in range(4)], [send, pass_on, finish],
                 {i: i for i in range(len(ws))}, mid_at=(pass_at,))


def _grad_view(g, w):
    R, C, by_col = W_SHAPES[w]
    return g.reshape(1, 2, R // 2, C) if by_col else g.reshape(N_CHIPS, 2, R // N_CHIPS // 2, C)


def _start_wait_plan(ins, outs, n_copies, copies):
    def start(pi, po, ps):
        for cp in copies(pi, po, ps):
            cp.start()

    def finish(pi, po, ps):
        for cp in copies(pi, po, ps):
            cp.wait()

    return _Plan(ins, outs, [pltpu.SemaphoreType.DMA((n_copies,)), pltpu.SemaphoreType.DMA((n_copies,))], [start, finish])


def _sibling_exchange_plan(g4s):
    pieces = [(i, p) for i, g in enumerate(g4s) for p in range(g.shape[0])]

    def copies(pi, po, ps):
        x, y, c = _place()
        return [pltpu.make_async_remote_copy(
            src_ref=pi[i].at[p, 1 - c], dst_ref=po[i].at[p], send_sem=ps[0].at[n], recv_sem=ps[1].at[n],
            device_id=(x, y, 1 - c), device_id_type=MESH) for n, (i, p) in enumerate(pieces)]

    return _start_wait_plan(list(g4s), [jax.ShapeDtypeStruct((g.shape[0],) + g.shape[2:], F32) for g in g4s],
                            len(pieces), copies)


def _pair_sum(g4, other, c_arr, name):
    P, _, hr, C = g4.shape
    br = _ew_rows(hr, C)

    def body(c_ref, g_ref, o_ref, p_ref):
        p_ref[...] = (g_ref[...] + o_ref[...]).astype(BF16)

    return _pcall(
        body, name=name,
        grid_spec=pltpu.PrefetchScalarGridSpec(
            num_scalar_prefetch=1, grid=(P, hr // br),
            in_specs=[pl.BlockSpec((None, None, br, C), lambda p, i, c_ref: (p, c_ref[0], i, 0)),
                      pl.BlockSpec((None, br, C), lambda p, i, c_ref: (p, i, 0))],
            out_specs=pl.BlockSpec((None, br, C), lambda p, i, c_ref: (p, i, 0))),
        out_shape=jax.ShapeDtypeStruct((P, hr, C), BF16),
        compiler_params=_params(("parallel", "parallel")),
    )(c_arr, g4, other)


def _pair_part(p_ref, w, chip):
    sr, sc = _shard_shape(w)
    return p_ref.at[0, :, pl.ds(chip * sc, sc)] if W_SHAPES[w][2] else p_ref.at[chip]


def _chip_exchange_plan(pairs):
    ws = sorted(pairs)

    def stage(wait):
        def run(pi, po, ps):
            def on(x, y, c, d):
                for i, w in enumerate(ws):
                    for k, (rx, ry) in enumerate(RELATIONS):
                        cp = pltpu.make_async_remote_copy(
                            src_ref=_pair_part(pi[i], w, (d >> 1) ^ (2 * rx + ry)), dst_ref=po[i].at[k],
                            send_sem=ps[0].at[i * 3 + k], recv_sem=ps[1].at[i * 3 + k],
                            device_id=(x ^ rx, y ^ ry, c), device_id_type=MESH)
                        if wait:
                            cp.wait()
                        else:
                            cp.start()
            _on_device(on)
        return run

    return _Plan([pairs[w] for w in ws], [jax.ShapeDtypeStruct((3,) + _half_shape(w), BF16) for w in ws],
                 [pltpu.SemaphoreType.DMA((3 * len(ws),)), pltpu.SemaphoreType.DMA((3 * len(ws),))],
                 [stage(False), stage(True)])


def _sum_slots(pair, slots, w, chip_arr, name):
    _, hr, C = slots.shape
    br = _ew_rows(hr, C)
    own_map = (lambda i, chip: (0, i, chip[0])) if W_SHAPES[w][2] else (lambda i, chip: (chip[0], i, 0))

    def body(chip_ref, p_ref, s_ref, o_ref):
        acc = p_ref[...].astype(F32)
        for k in range(3):
            acc = acc + s_ref[k].astype(F32)
        o_ref[...] = acc

    return _pcall(
        body, name=name,
        grid_spec=pltpu.PrefetchScalarGridSpec(
            num_scalar_prefetch=1, grid=(hr // br,),
            in_specs=[pl.BlockSpec((None, br, C), own_map), pl.BlockSpec((3, br, C), lambda i, chip: (0, i, 0))],
            out_specs=pl.BlockSpec((br, C), lambda i, chip: (i, 0))),
        out_shape=jax.ShapeDtypeStruct((hr, C), F32), compiler_params=_params(("parallel",)),
    )(chip_arr, pair, slots)


def _sibling_share_plan(halves):
    def copies(pi, po, ps):
        x, y, c = _place()
        return [pltpu.make_async_remote_copy(
            src_ref=pi[i], dst_ref=po[i], send_sem=ps[0].at[i], recv_sem=ps[1].at[i],
            device_id=(x, y, 1 - c), device_id_type=MESH) for i in range(len(halves))]

    return _start_wait_plan(list(halves), [jax.ShapeDtypeStruct(h.shape, F32) for h in halves], len(halves), copies)


def _pad_lanes(v, width=D_MODEL):
    return jnp.pad(v, ((0, 0), (0, width - v.shape[1])))


def _pack_small(b_ada, norm1, norm2, lb, o_gain, q_gain, k_gain, sinks):
    rows = [b_ada.reshape(N_MOD, D_MODEL), norm1, norm2, jnp.concatenate([lb[0:1], o_gain], axis=1),
            _pad_lanes(jnp.concatenate([q_gain, k_gain, sinks], axis=1)), _pad_lanes(lb[1:2]),
            jnp.zeros((SMALL_ROWS - 11, D_MODEL), F32)]
    return jnp.concatenate(rows, axis=0)


def _unpack_small(p):
    return (p[0:6].reshape(1, N_MOD * D_MODEL), p[6:7], p[7:8],
            jnp.concatenate([p[8:9, 0:A_WIDTH], p[10:11, 0:A_WIDTH]], axis=0), p[8:9, A_WIDTH:],
            p[9:10, 0:64], p[9:10, 64:128], p[9:10, 128:144])


def kernel(x, c, w_ada, b_ada, norm1_gain, w_in, lb_logits, hgrn_o_gain, q_norm_gain, k_norm_gain, sinks, w_branch_a, w_branch_b, w_out, norm2_gain, w_mlp_in, w_mlp_out, loss_target, m_w_ada, m_b_ada, m_norm1_gain, m_w_in, m_lb_logits, m_hgrn_o_gain, m_q_norm_gain, m_k_norm_gain, m_sinks, m_w_branch_a, m_w_branch_b, m_w_out, m_norm2_gain, m_w_mlp_in, m_w_mlp_out, v_w_ada, v_b_ada, v_norm1_gain, v_w_in, v_lb_logits, v_hgrn_o_gain, v_q_norm_gain, v_k_norm_gain, v_sinks, v_w_branch_a, v_w_branch_b, v_w_out, v_norm2_gain, v_w_mlp_in, v_w_mlp_out):
    xi, yi, ci = _place()
    chip = 2 * xi + yi
    me = 4 * xi + 2 * yi + ci
    ada_cols = w_ada.shape[2]

    c_all = _allgather_small(jnp.broadcast_to(c, (8, D_MODEL)), "gather_c").reshape(N_DEV, 8, D_MODEL)[:, 0]
    b_cols = lax.dynamic_slice(b_ada, (0, chip * ada_cols), (1, ada_cols))
    mod_part = _ada_fwd(c_all, w_ada[0], b_cols)
    mod_all = _allgather_small(mod_part, "gather_mod").reshape(N_CHIPS, 2, N_DEV, ada_cols)[:, 0]
    mod_mine = lax.dynamic_index_in_dim(mod_all, me, axis=1, keepdims=False).reshape(N_MOD, D_MODEL)
    mod8 = jnp.concatenate([mod_mine, jnp.zeros((2, D_MODEL), F32)], axis=0)

    shards = (w_in[0], w_branch_a[0], w_branch_b[0], w_out[0], w_mlp_in[0], w_mlp_out[0])
    chip_arr = chip.astype(jnp.int32).reshape(1)
    c_arr = ci.astype(jnp.int32).reshape(1)
    parts = [_cast_into_full(s, w, chip_arr, f"cast_w{w}") for w, s in enumerate(shards)]

    grad_x, halves, theirs, st = _local_step(x[0], loss_target[0], mod8, norm1_gain, norm2_gain, lb_logits, hgrn_o_gain,
                                             q_norm_gain, k_norm_gain, sinks, parts, c_arr, chip_arr)
    loss = lax.psum(0.5 * jnp.sum(st["loss"][0]) / D_MODEL, ("x", "y", "c"))
    moments =((m_w_in, v_w_in), (m_w_branch_a, v_w_branch_a), (m_w_branch_b, v_w_branch_b), (m_w_out, v_w_out),
               (m_w_mlp_in, v_w_mlp_in), (m_w_mlp_out, v_w_mlp_out))
    big = [_adamw_halves(shards[w], halves[w], theirs[w], moments[w][0][0], moments[w][1][0], c_arr, f"adamw{w}")
           for w in range(N_W)]

    swa = st["swa"]
    small = jnp.concatenate([
        st["n1"][1:2], st["n1"][0:1], st["n2"][3:4], st["n2"][1:2], st["n2"][0:1], st["loss"][1:2],
        st["n1"][2:3], st["n2"][2:3], jnp.concatenate([st["d_lb"][0:1], st["d_og"][0:1]], axis=1),
        _pad_lanes(jnp.concatenate([swa[0:1, 0:64], swa[1:2, 0:64], swa[2:3, 0:16]], axis=1)),
        jnp.zeros((SMALL_ROWS - 10, D_MODEL), F32)], axis=0)
    small_all = _allgather_small(small, "gather_small")
    g_small = _small_sum(small_all, lb_logits)
    small_w = (b_ada, norm1_gain, norm2_gain, lb_logits, hgrn_o_gain, q_norm_gain, k_norm_gain, sinks)
    small_m = (m_b_ada, m_norm1_gain, m_norm2_gain, m_lb_logits, m_hgrn_o_gain, m_q_norm_gain, m_k_norm_gain, m_sinks)
    small_v = (v_b_ada, v_norm1_gain, v_norm2_gain, v_lb_logits, v_hgrn_o_gain, v_q_norm_gain, v_k_norm_gain, v_sinks)
    sm = [_unpack_small(t) for t in
          (g_small,) + tuple(_adamw(_pack_small(*small_w), g_small, _pack_small(*small_m), _pack_small(*small_v),
                                    "adamw_small"))]
    g_b, g_n1, g_n2, g_lb, g_og, g_qg, g_kg, g_sk = ([t[i] for t in sm] for i in range(8))

    dmod_all = small_all.reshape(N_DEV, SMALL_ROWS, D_MODEL)[:, 0:N_MOD].reshape(N_DEV, N_MOD * D_MODEL)
    dmod_cols = lax.dynamic_slice(dmod_all, (0, chip * ada_cols), (N_DEV, ada_cols))
    ada = _ada_grad_adamw(c_all.T, dmod_cols, w_ada[0], m_w_ada[0], v_w_ada[0])

    def ordered(k):
        lead = lambda a: a[None]
        return (lead(ada[k]), g_b[k], g_n1[k], lead(big[0][k]), g_lb[k], g_og[k], g_qg[k], g_kg[k], g_sk[k],
                lead(big[1][k]), lead(big[2][k]), lead(big[3][k]), g_n2[k], lead(big[4][k]), lead(big[5][k]))

    return (loss, grad_x[None]) + ordered(0) + ordered(1) + ordered(2) + ordered(3)
```

```python
import functools

import jax
import jax.numpy as jnp
from jax import lax
from jax.experimental import pallas as pl
from jax.experimental.pallas import tpu as pltpu

F32 = jnp.float32
BF16 = jnp.bfloat16
HIGHEST = lax.Precision.HIGHEST
MESH = pl.DeviceIdType.MESH

D_MODEL = 2048
A_WIDTH = 1024
A_HEADS = 8
A_HEAD_DIM = 128
A_CHUNK = 64
B_WIDTH = 1024
B_HEAD_DIM = 64
B_GROUP = 4
B_KV_HEADS = 4
B_KV_WIDTH = 256
BLOCK = 128
MLP_HIDDEN = 8192
IN_WIDTH = 9728
N_MOD = 6
EPS = 1e-6
N_CHIPS = 4
N_DEV = 8

OFF_QA, OFF_FA, OFF_IA, OFF_GA = 0, 1024, 2048, 3072
OFF_QB, OFF_KB, OFF_VB = 4096, 5120, 5376
OFF_GATE_A, OFF_GATE_B = 5632, 7680

ADAM_LR = 0.001
ADAM_B1 = 0.9
ADAM_B2 = 0.999
ADAM_EPS = 1e-08
ADAM_WD = 0.01
ADAM_STEP = 10

VMEM_LIMIT_V7X = 48 * 1024 * 1024
NEG_BIG = -1e30


def _params(sem=None, vmem=VMEM_LIMIT_V7X):
    return pltpu.CompilerParams(dimension_semantics=sem, vmem_limit_bytes=vmem)


class _Plan:
    def __init__(self, ins, outs, sems, stages, aliases=None, mid_at=()):
        self.ins, self.outs, self.sems, self.stages, self.aliases = ins, outs, sems, stages, aliases or {}
        self.mid_at = tuple(mid_at)
        assert len(self.mid_at) == len(stages) - 2


def _join(a, b):
    assert len(a.stages) == 2 and len(b.stages) == 2
    ni, no, ns = len(a.ins), len(a.outs), len(a.sems)

    def stage(k):
        def run(pi, po, ps):
            a.stages[k](pi[:ni], po[:no], ps[:ns])
            b.stages[k](pi[ni:], po[no:], ps[ns:])
        return run

    aliases = dict(a.aliases)
    aliases.update({ni + i: no + o for i, o in b.aliases.items()})
    return _Plan(a.ins + b.ins, a.outs + b.outs, a.sems + b.sems, [stage(0), stage(1)], aliases)


def _pcall(body, plan=None, **kw):
    if plan is None:
        return pl.pallas_call(body, **kw)
    grid = kw["grid"]
    single = not isinstance(kw["out_specs"], (list, tuple))
    in_specs = list(kw["in_specs"])
    out_specs = [kw["out_specs"]] if single else list(kw["out_specs"])
    out_shape = [kw["out_shape"]] if single else list(kw["out_shape"])
    scratch = list(kw.get("scratch_shapes", ()))
    n_in, n_out, n_scr = len(in_specs), len(out_specs), len(scratch)
    n_pi, n_po = len(plan.ins), len(plan.outs)
    total = 1
    for g in grid:
        total *= g
    n_st = len(plan.stages)

    def wrapped(*refs):
        o0 = n_in + n_pi
        s0 = o0 + n_out + n_po
        pi, po, ps = refs[n_in:o0], refs[o0 + n_out:s0], refs[s0 + n_scr:]
        lin = 0
        for d, g in enumerate(grid):
            lin = lin * g + pl.program_id(d)
        for si, frac in enumerate((0.0,) + plan.mid_at):
            @pl.when(lin == int(frac * (total - 1)))
            def _(si=si):
                plan.stages[si](pi, po, ps)
        body(*refs[:n_in], *refs[o0:o0 + n_out], *refs[s0:s0 + n_scr])

        @pl.when(lin == total - 1)
        def _():
            plan.stages[-1](pi, po, ps)

    any_spec = pl.BlockSpec(memory_space=pl.ANY)
    call = pl.pallas_call(
        wrapped, name=kw["name"], grid=grid, in_specs=in_specs + [any_spec] * n_pi,
        out_specs=out_specs + [any_spec] * n_po, out_shape=out_shape + list(plan.outs),
        scratch_shapes=scratch + list(plan.sems),
        input_output_aliases={n_in + i: n_out + o for i, o in plan.aliases.items()},
        compiler_params=_params(("arbitrary",) * len(grid)))

    def run(*args):
        res = call(*args, *plan.ins)
        outs = list(res[:n_out])
        return (outs[0] if single else outs), list(res[n_out:])

    return run


def _run_plan(plan, name):
    return _pcall(lambda: None, plan=plan, name=name, grid=(1,), in_specs=[], out_specs=[], out_shape=[])()[1]


def _sig(x):
    return 1.0 / (1.0 + jnp.exp(-x))


def _nn(a, b):
    return lax.dot_general(a.astype(BF16), b.astype(BF16), (((1,), (0,)), ((), ())), preferred_element_type=F32)


def _nt(a, b):
    return lax.dot_general(a.astype(BF16), b.astype(BF16), (((1,), (1,)), ((), ())), preferred_element_type=F32)


def _tn(a, b):
    return lax.dot_general(a.astype(BF16), b.astype(BF16), (((0,), (0,)), ((), ())), preferred_element_type=F32)


def _mm(a, b, *, name, ta=False, tb=False, bm=1024, bn=1024, bk=2048, out_dtypes=(F32,), epi=None, extras=(), plan=None):
    if ta:
        K, M = a.shape
    else:
        M, K = a.shape
    if tb:
        N, K2 = b.shape
    else:
        K2, N = b.shape
    bm, bn, bk = min(bm, M), min(bn, N), min(bk, K)
    assert K == K2 and M % bm == 0 and N % bn == 0 and K % bk == 0, (name, a.shape, b.shape)
    nk = K // bk
    a_spec = pl.BlockSpec((bk, bm), lambda i, j, k: (k, i)) if ta else pl.BlockSpec((bm, bk), lambda i, j, k: (i, k))
    b_spec = pl.BlockSpec((bn, bk), lambda i, j, k: (j, k)) if tb else pl.BlockSpec((bk, bn), lambda i, j, k: (k, j))
    t_spec = pl.BlockSpec((bm, bn), lambda i, j, k: (i, j))
    dims = (((0 if ta else 1,), (1 if tb else 0,)), ((), ()))
    n_e, n_o = len(extras), len(out_dtypes)

    def body(*refs):
        a_ref, b_ref = refs[0], refs[1]
        e_refs = refs[2:2 + n_e]
        o_refs = refs[2 + n_e:2 + n_e + n_o]

        def finish(acc):
            outs = (acc,) if epi is None else epi(acc, *[e[...] for e in e_refs])
            for o_ref, o in zip(o_refs, outs):
                o_ref[...] = o.astype(o_ref.dtype)

        part = lax.dot_general(a_ref[...].astype(BF16), b_ref[...].astype(BF16), dims, preferred_element_type=F32)
        if nk == 1:
            finish(part)
        else:
            acc_ref = refs[-1]
            k = pl.program_id(2)

            @pl.when(k == 0)
            def _():
                acc_ref[...] = part

            @pl.when(k > 0)
            def _():
                acc_ref[...] += part

            @pl.when(k == nk - 1)
            def _():
                finish(acc_ref[...])

    out = _pcall(
        body, plan=plan, name=name, grid=(M // bm, N // bn, nk),
        in_specs=[a_spec, b_spec] + [t_spec] * n_e,
        out_specs=[t_spec] * n_o,
        out_shape=[jax.ShapeDtypeStruct((M, N), dt) for dt in out_dtypes],
        scratch_shapes=[pltpu.VMEM((bm, bn), F32)] if nk > 1 else [],
        compiler_params=_params(("parallel", "parallel", "arbitrary")),
    )(a, b, *extras)
    if plan is not None:
        return (out[0][0] if n_o == 1 else out[0]), out[1]
    return out[0] if n_o == 1 else out


def _ada_fwd(c_all, w_ada, b_cols):
    n = w_ada.shape[1]
    bn = 512

    def body(c_ref, w_ref, b_ref, o_ref):
        cv = c_ref[...]
        sc = cv * _sig(cv)
        o_ref[...] = jnp.dot(sc, w_ref[...], precision=HIGHEST, preferred_element_type=F32) + b_ref[...]

    return _pcall(
        body, name="ada_fwd", grid=(n // bn,),
        in_specs=[pl.BlockSpec((N_DEV, D_MODEL), lambda j: (0, 0)), pl.BlockSpec((D_MODEL, bn), lambda j: (0, j)),
                  pl.BlockSpec((1, bn), lambda j: (0, j))],
        out_specs=pl.BlockSpec((N_DEV, bn), lambda j: (0, j)),
        out_shape=jax.ShapeDtypeStruct((N_DEV, n), F32),
        compiler_params=_params(("parallel",)),
    )(c_all, w_ada, b_cols)


ROWS_EW = 256


def _rms_fwd_math(x, gain, scale, shift):
    rstd = lax.rsqrt(jnp.mean(x * x, axis=-1, keepdims=True) + EPS)
    xhat = x * rstd
    n = xhat * gain
    return n * (1.0 + scale) + shift, xhat, n, rstd


def _rms_bwd_math(dh, xhat, n, rstd, gain, scale):
    dn = dh * (1.0 + scale)
    dxhat = dn * gain
    dx = rstd * (dxhat - xhat * jnp.mean(dxhat * xhat, axis=-1, keepdims=True))
    d_scale = jnp.sum(dh * n, axis=0, keepdims=True)
    d_shift = jnp.sum(dh, axis=0, keepdims=True)
    d_gain = jnp.sum(dn * xhat, axis=0, keepdims=True)
    return dx, d_scale, d_shift, d_gain


def _row_spec(w=D_MODEL, br=ROWS_EW):
    return pl.BlockSpec((br, w), lambda i: (i, 0))


def _vec_spec(r=8, w=D_MODEL):
    return pl.BlockSpec((r, w), lambda i: (0, 0))


def _norm1_fwd(x, gain, mod8, plan=None):
    T = x.shape[0]

    def body(x_ref, g_ref, m_ref, h_ref):
        h, _, _, _ = _rms_fwd_math(x_ref[...], g_ref[...], m_ref[1:2, :], m_ref[0:1, :])
        h_ref[...] = h.astype(BF16)

    return _pcall(
        body, plan=plan, name="norm1_fwd", grid=(T // ROWS_EW,),
        in_specs=[_row_spec(), _vec_spec(1), _vec_spec()],
        out_specs=_row_spec(), out_shape=jax.ShapeDtypeStruct((T, D_MODEL), BF16),
        compiler_params=_params(("parallel",)),
    )(x, gain, mod8)


def _res_norm2_fwd(x, mo, gain, mod8):
    T = x.shape[0]

    def body(x_ref, mo_ref, g_ref, m_ref, x1_ref, h_ref):
        x1 = x_ref[...] + m_ref[2:3, :] * mo_ref[...]
        x1_ref[...] = x1
        h, _, _, _ = _rms_fwd_math(x1, g_ref[...], m_ref[4:5, :], m_ref[3:4, :])
        h_ref[...] = h.astype(BF16)

    return _pcall(
        body, name="res_norm2_fwd", grid=(T // ROWS_EW,),
        in_specs=[_row_spec(), _row_spec(), _vec_spec(1), _vec_spec()],
        out_specs=[_row_spec(), _row_spec()],
        out_shape=[jax.ShapeDtypeStruct((T, D_MODEL), F32), jax.ShapeDtypeStruct((T, D_MODEL), BF16)],
        compiler_params=_params(("parallel",)),
    )(x, mo, gain, mod8)


def _loss_bwd(x1, mlp, target, mod8):
    T = x1.shape[0]

    def body(x1_ref, mlp_ref, t_ref, m_ref, dy_ref, dmlp_ref, st_ref):
        i = pl.program_id(0)
        gate = m_ref[5:6, :]
        mlp_v = mlp_ref[...]
        err = x1_ref[...] + gate * mlp_v - t_ref[...]
        dy = err * (1.0 / D_MODEL)
        dy_ref[...] = dy
        dmlp_ref[...] = (dy * gate).astype(BF16)

        @pl.when(i == 0)
        def _():
            st_ref[...] = jnp.zeros_like(st_ref)

        st_ref[0:1, :] += jnp.sum(err * err, axis=0, keepdims=True)
        st_ref[1:2, :] += jnp.sum(dy * mlp_v, axis=0, keepdims=True)

    return _pcall(
        body, name="loss_bwd", grid=(T // ROWS_EW,),
        in_specs=[_row_spec(), _row_spec(), _row_spec(), _vec_spec()],
        out_specs=[_row_spec(), _row_spec(), _vec_spec()],
        out_shape=[jax.ShapeDtypeStruct((T, D_MODEL), F32), jax.ShapeDtypeStruct((T, D_MODEL), BF16),
                   jax.ShapeDtypeStruct((8, D_MODEL), F32)],
        compiler_params=_params(("arbitrary",)),
    )(x1, mlp, target, mod8)


def _norm2_bwd(dh2, x1, dy, mo, gain, mod8):
    T = x1.shape[0]

    def body(dh_ref, x1_ref, dy_ref, mo_ref, g_ref, m_ref, dx1_ref, dmo_ref, st_ref):
        i = pl.program_id(0)
        gain_v, scale = g_ref[...], m_ref[4:5, :]
        _, xhat, n, rstd = _rms_fwd_math(x1_ref[...], gain_v, scale, m_ref[3:4, :])
        dx, d_scale, d_shift, d_gain = _rms_bwd_math(dh_ref[...], xhat, n, rstd, gain_v, scale)
        dx1 = dy_ref[...] + dx
        dx1_ref[...] = dx1
        dmo_ref[...] = (dx1 * m_ref[2:3, :]).astype(BF16)

        @pl.when(i == 0)
        def _():
            st_ref[...] = jnp.zeros_like(st_ref)

        st_ref[0:1, :] += d_scale
        st_ref[1:2, :] += d_shift
        st_ref[2:3, :] += d_gain
        st_ref[3:4, :] += jnp.sum(dx1 * mo_ref[...], axis=0, keepdims=True)

    return _pcall(
        body, name="norm2_bwd", grid=(T // ROWS_EW,),
        in_specs=[_row_spec(), _row_spec(), _row_spec(), _row_spec(), _vec_spec(1), _vec_spec()],
        out_specs=[_row_spec(), _row_spec(), _vec_spec()],
        out_shape=[jax.ShapeDtypeStruct((T, D_MODEL), F32), jax.ShapeDtypeStruct((T, D_MODEL), BF16),
                   jax.ShapeDtypeStruct((8, D_MODEL), F32)],
        compiler_params=_params(("arbitrary",)),
    )(dh2, x1, dy, mo, gain, mod8)


def _norm1_bwd(dh, x, dx1, gain, mod8):
    T = x.shape[0]

    def body(dh_ref, x_ref, dx1_ref, g_ref, m_ref, dx_ref, st_ref):
        i = pl.program_id(0)
        gain_v, scale = g_ref[...], m_ref[1:2, :]
        _, xhat, n, rstd = _rms_fwd_math(x_ref[...], gain_v, scale, m_ref[0:1, :])
        dx, d_scale, d_shift, d_gain = _rms_bwd_math(dh_ref[...], xhat, n, rstd, gain_v, scale)
        dx_ref[...] = dx1_ref[...] + dx

        @pl.when(i == 0)
        def _():
            st_ref[...] = jnp.zeros_like(st_ref)

        st_ref[0:1, :] += d_scale
        st_ref[1:2, :] += d_shift
        st_ref[2:3, :] += d_gain

    return _pcall(
        body, name="norm1_bwd", grid=(T // ROWS_EW,),
        in_specs=[_row_spec(), _row_spec(), _row_spec(), _vec_spec(1), _vec_spec()],
        out_specs=[_row_spec(), _vec_spec()],
        out_shape=[jax.ShapeDtypeStruct((T, D_MODEL), F32), jax.ShapeDtypeStruct((8, D_MODEL), F32)],
        compiler_params=_params(("arbitrary",)),
    )(dh, x, dx1, gain, mod8)


MERGE_BC = 512


def _merge_specs():
    ga = pl.BlockSpec((ROWS_EW, MERGE_BC), lambda i, j: (i, OFF_GATE_A // MERGE_BC + j))
    gb = pl.BlockSpec((ROWS_EW, MERGE_BC), lambda i, j: (i, OFF_GATE_B // MERGE_BC + j))
    t = pl.BlockSpec((ROWS_EW, MERGE_BC), lambda i, j: (i, j))
    return ga, gb, t


def _merge_fwd(proj, ya, yb):
    T = proj.shape[0]
    ga, gb, t = _merge_specs()

    def body(ga_ref, gb_ref, ya_ref, yb_ref, o_ref):
        o_ref[...] = (_sig(ga_ref[...]) * ya_ref[...] + _sig(gb_ref[...]) * yb_ref[...]).astype(BF16)

    return _pcall(
        body, name="merge_fwd", grid=(T // ROWS_EW, D_MODEL // MERGE_BC),
        in_specs=[ga, gb, t, t], out_specs=t, out_shape=jax.ShapeDtypeStruct((T, D_MODEL), BF16),
        compiler_params=_params(("parallel", "parallel")),
    )(proj, proj, ya, yb)


def _merge_bwd(proj, ya, yb, dmerged):
    T = proj.shape[0]
    ga, gb, t = _merge_specs()

    def body(ga_ref, gb_ref, ya_ref, yb_ref, dm_ref, dya_ref, dyb_ref, dga_ref, dgb_ref):
        dm = dm_ref[...]
        sa, sb = _sig(ga_ref[...]), _sig(gb_ref[...])
        dya_ref[...] = (dm * sa).astype(BF16)
        dyb_ref[...] = (dm * sb).astype(BF16)
        dga_ref[...] = (dm * ya_ref[...] * sa * (1.0 - sa)).astype(BF16)
        dgb_ref[...] = (dm * yb_ref[...] * sb * (1.0 - sb)).astype(BF16)

    sh = jax.ShapeDtypeStruct((T, D_MODEL), BF16)
    return _pcall(
        body, name="merge_bwd", grid=(T // ROWS_EW, D_MODEL // MERGE_BC),
        in_specs=[ga, gb, t, t, t], out_specs=[t, t, t, t], out_shape=[sh, sh, sh, sh],
        compiler_params=_params(("parallel", "parallel")),
    )(proj, proj, ya, yb, dmerged)


def _hgrn_rows(T):
    return 512 if T >= 1024 else 128


def _lower_bound(lbl):
    e = jnp.exp(lbl - jnp.max(lbl, axis=0, keepdims=True))
    return e[0:1, :] / (e[0:1, :] + e[1:2, :])


def _chunk_sum_matrix(rows, backward):
    shift = A_CHUNK.bit_length() - 1
    r = lax.broadcasted_iota(jnp.int32, (rows, rows), 0)
    c = lax.broadcasted_iota(jnp.int32, (rows, rows), 1)
    same = jnp.right_shift(r, shift) == jnp.right_shift(c, shift)
    return (same & ((r <= c) if backward else (r >= c))).astype(BF16)


def _chunk_sums(m, x):
    n = x.shape[1]
    hi = x.astype(BF16)
    rest = x - hi.astype(F32)
    mid = rest.astype(BF16)
    lo = (rest - mid.astype(F32)).astype(BF16)
    y = jnp.dot(m, jnp.concatenate([hi, mid, lo], axis=1), preferred_element_type=F32)
    return y[:, 0:n] + y[:, n:2 * n] + y[:, 2 * n:3 * n]


def _hgrn_block_pre(q, fl, lb, m_fwd):
    sg = _sig(fl)
    f = lb + (1.0 - lb) * sg
    sq = _sig(q)
    return dict(sg=sg, f=f, k=1.0 - f, sq=sq, qf=q * sq, b=_chunk_sums(m_fwd, jnp.log(f)))


def _hgrn_chunk_fwd(pre, r, v, st):
    C = A_CHUNK
    qf, k, b = pre["qf"][r], pre["k"][r], pre["b"][r]
    causal = lax.broadcasted_iota(jnp.int32, (C, C), 0) >= lax.broadcasted_iota(jnp.int32, (C, C), 1)
    bm = b[C // 2 - 1:C // 2, :]
    bl = b[C - 1:C, :]
    e_q, e_k = jnp.exp(b - bm), jnp.exp(bm - b)
    e_b, e_l = jnp.exp(b), jnp.exp(bl - b)
    qd, kd = qf * e_q, k * e_k
    qe, ke = qf * e_b, k * e_l
    att = jnp.where(causal, _nt(qd, kd), 0.0)
    o = _nn(att, v) + _nt(qe, st)
    dec = jnp.exp(bl)
    st_next = st * dec + _tn(v, ke)
    return dict(causal=causal, e_q=e_q, e_k=e_k, e_b=e_b, e_l=e_l, qd=qd, kd=kd,
                qe=qe, ke=ke, att=att, o=o, dec=dec, st_next=st_next)


HGRN_HEADS_PER_STEP = 4


def _hgrn_fwd(proj, lb_logits, o_gain, plan=None):
    T = proj.shape[0]
    BR = _hgrn_rows(T)
    cps = BR // A_CHUNK
    K, NH = A_HEAD_DIM, HGRN_HEADS_PER_STEP
    W = NH * K

    def col(off):
        return pl.BlockSpec((BR, W), lambda h, cb: (cb, off // W + h))

    def body(q_ref, f_ref, i_ref, g_ref, lbl_ref, og_ref, o_ref, s_ref, st):
        @pl.when(pl.program_id(1) == 0)
        def _():
            st[...] = jnp.zeros_like(st)

        lb_all = _lower_bound(lbl_ref[...])
        m_fwd = _chunk_sum_matrix(BR, False)
        pre = [_hgrn_block_pre(q_ref[:, n * K:(n + 1) * K], f_ref[:, n * K:(n + 1) * K], lb_all[:, n * K:(n + 1) * K], m_fwd)
               for n in range(NH)]
        state = [st[n] for n in range(NH)]
        for ci in range(cps):
            r = slice(ci * A_CHUNK, (ci + 1) * A_CHUNK)
            for n in range(NH):
                hs = slice(n * K, (n + 1) * K)
                s_ref[n, ci] = state[n]
                c = _hgrn_chunk_fwd(pre[n], r, i_ref[r, hs], state[n])
                state[n] = c["st_next"]
                o = c["o"]
                on = o * lax.rsqrt(jnp.mean(o * o, axis=-1, keepdims=True) + EPS)
                g = g_ref[r, hs]
                o_ref[r, hs] = (on * og_ref[:, hs] * (g * _sig(g))).astype(BF16)
        for n in range(NH):
            st[n] = state[n]

    return _pcall(
        body, plan=plan, name="hgrn_fwd", grid=(A_HEADS // NH, T // BR),
        in_specs=[col(OFF_QA), col(OFF_FA), col(OFF_IA), col(OFF_GA),
                  pl.BlockSpec((2, W), lambda h, cb: (0, h)), pl.BlockSpec((1, W), lambda h, cb: (0, h))],
        out_specs=[pl.BlockSpec((BR, W), lambda h, cb: (cb, h)),
                   pl.BlockSpec((NH, cps, K, K), lambda h, cb: (h, cb, 0, 0))],
        out_shape=[jax.ShapeDtypeStruct((T, A_WIDTH), BF16),
                   jax.ShapeDtypeStruct((A_HEADS, T // A_CHUNK, K, K), F32)],
        scratch_shapes=[pltpu.VMEM((NH, K, K), F32)],
        compiler_params=_params(("parallel", "arbitrary")),
    )(proj, proj, proj, proj, lb_logits, o_gain)


def _hgrn_bwd(proj, lb_logits, o_gain, states, do, plan=None):
    T = proj.shape[0]
    BR = _hgrn_rows(T)
    cps = BR // A_CHUNK
    ncb = T // BR
    K, C, NH = A_HEAD_DIM, A_CHUNK, HGRN_HEADS_PER_STEP
    W = NH * K

    def col(off):
        return pl.BlockSpec((BR, W), lambda h, cb: (ncb - 1 - cb, off // W + h))

    def body(q_ref, f_ref, i_ref, g_ref, lbl_ref, og_ref, s_ref, do_ref,
             dq_ref, df_ref, di_ref, dg_ref, dlb_ref, dog_ref, dst):
        @pl.when(pl.program_id(1) == 0)
        def _():
            dst[...] = jnp.zeros_like(dst)
            dlb_ref[...] = jnp.zeros_like(dlb_ref)
            dog_ref[...] = jnp.zeros_like(dog_ref)

        lb_all = _lower_bound(lbl_ref[...])
        row = lax.broadcasted_iota(jnp.int32, (C, K), 0)
        m_fwd, m_bwd = _chunk_sum_matrix(BR, False), _chunk_sum_matrix(BR, True)
        pre = [_hgrn_block_pre(q_ref[:, n * K:(n + 1) * K], f_ref[:, n * K:(n + 1) * K], lb_all[:, n * K:(n + 1) * K], m_fwd)
               for n in range(NH)]
        d_state = [dst[n] for n in range(NH)]
        d_og = [jnp.zeros((1, K), F32) for _ in range(NH)]
        db_of = [[None] * cps for _ in range(NH)]
        dk_of = [[None] * cps for _ in range(NH)]
        for ci in reversed(range(cps)):
            r = slice(ci * C, (ci + 1) * C)
            for n in range(NH):
                hs = slice(n * K, (n + 1) * K)
                gain = og_ref[:, hs]
                st = s_ref[n, ci]
                v = i_ref[r, hs]
                q = q_ref[r, hs]
                c = _hgrn_chunk_fwd(pre[n], r, v, st)
                dst_next = d_state[n]
                o = c["o"]
                rn = lax.rsqrt(jnp.mean(o * o, axis=-1, keepdims=True) + EPS)
                on = o * rn
                g = g_ref[r, hs]
                sgg = _sig(g)
                dy = do_ref[r, hs]
                d_ong = dy * (g * sgg)
                dg_ref[r, hs] = (dy * (on * gain) * (sgg * (1.0 + g * (1.0 - sgg)))).astype(BF16)
                d_og[n] = d_og[n] + jnp.sum(d_ong * on, axis=0, keepdims=True)
                d_on = d_ong * gain
                d_o = rn * (d_on - on * jnp.mean(d_on * on, axis=-1, keepdims=True))
                datt = jnp.where(c["causal"], _nt(d_o, v), 0.0)
                dv = _tn(c["att"], d_o) + _nt(c["ke"], dst_next)
                dqd = _nn(datt, c["kd"])
                dkd = _tn(datt, c["qd"])
                dqe = _nn(d_o, st)
                dke = _nn(v, dst_next)
                d_state[n] = dst_next * c["dec"] + _tn(d_o, c["qe"])
                d_dec = jnp.sum(dst_next * st, axis=0, keepdims=True)
                t_q, t_k = dqd * c["qd"], dkd * c["kd"]
                t_e, t_l = dqe * c["qe"], dke * c["ke"]
                db = t_q - t_k + t_e - t_l
                dbm = jnp.sum(t_k - t_q, axis=0, keepdims=True)
                dbl = jnp.sum(t_l, axis=0, keepdims=True) + d_dec * c["dec"]
                db_of[n][ci] = db + jnp.where(row == C // 2 - 1, dbm, 0.0) + jnp.where(row == C - 1, dbl, 0.0)
                dqf = dqd * c["e_q"] + dqe * c["e_b"]
                sq = pre[n]["sq"][r]
                dq_ref[r, hs] = (dqf * (sq * (1.0 + q * (1.0 - sq)))).astype(BF16)
                dk_of[n][ci] = dkd * c["e_k"] + dke * c["e_l"]
                di_ref[r, hs] = dv.astype(BF16)
        for n in range(NH):
            hs = slice(n * K, (n + 1) * K)
            dst[n] = d_state[n]
            dog_ref[0:1, hs] += d_og[n]
            lb, sg = lb_all[:, hs], pre[n]["sg"]
            dlf = _chunk_sums(m_bwd, jnp.concatenate(db_of[n], axis=0))
            df = dlf / pre[n]["f"] - jnp.concatenate(dk_of[n], axis=0)
            df_ref[:, hs] = (df * (1.0 - lb) * sg * (1.0 - sg)).astype(BF16)
            dlb_ref[0:1, hs] += jnp.sum(df * (1.0 - sg), axis=0, keepdims=True)

    ocol = pl.BlockSpec((BR, W), lambda h, cb: (ncb - 1 - cb, h))
    vec = pl.BlockSpec((8, W), lambda h, cb: (0, h))
    return _pcall(
        body, plan=plan, name="hgrn_bwd", grid=(A_HEADS // NH, ncb),
        in_specs=[col(OFF_QA), col(OFF_FA), col(OFF_IA), col(OFF_GA),
                  pl.BlockSpec((2, W), lambda h, cb: (0, h)), pl.BlockSpec((1, W), lambda h, cb: (0, h)),
                  pl.BlockSpec((NH, cps, K, K), lambda h, cb: (h, ncb - 1 - cb, 0, 0)),
                  pl.BlockSpec((BR, W), lambda h, cb: (ncb - 1 - cb, h))],
        out_specs=[ocol, ocol, ocol, ocol, vec, vec],
        out_shape=[jax.ShapeDtypeStruct((T, A_WIDTH), BF16)] * 4 + [jax.ShapeDtypeStruct((8, A_WIDTH), F32)] * 2,
        scratch_shapes=[pltpu.VMEM((NH, K, K), F32)],
        compiler_params=_params(("parallel", "arbitrary")),
    )(proj, proj, proj, proj, lb_logits, o_gain, states, do)


def _head_norm(x):
    r = lax.rsqrt(jnp.mean(x * x, axis=-1, keepdims=True) + EPS)
    return x * r, r


def _head_norm_bwd(dy, xn, r, gain):
    dxn = dy * gain
    return r * (dxn - xn * jnp.mean(dxn * xn, axis=-1, keepdims=True)), jnp.sum(dy * xn, axis=0, keepdims=True)


def _swa_mask(has_prev):
    rows = B_GROUP * BLOCK
    r = lax.broadcasted_iota(jnp.int32, (rows, 2 * BLOCK), 0) % BLOCK
    c = lax.broadcasted_iota(jnp.int32, (rows, 2 * BLOCK), 1)
    rel = r + BLOCK - c
    return (rel >= 0) & (rel < BLOCK) & ((c >= BLOCK) | has_prev)


def _swa_head_fwd(j, q_ref, kp_ref, kc_ref, vp_ref, vc_ref, qg, kg, sk_ref, mask):
    hs = slice(j * B_HEAD_DIM, (j + 1) * B_HEAD_DIM)
    kcat = jnp.concatenate([kp_ref[:, hs], kc_ref[:, hs]], axis=0)
    vcat = jnp.concatenate([vp_ref[:, hs], vc_ref[:, hs]], axis=0)
    qs = jnp.concatenate([q_ref[:, pl.ds((j * B_GROUP + g) * B_HEAD_DIM, B_HEAD_DIM)] for g in range(B_GROUP)], axis=0)
    kn, kr = _head_norm(kcat)
    qn, qr = _head_norm(qs)
    kh, qh = kn * kg, qn * qg
    s = jnp.where(mask, _nt(qh, kh) * (B_HEAD_DIM ** -0.5), NEG_BIG)
    sink = jnp.concatenate(
        [jnp.broadcast_to(sk_ref[0:1, pl.ds(j * B_GROUP + g, 1)], (BLOCK, 1)) for g in range(B_GROUP)], axis=0)
    m = jnp.maximum(jnp.max(s, axis=-1, keepdims=True), sink)
    p = jnp.exp(s - m)
    e_sink = jnp.exp(sink - m)
    inv = 1.0 / (jnp.sum(p, axis=-1, keepdims=True) + e_sink)
    prob = p * inv
    return dict(vcat=vcat, kn=kn, kr=kr, qn=qn, qr=qr, kh=kh, qh=qh, prob=prob, p_sink=e_sink * inv)


def _swa_in_specs(nb, last):
    def qi(n):
        return jnp.minimum(n, last)

    q = pl.BlockSpec((BLOCK, B_WIDTH), lambda n: (qi(n), OFF_QB // B_WIDTH))
    kc = pl.BlockSpec((BLOCK, B_KV_WIDTH), lambda n: (qi(n), OFF_KB // B_KV_WIDTH))
    kp = pl.BlockSpec((BLOCK, B_KV_WIDTH), lambda n: (jnp.maximum(qi(n) - 1, 0), OFF_KB // B_KV_WIDTH))
    vc = pl.BlockSpec((BLOCK, B_KV_WIDTH), lambda n: (qi(n), OFF_VB // B_KV_WIDTH))
    vp = pl.BlockSpec((BLOCK, B_KV_WIDTH), lambda n: (jnp.maximum(qi(n) - 1, 0), OFF_VB // B_KV_WIDTH))
    small = [pl.BlockSpec((1, B_HEAD_DIM), lambda n: (0, 0)), pl.BlockSpec((1, B_HEAD_DIM), lambda n: (0, 0)),
             pl.BlockSpec((1, B_GROUP * B_KV_HEADS), lambda n: (0, 0))]
    return [q, kp, kc, vp, vc] + small


def _swa_fwd(proj, q_gain, k_gain, sinks, plan=None):
    T = proj.shape[0]
    nb = T // BLOCK

    def body(q_ref, kp_ref, kc_ref, vp_ref, vc_ref, qg_ref, kg_ref, sk_ref, o_ref):
        mask = _swa_mask(pl.program_id(0) > 0)
        for j in range(B_KV_HEADS):
            c = _swa_head_fwd(j, q_ref, kp_ref, kc_ref, vp_ref, vc_ref, qg_ref[...], kg_ref[...], sk_ref, mask)
            o = _nn(c["prob"], c["vcat"])
            for g in range(B_GROUP):
                o_ref[:, pl.ds((j * B_GROUP + g) * B_HEAD_DIM, B_HEAD_DIM)] = o[g * BLOCK:(g + 1) * BLOCK].astype(BF16)

    return _pcall(
        body, plan=plan, name="swa_fwd", grid=(nb,),
        in_specs=_swa_in_specs(nb, nb - 1),
        out_specs=pl.BlockSpec((BLOCK, B_WIDTH), lambda n: (n, 0)),
        out_shape=jax.ShapeDtypeStruct((T, B_WIDTH), BF16),
        compiler_params=_params(("parallel",)),
    )(proj, proj, proj, proj, proj, q_gain, k_gain, sinks)


def _swa_bwd(proj, q_gain, k_gain, sinks, do, plan=None):
    T = proj.shape[0]
    nb = T // BLOCK
    scale = B_HEAD_DIM ** -0.5

    def body(q_ref, kp_ref, kc_ref, vp_ref, vc_ref, qg_ref, kg_ref, sk_ref, do_ref,
             dq_ref, dkv_ref, sm_ref, ck, cv):
        n = pl.program_id(0)

        @pl.when(n == 0)
        def _():
            ck[...] = jnp.zeros_like(ck)
            cv[...] = jnp.zeros_like(cv)
            sm_ref[...] = jnp.zeros_like(sm_ref)

        @pl.when(n < nb)
        def _():
            mask = _swa_mask(n > 0)
            qg, kg = qg_ref[...], kg_ref[...]
            lane = lax.broadcasted_iota(jnp.int32, (1, BLOCK), 1)
            for j in range(B_KV_HEADS):
                hs = slice(j * B_HEAD_DIM, (j + 1) * B_HEAD_DIM)
                vs = slice(B_KV_WIDTH + j * B_HEAD_DIM, B_KV_WIDTH + (j + 1) * B_HEAD_DIM)
                c = _swa_head_fwd(j, q_ref, kp_ref, kc_ref, vp_ref, vc_ref, qg, kg, sk_ref, mask)
                d_out = jnp.concatenate(
                    [do_ref[:, pl.ds((j * B_GROUP + g) * B_HEAD_DIM, B_HEAD_DIM)] for g in range(B_GROUP)], axis=0)
                prob = c["prob"]
                out = _nn(prob, c["vcat"])
                delta = jnp.sum(d_out * out, axis=-1, keepdims=True)
                ds = prob * (_nt(d_out, c["vcat"]) - delta)
                d_sink = -c["p_sink"] * delta
                dqh = _nn(ds, c["kh"]) * scale
                dkh = _tn(ds, c["qh"]) * scale
                dv = _tn(prob, d_out)
                dq, dqg = _head_norm_bwd(dqh, c["qn"], c["qr"], qg)
                dk, dkg = _head_norm_bwd(dkh, c["kn"], c["kr"], kg)
                sm_ref[0:1, 0:B_HEAD_DIM] += dqg
                sm_ref[1:2, 0:B_HEAD_DIM] += dkg
                for g in range(B_GROUP):
                    dq_ref[:, pl.ds((j * B_GROUP + g) * B_HEAD_DIM, B_HEAD_DIM)] = dq[g * BLOCK:(g + 1) * BLOCK].astype(BF16)
                    tot = jnp.sum(d_sink[g * BLOCK:(g + 1) * BLOCK], axis=0, keepdims=True)
                    sm_ref[2:3, :] += jnp.where(lane == j * B_GROUP + g, tot, 0.0)
                dkv_ref[:, hs] = (ck[:, hs] + dk[0:BLOCK]).astype(BF16)
                dkv_ref[:, vs] = (cv[:, hs] + dv[0:BLOCK]).astype(BF16)
                ck[:, hs] = dk[BLOCK:2 * BLOCK]
                cv[:, hs] = dv[BLOCK:2 * BLOCK]

        @pl.when(n == nb)
        def _():
            dkv_ref[:, 0:B_KV_WIDTH] = ck[...].astype(BF16)
            dkv_ref[:, B_KV_WIDTH:2 * B_KV_WIDTH] = cv[...].astype(BF16)

    return _pcall(
        body, plan=plan, name="swa_bwd", grid=(nb + 1,),
        in_specs=_swa_in_specs(nb, nb - 1) + [pl.BlockSpec((BLOCK, B_WIDTH), lambda n: (jnp.minimum(n, nb - 1), 0))],
        out_specs=[pl.BlockSpec((BLOCK, B_WIDTH), lambda n: (jnp.minimum(n, nb - 1), 0)),
                   pl.BlockSpec((BLOCK, 2 * B_KV_WIDTH), lambda n: (jnp.maximum(n - 1, 0), 0)),
                   pl.BlockSpec((8, BLOCK), lambda n: (0, 0))],
        out_shape=[jax.ShapeDtypeStruct((T, B_WIDTH), BF16), jax.ShapeDtypeStruct((T, 2 * B_KV_WIDTH), BF16),
                   jax.ShapeDtypeStruct((8, BLOCK), F32)],
        scratch_shapes=[pltpu.VMEM((BLOCK, B_KV_WIDTH), F32), pltpu.VMEM((BLOCK, B_KV_WIDTH), F32)],
        compiler_params=_params(("arbitrary",)),
    )(proj, proj, proj, proj, proj, q_gain, k_gain, sinks, do)


W_IN, W_A, W_B, W_OUT, W_MI, W_MO = range(6)


def _local_step(x, target, mod8, norm1_gain, norm2_gain, lb_logits, o_gain, q_gain, k_gain, sinks, parts, c_arr, chip_arr):
    relu2 = lambda u: (u, jnp.square(jnp.maximum(u, 0.0)))
    pair, half = {}, {}

    def exchange(ws, grads):
        return _sibling_exchange_plan([_grad_view(g, w) for w, g in zip(ws, grads)])

    def pair_sums(ws, grads, others):
        for w, g, o in zip(ws, grads, others):
            pair[w] = _pair_sum(_grad_view(g, w), o, c_arr, f"pair_sum{w}")

    def sum_slots(ws, slots):
        for w, s in zip(ws, slots):
            half[w] = _sum_slots(pair[w], s, w, chip_arr, f"sum_slots{w}")

    h, (w_in,) = _norm1_fwd(x, norm1_gain, mod8, plan=_gather_plan({W_IN: parts[W_IN]}, pass_at=0.9))
    proj, (w_mi,) = _mm(h, w_in, name="mm_proj", bn=512, plan=_gather_plan({W_MI: parts[W_MI]}, pass_at=0.8))
    (o_a, states), (w_a, w_b, w_out) = _hgrn_fwd(
        proj, lb_logits, o_gain, plan=_gather_plan({w: parts[w] for w in (W_A, W_B, W_OUT)}, pass_at=0.8))
    o_b, (w_mo,) = _swa_fwd(proj, q_gain, k_gain, sinks, plan=_gather_plan({W_MO: parts[W_MO]}, pass_at=0.9))
    ya = _mm(o_a, w_a, name="mm_branch_a")
    yb = _mm(o_b, w_b, name="mm_branch_b")
    merged = _merge_fwd(proj, ya, yb)
    mo = _mm(merged, w_out, name="mm_out")
    x1, h2 = _res_norm2_fwd(x, mo, norm2_gain, mod8)
    u, act = _mm(h2, w_mi, name="mm_mlp_in", out_dtypes=(F32, BF16), epi=relu2)
    mlp = _mm(act, w_mo, name="mm_mlp_out")
    dy, dmlp, st_loss = _loss_bwd(x1, mlp, target, mod8)
    g_mo = _mm(act, dmlp, name="mm_g_mlp_out", ta=True)
    du, others = _mm(dmlp, w_mo, name="mm_d_act", tb=True, out_dtypes=(BF16,), extras=(u,),
                     epi=lambda acc, uu: (acc * (2.0 * jnp.maximum(uu, 0.0)),), plan=exchange([W_MO], [g_mo]))
    pair_sums([W_MO], [g_mo], others)
    g_mi, slots_mo = _mm(h2, du, name="mm_g_mlp_in", ta=True, plan=_chip_exchange_plan({W_MO: pair[W_MO]}))
    dh2, others = _mm(du, w_mi, name="mm_d_h2", tb=True, plan=exchange([W_MI], [g_mi]))
    pair_sums([W_MI], [g_mi], others)
    sum_slots([W_MO], slots_mo)
    dx1, dmo, st_n2 = _norm2_bwd(dh2, x1, dy, mo, norm2_gain, mod8)
    dmerged = _mm(dmo, w_out, name="mm_d_merged", tb=True)
    g_out = _mm(merged, dmo, name="mm_g_out", ta=True)
    dya, dyb, dga, dgb = _merge_bwd(proj, ya, yb, dmerged)
    do_a = _mm(dya, w_a, name="mm_d_oa", tb=True)
    g_a = _mm(o_a, dya, name="mm_g_branch_a", ta=True)
    do_b = _mm(dyb, w_b, name="mm_d_ob", tb=True)
    g_b = _mm(o_b, dyb, name="mm_g_branch_b", ta=True)
    mid = [W_A, W_B, W_OUT]
    (dqa, dfa, dia, dgga, d_lb, d_og), res = _hgrn_bwd(
        proj, lb_logits, o_gain, states, do_a,
        plan=_join(_chip_exchange_plan({W_MI: pair[W_MI]}), exchange(mid, [g_a, g_b, g_out])))
    sum_slots([W_MI], res[:1])
    pair_sums(mid, [g_a, g_b, g_out], res[1:])
    (dqb, dkvb, st_swa), slots_mid = _swa_bwd(proj, q_gain, k_gain, sinks, do_b,
                                              plan=_chip_exchange_plan({w: pair[w] for w in mid}))
    sum_slots(mid, slots_mid)
    dproj = jnp.concatenate([dqa, dfa, dia, dgga, dqb, dkvb, dga, dgb], axis=1)
    g_in = _mm(h, dproj, name="mm_g_in", ta=True, bn=512)
    done = [W_A, W_B, W_OUT, W_MI, W_MO]
    dh, res = _mm(dproj, w_in, name="mm_d_h", tb=True, bk=2432,
                  plan=_join(exchange([W_IN], [g_in]), _sibling_share_plan([half[w] for w in done])))
    pair_sums([W_IN], [g_in], res[:1])
    theirs = dict(zip(done, res[1:]))
    grad_x, st_n1 = _norm1_bwd(dh, x, dx1, norm1_gain, mod8)
    stats = dict(loss=st_loss, n2=st_n2, n1=st_n1, d_lb=d_lb, d_og=d_og, swa=st_swa)
    return grad_x, half, theirs, pair[W_IN], stats


def _ew_rows(rows, cols):
    br = 8
    while br * 2 <= rows and br * 2 * cols * 4 <= (1 << 20) and rows % (br * 2) == 0:
        br *= 2
    return br


def _cast_into_full(shard, w, chip_arr, name):
    sr, sc = shard.shape
    R, C, by_col = W_SHAPES[w]
    br = _ew_rows(sr, sc)
    nb = sr // br
    out_map = (lambda i, chip: (i, chip[0])) if by_col else (lambda i, chip: (chip[0] * nb + i, 0))

    def body(chip_ref, w_ref, o_ref):
        o_ref[...] = w_ref[...].astype(BF16)

    return _pcall(
        body, name=name,
        grid_spec=pltpu.PrefetchScalarGridSpec(
            num_scalar_prefetch=1, grid=(nb,),
            in_specs=[pl.BlockSpec((br, sc), lambda i, chip: (i, 0))],
            out_specs=pl.BlockSpec((br, sc), out_map)),
        out_shape=jax.ShapeDtypeStruct((R, C), BF16), compiler_params=_params(("parallel",)))(chip_arr, shard)


def _adamw_math(w, g, m, v):
    m = ADAM_B1 * m + (1.0 - ADAM_B1) * g
    v = ADAM_B2 * v + (1.0 - ADAM_B2) * (g * g)
    m_hat = m / (1.0 - ADAM_B1 ** ADAM_STEP)
    v_hat = v / (1.0 - ADAM_B2 ** ADAM_STEP)
    delta = -ADAM_LR * (m_hat / (jnp.sqrt(v_hat) + ADAM_EPS) + ADAM_WD * w)
    return delta, m, v


def _adamw(w, g, m, v, name):
    R, C = w.shape
    br = _ew_rows(R, C)
    spec = pl.BlockSpec((br, C), lambda i: (i, 0))

    def body(w_ref, g_ref, m_ref, v_ref, d_ref, nm_ref, nv_ref):
        d_ref[...], nm_ref[...], nv_ref[...] = _adamw_math(w_ref[...], g_ref[...], m_ref[...], v_ref[...])

    sh = jax.ShapeDtypeStruct((R, C), F32)
    return _pcall(body, name=name, grid=(R // br,), in_specs=[spec] * 4, out_specs=[spec] * 3, out_shape=[sh] * 3,
                  compiler_params=_params(("parallel",)))(w, g, m, v)


ADAMW_STEPS_PER_HALF = 16


def _adamw_multi(items, name, plan=None):
    S, L = ADAMW_STEPS_PER_HALF, D_MODEL
    n = len(items)
    shapes = [it[0].shape for it in items]
    flat, in_specs, out_specs, out_shape = [], [], [], []
    for (w, own, oth, m, v) in items:
        rows = w.size // L
        hr = rows // 2
        br = hr // S
        assert rows * L == w.size and br * S == hr and br % 8 == 0, w.shape
        full = pl.BlockSpec((br, L), lambda h, i, hb=hr // br: (h * hb + i, 0))
        half = pl.BlockSpec((br, L), lambda h, i: (i, 0))
        flat += [w.reshape(rows, L), own.reshape(hr, L), oth.reshape(hr, L), m.reshape(rows, L), v.reshape(rows, L)]
        in_specs += [full, half, half, full, full]
        out_specs += [full] * 4
        out_shape += [jax.ShapeDtypeStruct((rows, L), F32)] * 4

    def body(*refs):
        mine = pl.program_id(0) == lax.axis_index("c")
        for k in range(n):
            w_ref, own_ref, oth_ref, m_ref, v_ref = refs[5 * k:5 * k + 5]
            g_ref, d_ref, nm_ref, nv_ref = refs[5 * n + 4 * k:5 * n + 4 * k + 4]
            g = jnp.where(mine, own_ref[...], oth_ref[...])
            g_ref[...] = g
            d_ref[...], nm_ref[...], nv_ref[...] = _adamw_math(w_ref[...], g, m_ref[...], v_ref[...])

    res = _pcall(body, plan=plan, name=name, grid=(2, S), in_specs=in_specs, out_specs=out_specs, out_shape=out_shape,
                 compiler_params=_params(("arbitrary", "arbitrary")))(*flat)
    outs, extra = res if plan is not None else (res, None)
    per = [tuple(o.reshape(shapes[k]) for o in outs[4 * k:4 * k + 4]) for k in range(n)]
    return per if plan is None else (per, extra)


def _adamw_halves(w, own, other, m, v, c_arr, name):
    R, C = w.shape
    hr = R // 2
    br = _ew_rows(hr, C)
    nb = hr // br
    full = pl.BlockSpec((br, C), lambda h, i, c_ref: (h * nb + i, 0))
    half = pl.BlockSpec((br, C), lambda h, i, c_ref: (i, 0))

    def body(c_ref, w_ref, own_ref, oth_ref, m_ref, v_ref, g_ref, d_ref, nm_ref, nv_ref):
        g = jnp.where(pl.program_id(0) == c_ref[0], own_ref[...], oth_ref[...])
        g_ref[...] = g
        d_ref[...], nm_ref[...], nv_ref[...] = _adamw_math(w_ref[...], g, m_ref[...], v_ref[...])

    sh = jax.ShapeDtypeStruct((R, C), F32)
    return _pcall(
        body, name=name,
        grid_spec=pltpu.PrefetchScalarGridSpec(
            num_scalar_prefetch=1, grid=(2, nb), in_specs=[full, half, half, full, full], out_specs=[full] * 4),
        out_shape=[sh] * 4, compiler_params=_params(("parallel", "parallel")))(c_arr, w, own, other, m, v)


def _ada_grad_adamw(c_t, dmod, w, m, v):
    R, C = w.shape
    br = _ew_rows(R, C)
    spec = pl.BlockSpec((br, C), lambda i: (i, 0))

    def body(c_ref, dm_ref, w_ref, m_ref, v_ref, g_ref, d_ref, nm_ref, nv_ref):
        cv = c_ref[...]
        sc = cv * _sig(cv)
        g = sc[:, 0:1] * dm_ref[0:1, :]
        for b in range(1, N_DEV):
            g = g + sc[:, b:b + 1] * dm_ref[b:b + 1, :]
        g_ref[...] = g
        d_ref[...], nm_ref[...], nv_ref[...] = _adamw_math(w_ref[...], g, m_ref[...], v_ref[...])

    sh = jax.ShapeDtypeStruct((R, C), F32)
    return _pcall(
        body, name="ada_grad_adamw", grid=(R // br,),
        in_specs=[pl.BlockSpec((br, N_DEV), lambda i: (i, 0)), pl.BlockSpec((N_DEV, C), lambda i: (0, 0)), spec, spec, spec],
        out_specs=[spec] * 4, out_shape=[sh] * 4, compiler_params=_params(("parallel",)))(c_t, dmod, w, m, v)


SMALL_ROWS = 16


def _small_sum(small_all, lb_logits):
    def body(s_ref, lbl_ref, o_ref):
        acc = s_ref[0:SMALL_ROWS, :]
        for d in range(1, N_DEV):
            acc = acc + s_ref[d * SMALL_ROWS:(d + 1) * SMALL_ROWS, :]
        o_ref[...] = acc
        z = lbl_ref[...]
        e = jnp.exp(z - jnp.max(z, axis=0, keepdims=True))
        p0 = e[0:1, :] / (e[0:1, :] + e[1:2, :])
        dz = acc[8:9, 0:A_WIDTH] * p0 * (1.0 - p0)
        o_ref[8:9, 0:A_WIDTH] = dz
        o_ref[10:11, 0:A_WIDTH] = -dz

    return _pcall(body, name="small_sum", out_shape=jax.ShapeDtypeStruct((SMALL_ROWS, D_MODEL), F32),
                  in_specs=[pl.BlockSpec(memory_space=pltpu.VMEM)] * 2, out_specs=pl.BlockSpec(memory_space=pltpu.VMEM),
                  compiler_params=_params())(small_all, lb_logits)


RELATIONS = ((1, 0), (0, 1), (1, 1))
ANY = pl.BlockSpec(memory_space=pl.ANY)


def _place():
    x, y, c = lax.axis_index("x"), lax.axis_index("y"), lax.axis_index("c")
    return x, y, c


def _allgather_small(x_shard, name):
    m_per, n = x_shard.shape

    def body(x_ref, out_ref, send_sems, recv_sems, local_sem):
        x, y, c = _place()
        me, sibling = (x, y, c), (x, y, 1 - c)
        chips = [(1 - x, y), (x, 1 - y), (1 - x, 1 - y)]

        def rows(px, py, pc):
            return out_ref.at[pl.ds((4 * px + 2 * py + pc) * m_per, m_per), :]

        def copy(k, block, to, src=None):
            return pltpu.make_async_remote_copy(
                src_ref=rows(*block) if src is None else src, dst_ref=rows(*block),
                send_sem=send_sems.at[k], recv_sem=recv_sems.at[k], device_id=to, device_id_type=MESH)

        mine = pltpu.make_async_copy(x_ref, rows(*me), local_sem)
        mine.start()
        first = [copy(0, me, sibling, src=x_ref)]
        first += [copy(1 + j, me, (*chip, c), src=x_ref) for j, chip in enumerate(chips)]
        for cp in first:
            cp.start()
        passed = [copy(4 + j, (*chip, c), sibling) for j, chip in enumerate(chips)]
        for j, chip in enumerate(chips):
            copy(1 + j, (*chip, c), me).wait_recv()
            passed[j].start()
        copy(0, sibling, me).wait_recv()
        for j, chip in enumerate(chips):
            copy(4 + j, (*chip, 1 - c), me).wait_recv()
        for cp in first + passed:
            cp.wait_send()
        mine.wait()

    return _pcall(
        body, name=name, out_shape=jax.ShapeDtypeStruct((N_DEV * m_per, n), x_shard.dtype),
        in_specs=[pl.BlockSpec(memory_space=pltpu.VMEM)], out_specs=pl.BlockSpec(memory_space=pltpu.VMEM),
        scratch_shapes=[pltpu.SemaphoreType.DMA((7,)), pltpu.SemaphoreType.DMA((7,)), pltpu.SemaphoreType.DMA],
        compiler_params=_params(),
    )(x_shard)


W_SHAPES = ((D_MODEL, IN_WIDTH, True), (A_WIDTH, D_MODEL, True), (B_WIDTH, D_MODEL, True),
            (D_MODEL, D_MODEL, False), (D_MODEL, MLP_HIDDEN, True), (MLP_HIDDEN, D_MODEL, False))
N_W = len(W_SHAPES)


def _shard_shape(w):
    R, C, by_col = W_SHAPES[w]
    return (R, C // N_CHIPS) if by_col else (R // N_CHIPS, C)


def _half_shape(w):
    sr, sc = _shard_shape(w)
    return sr // 2, sc


def _region(full_ref, w, chip, half):
    sr, sc = _shard_shape(w)
    by_col = W_SHAPES[w][2]
    r0, c0 = (0, chip * sc) if by_col else (chip * sr, 0)
    if half is None:
        return full_ref.at[pl.ds(r0, sr), pl.ds(c0, sc)]
    return full_ref.at[pl.ds(r0 + half * (sr // 2), sr // 2), pl.ds(c0, sc)]


def _on_device(fn):
    x, y, c = _place()
    me = 4 * x + 2 * y + c
    for d in range(N_DEV):
        @pl.when(me == d)
        def _(d=d):
            fn(x, y, c, d)


def _gather_plan(partials, pass_at=0.5):
    ws = sorted(partials)
    pairs = [(i, w, k) for i, w in enumerate(ws) for k in range(3)]

    def first(pi, po, ps, x, y, c, d, i, w, k):
        chip, dc = d >> 1, d & 1
        rx, ry = RELATIONS[k]
        return pltpu.make_async_remote_copy(
            src_ref=_region(pi[i], w, chip, dc), dst_ref=_region(po[i], w, chip, dc),
            send_sem=ps[0].at[i * 3 + k], recv_sem=ps[1].at[i * 3 + k],
            device_id=(x ^ rx, y ^ ry, c), device_id_type=MESH)

    def landed(po, ps, x, y, c, d, i, w, k, half, to_sibling):
        rx, ry = RELATIONS[k]
        got = _region(po[i], w, (d >> 1) ^ (2 * rx + ry), half)
        s = 2 if to_sibling else 0
        return pltpu.make_async_remote_copy(
            src_ref=got, dst_ref=got, send_sem=ps[s].at[i * 3 + k], recv_sem=ps[s + 1].at[i * 3 + k],
            device_id=(x, y, 1 - c), device_id_type=MESH)

    def send(pi, po, ps):
        def run(x, y, c, d):
            for i, w, k in pairs:
                first(pi, po, ps, x, y, c, d, i, w, k).start()
        _on_device(run)

    def pass_on(pi, po, ps):
        def run(x, y, c, d):
            for i, w, k in pairs:
                landed(po, ps, x, y, c, d, i, w, k, d & 1, False).wait_recv()
                landed(po, ps, x, y, c, d, i, w, k, d & 1, True).start()
        _on_device(run)

    def finish(pi, po, ps):
        def run(x, y, c, d):
            for i, w, k in pairs:
                landed(po, ps, x, y, c, d, i, w, k, 1 - (d & 1), True).wait_recv()
            for i, w, k in pairs:
                first(pi, po, ps, x, y, c, d, i, w, k).wait_send()
                landed(po, ps, x, y, c, d, i, w, k, d & 1, True).wait_send()
        _on_device(run)

    return _Plan([partials[w] for w in ws], [jax.ShapeDtypeStruct(W_SHAPES[w][:2], BF16) for w in ws],
                 [pltpu.SemaphoreType.DMA((3 * len(ws),)) for _ in range(4)], [send, pass_on, finish],
                 {i: i for i in range(len(ws))}, mid_at=(pass_at,))


def _grad_view(g, w):
    R, C, by_col = W_SHAPES[w]
    return g.reshape(1, 2, R // 2, C) if by_col else g.reshape(N_CHIPS, 2, R // N_CHIPS // 2, C)


def _start_wait_plan(ins, outs, n_copies, copies):
    def start(pi, po, ps):
        for cp in copies(pi, po, ps):
            cp.start()

    def finish(pi, po, ps):
        for cp in copies(pi, po, ps):
            cp.wait()

    return _Plan(ins, outs, [pltpu.SemaphoreType.DMA((n_copies,)), pltpu.SemaphoreType.DMA((n_copies,))], [start, finish])


def _sibling_exchange_plan(g4s):
    pieces = [(i, p) for i, g in enumerate(g4s) for p in range(g.shape[0])]

    def copies(pi, po, ps):
        x, y, c = _place()
        return [pltpu.make_async_remote_copy(
            src_ref=pi[i].at[p, 1 - c], dst_ref=po[i].at[p], send_sem=ps[0].at[n], recv_sem=ps[1].at[n],
            device_id=(x, y, 1 - c), device_id_type=MESH) for n, (i, p) in enumerate(pieces)]

    return _start_wait_plan(list(g4s), [jax.ShapeDtypeStruct((g.shape[0],) + g.shape[2:], F32) for g in g4s],
                            len(pieces), copies)


def _pair_sum(g4, other, c_arr, name):
    P, _, hr, C = g4.shape
    br = _ew_rows(hr, C)

    def body(c_ref, g_ref, o_ref, p_ref):
        p_ref[...] = (g_ref[...] + o_ref[...]).astype(BF16)

    return _pcall(
        body, name=name,
        grid_spec=pltpu.PrefetchScalarGridSpec(
            num_scalar_prefetch=1, grid=(P, hr // br),
            in_specs=[pl.BlockSpec((None, None, br, C), lambda p, i, c_ref: (p, c_ref[0], i, 0)),
                      pl.BlockSpec((None, br, C), lambda p, i, c_ref: (p, i, 0))],
            out_specs=pl.BlockSpec((None, br, C), lambda p, i, c_ref: (p, i, 0))),
        out_shape=jax.ShapeDtypeStruct((P, hr, C), BF16),
        compiler_params=_params(("parallel", "parallel")),
    )(c_arr, g4, other)


def _pair_part(p_ref, w, chip):
    sr, sc = _shard_shape(w)
    return p_ref.at[0, :, pl.ds(chip * sc, sc)] if W_SHAPES[w][2] else p_ref.at[chip]


def _chip_exchange_plan(pairs):
    ws = sorted(pairs)

    def stage(wait):
        def run(pi, po, ps):
            def on(x, y, c, d):
                for i, w in enumerate(ws):
                    for k, (rx, ry) in enumerate(RELATIONS):
                        cp = pltpu.make_async_remote_copy(
                            src_ref=_pair_part(pi[i], w, (d >> 1) ^ (2 * rx + ry)), dst_ref=po[i].at[k],
                            send_sem=ps[0].at[i * 3 + k], recv_sem=ps[1].at[i * 3 + k],
                            device_id=(x ^ rx, y ^ ry, c), device_id_type=MESH)
                        if wait:
                            cp.wait()
                        else:
                            cp.start()
            _on_device(on)
        return run

    return _Plan([pairs[w] for w in ws], [jax.ShapeDtypeStruct((3,) + _half_shape(w), BF16) for w in ws],
                 [pltpu.SemaphoreType.DMA((3 * len(ws),)), pltpu.SemaphoreType.DMA((3 * len(ws),))],
                 [stage(False), stage(True)])


def _sum_slots(pair, slots, w, chip_arr, name):
    _, hr, C = slots.shape
    br = _ew_rows(hr, C)
    own_map = (lambda i, chip: (0, i, chip[0])) if W_SHAPES[w][2] else (lambda i, chip: (chip[0], i, 0))

    def body(chip_ref, p_ref, s_ref, o_ref):
        acc = p_ref[...].astype(F32)
        for k in range(3):
            acc = acc + s_ref[k].astype(F32)
        o_ref[...] = acc

    return _pcall(
        body, name=name,
        grid_spec=pltpu.PrefetchScalarGridSpec(
            num_scalar_prefetch=1, grid=(hr // br,),
            in_specs=[pl.BlockSpec((None, br, C), own_map), pl.BlockSpec((3, br, C), lambda i, chip: (0, i, 0))],
            out_specs=pl.BlockSpec((br, C), lambda i, chip: (i, 0))),
        out_shape=jax.ShapeDtypeStruct((hr, C), F32), compiler_params=_params(("parallel",)),
    )(chip_arr, pair, slots)


def _sibling_share_plan(halves):
    def copies(pi, po, ps):
        x, y, c = _place()
        return [pltpu.make_async_remote_copy(
            src_ref=pi[i], dst_ref=po[i], send_sem=ps[0].at[i], recv_sem=ps[1].at[i],
            device_id=(x, y, 1 - c), device_id_type=MESH) for i in range(len(halves))]

    return _start_wait_plan(list(halves), [jax.ShapeDtypeStruct(h.shape, F32) for h in halves], len(halves), copies)


def _pad_lanes(v, width=D_MODEL):
    return jnp.pad(v, ((0, 0), (0, width - v.shape[1])))


def _pack_small(b_ada, norm1, norm2, lb, o_gain, q_gain, k_gain, sinks):
    rows = [b_ada.reshape(N_MOD, D_MODEL), norm1, norm2, jnp.concatenate([lb[0:1], o_gain], axis=1),
            _pad_lanes(jnp.concatenate([q_gain, k_gain, sinks], axis=1)), _pad_lanes(lb[1:2]),
            jnp.zeros((SMALL_ROWS - 11, D_MODEL), F32)]
    return jnp.concatenate(rows, axis=0)


def _unpack_small(p):
    return (p[0:6].reshape(1, N_MOD * D_MODEL), p[6:7], p[7:8],
            jnp.concatenate([p[8:9, 0:A_WIDTH], p[10:11, 0:A_WIDTH]], axis=0), p[8:9, A_WIDTH:],
            p[9:10, 0:64], p[9:10, 64:128], p[9:10, 128:144])


def kernel(x, c, w_ada, b_ada, norm1_gain, w_in, lb_logits, hgrn_o_gain, q_norm_gain, k_norm_gain, sinks, w_branch_a, w_branch_b, w_out, norm2_gain, w_mlp_in, w_mlp_out, loss_target, m_w_ada, m_b_ada, m_norm1_gain, m_w_in, m_lb_logits, m_hgrn_o_gain, m_q_norm_gain, m_k_norm_gain, m_sinks, m_w_branch_a, m_w_branch_b, m_w_out, m_norm2_gain, m_w_mlp_in, m_w_mlp_out, v_w_ada, v_b_ada, v_norm1_gain, v_w_in, v_lb_logits, v_hgrn_o_gain, v_q_norm_gain, v_k_norm_gain, v_sinks, v_w_branch_a, v_w_branch_b, v_w_out, v_norm2_gain, v_w_mlp_in, v_w_mlp_out):
    xi, yi, ci = _place()
    chip = 2 * xi + yi
    me = 4 * xi + 2 * yi + ci
    ada_cols = w_ada.shape[2]

    c_all = _allgather_small(jnp.broadcast_to(c, (8, D_MODEL)), "gather_c").reshape(N_DEV, 8, D_MODEL)[:, 0]
    b_cols = lax.dynamic_slice(b_ada, (0, chip * ada_cols), (1, ada_cols))
    mod_part = _ada_fwd(c_all, w_ada[0], b_cols)
    mod_all = _allgather_small(mod_part, "gather_mod").reshape(N_CHIPS, 2, N_DEV, ada_cols)[:, 0]
    mod_mine = lax.dynamic_index_in_dim(mod_all, me, axis=1, keepdims=False).reshape(N_MOD, D_MODEL)
    mod8 = jnp.concatenate([mod_mine, jnp.zeros((2, D_MODEL), F32)], axis=0)

    shards = (w_in[0], w_branch_a[0], w_branch_b[0], w_out[0], w_mlp_in[0], w_mlp_out[0])
    chip_arr = chip.astype(jnp.int32).reshape(1)
    c_arr = ci.astype(jnp.int32).reshape(1)
    parts = [_cast_into_full(s, w, chip_arr, f"cast_w{w}") for w, s in enumerate(shards)]

    grad_x, half, theirs, pair_in, st = _local_step(x[0], loss_target[0], mod8, norm1_gain, norm2_gain, lb_logits,
                                                    hgrn_o_gain, q_norm_gain, k_norm_gain, sinks, parts, c_arr, chip_arr)
    loss = lax.psum(0.5 * jnp.sum(st["loss"][0]) / D_MODEL, ("x", "y", "c"))
    moments = ((m_w_in, v_w_in), (m_w_branch_a, v_w_branch_a), (m_w_branch_b, v_w_branch_b), (m_w_out, v_w_out),
               (m_w_mlp_in, v_w_mlp_in), (m_w_mlp_out, v_w_mlp_out))
    rest, (slots_in,) = _adamw_multi(
        [(shards[w], half[w], theirs[w], moments[w][0][0], moments[w][1][0]) for w in range(1, N_W)], "adamw_rest",
        plan=_chip_exchange_plan({W_IN: pair_in}))
    half_in = _sum_slots(pair_in, slots_in, W_IN, chip_arr, "sum_slots0")
    (their_in,) = _run_plan(_sibling_share_plan([half_in]), "sibling_share_w_in")
    big = [_adamw_halves(shards[0], half_in, their_in, m_w_in[0], v_w_in[0], c_arr, "adamw0")] + rest

    swa = st["swa"]
    small = jnp.concatenate([
        st["n1"][1:2], st["n1"][0:1], st["n2"][3:4], st["n2"][1:2], st["n2"][0:1], st["loss"][1:2],
        st["n1"][2:3], st["n2"][2:3], jnp.concatenate([st["d_lb"][0:1], st["d_og"][0:1]], axis=1),
        _pad_lanes(jnp.concatenate([swa[0:1, 0:64], swa[1:2, 0:64], swa[2:3, 0:16]], axis=1)),
        jnp.zeros((SMALL_ROWS - 10, D_MODEL), F32)], axis=0)
    small_all = _allgather_small(small, "gather_small")
    g_small = _small_sum(small_all, lb_logits)
    small_w = (b_ada, norm1_gain, norm2_gain, lb_logits, hgrn_o_gain, q_norm_gain, k_norm_gain, sinks)
    small_m = (m_b_ada, m_norm1_gain, m_norm2_gain, m_lb_logits, m_hgrn_o_gain, m_q_norm_gain, m_k_norm_gain, m_sinks)
    small_v = (v_b_ada, v_norm1_gain, v_norm2_gain, v_lb_logits, v_hgrn_o_gain, v_q_norm_gain, v_k_norm_gain, v_sinks)
    sm = [_unpack_small(t) for t in
          (g_small,) + tuple(_adamw(_pack_small(*small_w), g_small, _pack_small(*small_m), _pack_small(*small_v),
                                    "adamw_small"))]
    g_b, g_n1, g_n2, g_lb, g_og, g_qg, g_kg, g_sk = ([t[i] for t in sm] for i in range(8))

    dmod_all = small_all.reshape(N_DEV, SMALL_ROWS, D_MODEL)[:, 0:N_MOD].reshape(N_DEV, N_MOD * D_MODEL)
    dmod_cols = lax.dynamic_slice(dmod_all, (0, chip * ada_cols), (N_DEV, ada_cols))
    ada = _ada_grad_adamw(c_all.T, dmod_cols, w_ada[0], m_w_ada[0], v_w_ada[0])

    def ordered(k):
        lead = lambda a: a[None]
        return (lead(ada[k]), g_b[k], g_n1[k], lead(big[0][k]), g_lb[k], g_og[k], g_qg[k], g_kg[k], g_sk[k],
                lead(big[1][k]), lead(big[2][k]), lead(big[3][k]), g_n2[k], lead(big[4][k]), lead(big[5][k]))

    return (loss, grad_x[None]) + ordered(0) + ordered(1) + ordered(2) + ordered(3)
```

```python
import functools

import jax
import jax.numpy as jnp
from jax import lax
from jax.experimental import pallas as pl
from jax.experimental.pallas import tpu as pltpu

F32 = jnp.float32
BF16 = jnp.bfloat16
HIGHEST = lax.Precision.HIGHEST
MESH = pl.DeviceIdType.MESH

D_MODEL = 2048
A_WIDTH = 1024
A_HEADS = 8
A_HEAD_DIM = 128
A_CHUNK = 64
B_WIDTH = 1024
B_HEAD_DIM = 64
B_GROUP = 4
B_KV_HEADS = 4
B_KV_WIDTH = 256
BLOCK = 128
MLP_HIDDEN = 8192
IN_WIDTH = 9728
N_MOD = 6
EPS = 1e-6
N_CHIPS = 4
N_DEV = 8

OFF_QA, OFF_FA, OFF_IA, OFF_GA = 0, 1024, 2048, 3072
OFF_QB, OFF_KB, OFF_VB = 4096, 5120, 5376
OFF_GATE_A, OFF_GATE_B = 5632, 7680

ADAM_LR = 0.001
ADAM_B1 = 0.9
ADAM_B2 = 0.999
ADAM_EPS = 1e-08
ADAM_WD = 0.01
ADAM_STEP = 10

VMEM_LIMIT_V7X = 48 * 1024 * 1024
NEG_BIG = -1e30


def _params(sem=None, vmem=VMEM_LIMIT_V7X):
    return pltpu.CompilerParams(dimension_semantics=sem, vmem_limit_bytes=vmem)


class _Plan:
    def __init__(self, ins, outs, sems, stages, aliases=None, mid_at=()):
        self.ins, self.outs, self.sems, self.stages, self.aliases = ins, outs, sems, stages, aliases or {}
        self.mid_at = tuple(mid_at)
        assert len(self.mid_at) == len(stages) - 2


def _join(a, b):
    assert len(a.stages) == 2 and len(b.stages) == 2
    ni, no, ns = len(a.ins), len(a.outs), len(a.sems)

    def stage(k):
        def run(pi, po, ps):
            a.stages[k](pi[:ni], po[:no], ps[:ns])
            b.stages[k](pi[ni:], po[no:], ps[ns:])
        return run

    aliases = dict(a.aliases)
    aliases.update({ni + i: no + o for i, o in b.aliases.items()})
    return _Plan(a.ins + b.ins, a.outs + b.outs, a.sems + b.sems, [stage(0), stage(1)], aliases)


def _pcall(body, plan=None, **kw):
    if plan is None:
        return pl.pallas_call(body, **kw)
    grid = kw["grid"]
    single = not isinstance(kw["out_specs"], (list, tuple))
    in_specs = list(kw["in_specs"])
    out_specs = [kw["out_specs"]] if single else list(kw["out_specs"])
    out_shape = [kw["out_shape"]] if single else list(kw["out_shape"])
    scratch = list(kw.get("scratch_shapes", ()))
    n_in, n_out, n_scr = len(in_specs), len(out_specs), len(scratch)
    n_pi, n_po = len(plan.ins), len(plan.outs)
    total = 1
    for g in grid:
        total *= g
    n_st = len(plan.stages)

    def wrapped(*refs):
        o0 = n_in + n_pi
        s0 = o0 + n_out + n_po
        pi, po, ps = refs[n_in:o0], refs[o0 + n_out:s0], refs[s0 + n_scr:]
        lin = 0
        for d, g in enumerate(grid):
            lin = lin * g + pl.program_id(d)
        for si, frac in enumerate((0.0,) + plan.mid_at):
            @pl.when(lin == int(frac * (total - 1)))
            def _(si=si):
                plan.stages[si](pi, po, ps)
        body(*refs[:n_in], *refs[o0:o0 + n_out], *refs[s0:s0 + n_scr])

        @pl.when(lin == total - 1)
        def _():
            plan.stages[-1](pi, po, ps)

    any_spec = pl.BlockSpec(memory_space=pl.ANY)
    call = pl.pallas_call(
        wrapped, name=kw["name"], grid=grid, in_specs=in_specs + [any_spec] * n_pi,
        out_specs=out_specs + [any_spec] * n_po, out_shape=out_shape + list(plan.outs),
        scratch_shapes=scratch + list(plan.sems),
        input_output_aliases={n_in + i: n_out + o for i, o in plan.aliases.items()},
        compiler_params=_params(("arbitrary",) * len(grid)))

    def run(*args):
        res = call(*args, *plan.ins)
        outs = list(res[:n_out])
        return (outs[0] if single else outs), list(res[n_out:])

    return run


def _run_plan(plan, name):
    return _pcall(lambda: None, plan=plan, name=name, grid=(1,), in_specs=[], out_specs=[], out_shape=[])()[1]


def _sig(x):
    return 1.0 / (1.0 + jnp.exp(-x))


def _nn(a, b):
    return lax.dot_general(a.astype(BF16), b.astype(BF16), (((1,), (0,)), ((), ())), preferred_element_type=F32)


def _nt(a, b):
    return lax.dot_general(a.astype(BF16), b.astype(BF16), (((1,), (1,)), ((), ())), preferred_element_type=F32)


def _tn(a, b):
    return lax.dot_general(a.astype(BF16), b.astype(BF16), (((0,), (0,)), ((), ())), preferred_element_type=F32)


def _mm(a, b, *, name, ta=False, tb=False, bm=1024, bn=1024, bk=2048, out_dtypes=(F32,), epi=None, extras=(), plan=None):
    if ta:
        K, M = a.shape
    else:
        M, K = a.shape
    if tb:
        N, K2 = b.shape
    else:
        K2, N = b.shape
    bm, bn, bk = min(bm, M), min(bn, N), min(bk, K)
    assert K == K2 and M % bm == 0 and N % bn == 0 and K % bk == 0, (name, a.shape, b.shape)
    nk = K // bk
    a_spec = pl.BlockSpec((bk, bm), lambda i, j, k: (k, i)) if ta else pl.BlockSpec((bm, bk), lambda i, j, k: (i, k))
    b_spec = pl.BlockSpec((bn, bk), lambda i, j, k: (j, k)) if tb else pl.BlockSpec((bk, bn), lambda i, j, k: (k, j))
    t_spec = pl.BlockSpec((bm, bn), lambda i, j, k: (i, j))
    dims = (((0 if ta else 1,), (1 if tb else 0,)), ((), ()))
    n_e, n_o = len(extras), len(out_dtypes)

    def body(*refs):
        a_ref, b_ref = refs[0], refs[1]
        e_refs = refs[2:2 + n_e]
        o_refs = refs[2 + n_e:2 + n_e + n_o]

        def finish(acc):
            outs = (acc,) if epi is None else epi(acc, *[e[...] for e in e_refs])
            for o_ref, o in zip(o_refs, outs):
                o_ref[...] = o.astype(o_ref.dtype)

        part = lax.dot_general(a_ref[...].astype(BF16), b_ref[...].astype(BF16), dims, preferred_element_type=F32)
        if nk == 1:
            finish(part)
        else:
            acc_ref = refs[-1]
            k = pl.program_id(2)

            @pl.when(k == 0)
            def _():
                acc_ref[...] = part

            @pl.when(k > 0)
            def _():
                acc_ref[...] += part

            @pl.when(k == nk - 1)
            def _():
                finish(acc_ref[...])

    out = _pcall(
        body, plan=plan, name=name, grid=(M // bm, N // bn, nk),
        in_specs=[a_spec, b_spec] + [t_spec] * n_e,
        out_specs=[t_spec] * n_o,
        out_shape=[jax.ShapeDtypeStruct((M, N), dt) for dt in out_dtypes],
        scratch_shapes=[pltpu.VMEM((bm, bn), F32)] if nk > 1 else [],
        compiler_params=_params(("parallel", "parallel", "arbitrary")),
    )(a, b, *extras)
    if plan is not None:
        return (out[0][0] if n_o == 1 else out[0]), out[1]
    return out[0] if n_o == 1 else out


def _ada_fwd(c_all, w_ada, b_cols):
    n = w_ada.shape[1]
    bn = 512

    def body(c_ref, w_ref, b_ref, o_ref):
        cv = c_ref[...]
        sc = cv * _sig(cv)
        o_ref[...] = jnp.dot(sc, w_ref[...], precision=HIGHEST, preferred_element_type=F32) + b_ref[...]

    return _pcall(
        body, name="ada_fwd", grid=(n // bn,),
        in_specs=[pl.BlockSpec((N_DEV, D_MODEL), lambda j: (0, 0)), pl.BlockSpec((D_MODEL, bn), lambda j: (0, j)),
                  pl.BlockSpec((1, bn), lambda j: (0, j))],
        out_specs=pl.BlockSpec((N_DEV, bn), lambda j: (0, j)),
        out_shape=jax.ShapeDtypeStruct((N_DEV, n), F32),
        compiler_params=_params(("parallel",)),
    )(c_all, w_ada, b_cols)


ROWS_EW = 256


def _rms_fwd_math(x, gain, scale, shift):
    rstd = lax.rsqrt(jnp.mean(x * x, axis=-1, keepdims=True) + EPS)
    xhat = x * rstd
    n = xhat * gain
    return n * (1.0 + scale) + shift, xhat, n, rstd


def _rms_bwd_math(dh, xhat, n, rstd, gain, scale):
    dn = dh * (1.0 + scale)
    dxhat = dn * gain
    dx = rstd * (dxhat - xhat * jnp.mean(dxhat * xhat, axis=-1, keepdims=True))
    d_scale = jnp.sum(dh * n, axis=0, keepdims=True)
    d_shift = jnp.sum(dh, axis=0, keepdims=True)
    d_gain = jnp.sum(dn * xhat, axis=0, keepdims=True)
    return dx, d_scale, d_shift, d_gain


def _row_spec(w=D_MODEL, br=ROWS_EW):
    return pl.BlockSpec((br, w), lambda i: (i, 0))


def _vec_spec(r=8, w=D_MODEL):
    return pl.BlockSpec((r, w), lambda i: (0, 0))


def _norm1_fwd(x, gain, mod8, plan=None):
    T = x.shape[0]

    def body(x_ref, g_ref, m_ref, h_ref):
        h, _, _, _ = _rms_fwd_math(x_ref[...], g_ref[...], m_ref[1:2, :], m_ref[0:1, :])
        h_ref[...] = h.astype(BF16)

    return _pcall(
        body, plan=plan, name="norm1_fwd", grid=(T // ROWS_EW,),
        in_specs=[_row_spec(), _vec_spec(1), _vec_spec()],
        out_specs=_row_spec(), out_shape=jax.ShapeDtypeStruct((T, D_MODEL), BF16),
        compiler_params=_params(("parallel",)),
    )(x, gain, mod8)


def _res_norm2_fwd(x, mo, gain, mod8):
    T = x.shape[0]

    def body(x_ref, mo_ref, g_ref, m_ref, x1_ref, h_ref):
        x1 = x_ref[...] + m_ref[2:3, :] * mo_ref[...]
        x1_ref[...] = x1
        h, _, _, _ = _rms_fwd_math(x1, g_ref[...], m_ref[4:5, :], m_ref[3:4, :])
        h_ref[...] = h.astype(BF16)

    return _pcall(
        body, name="res_norm2_fwd", grid=(T // ROWS_EW,),
        in_specs=[_row_spec(), _row_spec(), _vec_spec(1), _vec_spec()],
        out_specs=[_row_spec(), _row_spec()],
        out_shape=[jax.ShapeDtypeStruct((T, D_MODEL), F32), jax.ShapeDtypeStruct((T, D_MODEL), BF16)],
        compiler_params=_params(("parallel",)),
    )(x, mo, gain, mod8)


def _loss_bwd(x1, mlp, target, mod8):
    T = x1.shape[0]

    def body(x1_ref, mlp_ref, t_ref, m_ref, dy_ref, dmlp_ref, st_ref):
        i = pl.program_id(0)
        gate = m_ref[5:6, :]
        mlp_v = mlp_ref[...]
        err = x1_ref[...] + gate * mlp_v - t_ref[...]
        dy = err * (1.0 / D_MODEL)
        dy_ref[...] = dy
        dmlp_ref[...] = (dy * gate).astype(BF16)

        @pl.when(i == 0)
        def _():
            st_ref[...] = jnp.zeros_like(st_ref)

        st_ref[0:1, :] += jnp.sum(err * err, axis=0, keepdims=True)
        st_ref[1:2, :] += jnp.sum(dy * mlp_v, axis=0, keepdims=True)

    return _pcall(
        body, name="loss_bwd", grid=(T // ROWS_EW,),
        in_specs=[_row_spec(), _row_spec(), _row_spec(), _vec_spec()],
        out_specs=[_row_spec(), _row_spec(), _vec_spec()],
        out_shape=[jax.ShapeDtypeStruct((T, D_MODEL), F32), jax.ShapeDtypeStruct((T, D_MODEL), BF16),
                   jax.ShapeDtypeStruct((8, D_MODEL), F32)],
        compiler_params=_params(("arbitrary",)),
    )(x1, mlp, target, mod8)


def _norm2_bwd(dh2, x1, dy, mo, gain, mod8):
    T = x1.shape[0]

    def body(dh_ref, x1_ref, dy_ref, mo_ref, g_ref, m_ref, dx1_ref, dmo_ref, st_ref):
        i = pl.program_id(0)
        gain_v, scale = g_ref[...], m_ref[4:5, :]
        _, xhat, n, rstd = _rms_fwd_math(x1_ref[...], gain_v, scale, m_ref[3:4, :])
        dx, d_scale, d_shift, d_gain = _rms_bwd_math(dh_ref[...], xhat, n, rstd, gain_v, scale)
        dx1 = dy_ref[...] + dx
        dx1_ref[...] = dx1
        dmo_ref[...] = (dx1 * m_ref[2:3, :]).astype(BF16)

        @pl.when(i == 0)
        def _():
            st_ref[...] = jnp.zeros_like(st_ref)

        st_ref[0:1, :] += d_scale
        st_ref[1:2, :] += d_shift
        st_ref[2:3, :] += d_gain
        st_ref[3:4, :] += jnp.sum(dx1 * mo_ref[...], axis=0, keepdims=True)

    return _pcall(
        body, name="norm2_bwd", grid=(T // ROWS_EW,),
        in_specs=[_row_spec(), _row_spec(), _row_spec(), _row_spec(), _vec_spec(1), _vec_spec()],
        out_specs=[_row_spec(), _row_spec(), _vec_spec()],
        out_shape=[jax.ShapeDtypeStruct((T, D_MODEL), F32), jax.ShapeDtypeStruct((T, D_MODEL), BF16),
                   jax.ShapeDtypeStruct((8, D_MODEL), F32)],
        compiler_params=_params(("arbitrary",)),
    )(dh2, x1, dy, mo, gain, mod8)


def _norm1_bwd(dh, x, dx1, gain, mod8):
    T = x.shape[0]

    def body(dh_ref, x_ref, dx1_ref, g_ref, m_ref, dx_ref, st_ref):
        i = pl.program_id(0)
        gain_v, scale = g_ref[...], m_ref[1:2, :]
        _, xhat, n, rstd = _rms_fwd_math(x_ref[...], gain_v, scale, m_ref[0:1, :])
        dx, d_scale, d_shift, d_gain = _rms_bwd_math(dh_ref[...], xhat, n, rstd, gain_v, scale)
        dx_ref[...] = dx1_ref[...] + dx

        @pl.when(i == 0)
        def _():
            st_ref[...] = jnp.zeros_like(st_ref)

        st_ref[0:1, :] += d_scale
        st_ref[1:2, :] += d_shift
        st_ref[2:3, :] += d_gain

    return _pcall(
        body, name="norm1_bwd", grid=(T // ROWS_EW,),
        in_specs=[_row_spec(), _row_spec(), _row_spec(), _vec_spec(1), _vec_spec()],
        out_specs=[_row_spec(), _vec_spec()],
        out_shape=[jax.ShapeDtypeStruct((T, D_MODEL), F32), jax.ShapeDtypeStruct((8, D_MODEL), F32)],
        compiler_params=_params(("arbitrary",)),
    )(dh, x, dx1, gain, mod8)


MERGE_BC = 512


def _merge_specs():
    ga = pl.BlockSpec((ROWS_EW, MERGE_BC), lambda i, j: (i, OFF_GATE_A // MERGE_BC + j))
    gb = pl.BlockSpec((ROWS_EW, MERGE_BC), lambda i, j: (i, OFF_GATE_B // MERGE_BC + j))
    t = pl.BlockSpec((ROWS_EW, MERGE_BC), lambda i, j: (i, j))
    return ga, gb, t


def _merge_fwd(proj, ya, yb):
    T = proj.shape[0]
    ga, gb, t = _merge_specs()

    def body(ga_ref, gb_ref, ya_ref, yb_ref, o_ref):
        o_ref[...] = (_sig(ga_ref[...]) * ya_ref[...] + _sig(gb_ref[...]) * yb_ref[...]).astype(BF16)

    return _pcall(
        body, name="merge_fwd", grid=(T // ROWS_EW, D_MODEL // MERGE_BC),
        in_specs=[ga, gb, t, t], out_specs=t, out_shape=jax.ShapeDtypeStruct((T, D_MODEL), BF16),
        compiler_params=_params(("parallel", "parallel")),
    )(proj, proj, ya, yb)


def _merge_bwd(proj, ya, yb, dmerged):
    T = proj.shape[0]
    ga, gb, t = _merge_specs()

    def body(ga_ref, gb_ref, ya_ref, yb_ref, dm_ref, dya_ref, dyb_ref, dga_ref, dgb_ref):
        dm = dm_ref[...]
        sa, sb = _sig(ga_ref[...]), _sig(gb_ref[...])
        dya_ref[...] = (dm * sa).astype(BF16)
        dyb_ref[...] = (dm * sb).astype(BF16)
        dga_ref[...] = (dm * ya_ref[...] * sa * (1.0 - sa)).astype(BF16)
        dgb_ref[...] = (dm * yb_ref[...] * sb * (1.0 - sb)).astype(BF16)

    sh = jax.ShapeDtypeStruct((T, D_MODEL), BF16)
    return _pcall(
        body, name="merge_bwd", grid=(T // ROWS_EW, D_MODEL // MERGE_BC),
        in_specs=[ga, gb, t, t, t], out_specs=[t, t, t, t], out_shape=[sh, sh, sh, sh],
        compiler_params=_params(("parallel", "parallel")),
    )(proj, proj, ya, yb, dmerged)


def _hgrn_rows(T):
    return 512 if T >= 1024 else 128


def _lower_bound(lbl):
    e = jnp.exp(lbl - jnp.max(lbl, axis=0, keepdims=True))
    return e[0:1, :] / (e[0:1, :] + e[1:2, :])


def _chunk_sum_matrix(rows, backward):
    shift = A_CHUNK.bit_length() - 1
    r = lax.broadcasted_iota(jnp.int32, (rows, rows), 0)
    c = lax.broadcasted_iota(jnp.int32, (rows, rows), 1)
    same = jnp.right_shift(r, shift) == jnp.right_shift(c, shift)
    return (same & ((r <= c) if backward else (r >= c))).astype(BF16)


def _chunk_sums(m, x):
    n = x.shape[1]
    hi = x.astype(BF16)
    rest = x - hi.astype(F32)
    mid = rest.astype(BF16)
    lo = (rest - mid.astype(F32)).astype(BF16)
    y = jnp.dot(m, jnp.concatenate([hi, mid, lo], axis=1), preferred_element_type=F32)
    return y[:, 0:n] + y[:, n:2 * n] + y[:, 2 * n:3 * n]


def _hgrn_block_pre(q, fl, lb, m_fwd):
    sg = _sig(fl)
    f = lb + (1.0 - lb) * sg
    sq = _sig(q)
    return dict(sg=sg, f=f, k=1.0 - f, sq=sq, qf=q * sq, b=_chunk_sums(m_fwd, jnp.log(f)))


def _hgrn_chunk_fwd(pre, r, v, st):
    C = A_CHUNK
    qf, k, b = pre["qf"][r], pre["k"][r], pre["b"][r]
    causal = lax.broadcasted_iota(jnp.int32, (C, C), 0) >= lax.broadcasted_iota(jnp.int32, (C, C), 1)
    bm = b[C // 2 - 1:C // 2, :]
    bl = b[C - 1:C, :]
    e_q, e_k = jnp.exp(b - bm), jnp.exp(bm - b)
    e_b, e_l = jnp.exp(b), jnp.exp(bl - b)
    qd, kd = qf * e_q, k * e_k
    qe, ke = qf * e_b, k * e_l
    att = jnp.where(causal, _nt(qd, kd), 0.0)
    o = _nn(att, v) + _nt(qe, st)
    dec = jnp.exp(bl)
    st_next = st * dec + _tn(v, ke)
    return dict(causal=causal, e_q=e_q, e_k=e_k, e_b=e_b, e_l=e_l, qd=qd, kd=kd,
                qe=qe, ke=ke, att=att, o=o, dec=dec, st_next=st_next)


HGRN_HEADS_PER_STEP = 4


def _hgrn_fwd(proj, lb_logits, o_gain, plan=None):
    T = proj.shape[0]
    BR = _hgrn_rows(T)
    cps = BR // A_CHUNK
    K, NH = A_HEAD_DIM, HGRN_HEADS_PER_STEP
    W = NH * K

    def col(off):
        return pl.BlockSpec((BR, W), lambda h, cb: (cb, off // W + h))

    def body(q_ref, f_ref, i_ref, g_ref, lbl_ref, og_ref, o_ref, s_ref, st):
        @pl.when(pl.program_id(1) == 0)
        def _():
            st[...] = jnp.zeros_like(st)

        lb_all = _lower_bound(lbl_ref[...])
        m_fwd = _chunk_sum_matrix(BR, False)
        pre = [_hgrn_block_pre(q_ref[:, n * K:(n + 1) * K], f_ref[:, n * K:(n + 1) * K], lb_all[:, n * K:(n + 1) * K], m_fwd)
               for n in range(NH)]
        state = [st[n] for n in range(NH)]
        for ci in range(cps):
            r = slice(ci * A_CHUNK, (ci + 1) * A_CHUNK)
            for n in range(NH):
                hs = slice(n * K, (n + 1) * K)
                s_ref[n, ci] = state[n]
                c = _hgrn_chunk_fwd(pre[n], r, i_ref[r, hs], state[n])
                state[n] = c["st_next"]
                o = c["o"]
                on = o * lax.rsqrt(jnp.mean(o * o, axis=-1, keepdims=True) + EPS)
                g = g_ref[r, hs]
                o_ref[r, hs] = (on * og_ref[:, hs] * (g * _sig(g))).astype(BF16)
        for n in range(NH):
            st[n] = state[n]

    return _pcall(
        body, plan=plan, name="hgrn_fwd", grid=(A_HEADS // NH, T // BR),
        in_specs=[col(OFF_QA), col(OFF_FA), col(OFF_IA), col(OFF_GA),
                  pl.BlockSpec((2, W), lambda h, cb: (0, h)), pl.BlockSpec((1, W), lambda h, cb: (0, h))],
        out_specs=[pl.BlockSpec((BR, W), lambda h, cb: (cb, h)),
                   pl.BlockSpec((NH, cps, K, K), lambda h, cb: (h, cb, 0, 0))],
        out_shape=[jax.ShapeDtypeStruct((T, A_WIDTH), BF16),
                   jax.ShapeDtypeStruct((A_HEADS, T // A_CHUNK, K, K), F32)],
        scratch_shapes=[pltpu.VMEM((NH, K, K), F32)],
        compiler_params=_params(("parallel", "arbitrary")),
    )(proj, proj, proj, proj, lb_logits, o_gain)


def _hgrn_bwd(proj, lb_logits, o_gain, states, do, plan=None):
    T = proj.shape[0]
    BR = _hgrn_rows(T)
    cps = BR // A_CHUNK
    ncb = T // BR
    K, C, NH = A_HEAD_DIM, A_CHUNK, HGRN_HEADS_PER_STEP
    W = NH * K

    def col(off):
        return pl.BlockSpec((BR, W), lambda h, cb: (ncb - 1 - cb, off // W + h))

    def body(q_ref, f_ref, i_ref, g_ref, lbl_ref, og_ref, s_ref, do_ref,
             dq_ref, df_ref, di_ref, dg_ref, dlb_ref, dog_ref, dst):
        @pl.when(pl.program_id(1) == 0)
        def _():
            dst[...] = jnp.zeros_like(dst)
            dlb_ref[...] = jnp.zeros_like(dlb_ref)
            dog_ref[...] = jnp.zeros_like(dog_ref)

        lb_all = _lower_bound(lbl_ref[...])
        row = lax.broadcasted_iota(jnp.int32, (C, K), 0)
        m_fwd, m_bwd = _chunk_sum_matrix(BR, False), _chunk_sum_matrix(BR, True)
        pre = [_hgrn_block_pre(q_ref[:, n * K:(n + 1) * K], f_ref[:, n * K:(n + 1) * K], lb_all[:, n * K:(n + 1) * K], m_fwd)
               for n in range(NH)]
        d_state = [dst[n] for n in range(NH)]
        d_og = [jnp.zeros((1, K), F32) for _ in range(NH)]
        db_of = [[None] * cps for _ in range(NH)]
        dk_of = [[None] * cps for _ in range(NH)]
        for ci in reversed(range(cps)):
            r = slice(ci * C, (ci + 1) * C)
            for n in range(NH):
                hs = slice(n * K, (n + 1) * K)
                gain = og_ref[:, hs]
                st = s_ref[n, ci]
                v = i_ref[r, hs]
                q = q_ref[r, hs]
                c = _hgrn_chunk_fwd(pre[n], r, v, st)
                dst_next = d_state[n]
                o = c["o"]
                rn = lax.rsqrt(jnp.mean(o * o, axis=-1, keepdims=True) + EPS)
                on = o * rn
                g = g_ref[r, hs]
                sgg = _sig(g)
                dy = do_ref[r, hs]
                d_ong = dy * (g * sgg)
                dg_ref[r, hs] = (dy * (on * gain) * (sgg * (1.0 + g * (1.0 - sgg)))).astype(BF16)
                d_og[n] = d_og[n] + jnp.sum(d_ong * on, axis=0, keepdims=True)
                d_on = d_ong * gain
                d_o = rn * (d_on - on * jnp.mean(d_on * on, axis=-1, keepdims=True))
                datt = jnp.where(c["causal"], _nt(d_o, v), 0.0)
                dv = _tn(c["att"], d_o) + _nt(c["ke"], dst_next)
                dqd = _nn(datt, c["kd"])
                dkd = _tn(datt, c["qd"])
                dqe = _nn(d_o, st)
                dke = _nn(v, dst_next)
                d_state[n] = dst_next * c["dec"] + _tn(d_o, c["qe"])
                d_dec = jnp.sum(dst_next * st, axis=0, keepdims=True)
                t_q, t_k = dqd * c["qd"], dkd * c["kd"]
                t_e, t_l = dqe * c["qe"], dke * c["ke"]
                db = t_q - t_k + t_e - t_l
                dbm = jnp.sum(t_k - t_q, axis=0, keepdims=True)
                dbl = jnp.sum(t_l, axis=0, keepdims=True) + d_dec * c["dec"]
                db_of[n][ci] = db + jnp.where(row == C // 2 - 1, dbm, 0.0) + jnp.where(row == C - 1, dbl, 0.0)
                dqf = dqd * c["e_q"] + dqe * c["e_b"]
                sq = pre[n]["sq"][r]
                dq_ref[r, hs] = (dqf * (sq * (1.0 + q * (1.0 - sq)))).astype(BF16)
                dk_of[n][ci] = dkd * c["e_k"] + dke * c["e_l"]
                di_ref[r, hs] = dv.astype(BF16)
        for n in range(NH):
            hs = slice(n * K, (n + 1) * K)
            dst[n] = d_state[n]
            dog_ref[0:1, hs] += d_og[n]
            lb, sg = lb_all[:, hs], pre[n]["sg"]
            dlf = _chunk_sums(m_bwd, jnp.concatenate(db_of[n], axis=0))
            df = dlf / pre[n]["f"] - jnp.concatenate(dk_of[n], axis=0)
            df_ref[:, hs] = (df * (1.0 - lb) * sg * (1.0 - sg)).astype(BF16)
            dlb_ref[0:1, hs] += jnp.sum(df * (1.0 - sg), axis=0, keepdims=True)

    ocol = pl.BlockSpec((BR, W), lambda h, cb: (ncb - 1 - cb, h))
    vec = pl.BlockSpec((8, W), lambda h, cb: (0, h))
    return _pcall(
        body, plan=plan, name="hgrn_bwd", grid=(A_HEADS // NH, ncb),
        in_specs=[col(OFF_QA), col(OFF_FA), col(OFF_IA), col(OFF_GA),
                  pl.BlockSpec((2, W), lambda h, cb: (0, h)), pl.BlockSpec((1, W), lambda h, cb: (0, h)),
                  pl.BlockSpec((NH, cps, K, K), lambda h, cb: (h, ncb - 1 - cb, 0, 0)),
                  pl.BlockSpec((BR, W), lambda h, cb: (ncb - 1 - cb, h))],
        out_specs=[ocol, ocol, ocol, ocol, vec, vec],
        out_shape=[jax.ShapeDtypeStruct((T, A_WIDTH), BF16)] * 4 + [jax.ShapeDtypeStruct((8, A_WIDTH), F32)] * 2,
        scratch_shapes=[pltpu.VMEM((NH, K, K), F32)],
        compiler_params=_params(("parallel", "arbitrary")),
    )(proj, proj, proj, proj, lb_logits, o_gain, states, do)


def _head_norm(x):
    r = lax.rsqrt(jnp.mean(x * x, axis=-1, keepdims=True) + EPS)
    return x * r, r


def _head_norm_bwd(dy, xn, r, gain):
    dxn = dy * gain
    return r * (dxn - xn * jnp.mean(dxn * xn, axis=-1, keepdims=True)), jnp.sum(dy * xn, axis=0, keepdims=True)


def _swa_mask(has_prev):
    rows = B_GROUP * BLOCK
    r = lax.broadcasted_iota(jnp.int32, (rows, 2 * BLOCK), 0) % BLOCK
    c = lax.broadcasted_iota(jnp.int32, (rows, 2 * BLOCK), 1)
    rel = r + BLOCK - c
    return (rel >= 0) & (rel < BLOCK) & ((c >= BLOCK) | has_prev)


def _swa_head_fwd(j, q_ref, kp_ref, kc_ref, vp_ref, vc_ref, qg, kg, sk_ref, mask):
    hs = slice(j * B_HEAD_DIM, (j + 1) * B_HEAD_DIM)
    kcat = jnp.concatenate([kp_ref[:, hs], kc_ref[:, hs]], axis=0)
    vcat = jnp.concatenate([vp_ref[:, hs], vc_ref[:, hs]], axis=0)
    qs = jnp.concatenate([q_ref[:, pl.ds((j * B_GROUP + g) * B_HEAD_DIM, B_HEAD_DIM)] for g in range(B_GROUP)], axis=0)
    kn, kr = _head_norm(kcat)
    qn, qr = _head_norm(qs)
    kh, qh = kn * kg, qn * qg
    s = jnp.where(mask, _nt(qh, kh) * (B_HEAD_DIM ** -0.5), NEG_BIG)
    sink = jnp.concatenate(
        [jnp.broadcast_to(sk_ref[0:1, pl.ds(j * B_GROUP + g, 1)], (BLOCK, 1)) for g in range(B_GROUP)], axis=0)
    m = jnp.maximum(jnp.max(s, axis=-1, keepdims=True), sink)
    p = jnp.exp(s - m)
    e_sink = jnp.exp(sink - m)
    inv = 1.0 / (jnp.sum(p, axis=-1, keepdims=True) + e_sink)
    prob = p * inv
    return dict(vcat=vcat, kn=kn, kr=kr, qn=qn, qr=qr, kh=kh, qh=qh, prob=prob, p_sink=e_sink * inv)


def _swa_in_specs(nb, last):
    def qi(n):
        return jnp.minimum(n, last)

    q = pl.BlockSpec((BLOCK, B_WIDTH), lambda n: (qi(n), OFF_QB // B_WIDTH))
    kc = pl.BlockSpec((BLOCK, B_KV_WIDTH), lambda n: (qi(n), OFF_KB // B_KV_WIDTH))
    kp = pl.BlockSpec((BLOCK, B_KV_WIDTH), lambda n: (jnp.maximum(qi(n) - 1, 0), OFF_KB // B_KV_WIDTH))
    vc = pl.BlockSpec((BLOCK, B_KV_WIDTH), lambda n: (qi(n), OFF_VB // B_KV_WIDTH))
    vp = pl.BlockSpec((BLOCK, B_KV_WIDTH), lambda n: (jnp.maximum(qi(n) - 1, 0), OFF_VB // B_KV_WIDTH))
    small = [pl.BlockSpec((1, B_HEAD_DIM), lambda n: (0, 0)), pl.BlockSpec((1, B_HEAD_DIM), lambda n: (0, 0)),
             pl.BlockSpec((1, B_GROUP * B_KV_HEADS), lambda n: (0, 0))]
    return [q, kp, kc, vp, vc] + small


def _swa_fwd(proj, q_gain, k_gain, sinks, plan=None):
    T = proj.shape[0]
    nb = T // BLOCK

    def body(q_ref, kp_ref, kc_ref, vp_ref, vc_ref, qg_ref, kg_ref, sk_ref, o_ref):
        mask = _swa_mask(pl.program_id(0) > 0)
        for j in range(B_KV_HEADS):
            c = _swa_head_fwd(j, q_ref, kp_ref, kc_ref, vp_ref, vc_ref, qg_ref[...], kg_ref[...], sk_ref, mask)
            o = _nn(c["prob"], c["vcat"])
            for g in range(B_GROUP):
                o_ref[:, pl.ds((j * B_GROUP + g) * B_HEAD_DIM, B_HEAD_DIM)] = o[g * BLOCK:(g + 1) * BLOCK].astype(BF16)

    return _pcall(
        body, plan=plan, name="swa_fwd", grid=(nb,),
        in_specs=_swa_in_specs(nb, nb - 1),
        out_specs=pl.BlockSpec((BLOCK, B_WIDTH), lambda n: (n, 0)),
        out_shape=jax.ShapeDtypeStruct((T, B_WIDTH), BF16),
        compiler_params=_params(("parallel",)),
    )(proj, proj, proj, proj, proj, q_gain, k_gain, sinks)


def _swa_bwd(proj, q_gain, k_gain, sinks, do, plan=None):
    T = proj.shape[0]
    nb = T // BLOCK
    scale = B_HEAD_DIM ** -0.5

    def body(q_ref, kp_ref, kc_ref, vp_ref, vc_ref, qg_ref, kg_ref, sk_ref, do_ref,
             dq_ref, dkv_ref, sm_ref, ck, cv):
        n = pl.program_id(0)

        @pl.when(n == 0)
        def _():
            ck[...] = jnp.zeros_like(ck)
            cv[...] = jnp.zeros_like(cv)
            sm_ref[...] = jnp.zeros_like(sm_ref)

        @pl.when(n < nb)
        def _():
            mask = _swa_mask(n > 0)
            qg, kg = qg_ref[...], kg_ref[...]
            lane = lax.broadcasted_iota(jnp.int32, (1, BLOCK), 1)
            for j in range(B_KV_HEADS):
                hs = slice(j * B_HEAD_DIM, (j + 1) * B_HEAD_DIM)
                vs = slice(B_KV_WIDTH + j * B_HEAD_DIM, B_KV_WIDTH + (j + 1) * B_HEAD_DIM)
                c = _swa_head_fwd(j, q_ref, kp_ref, kc_ref, vp_ref, vc_ref, qg, kg, sk_ref, mask)
                d_out = jnp.concatenate(
                    [do_ref[:, pl.ds((j * B_GROUP + g) * B_HEAD_DIM, B_HEAD_DIM)] for g in range(B_GROUP)], axis=0)
                prob = c["prob"]
                out = _nn(prob, c["vcat"])
                delta = jnp.sum(d_out * out, axis=-1, keepdims=True)
                ds = prob * (_nt(d_out, c["vcat"]) - delta)
                d_sink = -c["p_sink"] * delta
                dqh = _nn(ds, c["kh"]) * scale
                dkh = _tn(ds, c["qh"]) * scale
                dv = _tn(prob, d_out)
                dq, dqg = _head_norm_bwd(dqh, c["qn"], c["qr"], qg)
                dk, dkg = _head_norm_bwd(dkh, c["kn"], c["kr"], kg)
                sm_ref[0:1, 0:B_HEAD_DIM] += dqg
                sm_ref[1:2, 0:B_HEAD_DIM] += dkg
                for g in range(B_GROUP):
                    dq_ref[:, pl.ds((j * B_GROUP + g) * B_HEAD_DIM, B_HEAD_DIM)] = dq[g * BLOCK:(g + 1) * BLOCK].astype(BF16)
                    tot = jnp.sum(d_sink[g * BLOCK:(g + 1) * BLOCK], axis=0, keepdims=True)
                    sm_ref[2:3, :] += jnp.where(lane == j * B_GROUP + g, tot, 0.0)
                dkv_ref[:, hs] = (ck[:, hs] + dk[0:BLOCK]).astype(BF16)
                dkv_ref[:, vs] = (cv[:, hs] + dv[0:BLOCK]).astype(BF16)
                ck[:, hs] = dk[BLOCK:2 * BLOCK]
                cv[:, hs] = dv[BLOCK:2 * BLOCK]

        @pl.when(n == nb)
        def _():
            dkv_ref[:, 0:B_KV_WIDTH] = ck[...].astype(BF16)
            dkv_ref[:, B_KV_WIDTH:2 * B_KV_WIDTH] = cv[...].astype(BF16)

    return _pcall(
        body, plan=plan, name="swa_bwd", grid=(nb + 1,),
        in_specs=_swa_in_specs(nb, nb - 1) + [pl.BlockSpec((BLOCK, B_WIDTH), lambda n: (jnp.minimum(n, nb - 1), 0))],
        out_specs=[pl.BlockSpec((BLOCK, B_WIDTH), lambda n: (jnp.minimum(n, nb - 1), 0)),
                   pl.BlockSpec((BLOCK, 2 * B_KV_WIDTH), lambda n: (jnp.maximum(n - 1, 0), 0)),
                   pl.BlockSpec((8, BLOCK), lambda n: (0, 0))],
        out_shape=[jax.ShapeDtypeStruct((T, B_WIDTH), BF16), jax.ShapeDtypeStruct((T, 2 * B_KV_WIDTH), BF16),
                   jax.ShapeDtypeStruct((8, BLOCK), F32)],
        scratch_shapes=[pltpu.VMEM((BLOCK, B_KV_WIDTH), F32), pltpu.VMEM((BLOCK, B_KV_WIDTH), F32)],
        compiler_params=_params(("arbitrary",)),
    )(proj, proj, proj, proj, proj, q_gain, k_gain, sinks, do)


W_IN, W_A, W_B, W_OUT, W_MI, W_MO = range(6)


def _local_step(x, target, mod8, norm1_gain, norm2_gain, lb_logits, o_gain, q_gain, k_gain, sinks, parts, c_arr, chip_arr):
    relu2 = lambda u: (u, jnp.square(jnp.maximum(u, 0.0)))
    pair, half = {}, {}

    def exchange(ws, grads):
        return _sibling_exchange_plan([_grad_view(g, w) for w, g in zip(ws, grads)])

    def pair_sums(ws, grads, others):
        for w, g, o in zip(ws, grads, others):
            pair[w] = _pair_sum(_grad_view(g, w), o, c_arr, f"pair_sum{w}")

    def sum_slots(ws, slots):
        for w, s in zip(ws, slots):
            half[w] = _sum_slots(pair[w], s, w, chip_arr, f"sum_slots{w}")

    h, (w_in,) = _norm1_fwd(x, norm1_gain, mod8, plan=_gather_plan({W_IN: parts[W_IN]}, pass_at=0.9))
    proj, (w_mi,) = _mm(h, w_in, name="mm_proj", bn=512, plan=_gather_plan({W_MI: parts[W_MI]}, pass_at=0.8))
    (o_a, states), (w_a, w_b, w_out) = _hgrn_fwd(
        proj, lb_logits, o_gain, plan=_gather_plan({w: parts[w] for w in (W_A, W_B, W_OUT)}, pass_at=0.8))
    o_b, (w_mo,) = _swa_fwd(proj, q_gain, k_gain, sinks, plan=_gather_plan({W_MO: parts[W_MO]}, pass_at=0.9))
    ya = _mm(o_a, w_a, name="mm_branch_a")
    yb = _mm(o_b, w_b, name="mm_branch_b")
    merged = _merge_fwd(proj, ya, yb)
    mo = _mm(merged, w_out, name="mm_out")
    x1, h2 = _res_norm2_fwd(x, mo, norm2_gain, mod8)
    u, act = _mm(h2, w_mi, name="mm_mlp_in", out_dtypes=(F32, BF16), epi=relu2)
    mlp = _mm(act, w_mo, name="mm_mlp_out")
    dy, dmlp, st_loss = _loss_bwd(x1, mlp, target, mod8)
    g_mo = _mm(act, dmlp, name="mm_g_mlp_out", ta=True)
    du, others = _mm(dmlp, w_mo, name="mm_d_act", tb=True, out_dtypes=(BF16,), extras=(u,),
                     epi=lambda acc, uu: (acc * (2.0 * jnp.maximum(uu, 0.0)),), plan=exchange([W_MO], [g_mo]))
    pair_sums([W_MO], [g_mo], others)
    g_mi, slots_mo = _mm(h2, du, name="mm_g_mlp_in", ta=True, plan=_chip_exchange_plan({W_MO: pair[W_MO]}))
    dh2, others = _mm(du, w_mi, name="mm_d_h2", tb=True, plan=exchange([W_MI], [g_mi]))
    pair_sums([W_MI], [g_mi], others)
    sum_slots([W_MO], slots_mo)
    dx1, dmo, st_n2 = _norm2_bwd(dh2, x1, dy, mo, norm2_gain, mod8)
    dmerged = _mm(dmo, w_out, name="mm_d_merged", tb=True)
    g_out = _mm(merged, dmo, name="mm_g_out", ta=True)
    dya, dyb, dga, dgb = _merge_bwd(proj, ya, yb, dmerged)
    do_a = _mm(dya, w_a, name="mm_d_oa", tb=True)
    g_a = _mm(o_a, dya, name="mm_g_branch_a", ta=True)
    do_b = _mm(dyb, w_b, name="mm_d_ob", tb=True)
    g_b = _mm(o_b, dyb, name="mm_g_branch_b", ta=True)
    mid = [W_A, W_B, W_OUT]
    (dqa, dfa, dia, dgga, d_lb, d_og), res = _hgrn_bwd(
        proj, lb_logits, o_gain, states, do_a,
        plan=_join(_chip_exchange_plan({W_MI: pair[W_MI]}), exchange(mid, [g_a, g_b, g_out])))
    sum_slots([W_MI], res[:1])
    pair_sums(mid, [g_a, g_b, g_out], res[1:])
    (dqb, dkvb, st_swa), slots_mid = _swa_bwd(proj, q_gain, k_gain, sinks, do_b,
                                              plan=_chip_exchange_plan({w: pair[w] for w in mid}))
    sum_slots(mid, slots_mid)
    dproj = jnp.concatenate([dqa, dfa, dia, dgga, dqb, dkvb, dga, dgb], axis=1)
    hr = D_MODEL // 2
    h_send = lax.dynamic_slice(h, (0, (1 - c_arr[0]) * hr), (h.shape[0], hr))
    h_own = lax.dynamic_slice(h, (0, c_arr[0] * hr), (h.shape[0], hr))
    done = [W_A, W_B, W_OUT, W_MI, W_MO]
    g_send, res = _mm(h_send, dproj, name="mm_g_in_send", ta=True, bn=512,
                      plan=_sibling_share_plan([half[w] for w in done]))
    theirs = dict(zip(done, res))
    g_own, (g_other,) = _mm(h_own, dproj, name="mm_g_in_own", ta=True, bn=512, plan=_sibling_share_plan([g_send]))
    pair[W_IN] = _add_bf16(g_own, g_other, "pair_sum0")[None]
    dh, slots_in = _mm(dproj, w_in, name="mm_d_h", tb=True, bk=2432, plan=_chip_exchange_plan({W_IN: pair[W_IN]}))
    sum_slots([W_IN], slots_in)
    grad_x, st_n1 = _norm1_bwd(dh, x, dx1, norm1_gain, mod8)
    (theirs[W_IN],) = _run_plan(_sibling_share_plan([half[W_IN]]), "sibling_share_w_in")
    stats = dict(loss=st_loss, n2=st_n2, n1=st_n1, d_lb=d_lb, d_og=d_og, swa=st_swa)
    return grad_x, [half[w] for w in range(N_W)], [theirs[w] for w in range(N_W)], stats


def _ew_rows(rows, cols):
    br = 8
    while br * 2 <= rows and br * 2 * cols * 4 <= (1 << 20) and rows % (br * 2) == 0:
        br *= 2
    return br


def _cast_into_full(shard, w, chip_arr, name):
    sr, sc = shard.shape
    R, C, by_col = W_SHAPES[w]
    br = _ew_rows(sr, sc)
    nb = sr // br
    out_map = (lambda i, chip: (i, chip[0])) if by_col else (lambda i, chip: (chip[0] * nb + i, 0))

    def body(chip_ref, w_ref, o_ref):
        o_ref[...] = w_ref[...].astype(BF16)

    return _pcall(
        body, name=name,
        grid_spec=pltpu.PrefetchScalarGridSpec(
            num_scalar_prefetch=1, grid=(nb,),
            in_specs=[pl.BlockSpec((br, sc), lambda i, chip: (i, 0))],
            out_specs=pl.BlockSpec((br, sc), out_map)),
        out_shape=jax.ShapeDtypeStruct((R, C), BF16), compiler_params=_params(("parallel",)))(chip_arr, shard)


def _adamw_math(w, g, m, v):
    m = ADAM_B1 * m + (1.0 - ADAM_B1) * g
    v = ADAM_B2 * v + (1.0 - ADAM_B2) * (g * g)
    m_hat = m / (1.0 - ADAM_B1 ** ADAM_STEP)
    v_hat = v / (1.0 - ADAM_B2 ** ADAM_STEP)
    delta = -ADAM_LR * (m_hat / (jnp.sqrt(v_hat) + ADAM_EPS) + ADAM_WD * w)
    return delta, m, v


def _adamw(w, g, m, v, name):
    R, C = w.shape
    br = _ew_rows(R, C)
    spec = pl.BlockSpec((br, C), lambda i: (i, 0))

    def body(w_ref, g_ref, m_ref, v_ref, d_ref, nm_ref, nv_ref):
        d_ref[...], nm_ref[...], nv_ref[...] = _adamw_math(w_ref[...], g_ref[...], m_ref[...], v_ref[...])

    sh = jax.ShapeDtypeStruct((R, C), F32)
    return _pcall(body, name=name, grid=(R // br,), in_specs=[spec] * 4, out_specs=[spec] * 3, out_shape=[sh] * 3,
                  compiler_params=_params(("parallel",)))(w, g, m, v)


def _add_bf16(a, b, name):
    R, C = a.shape
    br = _ew_rows(R, C)
    spec = pl.BlockSpec((br, C), lambda i: (i, 0))

    def body(a_ref, b_ref, o_ref):
        o_ref[...] = (a_ref[...] + b_ref[...]).astype(BF16)

    return _pcall(body, name=name, grid=(R // br,), in_specs=[spec, spec], out_specs=spec,
                  out_shape=jax.ShapeDtypeStruct((R, C), BF16), compiler_params=_params(("parallel",)))(a, b)


def _adamw_halves(w, own, other, m, v, c_arr, name):
    R, C = w.shape
    hr = R // 2
    br = _ew_rows(hr, C)
    nb = hr // br
    full = pl.BlockSpec((br, C), lambda h, i, c_ref: (h * nb + i, 0))
    half = pl.BlockSpec((br, C), lambda h, i, c_ref: (i, 0))

    def body(c_ref, w_ref, own_ref, oth_ref, m_ref, v_ref, g_ref, d_ref, nm_ref, nv_ref):
        g = jnp.where(pl.program_id(0) == c_ref[0], own_ref[...], oth_ref[...])
        g_ref[...] = g
        d_ref[...], nm_ref[...], nv_ref[...] = _adamw_math(w_ref[...], g, m_ref[...], v_ref[...])

    sh = jax.ShapeDtypeStruct((R, C), F32)
    return _pcall(
        body, name=name,
        grid_spec=pltpu.PrefetchScalarGridSpec(
            num_scalar_prefetch=1, grid=(2, nb), in_specs=[full, half, half, full, full], out_specs=[full] * 4),
        out_shape=[sh] * 4, compiler_params=_params(("parallel", "parallel")))(c_arr, w, own, other, m, v)


def _ada_grad_adamw(c_t, dmod, w, m, v):
    R, C = w.shape
    br = _ew_rows(R, C)
    spec = pl.BlockSpec((br, C), lambda i: (i, 0))

    def body(c_ref, dm_ref, w_ref, m_ref, v_ref, g_ref, d_ref, nm_ref, nv_ref):
        cv = c_ref[...]
        sc = cv * _sig(cv)
        g = sc[:, 0:1] * dm_ref[0:1, :]
        for b in range(1, N_DEV):
            g = g + sc[:, b:b + 1] * dm_ref[b:b + 1, :]
        g_ref[...] = g
        d_ref[...], nm_ref[...], nv_ref[...] = _adamw_math(w_ref[...], g, m_ref[...], v_ref[...])

    sh = jax.ShapeDtypeStruct((R, C), F32)
    return _pcall(
        body, name="ada_grad_adamw", grid=(R // br,),
        in_specs=[pl.BlockSpec((br, N_DEV), lambda i: (i, 0)), pl.BlockSpec((N_DEV, C), lambda i: (0, 0)), spec, spec, spec],
        out_specs=[spec] * 4, out_shape=[sh] * 4, compiler_params=_params(("parallel",)))(c_t, dmod, w, m, v)


SMALL_ROWS = 16


def _small_sum(small_all, lb_logits):
    def body(s_ref, lbl_ref, o_ref):
        acc = s_ref[0:SMALL_ROWS, :]
        for d in range(1, N_DEV):
            acc = acc + s_ref[d * SMALL_ROWS:(d + 1) * SMALL_ROWS, :]
        o_ref[...] = acc
        z = lbl_ref[...]
        e = jnp.exp(z - jnp.max(z, axis=0, keepdims=True))
        p0 = e[0:1, :] / (e[0:1, :] + e[1:2, :])
        dz = acc[8:9, 0:A_WIDTH] * p0 * (1.0 - p0)
        o_ref[8:9, 0:A_WIDTH] = dz
        o_ref[10:11, 0:A_WIDTH] = -dz

    return _pcall(body, name="small_sum", out_shape=jax.ShapeDtypeStruct((SMALL_ROWS, D_MODEL), F32),
                  in_specs=[pl.BlockSpec(memory_space=pltpu.VMEM)] * 2, out_specs=pl.BlockSpec(memory_space=pltpu.VMEM),
                  compiler_params=_params())(small_all, lb_logits)


RELATIONS = ((1, 0), (0, 1), (1, 1))
ANY = pl.BlockSpec(memory_space=pl.ANY)


def _place():
    x, y, c = lax.axis_index("x"), lax.axis_index("y"), lax.axis_index("c")
    return x, y, c


def _allgather_small(x_shard, name):
    m_per, n = x_shard.shape

    def body(x_ref, out_ref, send_sems, recv_sems, local_sem):
        x, y, c = _place()
        me, sibling = (x, y, c), (x, y, 1 - c)
        chips = [(1 - x, y), (x, 1 - y), (1 - x, 1 - y)]

        def rows(px, py, pc):
            return out_ref.at[pl.ds((4 * px + 2 * py + pc) * m_per, m_per), :]

        def copy(k, block, to, src=None):
            return pltpu.make_async_remote_copy(
                src_ref=rows(*block) if src is None else src, dst_ref=rows(*block),
                send_sem=send_sems.at[k], recv_sem=recv_sems.at[k], device_id=to, device_id_type=MESH)

        mine = pltpu.make_async_copy(x_ref, rows(*me), local_sem)
        mine.start()
        first = [copy(0, me, sibling, src=x_ref)]
        first += [copy(1 + j, me, (*chip, c), src=x_ref) for j, chip in enumerate(chips)]
        for cp in first:
            cp.start()
        passed = [copy(4 + j, (*chip, c), sibling) for j, chip in enumerate(chips)]
        for j, chip in enumerate(chips):
            copy(1 + j, (*chip, c), me).wait_recv()
            passed[j].start()
        copy(0, sibling, me).wait_recv()
        for j, chip in enumerate(chips):
            copy(4 + j, (*chip, 1 - c), me).wait_recv()
        for cp in first + passed:
            cp.wait_send()
        mine.wait()

    return _pcall(
        body, name=name, out_shape=jax.ShapeDtypeStruct((N_DEV * m_per, n), x_shard.dtype),
        in_specs=[pl.BlockSpec(memory_space=pltpu.VMEM)], out_specs=pl.BlockSpec(memory_space=pltpu.VMEM),
        scratch_shapes=[pltpu.SemaphoreType.DMA((7,)), pltpu.SemaphoreType.DMA((7,)), pltpu.SemaphoreType.DMA],
        compiler_params=_params(),
    )(x_shard)


W_SHAPES = ((D_MODEL, IN_WIDTH, True), (A_WIDTH, D_MODEL, True), (B_WIDTH, D_MODEL, True),
            (D_MODEL, D_MODEL, False), (D_MODEL, MLP_HIDDEN, True), (MLP_HIDDEN, D_MODEL, False))
N_W = len(W_SHAPES)


def _shard_shape(w):
    R, C, by_col = W_SHAPES[w]
    return (R, C // N_CHIPS) if by_col else (R // N_CHIPS, C)


def _half_shape(w):
    sr, sc = _shard_shape(w)
    return sr // 2, sc


def _region(full_ref, w, chip, half):
    sr, sc = _shard_shape(w)
    by_col = W_SHAPES[w][2]
    r0, c0 = (0, chip * sc) if by_col else (chip * sr, 0)
    if half is None:
        return full_ref.at[pl.ds(r0, sr), pl.ds(c0, sc)]
    return full_ref.at[pl.ds(r0 + half * (sr // 2), sr // 2), pl.ds(c0, sc)]


def _on_device(fn):
    x, y, c = _place()
    me = 4 * x + 2 * y + c
    for d in range(N_DEV):
        @pl.when(me == d)
        def _(d=d):
            fn(x, y, c, d)


def _gather_plan(partials, pass_at=0.5):
    ws = sorted(partials)
    pairs = [(i, w, k) for i, w in enumerate(ws) for k in range(3)]

    def first(pi, po, ps, x, y, c, d, i, w, k):
        chip, dc = d >> 1, d & 1
        rx, ry = RELATIONS[k]
        return pltpu.make_async_remote_copy(
            src_ref=_region(pi[i], w, chip, dc), dst_ref=_region(po[i], w, chip, dc),
            send_sem=ps[0].at[i * 3 + k], recv_sem=ps[1].at[i * 3 + k],
            device_id=(x ^ rx, y ^ ry, c), device_id_type=MESH)

    def landed(po, ps, x, y, c, d, i, w, k, half, to_sibling):
        rx, ry = RELATIONS[k]
        got = _region(po[i], w, (d >> 1) ^ (2 * rx + ry), half)
        s = 2 if to_sibling else 0
        return pltpu.make_async_remote_copy(
            src_ref=got, dst_ref=got, send_sem=ps[s].at[i * 3 + k], recv_sem=ps[s + 1].at[i * 3 + k],
            device_id=(x, y, 1 - c), device_id_type=MESH)

    def send(pi, po, ps):
        def run(x, y, c, d):
            for i, w, k in pairs:
                first(pi, po, ps, x, y, c, d, i, w, k).start()
        _on_device(run)

    def pass_on(pi, po, ps):
        def run(x, y, c, d):
            for i, w, k in pairs:
                landed(po, ps, x, y, c, d, i, w, k, d & 1, False).wait_recv()
                landed(po, ps, x, y, c, d, i, w, k, d & 1, True).start()
        _on_device(run)

    def finish(pi, po, ps):
        def run(x, y, c, d):
            for i, w, k in pairs:
                landed(po, ps, x, y, c, d, i, w, k, 1 - (d & 1), True).wait_recv()
            for i, w, k in pairs:
                first(pi, po, ps, x, y, c, d, i, w, k).wait_send()
                landed(po, ps, x, y, c, d, i, w, k, d & 1, True).wait_send()
        _on_device(run)

    return _Plan([partials[w] for w in ws], [jax.ShapeDtypeStruct(W_SHAPES[w][:2], BF16) for w in ws],
                 [pltpu.SemaphoreType.DMA((3 * len(ws),)) for _ in range(4)], [send, pass_on, finish],
                 {i: i for i in range(len(ws))}, mid_at=(pass_at,))


def _grad_view(g, w):
    R, C, by_col = W_SHAPES[w]
    return g.reshape(1, 2, R // 2, C) if by_col else g.reshape(N_CHIPS, 2, R // N_CHIPS // 2, C)


def _start_wait_plan(ins, outs, n_copies, copies):
    def start(pi, po, ps):
        for cp in copies(pi, po, ps):
            cp.start()

    def finish(pi, po, ps):
        for cp in copies(pi, po, ps):
            cp.wait()

    return _Plan(ins, outs, [pltpu.SemaphoreType.DMA((n_copies,)), pltpu.SemaphoreType.DMA((n_copies,))], [start, finish])


def _sibling_exchange_plan(g4s):
    pieces = [(i, p) for i, g in enumerate(g4s) for p in range(g.shape[0])]

    def copies(pi, po, ps):
        x, y, c = _place()
        return [pltpu.make_async_remote_copy(
            src_ref=pi[i].at[p, 1 - c], dst_ref=po[i].at[p], send_sem=ps[0].at[n], recv_sem=ps[1].at[n],
            device_id=(x, y, 1 - c), device_id_type=MESH) for n, (i, p) in enumerate(pieces)]

    return _start_wait_plan(list(g4s), [jax.ShapeDtypeStruct((g.shape[0],) + g.shape[2:], F32) for g in g4s],
                            len(pieces), copies)


def _pair_sum(g4, other, c_arr, name):
    P, _, hr, C = g4.shape
    br = _ew_rows(hr, C)

    def body(c_ref, g_ref, o_ref, p_ref):
        p_ref[...] = (g_ref[...] + o_ref[...]).astype(BF16)

    return _pcall(
        body, name=name,
        grid_spec=pltpu.PrefetchScalarGridSpec(
            num_scalar_prefetch=1, grid=(P, hr // br),
            in_specs=[pl.BlockSpec((None, None, br, C), lambda p, i, c_ref: (p, c_ref[0], i, 0)),
                      pl.BlockSpec((None, br, C), lambda p, i, c_ref: (p, i, 0))],
            out_specs=pl.BlockSpec((None, br, C), lambda p, i, c_ref: (p, i, 0))),
        out_shape=jax.ShapeDtypeStruct((P, hr, C), BF16),
        compiler_params=_params(("parallel", "parallel")),
    )(c_arr, g4, other)


def _pair_part(p_ref, w, chip):
    sr, sc = _shard_shape(w)
    return p_ref.at[0, :, pl.ds(chip * sc, sc)] if W_SHAPES[w][2] else p_ref.at[chip]


def _chip_exchange_plan(pairs):
    ws = sorted(pairs)

    def stage(wait):
        def run(pi, po, ps):
            def on(x, y, c, d):
                for i, w in enumerate(ws):
                    for k, (rx, ry) in enumerate(RELATIONS):
                        cp = pltpu.make_async_remote_copy(
                            src_ref=_pair_part(pi[i], w, (d >> 1) ^ (2 * rx + ry)), dst_ref=po[i].at[k],
                            send_sem=ps[0].at[i * 3 + k], recv_sem=ps[1].at[i * 3 + k],
                            device_id=(x ^ rx, y ^ ry, c), device_id_type=MESH)
                        if wait:
                            cp.wait()
                        else:
                            cp.start()
            _on_device(on)
        return run

    return _Plan([pairs[w] for w in ws], [jax.ShapeDtypeStruct((3,) + _half_shape(w), BF16) for w in ws],
                 [pltpu.SemaphoreType.DMA((3 * len(ws),)), pltpu.SemaphoreType.DMA((3 * len(ws),))],
                 [stage(False), stage(True)])


def _sum_slots(pair, slots, w, chip_arr, name):
    _, hr, C = slots.shape
    br = _ew_rows(hr, C)
    own_map = (lambda i, chip: (0, i, chip[0])) if W_SHAPES[w][2] else (lambda i, chip: (chip[0], i, 0))

    def body(chip_ref, p_ref, s_ref, o_ref):
        acc = p_ref[...].astype(F32)
        for k in range(3):
            acc = acc + s_ref[k].astype(F32)
        o_ref[...] = acc

    return _pcall(
        body, name=name,
        grid_spec=pltpu.PrefetchScalarGridSpec(
            num_scalar_prefetch=1, grid=(hr // br,),
            in_specs=[pl.BlockSpec((None, br, C), own_map), pl.BlockSpec((3, br, C), lambda i, chip: (0, i, 0))],
            out_specs=pl.BlockSpec((br, C), lambda i, chip: (i, 0))),
        out_shape=jax.ShapeDtypeStruct((hr, C), F32), compiler_params=_params(("parallel",)),
    )(chip_arr, pair, slots)


def _sibling_share_plan(halves):
    def copies(pi, po, ps):
        x, y, c = _place()
        return [pltpu.make_async_remote_copy(
            src_ref=pi[i], dst_ref=po[i], send_sem=ps[0].at[i], recv_sem=ps[1].at[i],
            device_id=(x, y, 1 - c), device_id_type=MESH) for i in range(len(halves))]

    return _start_wait_plan(list(halves), [jax.ShapeDtypeStruct(h.shape, F32) for h in halves], len(halves), copies)


def _pad_lanes(v, width=D_MODEL):
    return jnp.pad(v, ((0, 0), (0, width - v.shape[1])))


def _pack_small(b_ada, norm1, norm2, lb, o_gain, q_gain, k_gain, sinks):
    rows = [b_ada.reshape(N_MOD, D_MODEL), norm1, norm2, jnp.concatenate([lb[0:1], o_gain], axis=1),
            _pad_lanes(jnp.concatenate([q_gain, k_gain, sinks], axis=1)), _pad_lanes(lb[1:2]),
            jnp.zeros((SMALL_ROWS - 11, D_MODEL), F32)]
    return jnp.concatenate(rows, axis=0)


def _unpack_small(p):
    return (p[0:6].reshape(1, N_MOD * D_MODEL), p[6:7], p[7:8],
            jnp.concatenate([p[8:9, 0:A_WIDTH], p[10:11, 0:A_WIDTH]], axis=0), p[8:9, A_WIDTH:],
            p[9:10, 0:64], p[9:10, 64:128], p[9:10, 128:144])


def kernel(x, c, w_ada, b_ada, norm1_gain, w_in, lb_logits, hgrn_o_gain, q_norm_gain, k_norm_gain, sinks, w_branch_a, w_branch_b, w_out, norm2_gain, w_mlp_in, w_mlp_out, loss_target, m_w_ada, m_b_ada, m_norm1_gain, m_w_in, m_lb_logits, m_hgrn_o_gain, m_q_norm_gain, m_k_norm_gain, m_sinks, m_w_branch_a, m_w_branch_b, m_w_out, m_norm2_gain, m_w_mlp_in, m_w_mlp_out, v_w_ada, v_b_ada, v_norm1_gain, v_w_in, v_lb_logits, v_hgrn_o_gain, v_q_norm_gain, v_k_norm_gain, v_sinks, v_w_branch_a, v_w_branch_b, v_w_out, v_norm2_gain, v_w_mlp_in, v_w_mlp_out):
    xi, yi, ci = _place()
    chip = 2 * xi + yi
    me = 4 * xi + 2 * yi + ci
    ada_cols = w_ada.shape[2]

    c_all = _allgather_small(jnp.broadcast_to(c, (8, D_MODEL)), "gather_c").reshape(N_DEV, 8, D_MODEL)[:, 0]
    b_cols = lax.dynamic_slice(b_ada, (0, chip * ada_cols), (1, ada_cols))
    mod_part = _ada_fwd(c_all, w_ada[0], b_cols)
    mod_all = _allgather_small(mod_part, "gather_mod").reshape(N_CHIPS, 2, N_DEV, ada_cols)[:, 0]
    mod_mine = lax.dynamic_index_in_dim(mod_all, me, axis=1, keepdims=False).reshape(N_MOD, D_MODEL)
    mod8 = jnp.concatenate([mod_mine, jnp.zeros((2, D_MODEL), F32)], axis=0)

    shards = (w_in[0], w_branch_a[0], w_branch_b[0], w_out[0], w_mlp_in[0], w_mlp_out[0])
    chip_arr = chip.astype(jnp.int32).reshape(1)
    c_arr = ci.astype(jnp.int32).reshape(1)
    parts = [_cast_into_full(s, w, chip_arr, f"cast_w{w}") for w, s in enumerate(shards)]

    grad_x, halves, theirs, st = _local_step(x[0], loss_target[0], mod8, norm1_gain, norm2_gain, lb_logits, hgrn_o_gain,
                                             q_norm_gain, k_norm_gain, sinks, parts, c_arr, chip_arr)
    loss = lax.psum(0.5 * jnp.sum(st["loss"][0]) / D_MODEL, ("x", "y", "c"))
    moments = ((m_w_in, v_w_in), (m_w_branch_a, v_w_branch_a), (m_w_branch_b, v_w_branch_b), (m_w_out, v_w_out),
               (m_w_mlp_in, v_w_mlp_in), (m_w_mlp_out, v_w_mlp_out))
    big = [_adamw_halves(shards[w], halves[w], theirs[w], moments[w][0][0], moments[w][1][0], c_arr, f"adamw{w}")
           for w in range(N_W)]

    swa = st["swa"]
    small = jnp.concatenate([
        st["n1"][1:2], st["n1"][0:1], st["n2"][3:4], st["n2"][1:2], st["n2"][0:1], st["loss"][1:2],
        st["n1"][2:3], st["n2"][2:3], jnp.concatenate([st["d_lb"][0:1], st["d_og"][0:1]], axis=1),
        _pad_lanes(jnp.concatenate([swa[0:1, 0:64], swa[1:2, 0:64], swa[2:3, 0:16]], axis=1)),
        jnp.zeros((SMALL_ROWS - 10, D_MODEL), F32)], axis=0)
    small_all = _allgather_small(small, "gather_small")
    g_small = _small_sum(small_all, lb_logits)
    small_w = (b_ada, norm1_gain, norm2_gain, lb_logits, hgrn_o_gain, q_norm_gain, k_norm_gain, sinks)
    small_m = (m_b_ada, m_norm1_gain, m_norm2_gain, m_lb_logits, m_hgrn_o_gain, m_q_norm_gain, m_k_norm_gain, m_sinks)
    small_v = (v_b_ada, v_norm1_gain, v_norm2_gain, v_lb_logits, v_hgrn_o_gain, v_q_norm_gain, v_k_norm_gain, v_sinks)
    sm = [_unpack_small(t) for t in
          (g_small,) + tuple(_adamw(_pack_small(*small_w), g_small, _pack_small(*small_m), _pack_small(*small_v),
                                    "adamw_small"))]
    g_b, g_n1, g_n2, g_lb, g_og, g_qg, g_kg, g_sk = ([t[i] for t in sm] for i in range(8))

    dmod_all = small_all.reshape(N_DEV, SMALL_ROWS, D_MODEL)[:, 0:N_MOD].reshape(N_DEV, N_MOD * D_MODEL)
    dmod_cols = lax.dynamic_slice(dmod_all, (0, chip * ada_cols), (N_DEV, ada_cols))
    ada = _ada_grad_adamw(c_all.T, dmod_cols, w_ada[0], m_w_ada[0], v_w_ada[0])

    def ordered(k):
        lead = lambda a: a[None]
        return (lead(ada[k]), g_b[k], g_n1[k], lead(big[0][k]), g_lb[k], g_og[k], g_qg[k], g_kg[k], g_sk[k],
                lead(big[1][k]), lead(big[2][k]), lead(big[3][k]), g_n2[k], lead(big[4][k]), lead(big[5][k]))

    return (loss, grad_x[None]) + ordered(0) + ordered(1) + ordered(2) + ordered(3)
```

```python
import functools

import jax
import jax.numpy as jnp
from jax import lax
from jax.experimental import pallas as pl
from jax.experimental.pallas import tpu as pltpu

F32 = jnp.float32
BF16 = jnp.bfloat16
HIGHEST = lax.Precision.HIGHEST
MESH = pl.DeviceIdType.MESH

D_MODEL = 2048
A_WIDTH = 1024
A_HEADS = 8
A_HEAD_DIM = 128
A_CHUNK = 64
B_WIDTH = 1024
B_HEAD_DIM = 64
B_GROUP = 4
B_KV_HEADS = 4
B_KV_WIDTH = 256
BLOCK = 128
MLP_HIDDEN = 8192
IN_WIDTH = 9728
N_MOD = 6
EPS = 1e-6
N_CHIPS = 4
N_DEV = 8

OFF_QA, OFF_FA, OFF_IA, OFF_GA = 0, 1024, 2048, 3072
OFF_QB, OFF_KB, OFF_VB = 4096, 5120, 5376
OFF_GATE_A, OFF_GATE_B = 5632, 7680

ADAM_LR = 0.001
ADAM_B1 = 0.9
ADAM_B2 = 0.999
ADAM_EPS = 1e-08
ADAM_WD = 0.01
ADAM_STEP = 10

VMEM_LIMIT_V7X = 48 * 1024 * 1024
NEG_BIG = -1e30


def _params(sem=None, vmem=VMEM_LIMIT_V7X):
    return pltpu.CompilerParams(dimension_semantics=sem, vmem_limit_bytes=vmem)


class _Plan:
    def __init__(self, ins, outs, sems, stages, aliases=None, mid_at=()):
        self.ins, self.outs, self.sems, self.stages, self.aliases = ins, outs, sems, stages, aliases or {}
        self.mid_at = tuple(mid_at)
        assert len(self.mid_at) == len(stages) - 2


def _join(a, b):
    assert len(a.stages) == 2 and len(b.stages) == 2
    ni, no, ns = len(a.ins), len(a.outs), len(a.sems)

    def stage(k):
        def run(pi, po, ps):
            a.stages[k](pi[:ni], po[:no], ps[:ns])
            b.stages[k](pi[ni:], po[no:], ps[ns:])
        return run

    aliases = dict(a.aliases)
    aliases.update({ni + i: no + o for i, o in b.aliases.items()})
    return _Plan(a.ins + b.ins, a.outs + b.outs, a.sems + b.sems, [stage(0), stage(1)], aliases)


def _pcall(body, plan=None, **kw):
    if plan is None:
        return pl.pallas_call(body, **kw)
    grid = kw["grid"]
    single = not isinstance(kw["out_specs"], (list, tuple))
    in_specs = list(kw["in_specs"])
    out_specs = [kw["out_specs"]] if single else list(kw["out_specs"])
    out_shape = [kw["out_shape"]] if single else list(kw["out_shape"])
    scratch = list(kw.get("scratch_shapes", ()))
    n_in, n_out, n_scr = len(in_specs), len(out_specs), len(scratch)
    n_pi, n_po = len(plan.ins), len(plan.outs)
    total = 1
    for g in grid:
        total *= g
    n_st = len(plan.stages)

    def wrapped(*refs):
        o0 = n_in + n_pi
        s0 = o0 + n_out + n_po
        pi, po, ps = refs[n_in:o0], refs[o0 + n_out:s0], refs[s0 + n_scr:]
        lin = 0
        for d, g in enumerate(grid):
            lin = lin * g + pl.program_id(d)
        for si, frac in enumerate((0.0,) + plan.mid_at):
            @pl.when(lin == int(frac * (total - 1)))
            def _(si=si):
                plan.stages[si](pi, po, ps)
        body(*refs[:n_in], *refs[o0:o0 + n_out], *refs[s0:s0 + n_scr])

        @pl.when(lin == total - 1)
        def _():
            plan.stages[-1](pi, po, ps)

    any_spec = pl.BlockSpec(memory_space=pl.ANY)
    call = pl.pallas_call(
        wrapped, name=kw["name"], grid=grid, in_specs=in_specs + [any_spec] * n_pi,
        out_specs=out_specs + [any_spec] * n_po, out_shape=out_shape + list(plan.outs),
        scratch_shapes=scratch + list(plan.sems),
        input_output_aliases={n_in + i: n_out + o for i, o in plan.aliases.items()},
        compiler_params=_params(("arbitrary",) * len(grid)))

    def run(*args):
        res = call(*args, *plan.ins)
        outs = list(res[:n_out])
        return (outs[0] if single else outs), list(res[n_out:])

    return run


def _run_plan(plan, name):
    return _pcall(lambda: None, plan=plan, name=name, grid=(1,), in_specs=[], out_specs=[], out_shape=[])()[1]


def _sig(x):
    return 1.0 / (1.0 + jnp.exp(-x))


def _nn(a, b):
    return lax.dot_general(a.astype(BF16), b.astype(BF16), (((1,), (0,)), ((), ())), preferred_element_type=F32)


def _nt(a, b):
    return lax.dot_general(a.astype(BF16), b.astype(BF16), (((1,), (1,)), ((), ())), preferred_element_type=F32)


def _tn(a, b):
    return lax.dot_general(a.astype(BF16), b.astype(BF16), (((0,), (0,)), ((), ())), preferred_element_type=F32)


def _mm(a, b, *, name, ta=False, tb=False, bm=1024, bn=1024, bk=2048, out_dtypes=(F32,), epi=None, extras=(),
        extra_cols=None, plan=None):
    if ta:
        K, M = a.shape
        bk = K
    else:
        M, K = a.shape
    if tb:
        N, K2 = b.shape
    else:
        K2, N = b.shape
    bm, bn, bk = min(bm, M), min(bn, N), min(bk, K)
    assert K == K2 and M % bm == 0 and N % bn == 0 and K % bk == 0, (name, a.shape, b.shape)
    nk = K // bk
    a_spec = pl.BlockSpec((bk, bm), lambda i, j, k: (k, i)) if ta else pl.BlockSpec((bm, bk), lambda i, j, k: (i, k))
    b_spec = pl.BlockSpec((bn, bk), lambda i, j, k: (j, k)) if tb else pl.BlockSpec((bk, bn), lambda i, j, k: (k, j))
    t_spec = pl.BlockSpec((bm, bn), lambda i, j, k: (i, j))
    extra_cols = extra_cols or (0,) * len(extras)
    e_specs = [pl.BlockSpec((bm, bn), lambda i, j, k, off=off: (i, off + j)) for off in extra_cols]
    dims = (((1,), (1 if tb else 0,)), ((), ()))
    n_e, n_o = len(extras), len(out_dtypes)

    def body(*refs):
        a_ref, b_ref = refs[0], refs[1]
        e_refs = refs[2:2 + n_e]
        o_refs = refs[2 + n_e:2 + n_e + n_o]

        def finish(acc):
            outs = (acc,) if epi is None else epi(acc, *[e[...] for e in e_refs])
            for o_ref, o in zip(o_refs, outs):
                o_ref[...] = o.astype(o_ref.dtype)

        if ta:
            at_ref = refs[-1]

            @pl.when(pl.program_id(1) == 0)
            def _():
                at_ref[...] = a_ref[...].T

            lhs = at_ref[...]
        else:
            lhs = a_ref[...].astype(BF16)
        part = lax.dot_general(lhs, b_ref[...].astype(BF16), dims, preferred_element_type=F32)
        if nk == 1:
            finish(part)
        else:
            acc_ref = refs[-1]
            k = pl.program_id(2)

            @pl.when(k == 0)
            def _():
                acc_ref[...] = part

            @pl.when(k > 0)
            def _():
                acc_ref[...] += part

            @pl.when(k == nk - 1)
            def _():
                finish(acc_ref[...])

    if ta:
        assert a.dtype == BF16 and nk == 1
        scratch = [pltpu.VMEM((bm, bk), BF16)]
    else:
        scratch = [pltpu.VMEM((bm, bn), F32)] if nk > 1 else []
    out = _pcall(
        body, plan=plan, name=name, grid=(M // bm, N // bn, nk),
        in_specs=[a_spec, b_spec] + e_specs,
        out_specs=[t_spec] * n_o,
        out_shape=[jax.ShapeDtypeStruct((M, N), dt) for dt in out_dtypes],
        scratch_shapes=scratch,
        compiler_params=_params(("parallel", "arbitrary", "arbitrary")),
    )(a, b, *extras)
    if plan is not None:
        return (out[0][0] if n_o == 1 else out[0]), out[1]
    return out[0] if n_o == 1 else out


def _ada_fwd(c_all, w_ada, b_cols):
    n = w_ada.shape[1]
    bn = 512

    def body(c_ref, w_ref, b_ref, o_ref):
        cv = c_ref[...]
        sc = cv * _sig(cv)
        o_ref[...] = jnp.dot(sc, w_ref[...], precision=HIGHEST, preferred_element_type=F32) + b_ref[...]

    return _pcall(
        body, name="ada_fwd", grid=(n // bn,),
        in_specs=[pl.BlockSpec((N_DEV, D_MODEL), lambda j: (0, 0)), pl.BlockSpec((D_MODEL, bn), lambda j: (0, j)),
                  pl.BlockSpec((1, bn), lambda j: (0, j))],
        out_specs=pl.BlockSpec((N_DEV, bn), lambda j: (0, j)),
        out_shape=jax.ShapeDtypeStruct((N_DEV, n), F32),
        compiler_params=_params(("parallel",)),
    )(c_all, w_ada, b_cols)


ROWS_EW = 256


def _rms_fwd_math(x, gain, scale, shift):
    rstd = lax.rsqrt(jnp.mean(x * x, axis=-1, keepdims=True) + EPS)
    xhat = x * rstd
    n = xhat * gain
    return n * (1.0 + scale) + shift, xhat, n, rstd


def _rms_bwd_math(dh, xhat, n, rstd, gain, scale):
    dn = dh * (1.0 + scale)
    dxhat = dn * gain
    dx = rstd * (dxhat - xhat * jnp.mean(dxhat * xhat, axis=-1, keepdims=True))
    d_scale = jnp.sum(dh * n, axis=0, keepdims=True)
    d_shift = jnp.sum(dh, axis=0, keepdims=True)
    d_gain = jnp.sum(dn * xhat, axis=0, keepdims=True)
    return dx, d_scale, d_shift, d_gain


def _row_spec(w=D_MODEL, br=ROWS_EW):
    return pl.BlockSpec((br, w), lambda i: (i, 0))


def _vec_spec(r=8, w=D_MODEL):
    return pl.BlockSpec((r, w), lambda i: (0, 0))


def _norm1_fwd(x, gain, mod8, plan=None):
    T = x.shape[0]

    def body(x_ref, g_ref, m_ref, h_ref):
        h, _, _, _ = _rms_fwd_math(x_ref[...], g_ref[...], m_ref[1:2, :], m_ref[0:1, :])
        h_ref[...] = h.astype(BF16)

    return _pcall(
        body, plan=plan, name="norm1_fwd", grid=(T // ROWS_EW,),
        in_specs=[_row_spec(), _vec_spec(1), _vec_spec()],
        out_specs=_row_spec(), out_shape=jax.ShapeDtypeStruct((T, D_MODEL), BF16),
        compiler_params=_params(("parallel",)),
    )(x, gain, mod8)


def _res_norm2_fwd(x, mo, gain, mod8):
    T = x.shape[0]

    def body(x_ref, mo_ref, g_ref, m_ref, x1_ref, h_ref):
        x1 = x_ref[...] + m_ref[2:3, :] * mo_ref[...]
        x1_ref[...] = x1
        h, _, _, _ = _rms_fwd_math(x1, g_ref[...], m_ref[4:5, :], m_ref[3:4, :])
        h_ref[...] = h.astype(BF16)

    return _pcall(
        body, name="res_norm2_fwd", grid=(T // ROWS_EW,),
        in_specs=[_row_spec(), _row_spec(), _vec_spec(1), _vec_spec()],
        out_specs=[_row_spec(), _row_spec()],
        out_shape=[jax.ShapeDtypeStruct((T, D_MODEL), F32), jax.ShapeDtypeStruct((T, D_MODEL), BF16)],
        compiler_params=_params(("parallel",)),
    )(x, mo, gain, mod8)


def _loss_bwd(x1, mlp, target, mod8):
    T = x1.shape[0]

    def body(x1_ref, mlp_ref, t_ref, m_ref, dy_ref, dmlp_ref, st_ref):
        i = pl.program_id(0)
        gate = m_ref[5:6, :]
        mlp_v = mlp_ref[...]
        err = x1_ref[...] + gate * mlp_v - t_ref[...]
        dy = err * (1.0 / D_MODEL)
        dy_ref[...] = dy
        dmlp_ref[...] = (dy * gate).astype(BF16)

        @pl.when(i == 0)
        def _():
            st_ref[...] = jnp.zeros_like(st_ref)

        st_ref[0:1, :] += jnp.sum(err * err, axis=0, keepdims=True)
        st_ref[1:2, :] += jnp.sum(dy * mlp_v, axis=0, keepdims=True)

    return _pcall(
        body, name="loss_bwd", grid=(T // ROWS_EW,),
        in_specs=[_row_spec(), _row_spec(), _row_spec(), _vec_spec()],
        out_specs=[_row_spec(), _row_spec(), _vec_spec()],
        out_shape=[jax.ShapeDtypeStruct((T, D_MODEL), F32), jax.ShapeDtypeStruct((T, D_MODEL), BF16),
                   jax.ShapeDtypeStruct((8, D_MODEL), F32)],
        compiler_params=_params(("arbitrary",)),
    )(x1, mlp, target, mod8)


def _norm2_bwd(dh2, x1, dy, mo, gain, mod8):
    T = x1.shape[0]

    def body(dh_ref, x1_ref, dy_ref, mo_ref, g_ref, m_ref, dx1_ref, dmo_ref, st_ref):
        i = pl.program_id(0)
        gain_v, scale = g_ref[...], m_ref[4:5, :]
        _, xhat, n, rstd = _rms_fwd_math(x1_ref[...], gain_v, scale, m_ref[3:4, :])
        dx, d_scale, d_shift, d_gain = _rms_bwd_math(dh_ref[...], xhat, n, rstd, gain_v, scale)
        dx1 = dy_ref[...] + dx
        dx1_ref[...] = dx1
        dmo_ref[...] = (dx1 * m_ref[2:3, :]).astype(BF16)

        @pl.when(i == 0)
        def _():
            st_ref[...] = jnp.zeros_like(st_ref)

        st_ref[0:1, :] += d_scale
        st_ref[1:2, :] += d_shift
        st_ref[2:3, :] += d_gain
        st_ref[3:4, :] += jnp.sum(dx1 * mo_ref[...], axis=0, keepdims=True)

    return _pcall(
        body, name="norm2_bwd", grid=(T // ROWS_EW,),
        in_specs=[_row_spec(), _row_spec(), _row_spec(), _row_spec(), _vec_spec(1), _vec_spec()],
        out_specs=[_row_spec(), _row_spec(), _vec_spec()],
        out_shape=[jax.ShapeDtypeStruct((T, D_MODEL), F32), jax.ShapeDtypeStruct((T, D_MODEL), BF16),
                   jax.ShapeDtypeStruct((8, D_MODEL), F32)],
        compiler_params=_params(("arbitrary",)),
    )(dh2, x1, dy, mo, gain, mod8)


def _norm1_bwd(dh, x, dx1, gain, mod8):
    T = x.shape[0]

    def body(dh_ref, x_ref, dx1_ref, g_ref, m_ref, dx_ref, st_ref):
        i = pl.program_id(0)
        gain_v, scale = g_ref[...], m_ref[1:2, :]
        _, xhat, n, rstd = _rms_fwd_math(x_ref[...], gain_v, scale, m_ref[0:1, :])
        dx, d_scale, d_shift, d_gain = _rms_bwd_math(dh_ref[...], xhat, n, rstd, gain_v, scale)
        dx_ref[...] = dx1_ref[...] + dx

        @pl.when(i == 0)
        def _():
            st_ref[...] = jnp.zeros_like(st_ref)

        st_ref[0:1, :] += d_scale
        st_ref[1:2, :] += d_shift
        st_ref[2:3, :] += d_gain

    return _pcall(
        body, name="norm1_bwd", grid=(T // ROWS_EW,),
        in_specs=[_row_spec(), _row_spec(), _row_spec(), _vec_spec(1), _vec_spec()],
        out_specs=[_row_spec(), _vec_spec()],
        out_shape=[jax.ShapeDtypeStruct((T, D_MODEL), F32), jax.ShapeDtypeStruct((8, D_MODEL), F32)],
        compiler_params=_params(("arbitrary",)),
    )(dh, x, dx1, gain, mod8)


MERGE_BC = 512


def _hgrn_rows(T):
    return 512 if T >= 1024 else 128


def _lower_bound(lbl):
    e = jnp.exp(lbl - jnp.max(lbl, axis=0, keepdims=True))
    return e[0:1, :] / (e[0:1, :] + e[1:2, :])


def _chunk_sum_matrix(rows, backward):
    shift = A_CHUNK.bit_length() - 1
    r = lax.broadcasted_iota(jnp.int32, (rows, rows), 0)
    c = lax.broadcasted_iota(jnp.int32, (rows, rows), 1)
    same = jnp.right_shift(r, shift) == jnp.right_shift(c, shift)
    return (same & ((r <= c) if backward else (r >= c))).astype(BF16)


def _chunk_sums(m, x):
    n = x.shape[1]
    hi = x.astype(BF16)
    rest = x - hi.astype(F32)
    mid = rest.astype(BF16)
    lo = (rest - mid.astype(F32)).astype(BF16)
    y = jnp.dot(m, jnp.concatenate([hi, mid, lo], axis=1), preferred_element_type=F32)
    return y[:, 0:n] + y[:, n:2 * n] + y[:, 2 * n:3 * n]


def _hgrn_block_pre(q, fl, lb, m_fwd):
    sg = _sig(fl)
    f = lb + (1.0 - lb) * sg
    sq = _sig(q)
    return dict(sg=sg, f=f, k=1.0 - f, sq=sq, qf=q * sq, b=_chunk_sums(m_fwd, jnp.log(f)))


def _hgrn_chunk_fwd(pre, r, v, st):
    C = A_CHUNK
    qf, k, b = pre["qf"][r], pre["k"][r], pre["b"][r]
    causal = lax.broadcasted_iota(jnp.int32, (C, C), 0) >= lax.broadcasted_iota(jnp.int32, (C, C), 1)
    bm = b[C // 2 - 1:C // 2, :]
    bl = b[C - 1:C, :]
    e_q, e_k = jnp.exp(b - bm), jnp.exp(bm - b)
    e_b, e_l = jnp.exp(b), jnp.exp(bl - b)
    qd, kd = qf * e_q, k * e_k
    qe, ke = qf * e_b, k * e_l
    att = jnp.where(causal, _nt(qd, kd), 0.0)
    o = _nn(att, v) + _nt(qe, st)
    dec = jnp.exp(bl)
    st_next = st * dec + _tn(v, ke)
    return dict(causal=causal, e_q=e_q, e_k=e_k, e_b=e_b, e_l=e_l, qd=qd, kd=kd,
                qe=qe, ke=ke, att=att, o=o, dec=dec, st_next=st_next)


HGRN_HEADS_PER_STEP = 4


def _hgrn_fwd(proj, lb_logits, o_gain, plan=None):
    T = proj.shape[0]
    BR = _hgrn_rows(T)
    cps = BR // A_CHUNK
    K, NH = A_HEAD_DIM, HGRN_HEADS_PER_STEP
    W = NH * K

    def col(off):
        return pl.BlockSpec((BR, W), lambda h, cb: (cb, off // W + h))

    def body(q_ref, f_ref, i_ref, g_ref, lbl_ref, og_ref, o_ref, s_ref, st):
        @pl.when(pl.program_id(1) == 0)
        def _():
            st[...] = jnp.zeros_like(st)

        lb_all = _lower_bound(lbl_ref[...])
        m_fwd = _chunk_sum_matrix(BR, False)
        pre = [_hgrn_block_pre(q_ref[:, n * K:(n + 1) * K], f_ref[:, n * K:(n + 1) * K], lb_all[:, n * K:(n + 1) * K], m_fwd)
               for n in range(NH)]
        state = [st[n] for n in range(NH)]
        for ci in range(cps):
            r = slice(ci * A_CHUNK, (ci + 1) * A_CHUNK)
            for n in range(NH):
                hs = slice(n * K, (n + 1) * K)
                s_ref[n, ci] = state[n]
                c = _hgrn_chunk_fwd(pre[n], r, i_ref[r, hs], state[n])
                state[n] = c["st_next"]
                o = c["o"]
                on = o * lax.rsqrt(jnp.mean(o * o, axis=-1, keepdims=True) + EPS)
                g = g_ref[r, hs]
                o_ref[r, hs] = (on * og_ref[:, hs] * (g * _sig(g))).astype(BF16)
        for n in range(NH):
            st[n] = state[n]

    return _pcall(
        body, plan=plan, name="hgrn_fwd", grid=(A_HEADS // NH, T // BR),
        in_specs=[col(OFF_QA), col(OFF_FA), col(OFF_IA), col(OFF_GA),
                  pl.BlockSpec((2, W), lambda h, cb: (0, h)), pl.BlockSpec((1, W), lambda h, cb: (0, h))],
        out_specs=[pl.BlockSpec((BR, W), lambda h, cb: (cb, h)),
                   pl.BlockSpec((NH, cps, K, K), lambda h, cb: (h, cb, 0, 0))],
        out_shape=[jax.ShapeDtypeStruct((T, A_WIDTH), BF16),
                   jax.ShapeDtypeStruct((A_HEADS, T // A_CHUNK, K, K), F32)],
        scratch_shapes=[pltpu.VMEM((NH, K, K), F32)],
        compiler_params=_params(("parallel", "arbitrary")),
    )(proj, proj, proj, proj, lb_logits, o_gain)


def _hgrn_bwd(proj, lb_logits, o_gain, states, do, plan=None):
    T = proj.shape[0]
    BR = _hgrn_rows(T)
    cps = BR // A_CHUNK
    ncb = T // BR
    K, C, NH = A_HEAD_DIM, A_CHUNK, HGRN_HEADS_PER_STEP
    W = NH * K

    def col(off):
        return pl.BlockSpec((BR, W), lambda h, cb: (ncb - 1 - cb, off // W + h))

    def body(q_ref, f_ref, i_ref, g_ref, lbl_ref, og_ref, s_ref, do_ref,
             dq_ref, df_ref, di_ref, dg_ref, dlb_ref, dog_ref, dst):
        @pl.when(pl.program_id(1) == 0)
        def _():
            dst[...] = jnp.zeros_like(dst)
            dlb_ref[...] = jnp.zeros_like(dlb_ref)
            dog_ref[...] = jnp.zeros_like(dog_ref)

        lb_all = _lower_bound(lbl_ref[...])
        row = lax.broadcasted_iota(jnp.int32, (C, K), 0)
        m_fwd, m_bwd = _chunk_sum_matrix(BR, False), _chunk_sum_matrix(BR, True)
        pre = [_hgrn_block_pre(q_ref[:, n * K:(n + 1) * K], f_ref[:, n * K:(n + 1) * K], lb_all[:, n * K:(n + 1) * K], m_fwd)
               for n in range(NH)]
        d_state = [dst[n] for n in range(NH)]
        d_og = [jnp.zeros((1, K), F32) for _ in range(NH)]
        db_of = [[None] * cps for _ in range(NH)]
        dk_of = [[None] * cps for _ in range(NH)]
        for ci in reversed(range(cps)):
            r = slice(ci * C, (ci + 1) * C)
            for n in range(NH):
                hs = slice(n * K, (n + 1) * K)
                gain = og_ref[:, hs]
                st = s_ref[n, ci]
                v = i_ref[r, hs]
                q = q_ref[r, hs]
                c = _hgrn_chunk_fwd(pre[n], r, v, st)
                dst_next = d_state[n]
                o = c["o"]
                rn = lax.rsqrt(jnp.mean(o * o, axis=-1, keepdims=True) + EPS)
                on = o * rn
                g = g_ref[r, hs]
                sgg = _sig(g)
                dy = do_ref[r, hs]
                d_ong = dy * (g * sgg)
                dg_ref[r, hs] = (dy * (on * gain) * (sgg * (1.0 + g * (1.0 - sgg)))).astype(BF16)
                d_og[n] = d_og[n] + jnp.sum(d_ong * on, axis=0, keepdims=True)
                d_on = d_ong * gain
                d_o = rn * (d_on - on * jnp.mean(d_on * on, axis=-1, keepdims=True))
                datt = jnp.where(c["causal"], _nt(d_o, v), 0.0)
                dv = _tn(c["att"], d_o) + _nt(c["ke"], dst_next)
                dqd = _nn(datt, c["kd"])
                dkd = _tn(datt, c["qd"])
                dqe = _nn(d_o, st)
                dke = _nn(v, dst_next)
                d_state[n] = dst_next * c["dec"] + _tn(d_o, c["qe"])
                d_dec = jnp.sum(dst_next * st, axis=0, keepdims=True)
                t_q, t_k = dqd * c["qd"], dkd * c["kd"]
                t_e, t_l = dqe * c["qe"], dke * c["ke"]
                db = t_q - t_k + t_e - t_l
                dbm = jnp.sum(t_k - t_q, axis=0, keepdims=True)
                dbl = jnp.sum(t_l, axis=0, keepdims=True) + d_dec * c["dec"]
                db_of[n][ci] = db + jnp.where(row == C // 2 - 1, dbm, 0.0) + jnp.where(row == C - 1, dbl, 0.0)
                dqf = dqd * c["e_q"] + dqe * c["e_b"]
                sq = pre[n]["sq"][r]
                dq_ref[r, hs] = (dqf * (sq * (1.0 + q * (1.0 - sq)))).astype(BF16)
                dk_of[n][ci] = dkd * c["e_k"] + dke * c["e_l"]
                di_ref[r, hs] = dv.astype(BF16)
        for n in range(NH):
            hs = slice(n * K, (n + 1) * K)
            dst[n] = d_state[n]
            dog_ref[0:1, hs] += d_og[n]
            lb, sg = lb_all[:, hs], pre[n]["sg"]
            dlf = _chunk_sums(m_bwd, jnp.concatenate(db_of[n], axis=0))
            df = dlf / pre[n]["f"] - jnp.concatenate(dk_of[n], axis=0)
            df_ref[:, hs] = (df * (1.0 - lb) * sg * (1.0 - sg)).astype(BF16)
            dlb_ref[0:1, hs] += jnp.sum(df * (1.0 - sg), axis=0, keepdims=True)

    ocol = pl.BlockSpec((BR, W), lambda h, cb: (ncb - 1 - cb, h))
    vec = pl.BlockSpec((8, W), lambda h, cb: (0, h))
    return _pcall(
        body, plan=plan, name="hgrn_bwd", grid=(A_HEADS // NH, ncb),
        in_specs=[col(OFF_QA), col(OFF_FA), col(OFF_IA), col(OFF_GA),
                  pl.BlockSpec((2, W), lambda h, cb: (0, h)), pl.BlockSpec((1, W), lambda h, cb: (0, h)),
                  pl.BlockSpec((NH, cps, K, K), lambda h, cb: (h, ncb - 1 - cb, 0, 0)),
                  pl.BlockSpec((BR, W), lambda h, cb: (ncb - 1 - cb, h))],
        out_specs=[ocol, ocol, ocol, ocol, vec, vec],
        out_shape=[jax.ShapeDtypeStruct((T, A_WIDTH), BF16)] * 4 + [jax.ShapeDtypeStruct((8, A_WIDTH), F32)] * 2,
        scratch_shapes=[pltpu.VMEM((NH, K, K), F32)],
        compiler_params=_params(("parallel", "arbitrary")),
    )(proj, proj, proj, proj, lb_logits, o_gain, states, do)


def _head_norm(x):
    r = lax.rsqrt(jnp.mean(x * x, axis=-1, keepdims=True) + EPS)
    return x * r, r


def _head_norm_bwd(dy, xn, r, gain):
    dxn = dy * gain
    return r * (dxn - xn * jnp.mean(dxn * xn, axis=-1, keepdims=True)), jnp.sum(dy * xn, axis=0, keepdims=True)


def _swa_mask(has_prev):
    rows = B_GROUP * BLOCK
    r = lax.broadcasted_iota(jnp.int32, (rows, 2 * BLOCK), 0) % BLOCK
    c = lax.broadcasted_iota(jnp.int32, (rows, 2 * BLOCK), 1)
    rel = r + BLOCK - c
    return (rel >= 0) & (rel < BLOCK) & ((c >= BLOCK) | has_prev)


def _swa_head_fwd(j, q_ref, kp_ref, kc_ref, vp_ref, vc_ref, qg, kg, sk_ref, mask):
    hs = slice(j * B_HEAD_DIM, (j + 1) * B_HEAD_DIM)
    kcat = jnp.concatenate([kp_ref[:, hs], kc_ref[:, hs]], axis=0)
    vcat = jnp.concatenate([vp_ref[:, hs], vc_ref[:, hs]], axis=0)
    qs = jnp.concatenate([q_ref[:, pl.ds((j * B_GROUP + g) * B_HEAD_DIM, B_HEAD_DIM)] for g in range(B_GROUP)], axis=0)
    kn, kr = _head_norm(kcat)
    qn, qr = _head_norm(qs)
    kh, qh = kn * kg, qn * qg
    s = jnp.where(mask, _nt(qh, kh) * (B_HEAD_DIM ** -0.5), NEG_BIG)
    sink = jnp.concatenate(
        [jnp.broadcast_to(sk_ref[0:1, pl.ds(j * B_GROUP + g, 1)], (BLOCK, 1)) for g in range(B_GROUP)], axis=0)
    m = jnp.maximum(jnp.max(s, axis=-1, keepdims=True), sink)
    p = jnp.exp(s - m)
    e_sink = jnp.exp(sink - m)
    inv = 1.0 / (jnp.sum(p, axis=-1, keepdims=True) + e_sink)
    prob = p * inv
    return dict(vcat=vcat, kn=kn, kr=kr, qn=qn, qr=qr, kh=kh, qh=qh, prob=prob, p_sink=e_sink * inv)


def _swa_in_specs(nb, last):
    def qi(n):
        return jnp.minimum(n, last)

    q = pl.BlockSpec((BLOCK, B_WIDTH), lambda n: (qi(n), OFF_QB // B_WIDTH))
    kc = pl.BlockSpec((BLOCK, B_KV_WIDTH), lambda n: (qi(n), OFF_KB // B_KV_WIDTH))
    kp = pl.BlockSpec((BLOCK, B_KV_WIDTH), lambda n: (jnp.maximum(qi(n) - 1, 0), OFF_KB // B_KV_WIDTH))
    vc = pl.BlockSpec((BLOCK, B_KV_WIDTH), lambda n: (qi(n), OFF_VB // B_KV_WIDTH))
    vp = pl.BlockSpec((BLOCK, B_KV_WIDTH), lambda n: (jnp.maximum(qi(n) - 1, 0), OFF_VB // B_KV_WIDTH))
    small = [pl.BlockSpec((1, B_HEAD_DIM), lambda n: (0, 0)), pl.BlockSpec((1, B_HEAD_DIM), lambda n: (0, 0)),
             pl.BlockSpec((1, B_GROUP * B_KV_HEADS), lambda n: (0, 0))]
    return [q, kp, kc, vp, vc] + small


def _swa_fwd(proj, q_gain, k_gain, sinks, plan=None):
    T = proj.shape[0]
    nb = T // BLOCK

    def body(q_ref, kp_ref, kc_ref, vp_ref, vc_ref, qg_ref, kg_ref, sk_ref, o_ref):
        mask = _swa_mask(pl.program_id(0) > 0)
        for j in range(B_KV_HEADS):
            c = _swa_head_fwd(j, q_ref, kp_ref, kc_ref, vp_ref, vc_ref, qg_ref[...], kg_ref[...], sk_ref, mask)
            o = _nn(c["prob"], c["vcat"])
            for g in range(B_GROUP):
                o_ref[:, pl.ds((j * B_GROUP + g) * B_HEAD_DIM, B_HEAD_DIM)] = o[g * BLOCK:(g + 1) * BLOCK].astype(BF16)

    return _pcall(
        body, plan=plan, name="swa_fwd", grid=(nb,),
        in_specs=_swa_in_specs(nb, nb - 1),
        out_specs=pl.BlockSpec((BLOCK, B_WIDTH), lambda n: (n, 0)),
        out_shape=jax.ShapeDtypeStruct((T, B_WIDTH), BF16),
        compiler_params=_params(("parallel",)),
    )(proj, proj, proj, proj, proj, q_gain, k_gain, sinks)


def _swa_bwd(proj, q_gain, k_gain, sinks, do, plan=None):
    T = proj.shape[0]
    nb = T // BLOCK
    scale = B_HEAD_DIM ** -0.5

    def body(q_ref, kp_ref, kc_ref, vp_ref, vc_ref, qg_ref, kg_ref, sk_ref, do_ref,
             dq_ref, dkv_ref, sm_ref, ck, cv):
        n = pl.program_id(0)

        @pl.when(n == 0)
        def _():
            ck[...] = jnp.zeros_like(ck)
            cv[...] = jnp.zeros_like(cv)
            sm_ref[...] = jnp.zeros_like(sm_ref)

        @pl.when(n < nb)
        def _():
            mask = _swa_mask(n > 0)
            qg, kg = qg_ref[...], kg_ref[...]
            lane = lax.broadcasted_iota(jnp.int32, (1, BLOCK), 1)
            for j in range(B_KV_HEADS):
                hs = slice(j * B_HEAD_DIM, (j + 1) * B_HEAD_DIM)
                vs = slice(B_KV_WIDTH + j * B_HEAD_DIM, B_KV_WIDTH + (j + 1) * B_HEAD_DIM)
                c = _swa_head_fwd(j, q_ref, kp_ref, kc_ref, vp_ref, vc_ref, qg, kg, sk_ref, mask)
                d_out = jnp.concatenate(
                    [do_ref[:, pl.ds((j * B_GROUP + g) * B_HEAD_DIM, B_HEAD_DIM)] for g in range(B_GROUP)], axis=0)
                prob = c["prob"]
                out = _nn(prob, c["vcat"])
                delta = jnp.sum(d_out * out, axis=-1, keepdims=True)
                ds = prob * (_nt(d_out, c["vcat"]) - delta)
                d_sink = -c["p_sink"] * delta
                dqh = _nn(ds, c["kh"]) * scale
                dkh = _tn(ds, c["qh"]) * scale
                dv = _tn(prob, d_out)
                dq, dqg = _head_norm_bwd(dqh, c["qn"], c["qr"], qg)
                dk, dkg = _head_norm_bwd(dkh, c["kn"], c["kr"], kg)
                sm_ref[0:1, 0:B_HEAD_DIM] += dqg
                sm_ref[1:2, 0:B_HEAD_DIM] += dkg
                for g in range(B_GROUP):
                    dq_ref[:, pl.ds((j * B_GROUP + g) * B_HEAD_DIM, B_HEAD_DIM)] = dq[g * BLOCK:(g + 1) * BLOCK].astype(BF16)
                    tot = jnp.sum(d_sink[g * BLOCK:(g + 1) * BLOCK], axis=0, keepdims=True)
                    sm_ref[2:3, :] += jnp.where(lane == j * B_GROUP + g, tot, 0.0)
                dkv_ref[:, hs] = (ck[:, hs] + dk[0:BLOCK]).astype(BF16)
                dkv_ref[:, vs] = (cv[:, hs] + dv[0:BLOCK]).astype(BF16)
                ck[:, hs] = dk[BLOCK:2 * BLOCK]
                cv[:, hs] = dv[BLOCK:2 * BLOCK]

        @pl.when(n == nb)
        def _():
            dkv_ref[:, 0:B_KV_WIDTH] = ck[...].astype(BF16)
            dkv_ref[:, B_KV_WIDTH:2 * B_KV_WIDTH] = cv[...].astype(BF16)

    return _pcall(
        body, plan=plan, name="swa_bwd", grid=(nb + 1,),
        in_specs=_swa_in_specs(nb, nb - 1) + [pl.BlockSpec((BLOCK, B_WIDTH), lambda n: (jnp.minimum(n, nb - 1), 0))],
        out_specs=[pl.BlockSpec((BLOCK, B_WIDTH), lambda n: (jnp.minimum(n, nb - 1), 0)),
                   pl.BlockSpec((BLOCK, 2 * B_KV_WIDTH), lambda n: (jnp.maximum(n - 1, 0), 0)),
                   pl.BlockSpec((8, BLOCK), lambda n: (0, 0))],
        out_shape=[jax.ShapeDtypeStruct((T, B_WIDTH), BF16), jax.ShapeDtypeStruct((T, 2 * B_KV_WIDTH), BF16),
                   jax.ShapeDtypeStruct((8, BLOCK), F32)],
        scratch_shapes=[pltpu.VMEM((BLOCK, B_KV_WIDTH), F32), pltpu.VMEM((BLOCK, B_KV_WIDTH), F32)],
        compiler_params=_params(("arbitrary",)),
    )(proj, proj, proj, proj, proj, q_gain, k_gain, sinks, do)


W_IN, W_A, W_B, W_OUT, W_MI, W_MO = range(6)


def _local_step(x, target, mod8, norm1_gain, norm2_gain, lb_logits, o_gain, q_gain, k_gain, sinks, parts, c_arr, chip_arr):
    relu2 = lambda u: (u, jnp.square(jnp.maximum(u, 0.0)))
    pair, half = {}, {}

    def exchange(ws, grads):
        return _sibling_exchange_plan([_grad_view(g, w) for w, g in zip(ws, grads)])

    def pair_sums(ws, grads, others):
        for w, g, o in zip(ws, grads, others):
            pair[w] = _pair_sum(_grad_view(g, w), o, c_arr, f"pair_sum{w}")

    def sum_slots(ws, slots):
        for w, s in zip(ws, slots):
            half[w] = _sum_slots(pair[w], s, w, chip_arr, f"sum_slots{w}")

    h, (w_in,) = _norm1_fwd(x, norm1_gain, mod8, plan=_gather_plan({W_IN: parts[W_IN]}, pass_at=0.9))
    proj, (w_mi,) = _mm(h, w_in, name="mm_proj", bn=512, plan=_gather_plan({W_MI: parts[W_MI]}, pass_at=0.8))
    (o_a, states), (w_a, w_b, w_out) = _hgrn_fwd(
        proj, lb_logits, o_gain, plan=_gather_plan({w: parts[w] for w in (W_A, W_B, W_OUT)}, pass_at=0.8))
    o_b, (w_mo,) = _swa_fwd(proj, q_gain, k_gain, sinks, plan=_gather_plan({W_MO: parts[W_MO]}, pass_at=0.9))
    ya = _mm(o_a, w_a, name="mm_branch_a")
    gate_cols = (OFF_GATE_A // MERGE_BC, OFF_GATE_B // MERGE_BC)
    yb, merged = _mm(o_b, w_b, name="mm_branch_b", bn=MERGE_BC, out_dtypes=(F32, BF16),
                     extras=(proj, proj, ya), extra_cols=gate_cols + (0,),
                     epi=lambda acc, ga, gb, ya_: (acc, _sig(ga) * ya_ + _sig(gb) * acc))
    mo = _mm(merged, w_out, name="mm_out")
    x1, h2 = _res_norm2_fwd(x, mo, norm2_gain, mod8)
    u, act = _mm(h2, w_mi, name="mm_mlp_in", out_dtypes=(F32, BF16), epi=relu2)
    mlp = _mm(act, w_mo, name="mm_mlp_out")
    dy, dmlp, st_loss = _loss_bwd(x1, mlp, target, mod8)
    g_mo = _mm(act, dmlp, name="mm_g_mlp_out", ta=True, bm=512)
    du, others = _mm(dmlp, w_mo, name="mm_d_act", tb=True, out_dtypes=(BF16,), extras=(u,),
                     epi=lambda acc, uu: (acc * (2.0 * jnp.maximum(uu, 0.0)),), plan=exchange([W_MO], [g_mo]))
    pair_sums([W_MO], [g_mo], others)
    g_mi, slots_mo = _mm(h2, du, name="mm_g_mlp_in", ta=True, bm=512, plan=_chip_exchange_plan({W_MO: pair[W_MO]}))
    dh2, others = _mm(du, w_mi, name="mm_d_h2", tb=True, plan=exchange([W_MI], [g_mi]))
    pair_sums([W_MI], [g_mi], others)
    sum_slots([W_MO], slots_mo)
    dx1, dmo, st_n2 = _norm2_bwd(dh2, x1, dy, mo, norm2_gain, mod8)
    def merge_bwd(dm, ga, gb, ya_, yb_):
        sa, sb = _sig(ga), _sig(gb)
        return dm * sa, dm * sb, dm * ya_ * sa * (1.0 - sa), dm * yb_ * sb * (1.0 - sb)

    dya, dyb, dga, dgb = _mm(dmo, w_out, name="mm_d_merged", tb=True, bn=MERGE_BC, out_dtypes=(BF16,) * 4,
                             extras=(proj, proj, ya, yb), extra_cols=gate_cols + (0, 0), epi=merge_bwd)
    g_out = _mm(merged, dmo, name="mm_g_out", ta=True, bm=512)
    do_a = _mm(dya, w_a, name="mm_d_oa", tb=True)
    g_a = _mm(o_a, dya, name="mm_g_branch_a", ta=True, bm=512)
    do_b = _mm(dyb, w_b, name="mm_d_ob", tb=True)
    g_b = _mm(o_b, dyb, name="mm_g_branch_b", ta=True, bm=512)
    mid = [W_A, W_B, W_OUT]
    (dqa, dfa, dia, dgga, d_lb, d_og), res = _hgrn_bwd(
        proj, lb_logits, o_gain, states, do_a,
        plan=_join(_chip_exchange_plan({W_MI: pair[W_MI]}), exchange(mid, [g_a, g_b, g_out])))
    sum_slots([W_MI], res[:1])
    pair_sums(mid, [g_a, g_b, g_out], res[1:])
    (dqb, dkvb, st_swa), slots_mid = _swa_bwd(proj, q_gain, k_gain, sinks, do_b,
                                              plan=_chip_exchange_plan({w: pair[w] for w in mid}))
    sum_slots(mid, slots_mid)
    dproj = jnp.concatenate([dqa, dfa, dia, dgga, dqb, dkvb, dga, dgb], axis=1)
    hr = D_MODEL // 2
    h_send = lax.dynamic_slice(h, (0, (1 - c_arr[0]) * hr), (h.shape[0], hr))
    h_own = lax.dynamic_slice(h, (0, c_arr[0] * hr), (h.shape[0], hr))
    done = [W_A, W_B, W_OUT, W_MI, W_MO]
    g_send, res = _mm(h_send, dproj, name="mm_g_in_send", ta=True, bm=512, bn=512,
                      plan=_sibling_share_plan([half[w] for w in done]))
    theirs = dict(zip(done, res))
    g_own, (g_other,) = _mm(h_own, dproj, name="mm_g_in_own", ta=True, bm=512, bn=512, plan=_sibling_share_plan([g_send]))
    pair[W_IN] = _add_bf16(g_own, g_other, "pair_sum0")[None]
    dh, slots_in = _mm(dproj, w_in, name="mm_d_h", tb=True, bk=2432, plan=_chip_exchange_plan({W_IN: pair[W_IN]}))
    sum_slots([W_IN], slots_in)
    grad_x, st_n1 = _norm1_bwd(dh, x, dx1, norm1_gain, mod8)
    (theirs[W_IN],) = _run_plan(_sibling_share_plan([half[W_IN]]), "sibling_share_w_in")
    stats = dict(loss=st_loss, n2=st_n2, n1=st_n1, d_lb=d_lb, d_og=d_og, swa=st_swa)
    return grad_x, [half[w] for w in range(N_W)], [theirs[w] for w in range(N_W)], stats


def _ew_rows(rows, cols):
    br = 8
    while br * 2 <= rows and br * 2 * cols * 4 <= (1 << 20) and rows % (br * 2) == 0:
        br *= 2
    return br


def _cast_into_full(shard, w, chip_arr, name):
    sr, sc = shard.shape
    R, C, by_col = W_SHAPES[w]
    br = _ew_rows(sr, sc)
    nb = sr // br
    out_map = (lambda i, chip: (i, chip[0])) if by_col else (lambda i, chip: (chip[0] * nb + i, 0))

    def body(chip_ref, w_ref, o_ref):
        o_ref[...] = w_ref[...].astype(BF16)

    return _pcall(
        body, name=name,
        grid_spec=pltpu.PrefetchScalarGridSpec(
            num_scalar_prefetch=1, grid=(nb,),
            in_specs=[pl.BlockSpec((br, sc), lambda i, chip: (i, 0))],
            out_specs=pl.BlockSpec((br, sc), out_map)),
        out_shape=jax.ShapeDtypeStruct((R, C), BF16), compiler_params=_params(("parallel",)))(chip_arr, shard)


def _adamw_math(w, g, m, v):
    m = ADAM_B1 * m + (1.0 - ADAM_B1) * g
    v = ADAM_B2 * v + (1.0 - ADAM_B2) * (g * g)
    m_hat = m / (1.0 - ADAM_B1 ** ADAM_STEP)
    v_hat = v / (1.0 - ADAM_B2 ** ADAM_STEP)
    delta = -ADAM_LR * (m_hat / (jnp.sqrt(v_hat) + ADAM_EPS) + ADAM_WD * w)
    return delta, m, v


def _adamw(w, g, m, v, name):
    R, C = w.shape
    br = _ew_rows(R, C)
    spec = pl.BlockSpec((br, C), lambda i: (i, 0))

    def body(w_ref, g_ref, m_ref, v_ref, d_ref, nm_ref, nv_ref):
        d_ref[...], nm_ref[...], nv_ref[...] = _adamw_math(w_ref[...], g_ref[...], m_ref[...], v_ref[...])

    sh = jax.ShapeDtypeStruct((R, C), F32)
    return _pcall(body, name=name, grid=(R // br,), in_specs=[spec] * 4, out_specs=[spec] * 3, out_shape=[sh] * 3,
                  compiler_params=_params(("parallel",)))(w, g, m, v)


def _add_bf16(a, b, name):
    R, C = a.shape
    br = _ew_rows(R, C)
    spec = pl.BlockSpec((br, C), lambda i: (i, 0))

    def body(a_ref, b_ref, o_ref):
        o_ref[...] = (a_ref[...] + b_ref[...]).astype(BF16)

    return _pcall(body, name=name, grid=(R // br,), in_specs=[spec, spec], out_specs=spec,
                  out_shape=jax.ShapeDtypeStruct((R, C), BF16), compiler_params=_params(("parallel",)))(a, b)


def _adamw_halves(w, own, other, m, v, c_arr, name):
    R, C = w.shape
    hr = R // 2
    br = _ew_rows(hr, C)
    nb = hr // br
    full = pl.BlockSpec((br, C), lambda h, i, c_ref: (h * nb + i, 0))
    half = pl.BlockSpec((br, C), lambda h, i, c_ref: (i, 0))

    def body(c_ref, w_ref, own_ref, oth_ref, m_ref, v_ref, g_ref, d_ref, nm_ref, nv_ref):
        g = jnp.where(pl.program_id(0) == c_ref[0], own_ref[...], oth_ref[...])
        g_ref[...] = g
        d_ref[...], nm_ref[...], nv_ref[...] = _adamw_math(w_ref[...], g, m_ref[...], v_ref[...])

    sh = jax.ShapeDtypeStruct((R, C), F32)
    return _pcall(
        body, name=name,
        grid_spec=pltpu.PrefetchScalarGridSpec(
            num_scalar_prefetch=1, grid=(2, nb), in_specs=[full, half, half, full, full], out_specs=[full] * 4),
        out_shape=[sh] * 4, compiler_params=_params(("parallel", "parallel")))(c_arr, w, own, other, m, v)


def _ada_grad_adamw(c_t, dmod, w, m, v):
    R, C = w.shape
    br = _ew_rows(R, C)
    spec = pl.BlockSpec((br, C), lambda i: (i, 0))

    def body(c_ref, dm_ref, w_ref, m_ref, v_ref, g_ref, d_ref, nm_ref, nv_ref):
        cv = c_ref[...]
        sc = cv * _sig(cv)
        g = sc[:, 0:1] * dm_ref[0:1, :]
        for b in range(1, N_DEV):
            g = g + sc[:, b:b + 1] * dm_ref[b:b + 1, :]
        g_ref[...] = g
        d_ref[...], nm_ref[...], nv_ref[...] = _adamw_math(w_ref[...], g, m_ref[...], v_ref[...])

    sh = jax.ShapeDtypeStruct((R, C), F32)
    return _pcall(
        body, name="ada_grad_adamw", grid=(R // br,),
        in_specs=[pl.BlockSpec((br, N_DEV), lambda i: (i, 0)), pl.BlockSpec((N_DEV, C), lambda i: (0, 0)), spec, spec, spec],
        out_specs=[spec] * 4, out_shape=[sh] * 4, compiler_params=_params(("parallel",)))(c_t, dmod, w, m, v)


SMALL_ROWS = 16


def _small_sum(small_all, lb_logits):
    def body(s_ref, lbl_ref, o_ref):
        acc = s_ref[0:SMALL_ROWS, :]
        for d in range(1, N_DEV):
            acc = acc + s_ref[d * SMALL_ROWS:(d + 1) * SMALL_ROWS, :]
        o_ref[...] = acc
        z = lbl_ref[...]
        e = jnp.exp(z - jnp.max(z, axis=0, keepdims=True))
        p0 = e[0:1, :] / (e[0:1, :] + e[1:2, :])
        dz = acc[8:9, 0:A_WIDTH] * p0 * (1.0 - p0)
        o_ref[8:9, 0:A_WIDTH] = dz
        o_ref[10:11, 0:A_WIDTH] = -dz

    return _pcall(body, name="small_sum", out_shape=jax.ShapeDtypeStruct((SMALL_ROWS, D_MODEL), F32),
                  in_specs=[pl.BlockSpec(memory_space=pltpu.VMEM)] * 2, out_specs=pl.BlockSpec(memory_space=pltpu.VMEM),
                  compiler_params=_params())(small_all, lb_logits)


RELATIONS = ((1, 0), (0, 1), (1, 1))
ANY = pl.BlockSpec(memory_space=pl.ANY)


def _place():
    x, y, c = lax.axis_index("x"), lax.axis_index("y"), lax.axis_index("c")
    return x, y, c


def _allgather_small(x_shard, name):
    m_per, n = x_shard.shape

    def body(x_ref, out_ref, send_sems, recv_sems, local_sem):
        x, y, c = _place()
        me, sibling = (x, y, c), (x, y, 1 - c)
        chips = [(1 - x, y), (x, 1 - y), (1 - x, 1 - y)]

        def rows(px, py, pc):
            return out_ref.at[pl.ds((4 * px + 2 * py + pc) * m_per, m_per), :]

        def copy(k, block, to, src=None):
            return pltpu.make_async_remote_copy(
                src_ref=rows(*block) if src is None else src, dst_ref=rows(*block),
                send_sem=send_sems.at[k], recv_sem=recv_sems.at[k], device_id=to, device_id_type=MESH)

        mine = pltpu.make_async_copy(x_ref, rows(*me), local_sem)
        mine.start()
        first = [copy(0, me, sibling, src=x_ref)]
        first += [copy(1 + j, me, (*chip, c), src=x_ref) for j, chip in enumerate(chips)]
        for cp in first:
            cp.start()
        passed = [copy(4 + j, (*chip, c), sibling) for j, chip in enumerate(chips)]
        for j, chip in enumerate(chips):
            copy(1 + j, (*chip, c), me).wait_recv()
            passed[j].start()
        copy(0, sibling, me).wait_recv()
        for j, chip in enumerate(chips):
            copy(4 + j, (*chip, 1 - c), me).wait_recv()
        for cp in first + passed:
            cp.wait_send()
        mine.wait()

    return _pcall(
        body, name=name, out_shape=jax.ShapeDtypeStruct((N_DEV * m_per, n), x_shard.dtype),
        in_specs=[pl.BlockSpec(memory_space=pltpu.VMEM)], out_specs=pl.BlockSpec(memory_space=pltpu.VMEM),
        scratch_shapes=[pltpu.SemaphoreType.DMA((7,)), pltpu.SemaphoreType.DMA((7,)), pltpu.SemaphoreType.DMA],
        compiler_params=_params(),
    )(x_shard)


W_SHAPES = ((D_MODEL, IN_WIDTH, True), (A_WIDTH, D_MODEL, True), (B_WIDTH, D_MODEL, True),
            (D_MODEL, D_MODEL, False), (D_MODEL, MLP_HIDDEN, True), (MLP_HIDDEN, D_MODEL, False))
N_W = len(W_SHAPES)


def _shard_shape(w):
    R, C, by_col = W_SHAPES[w]
    return (R, C // N_CHIPS) if by_col else (R // N_CHIPS, C)


def _half_shape(w):
    sr, sc = _shard_shape(w)
    return sr // 2, sc


def _region(full_ref, w, chip, half):
    sr, sc = _shard_shape(w)
    by_col = W_SHAPES[w][2]
    r0, c0 = (0, chip * sc) if by_col else (chip * sr, 0)
    if half is None:
        return full_ref.at[pl.ds(r0, sr), pl.ds(c0, sc)]
    return full_ref.at[pl.ds(r0 + half * (sr // 2), sr // 2), pl.ds(c0, sc)]


def _on_device(fn):
    x, y, c = _place()
    me = 4 * x + 2 * y + c
    for d in range(N_DEV):
        @pl.when(me == d)
        def _(d=d):
            fn(x, y, c, d)


def _gather_plan(partials, pass_at=0.5):
    ws = sorted(partials)
    pairs = [(i, w, k) for i, w in enumerate(ws) for k in range(3)]

    def first(pi, po, ps, x, y, c, d, i, w, k):
        chip, dc = d >> 1, d & 1
        rx, ry = RELATIONS[k]
        return pltpu.make_async_remote_copy(
            src_ref=_region(pi[i], w, chip, dc), dst_ref=_region(po[i], w, chip, dc),
            send_sem=ps[0].at[i * 3 + k], recv_sem=ps[1].at[i * 3 + k],
            device_id=(x ^ rx, y ^ ry, c), device_id_type=MESH)

    def landed(po, ps, x, y, c, d, i, w, k, half, to_sibling):
        rx, ry = RELATIONS[k]
        got = _region(po[i], w, (d >> 1) ^ (2 * rx + ry), half)
        s = 2 if to_sibling else 0
        return pltpu.make_async_remote_copy(
            src_ref=got, dst_ref=got, send_sem=ps[s].at[i * 3 + k], recv_sem=ps[s + 1].at[i * 3 + k],
            device_id=(x, y, 1 - c), device_id_type=MESH)

    def send(pi, po, ps):
        def run(x, y, c, d):
            for i, w, k in pairs:
                first(pi, po, ps, x, y, c, d, i, w, k).start()
        _on_device(run)

    def pass_on(pi, po, ps):
        def run(x, y, c, d):
            for i, w, k in pairs:
                landed(po, ps, x, y, c, d, i, w, k, d & 1, False).wait_recv()
                landed(po, ps, x, y, c, d, i, w, k, d & 1, True).start()
        _on_device(run)

    def finish(pi, po, ps):
        def run(x, y, c, d):
            for i, w, k in pairs:
                landed(po, ps, x, y, c, d, i, w, k, 1 - (d & 1), True).wait_recv()
            for i, w, k in pairs:
                first(pi, po, ps, x, y, c, d, i, w, k).wait_send()
                landed(po, ps, x, y, c, d, i, w, k, d & 1, True).wait_send()
        _on_device(run)

    return _Plan([partials[w] for w in ws], [jax.ShapeDtypeStruct(W_SHAPES[w][:2], BF16) for w in ws],
                 [pltpu.SemaphoreType.DMA((3 * len(ws),)) for _ in range(4)], [send, pass_on, finish],
                 {i: i for i in range(len(ws))}, mid_at=(pass_at,))


def _grad_view(g, w):
    R, C, by_col = W_SHAPES[w]
    return g.reshape(1, 2, R // 2, C) if by_col else g.reshape(N_CHIPS, 2, R // N_CHIPS // 2, C)


def _start_wait_plan(ins, outs, n_copies, copies):
    def start(pi, po, ps):
        for cp in copies(pi, po, ps):
            cp.start()

    def finish(pi, po, ps):
        for cp in copies(pi, po, ps):
            cp.wait()

    return _Plan(ins, outs, [pltpu.SemaphoreType.DMA((n_copies,)), pltpu.SemaphoreType.DMA((n_copies,))], [start, finish])


def _sibling_exchange_plan(g4s):
    pieces = [(i, p) for i, g in enumerate(g4s) for p in range(g.shape[0])]

    def copies(pi, po, ps):
        x, y, c = _place()
        return [pltpu.make_async_remote_copy(
            src_ref=pi[i].at[p, 1 - c], dst_ref=po[i].at[p], send_sem=ps[0].at[n], recv_sem=ps[1].at[n],
            device_id=(x, y, 1 - c), device_id_type=MESH) for n, (i, p) in enumerate(pieces)]

    return _start_wait_plan(list(g4s), [jax.ShapeDtypeStruct((g.shape[0],) + g.shape[2:], F32) for g in g4s],
                            len(pieces), copies)


def _pair_sum(g4, other, c_arr, name):
    P, _, hr, C = g4.shape
    br = _ew_rows(hr, C)

    def body(c_ref, g_ref, o_ref, p_ref):
        p_ref[...] = (g_ref[...] + o_ref[...]).astype(BF16)

    return _pcall(
        body, name=name,
        grid_spec=pltpu.PrefetchScalarGridSpec(
            num_scalar_prefetch=1, grid=(P, hr // br),
            in_specs=[pl.BlockSpec((None, None, br, C), lambda p, i, c_ref: (p, c_ref[0], i, 0)),
                      pl.BlockSpec((None, br, C), lambda p, i, c_ref: (p, i, 0))],
            out_specs=pl.BlockSpec((None, br, C), lambda p, i, c_ref: (p, i, 0))),
        out_shape=jax.ShapeDtypeStruct((P, hr, C), BF16),
        compiler_params=_params(("parallel", "parallel")),
    )(c_arr, g4, other)


def _pair_part(p_ref, w, chip):
    sr, sc = _shard_shape(w)
    return p_ref.at[0, :, pl.ds(chip * sc, sc)] if W_SHAPES[w][2] else p_ref.at[chip]


def _chip_exchange_plan(pairs):
    ws = sorted(pairs)

    def stage(wait):
        def run(pi, po, ps):
            def on(x, y, c, d):
                for i, w in enumerate(ws):
                    for k, (rx, ry) in enumerate(RELATIONS):
                        cp = pltpu.make_async_remote_copy(
                            src_ref=_pair_part(pi[i], w, (d >> 1) ^ (2 * rx + ry)), dst_ref=po[i].at[k],
                            send_sem=ps[0].at[i * 3 + k], recv_sem=ps[1].at[i * 3 + k],
                            device_id=(x ^ rx, y ^ ry, c), device_id_type=MESH)
                        if wait:
                            cp.wait()
                        else:
                            cp.start()
            _on_device(on)
        return run

    return _Plan([pairs[w] for w in ws], [jax.ShapeDtypeStruct((3,) + _half_shape(w), BF16) for w in ws],
                 [pltpu.SemaphoreType.DMA((3 * len(ws),)), pltpu.SemaphoreType.DMA((3 * len(ws),))],
                 [stage(False), stage(True)])


def _sum_slots(pair, slots, w, chip_arr, name):
    _, hr, C = slots.shape
    br = _ew_rows(hr, C)
    own_map = (lambda i, chip: (0, i, chip[0])) if W_SHAPES[w][2] else (lambda i, chip: (chip[0], i, 0))

    def body(chip_ref, p_ref, s_ref, o_ref):
        acc = p_ref[...].astype(F32)
        for k in range(3):
            acc = acc + s_ref[k].astype(F32)
        o_ref[...] = acc

    return _pcall(
        body, name=name,
        grid_spec=pltpu.PrefetchScalarGridSpec(
            num_scalar_prefetch=1, grid=(hr // br,),
            in_specs=[pl.BlockSpec((None, br, C), own_map), pl.BlockSpec((3, br, C), lambda i, chip: (0, i, 0))],
            out_specs=pl.BlockSpec((br, C), lambda i, chip: (i, 0))),
        out_shape=jax.ShapeDtypeStruct((hr, C), F32), compiler_params=_params(("parallel",)),
    )(chip_arr, pair, slots)


def _sibling_share_plan(halves):
    def copies(pi, po, ps):
        x, y, c = _place()
        return [pltpu.make_async_remote_copy(
            src_ref=pi[i], dst_ref=po[i], send_sem=ps[0].at[i], recv_sem=ps[1].at[i],
            device_id=(x, y, 1 - c), device_id_type=MESH) for i in range(len(halves))]

    return _start_wait_plan(list(halves), [jax.ShapeDtypeStruct(h.shape, F32) for h in halves], len(halves), copies)


def _pad_lanes(v, width=D_MODEL):
    return jnp.pad(v, ((0, 0), (0, width - v.shape[1])))


def _pack_small(b_ada, norm1, norm2, lb, o_gain, q_gain, k_gain, sinks):
    rows = [b_ada.reshape(N_MOD, D_MODEL), norm1, norm2, jnp.concatenate([lb[0:1], o_gain], axis=1),
            _pad_lanes(jnp.concatenate([q_gain, k_gain, sinks], axis=1)), _pad_lanes(lb[1:2]),
            jnp.zeros((SMALL_ROWS - 11, D_MODEL), F32)]
    return jnp.concatenate(rows, axis=0)


def _unpack_small(p):
    return (p[0:6].reshape(1, N_MOD * D_MODEL), p[6:7], p[7:8],
            jnp.concatenate([p[8:9, 0:A_WIDTH], p[10:11, 0:A_WIDTH]], axis=0), p[8:9, A_WIDTH:],
            p[9:10, 0:64], p[9:10, 64:128], p[9:10, 128:144])


def kernel(x, c, w_ada, b_ada, norm1_gain, w_in, lb_logits, hgrn_o_gain, q_norm_gain, k_norm_gain, sinks, w_branch_a, w_branch_b, w_out, norm2_gain, w_mlp_in, w_mlp_out, loss_target, m_w_ada, m_b_ada, m_norm1_gain, m_w_in, m_lb_logits, m_hgrn_o_gain, m_q_norm_gain, m_k_norm_gain, m_sinks, m_w_branch_a, m_w_branch_b, m_w_out, m_norm2_gain, m_w_mlp_in, m_w_mlp_out, v_w_ada, v_b_ada, v_norm1_gain, v_w_in, v_lb_logits, v_hgrn_o_gain, v_q_norm_gain, v_k_norm_gain, v_sinks, v_w_branch_a, v_w_branch_b, v_w_out, v_norm2_gain, v_w_mlp_in, v_w_mlp_out):
    xi, yi, ci = _place()
    chip = 2 * xi + yi
    me = 4 * xi + 2 * yi + ci
    ada_cols = w_ada.shape[2]

    c_all = _allgather_small(jnp.broadcast_to(c, (8, D_MODEL)), "gather_c").reshape(N_DEV, 8, D_MODEL)[:, 0]
    b_cols = lax.dynamic_slice(b_ada, (0, chip * ada_cols), (1, ada_cols))
    mod_part = _ada_fwd(c_all, w_ada[0], b_cols)
    mod_all = _allgather_small(mod_part, "gather_mod").reshape(N_CHIPS, 2, N_DEV, ada_cols)[:, 0]
    mod_mine = lax.dynamic_index_in_dim(mod_all, me, axis=1, keepdims=False).reshape(N_MOD, D_MODEL)
    mod8 = jnp.concatenate([mod_mine, jnp.zeros((2, D_MODEL), F32)], axis=0)

    shards = (w_in[0], w_branch_a[0], w_branch_b[0], w_out[0], w_mlp_in[0], w_mlp_out[0])
    chip_arr = chip.astype(jnp.int32).reshape(1)
    c_arr = ci.astype(jnp.int32).reshape(1)
    parts = [_cast_into_full(s, w, chip_arr, f"cast_w{w}") for w, s in enumerate(shards)]

    grad_x, halves, theirs, st = _local_step(x[0], loss_target[0], mod8, norm1_gain, norm2_gain, lb_logits, hgrn_o_gain,
                                             q_norm_gain, k_norm_gain, sinks, parts, c_arr, chip_arr)
    loss = lax.psum(0.5 * jnp.sum(st["loss"][0]) / D_MODEL, ("x", "y", "c"))
    moments = ((m_w_in, v_w_in), (m_w_branch_a, v_w_branch_a), (m_w_branch_b, v_w_branch_b), (m_w_out, v_w_out),
               (m_w_mlp_in, v_w_mlp_in), (m_w_mlp_out, v_w_mlp_out))
    big = [_adamw_halves(shards[w], halves[w], theirs[w], moments[w][0][0], moments[w][1][0], c_arr, f"adamw{w}")
           for w in range(N_W)]

    swa = st["swa"]
    small = jnp.concatenate([
        st["n1"][1:2], st["n1"][0:1], st["n2"][3:4], st["n2"][1:2], st["n2"][0:1], st["loss"][1:2],
        st["n1"][2:3], st["n2"][2:3], jnp.concatenate([st["d_lb"][0:1], st["d_og"][0:1]], axis=1),
        _pad_lanes(jnp.concatenate([swa[0:1, 0:64], swa[1:2, 0:64], swa[2:3, 0:16]], axis=1)),
        jnp.zeros((SMALL_ROWS - 10, D_MODEL), F32)], axis=0)
    small_all = _allgather_small(small, "gather_small")
    g_small = _small_sum(small_all, lb_logits)
    small_w = (b_ada, norm1_gain, norm2_gain, lb_logits, hgrn_o_gain, q_norm_gain, k_norm_gain, sinks)
    small_m = (m_b_ada, m_norm1_gain, m_norm2_gain, m_lb_logits, m_hgrn_o_gain, m_q_norm_gain, m_k_norm_gain, m_sinks)
    small_v = (v_b_ada, v_norm1_gain, v_norm2_gain, v_lb_logits, v_hgrn_o_gain, v_q_norm_gain, v_k_norm_gain, v_sinks)
    sm = [_unpack_small(t) for t in
          (g_small,) + tuple(_adamw(_pack_small(*small_w), g_small, _pack_small(*small_m), _pack_small(*small_v),
                                    "adamw_small"))]
    g_b, g_n1, g_n2, g_lb, g_og, g_qg, g_kg, g_sk = ([t[i] for t in sm] for i in range(8))

    dmod_all = small_all.reshape(N_DEV, SMALL_ROWS, D_MODEL)[:, 0:N_MOD].reshape(N_DEV, N_MOD * D_MODEL)
    dmod_cols = lax.dynamic_slice(dmod_all, (0, chip * ada_cols), (N_DEV, ada_cols))
    ada = _ada_grad_adamw(c_all.T, dmod_cols, w_ada[0], m_w_ada[0], v_w_ada[0])

    def ordered(k):
        lead = lambda a: a[None]
        return (lead(ada[k]), g_b[k], g_n1[k], lead(big[0][k]), g_lb[k], g_og[k], g_qg[k], g_kg[k], g_sk[k],
                lead(big[1][k]), lead(big[2][k]), lead(big[3][k]), g_n2[k], lead(big[4][k]), lead(big[5][k]))

    return (loss, grad_x[None]) + ordered(0) + ordered(1) + ordered(2) + ordered(3)
```

```python
import functools

import jax
import jax.numpy as jnp
from jax import lax
from jax.experimental import pallas as pl
from jax.experimental.pallas import tpu as pltpu

F32 = jnp.float32
BF16 = jnp.bfloat16
HIGHEST = lax.Precision.HIGHEST
MESH = pl.DeviceIdType.MESH

D_MODEL = 2048
A_WIDTH = 1024
A_HEADS = 8
A_HEAD_DIM = 128
A_CHUNK = 64
B_WIDTH = 1024
B_HEAD_DIM = 64
B_GROUP = 4
B_KV_HEADS = 4
B_KV_WIDTH = 256
BLOCK = 128
MLP_HIDDEN = 8192
IN_WIDTH = 9728
N_MOD = 6
EPS = 1e-6
N_CHIPS = 4
N_DEV = 8

OFF_QA, OFF_FA, OFF_IA, OFF_GA = 0, 1024, 2048, 3072
OFF_QB, OFF_KB, OFF_VB = 4096, 5120, 5376
OFF_GATE_A, OFF_GATE_B = 5632, 7680

ADAM_LR = 0.001
ADAM_B1 = 0.9
ADAM_B2 = 0.999
ADAM_EPS = 1e-08
ADAM_WD = 0.01
ADAM_STEP = 10

VMEM_LIMIT_V7X = 48 * 1024 * 1024
NEG_BIG = -1e30


def _params(sem=None, vmem=VMEM_LIMIT_V7X):
    return pltpu.CompilerParams(dimension_semantics=sem, vmem_limit_bytes=vmem)


class _Plan:
    def __init__(self, ins, outs, sems, stages, aliases=None, mid_at=()):
        self.ins, self.outs, self.sems, self.stages, self.aliases = ins, outs, sems, stages, aliases or {}
        self.mid_at = tuple(mid_at)
        assert len(self.mid_at) == len(stages) - 2


def _join(a, b):
    assert len(a.stages) == 2 and len(b.stages) == 2
    ni, no, ns = len(a.ins), len(a.outs), len(a.sems)

    def stage(k):
        def run(pi, po, ps):
            a.stages[k](pi[:ni], po[:no], ps[:ns])
            b.stages[k](pi[ni:], po[no:], ps[ns:])
        return run

    aliases = dict(a.aliases)
    aliases.update({ni + i: no + o for i, o in b.aliases.items()})
    return _Plan(a.ins + b.ins, a.outs + b.outs, a.sems + b.sems, [stage(0), stage(1)], aliases)


def _pcall(body, plan=None, **kw):
    if plan is None:
        return pl.pallas_call(body, **kw)
    grid = kw["grid"]
    single = not isinstance(kw["out_specs"], (list, tuple))
    in_specs = list(kw["in_specs"])
    out_specs = [kw["out_specs"]] if single else list(kw["out_specs"])
    out_shape = [kw["out_shape"]] if single else list(kw["out_shape"])
    scratch = list(kw.get("scratch_shapes", ()))
    n_in, n_out, n_scr = len(in_specs), len(out_specs), len(scratch)
    n_pi, n_po = len(plan.ins), len(plan.outs)
    total = 1
    for g in grid:
        total *= g
    n_st = len(plan.stages)

    def wrapped(*refs):
        o0 = n_in + n_pi
        s0 = o0 + n_out + n_po
        pi, po, ps = refs[n_in:o0], refs[o0 + n_out:s0], refs[s0 + n_scr:]
        lin = 0
        for d, g in enumerate(grid):
            lin = lin * g + pl.program_id(d)
        for si, frac in enumerate((0.0,) + plan.mid_at):
            @pl.when(lin == int(frac * (total - 1)))
            def _(si=si):
                plan.stages[si](pi, po, ps)
        body(*refs[:n_in], *refs[o0:o0 + n_out], *refs[s0:s0 + n_scr])

        @pl.when(lin == total - 1)
        def _():
            plan.stages[-1](pi, po, ps)

    any_spec = pl.BlockSpec(memory_space=pl.ANY)
    call = pl.pallas_call(
        wrapped, name=kw["name"], grid=grid, in_specs=in_specs + [any_spec] * n_pi,
        out_specs=out_specs + [any_spec] * n_po, out_shape=out_shape + list(plan.outs),
        scratch_shapes=scratch + list(plan.sems),
        input_output_aliases={n_in + i: n_out + o for i, o in plan.aliases.items()},
        compiler_params=_params(("arbitrary",) * len(grid)))

    def run(*args):
        res = call(*args, *plan.ins)
        outs = list(res[:n_out])
        return (outs[0] if single else outs), list(res[n_out:])

    return run


def _run_plan(plan, name):
    return _pcall(lambda: None, plan=plan, name=name, grid=(1,), in_specs=[], out_specs=[], out_shape=[])()[1]


def _sig(x):
    return 1.0 / (1.0 + jnp.exp(-x))


def _nn(a, b):
    return lax.dot_general(a.astype(BF16), b.astype(BF16), (((1,), (0,)), ((), ())), preferred_element_type=F32)


def _nt(a, b):
    return lax.dot_general(a.astype(BF16), b.astype(BF16), (((1,), (1,)), ((), ())), preferred_element_type=F32)


def _tn(a, b):
    return lax.dot_general(a.astype(BF16), b.astype(BF16), (((0,), (0,)), ((), ())), preferred_element_type=F32)


def _mm(a, b, *, name, ta=False, tb=False, bm=1024, bn=1024, bk=2048, out_dtypes=(F32,), epi=None, extras=(),
        extra_cols=None, plan=None):
    if ta:
        K, M = a.shape
        bk = K
    else:
        M, K = a.shape
    if tb:
        N, K2 = b.shape
    else:
        K2, N = b.shape
    bm, bn, bk = min(bm, M), min(bn, N), min(bk, K)
    assert K == K2 and M % bm == 0 and N % bn == 0 and K % bk == 0, (name, a.shape, b.shape)
    nk = K // bk
    a_spec = pl.BlockSpec((bk, bm), lambda i, j, k: (k, i)) if ta else pl.BlockSpec((bm, bk), lambda i, j, k: (i, k))
    b_spec = pl.BlockSpec((bn, bk), lambda i, j, k: (j, k)) if tb else pl.BlockSpec((bk, bn), lambda i, j, k: (k, j))
    t_spec = pl.BlockSpec((bm, bn), lambda i, j, k: (i, j))
    extra_cols = extra_cols or (0,) * len(extras)
    e_specs = [pl.BlockSpec((bm, bn), lambda i, j, k, off=off: (i, off + j)) for off in extra_cols]
    dims = (((1,), (1 if tb else 0,)), ((), ()))
    n_e, n_o = len(extras), len(out_dtypes)

    def body(*refs):
        a_ref, b_ref = refs[0], refs[1]
        e_refs = refs[2:2 + n_e]
        o_refs = refs[2 + n_e:2 + n_e + n_o]

        def finish(acc):
            outs = (acc,) if epi is None else epi(acc, *[e[...] for e in e_refs])
            for o_ref, o in zip(o_refs, outs):
                o_ref[...] = o.astype(o_ref.dtype)

        if ta:
            at_ref = refs[-1]

            @pl.when(pl.program_id(1) == 0)
            def _():
                at_ref[...] = a_ref[...].T

            lhs = at_ref[...]
        else:
            lhs = a_ref[...].astype(BF16)
        part = lax.dot_general(lhs, b_ref[...].astype(BF16), dims, preferred_element_type=F32)
        if nk == 1:
            finish(part)
        else:
            acc_ref = refs[-1]
            k = pl.program_id(2)

            @pl.when(k == 0)
            def _():
                acc_ref[...] = part

            @pl.when(k > 0)
            def _():
                acc_ref[...] += part

            @pl.when(k == nk - 1)
            def _():
                finish(acc_ref[...])

    if ta:
        assert a.dtype == BF16 and nk == 1
        scratch = [pltpu.VMEM((bm, bk), BF16)]
    else:
        scratch = [pltpu.VMEM((bm, bn), F32)] if nk > 1 else []
    out = _pcall(
        body, plan=plan, name=name, grid=(M // bm, N // bn, nk),
        in_specs=[a_spec, b_spec] + e_specs,
        out_specs=[t_spec] * n_o,
        out_shape=[jax.ShapeDtypeStruct((M, N), dt) for dt in out_dtypes],
        scratch_shapes=scratch,
        compiler_params=_params(("parallel", "arbitrary", "arbitrary")),
    )(a, b, *extras)
    if plan is not None:
        return (out[0][0] if n_o == 1 else out[0]), out[1]
    return out[0] if n_o == 1 else out


def _ada_fwd(c_all, w_ada, b_cols):
    n = w_ada.shape[1]
    bn = 512

    def body(c_ref, w_ref, b_ref, o_ref):
        cv = c_ref[...]
        sc = cv * _sig(cv)
        o_ref[...] = jnp.dot(sc, w_ref[...], precision=HIGHEST, preferred_element_type=F32) + b_ref[...]

    return _pcall(
        body, name="ada_fwd", grid=(n // bn,),
        in_specs=[pl.BlockSpec((N_DEV, D_MODEL), lambda j: (0, 0)), pl.BlockSpec((D_MODEL, bn), lambda j: (0, j)),
                  pl.BlockSpec((1, bn), lambda j: (0, j))],
        out_specs=pl.BlockSpec((N_DEV, bn), lambda j: (0, j)),
        out_shape=jax.ShapeDtypeStruct((N_DEV, n), F32),
        compiler_params=_params(("parallel",)),
    )(c_all, w_ada, b_cols)


ROWS_EW = 256


def _rms_fwd_math(x, gain, scale, shift):
    rstd = lax.rsqrt(jnp.mean(x * x, axis=-1, keepdims=True) + EPS)
    xhat = x * rstd
    n = xhat * gain
    return n * (1.0 + scale) + shift, xhat, n, rstd


def _rms_bwd_math(dh, xhat, n, rstd, gain, scale):
    dn = dh * (1.0 + scale)
    dxhat = dn * gain
    dx = rstd * (dxhat - xhat * jnp.mean(dxhat * xhat, axis=-1, keepdims=True))
    d_scale = jnp.sum(dh * n, axis=0, keepdims=True)
    d_shift = jnp.sum(dh, axis=0, keepdims=True)
    d_gain = jnp.sum(dn * xhat, axis=0, keepdims=True)
    return dx, d_scale, d_shift, d_gain


def _row_spec(w=D_MODEL, br=ROWS_EW):
    return pl.BlockSpec((br, w), lambda i: (i, 0))


def _vec_spec(r=8, w=D_MODEL):
    return pl.BlockSpec((r, w), lambda i: (0, 0))


def _norm1_fwd(x, gain, mod8, plan=None):
    T = x.shape[0]

    def body(x_ref, g_ref, m_ref, h_ref):
        h, _, _, _ = _rms_fwd_math(x_ref[...], g_ref[...], m_ref[1:2, :], m_ref[0:1, :])
        h_ref[...] = h.astype(BF16)

    return _pcall(
        body, plan=plan, name="norm1_fwd", grid=(T // ROWS_EW,),
        in_specs=[_row_spec(), _vec_spec(1), _vec_spec()],
        out_specs=_row_spec(), out_shape=jax.ShapeDtypeStruct((T, D_MODEL), BF16),
        compiler_params=_params(("parallel",)),
    )(x, gain, mod8)


def _res_norm2_fwd(x, mo, gain, mod8):
    T = x.shape[0]

    def body(x_ref, mo_ref, g_ref, m_ref, x1_ref, h_ref):
        x1 = x_ref[...] + m_ref[2:3, :] * mo_ref[...]
        x1_ref[...] = x1
        h, _, _, _ = _rms_fwd_math(x1, g_ref[...], m_ref[4:5, :], m_ref[3:4, :])
        h_ref[...] = h.astype(BF16)

    return _pcall(
        body, name="res_norm2_fwd", grid=(T // ROWS_EW,),
        in_specs=[_row_spec(), _row_spec(), _vec_spec(1), _vec_spec()],
        out_specs=[_row_spec(), _row_spec()],
        out_shape=[jax.ShapeDtypeStruct((T, D_MODEL), F32), jax.ShapeDtypeStruct((T, D_MODEL), BF16)],
        compiler_params=_params(("parallel",)),
    )(x, mo, gain, mod8)


def _loss_bwd(x1, mlp, target, mod8):
    T = x1.shape[0]

    def body(x1_ref, mlp_ref, t_ref, m_ref, dy_ref, dmlp_ref, st_ref):
        i = pl.program_id(0)
        gate = m_ref[5:6, :]
        mlp_v = mlp_ref[...]
        err = x1_ref[...] + gate * mlp_v - t_ref[...]
        dy = err * (1.0 / D_MODEL)
        dy_ref[...] = dy
        dmlp_ref[...] = (dy * gate).astype(BF16)

        @pl.when(i == 0)
        def _():
            st_ref[...] = jnp.zeros_like(st_ref)

        st_ref[0:1, :] += jnp.sum(err * err, axis=0, keepdims=True)
        st_ref[1:2, :] += jnp.sum(dy * mlp_v, axis=0, keepdims=True)

    return _pcall(
        body, name="loss_bwd", grid=(T // ROWS_EW,),
        in_specs=[_row_spec(), _row_spec(), _row_spec(), _vec_spec()],
        out_specs=[_row_spec(), _row_spec(), _vec_spec()],
        out_shape=[jax.ShapeDtypeStruct((T, D_MODEL), F32), jax.ShapeDtypeStruct((T, D_MODEL), BF16),
                   jax.ShapeDtypeStruct((8, D_MODEL), F32)],
        compiler_params=_params(("arbitrary",)),
    )(x1, mlp, target, mod8)


def _norm2_bwd(dh2, x1, dy, mo, gain, mod8):
    T = x1.shape[0]

    def body(dh_ref, x1_ref, dy_ref, mo_ref, g_ref, m_ref, dx1_ref, dmo_ref, st_ref):
        i = pl.program_id(0)
        gain_v, scale = g_ref[...], m_ref[4:5, :]
        _, xhat, n, rstd = _rms_fwd_math(x1_ref[...], gain_v, scale, m_ref[3:4, :])
        dx, d_scale, d_shift, d_gain = _rms_bwd_math(dh_ref[...], xhat, n, rstd, gain_v, scale)
        dx1 = dy_ref[...] + dx
        dx1_ref[...] = dx1
        dmo_ref[...] = (dx1 * m_ref[2:3, :]).astype(BF16)

        @pl.when(i == 0)
        def _():
            st_ref[...] = jnp.zeros_like(st_ref)

        st_ref[0:1, :] += d_scale
        st_ref[1:2, :] += d_shift
        st_ref[2:3, :] += d_gain
        st_ref[3:4, :] += jnp.sum(dx1 * mo_ref[...], axis=0, keepdims=True)

    return _pcall(
        body, name="norm2_bwd", grid=(T // ROWS_EW,),
        in_specs=[_row_spec(), _row_spec(), _row_spec(), _row_spec(), _vec_spec(1), _vec_spec()],
        out_specs=[_row_spec(), _row_spec(), _vec_spec()],
        out_shape=[jax.ShapeDtypeStruct((T, D_MODEL), F32), jax.ShapeDtypeStruct((T, D_MODEL), BF16),
                   jax.ShapeDtypeStruct((8, D_MODEL), F32)],
        compiler_params=_params(("arbitrary",)),
    )(dh2, x1, dy, mo, gain, mod8)


def _norm1_bwd(dh, x, dx1, gain, mod8):
    T = x.shape[0]

    def body(dh_ref, x_ref, dx1_ref, g_ref, m_ref, dx_ref, st_ref):
        i = pl.program_id(0)
        gain_v, scale = g_ref[...], m_ref[1:2, :]
        _, xhat, n, rstd = _rms_fwd_math(x_ref[...], gain_v, scale, m_ref[0:1, :])
        dx, d_scale, d_shift, d_gain = _rms_bwd_math(dh_ref[...], xhat, n, rstd, gain_v, scale)
        dx_ref[...] = dx1_ref[...] + dx

        @pl.when(i == 0)
        def _():
            st_ref[...] = jnp.zeros_like(st_ref)

        st_ref[0:1, :] += d_scale
        st_ref[1:2, :] += d_shift
        st_ref[2:3, :] += d_gain

    return _pcall(
        body, name="norm1_bwd", grid=(T // ROWS_EW,),
        in_specs=[_row_spec(), _row_spec(), _row_spec(), _vec_spec(1), _vec_spec()],
        out_specs=[_row_spec(), _vec_spec()],
        out_shape=[jax.ShapeDtypeStruct((T, D_MODEL), F32), jax.ShapeDtypeStruct((8, D_MODEL), F32)],
        compiler_params=_params(("arbitrary",)),
    )(dh, x, dx1, gain, mod8)


MERGE_BC = 512


def _hgrn_rows(T):
    return 512 if T >= 1024 else 128


def _lower_bound(lbl):
    e = jnp.exp(lbl - jnp.max(lbl, axis=0, keepdims=True))
    return e[0:1, :] / (e[0:1, :] + e[1:2, :])


def _chunk_sum_matrix(rows, backward):
    shift = A_CHUNK.bit_length() - 1
    r = lax.broadcasted_iota(jnp.int32, (rows, rows), 0)
    c = lax.broadcasted_iota(jnp.int32, (rows, rows), 1)
    same = jnp.right_shift(r, shift) == jnp.right_shift(c, shift)
    return (same & ((r <= c) if backward else (r >= c))).astype(BF16)


def _chunk_sums(m, x):
    n = x.shape[1]
    hi = x.astype(BF16)
    rest = x - hi.astype(F32)
    mid = rest.astype(BF16)
    lo = (rest - mid.astype(F32)).astype(BF16)
    y = jnp.dot(m, jnp.concatenate([hi, mid, lo], axis=1), preferred_element_type=F32)
    return y[:, 0:n] + y[:, n:2 * n] + y[:, 2 * n:3 * n]


def _hgrn_block_pre(q, fl, lb, m_fwd):
    sg = _sig(fl)
    f = lb + (1.0 - lb) * sg
    sq = _sig(q)
    return dict(sg=sg, f=f, k=1.0 - f, sq=sq, qf=q * sq, b=_chunk_sums(m_fwd, jnp.log(f)))


def _hgrn_chunk_local(pre, r):
    C = A_CHUNK
    qf, k, b = pre["qf"][r], pre["k"][r], pre["b"][r]
    causal = lax.broadcasted_iota(jnp.int32, (C, C), 0) >= lax.broadcasted_iota(jnp.int32, (C, C), 1)
    bm = b[C // 2 - 1:C // 2, :]
    bl = b[C - 1:C, :]
    e_q, e_k = jnp.exp(b - bm), jnp.exp(bm - b)
    e_b, e_l = jnp.exp(b), jnp.exp(bl - b)
    qd, kd = qf * e_q, k * e_k
    qe, ke = qf * e_b, k * e_l
    att = jnp.where(causal, _nt(qd, kd), 0.0)
    return dict(causal=causal, e_q=e_q, e_k=e_k, e_b=e_b, e_l=e_l, qd=qd, kd=kd, qe=qe, ke=ke, att=att, dec=jnp.exp(bl))


def _hgrn_chunk_fwd(pre, r, v, st):
    c = _hgrn_chunk_local(pre, r)
    c["o"] = _nn(c["att"], v) + _nt(c["qe"], st)
    return c


def _lockstep(gens):
    out = [None] * len(gens)
    live = list(enumerate(gens))
    while live:
        still = []
        for i, g in live:
            try:
                next(g)
                still.append((i, g))
            except StopIteration as done:
                out[i] = done.value
        live = still
    return out


HGRN_HEADS_PER_STEP = 4


def _hgrn_fwd(proj, lb_logits, o_gain, plan=None):
    T = proj.shape[0]
    BR = _hgrn_rows(T)
    cps = BR // A_CHUNK
    K, NH = A_HEAD_DIM, HGRN_HEADS_PER_STEP
    W = NH * K

    def col(off):
        return pl.BlockSpec((BR, W), lambda h, cb: (cb, off // W + h))

    def body(q_ref, f_ref, i_ref, g_ref, lbl_ref, og_ref, o_ref, s_ref, st):
        @pl.when(pl.program_id(1) == 0)
        def _():
            st[...] = jnp.zeros_like(st)

        lb_all = _lower_bound(lbl_ref[...])
        m_fwd = _chunk_sum_matrix(BR, False)
        pre = [_hgrn_block_pre(q_ref[:, n * K:(n + 1) * K], f_ref[:, n * K:(n + 1) * K], lb_all[:, n * K:(n + 1) * K], m_fwd)
               for n in range(NH)]
        def local(n, ci):
            r, hs = slice(ci * A_CHUNK, (ci + 1) * A_CHUNK), slice(n * K, (n + 1) * K)
            v = i_ref[r, hs]
            c = _hgrn_chunk_local(pre[n], r)
            yield
            return dict(o=_nn(c["att"], v), ds=_tn(v, c["ke"]), qe=c["qe"], dec=c["dec"])

        def chain(n, loc):
            hs = slice(n * K, (n + 1) * K)
            state = st[n]
            for ci, p in enumerate(loc):
                r = slice(ci * A_CHUNK, (ci + 1) * A_CHUNK)
                s_ref[n, ci] = state
                o = p["o"] + _nt(p["qe"], state)
                state = state * p["dec"] + p["ds"]
                yield
                on = o * lax.rsqrt(jnp.mean(o * o, axis=-1, keepdims=True) + EPS)
                g = g_ref[r, hs]
                o_ref[r, hs] = (on * og_ref[:, hs] * (g * _sig(g))).astype(BF16)
            st[n] = state

        loc = _lockstep([local(n, ci) for n in range(NH) for ci in range(cps)])
        _lockstep([chain(n, loc[n * cps:(n + 1) * cps]) for n in range(NH)])

    return _pcall(
        body, plan=plan, name="hgrn_fwd", grid=(A_HEADS // NH, T // BR),
        in_specs=[col(OFF_QA), col(OFF_FA), col(OFF_IA), col(OFF_GA),
                  pl.BlockSpec((2, W), lambda h, cb: (0, h)), pl.BlockSpec((1, W), lambda h, cb: (0, h))],
        out_specs=[pl.BlockSpec((BR, W), lambda h, cb: (cb, h)),
                   pl.BlockSpec((NH, cps, K, K), lambda h, cb: (h, cb, 0, 0))],
        out_shape=[jax.ShapeDtypeStruct((T, A_WIDTH), BF16),
                   jax.ShapeDtypeStruct((A_HEADS, T // A_CHUNK, K, K), F32)],
        scratch_shapes=[pltpu.VMEM((NH, K, K), F32)],
        compiler_params=_params(("parallel", "arbitrary")),
    )(proj, proj, proj, proj, lb_logits, o_gain)


def _hgrn_bwd(proj, lb_logits, o_gain, states, do, plan=None):
    T = proj.shape[0]
    BR = _hgrn_rows(T)
    cps = BR // A_CHUNK
    ncb = T // BR
    K, C, NH = A_HEAD_DIM, A_CHUNK, HGRN_HEADS_PER_STEP
    W = NH * K

    def col(off):
        return pl.BlockSpec((BR, W), lambda h, cb: (ncb - 1 - cb, off // W + h))

    def body(q_ref, f_ref, i_ref, g_ref, lbl_ref, og_ref, s_ref, do_ref,
             dq_ref, df_ref, di_ref, dg_ref, dlb_ref, dog_ref, dst):
        @pl.when(pl.program_id(1) == 0)
        def _():
            dst[...] = jnp.zeros_like(dst)
            dlb_ref[...] = jnp.zeros_like(dlb_ref)
            dog_ref[...] = jnp.zeros_like(dog_ref)

        lb_all = _lower_bound(lbl_ref[...])
        row = lax.broadcasted_iota(jnp.int32, (C, K), 0)
        m_fwd, m_bwd = _chunk_sum_matrix(BR, False), _chunk_sum_matrix(BR, True)
        pre = [_hgrn_block_pre(q_ref[:, n * K:(n + 1) * K], f_ref[:, n * K:(n + 1) * K], lb_all[:, n * K:(n + 1) * K], m_fwd)
               for n in range(NH)]
        def local(n, ci):
            r, hs = slice(ci * C, (ci + 1) * C), slice(n * K, (n + 1) * K)
            gain = og_ref[:, hs]
            st = s_ref[n, ci]
            v = i_ref[r, hs]
            q = q_ref[r, hs]
            c = _hgrn_chunk_fwd(pre[n], r, v, st)
            yield
            o = c["o"]
            rn = lax.rsqrt(jnp.mean(o * o, axis=-1, keepdims=True) + EPS)
            on = o * rn
            g = g_ref[r, hs]
            sgg = _sig(g)
            dy = do_ref[r, hs]
            d_ong = dy * (g * sgg)
            dg_ref[r, hs] = (dy * (on * gain) * (sgg * (1.0 + g * (1.0 - sgg)))).astype(BF16)
            d_on = d_ong * gain
            d_o = rn * (d_on - on * jnp.mean(d_on * on, axis=-1, keepdims=True))
            datt = jnp.where(c["causal"], _nt(d_o, v), 0.0)
            dqe = _nn(d_o, st)
            yield
            dqd = _nn(datt, c["kd"])
            dkd = _tn(datt, c["qd"])
            dv = _tn(c["att"], d_o)
            ds = _tn(d_o, c["qe"])
            yield
            t_q, t_k = dqd * c["qd"], dkd * c["kd"]
            sq = pre[n]["sq"][r]
            dq_ref[r, hs] = ((dqd * c["e_q"] + dqe * c["e_b"]) * (sq * (1.0 + q * (1.0 - sq)))).astype(BF16)
            return dict(v=v, st=st, ke=c["ke"], e_l=c["e_l"], dec=c["dec"], dv=dv, ds=ds, dk=dkd * c["e_k"],
                        db=t_q - t_k + dqe * c["qe"], dbm=jnp.sum(t_k - t_q, axis=0, keepdims=True),
                        d_og=jnp.sum(d_ong * on, axis=0, keepdims=True))

        def chain(n, loc):
            hs = slice(n * K, (n + 1) * K)
            dst_next = dst[n]
            db_of, dk_of = [None] * cps, [None] * cps
            for ci in reversed(range(cps)):
                p = loc[ci]
                di_ref[ci * C:(ci + 1) * C, hs] = (p["dv"] + _nt(p["ke"], dst_next)).astype(BF16)
                dke = _nn(p["v"], dst_next)
                yield
                t_l = dke * p["ke"]
                dbl = jnp.sum(t_l, axis=0, keepdims=True) + jnp.sum(dst_next * p["st"], axis=0, keepdims=True) * p["dec"]
                db_of[ci] = p["db"] - t_l + jnp.where(row == C // 2 - 1, p["dbm"], 0.0) + jnp.where(row == C - 1, dbl, 0.0)
                dk_of[ci] = p["dk"] + dke * p["e_l"]
                dst_next = dst_next * p["dec"] + p["ds"]
            dst[n] = dst_next
            return db_of, dk_of

        loc = _lockstep([local(n, ci) for n in range(NH) for ci in range(cps)])
        loc = [loc[n * cps:(n + 1) * cps] for n in range(NH)]
        chains = _lockstep([chain(n, loc[n]) for n in range(NH)])
        for n in range(NH):
            hs = slice(n * K, (n + 1) * K)
            db_of, dk_of = chains[n]
            d_og = loc[n][0]["d_og"]
            for p in loc[n][1:]:
                d_og = d_og + p["d_og"]
            dog_ref[0:1, hs] += d_og
            lb, sg = lb_all[:, hs], pre[n]["sg"]
            dlf = _chunk_sums(m_bwd, jnp.concatenate(db_of, axis=0))
            df = dlf / pre[n]["f"] - jnp.concatenate(dk_of, axis=0)
            df_ref[:, hs] = (df * (1.0 - lb) * sg * (1.0 - sg)).astype(BF16)
            dlb_ref[0:1, hs] += jnp.sum(df * (1.0 - sg), axis=0, keepdims=True)

    ocol = pl.BlockSpec((BR, W), lambda h, cb: (ncb - 1 - cb, h))
    vec = pl.BlockSpec((8, W), lambda h, cb: (0, h))
    return _pcall(
        body, plan=plan, name="hgrn_bwd", grid=(A_HEADS // NH, ncb),
        in_specs=[col(OFF_QA), col(OFF_FA), col(OFF_IA), col(OFF_GA),
                  pl.BlockSpec((2, W), lambda h, cb: (0, h)), pl.BlockSpec((1, W), lambda h, cb: (0, h)),
                  pl.BlockSpec((NH, cps, K, K), lambda h, cb: (h, ncb - 1 - cb, 0, 0)),
                  pl.BlockSpec((BR, W), lambda h, cb: (ncb - 1 - cb, h))],
        out_specs=[ocol, ocol, ocol, ocol, vec, vec],
        out_shape=[jax.ShapeDtypeStruct((T, A_WIDTH), BF16)] * 4 + [jax.ShapeDtypeStruct((8, A_WIDTH), F32)] * 2,
        scratch_shapes=[pltpu.VMEM((NH, K, K), F32)],
        compiler_params=_params(("parallel", "arbitrary")),
    )(proj, proj, proj, proj, lb_logits, o_gain, states, do)


def _head_norm(x):
    r = lax.rsqrt(jnp.mean(x * x, axis=-1, keepdims=True) + EPS)
    return x * r, r


def _head_norm_bwd(dy, xn, r, gain):
    dxn = dy * gain
    return r * (dxn - xn * jnp.mean(dxn * xn, axis=-1, keepdims=True)), jnp.sum(dy * xn, axis=0, keepdims=True)


def _swa_mask(has_prev):
    rows = B_GROUP * BLOCK
    r = lax.broadcasted_iota(jnp.int32, (rows, 2 * BLOCK), 0) % BLOCK
    c = lax.broadcasted_iota(jnp.int32, (rows, 2 * BLOCK), 1)
    rel = r + BLOCK - c
    return (rel >= 0) & (rel < BLOCK) & ((c >= BLOCK) | has_prev)


def _swa_head_fwd(j, q_ref, kp_ref, kc_ref, vp_ref, vc_ref, qg, kg, sk_ref, mask):
    hs = slice(j * B_HEAD_DIM, (j + 1) * B_HEAD_DIM)
    kcat = jnp.concatenate([kp_ref[:, hs], kc_ref[:, hs]], axis=0)
    vcat = jnp.concatenate([vp_ref[:, hs], vc_ref[:, hs]], axis=0)
    qs = jnp.concatenate([q_ref[:, pl.ds((j * B_GROUP + g) * B_HEAD_DIM, B_HEAD_DIM)] for g in range(B_GROUP)], axis=0)
    kn, kr = _head_norm(kcat)
    qn, qr = _head_norm(qs)
    kh, qh = kn * kg, qn * qg
    yield
    s = jnp.where(mask, _nt(qh, kh) * (B_HEAD_DIM ** -0.5), NEG_BIG)
    yield
    sink = jnp.concatenate(
        [jnp.broadcast_to(sk_ref[0:1, pl.ds(j * B_GROUP + g, 1)], (BLOCK, 1)) for g in range(B_GROUP)], axis=0)
    m = jnp.maximum(jnp.max(s, axis=-1, keepdims=True), sink)
    p = jnp.exp(s - m)
    e_sink = jnp.exp(sink - m)
    inv = 1.0 / (jnp.sum(p, axis=-1, keepdims=True) + e_sink)
    prob = p * inv
    return dict(vcat=vcat, kn=kn, kr=kr, qn=qn, qr=qr, kh=kh, qh=qh, prob=prob, p_sink=e_sink * inv)


def _swa_in_specs(nb, last):
    def qi(n):
        return jnp.minimum(n, last)

    q = pl.BlockSpec((BLOCK, B_WIDTH), lambda n: (qi(n), OFF_QB // B_WIDTH))
    kc = pl.BlockSpec((BLOCK, B_KV_WIDTH), lambda n: (qi(n), OFF_KB // B_KV_WIDTH))
    kp = pl.BlockSpec((BLOCK, B_KV_WIDTH), lambda n: (jnp.maximum(qi(n) - 1, 0), OFF_KB // B_KV_WIDTH))
    vc = pl.BlockSpec((BLOCK, B_KV_WIDTH), lambda n: (qi(n), OFF_VB // B_KV_WIDTH))
    vp = pl.BlockSpec((BLOCK, B_KV_WIDTH), lambda n: (jnp.maximum(qi(n) - 1, 0), OFF_VB // B_KV_WIDTH))
    small = [pl.BlockSpec((1, B_HEAD_DIM), lambda n: (0, 0)), pl.BlockSpec((1, B_HEAD_DIM), lambda n: (0, 0)),
             pl.BlockSpec((1, B_GROUP * B_KV_HEADS), lambda n: (0, 0))]
    return [q, kp, kc, vp, vc] + small


def _swa_fwd(proj, q_gain, k_gain, sinks, plan=None):
    T = proj.shape[0]
    nb = T // BLOCK

    def body(q_ref, kp_ref, kc_ref, vp_ref, vc_ref, qg_ref, kg_ref, sk_ref, o_ref):
        mask = _swa_mask(pl.program_id(0) > 0)

        def head(j):
            c = yield from _swa_head_fwd(j, q_ref, kp_ref, kc_ref, vp_ref, vc_ref, qg_ref[...], kg_ref[...], sk_ref, mask)
            yield
            o = _nn(c["prob"], c["vcat"])
            yield
            for g in range(B_GROUP):
                o_ref[:, pl.ds((j * B_GROUP + g) * B_HEAD_DIM, B_HEAD_DIM)] = o[g * BLOCK:(g + 1) * BLOCK].astype(BF16)

        _lockstep([head(j) for j in range(B_KV_HEADS)])

    return _pcall(
        body, plan=plan, name="swa_fwd", grid=(nb,),
        in_specs=_swa_in_specs(nb, nb - 1),
        out_specs=pl.BlockSpec((BLOCK, B_WIDTH), lambda n: (n, 0)),
        out_shape=jax.ShapeDtypeStruct((T, B_WIDTH), BF16),
        compiler_params=_params(("parallel",)),
    )(proj, proj, proj, proj, proj, q_gain, k_gain, sinks)


def _swa_bwd(proj, q_gain, k_gain, sinks, do, plan=None):
    T = proj.shape[0]
    nb = T // BLOCK
    scale = B_HEAD_DIM ** -0.5

    def body(q_ref, kp_ref, kc_ref, vp_ref, vc_ref, qg_ref, kg_ref, sk_ref, do_ref,
             dq_ref, dkv_ref, sm_ref, ck, cv):
        n = pl.program_id(0)

        @pl.when(n == 0)
        def _():
            ck[...] = jnp.zeros_like(ck)
            cv[...] = jnp.zeros_like(cv)
            sm_ref[...] = jnp.zeros_like(sm_ref)

        @pl.when(n < nb)
        def _():
            mask = _swa_mask(n > 0)
            qg, kg = qg_ref[...], kg_ref[...]
            lane = lax.broadcasted_iota(jnp.int32, (1, BLOCK), 1)
            def head(j):
                hs = slice(j * B_HEAD_DIM, (j + 1) * B_HEAD_DIM)
                vs = slice(B_KV_WIDTH + j * B_HEAD_DIM, B_KV_WIDTH + (j + 1) * B_HEAD_DIM)
                c = yield from _swa_head_fwd(j, q_ref, kp_ref, kc_ref, vp_ref, vc_ref, qg, kg, sk_ref, mask)
                d_out = jnp.concatenate(
                    [do_ref[:, pl.ds((j * B_GROUP + g) * B_HEAD_DIM, B_HEAD_DIM)] for g in range(B_GROUP)], axis=0)
                prob = c["prob"]
                yield
                out = _nn(prob, c["vcat"])
                d_prob = _nt(d_out, c["vcat"])
                dv = _tn(prob, d_out)
                yield
                delta = jnp.sum(d_out * out, axis=-1, keepdims=True)
                ds = prob * (d_prob - delta)
                d_sink = -c["p_sink"] * delta
                yield
                dqh = _nn(ds, c["kh"]) * scale
                dkh = _tn(ds, c["qh"]) * scale
                yield
                dq, dqg = _head_norm_bwd(dqh, c["qn"], c["qr"], qg)
                dk, dkg = _head_norm_bwd(dkh, c["kn"], c["kr"], kg)
                d_sinks = jnp.zeros((1, BLOCK), F32)
                for g in range(B_GROUP):
                    dq_ref[:, pl.ds((j * B_GROUP + g) * B_HEAD_DIM, B_HEAD_DIM)] = dq[g * BLOCK:(g + 1) * BLOCK].astype(BF16)
                    tot = jnp.sum(d_sink[g * BLOCK:(g + 1) * BLOCK], axis=0, keepdims=True)
                    d_sinks = d_sinks + jnp.where(lane == j * B_GROUP + g, tot, 0.0)
                dkv_ref[:, hs] = (ck[:, hs] + dk[0:BLOCK]).astype(BF16)
                dkv_ref[:, vs] = (cv[:, hs] + dv[0:BLOCK]).astype(BF16)
                ck[:, hs] = dk[BLOCK:2 * BLOCK]
                cv[:, hs] = dv[BLOCK:2 * BLOCK]
                return dqg, dkg, d_sinks

            small = _lockstep([head(j) for j in range(B_KV_HEADS)])
            sm_ref[0:1, 0:B_HEAD_DIM] += small[0][0] + small[1][0] + small[2][0] + small[3][0]
            sm_ref[1:2, 0:B_HEAD_DIM] += small[0][1] + small[1][1] + small[2][1] + small[3][1]
            sm_ref[2:3, :] += small[0][2] + small[1][2] + small[2][2] + small[3][2]

        @pl.when(n == nb)
        def _():
            dkv_ref[:, 0:B_KV_WIDTH] = ck[...].astype(BF16)
            dkv_ref[:, B_KV_WIDTH:2 * B_KV_WIDTH] = cv[...].astype(BF16)

    return _pcall(
        body, plan=plan, name="swa_bwd", grid=(nb + 1,),
        in_specs=_swa_in_specs(nb, nb - 1) + [pl.BlockSpec((BLOCK, B_WIDTH), lambda n: (jnp.minimum(n, nb - 1), 0))],
        out_specs=[pl.BlockSpec((BLOCK, B_WIDTH), lambda n: (jnp.minimum(n, nb - 1), 0)),
                   pl.BlockSpec((BLOCK, 2 * B_KV_WIDTH), lambda n: (jnp.maximum(n - 1, 0), 0)),
                   pl.BlockSpec((8, BLOCK), lambda n: (0, 0))],
        out_shape=[jax.ShapeDtypeStruct((T, B_WIDTH), BF16), jax.ShapeDtypeStruct((T, 2 * B_KV_WIDTH), BF16),
                   jax.ShapeDtypeStruct((8, BLOCK), F32)],
        scratch_shapes=[pltpu.VMEM((BLOCK, B_KV_WIDTH), F32), pltpu.VMEM((BLOCK, B_KV_WIDTH), F32)],
        compiler_params=_params(("arbitrary",)),
    )(proj, proj, proj, proj, proj, q_gain, k_gain, sinks, do)


W_IN, W_A, W_B, W_OUT, W_MI, W_MO = range(6)


def _local_step(x, target, mod8, norm1_gain, norm2_gain, lb_logits, o_gain, q_gain, k_gain, sinks, parts, c_arr, chip_arr):
    relu2 = lambda u: (u, jnp.square(jnp.maximum(u, 0.0)))
    pair, half = {}, {}

    def exchange(ws, grads):
        return _sibling_exchange_plan([_grad_view(g, w) for w, g in zip(ws, grads)])

    def pair_sums(ws, grads, others):
        for w, g, o in zip(ws, grads, others):
            pair[w] = _pair_sum(_grad_view(g, w), o, c_arr, f"pair_sum{w}")

    def sum_slots(ws, slots):
        for w, s in zip(ws, slots):
            half[w] = _sum_slots(pair[w], s, w, chip_arr, f"sum_slots{w}")

    h, (w_in,) = _norm1_fwd(x, norm1_gain, mod8, plan=_gather_plan({W_IN: parts[W_IN]}, pass_at=0.9))
    proj, (w_mi,) = _mm(h, w_in, name="mm_proj", bn=512, plan=_gather_plan({W_MI: parts[W_MI]}, pass_at=0.9))
    (o_a, states), (w_a, w_b) = _hgrn_fwd(
        proj, lb_logits, o_gain, plan=_gather_plan({w: parts[w] for w in (W_A, W_B)}, pass_at=0.7))
    o_b, (w_out,) = _swa_fwd(proj, q_gain, k_gain, sinks, plan=_gather_plan({W_OUT: parts[W_OUT]}, pass_at=0.6))
    ya = _mm(o_a, w_a, name="mm_branch_a")
    gate_cols = (OFF_GATE_A // MERGE_BC, OFF_GATE_B // MERGE_BC)
    yb, merged = _mm(o_b, w_b, name="mm_branch_b", bn=MERGE_BC, out_dtypes=(F32, BF16),
                     extras=(proj, proj, ya), extra_cols=gate_cols + (0,),
                     epi=lambda acc, ga, gb, ya_: (acc, _sig(ga) * ya_ + _sig(gb) * acc))
    mo = _mm(merged, w_out, name="mm_out")
    x1, h2 = _res_norm2_fwd(x, mo, norm2_gain, mod8)
    (u, act), (w_mo,) = _mm(h2, w_mi, name="mm_mlp_in", out_dtypes=(F32, BF16), epi=relu2,
                            plan=_gather_plan({W_MO: parts[W_MO]}, pass_at=0.97))
    mlp = _mm(act, w_mo, name="mm_mlp_out")
    dy, dmlp, st_loss = _loss_bwd(x1, mlp, target, mod8)
    g_mo = _mm(act, dmlp, name="mm_g_mlp_out", ta=True, bn=512)
    du, others = _mm(dmlp, w_mo, name="mm_d_act", tb=True, out_dtypes=(BF16,), extras=(u,),
                     epi=lambda acc, uu: (acc * (2.0 * jnp.maximum(uu, 0.0)),), plan=exchange([W_MO], [g_mo]))
    pair_sums([W_MO], [g_mo], others)
    g_mi, slots_mo = _mm(h2, du, name="mm_g_mlp_in", ta=True, bn=512, plan=_chip_exchange_plan({W_MO: pair[W_MO]}))
    dh2, others = _mm(du, w_mi, name="mm_d_h2", tb=True, plan=exchange([W_MI], [g_mi]))
    pair_sums([W_MI], [g_mi], others)
    sum_slots([W_MO], slots_mo)
    dx1, dmo, st_n2 = _norm2_bwd(dh2, x1, dy, mo, norm2_gain, mod8)
    def merge_bwd(dm, ga, gb, ya_, yb_):
        sa, sb = _sig(ga), _sig(gb)
        return dm * sa, dm * sb, dm * ya_ * sa * (1.0 - sa), dm * yb_ * sb * (1.0 - sb)

    dya, dyb, dga, dgb = _mm(dmo, w_out, name="mm_d_merged", tb=True, bn=MERGE_BC, out_dtypes=(BF16,) * 4,
                             extras=(proj, proj, ya, yb), extra_cols=gate_cols + (0, 0), epi=merge_bwd)
    g_out = _mm(merged, dmo, name="mm_g_out", ta=True, bn=512)
    do_a = _mm(dya, w_a, name="mm_d_oa", tb=True)
    g_a = _mm(o_a, dya, name="mm_g_branch_a", ta=True, bn=512)
    do_b = _mm(dyb, w_b, name="mm_d_ob", tb=True)
    g_b = _mm(o_b, dyb, name="mm_g_branch_b", ta=True, bn=512)
    mid = [W_A, W_B, W_OUT]
    (dqb, dkvb, st_swa), res = _swa_bwd(
        proj, q_gain, k_gain, sinks, do_b,
        plan=_join(_chip_exchange_plan({W_MI: pair[W_MI]}), exchange(mid, [g_a, g_b, g_out])))
    sum_slots([W_MI], res[:1])
    pair_sums(mid, [g_a, g_b, g_out], res[1:])
    (dqa, dfa, dia, dgga, d_lb, d_og), slots_mid = _hgrn_bwd(proj, lb_logits, o_gain, states, do_a,
                                                             plan=_chip_exchange_plan({w: pair[w] for w in mid}))
    sum_slots(mid, slots_mid)
    dproj = jnp.concatenate([dqa, dfa, dia, dgga, dqb, dkvb, dga, dgb], axis=1)
    hr = D_MODEL // 2
    h_send = lax.dynamic_slice(h, (0, (1 - c_arr[0]) * hr), (h.shape[0], hr))
    h_own = lax.dynamic_slice(h, (0, c_arr[0] * hr), (h.shape[0], hr))
    done = [W_A, W_B, W_OUT, W_MI, W_MO]
    g_send, res = _mm(h_send, dproj, name="mm_g_in_send", ta=True, bn=512,
                      plan=_sibling_share_plan([half[w] for w in done]))
    theirs = dict(zip(done, res))
    g_own, (g_other,) = _mm(h_own, dproj, name="mm_g_in_own", ta=True, bn=512, plan=_sibling_share_plan([g_send]))
    pair[W_IN] = _add_bf16(g_own, g_other, "pair_sum0")[None]
    dh, slots_in = _mm(dproj, w_in, name="mm_d_h", tb=True, bk=2432, plan=_chip_exchange_plan({W_IN: pair[W_IN]}))
    sum_slots([W_IN], slots_in)
    grad_x, st_n1 = _norm1_bwd(dh, x, dx1, norm1_gain, mod8)
    (theirs[W_IN],) = _run_plan(_sibling_share_plan([half[W_IN]]), "sibling_share_w_in")
    stats = dict(loss=st_loss, n2=st_n2, n1=st_n1, d_lb=d_lb, d_og=d_og, swa=st_swa)
    return grad_x, [half[w] for w in range(N_W)], [theirs[w] for w in range(N_W)], stats


def _ew_rows(rows, cols):
    br = 8
    while br * 2 <= rows and br * 2 * cols * 4 <= (1 << 20) and rows % (br * 2) == 0:
        br *= 2
    return br


def _cast_into_full(shard, w, chip_arr, name):
    sr, sc = shard.shape
    R, C, by_col = W_SHAPES[w]
    br = _ew_rows(sr, sc)
    nb = sr // br
    out_map = (lambda i, chip: (i, chip[0])) if by_col else (lambda i, chip: (chip[0] * nb + i, 0))

    def body(chip_ref, w_ref, o_ref):
        o_ref[...] = w_ref[...].astype(BF16)

    return _pcall(
        body, name=name,
        grid_spec=pltpu.PrefetchScalarGridSpec(
            num_scalar_prefetch=1, grid=(nb,),
            in_specs=[pl.BlockSpec((br, sc), lambda i, chip: (i, 0))],
            out_specs=pl.BlockSpec((br, sc), out_map)),
        out_shape=jax.ShapeDtypeStruct((R, C), BF16), compiler_params=_params(("parallel",)))(chip_arr, shard)


def _adamw_math(w, g, m, v):
    m = ADAM_B1 * m + (1.0 - ADAM_B1) * g
    v = ADAM_B2 * v + (1.0 - ADAM_B2) * (g * g)
    m_hat = m / (1.0 - ADAM_B1 ** ADAM_STEP)
    v_hat = v / (1.0 - ADAM_B2 ** ADAM_STEP)
    delta = -ADAM_LR * (m_hat / (jnp.sqrt(v_hat) + ADAM_EPS) + ADAM_WD * w)
    return delta, m, v


def _adamw(w, g, m, v, name):
    R, C = w.shape
    br = _ew_rows(R, C)
    spec = pl.BlockSpec((br, C), lambda i: (i, 0))

    def body(w_ref, g_ref, m_ref, v_ref, d_ref, nm_ref, nv_ref):
        d_ref[...], nm_ref[...], nv_ref[...] = _adamw_math(w_ref[...], g_ref[...], m_ref[...], v_ref[...])

    sh = jax.ShapeDtypeStruct((R, C), F32)
    return _pcall(body, name=name, grid=(R // br,), in_specs=[spec] * 4, out_specs=[spec] * 3, out_shape=[sh] * 3,
                  compiler_params=_params(("parallel",)))(w, g, m, v)


def _add_bf16(a, b, name):
    R, C = a.shape
    br = _ew_rows(R, C)
    spec = pl.BlockSpec((br, C), lambda i: (i, 0))

    def body(a_ref, b_ref, o_ref):
        o_ref[...] = (a_ref[...] + b_ref[...]).astype(BF16)

    return _pcall(body, name=name, grid=(R // br,), in_specs=[spec, spec], out_specs=spec,
                  out_shape=jax.ShapeDtypeStruct((R, C), BF16), compiler_params=_params(("parallel",)))(a, b)


def _adamw_halves(w, own, other, m, v, c_arr, name):
    R, C = w.shape
    hr = R // 2
    br = _ew_rows(hr, C)
    nb = hr // br
    full = pl.BlockSpec((br, C), lambda h, i, c_ref: (h * nb + i, 0))
    half = pl.BlockSpec((br, C), lambda h, i, c_ref: (i, 0))

    def body(c_ref, w_ref, own_ref, oth_ref, m_ref, v_ref, g_ref, d_ref, nm_ref, nv_ref):
        g = jnp.where(pl.program_id(0) == c_ref[0], own_ref[...], oth_ref[...])
        g_ref[...] = g
        d_ref[...], nm_ref[...], nv_ref[...] = _adamw_math(w_ref[...], g, m_ref[...], v_ref[...])

    sh = jax.ShapeDtypeStruct((R, C), F32)
    return _pcall(
        body, name=name,
        grid_spec=pltpu.PrefetchScalarGridSpec(
            num_scalar_prefetch=1, grid=(2, nb), in_specs=[full, half, half, full, full], out_specs=[full] * 4),
        out_shape=[sh] * 4, compiler_params=_params(("parallel", "parallel")))(c_arr, w, own, other, m, v)


def _ada_grad_adamw(c_t, dmod, w, m, v):
    R, C = w.shape
    br = _ew_rows(R, C)
    spec = pl.BlockSpec((br, C), lambda i: (i, 0))

    def body(c_ref, dm_ref, w_ref, m_ref, v_ref, g_ref, d_ref, nm_ref, nv_ref):
        cv = c_ref[...]
        sc = cv * _sig(cv)
        g = sc[:, 0:1] * dm_ref[0:1, :]
        for b in range(1, N_DEV):
            g = g + sc[:, b:b + 1] * dm_ref[b:b + 1, :]
        g_ref[...] = g
        d_ref[...], nm_ref[...], nv_ref[...] = _adamw_math(w_ref[...], g, m_ref[...], v_ref[...])

    sh = jax.ShapeDtypeStruct((R, C), F32)
    return _pcall(
        body, name="ada_grad_adamw", grid=(R // br,),
        in_specs=[pl.BlockSpec((br, N_DEV), lambda i: (i, 0)), pl.BlockSpec((N_DEV, C), lambda i: (0, 0)), spec, spec, spec],
        out_specs=[spec] * 4, out_shape=[sh] * 4, compiler_params=_params(("parallel",)))(c_t, dmod, w, m, v)


SMALL_ROWS = 16


def _small_sum(small_all, lb_logits):
    def body(s_ref, lbl_ref, o_ref):
        acc = s_ref[0:SMALL_ROWS, :]
        for d in range(1, N_DEV):
            acc = acc + s_ref[d * SMALL_ROWS:(d + 1) * SMALL_ROWS, :]
        o_ref[...] = acc
        z = lbl_ref[...]
        e = jnp.exp(z - jnp.max(z, axis=0, keepdims=True))
        p0 = e[0:1, :] / (e[0:1, :] + e[1:2, :])
        dz = acc[8:9, 0:A_WIDTH] * p0 * (1.0 - p0)
        o_ref[8:9, 0:A_WIDTH] = dz
        o_ref[10:11, 0:A_WIDTH] = -dz

    return _pcall(body, name="small_sum", out_shape=jax.ShapeDtypeStruct((SMALL_ROWS, D_MODEL), F32),
                  in_specs=[pl.BlockSpec(memory_space=pltpu.VMEM)] * 2, out_specs=pl.BlockSpec(memory_space=pltpu.VMEM),
                  compiler_params=_params())(small_all, lb_logits)


RELATIONS = ((1, 0), (0, 1), (1, 1))
ANY = pl.BlockSpec(memory_space=pl.ANY)


def _place():
    x, y, c = lax.axis_index("x"), lax.axis_index("y"), lax.axis_index("c")
    return x, y, c


def _allgather_small(x_shard, name):
    m_per, n = x_shard.shape

    def body(x_ref, out_ref, send_sems, recv_sems, local_sem):
        x, y, c = _place()
        me, sibling = (x, y, c), (x, y, 1 - c)
        chips = [(1 - x, y), (x, 1 - y), (1 - x, 1 - y)]

        def rows(px, py, pc):
            return out_ref.at[pl.ds((4 * px + 2 * py + pc) * m_per, m_per), :]

        def copy(k, block, to, src=None):
            return pltpu.make_async_remote_copy(
                src_ref=rows(*block) if src is None else src, dst_ref=rows(*block),
                send_sem=send_sems.at[k], recv_sem=recv_sems.at[k], device_id=to, device_id_type=MESH)

        mine = pltpu.make_async_copy(x_ref, rows(*me), local_sem)
        mine.start()
        first = [copy(0, me, sibling, src=x_ref)]
        first += [copy(1 + j, me, (*chip, c), src=x_ref) for j, chip in enumerate(chips)]
        for cp in first:
            cp.start()
        passed = [copy(4 + j, (*chip, c), sibling) for j, chip in enumerate(chips)]
        for j, chip in enumerate(chips):
            copy(1 + j, (*chip, c), me).wait_recv()
            passed[j].start()
        copy(0, sibling, me).wait_recv()
        for j, chip in enumerate(chips):
            copy(4 + j, (*chip, 1 - c), me).wait_recv()
        for cp in first + passed:
            cp.wait_send()
        mine.wait()

    return _pcall(
        body, name=name, out_shape=jax.ShapeDtypeStruct((N_DEV * m_per, n), x_shard.dtype),
        in_specs=[pl.BlockSpec(memory_space=pltpu.VMEM)], out_specs=pl.BlockSpec(memory_space=pltpu.VMEM),
        scratch_shapes=[pltpu.SemaphoreType.DMA((7,)), pltpu.SemaphoreType.DMA((7,)), pltpu.SemaphoreType.DMA],
        compiler_params=_params(),
    )(x_shard)


W_SHAPES = ((D_MODEL, IN_WIDTH, True), (A_WIDTH, D_MODEL, True), (B_WIDTH, D_MODEL, True),
            (D_MODEL, D_MODEL, False), (D_MODEL, MLP_HIDDEN, True), (MLP_HIDDEN, D_MODEL, False))
N_W = len(W_SHAPES)


def _shard_shape(w):
    R, C, by_col = W_SHAPES[w]
    return (R, C // N_CHIPS) if by_col else (R // N_CHIPS, C)


def _half_shape(w):
    sr, sc = _shard_shape(w)
    return sr // 2, sc


def _region(full_ref, w, chip, half):
    sr, sc = _shard_shape(w)
    by_col = W_SHAPES[w][2]
    r0, c0 = (0, chip * sc) if by_col else (chip * sr, 0)
    if half is None:
        return full_ref.at[pl.ds(r0, sr), pl.ds(c0, sc)]
    return full_ref.at[pl.ds(r0 + half * (sr // 2), sr // 2), pl.ds(c0, sc)]


def _on_device(fn):
    x, y, c = _place()
    me = 4 * x + 2 * y + c
    for d in range(N_DEV):
        @pl.when(me == d)
        def _(d=d):
            fn(x, y, c, d)


def _gather_plan(partials, pass_at=0.5):
    ws = sorted(partials)
    pairs = [(i, w, k) for i, w in enumerate(ws) for k in range(3)]

    def first(pi, po, ps, x, y, c, d, i, w, k):
        chip, dc = d >> 1, d & 1
        rx, ry = RELATIONS[k]
        return pltpu.make_async_remote_copy(
            src_ref=_region(pi[i], w, chip, dc), dst_ref=_region(po[i], w, chip, dc),
            send_sem=ps[0].at[i * 3 + k], recv_sem=ps[1].at[i * 3 + k],
            device_id=(x ^ rx, y ^ ry, c), device_id_type=MESH)

    def landed(po, ps, x, y, c, d, i, w, k, half, to_sibling):
        rx, ry = RELATIONS[k]
        got = _region(po[i], w, (d >> 1) ^ (2 * rx + ry), half)
        s = 2 if to_sibling else 0
        return pltpu.make_async_remote_copy(
            src_ref=got, dst_ref=got, send_sem=ps[s].at[i * 3 + k], recv_sem=ps[s + 1].at[i * 3 + k],
            device_id=(x, y, 1 - c), device_id_type=MESH)

    def send(pi, po, ps):
        def run(x, y, c, d):
            for i, w, k in pairs:
                first(pi, po, ps, x, y, c, d, i, w, k).start()
        _on_device(run)

    def pass_on(pi, po, ps):
        def run(x, y, c, d):
            for i, w, k in pairs:
                landed(po, ps, x, y, c, d, i, w, k, d & 1, False).wait_recv()
                landed(po, ps, x, y, c, d, i, w, k, d & 1, True).start()
        _on_device(run)

    def finish(pi, po, ps):
        def run(x, y, c, d):
            for i, w, k in pairs:
                landed(po, ps, x, y, c, d, i, w, k, 1 - (d & 1), True).wait_recv()
            for i, w, k in pairs:
                first(pi, po, ps, x, y, c, d, i, w, k).wait_send()
                landed(po, ps, x, y, c, d, i, w, k, d & 1, True).wait_send()
        _on_device(run)

    return _Plan([partials[w] for w in ws], [jax.ShapeDtypeStruct(W_SHAPES[w][:2], BF16) for w in ws],
                 [pltpu.SemaphoreType.DMA((3 * len(ws),)) for _ in range(4)], [send, pass_on, finish],
                 {i: i for i in range(len(ws))}, mid_at=(pass_at,))


def _grad_view(g, w):
    R, C, by_col = W_SHAPES[w]
    return g.reshape(1, 2, R // 2, C) if by_col else g.reshape(N_CHIPS, 2, R // N_CHIPS // 2, C)


def _start_wait_plan(ins, outs, n_copies, copies):
    def start(pi, po, ps):
        for cp in copies(pi, po, ps):
            cp.start()

    def finish(pi, po, ps):
        for cp in copies(pi, po, ps):
            cp.wait()

    return _Plan(ins, outs, [pltpu.SemaphoreType.DMA((n_copies,)), pltpu.SemaphoreType.DMA((n_copies,))], [start, finish])


def _sibling_exchange_plan(g4s):
    pieces = [(i, p) for i, g in enumerate(g4s) for p in range(g.shape[0])]

    def copies(pi, po, ps):
        x, y, c = _place()
        return [pltpu.make_async_remote_copy(
            src_ref=pi[i].at[p, 1 - c], dst_ref=po[i].at[p], send_sem=ps[0].at[n], recv_sem=ps[1].at[n],
            device_id=(x, y, 1 - c), device_id_type=MESH) for n, (i, p) in enumerate(pieces)]

    return _start_wait_plan(list(g4s), [jax.ShapeDtypeStruct((g.shape[0],) + g.shape[2:], F32) for g in g4s],
                            len(pieces), copies)


def _pair_sum(g4, other, c_arr, name):
    P, _, hr, C = g4.shape
    br = _ew_rows(hr, C)

    def body(c_ref, g_ref, o_ref, p_ref):
        p_ref[...] = (g_ref[...] + o_ref[...]).astype(BF16)

    return _pcall(
        body, name=name,
        grid_spec=pltpu.PrefetchScalarGridSpec(
            num_scalar_prefetch=1, grid=(P, hr // br),
            in_specs=[pl.BlockSpec((None, None, br, C), lambda p, i, c_ref: (p, c_ref[0], i, 0)),
                      pl.BlockSpec((None, br, C), lambda p, i, c_ref: (p, i, 0))],
            out_specs=pl.BlockSpec((None, br, C), lambda p, i, c_ref: (p, i, 0))),
        out_shape=jax.ShapeDtypeStruct((P, hr, C), BF16),
        compiler_params=_params(("parallel", "parallel")),
    )(c_arr, g4, other)


def _pair_part(p_ref, w, chip):
    sr, sc = _shard_shape(w)
    return p_ref.at[0, :, pl.ds(chip * sc, sc)] if W_SHAPES[w][2] else p_ref.at[chip]


def _chip_exchange_plan(pairs):
    ws = sorted(pairs)

    def stage(wait):
        def run(pi, po, ps):
            def on(x, y, c, d):
                for i, w in enumerate(ws):
                    for k, (rx, ry) in enumerate(RELATIONS):
                        cp = pltpu.make_async_remote_copy(
                            src_ref=_pair_part(pi[i], w, (d >> 1) ^ (2 * rx + ry)), dst_ref=po[i].at[k],
                            send_sem=ps[0].at[i * 3 + k], recv_sem=ps[1].at[i * 3 + k],
                            device_id=(x ^ rx, y ^ ry, c), device_id_type=MESH)
                        if wait:
                            cp.wait()
                        else:
                            cp.start()
            _on_device(on)
        return run

    return _Plan([pairs[w] for w in ws], [jax.ShapeDtypeStruct((3,) + _half_shape(w), BF16) for w in ws],
                 [pltpu.SemaphoreType.DMA((3 * len(ws),)), pltpu.SemaphoreType.DMA((3 * len(ws),))],
                 [stage(False), stage(True)])


def _sum_slots(pair, slots, w, chip_arr, name):
    _, hr, C = slots.shape
    br = _ew_rows(hr, C)
    own_map = (lambda i, chip: (0, i, chip[0])) if W_SHAPES[w][2] else (lambda i, chip: (chip[0], i, 0))

    def body(chip_ref, p_ref, s_ref, o_ref):
        acc = p_ref[...].astype(F32)
        for k in range(3):
            acc = acc + s_ref[k].astype(F32)
        o_ref[...] = acc

    return _pcall(
        body, name=name,
        grid_spec=pltpu.PrefetchScalarGridSpec(
            num_scalar_prefetch=1, grid=(hr // br,),
            in_specs=[pl.BlockSpec((None, br, C), own_map), pl.BlockSpec((3, br, C), lambda i, chip: (0, i, 0))],
            out_specs=pl.BlockSpec((br, C), lambda i, chip: (i, 0))),
        out_shape=jax.ShapeDtypeStruct((hr, C), F32), compiler_params=_params(("parallel",)),
    )(chip_arr, pair, slots)


def _sibling_share_plan(halves):
    def copies(pi, po, ps):
        x, y, c = _place()
        return [pltpu.make_async_remote_copy(
            src_ref=pi[i], dst_ref=po[i], send_sem=ps[0].at[i], recv_sem=ps[1].at[i],
            device_id=(x, y, 1 - c), device_id_type=MESH) for i in range(len(halves))]

    return _start_wait_plan(list(halves), [jax.ShapeDtypeStruct(h.shape, F32) for h in halves], len(halves), copies)


def _pad_lanes(v, width=D_MODEL):
    return jnp.pad(v, ((0, 0), (0, width - v.shape[1])))


def _pack_small(b_ada, norm1, norm2, lb, o_gain, q_gain, k_gain, sinks):
    rows = [b_ada.reshape(N_MOD, D_MODEL), norm1, norm2, jnp.concatenate([lb[0:1], o_gain], axis=1),
            _pad_lanes(jnp.concatenate([q_gain, k_gain, sinks], axis=1)), _pad_lanes(lb[1:2]),
            jnp.zeros((SMALL_ROWS - 11, D_MODEL), F32)]
    return jnp.concatenate(rows, axis=0)


def _unpack_small(p):
    return (p[0:6].reshape(1, N_MOD * D_MODEL), p[6:7], p[7:8],
            jnp.concatenate([p[8:9, 0:A_WIDTH], p[10:11, 0:A_WIDTH]], axis=0), p[8:9, A_WIDTH:],
            p[9:10, 0:64], p[9:10, 64:128], p[9:10, 128:144])


def kernel(x, c, w_ada, b_ada, norm1_gain, w_in, lb_logits, hgrn_o_gain, q_norm_gain, k_norm_gain, sinks, w_branch_a, w_branch_b, w_out, norm2_gain, w_mlp_in, w_mlp_out, loss_target, m_w_ada, m_b_ada, m_norm1_gain, m_w_in, m_lb_logits, m_hgrn_o_gain, m_q_norm_gain, m_k_norm_gain, m_sinks, m_w_branch_a, m_w_branch_b, m_w_out, m_norm2_gain, m_w_mlp_in, m_w_mlp_out, v_w_ada, v_b_ada, v_norm1_gain, v_w_in, v_lb_logits, v_hgrn_o_gain, v_q_norm_gain, v_k_norm_gain, v_sinks, v_w_branch_a, v_w_branch_b, v_w_out, v_norm2_gain, v_w_mlp_in, v_w_mlp_out):
    xi, yi, ci = _place()
    chip = 2 * xi + yi
    me = 4 * xi + 2 * yi + ci
    ada_cols = w_ada.shape[2]

    c_all = _allgather_small(jnp.broadcast_to(c, (8, D_MODEL)), "gather_c").reshape(N_DEV, 8, D_MODEL)[:, 0]
    b_cols = lax.dynamic_slice(b_ada, (0, chip * ada_cols), (1, ada_cols))
    mod_part = _ada_fwd(c_all, w_ada[0], b_cols)
    mod_all = _allgather_small(mod_part, "gather_mod").reshape(N_CHIPS, 2, N_DEV, ada_cols)[:, 0]
    mod_mine = lax.dynamic_index_in_dim(mod_all, me, axis=1, keepdims=False).reshape(N_MOD, D_MODEL)
    mod8 = jnp.concatenate([mod_mine, jnp.zeros((2, D_MODEL), F32)], axis=0)

    shards = (w_in[0], w_branch_a[0], w_branch_b[0], w_out[0], w_mlp_in[0], w_mlp_out[0])
    chip_arr = chip.astype(jnp.int32).reshape(1)
    c_arr = ci.astype(jnp.int32).reshape(1)
    parts = [_cast_into_full(s, w, chip_arr, f"cast_w{w}") for w, s in enumerate(shards)]

    grad_x, halves, theirs, st = _local_step(x[0], loss_target[0], mod8, norm1_gain, norm2_gain, lb_logits, hgrn_o_gain,
                                             q_norm_gain, k_norm_gain, sinks, parts, c_arr, chip_arr)
    loss = lax.psum(0.5 * jnp.sum(st["loss"][0]) / D_MODEL, ("x", "y", "c"))
    moments = ((m_w_in, v_w_in), (m_w_branch_a, v_w_branch_a), (m_w_branch_b, v_w_branch_b), (m_w_out, v_w_out),
               (m_w_mlp_in, v_w_mlp_in), (m_w_mlp_out, v_w_mlp_out))
    big = [_adamw_halves(shards[w], halves[w], theirs[w], moments[w][0][0], moments[w][1][0], c_arr, f"adamw{w}")
           for w in range(N_W)]

    swa = st["swa"]
    small = jnp.concatenate([
        st["n1"][1:2], st["n1"][0:1], st["n2"][3:4], st["n2"][1:2], st["n2"][0:1], st["loss"][1:2],
        st["n1"][2:3], st["n2"][2:3], jnp.concatenate([st["d_lb"][0:1], st["d_og"][0:1]], axis=1),
        _pad_lanes(jnp.concatenate([swa[0:1, 0:64], swa[1:2, 0:64], swa[2:3, 0:16]], axis=1)),
        jnp.zeros((SMALL_ROWS - 10, D_MODEL), F32)], axis=0)
    small_all = _allgather_small(small, "gather_small")
    g_small = _small_sum(small_all, lb_logits)
    small_w = (b_ada, norm1_gain, norm2_gain, lb_logits, hgrn_o_gain, q_norm_gain, k_norm_gain, sinks)
    small_m = (m_b_ada, m_norm1_gain, m_norm2_gain, m_lb_logits, m_hgrn_o_gain, m_q_norm_gain, m_k_norm_gain, m_sinks)
    small_v = (v_b_ada, v_norm1_gain, v_norm2_gain, v_lb_logits, v_hgrn_o_gain, v_q_norm_gain, v_k_norm_gain, v_sinks)
    sm = [_unpack_small(t) for t in
          (g_small,) + tuple(_adamw(_pack_small(*small_w), g_small, _pack_small(*small_m), _pack_small(*small_v),
                                    "adamw_small"))]
    g_b, g_n1, g_n2, g_lb, g_og, g_qg, g_kg, g_sk = ([t[i] for t in sm] for i in range(8))

    dmod_all = small_all.reshape(N_DEV, SMALL_ROWS, D_MODEL)[:, 0:N_MOD].reshape(N_DEV, N_MOD * D_MODEL)
    dmod_cols = lax.dynamic_slice(dmod_all, (0, chip * ada_cols), (N_DEV, ada_cols))
    ada = _ada_grad_adamw(c_all.T, dmod_cols, w_ada[0], m_w_ada[0], v_w_ada[0])

    def ordered(k):
        lead = lambda a: a[None]
        return (lead(ada[k]), g_b[k], g_n1[k], lead(big[0][k]), g_lb[k], g_og[k], g_qg[k], g_kg[k], g_sk[k],
                lead(big[1][k]), lead(big[2][k]), lead(big[3][k]), g_n2[k], lead(big[4][k]), lead(big[5][k]))

    return (loss, grad_x[None]) + ordered(0) + ordered(1) + ordered(2) + ordered(3)
```

```python
import functools

import jax
import jax.numpy as jnp
from jax import lax
from jax.experimental import pallas as pl
from jax.experimental.pallas import tpu as pltpu

F32 = jnp.float32
BF16 = jnp.bfloat16
HIGHEST = lax.Precision.HIGHEST
MESH = pl.DeviceIdType.MESH

D_MODEL = 2048
A_WIDTH = 1024
A_HEADS = 8
A_HEAD_DIM = 128
A_CHUNK = 64
B_WIDTH = 1024
B_HEAD_DIM = 64
B_GROUP = 4
B_KV_HEADS = 4
B_KV_WIDTH = 256
BLOCK = 128
MLP_HIDDEN = 8192
IN_WIDTH = 9728
N_MOD = 6
EPS = 1e-6
N_CHIPS = 4
N_DEV = 8

OFF_QA, OFF_FA, OFF_IA, OFF_GA = 0, 1024, 2048, 3072
OFF_QB, OFF_KB, OFF_VB = 4096, 5120, 5376
OFF_GATE_A, OFF_GATE_B = 5632, 7680

ADAM_LR = 0.001
ADAM_B1 = 0.9
ADAM_B2 = 0.999
ADAM_EPS = 1e-08
ADAM_WD = 0.01
ADAM_STEP = 10

VMEM_LIMIT_V7X = 48 * 1024 * 1024
NEG_BIG = -1e30


def _params(sem=None, vmem=VMEM_LIMIT_V7X):
    return pltpu.CompilerParams(dimension_semantics=sem, vmem_limit_bytes=vmem)


class _Plan:
    def __init__(self, ins, outs, sems, stages, aliases=None, mid_at=()):
        self.ins, self.outs, self.sems, self.stages, self.aliases = ins, outs, sems, stages, aliases or {}
        self.mid_at = tuple(mid_at)
        assert len(self.mid_at) == len(stages) - 2


def _join(a, b):
    assert len(a.stages) == 2 and len(b.stages) == 2
    ni, no, ns = len(a.ins), len(a.outs), len(a.sems)

    def stage(k):
        def run(pi, po, ps):
            a.stages[k](pi[:ni], po[:no], ps[:ns])
            b.stages[k](pi[ni:], po[no:], ps[ns:])
        return run

    aliases = dict(a.aliases)
    aliases.update({ni + i: no + o for i, o in b.aliases.items()})
    return _Plan(a.ins + b.ins, a.outs + b.outs, a.sems + b.sems, [stage(0), stage(1)], aliases)


def _pcall(body, plan=None, **kw):
    if plan is None:
        return pl.pallas_call(body, **kw)
    grid = kw["grid"]
    single = not isinstance(kw["out_specs"], (list, tuple))
    in_specs = list(kw["in_specs"])
    out_specs = [kw["out_specs"]] if single else list(kw["out_specs"])
    out_shape = [kw["out_shape"]] if single else list(kw["out_shape"])
    scratch = list(kw.get("scratch_shapes", ()))
    n_in, n_out, n_scr = len(in_specs), len(out_specs), len(scratch)
    n_pi, n_po = len(plan.ins), len(plan.outs)
    total = 1
    for g in grid:
        total *= g
    n_st = len(plan.stages)

    def wrapped(*refs):
        o0 = n_in + n_pi
        s0 = o0 + n_out + n_po
        pi, po, ps = refs[n_in:o0], refs[o0 + n_out:s0], refs[s0 + n_scr:]
        lin = 0
        for d, g in enumerate(grid):
            lin = lin * g + pl.program_id(d)
        for si, frac in enumerate((0.0,) + plan.mid_at):
            @pl.when(lin == int(frac * (total - 1)))
            def _(si=si):
                plan.stages[si](pi, po, ps)
        body(*refs[:n_in], *refs[o0:o0 + n_out], *refs[s0:s0 + n_scr])

        @pl.when(lin == total - 1)
        def _():
            plan.stages[-1](pi, po, ps)

    any_spec = pl.BlockSpec(memory_space=pl.ANY)
    call = pl.pallas_call(
        wrapped, name=kw["name"], grid=grid, in_specs=in_specs + [any_spec] * n_pi,
        out_specs=out_specs + [any_spec] * n_po, out_shape=out_shape + list(plan.outs),
        scratch_shapes=scratch + list(plan.sems),
        input_output_aliases={n_in + i: n_out + o for i, o in plan.aliases.items()},
        compiler_params=_params(("arbitrary",) * len(grid)))

    def run(*args):
        res = call(*args, *plan.ins)
        outs = list(res[:n_out])
        return (outs[0] if single else outs), list(res[n_out:])

    return run


def _run_plan(plan, name):
    return _pcall(lambda: None, plan=plan, name=name, grid=(1,), in_specs=[], out_specs=[], out_shape=[])()[1]


def _sig(x):
    return 1.0 / (1.0 + jnp.exp(-x))


def _nn(a, b):
    return lax.dot_general(a.astype(BF16), b.astype(BF16), (((1,), (0,)), ((), ())), preferred_element_type=F32)


def _nt(a, b):
    return lax.dot_general(a.astype(BF16), b.astype(BF16), (((1,), (1,)), ((), ())), preferred_element_type=F32)


def _tn(a, b):
    return lax.dot_general(a.astype(BF16), b.astype(BF16), (((0,), (0,)), ((), ())), preferred_element_type=F32)


def _mm(a, b, *, name, ta=False, tb=False, bm=1024, bn=1024, bk=2048, out_dtypes=(F32,), epi=None, extras=(),
        extra_cols=None, plan=None):
    if ta:
        K, M = a.shape
        bk = K
    else:
        M, K = a.shape
    if tb:
        N, K2 = b.shape
    else:
        K2, N = b.shape
    bm, bn, bk = min(bm, M), min(bn, N), min(bk, K)
    assert K == K2 and M % bm == 0 and N % bn == 0 and K % bk == 0, (name, a.shape, b.shape)
    nk = K // bk
    a_spec = pl.BlockSpec((bk, bm), lambda i, j, k: (k, i)) if ta else pl.BlockSpec((bm, bk), lambda i, j, k: (i, k))
    b_spec = pl.BlockSpec((bn, bk), lambda i, j, k: (j, k)) if tb else pl.BlockSpec((bk, bn), lambda i, j, k: (k, j))
    t_spec = pl.BlockSpec((bm, bn), lambda i, j, k: (i, j))
    extra_cols = extra_cols or (0,) * len(extras)
    e_specs = [pl.BlockSpec((bm, bn), lambda i, j, k, off=off: (i, off + j)) for off in extra_cols]
    dims = (((1,), (1 if tb else 0,)), ((), ()))
    n_e, n_o = len(extras), len(out_dtypes)

    def body(*refs):
        a_ref, b_ref = refs[0], refs[1]
        e_refs = refs[2:2 + n_e]
        o_refs = refs[2 + n_e:2 + n_e + n_o]

        def finish(acc):
            outs = (acc,) if epi is None else epi(acc, *[e[...] for e in e_refs])
            for o_ref, o in zip(o_refs, outs):
                o_ref[...] = o.astype(o_ref.dtype)

        if ta:
            at_ref = refs[-1]

            @pl.when(pl.program_id(1) == 0)
            def _():
                at_ref[...] = a_ref[...].T

            lhs = at_ref[...]
        else:
            lhs = a_ref[...].astype(BF16)
        part = lax.dot_general(lhs, b_ref[...].astype(BF16), dims, preferred_element_type=F32)
        if nk == 1:
            finish(part)
        else:
            acc_ref = refs[-1]
            k = pl.program_id(2)

            @pl.when(k == 0)
            def _():
                acc_ref[...] = part

            @pl.when(k > 0)
            def _():
                acc_ref[...] += part

            @pl.when(k == nk - 1)
            def _():
                finish(acc_ref[...])

    if ta:
        assert a.dtype == BF16 and nk == 1
        scratch = [pltpu.VMEM((bm, bk), BF16)]
    else:
        scratch = [pltpu.VMEM((bm, bn), F32)] if nk > 1 else []
    out = _pcall(
        body, plan=plan, name=name, grid=(M // bm, N // bn, nk),
        in_specs=[a_spec, b_spec] + e_specs,
        out_specs=[t_spec] * n_o,
        out_shape=[jax.ShapeDtypeStruct((M, N), dt) for dt in out_dtypes],
        scratch_shapes=scratch,
        compiler_params=_params(("parallel", "arbitrary", "arbitrary")),
    )(a, b, *extras)
    if plan is not None:
        return (out[0][0] if n_o == 1 else out[0]), out[1]
    return out[0] if n_o == 1 else out


def _ada_fwd(c_all, w_ada, b_cols):
    n = w_ada.shape[1]
    bn = 512

    def body(c_ref, w_ref, b_ref, o_ref):
        cv = c_ref[...]
        sc = cv * _sig(cv)
        o_ref[...] = jnp.dot(sc, w_ref[...], precision=HIGHEST, preferred_element_type=F32) + b_ref[...]

    return _pcall(
        body, name="ada_fwd", grid=(n // bn,),
        in_specs=[pl.BlockSpec((N_DEV, D_MODEL), lambda j: (0, 0)), pl.BlockSpec((D_MODEL, bn), lambda j: (0, j)),
                  pl.BlockSpec((1, bn), lambda j: (0, j))],
        out_specs=pl.BlockSpec((N_DEV, bn), lambda j: (0, j)),
        out_shape=jax.ShapeDtypeStruct((N_DEV, n), F32),
        compiler_params=_params(("parallel",)),
    )(c_all, w_ada, b_cols)


ROWS_EW = 256


def _rms_fwd_math(x, gain, scale, shift):
    rstd = lax.rsqrt(jnp.mean(x * x, axis=-1, keepdims=True) + EPS)
    xhat = x * rstd
    n = xhat * gain
    return n * (1.0 + scale) + shift, xhat, n, rstd


def _rms_bwd_math(dh, xhat, n, rstd, gain, scale):
    dn = dh * (1.0 + scale)
    dxhat = dn * gain
    dx = rstd * (dxhat - xhat * jnp.mean(dxhat * xhat, axis=-1, keepdims=True))
    d_scale = jnp.sum(dh * n, axis=0, keepdims=True)
    d_shift = jnp.sum(dh, axis=0, keepdims=True)
    d_gain = jnp.sum(dn * xhat, axis=0, keepdims=True)
    return dx, d_scale, d_shift, d_gain


def _row_spec(w=D_MODEL, br=ROWS_EW):
    return pl.BlockSpec((br, w), lambda i: (i, 0))


def _vec_spec(r=8, w=D_MODEL):
    return pl.BlockSpec((r, w), lambda i: (0, 0))


def _norm1_fwd(x, gain, mod8, plan=None):
    T = x.shape[0]

    def body(x_ref, g_ref, m_ref, h_ref):
        h, _, _, _ = _rms_fwd_math(x_ref[...], g_ref[...], m_ref[1:2, :], m_ref[0:1, :])
        h_ref[...] = h.astype(BF16)

    return _pcall(
        body, plan=plan, name="norm1_fwd", grid=(T // ROWS_EW,),
        in_specs=[_row_spec(), _vec_spec(1), _vec_spec()],
        out_specs=_row_spec(), out_shape=jax.ShapeDtypeStruct((T, D_MODEL), BF16),
        compiler_params=_params(("parallel",)),
    )(x, gain, mod8)


def _res_norm2_fwd(x, mo, gain, mod8):
    T = x.shape[0]

    def body(x_ref, mo_ref, g_ref, m_ref, x1_ref, h_ref):
        x1 = x_ref[...] + m_ref[2:3, :] * mo_ref[...]
        x1_ref[...] = x1
        h, _, _, _ = _rms_fwd_math(x1, g_ref[...], m_ref[4:5, :], m_ref[3:4, :])
        h_ref[...] = h.astype(BF16)

    return _pcall(
        body, name="res_norm2_fwd", grid=(T // ROWS_EW,),
        in_specs=[_row_spec(), _row_spec(), _vec_spec(1), _vec_spec()],
        out_specs=[_row_spec(), _row_spec()],
        out_shape=[jax.ShapeDtypeStruct((T, D_MODEL), F32), jax.ShapeDtypeStruct((T, D_MODEL), BF16)],
        compiler_params=_params(("parallel",)),
    )(x, mo, gain, mod8)


def _loss_bwd(x1, mlp, target, mod8):
    T = x1.shape[0]

    def body(x1_ref, mlp_ref, t_ref, m_ref, dy_ref, dmlp_ref, st_ref):
        i = pl.program_id(0)
        gate = m_ref[5:6, :]
        mlp_v = mlp_ref[...]
        err = x1_ref[...] + gate * mlp_v - t_ref[...]
        dy = err * (1.0 / D_MODEL)
        dy_ref[...] = dy
        dmlp_ref[...] = (dy * gate).astype(BF16)

        @pl.when(i == 0)
        def _():
            st_ref[...] = jnp.zeros_like(st_ref)

        st_ref[0:1, :] += jnp.sum(err * err, axis=0, keepdims=True)
        st_ref[1:2, :] += jnp.sum(dy * mlp_v, axis=0, keepdims=True)

    return _pcall(
        body, name="loss_bwd", grid=(T // ROWS_EW,),
        in_specs=[_row_spec(), _row_spec(), _row_spec(), _vec_spec()],
        out_specs=[_row_spec(), _row_spec(), _vec_spec()],
        out_shape=[jax.ShapeDtypeStruct((T, D_MODEL), F32), jax.ShapeDtypeStruct((T, D_MODEL), BF16),
                   jax.ShapeDtypeStruct((8, D_MODEL), F32)],
        compiler_params=_params(("arbitrary",)),
    )(x1, mlp, target, mod8)


def _norm2_bwd(dh2, x1, dy, mo, gain, mod8):
    T = x1.shape[0]

    def body(dh_ref, x1_ref, dy_ref, mo_ref, g_ref, m_ref, dx1_ref, dmo_ref, st_ref):
        i = pl.program_id(0)
        gain_v, scale = g_ref[...], m_ref[4:5, :]
        _, xhat, n, rstd = _rms_fwd_math(x1_ref[...], gain_v, scale, m_ref[3:4, :])
        dx, d_scale, d_shift, d_gain = _rms_bwd_math(dh_ref[...], xhat, n, rstd, gain_v, scale)
        dx1 = dy_ref[...] + dx
        dx1_ref[...] = dx1
        dmo_ref[...] = (dx1 * m_ref[2:3, :]).astype(BF16)

        @pl.when(i == 0)
        def _():
            st_ref[...] = jnp.zeros_like(st_ref)

        st_ref[0:1, :] += d_scale
        st_ref[1:2, :] += d_shift
        st_ref[2:3, :] += d_gain
        st_ref[3:4, :] += jnp.sum(dx1 * mo_ref[...], axis=0, keepdims=True)

    return _pcall(
        body, name="norm2_bwd", grid=(T // ROWS_EW,),
        in_specs=[_row_spec(), _row_spec(), _row_spec(), _row_spec(), _vec_spec(1), _vec_spec()],
        out_specs=[_row_spec(), _row_spec(), _vec_spec()],
        out_shape=[jax.ShapeDtypeStruct((T, D_MODEL), F32), jax.ShapeDtypeStruct((T, D_MODEL), BF16),
                   jax.ShapeDtypeStruct((8, D_MODEL), F32)],
        compiler_params=_params(("arbitrary",)),
    )(dh2, x1, dy, mo, gain, mod8)


def _norm1_bwd(dh, x, dx1, gain, mod8):
    T = x.shape[0]

    def body(dh_ref, x_ref, dx1_ref, g_ref, m_ref, dx_ref, st_ref):
        i = pl.program_id(0)
        gain_v, scale = g_ref[...], m_ref[1:2, :]
        _, xhat, n, rstd = _rms_fwd_math(x_ref[...], gain_v, scale, m_ref[0:1, :])
        dx, d_scale, d_shift, d_gain = _rms_bwd_math(dh_ref[...], xhat, n, rstd, gain_v, scale)
        dx_ref[...] = dx1_ref[...] + dx

        @pl.when(i == 0)
        def _():
            st_ref[...] = jnp.zeros_like(st_ref)

        st_ref[0:1, :] += d_scale
        st_ref[1:2, :] += d_shift
        st_ref[2:3, :] += d_gain

    return _pcall(
        body, name="norm1_bwd", grid=(T // ROWS_EW,),
        in_specs=[_row_spec(), _row_spec(), _row_spec(), _vec_spec(1), _vec_spec()],
        out_specs=[_row_spec(), _vec_spec()],
        out_shape=[jax.ShapeDtypeStruct((T, D_MODEL), F32), jax.ShapeDtypeStruct((8, D_MODEL), F32)],
        compiler_params=_params(("arbitrary",)),
    )(dh, x, dx1, gain, mod8)


MERGE_BC = 512


def _hgrn_rows(T):
    return 512 if T >= 1024 else 128


def _lower_bound(lbl):
    e = jnp.exp(lbl - jnp.max(lbl, axis=0, keepdims=True))
    return e[0:1, :] / (e[0:1, :] + e[1:2, :])


def _chunk_sum_matrix(rows, backward):
    shift = A_CHUNK.bit_length() - 1
    r = lax.broadcasted_iota(jnp.int32, (rows, rows), 0)
    c = lax.broadcasted_iota(jnp.int32, (rows, rows), 1)
    same = jnp.right_shift(r, shift) == jnp.right_shift(c, shift)
    return (same & ((r <= c) if backward else (r >= c))).astype(BF16)


def _chunk_sums(m, x):
    n = x.shape[1]
    hi = x.astype(BF16)
    rest = x - hi.astype(F32)
    mid = rest.astype(BF16)
    lo = (rest - mid.astype(F32)).astype(BF16)
    y = jnp.dot(m, jnp.concatenate([hi, mid, lo], axis=1), preferred_element_type=F32)
    return y[:, 0:n] + y[:, n:2 * n] + y[:, 2 * n:3 * n]


def _hgrn_block_pre(q, fl, lb, m_fwd):
    sg = _sig(fl)
    f = lb + (1.0 - lb) * sg
    sq = _sig(q)
    return dict(sg=sg, f=f, k=1.0 - f, sq=sq, qf=q * sq, b=_chunk_sums(m_fwd, jnp.log(f)))


def _hgrn_chunk_local(pre, r):
    C = A_CHUNK
    qf, k, b = pre["qf"][r], pre["k"][r], pre["b"][r]
    causal = lax.broadcasted_iota(jnp.int32, (C, C), 0) >= lax.broadcasted_iota(jnp.int32, (C, C), 1)
    bm = b[C // 2 - 1:C // 2, :]
    bl = b[C - 1:C, :]
    e_q, e_k = jnp.exp(b - bm), jnp.exp(bm - b)
    e_b, e_l = jnp.exp(b), jnp.exp(bl - b)
    qd, kd = qf * e_q, k * e_k
    qe, ke = qf * e_b, k * e_l
    att = jnp.where(causal, _nt(qd, kd), 0.0)
    return dict(causal=causal, e_q=e_q, e_k=e_k, e_b=e_b, e_l=e_l, qd=qd, kd=kd, qe=qe, ke=ke, att=att, dec=jnp.exp(bl))


def _hgrn_chunk_fwd(pre, r, v, st):
    c = _hgrn_chunk_local(pre, r)
    c["o"] = _nn(c["att"], v) + _nt(c["qe"], st)
    return c


def _lockstep(gens):
    out = [None] * len(gens)
    live = list(enumerate(gens))
    while live:
        still = []
        for i, g in live:
            try:
                next(g)
                still.append((i, g))
            except StopIteration as done:
                out[i] = done.value
        live = still
    return out


HGRN_HEADS_PER_STEP = 4


def _hgrn_fwd(proj, lb_logits, o_gain, plan=None):
    T = proj.shape[0]
    BR = _hgrn_rows(T)
    cps = BR // A_CHUNK
    K, NH = A_HEAD_DIM, HGRN_HEADS_PER_STEP
    W = NH * K

    def col(off):
        return pl.BlockSpec((BR, W), lambda h, cb: (cb, off // W + h))

    def body(q_ref, f_ref, i_ref, g_ref, lbl_ref, og_ref, o_ref, s_ref, st):
        @pl.when(pl.program_id(1) == 0)
        def _():
            st[...] = jnp.zeros_like(st)

        lb_all = _lower_bound(lbl_ref[...])
        m_fwd = _chunk_sum_matrix(BR, False)
        pre = [_hgrn_block_pre(q_ref[:, n * K:(n + 1) * K], f_ref[:, n * K:(n + 1) * K], lb_all[:, n * K:(n + 1) * K], m_fwd)
               for n in range(NH)]
        def local(n, ci):
            r, hs = slice(ci * A_CHUNK, (ci + 1) * A_CHUNK), slice(n * K, (n + 1) * K)
            v = i_ref[r, hs]
            c = _hgrn_chunk_local(pre[n], r)
            yield
            return dict(o=_nn(c["att"], v), ds=_tn(v, c["ke"]), qe=c["qe"], dec=c["dec"])

        def chain(n, loc):
            hs = slice(n * K, (n + 1) * K)
            state = st[n]
            for ci, p in enumerate(loc):
                r = slice(ci * A_CHUNK, (ci + 1) * A_CHUNK)
                s_ref[n, ci] = state
                o = p["o"] + _nt(p["qe"], state)
                state = state * p["dec"] + p["ds"]
                yield
                on = o * lax.rsqrt(jnp.mean(o * o, axis=-1, keepdims=True) + EPS)
                g = g_ref[r, hs]
                o_ref[r, hs] = (on * og_ref[:, hs] * (g * _sig(g))).astype(BF16)
            st[n] = state

        loc = _lockstep([local(n, ci) for n in range(NH) for ci in range(cps)])
        _lockstep([chain(n, loc[n * cps:(n + 1) * cps]) for n in range(NH)])

    return _pcall(
        body, plan=plan, name="hgrn_fwd", grid=(A_HEADS // NH, T // BR),
        in_specs=[col(OFF_QA), col(OFF_FA), col(OFF_IA), col(OFF_GA),
                  pl.BlockSpec((2, W), lambda h, cb: (0, h)), pl.BlockSpec((1, W), lambda h, cb: (0, h))],
        out_specs=[pl.BlockSpec((BR, W), lambda h, cb: (cb, h)),
                   pl.BlockSpec((NH, cps, K, K), lambda h, cb: (h, cb, 0, 0))],
        out_shape=[jax.ShapeDtypeStruct((T, A_WIDTH), BF16),
                   jax.ShapeDtypeStruct((A_HEADS, T // A_CHUNK, K, K), F32)],
        scratch_shapes=[pltpu.VMEM((NH, K, K), F32)],
        compiler_params=_params(("parallel", "arbitrary")),
    )(proj, proj, proj, proj, lb_logits, o_gain)


def _hgrn_bwd(proj, lb_logits, o_gain, states, do, plan=None):
    T = proj.shape[0]
    BR = _hgrn_rows(T)
    cps = BR // A_CHUNK
    ncb = T // BR
    K, C, NH = A_HEAD_DIM, A_CHUNK, HGRN_HEADS_PER_STEP
    W = NH * K

    def col(off):
        return pl.BlockSpec((BR, W), lambda h, cb: (ncb - 1 - cb, off // W + h))

    def body(q_ref, f_ref, i_ref, g_ref, lbl_ref, og_ref, s_ref, do_ref,
             dq_ref, df_ref, di_ref, dg_ref, dlb_ref, dog_ref, dst):
        @pl.when(pl.program_id(1) == 0)
        def _():
            dst[...] = jnp.zeros_like(dst)
            dlb_ref[...] = jnp.zeros_like(dlb_ref)
            dog_ref[...] = jnp.zeros_like(dog_ref)

        lb_all = _lower_bound(lbl_ref[...])
        row = lax.broadcasted_iota(jnp.int32, (C, K), 0)
        m_fwd, m_bwd = _chunk_sum_matrix(BR, False), _chunk_sum_matrix(BR, True)
        pre = [_hgrn_block_pre(q_ref[:, n * K:(n + 1) * K], f_ref[:, n * K:(n + 1) * K], lb_all[:, n * K:(n + 1) * K], m_fwd)
               for n in range(NH)]
        def local(n, ci):
            r, hs = slice(ci * C, (ci + 1) * C), slice(n * K, (n + 1) * K)
            gain = og_ref[:, hs]
            st = s_ref[n, ci]
            v = i_ref[r, hs]
            q = q_ref[r, hs]
            c = _hgrn_chunk_fwd(pre[n], r, v, st)
            yield
            o = c["o"]
            rn = lax.rsqrt(jnp.mean(o * o, axis=-1, keepdims=True) + EPS)
            on = o * rn
            g = g_ref[r, hs]
            sgg = _sig(g)
            dy = do_ref[r, hs]
            d_ong = dy * (g * sgg)
            dg_ref[r, hs] = (dy * (on * gain) * (sgg * (1.0 + g * (1.0 - sgg)))).astype(BF16)
            d_on = d_ong * gain
            d_o = rn * (d_on - on * jnp.mean(d_on * on, axis=-1, keepdims=True))
            datt = jnp.where(c["causal"], _nt(d_o, v), 0.0)
            dqe = _nn(d_o, st)
            yield
            dqd = _nn(datt, c["kd"])
            dkd = _tn(datt, c["qd"])
            dv = _tn(c["att"], d_o)
            ds = _tn(d_o, c["qe"])
            yield
            t_q, t_k = dqd * c["qd"], dkd * c["kd"]
            sq = pre[n]["sq"][r]
            dq_ref[r, hs] = ((dqd * c["e_q"] + dqe * c["e_b"]) * (sq * (1.0 + q * (1.0 - sq)))).astype(BF16)
            return dict(v=v, st=st, ke=c["ke"], e_l=c["e_l"], dec=c["dec"], dv=dv, ds=ds, dk=dkd * c["e_k"],
                        db=t_q - t_k + dqe * c["qe"], dbm=jnp.sum(t_k - t_q, axis=0, keepdims=True),
                        d_og=jnp.sum(d_ong * on, axis=0, keepdims=True))

        def chain(n, loc):
            hs = slice(n * K, (n + 1) * K)
            dst_next = dst[n]
            db_of, dk_of = [None] * cps, [None] * cps
            for ci in reversed(range(cps)):
                p = loc[ci]
                di_ref[ci * C:(ci + 1) * C, hs] = (p["dv"] + _nt(p["ke"], dst_next)).astype(BF16)
                dke = _nn(p["v"], dst_next)
                yield
                t_l = dke * p["ke"]
                dbl = jnp.sum(t_l, axis=0, keepdims=True) + jnp.sum(dst_next * p["st"], axis=0, keepdims=True) * p["dec"]
                db_of[ci] = p["db"] - t_l + jnp.where(row == C // 2 - 1, p["dbm"], 0.0) + jnp.where(row == C - 1, dbl, 0.0)
                dk_of[ci] = p["dk"] + dke * p["e_l"]
                dst_next = dst_next * p["dec"] + p["ds"]
            dst[n] = dst_next
            return db_of, dk_of

        loc = _lockstep([local(n, ci) for n in range(NH) for ci in range(cps)])
        loc = [loc[n * cps:(n + 1) * cps] for n in range(NH)]
        chains = _lockstep([chain(n, loc[n]) for n in range(NH)])
        for n in range(NH):
            hs = slice(n * K, (n + 1) * K)
            db_of, dk_of = chains[n]
            d_og = loc[n][0]["d_og"]
            for p in loc[n][1:]:
                d_og = d_og + p["d_og"]
            dog_ref[0:1, hs] += d_og
            lb, sg = lb_all[:, hs], pre[n]["sg"]
            dlf = _chunk_sums(m_bwd, jnp.concatenate(db_of, axis=0))
            df = dlf / pre[n]["f"] - jnp.concatenate(dk_of, axis=0)
            df_ref[:, hs] = (df * (1.0 - lb) * sg * (1.0 - sg)).astype(BF16)
            dlb_ref[0:1, hs] += jnp.sum(df * (1.0 - sg), axis=0, keepdims=True)

    ocol = pl.BlockSpec((BR, W), lambda h, cb: (ncb - 1 - cb, h))
    vec = pl.BlockSpec((8, W), lambda h, cb: (0, h))
    return _pcall(
        body, plan=plan, name="hgrn_bwd", grid=(A_HEADS // NH, ncb),
        in_specs=[col(OFF_QA), col(OFF_FA), col(OFF_IA), col(OFF_GA),
                  pl.BlockSpec((2, W), lambda h, cb: (0, h)), pl.BlockSpec((1, W), lambda h, cb: (0, h)),
                  pl.BlockSpec((NH, cps, K, K), lambda h, cb: (h, ncb - 1 - cb, 0, 0)),
                  pl.BlockSpec((BR, W), lambda h, cb: (ncb - 1 - cb, h))],
        out_specs=[ocol, ocol, ocol, ocol, vec, vec],
        out_shape=[jax.ShapeDtypeStruct((T, A_WIDTH), BF16)] * 4 + [jax.ShapeDtypeStruct((8, A_WIDTH), F32)] * 2,
        scratch_shapes=[pltpu.VMEM((NH, K, K), F32)],
        compiler_params=_params(("parallel", "arbitrary")),
    )(proj, proj, proj, proj, lb_logits, o_gain, states, do)


def _head_norm(x):
    r = lax.rsqrt(jnp.mean(x * x, axis=-1, keepdims=True) + EPS)
    return x * r, r


def _head_norm_bwd(dy, xn, r, gain):
    dxn = dy * gain
    return r * (dxn - xn * jnp.mean(dxn * xn, axis=-1, keepdims=True)), jnp.sum(dy * xn, axis=0, keepdims=True)


def _swa_mask(has_prev):
    rows = B_GROUP * BLOCK
    r = lax.broadcasted_iota(jnp.int32, (rows, 2 * BLOCK), 0) % BLOCK
    c = lax.broadcasted_iota(jnp.int32, (rows, 2 * BLOCK), 1)
    rel = r + BLOCK - c
    return (rel >= 0) & (rel < BLOCK) & ((c >= BLOCK) | has_prev)


def _swa_head_fwd(j, q_ref, kp_ref, kc_ref, vp_ref, vc_ref, qg, kg, sk_ref, mask):
    hs = slice(j * B_HEAD_DIM, (j + 1) * B_HEAD_DIM)
    kcat = jnp.concatenate([kp_ref[:, hs], kc_ref[:, hs]], axis=0)
    vcat = jnp.concatenate([vp_ref[:, hs], vc_ref[:, hs]], axis=0)
    qs = jnp.concatenate([q_ref[:, pl.ds((j * B_GROUP + g) * B_HEAD_DIM, B_HEAD_DIM)] for g in range(B_GROUP)], axis=0)
    kn, kr = _head_norm(kcat)
    qn, qr = _head_norm(qs)
    kh, qh = kn * kg, qn * qg
    yield
    s = jnp.where(mask, _nt(qh, kh) * (B_HEAD_DIM ** -0.5), NEG_BIG)
    yield
    sink = jnp.concatenate(
        [jnp.broadcast_to(sk_ref[0:1, pl.ds(j * B_GROUP + g, 1)], (BLOCK, 1)) for g in range(B_GROUP)], axis=0)
    m = jnp.maximum(jnp.max(s, axis=-1, keepdims=True), sink)
    p = jnp.exp(s - m)
    e_sink = jnp.exp(sink - m)
    inv = 1.0 / (jnp.sum(p, axis=-1, keepdims=True) + e_sink)
    prob = p * inv
    return dict(vcat=vcat, kn=kn, kr=kr, qn=qn, qr=qr, kh=kh, qh=qh, prob=prob, p_sink=e_sink * inv)


def _swa_in_specs(nb, last):
    def qi(n):
        return jnp.minimum(n, last)

    q = pl.BlockSpec((BLOCK, B_WIDTH), lambda n: (qi(n), OFF_QB // B_WIDTH))
    kc = pl.BlockSpec((BLOCK, B_KV_WIDTH), lambda n: (qi(n), OFF_KB // B_KV_WIDTH))
    kp = pl.BlockSpec((BLOCK, B_KV_WIDTH), lambda n: (jnp.maximum(qi(n) - 1, 0), OFF_KB // B_KV_WIDTH))
    vc = pl.BlockSpec((BLOCK, B_KV_WIDTH), lambda n: (qi(n), OFF_VB // B_KV_WIDTH))
    vp = pl.BlockSpec((BLOCK, B_KV_WIDTH), lambda n: (jnp.maximum(qi(n) - 1, 0), OFF_VB // B_KV_WIDTH))
    small = [pl.BlockSpec((1, B_HEAD_DIM), lambda n: (0, 0)), pl.BlockSpec((1, B_HEAD_DIM), lambda n: (0, 0)),
             pl.BlockSpec((1, B_GROUP * B_KV_HEADS), lambda n: (0, 0))]
    return [q, kp, kc, vp, vc] + small


def _swa_fwd(proj, q_gain, k_gain, sinks, plan=None):
    T = proj.shape[0]
    nb = T // BLOCK

    def body(q_ref, kp_ref, kc_ref, vp_ref, vc_ref, qg_ref, kg_ref, sk_ref, o_ref):
        mask = _swa_mask(pl.program_id(0) > 0)

        def head(j):
            c = yield from _swa_head_fwd(j, q_ref, kp_ref, kc_ref, vp_ref, vc_ref, qg_ref[...], kg_ref[...], sk_ref, mask)
            yield
            o = _nn(c["prob"], c["vcat"])
            yield
            for g in range(B_GROUP):
                o_ref[:, pl.ds((j * B_GROUP + g) * B_HEAD_DIM, B_HEAD_DIM)] = o[g * BLOCK:(g + 1) * BLOCK].astype(BF16)

        _lockstep([head(j) for j in range(B_KV_HEADS)])

    return _pcall(
        body, plan=plan, name="swa_fwd", grid=(nb,),
        in_specs=_swa_in_specs(nb, nb - 1),
        out_specs=pl.BlockSpec((BLOCK, B_WIDTH), lambda n: (n, 0)),
        out_shape=jax.ShapeDtypeStruct((T, B_WIDTH), BF16),
        compiler_params=_params(("parallel",)),
    )(proj, proj, proj, proj, proj, q_gain, k_gain, sinks)


def _swa_bwd(proj, q_gain, k_gain, sinks, do, plan=None):
    T = proj.shape[0]
    nb = T // BLOCK
    scale = B_HEAD_DIM ** -0.5

    def body(q_ref, kp_ref, kc_ref, vp_ref, vc_ref, qg_ref, kg_ref, sk_ref, do_ref,
             dq_ref, dkv_ref, sm_ref, ck, cv):
        n = pl.program_id(0)

        @pl.when(n == 0)
        def _():
            ck[...] = jnp.zeros_like(ck)
            cv[...] = jnp.zeros_like(cv)
            sm_ref[...] = jnp.zeros_like(sm_ref)

        @pl.when(n < nb)
        def _():
            mask = _swa_mask(n > 0)
            qg, kg = qg_ref[...], kg_ref[...]
            lane = lax.broadcasted_iota(jnp.int32, (1, BLOCK), 1)
            def head(j):
                hs = slice(j * B_HEAD_DIM, (j + 1) * B_HEAD_DIM)
                vs = slice(B_KV_WIDTH + j * B_HEAD_DIM, B_KV_WIDTH + (j + 1) * B_HEAD_DIM)
                c = yield from _swa_head_fwd(j, q_ref, kp_ref, kc_ref, vp_ref, vc_ref, qg, kg, sk_ref, mask)
                d_out = jnp.concatenate(
                    [do_ref[:, pl.ds((j * B_GROUP + g) * B_HEAD_DIM, B_HEAD_DIM)] for g in range(B_GROUP)], axis=0)
                prob = c["prob"]
                yield
                out = _nn(prob, c["vcat"])
                d_prob = _nt(d_out, c["vcat"])
                dv = _tn(prob, d_out)
                yield
                delta = jnp.sum(d_out * out, axis=-1, keepdims=True)
                ds = prob * (d_prob - delta)
                d_sink = -c["p_sink"] * delta
                yield
                dqh = _nn(ds, c["kh"]) * scale
                dkh = _tn(ds, c["qh"]) * scale
                yield
                dq, dqg = _head_norm_bwd(dqh, c["qn"], c["qr"], qg)
                dk, dkg = _head_norm_bwd(dkh, c["kn"], c["kr"], kg)
                d_sinks = jnp.zeros((1, BLOCK), F32)
                for g in range(B_GROUP):
                    dq_ref[:, pl.ds((j * B_GROUP + g) * B_HEAD_DIM, B_HEAD_DIM)] = dq[g * BLOCK:(g + 1) * BLOCK].astype(BF16)
                    tot = jnp.sum(d_sink[g * BLOCK:(g + 1) * BLOCK], axis=0, keepdims=True)
                    d_sinks = d_sinks + jnp.where(lane == j * B_GROUP + g, tot, 0.0)
                dkv_ref[:, hs] = (ck[:, hs] + dk[0:BLOCK]).astype(BF16)
                dkv_ref[:, vs] = (cv[:, hs] + dv[0:BLOCK]).astype(BF16)
                ck[:, hs] = dk[BLOCK:2 * BLOCK]
                cv[:, hs] = dv[BLOCK:2 * BLOCK]
                return dqg, dkg, d_sinks

            small = _lockstep([head(j) for j in range(B_KV_HEADS)])
            sm_ref[0:1, 0:B_HEAD_DIM] += small[0][0] + small[1][0] + small[2][0] + small[3][0]
            sm_ref[1:2, 0:B_HEAD_DIM] += small[0][1] + small[1][1] + small[2][1] + small[3][1]
            sm_ref[2:3, :] += small[0][2] + small[1][2] + small[2][2] + small[3][2]

        @pl.when(n == nb)
        def _():
            dkv_ref[:, 0:B_KV_WIDTH] = ck[...].astype(BF16)
            dkv_ref[:, B_KV_WIDTH:2 * B_KV_WIDTH] = cv[...].astype(BF16)

    return _pcall(
        body, plan=plan, name="swa_bwd", grid=(nb + 1,),
        in_specs=_swa_in_specs(nb, nb - 1) + [pl.BlockSpec((BLOCK, B_WIDTH), lambda n: (jnp.minimum(n, nb - 1), 0))],
        out_specs=[pl.BlockSpec((BLOCK, B_WIDTH), lambda n: (jnp.minimum(n, nb - 1), 0)),
                   pl.BlockSpec((BLOCK, 2 * B_KV_WIDTH), lambda n: (jnp.maximum(n - 1, 0), 0)),
                   pl.BlockSpec((8, BLOCK), lambda n: (0, 0))],
        out_shape=[jax.ShapeDtypeStruct((T, B_WIDTH), BF16), jax.ShapeDtypeStruct((T, 2 * B_KV_WIDTH), BF16),
                   jax.ShapeDtypeStruct((8, BLOCK), F32)],
        scratch_shapes=[pltpu.VMEM((BLOCK, B_KV_WIDTH), F32), pltpu.VMEM((BLOCK, B_KV_WIDTH), F32)],
        compiler_params=_params(("arbitrary",)),
    )(proj, proj, proj, proj, proj, q_gain, k_gain, sinks, do)


W_IN, W_A, W_B, W_OUT, W_MI, W_MO = range(6)


def _local_step(x, target, mod8, norm1_gain, norm2_gain, lb_logits, o_gain, q_gain, k_gain, sinks, parts, c_arr, chip_arr):
    relu2 = lambda u: (u, jnp.square(jnp.maximum(u, 0.0)))
    pair, half = {}, {}

    def exchange(ws, grads):
        return _sibling_exchange_plan([_grad_view(g, w) for w, g in zip(ws, grads)])

    def pair_sums(ws, grads, others):
        for w, g, o in zip(ws, grads, others):
            pair[w] = _pair_sum(_grad_view(g, w), o, c_arr, f"pair_sum{w}")

    def sum_slots(ws, slots):
        for w, s in zip(ws, slots):
            half[w] = _sum_slots(pair[w], s, w, chip_arr, f"sum_slots{w}")

    h, (w_in,) = _norm1_fwd(x, norm1_gain, mod8, plan=_gather_plan({W_IN: parts[W_IN]}, pass_at=(0.4, 0.7)))
    proj, (w_mi,) = _mm(h, w_in, name="mm_proj", bn=512, plan=_gather_plan({W_MI: parts[W_MI]}, pass_at=(0.47, 0.72)))
    (o_a, states), (w_a, w_b) = _hgrn_fwd(
        proj, lb_logits, o_gain, plan=_gather_plan({w: parts[w] for w in (W_A, W_B)}, pass_at=(0.4, 0.65)))
    o_b, (w_out,) = _swa_fwd(proj, q_gain, k_gain, sinks, plan=_gather_plan({W_OUT: parts[W_OUT]}, pass_at=(0.3, 0.5)))
    ya = _mm(o_a, w_a, name="mm_branch_a")
    gate_cols = (OFF_GATE_A // MERGE_BC, OFF_GATE_B // MERGE_BC)
    yb, merged = _mm(o_b, w_b, name="mm_branch_b", bn=MERGE_BC, out_dtypes=(F32, BF16),
                     extras=(proj, proj, ya), extra_cols=gate_cols + (0,),
                     epi=lambda acc, ga, gb, ya_: (acc, _sig(ga) * ya_ + _sig(gb) * acc))
    mo = _mm(merged, w_out, name="mm_out")
    x1, h2 = _res_norm2_fwd(x, mo, norm2_gain, mod8)
    (u, act), (w_mo,) = _mm(h2, w_mi, name="mm_mlp_in", out_dtypes=(F32, BF16), epi=relu2,
                            plan=_gather_plan({W_MO: parts[W_MO]}, pass_at=(0.6, 0.9)))
    mlp = _mm(act, w_mo, name="mm_mlp_out")
    dy, dmlp, st_loss = _loss_bwd(x1, mlp, target, mod8)
    g_mo = _mm(act, dmlp, name="mm_g_mlp_out", ta=True, bn=512)
    du, others = _mm(dmlp, w_mo, name="mm_d_act", tb=True, out_dtypes=(BF16,), extras=(u,),
                     epi=lambda acc, uu: (acc * (2.0 * jnp.maximum(uu, 0.0)),), plan=exchange([W_MO], [g_mo]))
    pair_sums([W_MO], [g_mo], others)
    g_mi, slots_mo = _mm(h2, du, name="mm_g_mlp_in", ta=True, bn=512, plan=_chip_exchange_plan({W_MO: pair[W_MO]}))
    dh2, others = _mm(du, w_mi, name="mm_d_h2", tb=True, plan=exchange([W_MI], [g_mi]))
    pair_sums([W_MI], [g_mi], others)
    sum_slots([W_MO], slots_mo)
    dx1, dmo, st_n2 = _norm2_bwd(dh2, x1, dy, mo, norm2_gain, mod8)
    def merge_bwd(dm, ga, gb, ya_, yb_):
        sa, sb = _sig(ga), _sig(gb)
        return dm * sa, dm * sb, dm * ya_ * sa * (1.0 - sa), dm * yb_ * sb * (1.0 - sb)

    dya, dyb, dga, dgb = _mm(dmo, w_out, name="mm_d_merged", tb=True, bn=MERGE_BC, out_dtypes=(BF16,) * 4,
                             extras=(proj, proj, ya, yb), extra_cols=gate_cols + (0, 0), epi=merge_bwd)
    g_out = _mm(merged, dmo, name="mm_g_out", ta=True, bn=512)
    do_a = _mm(dya, w_a, name="mm_d_oa", tb=True)
    g_a = _mm(o_a, dya, name="mm_g_branch_a", ta=True, bn=512)
    do_b = _mm(dyb, w_b, name="mm_d_ob", tb=True)
    g_b = _mm(o_b, dyb, name="mm_g_branch_b", ta=True, bn=512)
    mid = [W_A, W_B, W_OUT]
    (dqb, dkvb, st_swa), res = _swa_bwd(
        proj, q_gain, k_gain, sinks, do_b,
        plan=_join(_chip_exchange_plan({W_MI: pair[W_MI]}), exchange(mid, [g_a, g_b, g_out])))
    sum_slots([W_MI], res[:1])
    pair_sums(mid, [g_a, g_b, g_out], res[1:])
    (dqa, dfa, dia, dgga, d_lb, d_og), slots_mid = _hgrn_bwd(proj, lb_logits, o_gain, states, do_a,
                                                             plan=_chip_exchange_plan({w: pair[w] for w in mid}))
    sum_slots(mid, slots_mid)
    dproj = jnp.concatenate([dqa, dfa, dia, dgga, dqb, dkvb, dga, dgb], axis=1)
    hr = D_MODEL // 2
    h_send = lax.dynamic_slice(h, (0, (1 - c_arr[0]) * hr), (h.shape[0], hr))
    h_own = lax.dynamic_slice(h, (0, c_arr[0] * hr), (h.shape[0], hr))
    done = [W_A, W_B, W_OUT, W_MI, W_MO]
    g_send, res = _mm(h_send, dproj, name="mm_g_in_send", ta=True, bn=512,
                      plan=_sibling_share_plan([half[w] for w in done]))
    theirs = dict(zip(done, res))
    g_own, (g_other,) = _mm(h_own, dproj, name="mm_g_in_own", ta=True, bn=512, plan=_sibling_share_plan([g_send]))
    pair[W_IN] = _add_bf16(g_own, g_other, "pair_sum0")[None]
    dh, slots_in = _mm(dproj, w_in, name="mm_d_h", tb=True, bk=2432, plan=_chip_exchange_plan({W_IN: pair[W_IN]}))
    sum_slots([W_IN], slots_in)
    grad_x, st_n1 = _norm1_bwd(dh, x, dx1, norm1_gain, mod8)
    (theirs[W_IN],) = _run_plan(_sibling_share_plan([half[W_IN]]), "sibling_share_w_in")
    stats = dict(loss=st_loss, n2=st_n2, n1=st_n1, d_lb=d_lb, d_og=d_og, swa=st_swa)
    return grad_x, [half[w] for w in range(N_W)], [theirs[w] for w in range(N_W)], stats


def _ew_rows(rows, cols):
    br = 8
    while br * 2 <= rows and br * 2 * cols * 4 <= (1 << 20) and rows % (br * 2) == 0:
        br *= 2
    return br


def _cast_into_full(shard, w, chip_arr, name):
    sr, sc = shard.shape
    R, C, by_col = W_SHAPES[w]
    br = _ew_rows(sr, sc)
    nb = sr // br
    out_map = (lambda i, chip: (i, chip[0])) if by_col else (lambda i, chip: (chip[0] * nb + i, 0))

    def body(chip_ref, w_ref, o_ref):
        o_ref[...] = w_ref[...].astype(BF16)

    return _pcall(
        body, name=name,
        grid_spec=pltpu.PrefetchScalarGridSpec(
            num_scalar_prefetch=1, grid=(nb,),
            in_specs=[pl.BlockSpec((br, sc), lambda i, chip: (i, 0))],
            out_specs=pl.BlockSpec((br, sc), out_map)),
        out_shape=jax.ShapeDtypeStruct((R, C), BF16), compiler_params=_params(("parallel",)))(chip_arr, shard)


def _adamw_math(w, g, m, v):
    m = ADAM_B1 * m + (1.0 - ADAM_B1) * g
    v = ADAM_B2 * v + (1.0 - ADAM_B2) * (g * g)
    m_hat = m / (1.0 - ADAM_B1 ** ADAM_STEP)
    v_hat = v / (1.0 - ADAM_B2 ** ADAM_STEP)
    delta = -ADAM_LR * (m_hat / (jnp.sqrt(v_hat) + ADAM_EPS) + ADAM_WD * w)
    return delta, m, v


def _adamw(w, g, m, v, name):
    R, C = w.shape
    br = _ew_rows(R, C)
    spec = pl.BlockSpec((br, C), lambda i: (i, 0))

    def body(w_ref, g_ref, m_ref, v_ref, d_ref, nm_ref, nv_ref):
        d_ref[...], nm_ref[...], nv_ref[...] = _adamw_math(w_ref[...], g_ref[...], m_ref[...], v_ref[...])

    sh = jax.ShapeDtypeStruct((R, C), F32)
    return _pcall(body, name=name, grid=(R // br,), in_specs=[spec] * 4, out_specs=[spec] * 3, out_shape=[sh] * 3,
                  compiler_params=_params(("parallel",)))(w, g, m, v)


def _add_bf16(a, b, name):
    R, C = a.shape
    br = _ew_rows(R, C)
    spec = pl.BlockSpec((br, C), lambda i: (i, 0))

    def body(a_ref, b_ref, o_ref):
        o_ref[...] = (a_ref[...] + b_ref[...]).astype(BF16)

    return _pcall(body, name=name, grid=(R // br,), in_specs=[spec, spec], out_specs=spec,
                  out_shape=jax.ShapeDtypeStruct((R, C), BF16), compiler_params=_params(("parallel",)))(a, b)


def _adamw_halves(w, own, other, m, v, c_arr, name):
    R, C = w.shape
    hr = R // 2
    br = _ew_rows(hr, C)
    nb = hr // br
    full = pl.BlockSpec((br, C), lambda h, i, c_ref: (h * nb + i, 0))
    half = pl.BlockSpec((br, C), lambda h, i, c_ref: (i, 0))

    def body(c_ref, w_ref, own_ref, oth_ref, m_ref, v_ref, g_ref, d_ref, nm_ref, nv_ref):
        g = jnp.where(pl.program_id(0) == c_ref[0], own_ref[...], oth_ref[...])
        g_ref[...] = g
        d_ref[...], nm_ref[...], nv_ref[...] = _adamw_math(w_ref[...], g, m_ref[...], v_ref[...])

    sh = jax.ShapeDtypeStruct((R, C), F32)
    return _pcall(
        body, name=name,
        grid_spec=pltpu.PrefetchScalarGridSpec(
            num_scalar_prefetch=1, grid=(2, nb), in_specs=[full, half, half, full, full], out_specs=[full] * 4),
        out_shape=[sh] * 4, compiler_params=_params(("parallel", "parallel")))(c_arr, w, own, other, m, v)


def _ada_grad_adamw(c_t, dmod, w, m, v):
    R, C = w.shape
    br = _ew_rows(R, C)
    spec = pl.BlockSpec((br, C), lambda i: (i, 0))

    def body(c_ref, dm_ref, w_ref, m_ref, v_ref, g_ref, d_ref, nm_ref, nv_ref):
        cv = c_ref[...]
        sc = cv * _sig(cv)
        g = sc[:, 0:1] * dm_ref[0:1, :]
        for b in range(1, N_DEV):
            g = g + sc[:, b:b + 1] * dm_ref[b:b + 1, :]
        g_ref[...] = g
        d_ref[...], nm_ref[...], nv_ref[...] = _adamw_math(w_ref[...], g, m_ref[...], v_ref[...])

    sh = jax.ShapeDtypeStruct((R, C), F32)
    return _pcall(
        body, name="ada_grad_adamw", grid=(R // br,),
        in_specs=[pl.BlockSpec((br, N_DEV), lambda i: (i, 0)), pl.BlockSpec((N_DEV, C), lambda i: (0, 0)), spec, spec, spec],
        out_specs=[spec] * 4, out_shape=[sh] * 4, compiler_params=_params(("parallel",)))(c_t, dmod, w, m, v)


SMALL_ROWS = 16


def _small_sum(small_all, lb_logits):
    def body(s_ref, lbl_ref, o_ref):
        acc = s_ref[0:SMALL_ROWS, :]
        for d in range(1, N_DEV):
            acc = acc + s_ref[d * SMALL_ROWS:(d + 1) * SMALL_ROWS, :]
        o_ref[...] = acc
        z = lbl_ref[...]
        e = jnp.exp(z - jnp.max(z, axis=0, keepdims=True))
        p0 = e[0:1, :] / (e[0:1, :] + e[1:2, :])
        dz = acc[8:9, 0:A_WIDTH] * p0 * (1.0 - p0)
        o_ref[8:9, 0:A_WIDTH] = dz
        o_ref[10:11, 0:A_WIDTH] = -dz

    return _pcall(body, name="small_sum", out_shape=jax.ShapeDtypeStruct((SMALL_ROWS, D_MODEL), F32),
                  in_specs=[pl.BlockSpec(memory_space=pltpu.VMEM)] * 2, out_specs=pl.BlockSpec(memory_space=pltpu.VMEM),
                  compiler_params=_params())(small_all, lb_logits)


RELATIONS = ((1, 0), (0, 1), (1, 1))
ANY = pl.BlockSpec(memory_space=pl.ANY)


def _place():
    x, y, c = lax.axis_index("x"), lax.axis_index("y"), lax.axis_index("c")
    return x, y, c


def _allgather_small(x_shard, name):
    m_per, n = x_shard.shape

    def body(x_ref, out_ref, send_sems, recv_sems, local_sem):
        x, y, c = _place()
        me, sibling = (x, y, c), (x, y, 1 - c)
        chips = [(1 - x, y), (x, 1 - y), (1 - x, 1 - y)]

        def rows(px, py, pc):
            return out_ref.at[pl.ds((4 * px + 2 * py + pc) * m_per, m_per), :]

        def copy(k, block, to, src=None):
            return pltpu.make_async_remote_copy(
                src_ref=rows(*block) if src is None else src, dst_ref=rows(*block),
                send_sem=send_sems.at[k], recv_sem=recv_sems.at[k], device_id=to, device_id_type=MESH)

        mine = pltpu.make_async_copy(x_ref, rows(*me), local_sem)
        mine.start()
        first = [copy(0, me, sibling, src=x_ref)]
        first += [copy(1 + j, me, (*chip, c), src=x_ref) for j, chip in enumerate(chips)]
        for cp in first:
            cp.start()
        passed = [copy(4 + j, (*chip, c), sibling) for j, chip in enumerate(chips)]
        for j, chip in enumerate(chips):
            copy(1 + j, (*chip, c), me).wait_recv()
            passed[j].start()
        copy(0, sibling, me).wait_recv()
        for j, chip in enumerate(chips):
            copy(4 + j, (*chip, 1 - c), me).wait_recv()
        for cp in first + passed:
            cp.wait_send()
        mine.wait()

    return _pcall(
        body, name=name, out_shape=jax.ShapeDtypeStruct((N_DEV * m_per, n), x_shard.dtype),
        in_specs=[pl.BlockSpec(memory_space=pltpu.VMEM)], out_specs=pl.BlockSpec(memory_space=pltpu.VMEM),
        scratch_shapes=[pltpu.SemaphoreType.DMA((7,)), pltpu.SemaphoreType.DMA((7,)), pltpu.SemaphoreType.DMA],
        compiler_params=_params(),
    )(x_shard)


W_SHAPES = ((D_MODEL, IN_WIDTH, True), (A_WIDTH, D_MODEL, True), (B_WIDTH, D_MODEL, True),
            (D_MODEL, D_MODEL, False), (D_MODEL, MLP_HIDDEN, True), (MLP_HIDDEN, D_MODEL, False))
N_W = len(W_SHAPES)


def _shard_shape(w):
    R, C, by_col = W_SHAPES[w]
    return (R, C // N_CHIPS) if by_col else (R // N_CHIPS, C)


def _half_shape(w):
    sr, sc = _shard_shape(w)
    return sr // 2, sc


def _region(full_ref, w, chip, half, quarter=None):
    sr, sc = _shard_shape(w)
    by_col = W_SHAPES[w][2]
    r0, c0 = (0, chip * sc) if by_col else (chip * sr, 0)
    r0, rows = r0 + half * (sr // 2), sr // 2
    if quarter is not None:
        r0, rows = r0 + quarter * (rows // 2), rows // 2
    return full_ref.at[pl.ds(r0, rows), pl.ds(c0, sc)]


def _on_device(fn):
    x, y, c = _place()
    me = 4 * x + 2 * y + c
    for d in range(N_DEV):
        @pl.when(me == d)
        def _(d=d):
            fn(x, y, c, d)


GATHER_COPIES = (
    (0, 0, None, "x"), (0, 0, None, "y"),
    (1, 2, 0, "y"), (1, 1, 1, "x"),
    (1, 2, None, "s"), (1, 1, None, "s"),
    (2, 3, 0, "s"), (2, 3, 1, "s"),
)
PEER_FLIP = {"x": 2, "y": 1, "s": 0}


def _gather_plan(partials, pass_at=(0.5, 0.75)):
    ws = sorted(partials)
    n_t = len(GATHER_COPIES)
    jobs = [(i, w, t) for i, w in enumerate(ws) for t in range(n_t)]

    def copy(pi, po, ps, x, y, c, d, i, w, t, landing):
        chip, dc = d >> 1, d & 1
        stage, flip, quarter, to = GATHER_COPIES[t]
        if landing:
            peer_chip = chip ^ PEER_FLIP[to]
            part = _region(po[i], w, peer_chip ^ flip, (1 - dc) if to == "s" else dc, quarter)
            src = part
        else:
            part = _region(po[i], w, chip ^ flip, dc, quarter)
            src = _region(pi[i], w, chip, dc, quarter) if flip == 0 else part
        target = {"x": (x ^ 1, y, c), "y": (x, y ^ 1, c), "s": (x, y, 1 - c)}[to]
        return pltpu.make_async_remote_copy(
            src_ref=src, dst_ref=part, send_sem=ps[0].at[i * n_t + t], recv_sem=ps[1].at[i * n_t + t],
            device_id=target, device_id_type=MESH)

    def stage(k):
        def run(pi, po, ps):
            def on(x, y, c, d):
                for i, w, t in jobs:
                    if k >= 1 and GATHER_COPIES[t][0] == k - 1 and GATHER_COPIES[t][3] != "s":
                        copy(pi, po, ps, x, y, c, d, i, w, t, True).wait_recv()
                for i, w, t in jobs:
                    if GATHER_COPIES[t][0] == k:
                        copy(pi, po, ps, x, y, c, d, i, w, t, False).start()
                if k == 3:
                    for i, w, t in jobs:
                        if GATHER_COPIES[t][3] == "s":
                            copy(pi, po, ps, x, y, c, d, i, w, t, True).wait_recv()
                    for i, w, t in jobs:
                        copy(pi, po, ps, x, y, c, d, i, w, t, False).wait_send()
            _on_device(on)
        return run

    return _Plan([partials[w] for w in ws], [jax.ShapeDtypeStruct(W_SHAPES[w][:2], BF16) for w in ws],
                 [pltpu.SemaphoreType.DMA((n_t * len(ws),)) for _ in range(2)], [stage(k) for k in range(4)],
                 {i: i for i in range(len(ws))}, mid_at=tuple(pass_at))


def _grad_view(g, w):
    R, C, by_col = W_SHAPES[w]
    return g.reshape(1, 2, R // 2, C) if by_col else g.reshape(N_CHIPS, 2, R // N_CHIPS // 2, C)


def _start_wait_plan(ins, outs, n_copies, copies):
    def start(pi, po, ps):
        for cp in copies(pi, po, ps):
            cp.start()

    def finish(pi, po, ps):
        for cp in copies(pi, po, ps):
            cp.wait()

    return _Plan(ins, outs, [pltpu.SemaphoreType.DMA((n_copies,)), pltpu.SemaphoreType.DMA((n_copies,))], [start, finish])


def _sibling_exchange_plan(g4s):
    pieces = [(i, p) for i, g in enumerate(g4s) for p in range(g.shape[0])]

    def copies(pi, po, ps):
        x, y, c = _place()
        return [pltpu.make_async_remote_copy(
            src_ref=pi[i].at[p, 1 - c], dst_ref=po[i].at[p], send_sem=ps[0].at[n], recv_sem=ps[1].at[n],
            device_id=(x, y, 1 - c), device_id_type=MESH) for n, (i, p) in enumerate(pieces)]

    return _start_wait_plan(list(g4s), [jax.ShapeDtypeStruct((g.shape[0],) + g.shape[2:], F32) for g in g4s],
                            len(pieces), copies)


def _pair_sum(g4, other, c_arr, name):
    P, _, hr, C = g4.shape
    br = _ew_rows(hr, C)

    def body(c_ref, g_ref, o_ref, p_ref):
        p_ref[...] = (g_ref[...] + o_ref[...]).astype(BF16)

    return _pcall(
        body, name=name,
        grid_spec=pltpu.PrefetchScalarGridSpec(
            num_scalar_prefetch=1, grid=(P, hr // br),
            in_specs=[pl.BlockSpec((None, None, br, C), lambda p, i, c_ref: (p, c_ref[0], i, 0)),
                      pl.BlockSpec((None, br, C), lambda p, i, c_ref: (p, i, 0))],
            out_specs=pl.BlockSpec((None, br, C), lambda p, i, c_ref: (p, i, 0))),
        out_shape=jax.ShapeDtypeStruct((P, hr, C), BF16),
        compiler_params=_params(("parallel", "parallel")),
    )(c_arr, g4, other)


def _pair_part(p_ref, w, chip):
    sr, sc = _shard_shape(w)
    return p_ref.at[0, :, pl.ds(chip * sc, sc)] if W_SHAPES[w][2] else p_ref.at[chip]


def _chip_exchange_plan(pairs):
    ws = sorted(pairs)

    def stage(wait):
        def run(pi, po, ps):
            def on(x, y, c, d):
                for i, w in enumerate(ws):
                    for k, (rx, ry) in enumerate(RELATIONS):
                        cp = pltpu.make_async_remote_copy(
                            src_ref=_pair_part(pi[i], w, (d >> 1) ^ (2 * rx + ry)), dst_ref=po[i].at[k],
                            send_sem=ps[0].at[i * 3 + k], recv_sem=ps[1].at[i * 3 + k],
                            device_id=(x ^ rx, y ^ ry, c), device_id_type=MESH)
                        if wait:
                            cp.wait()
                        else:
                            cp.start()
            _on_device(on)
        return run

    return _Plan([pairs[w] for w in ws], [jax.ShapeDtypeStruct((3,) + _half_shape(w), BF16) for w in ws],
                 [pltpu.SemaphoreType.DMA((3 * len(ws),)), pltpu.SemaphoreType.DMA((3 * len(ws),))],
                 [stage(False), stage(True)])


def _sum_slots(pair, slots, w, chip_arr, name):
    _, hr, C = slots.shape
    br = _ew_rows(hr, C)
    own_map = (lambda i, chip: (0, i, chip[0])) if W_SHAPES[w][2] else (lambda i, chip: (chip[0], i, 0))

    def body(chip_ref, p_ref, s_ref, o_ref):
        acc = p_ref[...].astype(F32)
        for k in range(3):
            acc = acc + s_ref[k].astype(F32)
        o_ref[...] = acc

    return _pcall(
        body, name=name,
        grid_spec=pltpu.PrefetchScalarGridSpec(
            num_scalar_prefetch=1, grid=(hr // br,),
            in_specs=[pl.BlockSpec((None, br, C), own_map), pl.BlockSpec((3, br, C), lambda i, chip: (0, i, 0))],
            out_specs=pl.BlockSpec((br, C), lambda i, chip: (i, 0))),
        out_shape=jax.ShapeDtypeStruct((hr, C), F32), compiler_params=_params(("parallel",)),
    )(chip_arr, pair, slots)


def _sibling_share_plan(halves):
    def copies(pi, po, ps):
        x, y, c = _place()
        return [pltpu.make_async_remote_copy(
            src_ref=pi[i], dst_ref=po[i], send_sem=ps[0].at[i], recv_sem=ps[1].at[i],
            device_id=(x, y, 1 - c), device_id_type=MESH) for i in range(len(halves))]

    return _start_wait_plan(list(halves), [jax.ShapeDtypeStruct(h.shape, F32) for h in halves], len(halves), copies)


def _pad_lanes(v, width=D_MODEL):
    return jnp.pad(v, ((0, 0), (0, width - v.shape[1])))


def _pack_small(b_ada, norm1, norm2, lb, o_gain, q_gain, k_gain, sinks):
    rows = [b_ada.reshape(N_MOD, D_MODEL), norm1, norm2, jnp.concatenate([lb[0:1], o_gain], axis=1),
            _pad_lanes(jnp.concatenate([q_gain, k_gain, sinks], axis=1)), _pad_lanes(lb[1:2]),
            jnp.zeros((SMALL_ROWS - 11, D_MODEL), F32)]
    return jnp.concatenate(rows, axis=0)


def _unpack_small(p):
    return (p[0:6].reshape(1, N_MOD * D_MODEL), p[6:7], p[7:8],
            jnp.concatenate([p[8:9, 0:A_WIDTH], p[10:11, 0:A_WIDTH]], axis=0), p[8:9, A_WIDTH:],
            p[9:10, 0:64], p[9:10, 64:128], p[9:10, 128:144])


def kernel(x, c, w_ada, b_ada, norm1_gain, w_in, lb_logits, hgrn_o_gain, q_norm_gain, k_norm_gain, sinks, w_branch_a, w_branch_b, w_out, norm2_gain, w_mlp_in, w_mlp_out, loss_target, m_w_ada, m_b_ada, m_norm1_gain, m_w_in, m_lb_logits, m_hgrn_o_gain, m_q_norm_gain, m_k_norm_gain, m_sinks, m_w_branch_a, m_w_branch_b, m_w_out, m_norm2_gain, m_w_mlp_in, m_w_mlp_out, v_w_ada, v_b_ada, v_norm1_gain, v_w_in, v_lb_logits, v_hgrn_o_gain, v_q_norm_gain, v_k_norm_gain, v_sinks, v_w_branch_a, v_w_branch_b, v_w_out, v_norm2_gain, v_w_mlp_in, v_w_mlp_out):
    xi, yi, ci = _place()
    chip = 2 * xi + yi
    me = 4 * xi + 2 * yi + ci
    ada_cols = w_ada.shape[2]

    c_all = _allgather_small(jnp.broadcast_to(c, (8, D_MODEL)), "gather_c").reshape(N_DEV, 8, D_MODEL)[:, 0]
    b_cols = lax.dynamic_slice(b_ada, (0, chip * ada_cols), (1, ada_cols))
    mod_part = _ada_fwd(c_all, w_ada[0], b_cols)
    mod_all = _allgather_small(mod_part, "gather_mod").reshape(N_CHIPS, 2, N_DEV, ada_cols)[:, 0]
    mod_mine = lax.dynamic_index_in_dim(mod_all, me, axis=1, keepdims=False).reshape(N_MOD, D_MODEL)
    mod8 = jnp.concatenate([mod_mine, jnp.zeros((2, D_MODEL), F32)], axis=0)

    shards = (w_in[0], w_branch_a[0], w_branch_b[0], w_out[0], w_mlp_in[0], w_mlp_out[0])
    chip_arr = chip.astype(jnp.int32).reshape(1)
    c_arr = ci.astype(jnp.int32).reshape(1)
    parts = [_cast_into_full(s, w, chip_arr, f"cast_w{w}") for w, s in enumerate(shards)]

    grad_x, halves, theirs, st = _local_step(x[0], loss_target[0], mod8, norm1_gain, norm2_gain, lb_logits, hgrn_o_gain,
                                             q_norm_gain, k_norm_gain, sinks, parts, c_arr, chip_arr)
    loss = lax.psum(0.5 * jnp.sum(st["loss"][0]) / D_MODEL, ("x", "y", "c"))
    moments = ((m_w_in, v_w_in), (m_w_branch_a, v_w_branch_a), (m_w_branch_b, v_w_branch_b), (m_w_out, v_w_out),
               (m_w_mlp_in, v_w_mlp_in), (m_w_mlp_out, v_w_mlp_out))
    big = [_adamw_halves(shards[w], halves[w], theirs[w], moments[w][0][0], moments[w][1][0], c_arr, f"adamw{w}")
           for w in range(N_W)]

    swa = st["swa"]
    small = jnp.concatenate([
        st["n1"][1:2], st["n1"][0:1], st["n2"][3:4], st["n2"][1:2], st["n2"][0:1], st["loss"][1:2],
        st["n1"][2:3], st["n2"][2:3], jnp.concatenate([st["d_lb"][0:1], st["d_og"][0:1]], axis=1),
        _pad_lanes(jnp.concatenate([swa[0:1, 0:64], swa[1:2, 0:64], swa[2:3, 0:16]], axis=1)),
        jnp.zeros((SMALL_ROWS - 10, D_MODEL), F32)], axis=0)
    small_all = _allgather_small(small, "gather_small")
    g_small = _small_sum(small_all, lb_logits)
    small_w = (b_ada, norm1_gain, norm2_gain, lb_logits, hgrn_o_gain, q_norm_gain, k_norm_gain, sinks)
    small_m = (m_b_ada, m_norm1_gain, m_norm2_gain, m_lb_logits, m_hgrn_o_gain, m_q_norm_gain, m_k_norm_gain, m_sinks)
    small_v = (v_b_ada, v_norm1_gain, v_norm2_gain, v_lb_logits, v_hgrn_o_gain, v_q_norm_gain, v_k_norm_gain, v_sinks)
    sm = [_unpack_small(t) for t in
          (g_small,) + tuple(_adamw(_pack_small(*small_w), g_small, _pack_small(*small_m), _pack_small(*small_v),
                                    "adamw_small"))]
    g_b, g_n1, g_n2, g_lb, g_og, g_qg, g_kg, g_sk = ([t[i] for t in sm] for i in range(8))

    dmod_all = small_all.reshape(N_DEV, SMALL_ROWS, D_MODEL)[:, 0:N_MOD].reshape(N_DEV, N_MOD * D_MODEL)
    dmod_cols = lax.dynamic_slice(dmod_all, (0, chip * ada_cols), (N_DEV, ada_cols))
    ada = _ada_grad_adamw(c_all.T, dmod_cols, w_ada[0], m_w_ada[0], v_w_ada[0])

    def ordered(k):
        lead = lambda a: a[None]
        return (lead(ada[k]), g_b[k], g_n1[k], lead(big[0][k]), g_lb[k], g_og[k], g_qg[k], g_kg[k], g_sk[k],
                lead(big[1][k]), lead(big[2][k]), lead(big[3][k]), g_n2[k], lead(big[4][k]), lead(big[5][k]))

    return (loss, grad_x[None]) + ordered(0) + ordered(1) + ordered(2) + ordered(3)
```

```python
import functools

import jax
import jax.numpy as jnp
from jax import lax
from jax.experimental import pallas as pl
from jax.experimental.pallas import tpu as pltpu

F32 = jnp.float32
BF16 = jnp.bfloat16
HIGHEST = lax.Precision.HIGHEST
MESH = pl.DeviceIdType.MESH

D_MODEL = 2048
A_WIDTH = 1024
A_HEADS = 8
A_HEAD_DIM = 128
A_CHUNK = 64
B_WIDTH = 1024
B_HEAD_DIM = 64
B_GROUP = 4
B_KV_HEADS = 4
B_KV_WIDTH = 256
BLOCK = 128
MLP_HIDDEN = 8192
IN_WIDTH = 9728
N_MOD = 6
EPS = 1e-6
N_CHIPS = 4
N_DEV = 8

OFF_QA, OFF_FA, OFF_IA, OFF_GA = 0, 1024, 2048, 3072
OFF_QB, OFF_KB, OFF_VB = 4096, 5120, 5376
OFF_GATE_A, OFF_GATE_B = 5632, 7680

ADAM_LR = 0.001
ADAM_B1 = 0.9
ADAM_B2 = 0.999
ADAM_EPS = 1e-08
ADAM_WD = 0.01
ADAM_STEP = 10

VMEM_LIMIT_V7X = 48 * 1024 * 1024
NEG_BIG = -1e30


def _params(sem=None, vmem=VMEM_LIMIT_V7X):
    return pltpu.CompilerParams(dimension_semantics=sem, vmem_limit_bytes=vmem)


class _Plan:
    def __init__(self, ins, outs, sems, stages, aliases=None, mid_at=()):
        self.ins, self.outs, self.sems, self.stages, self.aliases = ins, outs, sems, stages, aliases or {}
        self.mid_at = tuple(mid_at)
        assert len(self.mid_at) == len(stages) - 2


def _join(a, b):
    assert len(a.stages) == 2 and len(b.stages) == 2
    ni, no, ns = len(a.ins), len(a.outs), len(a.sems)

    def stage(k):
        def run(pi, po, ps):
            a.stages[k](pi[:ni], po[:no], ps[:ns])
            b.stages[k](pi[ni:], po[no:], ps[ns:])
        return run

    aliases = dict(a.aliases)
    aliases.update({ni + i: no + o for i, o in b.aliases.items()})
    return _Plan(a.ins + b.ins, a.outs + b.outs, a.sems + b.sems, [stage(0), stage(1)], aliases)


def _pcall(body, plan=None, **kw):
    if plan is None:
        return pl.pallas_call(body, **kw)
    grid = kw["grid"]
    single = not isinstance(kw["out_specs"], (list, tuple))
    in_specs = list(kw["in_specs"])
    out_specs = [kw["out_specs"]] if single else list(kw["out_specs"])
    out_shape = [kw["out_shape"]] if single else list(kw["out_shape"])
    scratch = list(kw.get("scratch_shapes", ()))
    n_in, n_out, n_scr = len(in_specs), len(out_specs), len(scratch)
    n_pi, n_po = len(plan.ins), len(plan.outs)
    total = 1
    for g in grid:
        total *= g
    n_st = len(plan.stages)

    def wrapped(*refs):
        o0 = n_in + n_pi
        s0 = o0 + n_out + n_po
        pi, po, ps = refs[n_in:o0], refs[o0 + n_out:s0], refs[s0 + n_scr:]
        lin = 0
        for d, g in enumerate(grid):
            lin = lin * g + pl.program_id(d)
        for si, frac in enumerate((0.0,) + plan.mid_at):
            @pl.when(lin == int(frac * (total - 1)))
            def _(si=si):
                plan.stages[si](pi, po, ps)
        body(*refs[:n_in], *refs[o0:o0 + n_out], *refs[s0:s0 + n_scr])

        @pl.when(lin == total - 1)
        def _():
            plan.stages[-1](pi, po, ps)

    any_spec = pl.BlockSpec(memory_space=pl.ANY)
    call = pl.pallas_call(
        wrapped, name=kw["name"], grid=grid, in_specs=in_specs + [any_spec] * n_pi,
        out_specs=out_specs + [any_spec] * n_po, out_shape=out_shape + list(plan.outs),
        scratch_shapes=scratch + list(plan.sems),
        input_output_aliases={n_in + i: n_out + o for i, o in plan.aliases.items()},
        compiler_params=_params(("arbitrary",) * len(grid)))

    def run(*args):
        res = call(*args, *plan.ins)
        outs = list(res[:n_out])
        return (outs[0] if single else outs), list(res[n_out:])

    return run


def _run_plan(plan, name):
    return _pcall(lambda: None, plan=plan, name=name, grid=(1,), in_specs=[], out_specs=[], out_shape=[])()[1]


def _sig(x):
    return 1.0 / (1.0 + jnp.exp(-x))


def _nn(a, b):
    return lax.dot_general(a.astype(BF16), b.astype(BF16), (((1,), (0,)), ((), ())), preferred_element_type=F32)


def _nt(a, b):
    return lax.dot_general(a.astype(BF16), b.astype(BF16), (((1,), (1,)), ((), ())), preferred_element_type=F32)


def _tn(a, b):
    return lax.dot_general(a.astype(BF16), b.astype(BF16), (((0,), (0,)), ((), ())), preferred_element_type=F32)


def _mm(a, b, *, name, ta=False, tb=False, bm=1024, bn=1024, bk=2048, out_dtypes=(F32,), epi=None, extras=(),
        extra_cols=None, plan=None):
    if ta:
        K, M = a.shape
        bk = K
    else:
        M, K = a.shape
    if tb:
        N, K2 = b.shape
    else:
        K2, N = b.shape
    bm, bn, bk = min(bm, M), min(bn, N), min(bk, K)
    assert K == K2 and M % bm == 0 and N % bn == 0 and K % bk == 0, (name, a.shape, b.shape)
    nk = K // bk
    a_spec = pl.BlockSpec((bk, bm), lambda i, j, k: (k, i)) if ta else pl.BlockSpec((bm, bk), lambda i, j, k: (i, k))
    b_spec = pl.BlockSpec((bn, bk), lambda i, j, k: (j, k)) if tb else pl.BlockSpec((bk, bn), lambda i, j, k: (k, j))
    t_spec = pl.BlockSpec((bm, bn), lambda i, j, k: (i, j))
    extra_cols = extra_cols or (0,) * len(extras)
    e_specs = [pl.BlockSpec((bm, bn), lambda i, j, k, off=off: (i, off + j)) for off in extra_cols]
    dims = (((1,), (1 if tb else 0,)), ((), ()))
    n_e, n_o = len(extras), len(out_dtypes)

    def body(*refs):
        a_ref, b_ref = refs[0], refs[1]
        e_refs = refs[2:2 + n_e]
        o_refs = refs[2 + n_e:2 + n_e + n_o]

        def finish(acc):
            outs = (acc,) if epi is None else epi(acc, *[e[...] for e in e_refs])
            for o_ref, o in zip(o_refs, outs):
                o_ref[...] = o.astype(o_ref.dtype)

        if ta:
            at_ref = refs[-1]

            @pl.when(pl.program_id(1) == 0)
            def _():
                at_ref[...] = a_ref[...].T

            lhs = at_ref[...]
        else:
            lhs = a_ref[...].astype(BF16)
        part = lax.dot_general(lhs, b_ref[...].astype(BF16), dims, preferred_element_type=F32)
        if nk == 1:
            finish(part)
        else:
            acc_ref = refs[-1]
            k = pl.program_id(2)

            @pl.when(k == 0)
            def _():
                acc_ref[...] = part

            @pl.when(k > 0)
            def _():
                acc_ref[...] += part

            @pl.when(k == nk - 1)
            def _():
                finish(acc_ref[...])

    if ta:
        assert a.dtype == BF16 and nk == 1
        scratch = [pltpu.VMEM((bm, bk), BF16)]
    else:
        scratch = [pltpu.VMEM((bm, bn), F32)] if nk > 1 else []
    out = _pcall(
        body, plan=plan, name=name, grid=(M // bm, N // bn, nk),
        in_specs=[a_spec, b_spec] + e_specs,
        out_specs=[t_spec] * n_o,
        out_shape=[jax.ShapeDtypeStruct((M, N), dt) for dt in out_dtypes],
        scratch_shapes=scratch,
        compiler_params=_params(("parallel", "arbitrary", "arbitrary")),
    )(a, b, *extras)
    if plan is not None:
        return (out[0][0] if n_o == 1 else out[0]), out[1]
    return out[0] if n_o == 1 else out


def _ada_fwd(c_all, w_ada, b_cols):
    n = w_ada.shape[1]
    bn = 512

    def body(c_ref, w_ref, b_ref, o_ref):
        cv = c_ref[...]
        sc = cv * _sig(cv)
        o_ref[...] = jnp.dot(sc, w_ref[...], precision=HIGHEST, preferred_element_type=F32) + b_ref[...]

    return _pcall(
        body, name="ada_fwd", grid=(n // bn,),
        in_specs=[pl.BlockSpec((N_DEV, D_MODEL), lambda j: (0, 0)), pl.BlockSpec((D_MODEL, bn), lambda j: (0, j)),
                  pl.BlockSpec((1, bn), lambda j: (0, j))],
        out_specs=pl.BlockSpec((N_DEV, bn), lambda j: (0, j)),
        out_shape=jax.ShapeDtypeStruct((N_DEV, n), F32),
        compiler_params=_params(("parallel",)),
    )(c_all, w_ada, b_cols)


ROWS_EW = 256


def _rms_fwd_math(x, gain, scale, shift):
    rstd = lax.rsqrt(jnp.mean(x * x, axis=-1, keepdims=True) + EPS)
    xhat = x * rstd
    n = xhat * gain
    return n * (1.0 + scale) + shift, xhat, n, rstd


def _rms_bwd_math(dh, xhat, n, rstd, gain, scale):
    dn = dh * (1.0 + scale)
    dxhat = dn * gain
    dx = rstd * (dxhat - xhat * jnp.mean(dxhat * xhat, axis=-1, keepdims=True))
    d_scale = jnp.sum(dh * n, axis=0, keepdims=True)
    d_shift = jnp.sum(dh, axis=0, keepdims=True)
    d_gain = jnp.sum(dn * xhat, axis=0, keepdims=True)
    return dx, d_scale, d_shift, d_gain


def _row_spec(w=D_MODEL, br=ROWS_EW):
    return pl.BlockSpec((br, w), lambda i: (i, 0))


def _vec_spec(r=8, w=D_MODEL):
    return pl.BlockSpec((r, w), lambda i: (0, 0))


def _norm1_fwd(x, gain, mod8, plan=None):
    T = x.shape[0]

    def body(x_ref, g_ref, m_ref, h_ref):
        h, _, _, _ = _rms_fwd_math(x_ref[...], g_ref[...], m_ref[1:2, :], m_ref[0:1, :])
        h_ref[...] = h.astype(BF16)

    return _pcall(
        body, plan=plan, name="norm1_fwd", grid=(T // ROWS_EW,),
        in_specs=[_row_spec(), _vec_spec(1), _vec_spec()],
        out_specs=_row_spec(), out_shape=jax.ShapeDtypeStruct((T, D_MODEL), BF16),
        compiler_params=_params(("parallel",)),
    )(x, gain, mod8)


def _res_norm2_fwd(x, mo, gain, mod8):
    T = x.shape[0]

    def body(x_ref, mo_ref, g_ref, m_ref, x1_ref, h_ref):
        x1 = x_ref[...] + m_ref[2:3, :] * mo_ref[...]
        x1_ref[...] = x1
        h, _, _, _ = _rms_fwd_math(x1, g_ref[...], m_ref[4:5, :], m_ref[3:4, :])
        h_ref[...] = h.astype(BF16)

    return _pcall(
        body, name="res_norm2_fwd", grid=(T // ROWS_EW,),
        in_specs=[_row_spec(), _row_spec(), _vec_spec(1), _vec_spec()],
        out_specs=[_row_spec(), _row_spec()],
        out_shape=[jax.ShapeDtypeStruct((T, D_MODEL), F32), jax.ShapeDtypeStruct((T, D_MODEL), BF16)],
        compiler_params=_params(("parallel",)),
    )(x, mo, gain, mod8)


def _loss_bwd(x1, mlp, target, mod8):
    T = x1.shape[0]

    def body(x1_ref, mlp_ref, t_ref, m_ref, dy_ref, dmlp_ref, st_ref):
        i = pl.program_id(0)
        gate = m_ref[5:6, :]
        mlp_v = mlp_ref[...]
        err = x1_ref[...] + gate * mlp_v - t_ref[...]
        dy = err * (1.0 / D_MODEL)
        dy_ref[...] = dy
        dmlp_ref[...] = (dy * gate).astype(BF16)

        @pl.when(i == 0)
        def _():
            st_ref[...] = jnp.zeros_like(st_ref)

        st_ref[0:1, :] += jnp.sum(err * err, axis=0, keepdims=True)
        st_ref[1:2, :] += jnp.sum(dy * mlp_v, axis=0, keepdims=True)

    return _pcall(
        body, name="loss_bwd", grid=(T // ROWS_EW,),
        in_specs=[_row_spec(), _row_spec(), _row_spec(), _vec_spec()],
        out_specs=[_row_spec(), _row_spec(), _vec_spec()],
        out_shape=[jax.ShapeDtypeStruct((T, D_MODEL), F32), jax.ShapeDtypeStruct((T, D_MODEL), BF16),
                   jax.ShapeDtypeStruct((8, D_MODEL), F32)],
        compiler_params=_params(("arbitrary",)),
    )(x1, mlp, target, mod8)


def _norm2_bwd(dh2, x1, dy, mo, gain, mod8):
    T = x1.shape[0]

    def body(dh_ref, x1_ref, dy_ref, mo_ref, g_ref, m_ref, dx1_ref, dmo_ref, st_ref):
        i = pl.program_id(0)
        gain_v, scale = g_ref[...], m_ref[4:5, :]
        _, xhat, n, rstd = _rms_fwd_math(x1_ref[...], gain_v, scale, m_ref[3:4, :])
        dx, d_scale, d_shift, d_gain = _rms_bwd_math(dh_ref[...], xhat, n, rstd, gain_v, scale)
        dx1 = dy_ref[...] + dx
        dx1_ref[...] = dx1
        dmo_ref[...] = (dx1 * m_ref[2:3, :]).astype(BF16)

        @pl.when(i == 0)
        def _():
            st_ref[...] = jnp.zeros_like(st_ref)

        st_ref[0:1, :] += d_scale
        st_ref[1:2, :] += d_shift
        st_ref[2:3, :] += d_gain
        st_ref[3:4, :] += jnp.sum(dx1 * mo_ref[...], axis=0, keepdims=True)

    return _pcall(
        body, name="norm2_bwd", grid=(T // ROWS_EW,),
        in_specs=[_row_spec(), _row_spec(), _row_spec(), _row_spec(), _vec_spec(1), _vec_spec()],
        out_specs=[_row_spec(), _row_spec(), _vec_spec()],
        out_shape=[jax.ShapeDtypeStruct((T, D_MODEL), F32), jax.ShapeDtypeStruct((T, D_MODEL), BF16),
                   jax.ShapeDtypeStruct((8, D_MODEL), F32)],
        compiler_params=_params(("arbitrary",)),
    )(dh2, x1, dy, mo, gain, mod8)


def _norm1_bwd(dh, x, dx1, gain, mod8):
    T = x.shape[0]

    def body(dh_ref, x_ref, dx1_ref, g_ref, m_ref, dx_ref, st_ref):
        i = pl.program_id(0)
        gain_v, scale = g_ref[...], m_ref[1:2, :]
        _, xhat, n, rstd = _rms_fwd_math(x_ref[...], gain_v, scale, m_ref[0:1, :])
        dx, d_scale, d_shift, d_gain = _rms_bwd_math(dh_ref[...], xhat, n, rstd, gain_v, scale)
        dx_ref[...] = dx1_ref[...] + dx

        @pl.when(i == 0)
        def _():
            st_ref[...] = jnp.zeros_like(st_ref)

        st_ref[0:1, :] += d_scale
        st_ref[1:2, :] += d_shift
        st_ref[2:3, :] += d_gain

    return _pcall(
        body, name="norm1_bwd", grid=(T // ROWS_EW,),
        in_specs=[_row_spec(), _row_spec(), _row_spec(), _vec_spec(1), _vec_spec()],
        out_specs=[_row_spec(), _vec_spec()],
        out_shape=[jax.ShapeDtypeStruct((T, D_MODEL), F32), jax.ShapeDtypeStruct((8, D_MODEL), F32)],
        compiler_params=_params(("arbitrary",)),
    )(dh, x, dx1, gain, mod8)


MERGE_BC = 512


def _hgrn_rows(T):
    return 512 if T >= 1024 else 128


def _lower_bound(lbl):
    e = jnp.exp(lbl - jnp.max(lbl, axis=0, keepdims=True))
    return e[0:1, :] / (e[0:1, :] + e[1:2, :])


def _chunk_sum_matrix(rows, backward):
    shift = A_CHUNK.bit_length() - 1
    r = lax.broadcasted_iota(jnp.int32, (rows, rows), 0)
    c = lax.broadcasted_iota(jnp.int32, (rows, rows), 1)
    same = jnp.right_shift(r, shift) == jnp.right_shift(c, shift)
    return (same & ((r <= c) if backward else (r >= c))).astype(BF16)


def _chunk_sums(m, x):
    n = x.shape[1]
    hi = x.astype(BF16)
    rest = x - hi.astype(F32)
    mid = rest.astype(BF16)
    lo = (rest - mid.astype(F32)).astype(BF16)
    y = jnp.dot(m, jnp.concatenate([hi, mid, lo], axis=1), preferred_element_type=F32)
    return y[:, 0:n] + y[:, n:2 * n] + y[:, 2 * n:3 * n]


def _hgrn_block_pre(q, fl, lb, m_fwd):
    sg = _sig(fl)
    f = lb + (1.0 - lb) * sg
    sq = _sig(q)
    return dict(sg=sg, f=f, k=1.0 - f, sq=sq, qf=q * sq, b=_chunk_sums(m_fwd, jnp.log(f)))


def _hgrn_chunk_local(pre, r):
    C = A_CHUNK
    qf, k, b = pre["qf"][r], pre["k"][r], pre["b"][r]
    causal = lax.broadcasted_iota(jnp.int32, (C, C), 0) >= lax.broadcasted_iota(jnp.int32, (C, C), 1)
    bm = b[C // 2 - 1:C // 2, :]
    bl = b[C - 1:C, :]
    e_q, e_k = jnp.exp(b - bm), jnp.exp(bm - b)
    e_b, e_l = jnp.exp(b), jnp.exp(bl - b)
    qd, kd = qf * e_q, k * e_k
    qe, ke = qf * e_b, k * e_l
    att = jnp.where(causal, _nt(qd, kd), 0.0)
    return dict(causal=causal, e_q=e_q, e_k=e_k, e_b=e_b, e_l=e_l, qd=qd, kd=kd, qe=qe, ke=ke, att=att, dec=jnp.exp(bl))


def _hgrn_chunk_fwd(pre, r, v, st):
    c = _hgrn_chunk_local(pre, r)
    c["o"] = _nn(c["att"], v) + _nt(c["qe"], st)
    return c


def _lockstep(gens):
    out = [None] * len(gens)
    live = list(enumerate(gens))
    while live:
        still = []
        for i, g in live:
            try:
                next(g)
                still.append((i, g))
            except StopIteration as done:
                out[i] = done.value
        live = still
    return out


HGRN_HEADS_PER_STEP = 4


def _hgrn_fwd(proj, lb_logits, o_gain, plan=None):
    T = proj.shape[0]
    BR = _hgrn_rows(T)
    cps = BR // A_CHUNK
    K, NH = A_HEAD_DIM, HGRN_HEADS_PER_STEP
    W = NH * K

    def col(off):
        return pl.BlockSpec((BR, W), lambda h, cb: (cb, off // W + h))

    def body(q_ref, f_ref, i_ref, g_ref, lbl_ref, og_ref, o_ref, s_ref, st):
        @pl.when(pl.program_id(1) == 0)
        def _():
            st[...] = jnp.zeros_like(st)

        lb_all = _lower_bound(lbl_ref[...])
        m_fwd = _chunk_sum_matrix(BR, False)
        pre = [_hgrn_block_pre(q_ref[:, n * K:(n + 1) * K], f_ref[:, n * K:(n + 1) * K], lb_all[:, n * K:(n + 1) * K], m_fwd)
               for n in range(NH)]
        def local(n, ci):
            r, hs = slice(ci * A_CHUNK, (ci + 1) * A_CHUNK), slice(n * K, (n + 1) * K)
            v = i_ref[r, hs]
            c = _hgrn_chunk_local(pre[n], r)
            yield
            return dict(o=_nn(c["att"], v), ds=_tn(v, c["ke"]), qe=c["qe"], dec=c["dec"])

        def chain(n, loc):
            hs = slice(n * K, (n + 1) * K)
            state = st[n]
            for ci, p in enumerate(loc):
                r = slice(ci * A_CHUNK, (ci + 1) * A_CHUNK)
                s_ref[n, ci] = state
                o = p["o"] + _nt(p["qe"], state)
                state = state * p["dec"] + p["ds"]
                yield
                on = o * lax.rsqrt(jnp.mean(o * o, axis=-1, keepdims=True) + EPS)
                g = g_ref[r, hs]
                o_ref[r, hs] = (on * og_ref[:, hs] * (g * _sig(g))).astype(BF16)
            st[n] = state

        loc = _lockstep([local(n, ci) for n in range(NH) for ci in range(cps)])
        _lockstep([chain(n, loc[n * cps:(n + 1) * cps]) for n in range(NH)])

    return _pcall(
        body, plan=plan, name="hgrn_fwd", grid=(A_HEADS // NH, T // BR),
        in_specs=[col(OFF_QA), col(OFF_FA), col(OFF_IA), col(OFF_GA),
                  pl.BlockSpec((2, W), lambda h, cb: (0, h)), pl.BlockSpec((1, W), lambda h, cb: (0, h))],
        out_specs=[pl.BlockSpec((BR, W), lambda h, cb: (cb, h)),
                   pl.BlockSpec((NH, cps, K, K), lambda h, cb: (h, cb, 0, 0))],
        out_shape=[jax.ShapeDtypeStruct((T, A_WIDTH), BF16),
                   jax.ShapeDtypeStruct((A_HEADS, T // A_CHUNK, K, K), F32)],
        scratch_shapes=[pltpu.VMEM((NH, K, K), F32)],
        compiler_params=_params(("parallel", "arbitrary")),
    )(proj, proj, proj, proj, lb_logits, o_gain)


def _hgrn_bwd(proj, lb_logits, o_gain, states, do, plan=None):
    T = proj.shape[0]
    BR = _hgrn_rows(T)
    cps = BR // A_CHUNK
    ncb = T // BR
    K, C, NH = A_HEAD_DIM, A_CHUNK, HGRN_HEADS_PER_STEP
    W = NH * K

    def col(off):
        return pl.BlockSpec((BR, W), lambda h, cb: (ncb - 1 - cb, off // W + h))

    def body(q_ref, f_ref, i_ref, g_ref, lbl_ref, og_ref, s_ref, do_ref,
             dq_ref, df_ref, di_ref, dg_ref, dlb_ref, dog_ref, dst):
        @pl.when(pl.program_id(1) == 0)
        def _():
            dst[...] = jnp.zeros_like(dst)
            dlb_ref[...] = jnp.zeros_like(dlb_ref)
            dog_ref[...] = jnp.zeros_like(dog_ref)

        lb_all = _lower_bound(lbl_ref[...])
        row = lax.broadcasted_iota(jnp.int32, (C, K), 0)
        m_fwd, m_bwd = _chunk_sum_matrix(BR, False), _chunk_sum_matrix(BR, True)
        pre = [_hgrn_block_pre(q_ref[:, n * K:(n + 1) * K], f_ref[:, n * K:(n + 1) * K], lb_all[:, n * K:(n + 1) * K], m_fwd)
               for n in range(NH)]
        def local(n, ci):
            r, hs = slice(ci * C, (ci + 1) * C), slice(n * K, (n + 1) * K)
            gain = og_ref[:, hs]
            st = s_ref[n, ci]
            v = i_ref[r, hs]
            q = q_ref[r, hs]
            c = _hgrn_chunk_fwd(pre[n], r, v, st)
            yield
            o = c["o"]
            rn = lax.rsqrt(jnp.mean(o * o, axis=-1, keepdims=True) + EPS)
            on = o * rn
            g = g_ref[r, hs]
            sgg = _sig(g)
            dy = do_ref[r, hs]
            d_ong = dy * (g * sgg)
            dg_ref[r, hs] = (dy * (on * gain) * (sgg * (1.0 + g * (1.0 - sgg)))).astype(BF16)
            d_on = d_ong * gain
            d_o = rn * (d_on - on * jnp.mean(d_on * on, axis=-1, keepdims=True))
            datt = jnp.where(c["causal"], _nt(d_o, v), 0.0)
            dqe = _nn(d_o, st)
            yield
            dqd = _nn(datt, c["kd"])
            dkd = _tn(datt, c["qd"])
            dv = _tn(c["att"], d_o)
            ds = _tn(d_o, c["qe"])
            yield
            t_q, t_k = dqd * c["qd"], dkd * c["kd"]
            sq = pre[n]["sq"][r]
            dq_ref[r, hs] = ((dqd * c["e_q"] + dqe * c["e_b"]) * (sq * (1.0 + q * (1.0 - sq)))).astype(BF16)
            return dict(v=v, st=st, ke=c["ke"], e_l=c["e_l"], dec=c["dec"], dv=dv, ds=ds, dk=dkd * c["e_k"],
                        db=t_q - t_k + dqe * c["qe"], dbm=jnp.sum(t_k - t_q, axis=0, keepdims=True),
                        d_og=jnp.sum(d_ong * on, axis=0, keepdims=True))

        def chain(n, loc):
            hs = slice(n * K, (n + 1) * K)
            dst_next = dst[n]
            db_of, dk_of = [None] * cps, [None] * cps
            for ci in reversed(range(cps)):
                p = loc[ci]
                di_ref[ci * C:(ci + 1) * C, hs] = (p["dv"] + _nt(p["ke"], dst_next)).astype(BF16)
                dke = _nn(p["v"], dst_next)
                yield
                t_l = dke * p["ke"]
                dbl = jnp.sum(t_l, axis=0, keepdims=True) + jnp.sum(dst_next * p["st"], axis=0, keepdims=True) * p["dec"]
                db_of[ci] = p["db"] - t_l + jnp.where(row == C // 2 - 1, p["dbm"], 0.0) + jnp.where(row == C - 1, dbl, 0.0)
                dk_of[ci] = p["dk"] + dke * p["e_l"]
                dst_next = dst_next * p["dec"] + p["ds"]
            dst[n] = dst_next
            return db_of, dk_of

        loc = _lockstep([local(n, ci) for n in range(NH) for ci in range(cps)])
        loc = [loc[n * cps:(n + 1) * cps] for n in range(NH)]
        chains = _lockstep([chain(n, loc[n]) for n in range(NH)])
        for n in range(NH):
            hs = slice(n * K, (n + 1) * K)
            db_of, dk_of = chains[n]
            d_og = loc[n][0]["d_og"]
            for p in loc[n][1:]:
                d_og = d_og + p["d_og"]
            dog_ref[0:1, hs] += d_og
            lb, sg = lb_all[:, hs], pre[n]["sg"]
            dlf = _chunk_sums(m_bwd, jnp.concatenate(db_of, axis=0))
            df = dlf / pre[n]["f"] - jnp.concatenate(dk_of, axis=0)
            df_ref[:, hs] = (df * (1.0 - lb) * sg * (1.0 - sg)).astype(BF16)
            dlb_ref[0:1, hs] += jnp.sum(df * (1.0 - sg), axis=0, keepdims=True)

    ocol = pl.BlockSpec((BR, W), lambda h, cb: (ncb - 1 - cb, h))
    vec = pl.BlockSpec((8, W), lambda h, cb: (0, h))
    return _pcall(
        body, plan=plan, name="hgrn_bwd", grid=(A_HEADS // NH, ncb),
        in_specs=[col(OFF_QA), col(OFF_FA), col(OFF_IA), col(OFF_GA),
                  pl.BlockSpec((2, W), lambda h, cb: (0, h)), pl.BlockSpec((1, W), lambda h, cb: (0, h)),
                  pl.BlockSpec((NH, cps, K, K), lambda h, cb: (h, ncb - 1 - cb, 0, 0)),
                  pl.BlockSpec((BR, W), lambda h, cb: (ncb - 1 - cb, h))],
        out_specs=[ocol, ocol, ocol, ocol, vec, vec],
        out_shape=[jax.ShapeDtypeStruct((T, A_WIDTH), BF16)] * 4 + [jax.ShapeDtypeStruct((8, A_WIDTH), F32)] * 2,
        scratch_shapes=[pltpu.VMEM((NH, K, K), F32)],
        compiler_params=_params(("parallel", "arbitrary")),
    )(proj, proj, proj, proj, lb_logits, o_gain, states, do)


def _head_norm(x):
    r = lax.rsqrt(jnp.mean(x * x, axis=-1, keepdims=True) + EPS)
    return x * r, r


def _head_norm_bwd(dy, xn, r, gain):
    dxn = dy * gain
    return r * (dxn - xn * jnp.mean(dxn * xn, axis=-1, keepdims=True)), jnp.sum(dy * xn, axis=0, keepdims=True)


def _swa_mask(has_prev):
    rows = B_GROUP * BLOCK
    r = lax.broadcasted_iota(jnp.int32, (rows, 2 * BLOCK), 0) % BLOCK
    c = lax.broadcasted_iota(jnp.int32, (rows, 2 * BLOCK), 1)
    rel = r + BLOCK - c
    return (rel >= 0) & (rel < BLOCK) & ((c >= BLOCK) | has_prev)


def _swa_head_fwd(j, q_ref, kp_ref, kc_ref, vp_ref, vc_ref, qg, kg, sk_ref, mask):
    hs = slice(j * B_HEAD_DIM, (j + 1) * B_HEAD_DIM)
    kcat = jnp.concatenate([kp_ref[:, hs], kc_ref[:, hs]], axis=0)
    vcat = jnp.concatenate([vp_ref[:, hs], vc_ref[:, hs]], axis=0)
    qs = jnp.concatenate([q_ref[:, pl.ds((j * B_GROUP + g) * B_HEAD_DIM, B_HEAD_DIM)] for g in range(B_GROUP)], axis=0)
    kn, kr = _head_norm(kcat)
    qn, qr = _head_norm(qs)
    kh, qh = kn * kg, qn * qg
    yield
    s = jnp.where(mask, _nt(qh, kh) * (B_HEAD_DIM ** -0.5), NEG_BIG)
    yield
    sink = jnp.concatenate(
        [jnp.broadcast_to(sk_ref[0:1, pl.ds(j * B_GROUP + g, 1)], (BLOCK, 1)) for g in range(B_GROUP)], axis=0)
    m = jnp.maximum(jnp.max(s, axis=-1, keepdims=True), sink)
    p = jnp.exp(s - m)
    e_sink = jnp.exp(sink - m)
    inv = 1.0 / (jnp.sum(p, axis=-1, keepdims=True) + e_sink)
    prob = p * inv
    return dict(vcat=vcat, kn=kn, kr=kr, qn=qn, qr=qr, kh=kh, qh=qh, prob=prob, p_sink=e_sink * inv)


def _swa_in_specs(nb, last):
    def qi(n):
        return jnp.minimum(n, last)

    q = pl.BlockSpec((BLOCK, B_WIDTH), lambda n: (qi(n), OFF_QB // B_WIDTH))
    kc = pl.BlockSpec((BLOCK, B_KV_WIDTH), lambda n: (qi(n), OFF_KB // B_KV_WIDTH))
    kp = pl.BlockSpec((BLOCK, B_KV_WIDTH), lambda n: (jnp.maximum(qi(n) - 1, 0), OFF_KB // B_KV_WIDTH))
    vc = pl.BlockSpec((BLOCK, B_KV_WIDTH), lambda n: (qi(n), OFF_VB // B_KV_WIDTH))
    vp = pl.BlockSpec((BLOCK, B_KV_WIDTH), lambda n: (jnp.maximum(qi(n) - 1, 0), OFF_VB // B_KV_WIDTH))
    small = [pl.BlockSpec((1, B_HEAD_DIM), lambda n: (0, 0)), pl.BlockSpec((1, B_HEAD_DIM), lambda n: (0, 0)),
             pl.BlockSpec((1, B_GROUP * B_KV_HEADS), lambda n: (0, 0))]
    return [q, kp, kc, vp, vc] + small


def _swa_fwd(proj, q_gain, k_gain, sinks, plan=None):
    T = proj.shape[0]
    nb = T // BLOCK

    def body(q_ref, kp_ref, kc_ref, vp_ref, vc_ref, qg_ref, kg_ref, sk_ref, o_ref):
        mask = _swa_mask(pl.program_id(0) > 0)

        def head(j):
            c = yield from _swa_head_fwd(j, q_ref, kp_ref, kc_ref, vp_ref, vc_ref, qg_ref[...], kg_ref[...], sk_ref, mask)
            yield
            o = _nn(c["prob"], c["vcat"])
            yield
            for g in range(B_GROUP):
                o_ref[:, pl.ds((j * B_GROUP + g) * B_HEAD_DIM, B_HEAD_DIM)] = o[g * BLOCK:(g + 1) * BLOCK].astype(BF16)

        _lockstep([head(j) for j in range(B_KV_HEADS)])

    return _pcall(
        body, plan=plan, name="swa_fwd", grid=(nb,),
        in_specs=_swa_in_specs(nb, nb - 1),
        out_specs=pl.BlockSpec((BLOCK, B_WIDTH), lambda n: (n, 0)),
        out_shape=jax.ShapeDtypeStruct((T, B_WIDTH), BF16),
        compiler_params=_params(("parallel",)),
    )(proj, proj, proj, proj, proj, q_gain, k_gain, sinks)


def _swa_bwd(proj, q_gain, k_gain, sinks, do, plan=None):
    T = proj.shape[0]
    nb = T // BLOCK
    scale = B_HEAD_DIM ** -0.5

    def body(q_ref, kp_ref, kc_ref, vp_ref, vc_ref, qg_ref, kg_ref, sk_ref, do_ref,
             dq_ref, dkv_ref, sm_ref, ck, cv):
        n = pl.program_id(0)

        @pl.when(n == 0)
        def _():
            ck[...] = jnp.zeros_like(ck)
            cv[...] = jnp.zeros_like(cv)
            sm_ref[...] = jnp.zeros_like(sm_ref)

        @pl.when(n < nb)
        def _():
            mask = _swa_mask(n > 0)
            qg, kg = qg_ref[...], kg_ref[...]
            lane = lax.broadcasted_iota(jnp.int32, (1, BLOCK), 1)
            def head(j):
                hs = slice(j * B_HEAD_DIM, (j + 1) * B_HEAD_DIM)
                vs = slice(B_KV_WIDTH + j * B_HEAD_DIM, B_KV_WIDTH + (j + 1) * B_HEAD_DIM)
                c = yield from _swa_head_fwd(j, q_ref, kp_ref, kc_ref, vp_ref, vc_ref, qg, kg, sk_ref, mask)
                d_out = jnp.concatenate(
                    [do_ref[:, pl.ds((j * B_GROUP + g) * B_HEAD_DIM, B_HEAD_DIM)] for g in range(B_GROUP)], axis=0)
                prob = c["prob"]
                yield
                out = _nn(prob, c["vcat"])
                d_prob = _nt(d_out, c["vcat"])
                dv = _tn(prob, d_out)
                yield
                delta = jnp.sum(d_out * out, axis=-1, keepdims=True)
                ds = prob * (d_prob - delta)
                d_sink = -c["p_sink"] * delta
                yield
                dqh = _nn(ds, c["kh"]) * scale
                dkh = _tn(ds, c["qh"]) * scale
                yield
                dq, dqg = _head_norm_bwd(dqh, c["qn"], c["qr"], qg)
                dk, dkg = _head_norm_bwd(dkh, c["kn"], c["kr"], kg)
                d_sinks = jnp.zeros((1, BLOCK), F32)
                for g in range(B_GROUP):
                    dq_ref[:, pl.ds((j * B_GROUP + g) * B_HEAD_DIM, B_HEAD_DIM)] = dq[g * BLOCK:(g + 1) * BLOCK].astype(BF16)
                    tot = jnp.sum(d_sink[g * BLOCK:(g + 1) * BLOCK], axis=0, keepdims=True)
                    d_sinks = d_sinks + jnp.where(lane == j * B_GROUP + g, tot, 0.0)
                dkv_ref[:, hs] = (ck[:, hs] + dk[0:BLOCK]).astype(BF16)
                dkv_ref[:, vs] = (cv[:, hs] + dv[0:BLOCK]).astype(BF16)
                ck[:, hs] = dk[BLOCK:2 * BLOCK]
                cv[:, hs] = dv[BLOCK:2 * BLOCK]
                return dqg, dkg, d_sinks

            small = _lockstep([head(j) for j in range(B_KV_HEADS)])
            sm_ref[0:1, 0:B_HEAD_DIM] += small[0][0] + small[1][0] + small[2][0] + small[3][0]
            sm_ref[1:2, 0:B_HEAD_DIM] += small[0][1] + small[1][1] + small[2][1] + small[3][1]
            sm_ref[2:3, :] += small[0][2] + small[1][2] + small[2][2] + small[3][2]

        @pl.when(n == nb)
        def _():
            dkv_ref[:, 0:B_KV_WIDTH] = ck[...].astype(BF16)
            dkv_ref[:, B_KV_WIDTH:2 * B_KV_WIDTH] = cv[...].astype(BF16)

    return _pcall(
        body, plan=plan, name="swa_bwd", grid=(nb + 1,),
        in_specs=_swa_in_specs(nb, nb - 1) + [pl.BlockSpec((BLOCK, B_WIDTH), lambda n: (jnp.minimum(n, nb - 1), 0))],
        out_specs=[pl.BlockSpec((BLOCK, B_WIDTH), lambda n: (jnp.minimum(n, nb - 1), 0)),
                   pl.BlockSpec((BLOCK, 2 * B_KV_WIDTH), lambda n: (jnp.maximum(n - 1, 0), 0)),
                   pl.BlockSpec((8, BLOCK), lambda n: (0, 0))],
        out_shape=[jax.ShapeDtypeStruct((T, B_WIDTH), BF16), jax.ShapeDtypeStruct((T, 2 * B_KV_WIDTH), BF16),
                   jax.ShapeDtypeStruct((8, BLOCK), F32)],
        scratch_shapes=[pltpu.VMEM((BLOCK, B_KV_WIDTH), F32), pltpu.VMEM((BLOCK, B_KV_WIDTH), F32)],
        compiler_params=_params(("arbitrary",)),
    )(proj, proj, proj, proj, proj, q_gain, k_gain, sinks, do)


W_IN, W_A, W_B, W_OUT, W_MI, W_MO = range(6)


def _local_step(x, target, mod8, norm1_gain, norm2_gain, lb_logits, o_gain, q_gain, k_gain, sinks, parts, c_arr, chip_arr):
    relu2 = lambda u: (u, jnp.square(jnp.maximum(u, 0.0)))
    pair, half = {}, {}

    def exchange(ws, grads):
        return _sibling_exchange_plan([_grad_view(g, w) for w, g in zip(ws, grads)])

    def pair_sums(ws, grads, others):
        for w, g, o in zip(ws, grads, others):
            pair[w] = _pair_sum(_grad_view(g, w), o, c_arr, f"pair_sum{w}")

    def sum_slots(ws, slots):
        for w, s in zip(ws, slots):
            half[w] = _sum_slots(pair[w], s, w, chip_arr, f"sum_slots{w}")

    h, (w_in,) = _norm1_fwd(x, norm1_gain, mod8, plan=_gather_plan({W_IN: parts[W_IN]}, pass_at=(0.4, 0.7)))
    proj, (w_mi,) = _mm(h, w_in, name="mm_proj", bn=512, plan=_gather_plan({W_MI: parts[W_MI]}, pass_at=(0.47, 0.72)))
    (o_a, states), (w_a, w_b) = _hgrn_fwd(
        proj, lb_logits, o_gain, plan=_gather_plan({w: parts[w] for w in (W_A, W_B)}, pass_at=(0.4, 0.65)))
    o_b, (w_out,) = _swa_fwd(proj, q_gain, k_gain, sinks, plan=_gather_plan({W_OUT: parts[W_OUT]}, pass_at=(0.3, 0.5)))
    ya = _mm(o_a, w_a, name="mm_branch_a")
    gate_cols = (OFF_GATE_A // MERGE_BC, OFF_GATE_B // MERGE_BC)
    yb, merged = _mm(o_b, w_b, name="mm_branch_b", bn=MERGE_BC, out_dtypes=(F32, BF16),
                     extras=(proj, proj, ya), extra_cols=gate_cols + (0,),
                     epi=lambda acc, ga, gb, ya_: (acc, _sig(ga) * ya_ + _sig(gb) * acc))
    mo = _mm(merged, w_out, name="mm_out")
    x1, h2 = _res_norm2_fwd(x, mo, norm2_gain, mod8)
    (u, act), (w_mo,) = _mm(h2, w_mi, name="mm_mlp_in", out_dtypes=(F32, BF16), epi=relu2,
                            plan=_gather_plan({W_MO: parts[W_MO]}, pass_at=(0.6, 0.9)))
    mlp = _mm(act, w_mo, name="mm_mlp_out")
    dy, dmlp, st_loss = _loss_bwd(x1, mlp, target, mod8)
    g_mo = _mm(act, dmlp, name="mm_g_mlp_out", ta=True, bn=512)
    du, others = _mm(dmlp, w_mo, name="mm_d_act", tb=True, out_dtypes=(BF16,), extras=(u,),
                     epi=lambda acc, uu: (acc * (2.0 * jnp.maximum(uu, 0.0)),), plan=exchange([W_MO], [g_mo]))
    pair_sums([W_MO], [g_mo], others)
    near, far = (0, 1), (2,)
    g_mi, (part,) = _mm(h2, du, name="mm_g_mlp_in", ta=True, bn=512,
                        plan=_chip_exchange_plan({W_MO: pair[W_MO]}, near))
    dh2, res = _mm(du, w_mi, name="mm_d_h2", tb=True,
                   plan=_join(_chip_exchange_plan({W_MO: pair[W_MO]}, far, {W_MO: part}), exchange([W_MI], [g_mi])))
    sum_slots([W_MO], res[:1])
    pair_sums([W_MI], [g_mi], res[1:])
    dx1, dmo, st_n2 = _norm2_bwd(dh2, x1, dy, mo, norm2_gain, mod8)
    def merge_bwd(dm, ga, gb, ya_, yb_):
        sa, sb = _sig(ga), _sig(gb)
        return dm * sa, dm * sb, dm * ya_ * sa * (1.0 - sa), dm * yb_ * sb * (1.0 - sb)

    (dya, dyb, dga, dgb), (part,) = _mm(dmo, w_out, name="mm_d_merged", tb=True, bn=MERGE_BC, out_dtypes=(BF16,) * 4,
                                        extras=(proj, proj, ya, yb), extra_cols=gate_cols + (0, 0), epi=merge_bwd,
                                        plan=_chip_exchange_plan({W_MI: pair[W_MI]}, near))
    g_out = _mm(merged, dmo, name="mm_g_out", ta=True, bn=512)
    do_a = _mm(dya, w_a, name="mm_d_oa", tb=True)
    g_a = _mm(o_a, dya, name="mm_g_branch_a", ta=True, bn=512)
    do_b = _mm(dyb, w_b, name="mm_d_ob", tb=True)
    g_b = _mm(o_b, dyb, name="mm_g_branch_b", ta=True, bn=512)
    mid = [W_A, W_B, W_OUT]
    (dqb, dkvb, st_swa), res = _swa_bwd(
        proj, q_gain, k_gain, sinks, do_b,
        plan=_join(_chip_exchange_plan({W_MI: pair[W_MI]}, far, {W_MI: part}), exchange(mid, [g_a, g_b, g_out])))
    sum_slots([W_MI], res[:1])
    pair_sums(mid, [g_a, g_b, g_out], res[1:])
    (dqa, dfa, dia, dgga, d_lb, d_og), parts_mid = _hgrn_bwd(
        proj, lb_logits, o_gain, states, do_a, plan=_chip_exchange_plan({w: pair[w] for w in mid}, near))
    dproj = jnp.concatenate([dqa, dfa, dia, dgga, dqb, dkvb, dga, dgb], axis=1)
    hr = D_MODEL // 2
    h_send = lax.dynamic_slice(h, (0, (1 - c_arr[0]) * hr), (h.shape[0], hr))
    h_own = lax.dynamic_slice(h, (0, c_arr[0] * hr), (h.shape[0], hr))
    done = [W_A, W_B, W_OUT, W_MI, W_MO]
    g_send, slots_mid = _mm(h_send, dproj, name="mm_g_in_send", ta=True, bn=512,
                            plan=_chip_exchange_plan({w: pair[w] for w in mid}, far, dict(zip(mid, parts_mid))))
    sum_slots(mid, slots_mid)
    g_own, res = _mm(h_own, dproj, name="mm_g_in_own", ta=True, bn=512,
                     plan=_sibling_share_plan([g_send] + [half[w] for w in done]))
    g_other, theirs = res[0], dict(zip(done, res[1:]))
    pair[W_IN] = _add_bf16(g_own, g_other, "pair_sum0")[None]
    dh, slots_in = _mm(dproj, w_in, name="mm_d_h", tb=True, bk=2432, plan=_chip_exchange_plan({W_IN: pair[W_IN]}))
    sum_slots([W_IN], slots_in)
    grad_x, st_n1 = _norm1_bwd(dh, x, dx1, norm1_gain, mod8)
    (theirs[W_IN],) = _run_plan(_sibling_share_plan([half[W_IN]]), "sibling_share_w_in")
    stats = dict(loss=st_loss, n2=st_n2, n1=st_n1, d_lb=d_lb, d_og=d_og, swa=st_swa)
    return grad_x, [half[w] for w in range(N_W)], [theirs[w] for w in range(N_W)], stats


def _ew_rows(rows, cols):
    br = 8
    while br * 2 <= rows and br * 2 * cols * 4 <= (1 << 20) and rows % (br * 2) == 0:
        br *= 2
    return br


def _cast_into_full(shard, w, chip_arr, name):
    sr, sc = shard.shape
    R, C, by_col = W_SHAPES[w]
    br = _ew_rows(sr, sc)
    nb = sr // br
    out_map = (lambda i, chip: (i, chip[0])) if by_col else (lambda i, chip: (chip[0] * nb + i, 0))

    def body(chip_ref, w_ref, o_ref):
        o_ref[...] = w_ref[...].astype(BF16)

    return _pcall(
        body, name=name,
        grid_spec=pltpu.PrefetchScalarGridSpec(
            num_scalar_prefetch=1, grid=(nb,),
            in_specs=[pl.BlockSpec((br, sc), lambda i, chip: (i, 0))],
            out_specs=pl.BlockSpec((br, sc), out_map)),
        out_shape=jax.ShapeDtypeStruct((R, C), BF16), compiler_params=_params(("parallel",)))(chip_arr, shard)


def _adamw_math(w, g, m, v):
    m = ADAM_B1 * m + (1.0 - ADAM_B1) * g
    v = ADAM_B2 * v + (1.0 - ADAM_B2) * (g * g)
    m_hat = m / (1.0 - ADAM_B1 ** ADAM_STEP)
    v_hat = v / (1.0 - ADAM_B2 ** ADAM_STEP)
    delta = -ADAM_LR * (m_hat / (jnp.sqrt(v_hat) + ADAM_EPS) + ADAM_WD * w)
    return delta, m, v


def _adamw(w, g, m, v, name):
    R, C = w.shape
    br = _ew_rows(R, C)
    spec = pl.BlockSpec((br, C), lambda i: (i, 0))

    def body(w_ref, g_ref, m_ref, v_ref, d_ref, nm_ref, nv_ref):
        d_ref[...], nm_ref[...], nv_ref[...] = _adamw_math(w_ref[...], g_ref[...], m_ref[...], v_ref[...])

    sh = jax.ShapeDtypeStruct((R, C), F32)
    return _pcall(body, name=name, grid=(R // br,), in_specs=[spec] * 4, out_specs=[spec] * 3, out_shape=[sh] * 3,
                  compiler_params=_params(("parallel",)))(w, g, m, v)


def _add_bf16(a, b, name):
    R, C = a.shape
    br = _ew_rows(R, C)
    spec = pl.BlockSpec((br, C), lambda i: (i, 0))

    def body(a_ref, b_ref, o_ref):
        o_ref[...] = (a_ref[...] + b_ref[...]).astype(BF16)

    return _pcall(body, name=name, grid=(R // br,), in_specs=[spec, spec], out_specs=spec,
                  out_shape=jax.ShapeDtypeStruct((R, C), BF16), compiler_params=_params(("parallel",)))(a, b)


def _adamw_halves(w, own, other, m, v, c_arr, name):
    R, C = w.shape
    hr = R // 2
    br = _ew_rows(hr, C)
    nb = hr // br
    full = pl.BlockSpec((br, C), lambda h, i, c_ref: (h * nb + i, 0))
    half = pl.BlockSpec((br, C), lambda h, i, c_ref: (i, 0))

    def body(c_ref, w_ref, own_ref, oth_ref, m_ref, v_ref, g_ref, d_ref, nm_ref, nv_ref):
        g = jnp.where(pl.program_id(0) == c_ref[0], own_ref[...], oth_ref[...])
        g_ref[...] = g
        d_ref[...], nm_ref[...], nv_ref[...] = _adamw_math(w_ref[...], g, m_ref[...], v_ref[...])

    sh = jax.ShapeDtypeStruct((R, C), F32)
    return _pcall(
        body, name=name,
        grid_spec=pltpu.PrefetchScalarGridSpec(
            num_scalar_prefetch=1, grid=(2, nb), in_specs=[full, half, half, full, full], out_specs=[full] * 4),
        out_shape=[sh] * 4, compiler_params=_params(("parallel", "parallel")))(c_arr, w, own, other, m, v)


def _ada_grad_adamw(c_t, dmod, w, m, v):
    R, C = w.shape
    br = _ew_rows(R, C)
    spec = pl.BlockSpec((br, C), lambda i: (i, 0))

    def body(c_ref, dm_ref, w_ref, m_ref, v_ref, g_ref, d_ref, nm_ref, nv_ref):
        cv = c_ref[...]
        sc = cv * _sig(cv)
        g = sc[:, 0:1] * dm_ref[0:1, :]
        for b in range(1, N_DEV):
            g = g + sc[:, b:b + 1] * dm_ref[b:b + 1, :]
        g_ref[...] = g
        d_ref[...], nm_ref[...], nv_ref[...] = _adamw_math(w_ref[...], g, m_ref[...], v_ref[...])

    sh = jax.ShapeDtypeStruct((R, C), F32)
    return _pcall(
        body, name="ada_grad_adamw", grid=(R // br,),
        in_specs=[pl.BlockSpec((br, N_DEV), lambda i: (i, 0)), pl.BlockSpec((N_DEV, C), lambda i: (0, 0)), spec, spec, spec],
        out_specs=[spec] * 4, out_shape=[sh] * 4, compiler_params=_params(("parallel",)))(c_t, dmod, w, m, v)


SMALL_ROWS = 16


def _small_sum(small_all, lb_logits):
    def body(s_ref, lbl_ref, o_ref):
        acc = s_ref[0:SMALL_ROWS, :]
        for d in range(1, N_DEV):
            acc = acc + s_ref[d * SMALL_ROWS:(d + 1) * SMALL_ROWS, :]
        o_ref[...] = acc
        z = lbl_ref[...]
        e = jnp.exp(z - jnp.max(z, axis=0, keepdims=True))
        p0 = e[0:1, :] / (e[0:1, :] + e[1:2, :])
        dz = acc[8:9, 0:A_WIDTH] * p0 * (1.0 - p0)
        o_ref[8:9, 0:A_WIDTH] = dz
        o_ref[10:11, 0:A_WIDTH] = -dz

    return _pcall(body, name="small_sum", out_shape=jax.ShapeDtypeStruct((SMALL_ROWS, D_MODEL), F32),
                  in_specs=[pl.BlockSpec(memory_space=pltpu.VMEM)] * 2, out_specs=pl.BlockSpec(memory_space=pltpu.VMEM),
                  compiler_params=_params())(small_all, lb_logits)


RELATIONS = ((1, 0), (0, 1), (1, 1))
ANY = pl.BlockSpec(memory_space=pl.ANY)


def _place():
    x, y, c = lax.axis_index("x"), lax.axis_index("y"), lax.axis_index("c")
    return x, y, c


def _allgather_small(x_shard, name):
    m_per, n = x_shard.shape

    def body(x_ref, out_ref, send_sems, recv_sems, local_sem):
        x, y, c = _place()
        me, sibling = (x, y, c), (x, y, 1 - c)
        chips = [(1 - x, y), (x, 1 - y), (1 - x, 1 - y)]

        def rows(px, py, pc):
            return out_ref.at[pl.ds((4 * px + 2 * py + pc) * m_per, m_per), :]

        def copy(k, block, to, src=None):
            return pltpu.make_async_remote_copy(
                src_ref=rows(*block) if src is None else src, dst_ref=rows(*block),
                send_sem=send_sems.at[k], recv_sem=recv_sems.at[k], device_id=to, device_id_type=MESH)

        mine = pltpu.make_async_copy(x_ref, rows(*me), local_sem)
        mine.start()
        first = [copy(0, me, sibling, src=x_ref)]
        first += [copy(1 + j, me, (*chip, c), src=x_ref) for j, chip in enumerate(chips)]
        for cp in first:
            cp.start()
        passed = [copy(4 + j, (*chip, c), sibling) for j, chip in enumerate(chips)]
        for j, chip in enumerate(chips):
            copy(1 + j, (*chip, c), me).wait_recv()
            passed[j].start()
        copy(0, sibling, me).wait_recv()
        for j, chip in enumerate(chips):
            copy(4 + j, (*chip, 1 - c), me).wait_recv()
        for cp in first + passed:
            cp.wait_send()
        mine.wait()

    return _pcall(
        body, name=name, out_shape=jax.ShapeDtypeStruct((N_DEV * m_per, n), x_shard.dtype),
        in_specs=[pl.BlockSpec(memory_space=pltpu.VMEM)], out_specs=pl.BlockSpec(memory_space=pltpu.VMEM),
        scratch_shapes=[pltpu.SemaphoreType.DMA((7,)), pltpu.SemaphoreType.DMA((7,)), pltpu.SemaphoreType.DMA],
        compiler_params=_params(),
    )(x_shard)


W_SHAPES = ((D_MODEL, IN_WIDTH, True), (A_WIDTH, D_MODEL, True), (B_WIDTH, D_MODEL, True),
            (D_MODEL, D_MODEL, False), (D_MODEL, MLP_HIDDEN, True), (MLP_HIDDEN, D_MODEL, False))
N_W = len(W_SHAPES)


def _shard_shape(w):
    R, C, by_col = W_SHAPES[w]
    return (R, C // N_CHIPS) if by_col else (R // N_CHIPS, C)


def _half_shape(w):
    sr, sc = _shard_shape(w)
    return sr // 2, sc


def _region(full_ref, w, chip, half, quarter=None):
    sr, sc = _shard_shape(w)
    by_col = W_SHAPES[w][2]
    r0, c0 = (0, chip * sc) if by_col else (chip * sr, 0)
    r0, rows = r0 + half * (sr // 2), sr // 2
    if quarter is not None:
        r0, rows = r0 + quarter * (rows // 2), rows // 2
    return full_ref.at[pl.ds(r0, rows), pl.ds(c0, sc)]


def _on_device(fn):
    x, y, c = _place()
    me = 4 * x + 2 * y + c
    for d in range(N_DEV):
        @pl.when(me == d)
        def _(d=d):
            fn(x, y, c, d)


GATHER_COPIES = (
    (0, 0, None, "x"), (0, 0, None, "y"),
    (1, 2, 0, "y"), (1, 1, 1, "x"),
    (1, 2, None, "s"), (1, 1, None, "s"),
    (2, 3, 0, "s"), (2, 3, 1, "s"),
)
PEER_FLIP = {"x": 2, "y": 1, "s": 0}


def _gather_plan(partials, pass_at=(0.5, 0.75)):
    ws = sorted(partials)
    n_t = len(GATHER_COPIES)
    jobs = [(i, w, t) for i, w in enumerate(ws) for t in range(n_t)]

    def copy(pi, po, ps, x, y, c, d, i, w, t, landing):
        chip, dc = d >> 1, d & 1
        stage, flip, quarter, to = GATHER_COPIES[t]
        if landing:
            peer_chip = chip ^ PEER_FLIP[to]
            part = _region(po[i], w, peer_chip ^ flip, (1 - dc) if to == "s" else dc, quarter)
            src = part
        else:
            part = _region(po[i], w, chip ^ flip, dc, quarter)
            src = _region(pi[i], w, chip, dc, quarter) if flip == 0 else part
        target = {"x": (x ^ 1, y, c), "y": (x, y ^ 1, c), "s": (x, y, 1 - c)}[to]
        return pltpu.make_async_remote_copy(
            src_ref=src, dst_ref=part, send_sem=ps[0].at[i * n_t + t], recv_sem=ps[1].at[i * n_t + t],
            device_id=target, device_id_type=MESH)

    def stage(k):
        def run(pi, po, ps):
            def on(x, y, c, d):
                for i, w, t in jobs:
                    if k >= 1 and GATHER_COPIES[t][0] == k - 1 and GATHER_COPIES[t][3] != "s":
                        copy(pi, po, ps, x, y, c, d, i, w, t, True).wait_recv()
                for i, w, t in jobs:
                    if GATHER_COPIES[t][0] == k:
                        copy(pi, po, ps, x, y, c, d, i, w, t, False).start()
                if k == 3:
                    for i, w, t in jobs:
                        if GATHER_COPIES[t][3] == "s":
                            copy(pi, po, ps, x, y, c, d, i, w, t, True).wait_recv()
                    for i, w, t in jobs:
                        copy(pi, po, ps, x, y, c, d, i, w, t, False).wait_send()
            _on_device(on)
        return run

    return _Plan([partials[w] for w in ws], [jax.ShapeDtypeStruct(W_SHAPES[w][:2], BF16) for w in ws],
                 [pltpu.SemaphoreType.DMA((n_t * len(ws),)) for _ in range(2)], [stage(k) for k in range(4)],
                 {i: i for i in range(len(ws))}, mid_at=tuple(pass_at))


def _grad_view(g, w):
    R, C, by_col = W_SHAPES[w]
    return g.reshape(1, 2, R // 2, C) if by_col else g.reshape(N_CHIPS, 2, R // N_CHIPS // 2, C)


def _start_wait_plan(ins, outs, n_copies, copies):
    def start(pi, po, ps):
        for cp in copies(pi, po, ps):
            cp.start()

    def finish(pi, po, ps):
        for cp in copies(pi, po, ps):
            cp.wait()

    return _Plan(ins, outs, [pltpu.SemaphoreType.DMA((n_copies,)), pltpu.SemaphoreType.DMA((n_copies,))], [start, finish])


def _sibling_exchange_plan(g4s):
    pieces = [(i, p) for i, g in enumerate(g4s) for p in range(g.shape[0])]

    def copies(pi, po, ps):
        x, y, c = _place()
        return [pltpu.make_async_remote_copy(
            src_ref=pi[i].at[p, 1 - c], dst_ref=po[i].at[p], send_sem=ps[0].at[n], recv_sem=ps[1].at[n],
            device_id=(x, y, 1 - c), device_id_type=MESH) for n, (i, p) in enumerate(pieces)]

    return _start_wait_plan(list(g4s), [jax.ShapeDtypeStruct((g.shape[0],) + g.shape[2:], F32) for g in g4s],
                            len(pieces), copies)


def _pair_sum(g4, other, c_arr, name):
    P, _, hr, C = g4.shape
    br = _ew_rows(hr, C)

    def body(c_ref, g_ref, o_ref, p_ref):
        p_ref[...] = (g_ref[...] + o_ref[...]).astype(BF16)

    return _pcall(
        body, name=name,
        grid_spec=pltpu.PrefetchScalarGridSpec(
            num_scalar_prefetch=1, grid=(P, hr // br),
            in_specs=[pl.BlockSpec((None, None, br, C), lambda p, i, c_ref: (p, c_ref[0], i, 0)),
                      pl.BlockSpec((None, br, C), lambda p, i, c_ref: (p, i, 0))],
            out_specs=pl.BlockSpec((None, br, C), lambda p, i, c_ref: (p, i, 0))),
        out_shape=jax.ShapeDtypeStruct((P, hr, C), BF16),
        compiler_params=_params(("parallel", "parallel")),
    )(c_arr, g4, other)


def _pair_part(p_ref, w, chip):
    sr, sc = _shard_shape(w)
    return p_ref.at[0, :, pl.ds(chip * sc, sc)] if W_SHAPES[w][2] else p_ref.at[chip]


def _chip_exchange_plan(pairs, rels=(0, 1, 2), into=None):
    ws = sorted(pairs)
    n = len(ws)

    def stage(wait):
        def run(pi, po, ps):
            def on(x, y, c, d):
                for i, w in enumerate(ws):
                    for k, (rx, ry) in enumerate(RELATIONS):
                        if k not in rels:
                            continue
                        cp = pltpu.make_async_remote_copy(
                            src_ref=_pair_part(pi[i], w, (d >> 1) ^ (2 * rx + ry)), dst_ref=po[i].at[k],
                            send_sem=ps[0].at[i * 3 + k], recv_sem=ps[1].at[i * 3 + k],
                            device_id=(x ^ rx, y ^ ry, c), device_id_type=MESH)
                        if wait:
                            cp.wait()
                        else:
                            cp.start()
            _on_device(on)
        return run

    ins = [pairs[w] for w in ws] + ([into[w] for w in ws] if into else [])
    return _Plan(ins, [jax.ShapeDtypeStruct((3,) + _half_shape(w), BF16) for w in ws],
                 [pltpu.SemaphoreType.DMA((3 * n,)), pltpu.SemaphoreType.DMA((3 * n,))],
                 [stage(False), stage(True)], {n + i: i for i in range(n)} if into else None)


def _sum_slots(pair, slots, w, chip_arr, name):
    _, hr, C = slots.shape
    br = _ew_rows(hr, C)
    own_map = (lambda i, chip: (0, i, chip[0])) if W_SHAPES[w][2] else (lambda i, chip: (chip[0], i, 0))

    def body(chip_ref, p_ref, s_ref, o_ref):
        acc = p_ref[...].astype(F32)
        for k in range(3):
            acc = acc + s_ref[k].astype(F32)
        o_ref[...] = acc

    return _pcall(
        body, name=name,
        grid_spec=pltpu.PrefetchScalarGridSpec(
            num_scalar_prefetch=1, grid=(hr // br,),
            in_specs=[pl.BlockSpec((None, br, C), own_map), pl.BlockSpec((3, br, C), lambda i, chip: (0, i, 0))],
            out_specs=pl.BlockSpec((br, C), lambda i, chip: (i, 0))),
        out_shape=jax.ShapeDtypeStruct((hr, C), F32), compiler_params=_params(("parallel",)),
    )(chip_arr, pair, slots)


def _sibling_share_plan(halves):
    def copies(pi, po, ps):
        x, y, c = _place()
        return [pltpu.make_async_remote_copy(
            src_ref=pi[i], dst_ref=po[i], send_sem=ps[0].at[i], recv_sem=ps[1].at[i],
            device_id=(x, y, 1 - c), device_id_type=MESH) for i in range(len(halves))]

    return _start_wait_plan(list(halves), [jax.ShapeDtypeStruct(h.shape, F32) for h in halves], len(halves), copies)


def _pad_lanes(v, width=D_MODEL):
    return jnp.pad(v, ((0, 0), (0, width - v.shape[1])))


def _pack_small(b_ada, norm1, norm2, lb, o_gain, q_gain, k_gain, sinks):
    rows = [b_ada.reshape(N_MOD, D_MODEL), norm1, norm2, jnp.concatenate([lb[0:1], o_gain], axis=1),
            _pad_lanes(jnp.concatenate([q_gain, k_gain, sinks], axis=1)), _pad_lanes(lb[1:2]),
            jnp.zeros((SMALL_ROWS - 11, D_MODEL), F32)]
    return jnp.concatenate(rows, axis=0)


def _unpack_small(p):
    return (p[0:6].reshape(1, N_MOD * D_MODEL), p[6:7], p[7:8],
            jnp.concatenate([p[8:9, 0:A_WIDTH], p[10:11, 0:A_WIDTH]], axis=0), p[8:9, A_WIDTH:],
            p[9:10, 0:64], p[9:10, 64:128], p[9:10, 128:144])


def kernel(x, c, w_ada, b_ada, norm1_gain, w_in, lb_logits, hgrn_o_gain, q_norm_gain, k_norm_gain, sinks, w_branch_a, w_branch_b, w_out, norm2_gain, w_mlp_in, w_mlp_out, loss_target, m_w_ada, m_b_ada, m_norm1_gain, m_w_in, m_lb_logits, m_hgrn_o_gain, m_q_norm_gain, m_k_norm_gain, m_sinks, m_w_branch_a, m_w_branch_b, m_w_out, m_norm2_gain, m_w_mlp_in, m_w_mlp_out, v_w_ada, v_b_ada, v_norm1_gain, v_w_in, v_lb_logits, v_hgrn_o_gain, v_q_norm_gain, v_k_norm_gain, v_sinks, v_w_branch_a, v_w_branch_b, v_w_out, v_norm2_gain, v_w_mlp_in, v_w_mlp_out):
    xi, yi, ci = _place()
    chip = 2 * xi + yi
    me = 4 * xi + 2 * yi + ci
    ada_cols = w_ada.shape[2]

    c_all = _allgather_small(jnp.broadcast_to(c, (8, D_MODEL)), "gather_c").reshape(N_DEV, 8, D_MODEL)[:, 0]
    b_cols = lax.dynamic_slice(b_ada, (0, chip * ada_cols), (1, ada_cols))
    mod_part = _ada_fwd(c_all, w_ada[0], b_cols)
    mod_all = _allgather_small(mod_part, "gather_mod").reshape(N_CHIPS, 2, N_DEV, ada_cols)[:, 0]
    mod_mine = lax.dynamic_index_in_dim(mod_all, me, axis=1, keepdims=False).reshape(N_MOD, D_MODEL)
    mod8 = jnp.concatenate([mod_mine, jnp.zeros((2, D_MODEL), F32)], axis=0)

    shards = (w_in[0], w_branch_a[0], w_branch_b[0], w_out[0], w_mlp_in[0], w_mlp_out[0])
    chip_arr = chip.astype(jnp.int32).reshape(1)
    c_arr = ci.astype(jnp.int32).reshape(1)
    parts = [_cast_into_full(s, w, chip_arr, f"cast_w{w}") for w, s in enumerate(shards)]

    grad_x, halves, theirs, st = _local_step(x[0], loss_target[0], mod8, norm1_gain, norm2_gain, lb_logits, hgrn_o_gain,
                                             q_norm_gain, k_norm_gain, sinks, parts, c_arr, chip_arr)
    loss = lax.psum(0.5 * jnp.sum(st["loss"][0]) / D_MODEL, ("x", "y", "c"))
    moments = ((m_w_in, v_w_in), (m_w_branch_a, v_w_branch_a), (m_w_branch_b, v_w_branch_b), (m_w_out, v_w_out),
               (m_w_mlp_in, v_w_mlp_in), (m_w_mlp_out, v_w_mlp_out))
    big = [_adamw_halves(shards[w], halves[w], theirs[w], moments[w][0][0], moments[w][1][0], c_arr, f"adamw{w}")
           for w in range(N_W)]

    swa = st["swa"]
    small = jnp.concatenate([
        st["n1"][1:2], st["n1"][0:1], st["n2"][3:4], st["n2"][1:2], st["n2"][0:1], st["loss"][1:2],
        st["n1"][2:3], st["n2"][2:3], jnp.concatenate([st["d_lb"][0:1], st["d_og"][0:1]], axis=1),
        _pad_lanes(jnp.concatenate([swa[0:1, 0:64], swa[1:2, 0:64], swa[2:3, 0:16]], axis=1)),
        jnp.zeros((SMALL_ROWS - 10, D_MODEL), F32)], axis=0)
    small_all = _allgather_small(small, "gather_small")
    g_small = _small_sum(small_all, lb_logits)
    small_w = (b_ada, norm1_gain, norm2_gain, lb_logits, hgrn_o_gain, q_norm_gain, k_norm_gain, sinks)
    small_m = (m_b_ada, m_norm1_gain, m_norm2_gain, m_lb_logits, m_hgrn_o_gain, m_q_norm_gain, m_k_norm_gain, m_sinks)
    small_v = (v_b_ada, v_norm1_gain, v_norm2_gain, v_lb_logits, v_hgrn_o_gain, v_q_norm_gain, v_k_norm_gain, v_sinks)
    sm = [_unpack_small(t) for t in
          (g_small,) + tuple(_adamw(_pack_small(*small_w), g_small, _pack_small(*small_m), _pack_small(*small_v),
                                    "adamw_small"))]
    g_b, g_n1, g_n2, g_lb, g_og, g_qg, g_kg, g_sk = ([t[i] for t in sm] for i in range(8))

    dmod_all = small_all.reshape(N_DEV, SMALL_ROWS, D_MODEL)[:, 0:N_MOD].reshape(N_DEV, N_MOD * D_MODEL)
    dmod_cols = lax.dynamic_slice(dmod_all, (0, chip * ada_cols), (N_DEV, ada_cols))
    ada = _ada_grad_adamw(c_all.T, dmod_cols, w_ada[0], m_w_ada[0], v_w_ada[0])

    def ordered(k):
        lead = lambda a: a[None]
        return (lead(ada[k]), g_b[k], g_n1[k], lead(big[0][k]), g_lb[k], g_og[k], g_qg[k], g_kg[k], g_sk[k],
                lead(big[1][k]), lead(big[2][k]), lead(big[3][k]), g_n2[k], lead(big[4][k]), lead(big[5][k]))

    return (loss, grad_x[None]) + ordered(0) + ordered(1) + ordered(2) + ordered(3)
```

```python
import functools

import jax
import jax.numpy as jnp
from jax import lax
from jax.experimental import pallas as pl
from jax.experimental.pallas import tpu as pltpu

F32 = jnp.float32
BF16 = jnp.bfloat16
HIGHEST = lax.Precision.HIGHEST
MESH = pl.DeviceIdType.MESH

D_MODEL = 2048
A_WIDTH = 1024
A_HEADS = 8
A_HEAD_DIM = 128
A_CHUNK = 64
B_WIDTH = 1024
B_HEAD_DIM = 64
B_GROUP = 4
B_KV_HEADS = 4
B_KV_WIDTH = 256
BLOCK = 128
MLP_HIDDEN = 8192
IN_WIDTH = 9728
N_MOD = 6
EPS = 1e-6
N_CHIPS = 4
N_DEV = 8

OFF_QA, OFF_FA, OFF_IA, OFF_GA = 0, 1024, 2048, 3072
OFF_QB, OFF_KB, OFF_VB = 4096, 5120, 5376
OFF_GATE_A, OFF_GATE_B = 5632, 7680

ADAM_LR = 0.001
ADAM_B1 = 0.9
ADAM_B2 = 0.999
ADAM_EPS = 1e-08
ADAM_WD = 0.01
ADAM_STEP = 10

VMEM_LIMIT_V7X = 48 * 1024 * 1024
NEG_BIG = -1e30


def _params(sem=None, vmem=VMEM_LIMIT_V7X):
    return pltpu.CompilerParams(dimension_semantics=sem, vmem_limit_bytes=vmem)


class _Plan:
    def __init__(self, ins, outs, sems, stages, aliases=None, mid_at=()):
        self.ins, self.outs, self.sems, self.stages, self.aliases = ins, outs, sems, stages, aliases or {}
        self.mid_at = tuple(mid_at)
        assert len(self.mid_at) == len(stages) - 2


def _join(a, b):
    assert len(a.stages) == 2 and len(b.stages) == 2
    ni, no, ns = len(a.ins), len(a.outs), len(a.sems)

    def stage(k):
        def run(pi, po, ps):
            a.stages[k](pi[:ni], po[:no], ps[:ns])
            b.stages[k](pi[ni:], po[no:], ps[ns:])
        return run

    aliases = dict(a.aliases)
    aliases.update({ni + i: no + o for i, o in b.aliases.items()})
    return _Plan(a.ins + b.ins, a.outs + b.outs, a.sems + b.sems, [stage(0), stage(1)], aliases)


def _pcall(body, plan=None, **kw):
    if plan is None:
        return pl.pallas_call(body, **kw)
    grid = kw["grid"]
    single = not isinstance(kw["out_specs"], (list, tuple))
    in_specs = list(kw["in_specs"])
    out_specs = [kw["out_specs"]] if single else list(kw["out_specs"])
    out_shape = [kw["out_shape"]] if single else list(kw["out_shape"])
    scratch = list(kw.get("scratch_shapes", ()))
    n_in, n_out, n_scr = len(in_specs), len(out_specs), len(scratch)
    n_pi, n_po = len(plan.ins), len(plan.outs)
    total = 1
    for g in grid:
        total *= g
    n_st = len(plan.stages)

    def wrapped(*refs):
        o0 = n_in + n_pi
        s0 = o0 + n_out + n_po
        pi, po, ps = refs[n_in:o0], refs[o0 + n_out:s0], refs[s0 + n_scr:]
        lin = 0
        for d, g in enumerate(grid):
            lin = lin * g + pl.program_id(d)
        for si, frac in enumerate((0.0,) + plan.mid_at):
            @pl.when(lin == int(frac * (total - 1)))
            def _(si=si):
                plan.stages[si](pi, po, ps)
        body(*refs[:n_in], *refs[o0:o0 + n_out], *refs[s0:s0 + n_scr])

        @pl.when(lin == total - 1)
        def _():
            plan.stages[-1](pi, po, ps)

    any_spec = pl.BlockSpec(memory_space=pl.ANY)
    call = pl.pallas_call(
        wrapped, name=kw["name"], grid=grid, in_specs=in_specs + [any_spec] * n_pi,
        out_specs=out_specs + [any_spec] * n_po, out_shape=out_shape + list(plan.outs),
        scratch_shapes=scratch + list(plan.sems),
        input_output_aliases={n_in + i: n_out + o for i, o in plan.aliases.items()},
        compiler_params=_params(("arbitrary",) * len(grid)))

    def run(*args):
        res = call(*args, *plan.ins)
        outs = list(res[:n_out])
        return (outs[0] if single else outs), list(res[n_out:])

    return run


def _run_plan(plan, name):
    return _pcall(lambda: None, plan=plan, name=name, grid=(1,), in_specs=[], out_specs=[], out_shape=[])()[1]


def _sig(x):
    return 1.0 / (1.0 + jnp.exp(-x))


def _nn(a, b):
    return lax.dot_general(a.astype(BF16), b.astype(BF16), (((1,), (0,)), ((), ())), preferred_element_type=F32)


def _nt(a, b):
    return lax.dot_general(a.astype(BF16), b.astype(BF16), (((1,), (1,)), ((), ())), preferred_element_type=F32)


def _tn(a, b):
    return lax.dot_general(a.astype(BF16), b.astype(BF16), (((0,), (0,)), ((), ())), preferred_element_type=F32)


def _mm(a, b, *, name, ta=False, tb=False, bm=1024, bn=1024, bk=2048, out_dtypes=(F32,), epi=None, extras=(),
        extra_cols=None, plan=None):
    if ta:
        K, M = a.shape
        bk = K
    else:
        M, K = a.shape
    if tb:
        N, K2 = b.shape
    else:
        K2, N = b.shape
    bm, bn, bk = min(bm, M), min(bn, N), min(bk, K)
    assert K == K2 and M % bm == 0 and N % bn == 0 and K % bk == 0, (name, a.shape, b.shape)
    nk = K // bk
    a_spec = pl.BlockSpec((bk, bm), lambda i, j, k: (k, i)) if ta else pl.BlockSpec((bm, bk), lambda i, j, k: (i, k))
    b_spec = pl.BlockSpec((bn, bk), lambda i, j, k: (j, k)) if tb else pl.BlockSpec((bk, bn), lambda i, j, k: (k, j))
    t_spec = pl.BlockSpec((bm, bn), lambda i, j, k: (i, j))
    extra_cols = extra_cols or (0,) * len(extras)
    e_specs = [pl.BlockSpec((bm, bn), lambda i, j, k, off=off: (i, off + j)) for off in extra_cols]
    dims = (((1,), (1 if tb else 0,)), ((), ()))
    n_e, n_o = len(extras), len(out_dtypes)

    def body(*refs):
        a_ref, b_ref = refs[0], refs[1]
        e_refs = refs[2:2 + n_e]
        o_refs = refs[2 + n_e:2 + n_e + n_o]

        def finish(acc):
            outs = (acc,) if epi is None else epi(acc, *[e[...] for e in e_refs])
            for o_ref, o in zip(o_refs, outs):
                o_ref[...] = o.astype(o_ref.dtype)

        if ta:
            at_ref = refs[-1]

            @pl.when(pl.program_id(1) == 0)
            def _():
                at_ref[...] = a_ref[...].T

            lhs = at_ref[...]
        else:
            lhs = a_ref[...].astype(BF16)
        part = lax.dot_general(lhs, b_ref[...].astype(BF16), dims, preferred_element_type=F32)
        if nk == 1:
            finish(part)
        else:
            acc_ref = refs[-1]
            k = pl.program_id(2)

            @pl.when(k == 0)
            def _():
                acc_ref[...] = part

            @pl.when(k > 0)
            def _():
                acc_ref[...] += part

            @pl.when(k == nk - 1)
            def _():
                finish(acc_ref[...])

    if ta:
        assert a.dtype == BF16 and nk == 1
        scratch = [pltpu.VMEM((bm, bk), BF16)]
    else:
        scratch = [pltpu.VMEM((bm, bn), F32)] if nk > 1 else []
    out = _pcall(
        body, plan=plan, name=name, grid=(M // bm, N // bn, nk),
        in_specs=[a_spec, b_spec] + e_specs,
        out_specs=[t_spec] * n_o,
        out_shape=[jax.ShapeDtypeStruct((M, N), dt) for dt in out_dtypes],
        scratch_shapes=scratch,
        compiler_params=_params(("parallel", "arbitrary", "arbitrary")),
    )(a, b, *extras)
    if plan is not None:
        return (out[0][0] if n_o == 1 else out[0]), out[1]
    return out[0] if n_o == 1 else out


def _ada_fwd(c_all, w_ada, b_cols):
    n = w_ada.shape[1]
    bn = 512

    def body(c_ref, w_ref, b_ref, o_ref):
        cv = c_ref[...]
        sc = cv * _sig(cv)
        o_ref[...] = jnp.dot(sc, w_ref[...], precision=HIGHEST, preferred_element_type=F32) + b_ref[...]

    return _pcall(
        body, name="ada_fwd", grid=(n // bn,),
        in_specs=[pl.BlockSpec((N_DEV, D_MODEL), lambda j: (0, 0)), pl.BlockSpec((D_MODEL, bn), lambda j: (0, j)),
                  pl.BlockSpec((1, bn), lambda j: (0, j))],
        out_specs=pl.BlockSpec((N_DEV, bn), lambda j: (0, j)),
        out_shape=jax.ShapeDtypeStruct((N_DEV, n), F32),
        compiler_params=_params(("parallel",)),
    )(c_all, w_ada, b_cols)


ROWS_EW = 256


def _rms_fwd_math(x, gain, scale, shift):
    rstd = lax.rsqrt(jnp.mean(x * x, axis=-1, keepdims=True) + EPS)
    xhat = x * rstd
    n = xhat * gain
    return n * (1.0 + scale) + shift, xhat, n, rstd


def _rms_bwd_math(dh, xhat, n, rstd, gain, scale):
    dn = dh * (1.0 + scale)
    dxhat = dn * gain
    dx = rstd * (dxhat - xhat * jnp.mean(dxhat * xhat, axis=-1, keepdims=True))
    d_scale = jnp.sum(dh * n, axis=0, keepdims=True)
    d_shift = jnp.sum(dh, axis=0, keepdims=True)
    d_gain = jnp.sum(dn * xhat, axis=0, keepdims=True)
    return dx, d_scale, d_shift, d_gain


def _row_spec(w=D_MODEL, br=ROWS_EW):
    return pl.BlockSpec((br, w), lambda i: (i, 0))


def _vec_spec(r=8, w=D_MODEL):
    return pl.BlockSpec((r, w), lambda i: (0, 0))


def _norm1_fwd(x, gain, mod8, plan=None):
    T = x.shape[0]

    def body(x_ref, g_ref, m_ref, h_ref):
        h, _, _, _ = _rms_fwd_math(x_ref[...], g_ref[...], m_ref[1:2, :], m_ref[0:1, :])
        h_ref[...] = h.astype(BF16)

    return _pcall(
        body, plan=plan, name="norm1_fwd", grid=(T // ROWS_EW,),
        in_specs=[_row_spec(), _vec_spec(1), _vec_spec()],
        out_specs=_row_spec(), out_shape=jax.ShapeDtypeStruct((T, D_MODEL), BF16),
        compiler_params=_params(("parallel",)),
    )(x, gain, mod8)


def _res_norm2_fwd(x, mo, gain, mod8):
    T = x.shape[0]

    def body(x_ref, mo_ref, g_ref, m_ref, x1_ref, h_ref):
        x1 = x_ref[...] + m_ref[2:3, :] * mo_ref[...]
        x1_ref[...] = x1
        h, _, _, _ = _rms_fwd_math(x1, g_ref[...], m_ref[4:5, :], m_ref[3:4, :])
        h_ref[...] = h.astype(BF16)

    return _pcall(
        body, name="res_norm2_fwd", grid=(T // ROWS_EW,),
        in_specs=[_row_spec(), _row_spec(), _vec_spec(1), _vec_spec()],
        out_specs=[_row_spec(), _row_spec()],
        out_shape=[jax.ShapeDtypeStruct((T, D_MODEL), F32), jax.ShapeDtypeStruct((T, D_MODEL), BF16)],
        compiler_params=_params(("parallel",)),
    )(x, mo, gain, mod8)


def _loss_bwd(x1, mlp, target, mod8):
    T = x1.shape[0]

    def body(x1_ref, mlp_ref, t_ref, m_ref, dy_ref, dmlp_ref, st_ref):
        i = pl.program_id(0)
        gate = m_ref[5:6, :]
        mlp_v = mlp_ref[...]
        err = x1_ref[...] + gate * mlp_v - t_ref[...]
        dy = err * (1.0 / D_MODEL)
        dy_ref[...] = dy
        dmlp_ref[...] = (dy * gate).astype(BF16)

        @pl.when(i == 0)
        def _():
            st_ref[...] = jnp.zeros_like(st_ref)

        st_ref[0:1, :] += jnp.sum(err * err, axis=0, keepdims=True)
        st_ref[1:2, :] += jnp.sum(dy * mlp_v, axis=0, keepdims=True)

    return _pcall(
        body, name="loss_bwd", grid=(T // ROWS_EW,),
        in_specs=[_row_spec(), _row_spec(), _row_spec(), _vec_spec()],
        out_specs=[_row_spec(), _row_spec(), _vec_spec()],
        out_shape=[jax.ShapeDtypeStruct((T, D_MODEL), F32), jax.ShapeDtypeStruct((T, D_MODEL), BF16),
                   jax.ShapeDtypeStruct((8, D_MODEL), F32)],
        compiler_params=_params(("arbitrary",)),
    )(x1, mlp, target, mod8)


def _norm2_bwd(dh2, x1, dy, mo, gain, mod8):
    T = x1.shape[0]

    def body(dh_ref, x1_ref, dy_ref, mo_ref, g_ref, m_ref, dx1_ref, dmo_ref, st_ref):
        i = pl.program_id(0)
        gain_v, scale = g_ref[...], m_ref[4:5, :]
        _, xhat, n, rstd = _rms_fwd_math(x1_ref[...], gain_v, scale, m_ref[3:4, :])
        dx, d_scale, d_shift, d_gain = _rms_bwd_math(dh_ref[...], xhat, n, rstd, gain_v, scale)
        dx1 = dy_ref[...] + dx
        dx1_ref[...] = dx1
        dmo_ref[...] = (dx1 * m_ref[2:3, :]).astype(BF16)

        @pl.when(i == 0)
        def _():
            st_ref[...] = jnp.zeros_like(st_ref)

        st_ref[0:1, :] += d_scale
        st_ref[1:2, :] += d_shift
        st_ref[2:3, :] += d_gain
        st_ref[3:4, :] += jnp.sum(dx1 * mo_ref[...], axis=0, keepdims=True)

    return _pcall(
        body, name="norm2_bwd", grid=(T // ROWS_EW,),
        in_specs=[_row_spec(), _row_spec(), _row_spec(), _row_spec(), _vec_spec(1), _vec_spec()],
        out_specs=[_row_spec(), _row_spec(), _vec_spec()],
        out_shape=[jax.ShapeDtypeStruct((T, D_MODEL), F32), jax.ShapeDtypeStruct((T, D_MODEL), BF16),
                   jax.ShapeDtypeStruct((8, D_MODEL), F32)],
        compiler_params=_params(("arbitrary",)),
    )(dh2, x1, dy, mo, gain, mod8)


def _norm1_bwd(dh, x, dx1, gain, mod8):
    T = x.shape[0]

    def body(dh_ref, x_ref, dx1_ref, g_ref, m_ref, dx_ref, st_ref):
        i = pl.program_id(0)
        gain_v, scale = g_ref[...], m_ref[1:2, :]
        _, xhat, n, rstd = _rms_fwd_math(x_ref[...], gain_v, scale, m_ref[0:1, :])
        dx, d_scale, d_shift, d_gain = _rms_bwd_math(dh_ref[...], xhat, n, rstd, gain_v, scale)
        dx_ref[...] = dx1_ref[...] + dx

        @pl.when(i == 0)
        def _():
            st_ref[...] = jnp.zeros_like(st_ref)

        st_ref[0:1, :] += d_scale
        st_ref[1:2, :] += d_shift
        st_ref[2:3, :] += d_gain

    return _pcall(
        body, name="norm1_bwd", grid=(T // ROWS_EW,),
        in_specs=[_row_spec(), _row_spec(), _row_spec(), _vec_spec(1), _vec_spec()],
        out_specs=[_row_spec(), _vec_spec()],
        out_shape=[jax.ShapeDtypeStruct((T, D_MODEL), F32), jax.ShapeDtypeStruct((8, D_MODEL), F32)],
        compiler_params=_params(("arbitrary",)),
    )(dh, x, dx1, gain, mod8)


MERGE_BC = 512


def _hgrn_rows(T):
    return 512 if T >= 1024 else 128


def _lower_bound(lbl):
    e = jnp.exp(lbl - jnp.max(lbl, axis=0, keepdims=True))
    return e[0:1, :] / (e[0:1, :] + e[1:2, :])


def _chunk_sum_matrix(rows, backward):
    shift = A_CHUNK.bit_length() - 1
    r = lax.broadcasted_iota(jnp.int32, (rows, rows), 0)
    c = lax.broadcasted_iota(jnp.int32, (rows, rows), 1)
    same = jnp.right_shift(r, shift) == jnp.right_shift(c, shift)
    return (same & ((r <= c) if backward else (r >= c))).astype(BF16)


def _chunk_sums(m, x):
    n = x.shape[1]
    hi = x.astype(BF16)
    rest = x - hi.astype(F32)
    mid = rest.astype(BF16)
    lo = (rest - mid.astype(F32)).astype(BF16)
    y = jnp.dot(m, jnp.concatenate([hi, mid, lo], axis=1), preferred_element_type=F32)
    return y[:, 0:n] + y[:, n:2 * n] + y[:, 2 * n:3 * n]


def _hgrn_block_pre(q, fl, lb, m_fwd):
    sg = _sig(fl)
    f = lb + (1.0 - lb) * sg
    sq = _sig(q)
    return dict(sg=sg, f=f, k=1.0 - f, sq=sq, qf=q * sq, b=_chunk_sums(m_fwd, jnp.log(f)))


def _hgrn_chunk_local(pre, r):
    C = A_CHUNK
    qf, k, b = pre["qf"][r], pre["k"][r], pre["b"][r]
    causal = lax.broadcasted_iota(jnp.int32, (C, C), 0) >= lax.broadcasted_iota(jnp.int32, (C, C), 1)
    bm = b[C // 2 - 1:C // 2, :]
    bl = b[C - 1:C, :]
    e_q, e_k = jnp.exp(b - bm), jnp.exp(bm - b)
    e_b, e_l = jnp.exp(b), jnp.exp(bl - b)
    qd, kd = qf * e_q, k * e_k
    qe, ke = qf * e_b, k * e_l
    att = jnp.where(causal, _nt(qd, kd), 0.0)
    return dict(causal=causal, e_q=e_q, e_k=e_k, e_b=e_b, e_l=e_l, qd=qd, kd=kd, qe=qe, ke=ke, att=att, dec=jnp.exp(bl))


def _hgrn_chunk_fwd(pre, r, v, st):
    c = _hgrn_chunk_local(pre, r)
    c["o"] = _nn(c["att"], v) + _nt(c["qe"], st)
    return c


def _lockstep(gens):
    out = [None] * len(gens)
    live = list(enumerate(gens))
    while live:
        still = []
        for i, g in live:
            try:
                next(g)
                still.append((i, g))
            except StopIteration as done:
                out[i] = done.value
        live = still
    return out


HGRN_HEADS_PER_STEP = 4


def _hgrn_fwd(proj, lb_logits, o_gain, plan=None):
    T = proj.shape[0]
    BR = _hgrn_rows(T)
    cps = BR // A_CHUNK
    K, NH = A_HEAD_DIM, HGRN_HEADS_PER_STEP
    W = NH * K

    def col(off):
        return pl.BlockSpec((BR, W), lambda h, cb: (cb, off // W + h))

    def body(q_ref, f_ref, i_ref, g_ref, lbl_ref, og_ref, o_ref, s_ref, st):
        @pl.when(pl.program_id(1) == 0)
        def _():
            st[...] = jnp.zeros_like(st)

        lb_all = _lower_bound(lbl_ref[...])
        m_fwd = _chunk_sum_matrix(BR, False)
        pre = [_hgrn_block_pre(q_ref[:, n * K:(n + 1) * K], f_ref[:, n * K:(n + 1) * K], lb_all[:, n * K:(n + 1) * K], m_fwd)
               for n in range(NH)]
        def local(n, ci):
            r, hs = slice(ci * A_CHUNK, (ci + 1) * A_CHUNK), slice(n * K, (n + 1) * K)
            v = i_ref[r, hs]
            c = _hgrn_chunk_local(pre[n], r)
            yield
            return dict(o=_nn(c["att"], v), ds=_tn(v, c["ke"]), qe=c["qe"], dec=c["dec"])

        def chain(n, loc):
            hs = slice(n * K, (n + 1) * K)
            state = st[n]
            for ci, p in enumerate(loc):
                r = slice(ci * A_CHUNK, (ci + 1) * A_CHUNK)
                s_ref[n, ci] = state
                o = p["o"] + _nt(p["qe"], state)
                state = state * p["dec"] + p["ds"]
                yield
                on = o * lax.rsqrt(jnp.mean(o * o, axis=-1, keepdims=True) + EPS)
                g = g_ref[r, hs]
                o_ref[r, hs] = (on * og_ref[:, hs] * (g * _sig(g))).astype(BF16)
            st[n] = state

        loc = _lockstep([local(n, ci) for n in range(NH) for ci in range(cps)])
        _lockstep([chain(n, loc[n * cps:(n + 1) * cps]) for n in range(NH)])

    return _pcall(
        body, plan=plan, name="hgrn_fwd", grid=(A_HEADS // NH, T // BR),
        in_specs=[col(OFF_QA), col(OFF_FA), col(OFF_IA), col(OFF_GA),
                  pl.BlockSpec((2, W), lambda h, cb: (0, h)), pl.BlockSpec((1, W), lambda h, cb: (0, h))],
        out_specs=[pl.BlockSpec((BR, W), lambda h, cb: (cb, h)),
                   pl.BlockSpec((NH, cps, K, K), lambda h, cb: (h, cb, 0, 0))],
        out_shape=[jax.ShapeDtypeStruct((T, A_WIDTH), BF16),
                   jax.ShapeDtypeStruct((A_HEADS, T // A_CHUNK, K, K), F32)],
        scratch_shapes=[pltpu.VMEM((NH, K, K), F32)],
        compiler_params=_params(("parallel", "arbitrary")),
    )(proj, proj, proj, proj, lb_logits, o_gain)


def _hgrn_bwd(proj, lb_logits, o_gain, states, do, plan=None):
    T = proj.shape[0]
    BR = _hgrn_rows(T)
    cps = BR // A_CHUNK
    ncb = T // BR
    K, C, NH = A_HEAD_DIM, A_CHUNK, HGRN_HEADS_PER_STEP
    W = NH * K

    def col(off):
        return pl.BlockSpec((BR, W), lambda h, cb: (ncb - 1 - cb, off // W + h))

    def body(q_ref, f_ref, i_ref, g_ref, lbl_ref, og_ref, s_ref, do_ref,
             dq_ref, df_ref, di_ref, dg_ref, dlb_ref, dog_ref, dst):
        @pl.when(pl.program_id(1) == 0)
        def _():
            dst[...] = jnp.zeros_like(dst)
            dlb_ref[...] = jnp.zeros_like(dlb_ref)
            dog_ref[...] = jnp.zeros_like(dog_ref)

        lb_all = _lower_bound(lbl_ref[...])
        row = lax.broadcasted_iota(jnp.int32, (C, K), 0)
        m_fwd, m_bwd = _chunk_sum_matrix(BR, False), _chunk_sum_matrix(BR, True)
        pre = [_hgrn_block_pre(q_ref[:, n * K:(n + 1) * K], f_ref[:, n * K:(n + 1) * K], lb_all[:, n * K:(n + 1) * K], m_fwd)
               for n in range(NH)]
        def local(n, ci):
            r, hs = slice(ci * C, (ci + 1) * C), slice(n * K, (n + 1) * K)
            gain = og_ref[:, hs]
            st = s_ref[n, ci]
            v = i_ref[r, hs]
            q = q_ref[r, hs]
            c = _hgrn_chunk_fwd(pre[n], r, v, st)
            yield
            o = c["o"]
            rn = lax.rsqrt(jnp.mean(o * o, axis=-1, keepdims=True) + EPS)
            on = o * rn
            g = g_ref[r, hs]
            sgg = _sig(g)
            dy = do_ref[r, hs]
            d_ong = dy * (g * sgg)
            dg_ref[r, hs] = (dy * (on * gain) * (sgg * (1.0 + g * (1.0 - sgg)))).astype(BF16)
            d_on = d_ong * gain
            d_o = rn * (d_on - on * jnp.mean(d_on * on, axis=-1, keepdims=True))
            datt = jnp.where(c["causal"], _nt(d_o, v), 0.0)
            dqe = _nn(d_o, st)
            yield
            dqd = _nn(datt, c["kd"])
            dkd = _tn(datt, c["qd"])
            dv = _tn(c["att"], d_o)
            ds = _tn(d_o, c["qe"])
            yield
            t_q, t_k = dqd * c["qd"], dkd * c["kd"]
            sq = pre[n]["sq"][r]
            dq_ref[r, hs] = ((dqd * c["e_q"] + dqe * c["e_b"]) * (sq * (1.0 + q * (1.0 - sq)))).astype(BF16)
            return dict(v=v, st=st, ke=c["ke"], e_l=c["e_l"], dec=c["dec"], dv=dv, ds=ds, dk=dkd * c["e_k"],
                        db=t_q - t_k + dqe * c["qe"], dbm=jnp.sum(t_k - t_q, axis=0, keepdims=True),
                        d_og=jnp.sum(d_ong * on, axis=0, keepdims=True))

        def chain(n, loc):
            hs = slice(n * K, (n + 1) * K)
            dst_next = dst[n]
            db_of, dk_of = [None] * cps, [None] * cps
            for ci in reversed(range(cps)):
                p = loc[ci]
                di_ref[ci * C:(ci + 1) * C, hs] = (p["dv"] + _nt(p["ke"], dst_next)).astype(BF16)
                dke = _nn(p["v"], dst_next)
                yield
                t_l = dke * p["ke"]
                dbl = jnp.sum(t_l, axis=0, keepdims=True) + jnp.sum(dst_next * p["st"], axis=0, keepdims=True) * p["dec"]
                db_of[ci] = p["db"] - t_l + jnp.where(row == C // 2 - 1, p["dbm"], 0.0) + jnp.where(row == C - 1, dbl, 0.0)
                dk_of[ci] = p["dk"] + dke * p["e_l"]
                dst_next = dst_next * p["dec"] + p["ds"]
            dst[n] = dst_next
            return db_of, dk_of

        loc = _lockstep([local(n, ci) for n in range(NH) for ci in range(cps)])
        loc = [loc[n * cps:(n + 1) * cps] for n in range(NH)]
        chains = _lockstep([chain(n, loc[n]) for n in range(NH)])
        for n in range(NH):
            hs = slice(n * K, (n + 1) * K)
            db_of, dk_of = chains[n]
            d_og = loc[n][0]["d_og"]
            for p in loc[n][1:]:
                d_og = d_og + p["d_og"]
            dog_ref[0:1, hs] += d_og
            lb, sg = lb_all[:, hs], pre[n]["sg"]
            dlf = _chunk_sums(m_bwd, jnp.concatenate(db_of, axis=0))
            df = dlf / pre[n]["f"] - jnp.concatenate(dk_of, axis=0)
            df_ref[:, hs] = (df * (1.0 - lb) * sg * (1.0 - sg)).astype(BF16)
            dlb_ref[0:1, hs] += jnp.sum(df * (1.0 - sg), axis=0, keepdims=True)

    ocol = pl.BlockSpec((BR, W), lambda h, cb: (ncb - 1 - cb, h))
    vec = pl.BlockSpec((8, W), lambda h, cb: (0, h))
    return _pcall(
        body, plan=plan, name="hgrn_bwd", grid=(A_HEADS // NH, ncb),
        in_specs=[col(OFF_QA), col(OFF_FA), col(OFF_IA), col(OFF_GA),
                  pl.BlockSpec((2, W), lambda h, cb: (0, h)), pl.BlockSpec((1, W), lambda h, cb: (0, h)),
                  pl.BlockSpec((NH, cps, K, K), lambda h, cb: (h, ncb - 1 - cb, 0, 0)),
                  pl.BlockSpec((BR, W), lambda h, cb: (ncb - 1 - cb, h))],
        out_specs=[ocol, ocol, ocol, ocol, vec, vec],
        out_shape=[jax.ShapeDtypeStruct((T, A_WIDTH), BF16)] * 4 + [jax.ShapeDtypeStruct((8, A_WIDTH), F32)] * 2,
        scratch_shapes=[pltpu.VMEM((NH, K, K), F32)],
        compiler_params=_params(("parallel", "arbitrary")),
    )(proj, proj, proj, proj, lb_logits, o_gain, states, do)


def _head_norm(x):
    r = lax.rsqrt(jnp.mean(x * x, axis=-1, keepdims=True) + EPS)
    return x * r, r


def _head_norm_bwd(dy, xn, r, gain):
    dxn = dy * gain
    return r * (dxn - xn * jnp.mean(dxn * xn, axis=-1, keepdims=True)), jnp.sum(dy * xn, axis=0, keepdims=True)


def _swa_mask(has_prev):
    rows = B_GROUP * BLOCK
    r = lax.broadcasted_iota(jnp.int32, (rows, 2 * BLOCK), 0) % BLOCK
    c = lax.broadcasted_iota(jnp.int32, (rows, 2 * BLOCK), 1)
    rel = r + BLOCK - c
    return (rel >= 0) & (rel < BLOCK) & ((c >= BLOCK) | has_prev)


def _swa_head_fwd(j, q_ref, kp_ref, kc_ref, vp_ref, vc_ref, qg, kg, sk_ref, mask):
    hs = slice(j * B_HEAD_DIM, (j + 1) * B_HEAD_DIM)
    kcat = jnp.concatenate([kp_ref[:, hs], kc_ref[:, hs]], axis=0)
    vcat = jnp.concatenate([vp_ref[:, hs], vc_ref[:, hs]], axis=0)
    qs = jnp.concatenate([q_ref[:, pl.ds((j * B_GROUP + g) * B_HEAD_DIM, B_HEAD_DIM)] for g in range(B_GROUP)], axis=0)
    kn, kr = _head_norm(kcat)
    qn, qr = _head_norm(qs)
    kh, qh = kn * kg, qn * qg
    yield
    s = jnp.where(mask, _nt(qh, kh) * (B_HEAD_DIM ** -0.5), NEG_BIG)
    yield
    sink = jnp.concatenate(
        [jnp.broadcast_to(sk_ref[0:1, pl.ds(j * B_GROUP + g, 1)], (BLOCK, 1)) for g in range(B_GROUP)], axis=0)
    m = jnp.maximum(jnp.max(s, axis=-1, keepdims=True), sink)
    p = jnp.exp(s - m)
    e_sink = jnp.exp(sink - m)
    inv = 1.0 / (jnp.sum(p, axis=-1, keepdims=True) + e_sink)
    prob = p * inv
    return dict(vcat=vcat, kn=kn, kr=kr, qn=qn, qr=qr, kh=kh, qh=qh, prob=prob, p_sink=e_sink * inv)


def _swa_in_specs(nb, last):
    def qi(n):
        return jnp.minimum(n, last)

    q = pl.BlockSpec((BLOCK, B_WIDTH), lambda n: (qi(n), OFF_QB // B_WIDTH))
    kc = pl.BlockSpec((BLOCK, B_KV_WIDTH), lambda n: (qi(n), OFF_KB // B_KV_WIDTH))
    kp = pl.BlockSpec((BLOCK, B_KV_WIDTH), lambda n: (jnp.maximum(qi(n) - 1, 0), OFF_KB // B_KV_WIDTH))
    vc = pl.BlockSpec((BLOCK, B_KV_WIDTH), lambda n: (qi(n), OFF_VB // B_KV_WIDTH))
    vp = pl.BlockSpec((BLOCK, B_KV_WIDTH), lambda n: (jnp.maximum(qi(n) - 1, 0), OFF_VB // B_KV_WIDTH))
    small = [pl.BlockSpec((1, B_HEAD_DIM), lambda n: (0, 0)), pl.BlockSpec((1, B_HEAD_DIM), lambda n: (0, 0)),
             pl.BlockSpec((1, B_GROUP * B_KV_HEADS), lambda n: (0, 0))]
    return [q, kp, kc, vp, vc] + small


def _swa_fwd(proj, q_gain, k_gain, sinks, plan=None):
    T = proj.shape[0]
    nb = T // BLOCK

    def body(q_ref, kp_ref, kc_ref, vp_ref, vc_ref, qg_ref, kg_ref, sk_ref, o_ref):
        mask = _swa_mask(pl.program_id(0) > 0)

        def head(j):
            c = yield from _swa_head_fwd(j, q_ref, kp_ref, kc_ref, vp_ref, vc_ref, qg_ref[...], kg_ref[...], sk_ref, mask)
            yield
            o = _nn(c["prob"], c["vcat"])
            yield
            for g in range(B_GROUP):
                o_ref[:, pl.ds((j * B_GROUP + g) * B_HEAD_DIM, B_HEAD_DIM)] = o[g * BLOCK:(g + 1) * BLOCK].astype(BF16)

        _lockstep([head(j) for j in range(B_KV_HEADS)])

    return _pcall(
        body, plan=plan, name="swa_fwd", grid=(nb,),
        in_specs=_swa_in_specs(nb, nb - 1),
        out_specs=pl.BlockSpec((BLOCK, B_WIDTH), lambda n: (n, 0)),
        out_shape=jax.ShapeDtypeStruct((T, B_WIDTH), BF16),
        compiler_params=_params(("parallel",)),
    )(proj, proj, proj, proj, proj, q_gain, k_gain, sinks)


def _swa_bwd(proj, q_gain, k_gain, sinks, do, plan=None):
    T = proj.shape[0]
    nb = T // BLOCK
    scale = B_HEAD_DIM ** -0.5

    def body(q_ref, kp_ref, kc_ref, vp_ref, vc_ref, qg_ref, kg_ref, sk_ref, do_ref,
             dq_ref, dkv_ref, sm_ref, ck, cv):
        n = pl.program_id(0)

        @pl.when(n == 0)
        def _():
            ck[...] = jnp.zeros_like(ck)
            cv[...] = jnp.zeros_like(cv)
            sm_ref[...] = jnp.zeros_like(sm_ref)

        @pl.when(n < nb)
        def _():
            mask = _swa_mask(n > 0)
            qg, kg = qg_ref[...], kg_ref[...]
            lane = lax.broadcasted_iota(jnp.int32, (1, BLOCK), 1)
            def head(j):
                hs = slice(j * B_HEAD_DIM, (j + 1) * B_HEAD_DIM)
                vs = slice(B_KV_WIDTH + j * B_HEAD_DIM, B_KV_WIDTH + (j + 1) * B_HEAD_DIM)
                c = yield from _swa_head_fwd(j, q_ref, kp_ref, kc_ref, vp_ref, vc_ref, qg, kg, sk_ref, mask)
                d_out = jnp.concatenate(
                    [do_ref[:, pl.ds((j * B_GROUP + g) * B_HEAD_DIM, B_HEAD_DIM)] for g in range(B_GROUP)], axis=0)
                prob = c["prob"]
                yield
                out = _nn(prob, c["vcat"])
                d_prob = _nt(d_out, c["vcat"])
                dv = _tn(prob, d_out)
                yield
                delta = jnp.sum(d_out * out, axis=-1, keepdims=True)
                ds = prob * (d_prob - delta)
                d_sink = -c["p_sink"] * delta
                yield
                dqh = _nn(ds, c["kh"]) * scale
                dkh = _tn(ds, c["qh"]) * scale
                yield
                dq, dqg = _head_norm_bwd(dqh, c["qn"], c["qr"], qg)
                dk, dkg = _head_norm_bwd(dkh, c["kn"], c["kr"], kg)
                d_sinks = jnp.zeros((1, BLOCK), F32)
                for g in range(B_GROUP):
                    dq_ref[:, pl.ds((j * B_GROUP + g) * B_HEAD_DIM, B_HEAD_DIM)] = dq[g * BLOCK:(g + 1) * BLOCK].astype(BF16)
                    tot = jnp.sum(d_sink[g * BLOCK:(g + 1) * BLOCK], axis=0, keepdims=True)
                    d_sinks = d_sinks + jnp.where(lane == j * B_GROUP + g, tot, 0.0)
                dkv_ref[:, hs] = (ck[:, hs] + dk[0:BLOCK]).astype(BF16)
                dkv_ref[:, vs] = (cv[:, hs] + dv[0:BLOCK]).astype(BF16)
                ck[:, hs] = dk[BLOCK:2 * BLOCK]
                cv[:, hs] = dv[BLOCK:2 * BLOCK]
                return dqg, dkg, d_sinks

            small = _lockstep([head(j) for j in range(B_KV_HEADS)])
            sm_ref[0:1, 0:B_HEAD_DIM] += small[0][0] + small[1][0] + small[2][0] + small[3][0]
            sm_ref[1:2, 0:B_HEAD_DIM] += small[0][1] + small[1][1] + small[2][1] + small[3][1]
            sm_ref[2:3, :] += small[0][2] + small[1][2] + small[2][2] + small[3][2]

        @pl.when(n == nb)
        def _():
            dkv_ref[:, 0:B_KV_WIDTH] = ck[...].astype(BF16)
            dkv_ref[:, B_KV_WIDTH:2 * B_KV_WIDTH] = cv[...].astype(BF16)

    return _pcall(
        body, plan=plan, name="swa_bwd", grid=(nb + 1,),
        in_specs=_swa_in_specs(nb, nb - 1) + [pl.BlockSpec((BLOCK, B_WIDTH), lambda n: (jnp.minimum(n, nb - 1), 0))],
        out_specs=[pl.BlockSpec((BLOCK, B_WIDTH), lambda n: (jnp.minimum(n, nb - 1), 0)),
                   pl.BlockSpec((BLOCK, 2 * B_KV_WIDTH), lambda n: (jnp.maximum(n - 1, 0), 0)),
                   pl.BlockSpec((8, BLOCK), lambda n: (0, 0))],
        out_shape=[jax.ShapeDtypeStruct((T, B_WIDTH), BF16), jax.ShapeDtypeStruct((T, 2 * B_KV_WIDTH), BF16),
                   jax.ShapeDtypeStruct((8, BLOCK), F32)],
        scratch_shapes=[pltpu.VMEM((BLOCK, B_KV_WIDTH), F32), pltpu.VMEM((BLOCK, B_KV_WIDTH), F32)],
        compiler_params=_params(("arbitrary",)),
    )(proj, proj, proj, proj, proj, q_gain, k_gain, sinks, do)


W_IN, W_A, W_B, W_OUT, W_MI, W_MO = range(6)


def _local_step(x, target, mod8, norm1_gain, norm2_gain, lb_logits, o_gain, q_gain, k_gain, sinks, shards, c_arr, chip_arr):
    relu2 = lambda u: (u, jnp.square(jnp.maximum(u, 0.0)))
    pair, half = {}, {}

    def exchange(ws, grads):
        return _sibling_exchange_plan([_grad_view(g, w) for w, g in zip(ws, grads)])

    def pair_sums(ws, grads, others):
        for w, g, o in zip(ws, grads, others):
            pair[w] = _pair_sum(_grad_view(g, w), o, c_arr, f"pair_sum{w}")

    def sum_slots(ws, slots):
        for w, s in zip(ws, slots):
            half[w] = _sum_slots(pair[w], s, w, chip_arr, f"sum_slots{w}")

    part_in = _cast_into_full({W_IN: shards[W_IN]}, "cast_w_in")[W_IN]
    h, (part_in,) = _norm1_fwd(x, norm1_gain, mod8, plan=_gather_plan({W_IN: part_in}, part="near"))
    parts, (w_in,) = _cast_into_full({w: shards[w] for w in range(1, N_W)}, "cast_rest",
                                     plan=_gather_plan({W_IN: part_in}, pass_at=(0.8,), part="far"))
    proj, (w_mi,) = _mm(h, w_in, name="mm_proj", bn=512, plan=_gather_plan({W_MI: parts[W_MI]}, pass_at=(0.47, 0.72)))
    (o_a, states), (w_a, w_b) = _hgrn_fwd(
        proj, lb_logits, o_gain, plan=_gather_plan({w: parts[w] for w in (W_A, W_B)}, pass_at=(0.4, 0.65)))
    o_b, (w_out,) = _swa_fwd(proj, q_gain, k_gain, sinks, plan=_gather_plan({W_OUT: parts[W_OUT]}, pass_at=(0.3, 0.5)))
    ya = _mm(o_a, w_a, name="mm_branch_a")
    gate_cols = (OFF_GATE_A // MERGE_BC, OFF_GATE_B // MERGE_BC)
    yb, merged = _mm(o_b, w_b, name="mm_branch_b", bn=MERGE_BC, out_dtypes=(F32, BF16),
                     extras=(proj, proj, ya), extra_cols=gate_cols + (0,),
                     epi=lambda acc, ga, gb, ya_: (acc, _sig(ga) * ya_ + _sig(gb) * acc))
    mo = _mm(merged, w_out, name="mm_out")
    x1, h2 = _res_norm2_fwd(x, mo, norm2_gain, mod8)
    (u, act), (w_mo,) = _mm(h2, w_mi, name="mm_mlp_in", out_dtypes=(F32, BF16), epi=relu2,
                            plan=_gather_plan({W_MO: parts[W_MO]}, pass_at=(0.6, 0.9)))
    mlp = _mm(act, w_mo, name="mm_mlp_out")
    dy, dmlp, st_loss = _loss_bwd(x1, mlp, target, mod8)
    g_mo = _mm(act, dmlp, name="mm_g_mlp_out", ta=True, bn=512)
    du, others = _mm(dmlp, w_mo, name="mm_d_act", tb=True, out_dtypes=(BF16,), extras=(u,),
                     epi=lambda acc, uu: (acc * (2.0 * jnp.maximum(uu, 0.0)),), plan=exchange([W_MO], [g_mo]))
    pair_sums([W_MO], [g_mo], others)
    near, far = (0, 1), (2,)
    g_mi, (part,) = _mm(h2, du, name="mm_g_mlp_in", ta=True, bn=512,
                        plan=_chip_exchange_plan({W_MO: pair[W_MO]}, near))
    dh2, res = _mm(du, w_mi, name="mm_d_h2", tb=True,
                   plan=_join(_chip_exchange_plan({W_MO: pair[W_MO]}, far, {W_MO: part}), exchange([W_MI], [g_mi])))
    sum_slots([W_MO], res[:1])
    pair_sums([W_MI], [g_mi], res[1:])
    dx1, dmo, st_n2 = _norm2_bwd(dh2, x1, dy, mo, norm2_gain, mod8)
    def merge_bwd(dm, ga, gb, ya_, yb_):
        sa, sb = _sig(ga), _sig(gb)
        return dm * sa, dm * sb, dm * ya_ * sa * (1.0 - sa), dm * yb_ * sb * (1.0 - sb)

    (dya, dyb, dga, dgb), (part,) = _mm(dmo, w_out, name="mm_d_merged", tb=True, bn=MERGE_BC, out_dtypes=(BF16,) * 4,
                                        extras=(proj, proj, ya, yb), extra_cols=gate_cols + (0, 0), epi=merge_bwd,
                                        plan=_chip_exchange_plan({W_MI: pair[W_MI]}, near))
    g_out = _mm(merged, dmo, name="mm_g_out", ta=True, bn=512)
    do_a = _mm(dya, w_a, name="mm_d_oa", tb=True)
    g_a = _mm(o_a, dya, name="mm_g_branch_a", ta=True, bn=512)
    do_b = _mm(dyb, w_b, name="mm_d_ob", tb=True)
    g_b = _mm(o_b, dyb, name="mm_g_branch_b", ta=True, bn=512)
    mid = [W_A, W_B, W_OUT]
    (dqb, dkvb, st_swa), res = _swa_bwd(
        proj, q_gain, k_gain, sinks, do_b,
        plan=_join(_chip_exchange_plan({W_MI: pair[W_MI]}, far, {W_MI: part}), exchange(mid, [g_a, g_b, g_out])))
    sum_slots([W_MI], res[:1])
    pair_sums(mid, [g_a, g_b, g_out], res[1:])
    (dqa, dfa, dia, dgga, d_lb, d_og), parts_mid = _hgrn_bwd(
        proj, lb_logits, o_gain, states, do_a, plan=_chip_exchange_plan({w: pair[w] for w in mid}, near))
    dproj = jnp.concatenate([dqa, dfa, dia, dgga, dqb, dkvb, dga, dgb], axis=1)
    hr = D_MODEL // 2
    h_send = lax.dynamic_slice(h, (0, (1 - c_arr[0]) * hr), (h.shape[0], hr))
    h_own = lax.dynamic_slice(h, (0, c_arr[0] * hr), (h.shape[0], hr))
    done = [W_A, W_B, W_OUT, W_MI, W_MO]
    g_send, slots_mid = _mm(h_send, dproj, name="mm_g_in_send", ta=True, bn=512,
                            plan=_chip_exchange_plan({w: pair[w] for w in mid}, far, dict(zip(mid, parts_mid))))
    sum_slots(mid, slots_mid)
    g_own, res = _mm(h_own, dproj, name="mm_g_in_own", ta=True, bn=512,
                     plan=_sibling_share_plan([g_send] + [half[w] for w in done]))
    g_other, theirs = res[0], dict(zip(done, res[1:]))
    pair[W_IN] = _add_bf16(g_own, g_other, "pair_sum0")[None]
    dh, slots_in = _mm(dproj, w_in, name="mm_d_h", tb=True, bk=2432, plan=_chip_exchange_plan({W_IN: pair[W_IN]}))
    sum_slots([W_IN], slots_in)
    grad_x, st_n1 = _norm1_bwd(dh, x, dx1, norm1_gain, mod8)
    (theirs[W_IN],) = _run_plan(_sibling_share_plan([half[W_IN]]), "sibling_share_w_in")
    stats = dict(loss=st_loss, n2=st_n2, n1=st_n1, d_lb=d_lb, d_og=d_og, swa=st_swa)
    return grad_x, [half[w] for w in range(N_W)], [theirs[w] for w in range(N_W)], stats


def _ew_rows(rows, cols):
    br = 8
    while br * 2 <= rows and br * 2 * cols * 4 <= (1 << 20) and rows % (br * 2) == 0:
        br *= 2
    return br


CAST_STEPS = 16


def _cast_into_full(shards, name, plan=None):
    ws = sorted(shards)
    in_specs, out_specs, out_shape = [], [], []
    for w in ws:
        sr, sc = shards[w].shape
        R, C, by_col = W_SHAPES[w]
        br = sr // CAST_STEPS
        assert br * CAST_STEPS == sr and br % 16 == 0, (w, sr)

        def out_map(i, by_col=by_col):
            chip = 2 * lax.axis_index("x") + lax.axis_index("y")
            return (i, chip) if by_col else (chip * CAST_STEPS + i, 0)

        in_specs.append(pl.BlockSpec((br, sc), lambda i: (i, 0)))
        out_specs.append(pl.BlockSpec((br, sc), out_map))
        out_shape.append(jax.ShapeDtypeStruct((R, C), BF16))

    def body(*refs):
        for w_ref, o_ref in zip(refs[:len(ws)], refs[len(ws):]):
            o_ref[...] = w_ref[...].astype(BF16)

    res = _pcall(body, plan=plan, name=name, grid=(CAST_STEPS,), in_specs=in_specs, out_specs=out_specs,
                 out_shape=out_shape, compiler_params=_params(("arbitrary",)))(*[shards[w] for w in ws])
    if plan is None:
        return dict(zip(ws, res))
    return dict(zip(ws, res[0])), res[1]


def _adamw_math(w, g, m, v):
    m = ADAM_B1 * m + (1.0 - ADAM_B1) * g
    v = ADAM_B2 * v + (1.0 - ADAM_B2) * (g * g)
    m_hat = m / (1.0 - ADAM_B1 ** ADAM_STEP)
    v_hat = v / (1.0 - ADAM_B2 ** ADAM_STEP)
    delta = -ADAM_LR * (m_hat / (jnp.sqrt(v_hat) + ADAM_EPS) + ADAM_WD * w)
    return delta, m, v


def _adamw(w, g, m, v, name):
    R, C = w.shape
    br = _ew_rows(R, C)
    spec = pl.BlockSpec((br, C), lambda i: (i, 0))

    def body(w_ref, g_ref, m_ref, v_ref, d_ref, nm_ref, nv_ref):
        d_ref[...], nm_ref[...], nv_ref[...] = _adamw_math(w_ref[...], g_ref[...], m_ref[...], v_ref[...])

    sh = jax.ShapeDtypeStruct((R, C), F32)
    return _pcall(body, name=name, grid=(R // br,), in_specs=[spec] * 4, out_specs=[spec] * 3, out_shape=[sh] * 3,
                  compiler_params=_params(("parallel",)))(w, g, m, v)


def _add_bf16(a, b, name):
    R, C = a.shape
    br = _ew_rows(R, C)
    spec = pl.BlockSpec((br, C), lambda i: (i, 0))

    def body(a_ref, b_ref, o_ref):
        o_ref[...] = (a_ref[...] + b_ref[...]).astype(BF16)

    return _pcall(body, name=name, grid=(R // br,), in_specs=[spec, spec], out_specs=spec,
                  out_shape=jax.ShapeDtypeStruct((R, C), BF16), compiler_params=_params(("parallel",)))(a, b)


def _adamw_halves(w, own, other, m, v, c_arr, name):
    R, C = w.shape
    hr = R // 2
    br = _ew_rows(hr, C)
    nb = hr // br
    full = pl.BlockSpec((br, C), lambda h, i, c_ref: (h * nb + i, 0))
    half = pl.BlockSpec((br, C), lambda h, i, c_ref: (i, 0))

    def body(c_ref, w_ref, own_ref, oth_ref, m_ref, v_ref, g_ref, d_ref, nm_ref, nv_ref):
        g = jnp.where(pl.program_id(0) == c_ref[0], own_ref[...], oth_ref[...])
        g_ref[...] = g
        d_ref[...], nm_ref[...], nv_ref[...] = _adamw_math(w_ref[...], g, m_ref[...], v_ref[...])

    sh = jax.ShapeDtypeStruct((R, C), F32)
    return _pcall(
        body, name=name,
        grid_spec=pltpu.PrefetchScalarGridSpec(
            num_scalar_prefetch=1, grid=(2, nb), in_specs=[full, half, half, full, full], out_specs=[full] * 4),
        out_shape=[sh] * 4, compiler_params=_params(("parallel", "parallel")))(c_arr, w, own, other, m, v)


def _ada_grad_adamw(c_t, dmod, w, m, v):
    R, C = w.shape
    br = _ew_rows(R, C)
    spec = pl.BlockSpec((br, C), lambda i: (i, 0))

    def body(c_ref, dm_ref, w_ref, m_ref, v_ref, g_ref, d_ref, nm_ref, nv_ref):
        cv = c_ref[...]
        sc = cv * _sig(cv)
        g = sc[:, 0:1] * dm_ref[0:1, :]
        for b in range(1, N_DEV):
            g = g + sc[:, b:b + 1] * dm_ref[b:b + 1, :]
        g_ref[...] = g
        d_ref[...], nm_ref[...], nv_ref[...] = _adamw_math(w_ref[...], g, m_ref[...], v_ref[...])

    sh = jax.ShapeDtypeStruct((R, C), F32)
    return _pcall(
        body, name="ada_grad_adamw", grid=(R // br,),
        in_specs=[pl.BlockSpec((br, N_DEV), lambda i: (i, 0)), pl.BlockSpec((N_DEV, C), lambda i: (0, 0)), spec, spec, spec],
        out_specs=[spec] * 4, out_shape=[sh] * 4, compiler_params=_params(("parallel",)))(c_t, dmod, w, m, v)


SMALL_ROWS = 16


def _small_sum(small_all, lb_logits):
    def body(s_ref, lbl_ref, o_ref):
        acc = s_ref[0:SMALL_ROWS, :]
        for d in range(1, N_DEV):
            acc = acc + s_ref[d * SMALL_ROWS:(d + 1) * SMALL_ROWS, :]
        o_ref[...] = acc
        z = lbl_ref[...]
        e = jnp.exp(z - jnp.max(z, axis=0, keepdims=True))
        p0 = e[0:1, :] / (e[0:1, :] + e[1:2, :])
        dz = acc[8:9, 0:A_WIDTH] * p0 * (1.0 - p0)
        o_ref[8:9, 0:A_WIDTH] = dz
        o_ref[10:11, 0:A_WIDTH] = -dz

    return _pcall(body, name="small_sum", out_shape=jax.ShapeDtypeStruct((SMALL_ROWS, D_MODEL), F32),
                  in_specs=[pl.BlockSpec(memory_space=pltpu.VMEM)] * 2, out_specs=pl.BlockSpec(memory_space=pltpu.VMEM),
                  compiler_params=_params())(small_all, lb_logits)


RELATIONS = ((1, 0), (0, 1), (1, 1))
ANY = pl.BlockSpec(memory_space=pl.ANY)


def _place():
    x, y, c = lax.axis_index("x"), lax.axis_index("y"), lax.axis_index("c")
    return x, y, c


def _allgather_small(x_shard, name):
    m_per, n = x_shard.shape

    def body(x_ref, out_ref, send_sems, recv_sems, local_sem):
        x, y, c = _place()
        me, sibling = (x, y, c), (x, y, 1 - c)
        chips = [(1 - x, y), (x, 1 - y), (1 - x, 1 - y)]

        def rows(px, py, pc):
            return out_ref.at[pl.ds((4 * px + 2 * py + pc) * m_per, m_per), :]

        def copy(k, block, to, src=None):
            return pltpu.make_async_remote_copy(
                src_ref=rows(*block) if src is None else src, dst_ref=rows(*block),
                send_sem=send_sems.at[k], recv_sem=recv_sems.at[k], device_id=to, device_id_type=MESH)

        mine = pltpu.make_async_copy(x_ref, rows(*me), local_sem)
        mine.start()
        first = [copy(0, me, sibling, src=x_ref)]
        first += [copy(1 + j, me, (*chip, c), src=x_ref) for j, chip in enumerate(chips)]
        for cp in first:
            cp.start()
        passed = [copy(4 + j, (*chip, c), sibling) for j, chip in enumerate(chips)]
        for j, chip in enumerate(chips):
            copy(1 + j, (*chip, c), me).wait_recv()
            passed[j].start()
        copy(0, sibling, me).wait_recv()
        for j, chip in enumerate(chips):
            copy(4 + j, (*chip, 1 - c), me).wait_recv()
        for cp in first + passed:
            cp.wait_send()
        mine.wait()

    return _pcall(
        body, name=name, out_shape=jax.ShapeDtypeStruct((N_DEV * m_per, n), x_shard.dtype),
        in_specs=[pl.BlockSpec(memory_space=pltpu.VMEM)], out_specs=pl.BlockSpec(memory_space=pltpu.VMEM),
        scratch_shapes=[pltpu.SemaphoreType.DMA((7,)), pltpu.SemaphoreType.DMA((7,)), pltpu.SemaphoreType.DMA],
        compiler_params=_params(),
    )(x_shard)


W_SHAPES = ((D_MODEL, IN_WIDTH, True), (A_WIDTH, D_MODEL, True), (B_WIDTH, D_MODEL, True),
            (D_MODEL, D_MODEL, False), (D_MODEL, MLP_HIDDEN, True), (MLP_HIDDEN, D_MODEL, False))
N_W = len(W_SHAPES)


def _shard_shape(w):
    R, C, by_col = W_SHAPES[w]
    return (R, C // N_CHIPS) if by_col else (R // N_CHIPS, C)


def _half_shape(w):
    sr, sc = _shard_shape(w)
    return sr // 2, sc


def _region(full_ref, w, chip, half, quarter=None):
    sr, sc = _shard_shape(w)
    by_col = W_SHAPES[w][2]
    r0, c0 = (0, chip * sc) if by_col else (chip * sr, 0)
    r0, rows = r0 + half * (sr // 2), sr // 2
    if quarter is not None:
        r0, rows = r0 + quarter * (rows // 2), rows // 2
    return full_ref.at[pl.ds(r0, rows), pl.ds(c0, sc)]


def _on_device(fn):
    x, y, c = _place()
    me = 4 * x + 2 * y + c
    for d in range(N_DEV):
        @pl.when(me == d)
        def _(d=d):
            fn(x, y, c, d)


GATHER_COPIES = (
    (0, 0, None, "x"), (0, 0, None, "y"),
    (1, 2, 0, "y"), (1, 1, 1, "x"),
    (1, 2, None, "s"), (1, 1, None, "s"),
    (2, 3, 0, "s"), (2, 3, 1, "s"),
)
PEER_FLIP = {"x": 2, "y": 1, "s": 0}


GATHER_STAGES = {
    None: (((), (0, 1), ()), ((0, 1), (2, 3, 4, 5), ()), ((2, 3), (6, 7), ()), ((4, 5, 6, 7), (), tuple(range(8)))),
    "near": (((), (0, 1), ()), ((0, 1), (), (0, 1))),
    "far": (((), (2, 3, 4, 5), ()), ((2, 3), (6, 7), ()), ((4, 5, 6, 7), (), (2, 3, 4, 5, 6, 7))),
}


def _gather_plan(partials, pass_at=(0.5, 0.75), part=None):
    ws = sorted(partials)
    n_t = len(GATHER_COPIES)
    jobs = [(i, w) for i, w in enumerate(ws)]

    def copy(pi, po, ps, x, y, c, d, i, w, t, landing):
        chip, dc = d >> 1, d & 1
        stage, flip, quarter, to = GATHER_COPIES[t]
        if landing:
            peer_chip = chip ^ PEER_FLIP[to]
            part = _region(po[i], w, peer_chip ^ flip, (1 - dc) if to == "s" else dc, quarter)
            src = part
        else:
            part = _region(po[i], w, chip ^ flip, dc, quarter)
            here = flip != 0 and (part_of is None or stage == 2)
            src = part if here else _region(pi[i], w, chip ^ flip, dc, quarter)
        target = {"x": (x ^ 1, y, c), "y": (x, y ^ 1, c), "s": (x, y, 1 - c)}[to]
        return pltpu.make_async_remote_copy(
            src_ref=src, dst_ref=part, send_sem=ps[0].at[i * n_t + t], recv_sem=ps[1].at[i * n_t + t],
            device_id=target, device_id_type=MESH)

    part_of = part

    def stage(landed, started, sent):
        def run(pi, po, ps):
            def on(x, y, c, d):
                for i, w in jobs:
                    for t in landed:
                        copy(pi, po, ps, x, y, c, d, i, w, t, True).wait_recv()
                for i, w in jobs:
                    for t in started:
                        copy(pi, po, ps, x, y, c, d, i, w, t, False).start()
                for i, w in jobs:
                    for t in sent:
                        copy(pi, po, ps, x, y, c, d, i, w, t, False).wait_send()
            _on_device(on)
        return run

    stages = [stage(*st) for st in GATHER_STAGES[part]]
    mid_at = tuple(pass_at) if part is None else tuple(pass_at)[:len(stages) - 2]
    return _Plan([partials[w] for w in ws], [jax.ShapeDtypeStruct(W_SHAPES[w][:2], BF16) for w in ws],
                 [pltpu.SemaphoreType.DMA((n_t * len(ws),)) for _ in range(2)], stages,
                 {i: i for i in range(len(ws))}, mid_at=mid_at)


def _grad_view(g, w):
    R, C, by_col = W_SHAPES[w]
    return g.reshape(1, 2, R // 2, C) if by_col else g.reshape(N_CHIPS, 2, R // N_CHIPS // 2, C)


def _start_wait_plan(ins, outs, n_copies, copies):
    def start(pi, po, ps):
        for cp in copies(pi, po, ps):
            cp.start()

    def finish(pi, po, ps):
        for cp in copies(pi, po, ps):
            cp.wait()

    return _Plan(ins, outs, [pltpu.SemaphoreType.DMA((n_copies,)), pltpu.SemaphoreType.DMA((n_copies,))], [start, finish])


def _sibling_exchange_plan(g4s):
    pieces = [(i, p) for i, g in enumerate(g4s) for p in range(g.shape[0])]

    def copies(pi, po, ps):
        x, y, c = _place()
        return [pltpu.make_async_remote_copy(
            src_ref=pi[i].at[p, 1 - c], dst_ref=po[i].at[p], send_sem=ps[0].at[n], recv_sem=ps[1].at[n],
            device_id=(x, y, 1 - c), device_id_type=MESH) for n, (i, p) in enumerate(pieces)]

    return _start_wait_plan(list(g4s), [jax.ShapeDtypeStruct((g.shape[0],) + g.shape[2:], F32) for g in g4s],
                            len(pieces), copies)


def _pair_sum(g4, other, c_arr, name):
    P, _, hr, C = g4.shape
    br = _ew_rows(hr, C)

    def body(c_ref, g_ref, o_ref, p_ref):
        p_ref[...] = (g_ref[...] + o_ref[...]).astype(BF16)

    return _pcall(
        body, name=name,
        grid_spec=pltpu.PrefetchScalarGridSpec(
            num_scalar_prefetch=1, grid=(P, hr // br),
            in_specs=[pl.BlockSpec((None, None, br, C), lambda p, i, c_ref: (p, c_ref[0], i, 0)),
                      pl.BlockSpec((None, br, C), lambda p, i, c_ref: (p, i, 0))],
            out_specs=pl.BlockSpec((None, br, C), lambda p, i, c_ref: (p, i, 0))),
        out_shape=jax.ShapeDtypeStruct((P, hr, C), BF16),
        compiler_params=_params(("parallel", "parallel")),
    )(c_arr, g4, other)


def _pair_part(p_ref, w, chip):
    sr, sc = _shard_shape(w)
    return p_ref.at[0, :, pl.ds(chip * sc, sc)] if W_SHAPES[w][2] else p_ref.at[chip]


def _chip_exchange_plan(pairs, rels=(0, 1, 2), into=None):
    ws = sorted(pairs)
    n = len(ws)

    def stage(wait):
        def run(pi, po, ps):
            def on(x, y, c, d):
                for i, w in enumerate(ws):
                    for k, (rx, ry) in enumerate(RELATIONS):
                        if k not in rels:
                            continue
                        cp = pltpu.make_async_remote_copy(
                            src_ref=_pair_part(pi[i], w, (d >> 1) ^ (2 * rx + ry)), dst_ref=po[i].at[k],
                            send_sem=ps[0].at[i * 3 + k], recv_sem=ps[1].at[i * 3 + k],
                            device_id=(x ^ rx, y ^ ry, c), device_id_type=MESH)
                        if wait:
                            cp.wait()
                        else:
                            cp.start()
            _on_device(on)
        return run

    ins = [pairs[w] for w in ws] + ([into[w] for w in ws] if into else [])
    return _Plan(ins, [jax.ShapeDtypeStruct((3,) + _half_shape(w), BF16) for w in ws],
                 [pltpu.SemaphoreType.DMA((3 * n,)), pltpu.SemaphoreType.DMA((3 * n,))],
                 [stage(False), stage(True)], {n + i: i for i in range(n)} if into else None)


def _sum_slots(pair, slots, w, chip_arr, name):
    _, hr, C = slots.shape
    br = _ew_rows(hr, C)
    own_map = (lambda i, chip: (0, i, chip[0])) if W_SHAPES[w][2] else (lambda i, chip: (chip[0], i, 0))

    def body(chip_ref, p_ref, s_ref, o_ref):
        acc = p_ref[...].astype(F32)
        for k in range(3):
            acc = acc + s_ref[k].astype(F32)
        o_ref[...] = acc

    return _pcall(
        body, name=name,
        grid_spec=pltpu.PrefetchScalarGridSpec(
            num_scalar_prefetch=1, grid=(hr // br,),
            in_specs=[pl.BlockSpec((None, br, C), own_map), pl.BlockSpec((3, br, C), lambda i, chip: (0, i, 0))],
            out_specs=pl.BlockSpec((br, C), lambda i, chip: (i, 0))),
        out_shape=jax.ShapeDtypeStruct((hr, C), F32), compiler_params=_params(("parallel",)),
    )(chip_arr, pair, slots)


def _sibling_share_plan(halves):
    def copies(pi, po, ps):
        x, y, c = _place()
        return [pltpu.make_async_remote_copy(
            src_ref=pi[i], dst_ref=po[i], send_sem=ps[0].at[i], recv_sem=ps[1].at[i],
            device_id=(x, y, 1 - c), device_id_type=MESH) for i in range(len(halves))]

    return _start_wait_plan(list(halves), [jax.ShapeDtypeStruct(h.shape, F32) for h in halves], len(halves), copies)


def _pad_lanes(v, width=D_MODEL):
    return jnp.pad(v, ((0, 0), (0, width - v.shape[1])))


def _pack_small(b_ada, norm1, norm2, lb, o_gain, q_gain, k_gain, sinks):
    rows = [b_ada.reshape(N_MOD, D_MODEL), norm1, norm2, jnp.concatenate([lb[0:1], o_gain], axis=1),
            _pad_lanes(jnp.concatenate([q_gain, k_gain, sinks], axis=1)), _pad_lanes(lb[1:2]),
            jnp.zeros((SMALL_ROWS - 11, D_MODEL), F32)]
    return jnp.concatenate(rows, axis=0)


def _unpack_small(p):
    return (p[0:6].reshape(1, N_MOD * D_MODEL), p[6:7], p[7:8],
            jnp.concatenate([p[8:9, 0:A_WIDTH], p[10:11, 0:A_WIDTH]], axis=0), p[8:9, A_WIDTH:],
            p[9:10, 0:64], p[9:10, 64:128], p[9:10, 128:144])


def kernel(x, c, w_ada, b_ada, norm1_gain, w_in, lb_logits, hgrn_o_gain, q_norm_gain, k_norm_gain, sinks, w_branch_a, w_branch_b, w_out, norm2_gain, w_mlp_in, w_mlp_out, loss_target, m_w_ada, m_b_ada, m_norm1_gain, m_w_in, m_lb_logits, m_hgrn_o_gain, m_q_norm_gain, m_k_norm_gain, m_sinks, m_w_branch_a, m_w_branch_b, m_w_out, m_norm2_gain, m_w_mlp_in, m_w_mlp_out, v_w_ada, v_b_ada, v_norm1_gain, v_w_in, v_lb_logits, v_hgrn_o_gain, v_q_norm_gain, v_k_norm_gain, v_sinks, v_w_branch_a, v_w_branch_b, v_w_out, v_norm2_gain, v_w_mlp_in, v_w_mlp_out):
    xi, yi, ci = _place()
    chip = 2 * xi + yi
    me = 4 * xi + 2 * yi + ci
    ada_cols = w_ada.shape[2]

    c_all = _allgather_small(jnp.broadcast_to(c, (8, D_MODEL)), "gather_c").reshape(N_DEV, 8, D_MODEL)[:, 0]
    b_cols = lax.dynamic_slice(b_ada, (0, chip * ada_cols), (1, ada_cols))
    mod_part = _ada_fwd(c_all, w_ada[0], b_cols)
    mod_all = _allgather_small(mod_part, "gather_mod").reshape(N_CHIPS, 2, N_DEV, ada_cols)[:, 0]
    mod_mine = lax.dynamic_index_in_dim(mod_all, me, axis=1, keepdims=False).reshape(N_MOD, D_MODEL)
    mod8 = jnp.concatenate([mod_mine, jnp.zeros((2, D_MODEL), F32)], axis=0)

    shards = (w_in[0], w_branch_a[0], w_branch_b[0], w_out[0], w_mlp_in[0], w_mlp_out[0])
    chip_arr = chip.astype(jnp.int32).reshape(1)
    c_arr = ci.astype(jnp.int32).reshape(1)

    grad_x, halves, theirs, st = _local_step(x[0], loss_target[0], mod8, norm1_gain, norm2_gain, lb_logits, hgrn_o_gain,
                                             q_norm_gain, k_norm_gain, sinks, shards, c_arr, chip_arr)
    loss = lax.psum(0.5 * jnp.sum(st["loss"][0]) / D_MODEL, ("x", "y", "c"))
    moments = ((m_w_in, v_w_in), (m_w_branch_a, v_w_branch_a), (m_w_branch_b, v_w_branch_b), (m_w_out, v_w_out),
               (m_w_mlp_in, v_w_mlp_in), (m_w_mlp_out, v_w_mlp_out))
    big = [_adamw_halves(shards[w], halves[w], theirs[w], moments[w][0][0], moments[w][1][0], c_arr, f"adamw{w}")
           for w in range(N_W)]

    swa = st["swa"]
    small = jnp.concatenate([
        st["n1"][1:2], st["n1"][0:1], st["n2"][3:4], st["n2"][1:2], st["n2"][0:1], st["loss"][1:2],
        st["n1"][2:3], st["n2"][2:3], jnp.concatenate([st["d_lb"][0:1], st["d_og"][0:1]], axis=1),
        _pad_lanes(jnp.concatenate([swa[0:1, 0:64], swa[1:2, 0:64], swa[2:3, 0:16]], axis=1)),
        jnp.zeros((SMALL_ROWS - 10, D_MODEL), F32)], axis=0)
    small_all = _allgather_small(small, "gather_small")
    g_small = _small_sum(small_all, lb_logits)
    small_w = (b_ada, norm1_gain, norm2_gain, lb_logits, hgrn_o_gain, q_norm_gain, k_norm_gain, sinks)
    small_m = (m_b_ada, m_norm1_gain, m_norm2_gain, m_lb_logits, m_hgrn_o_gain, m_q_norm_gain, m_k_norm_gain, m_sinks)
    small_v = (v_b_ada, v_norm1_gain, v_norm2_gain, v_lb_logits, v_hgrn_o_gain, v_q_norm_gain, v_k_norm_gain, v_sinks)
    sm = [_unpack_small(t) for t in
          (g_small,) + tuple(_adamw(_pack_small(*small_w), g_small, _pack_small(*small_m), _pack_small(*small_v),
                                    "adamw_small"))]
    g_b, g_n1, g_n2, g_lb, g_og, g_qg, g_kg, g_sk = ([t[i] for t in sm] for i in range(8))

    dmod_all = small_all.reshape(N_DEV, SMALL_ROWS, D_MODEL)[:, 0:N_MOD].reshape(N_DEV, N_MOD * D_MODEL)
    dmod_cols = lax.dynamic_slice(dmod_all, (0, chip * ada_cols), (N_DEV, ada_cols))
    ada = _ada_grad_adamw(c_all.T, dmod_cols, w_ada[0], m_w_ada[0], v_w_ada[0])

    def ordered(k):
        lead = lambda a: a[None]
        return (lead(ada[k]), g_b[k], g_n1[k], lead(big[0][k]), g_lb[k], g_og[k], g_qg[k], g_kg[k], g_sk[k],
                lead(big[1][k]), lead(big[2][k]), lead(big[3][k]), g_n2[k], lead(big[4][k]), lead(big[5][k]))

    return (loss, grad_x[None]) + ordered(0) + ordered(1) + ordered(2) + ordered(3)
```

```python
import functools

import jax
import jax.numpy as jnp
from jax import lax
from jax.experimental import pallas as pl
from jax.experimental.pallas import tpu as pltpu

F32 = jnp.float32
BF16 = jnp.bfloat16
HIGHEST = lax.Precision.HIGHEST
MESH = pl.DeviceIdType.MESH

D_MODEL = 2048
A_WIDTH = 1024
A_HEADS = 8
A_HEAD_DIM = 128
A_CHUNK = 64
B_WIDTH = 1024
B_HEAD_DIM = 64
B_GROUP = 4
B_KV_HEADS = 4
B_KV_WIDTH = 256
BLOCK = 128
MLP_HIDDEN = 8192
IN_WIDTH = 9728
N_MOD = 6
EPS = 1e-6
N_CHIPS = 4
N_DEV = 8

OFF_QA, OFF_FA, OFF_IA, OFF_GA = 0, 1024, 2048, 3072
OFF_QB, OFF_KB, OFF_VB = 4096, 5120, 5376
OFF_GATE_A, OFF_GATE_B = 5632, 7680

ADAM_LR = 0.001
ADAM_B1 = 0.9
ADAM_B2 = 0.999
ADAM_EPS = 1e-08
ADAM_WD = 0.01
ADAM_STEP = 10

VMEM_LIMIT_V7X = 48 * 1024 * 1024
NEG_BIG = -1e30


def _params(sem=None, vmem=VMEM_LIMIT_V7X):
    return pltpu.CompilerParams(dimension_semantics=sem, vmem_limit_bytes=vmem)


class _Plan:
    def __init__(self, ins, outs, sems, stages, aliases=None, mid_at=()):
        self.ins, self.outs, self.sems, self.stages, self.aliases = ins, outs, sems, stages, aliases or {}
        self.mid_at = tuple(mid_at)
        assert len(self.mid_at) == len(stages) - 2


def _join(a, b):
    assert len(a.stages) == 2 and len(b.stages) == 2
    ni, no, ns = len(a.ins), len(a.outs), len(a.sems)

    def stage(k):
        def run(pi, po, ps):
            a.stages[k](pi[:ni], po[:no], ps[:ns])
            b.stages[k](pi[ni:], po[no:], ps[ns:])
        return run

    aliases = dict(a.aliases)
    aliases.update({ni + i: no + o for i, o in b.aliases.items()})
    return _Plan(a.ins + b.ins, a.outs + b.outs, a.sems + b.sems, [stage(0), stage(1)], aliases)


def _pcall(body, plan=None, **kw):
    if plan is None:
        return pl.pallas_call(body, **kw)
    grid = kw["grid"]
    single = not isinstance(kw["out_specs"], (list, tuple))
    in_specs = list(kw["in_specs"])
    out_specs = [kw["out_specs"]] if single else list(kw["out_specs"])
    out_shape = [kw["out_shape"]] if single else list(kw["out_shape"])
    scratch = list(kw.get("scratch_shapes", ()))
    n_in, n_out, n_scr = len(in_specs), len(out_specs), len(scratch)
    n_pi, n_po = len(plan.ins), len(plan.outs)
    total = 1
    for g in grid:
        total *= g
    n_st = len(plan.stages)

    def wrapped(*refs):
        o0 = n_in + n_pi
        s0 = o0 + n_out + n_po
        pi, po, ps = refs[n_in:o0], refs[o0 + n_out:s0], refs[s0 + n_scr:]
        lin = 0
        for d, g in enumerate(grid):
            lin = lin * g + pl.program_id(d)
        for si, frac in enumerate((0.0,) + plan.mid_at):
            @pl.when(lin == int(frac * (total - 1)))
            def _(si=si):
                plan.stages[si](pi, po, ps)
        body(*refs[:n_in], *refs[o0:o0 + n_out], *refs[s0:s0 + n_scr])

        @pl.when(lin == total - 1)
        def _():
            plan.stages[-1](pi, po, ps)

    any_spec = pl.BlockSpec(memory_space=pl.ANY)
    call = pl.pallas_call(
        wrapped, name=kw["name"], grid=grid, in_specs=in_specs + [any_spec] * n_pi,
        out_specs=out_specs + [any_spec] * n_po, out_shape=out_shape + list(plan.outs),
        scratch_shapes=scratch + list(plan.sems),
        input_output_aliases={n_in + i: n_out + o for i, o in plan.aliases.items()},
        compiler_params=_params(("arbitrary",) * len(grid)))

    def run(*args):
        res = call(*args, *plan.ins)
        outs = list(res[:n_out])
        return (outs[0] if single else outs), list(res[n_out:])

    return run


def _run_plan(plan, name):
    return _pcall(lambda: None, plan=plan, name=name, grid=(1,), in_specs=[], out_specs=[], out_shape=[])()[1]


def _sig(x):
    return 1.0 / (1.0 + jnp.exp(-x))


def _nn(a, b):
    return lax.dot_general(a.astype(BF16), b.astype(BF16), (((1,), (0,)), ((), ())), preferred_element_type=F32)


def _nt(a, b):
    return lax.dot_general(a.astype(BF16), b.astype(BF16), (((1,), (1,)), ((), ())), preferred_element_type=F32)


def _tn(a, b):
    return lax.dot_general(a.astype(BF16), b.astype(BF16), (((0,), (0,)), ((), ())), preferred_element_type=F32)


def _mm(a, b, *, name, ta=False, tb=False, bm=1024, bn=1024, bk=2048, out_dtypes=(F32,), epi=None, extras=(),
        extra_cols=None, plan=None):
    if ta:
        K, M = a.shape
        bk = K
    else:
        M, K = a.shape
    if tb:
        N, K2 = b.shape
    else:
        K2, N = b.shape
    bm, bn, bk = min(bm, M), min(bn, N), min(bk, K)
    assert K == K2 and M % bm == 0 and N % bn == 0 and K % bk == 0, (name, a.shape, b.shape)
    nk = K // bk
    a_spec = pl.BlockSpec((bk, bm), lambda i, j, k: (k, i)) if ta else pl.BlockSpec((bm, bk), lambda i, j, k: (i, k))
    b_spec = pl.BlockSpec((bn, bk), lambda i, j, k: (j, k)) if tb else pl.BlockSpec((bk, bn), lambda i, j, k: (k, j))
    t_spec = pl.BlockSpec((bm, bn), lambda i, j, k: (i, j))
    extra_cols = extra_cols or (0,) * len(extras)
    e_specs = [pl.BlockSpec((bm, bn), lambda i, j, k, off=off: (i, off + j)) for off in extra_cols]
    dims = (((1,), (1 if tb else 0,)), ((), ()))
    n_e, n_o = len(extras), len(out_dtypes)

    def body(*refs):
        a_ref, b_ref = refs[0], refs[1]
        e_refs = refs[2:2 + n_e]
        o_refs = refs[2 + n_e:2 + n_e + n_o]

        def finish(acc):
            outs = (acc,) if epi is None else epi(acc, *[e[...] for e in e_refs])
            for o_ref, o in zip(o_refs, outs):
                o_ref[...] = o.astype(o_ref.dtype)

        if ta:
            at_ref = refs[-1]

            @pl.when(pl.program_id(1) == 0)
            def _():
                at_ref[...] = a_ref[...].T

            lhs = at_ref[...]
        else:
            lhs = a_ref[...].astype(BF16)
        part = lax.dot_general(lhs, b_ref[...].astype(BF16), dims, preferred_element_type=F32)
        if nk == 1:
            finish(part)
        else:
            acc_ref = refs[-1]
            k = pl.program_id(2)

            @pl.when(k == 0)
            def _():
                acc_ref[...] = part

            @pl.when(k > 0)
            def _():
                acc_ref[...] += part

            @pl.when(k == nk - 1)
            def _():
                finish(acc_ref[...])

    if ta:
        assert a.dtype == BF16 and nk == 1
        scratch = [pltpu.VMEM((bm, bk), BF16)]
    else:
        scratch = [pltpu.VMEM((bm, bn), F32)] if nk > 1 else []
    out = _pcall(
        body, plan=plan, name=name, grid=(M // bm, N // bn, nk),
        in_specs=[a_spec, b_spec] + e_specs,
        out_specs=[t_spec] * n_o,
        out_shape=[jax.ShapeDtypeStruct((M, N), dt) for dt in out_dtypes],
        scratch_shapes=scratch,
        compiler_params=_params(("parallel", "arbitrary", "arbitrary")),
    )(a, b, *extras)
    if plan is not None:
        return (out[0][0] if n_o == 1 else out[0]), out[1]
    return out[0] if n_o == 1 else out


def _ada_fwd(c_all, w_ada, b_cols):
    n = w_ada.shape[1]
    bn = 512

    def body(c_ref, w_ref, b_ref, o_ref):
        cv = c_ref[...]
        sc = cv * _sig(cv)
        o_ref[...] = jnp.dot(sc, w_ref[...], precision=HIGHEST, preferred_element_type=F32) + b_ref[...]

    return _pcall(
        body, name="ada_fwd", grid=(n // bn,),
        in_specs=[pl.BlockSpec((N_DEV, D_MODEL), lambda j: (0, 0)), pl.BlockSpec((D_MODEL, bn), lambda j: (0, j)),
                  pl.BlockSpec((1, bn), lambda j: (0, j))],
        out_specs=pl.BlockSpec((N_DEV, bn), lambda j: (0, j)),
        out_shape=jax.ShapeDtypeStruct((N_DEV, n), F32),
        compiler_params=_params(("parallel",)),
    )(c_all, w_ada, b_cols)


ROWS_EW = 256


def _rms_fwd_math(x, gain, scale, shift):
    rstd = lax.rsqrt(jnp.mean(x * x, axis=-1, keepdims=True) + EPS)
    xhat = x * rstd
    n = xhat * gain
    return n * (1.0 + scale) + shift, xhat, n, rstd


def _rms_bwd_math(dh, xhat, n, rstd, gain, scale):
    dn = dh * (1.0 + scale)
    dxhat = dn * gain
    dx = rstd * (dxhat - xhat * jnp.mean(dxhat * xhat, axis=-1, keepdims=True))
    d_scale = jnp.sum(dh * n, axis=0, keepdims=True)
    d_shift = jnp.sum(dh, axis=0, keepdims=True)
    d_gain = jnp.sum(dn * xhat, axis=0, keepdims=True)
    return dx, d_scale, d_shift, d_gain


def _row_spec(w=D_MODEL, br=ROWS_EW):
    return pl.BlockSpec((br, w), lambda i: (i, 0))


def _vec_spec(r=8, w=D_MODEL):
    return pl.BlockSpec((r, w), lambda i: (0, 0))


def _norm1_fwd(x, gain, mod8, plan=None):
    T = x.shape[0]

    def body(x_ref, g_ref, m_ref, h_ref):
        h, _, _, _ = _rms_fwd_math(x_ref[...], g_ref[...], m_ref[1:2, :], m_ref[0:1, :])
        h_ref[...] = h.astype(BF16)

    return _pcall(
        body, plan=plan, name="norm1_fwd", grid=(T // ROWS_EW,),
        in_specs=[_row_spec(), _vec_spec(1), _vec_spec()],
        out_specs=_row_spec(), out_shape=jax.ShapeDtypeStruct((T, D_MODEL), BF16),
        compiler_params=_params(("parallel",)),
    )(x, gain, mod8)


def _res_norm2_fwd(x, mo, gain, mod8):
    T = x.shape[0]

    def body(x_ref, mo_ref, g_ref, m_ref, x1_ref, h_ref):
        x1 = x_ref[...] + m_ref[2:3, :] * mo_ref[...]
        x1_ref[...] = x1
        h, _, _, _ = _rms_fwd_math(x1, g_ref[...], m_ref[4:5, :], m_ref[3:4, :])
        h_ref[...] = h.astype(BF16)

    return _pcall(
        body, name="res_norm2_fwd", grid=(T // ROWS_EW,),
        in_specs=[_row_spec(), _row_spec(), _vec_spec(1), _vec_spec()],
        out_specs=[_row_spec(), _row_spec()],
        out_shape=[jax.ShapeDtypeStruct((T, D_MODEL), F32), jax.ShapeDtypeStruct((T, D_MODEL), BF16)],
        compiler_params=_params(("parallel",)),
    )(x, mo, gain, mod8)


def _loss_bwd(x1, mlp, target, mod8):
    T = x1.shape[0]

    def body(x1_ref, mlp_ref, t_ref, m_ref, dy_ref, dmlp_ref, st_ref):
        i = pl.program_id(0)
        gate = m_ref[5:6, :]
        mlp_v = mlp_ref[...]
        err = x1_ref[...] + gate * mlp_v - t_ref[...]
        dy = err * (1.0 / D_MODEL)
        dy_ref[...] = dy
        dmlp_ref[...] = (dy * gate).astype(BF16)

        @pl.when(i == 0)
        def _():
            st_ref[...] = jnp.zeros_like(st_ref)

        st_ref[0:1, :] += jnp.sum(err * err, axis=0, keepdims=True)
        st_ref[1:2, :] += jnp.sum(dy * mlp_v, axis=0, keepdims=True)

    return _pcall(
        body, name="loss_bwd", grid=(T // ROWS_EW,),
        in_specs=[_row_spec(), _row_spec(), _row_spec(), _vec_spec()],
        out_specs=[_row_spec(), _row_spec(), _vec_spec()],
        out_shape=[jax.ShapeDtypeStruct((T, D_MODEL), F32), jax.ShapeDtypeStruct((T, D_MODEL), BF16),
                   jax.ShapeDtypeStruct((8, D_MODEL), F32)],
        compiler_params=_params(("arbitrary",)),
    )(x1, mlp, target, mod8)


def _norm2_bwd(dh2, x1, dy, mo, gain, mod8):
    T = x1.shape[0]

    def body(dh_ref, x1_ref, dy_ref, mo_ref, g_ref, m_ref, dx1_ref, dmo_ref, st_ref):
        i = pl.program_id(0)
        gain_v, scale = g_ref[...], m_ref[4:5, :]
        _, xhat, n, rstd = _rms_fwd_math(x1_ref[...], gain_v, scale, m_ref[3:4, :])
        dx, d_scale, d_shift, d_gain = _rms_bwd_math(dh_ref[...], xhat, n, rstd, gain_v, scale)
        dx1 = dy_ref[...] + dx
        dx1_ref[...] = dx1
        dmo_ref[...] = (dx1 * m_ref[2:3, :]).astype(BF16)

        @pl.when(i == 0)
        def _():
            st_ref[...] = jnp.zeros_like(st_ref)

        st_ref[0:1, :] += d_scale
        st_ref[1:2, :] += d_shift
        st_ref[2:3, :] += d_gain
        st_ref[3:4, :] += jnp.sum(dx1 * mo_ref[...], axis=0, keepdims=True)

    return _pcall(
        body, name="norm2_bwd", grid=(T // ROWS_EW,),
        in_specs=[_row_spec(), _row_spec(), _row_spec(), _row_spec(), _vec_spec(1), _vec_spec()],
        out_specs=[_row_spec(), _row_spec(), _vec_spec()],
        out_shape=[jax.ShapeDtypeStruct((T, D_MODEL), F32), jax.ShapeDtypeStruct((T, D_MODEL), BF16),
                   jax.ShapeDtypeStruct((8, D_MODEL), F32)],
        compiler_params=_params(("arbitrary",)),
    )(dh2, x1, dy, mo, gain, mod8)


def _norm1_bwd(dh, x, dx1, gain, mod8):
    T = x.shape[0]

    def body(dh_ref, x_ref, dx1_ref, g_ref, m_ref, dx_ref, st_ref):
        i = pl.program_id(0)
        gain_v, scale = g_ref[...], m_ref[1:2, :]
        _, xhat, n, rstd = _rms_fwd_math(x_ref[...], gain_v, scale, m_ref[0:1, :])
        dx, d_scale, d_shift, d_gain = _rms_bwd_math(dh_ref[...], xhat, n, rstd, gain_v, scale)
        dx_ref[...] = dx1_ref[...] + dx

        @pl.when(i == 0)
        def _():
            st_ref[...] = jnp.zeros_like(st_ref)

        st_ref[0:1, :] += d_scale
        st_ref[1:2, :] += d_shift
        st_ref[2:3, :] += d_gain

    return _pcall(
        body, name="norm1_bwd", grid=(T // ROWS_EW,),
        in_specs=[_row_spec(), _row_spec(), _row_spec(), _vec_spec(1), _vec_spec()],
        out_specs=[_row_spec(), _vec_spec()],
        out_shape=[jax.ShapeDtypeStruct((T, D_MODEL), F32), jax.ShapeDtypeStruct((8, D_MODEL), F32)],
        compiler_params=_params(("arbitrary",)),
    )(dh, x, dx1, gain, mod8)


MERGE_BC = 512


def _hgrn_rows(T):
    return 512 if T >= 1024 else 128


def _lower_bound(lbl):
    e = jnp.exp(lbl - jnp.max(lbl, axis=0, keepdims=True))
    return e[0:1, :] / (e[0:1, :] + e[1:2, :])


def _chunk_sum_matrix(rows, backward):
    shift = A_CHUNK.bit_length() - 1
    r = lax.broadcasted_iota(jnp.int32, (rows, rows), 0)
    c = lax.broadcasted_iota(jnp.int32, (rows, rows), 1)
    same = jnp.right_shift(r, shift) == jnp.right_shift(c, shift)
    return (same & ((r <= c) if backward else (r >= c))).astype(BF16)


def _chunk_sums(m, x):
    n = x.shape[1]
    hi = x.astype(BF16)
    rest = x - hi.astype(F32)
    mid = rest.astype(BF16)
    lo = (rest - mid.astype(F32)).astype(BF16)
    y = jnp.dot(m, jnp.concatenate([hi, mid, lo], axis=1), preferred_element_type=F32)
    return y[:, 0:n] + y[:, n:2 * n] + y[:, 2 * n:3 * n]


def _hgrn_block_pre(q, fl, lb, m_fwd):
    sg = _sig(fl)
    f = lb + (1.0 - lb) * sg
    sq = _sig(q)
    return dict(sg=sg, f=f, k=1.0 - f, sq=sq, qf=q * sq, b=_chunk_sums(m_fwd, jnp.log(f)))


def _hgrn_chunk_local(pre, r):
    C = A_CHUNK
    qf, k, b = pre["qf"][r], pre["k"][r], pre["b"][r]
    causal = lax.broadcasted_iota(jnp.int32, (C, C), 0) >= lax.broadcasted_iota(jnp.int32, (C, C), 1)
    bm = b[C // 2 - 1:C // 2, :]
    bl = b[C - 1:C, :]
    e_q, e_k = jnp.exp(b - bm), jnp.exp(bm - b)
    e_b, e_l = jnp.exp(b), jnp.exp(bl - b)
    qd, kd = qf * e_q, k * e_k
    qe, ke = qf * e_b, k * e_l
    att = jnp.where(causal, _nt(qd, kd), 0.0)
    return dict(causal=causal, e_q=e_q, e_k=e_k, e_b=e_b, e_l=e_l, qd=qd, kd=kd, qe=qe, ke=ke, att=att, dec=jnp.exp(bl))


def _hgrn_chunk_fwd(pre, r, v, st):
    c = _hgrn_chunk_local(pre, r)
    c["o"] = _nn(c["att"], v) + _nt(c["qe"], st)
    return c


def _lockstep(gens):
    out = [None] * len(gens)
    live = list(enumerate(gens))
    while live:
        still = []
        for i, g in live:
            try:
                next(g)
                still.append((i, g))
            except StopIteration as done:
                out[i] = done.value
        live = still
    return out


HGRN_HEADS_PER_STEP = 4


def _hgrn_fwd(proj, lb_logits, o_gain, plan=None):
    T = proj.shape[0]
    BR = _hgrn_rows(T)
    cps = BR // A_CHUNK
    K, NH = A_HEAD_DIM, HGRN_HEADS_PER_STEP
    W = NH * K

    def col(off):
        return pl.BlockSpec((BR, W), lambda h, cb: (cb, off // W + h))

    def body(q_ref, f_ref, i_ref, g_ref, lbl_ref, og_ref, o_ref, s_ref, st):
        @pl.when(pl.program_id(1) == 0)
        def _():
            st[...] = jnp.zeros_like(st)

        lb_all = _lower_bound(lbl_ref[...])
        m_fwd = _chunk_sum_matrix(BR, False)
        pre = [_hgrn_block_pre(q_ref[:, n * K:(n + 1) * K], f_ref[:, n * K:(n + 1) * K], lb_all[:, n * K:(n + 1) * K], m_fwd)
               for n in range(NH)]
        def local(n, ci):
            r, hs = slice(ci * A_CHUNK, (ci + 1) * A_CHUNK), slice(n * K, (n + 1) * K)
            v = i_ref[r, hs]
            c = _hgrn_chunk_local(pre[n], r)
            yield
            return dict(o=_nn(c["att"], v), ds=_tn(v, c["ke"]), qe=c["qe"], dec=c["dec"])

        def chain(n, loc):
            hs = slice(n * K, (n + 1) * K)
            state = st[n]
            for ci, p in enumerate(loc):
                r = slice(ci * A_CHUNK, (ci + 1) * A_CHUNK)
                s_ref[n, ci] = state
                o = p["o"] + _nt(p["qe"], state)
                state = state * p["dec"] + p["ds"]
                yield
                on = o * lax.rsqrt(jnp.mean(o * o, axis=-1, keepdims=True) + EPS)
                g = g_ref[r, hs]
                o_ref[r, hs] = (on * og_ref[:, hs] * (g * _sig(g))).astype(BF16)
            st[n] = state

        loc = _lockstep([local(n, ci) for n in range(NH) for ci in range(cps)])
        _lockstep([chain(n, loc[n * cps:(n + 1) * cps]) for n in range(NH)])

    return _pcall(
        body, plan=plan, name="hgrn_fwd", grid=(A_HEADS // NH, T // BR),
        in_specs=[col(OFF_QA), col(OFF_FA), col(OFF_IA), col(OFF_GA),
                  pl.BlockSpec((2, W), lambda h, cb: (0, h)), pl.BlockSpec((1, W), lambda h, cb: (0, h))],
        out_specs=[pl.BlockSpec((BR, W), lambda h, cb: (cb, h)),
                   pl.BlockSpec((NH, cps, K, K), lambda h, cb: (h, cb, 0, 0))],
        out_shape=[jax.ShapeDtypeStruct((T, A_WIDTH), BF16),
                   jax.ShapeDtypeStruct((A_HEADS, T // A_CHUNK, K, K), F32)],
        scratch_shapes=[pltpu.VMEM((NH, K, K), F32)],
        compiler_params=_params(("parallel", "arbitrary")),
    )(proj, proj, proj, proj, lb_logits, o_gain)


def _hgrn_bwd(proj, lb_logits, o_gain, states, do, plan=None):
    T = proj.shape[0]
    BR = _hgrn_rows(T)
    cps = BR // A_CHUNK
    ncb = T // BR
    K, C, NH = A_HEAD_DIM, A_CHUNK, HGRN_HEADS_PER_STEP
    W = NH * K

    def col(off):
        return pl.BlockSpec((BR, W), lambda h, cb: (ncb - 1 - cb, off // W + h))

    def body(q_ref, f_ref, i_ref, g_ref, lbl_ref, og_ref, s_ref, do_ref,
             dq_ref, df_ref, di_ref, dg_ref, dlb_ref, dog_ref, dst):
        @pl.when(pl.program_id(1) == 0)
        def _():
            dst[...] = jnp.zeros_like(dst)
            dlb_ref[...] = jnp.zeros_like(dlb_ref)
            dog_ref[...] = jnp.zeros_like(dog_ref)

        lb_all = _lower_bound(lbl_ref[...])
        row = lax.broadcasted_iota(jnp.int32, (C, K), 0)
        m_fwd, m_bwd = _chunk_sum_matrix(BR, False), _chunk_sum_matrix(BR, True)
        pre = [_hgrn_block_pre(q_ref[:, n * K:(n + 1) * K], f_ref[:, n * K:(n + 1) * K], lb_all[:, n * K:(n + 1) * K], m_fwd)
               for n in range(NH)]
        def local(n, ci):
            r, hs = slice(ci * C, (ci + 1) * C), slice(n * K, (n + 1) * K)
            gain = og_ref[:, hs]
            st = s_ref[n, ci]
            v = i_ref[r, hs]
            q = q_ref[r, hs]
            c = _hgrn_chunk_fwd(pre[n], r, v, st)
            yield
            o = c["o"]
            rn = lax.rsqrt(jnp.mean(o * o, axis=-1, keepdims=True) + EPS)
            on = o * rn
            g = g_ref[r, hs]
            sgg = _sig(g)
            dy = do_ref[r, hs]
            d_ong = dy * (g * sgg)
            dg_ref[r, hs] = (dy * (on * gain) * (sgg * (1.0 + g * (1.0 - sgg)))).astype(BF16)
            d_on = d_ong * gain
            d_o = rn * (d_on - on * jnp.mean(d_on * on, axis=-1, keepdims=True))
            datt = jnp.where(c["causal"], _nt(d_o, v), 0.0)
            dqe = _nn(d_o, st)
            yield
            dqd = _nn(datt, c["kd"])
            dkd = _tn(datt, c["qd"])
            dv = _tn(c["att"], d_o)
            ds = _tn(d_o, c["qe"])
            yield
            t_q, t_k = dqd * c["qd"], dkd * c["kd"]
            sq = pre[n]["sq"][r]
            dq_ref[r, hs] = ((dqd * c["e_q"] + dqe * c["e_b"]) * (sq * (1.0 + q * (1.0 - sq)))).astype(BF16)
            return dict(v=v, st=st, ke=c["ke"], e_l=c["e_l"], dec=c["dec"], dv=dv, ds=ds, dk=dkd * c["e_k"],
                        db=t_q - t_k + dqe * c["qe"], dbm=jnp.sum(t_k - t_q, axis=0, keepdims=True),
                        d_og=jnp.sum(d_ong * on, axis=0, keepdims=True))

        def chain(n, loc):
            hs = slice(n * K, (n + 1) * K)
            dst_next = dst[n]
            db_of, dk_of = [None] * cps, [None] * cps
            for ci in reversed(range(cps)):
                p = loc[ci]
                di_ref[ci * C:(ci + 1) * C, hs] = (p["dv"] + _nt(p["ke"], dst_next)).astype(BF16)
                dke = _nn(p["v"], dst_next)
                yield
                t_l = dke * p["ke"]
                dbl = jnp.sum(t_l, axis=0, keepdims=True) + jnp.sum(dst_next * p["st"], axis=0, keepdims=True) * p["dec"]
                db_of[ci] = p["db"] - t_l + jnp.where(row == C // 2 - 1, p["dbm"], 0.0) + jnp.where(row == C - 1, dbl, 0.0)
                dk_of[ci] = p["dk"] + dke * p["e_l"]
                dst_next = dst_next * p["dec"] + p["ds"]
            dst[n] = dst_next
            return db_of, dk_of

        loc = _lockstep([local(n, ci) for n in range(NH) for ci in range(cps)])
        loc = [loc[n * cps:(n + 1) * cps] for n in range(NH)]
        chains = _lockstep([chain(n, loc[n]) for n in range(NH)])
        for n in range(NH):
            hs = slice(n * K, (n + 1) * K)
            db_of, dk_of = chains[n]
            d_og = loc[n][0]["d_og"]
            for p in loc[n][1:]:
                d_og = d_og + p["d_og"]
            dog_ref[0:1, hs] += d_og
            lb, sg = lb_all[:, hs], pre[n]["sg"]
            dlf = _chunk_sums(m_bwd, jnp.concatenate(db_of, axis=0))
            df = dlf / pre[n]["f"] - jnp.concatenate(dk_of, axis=0)
            df_ref[:, hs] = (df * (1.0 - lb) * sg * (1.0 - sg)).astype(BF16)
            dlb_ref[0:1, hs] += jnp.sum(df * (1.0 - sg), axis=0, keepdims=True)

    ocol = pl.BlockSpec((BR, W), lambda h, cb: (ncb - 1 - cb, h))
    vec = pl.BlockSpec((8, W), lambda h, cb: (0, h))
    return _pcall(
        body, plan=plan, name="hgrn_bwd", grid=(A_HEADS // NH, ncb),
        in_specs=[col(OFF_QA), col(OFF_FA), col(OFF_IA), col(OFF_GA),
                  pl.BlockSpec((2, W), lambda h, cb: (0, h)), pl.BlockSpec((1, W), lambda h, cb: (0, h)),
                  pl.BlockSpec((NH, cps, K, K), lambda h, cb: (h, ncb - 1 - cb, 0, 0)),
                  pl.BlockSpec((BR, W), lambda h, cb: (ncb - 1 - cb, h))],
        out_specs=[ocol, ocol, ocol, ocol, vec, vec],
        out_shape=[jax.ShapeDtypeStruct((T, A_WIDTH), BF16)] * 4 + [jax.ShapeDtypeStruct((8, A_WIDTH), F32)] * 2,
        scratch_shapes=[pltpu.VMEM((NH, K, K), F32)],
        compiler_params=_params(("parallel", "arbitrary")),
    )(proj, proj, proj, proj, lb_logits, o_gain, states, do)


def _head_norm(x):
    r = lax.rsqrt(jnp.mean(x * x, axis=-1, keepdims=True) + EPS)
    return x * r, r


def _head_norm_bwd(dy, xn, r, gain):
    dxn = dy * gain
    return r * (dxn - xn * jnp.mean(dxn * xn, axis=-1, keepdims=True)), jnp.sum(dy * xn, axis=0, keepdims=True)


def _swa_mask(has_prev):
    rows = B_GROUP * BLOCK
    r = lax.broadcasted_iota(jnp.int32, (rows, 2 * BLOCK), 0) % BLOCK
    c = lax.broadcasted_iota(jnp.int32, (rows, 2 * BLOCK), 1)
    rel = r + BLOCK - c
    return (rel >= 0) & (rel < BLOCK) & ((c >= BLOCK) | has_prev)


def _swa_head_fwd(j, q_ref, kp_ref, kc_ref, vp_ref, vc_ref, qg, kg, sk_ref, mask):
    hs = slice(j * B_HEAD_DIM, (j + 1) * B_HEAD_DIM)
    kcat = jnp.concatenate([kp_ref[:, hs], kc_ref[:, hs]], axis=0)
    vcat = jnp.concatenate([vp_ref[:, hs], vc_ref[:, hs]], axis=0)
    qs = jnp.concatenate([q_ref[:, pl.ds((j * B_GROUP + g) * B_HEAD_DIM, B_HEAD_DIM)] for g in range(B_GROUP)], axis=0)
    kn, kr = _head_norm(kcat)
    qn, qr = _head_norm(qs)
    kh, qh = kn * kg, qn * qg
    yield
    s = jnp.where(mask, _nt(qh, kh) * (B_HEAD_DIM ** -0.5), NEG_BIG)
    yield
    sink = jnp.concatenate(
        [jnp.broadcast_to(sk_ref[0:1, pl.ds(j * B_GROUP + g, 1)], (BLOCK, 1)) for g in range(B_GROUP)], axis=0)
    m = jnp.maximum(jnp.max(s, axis=-1, keepdims=True), sink)
    p = jnp.exp(s - m)
    e_sink = jnp.exp(sink - m)
    inv = 1.0 / (jnp.sum(p, axis=-1, keepdims=True) + e_sink)
    prob = p * inv
    return dict(vcat=vcat, kn=kn, kr=kr, qn=qn, qr=qr, kh=kh, qh=qh, prob=prob, p_sink=e_sink * inv)


def _swa_in_specs(nb, last):
    def qi(n):
        return jnp.minimum(n, last)

    q = pl.BlockSpec((BLOCK, B_WIDTH), lambda n: (qi(n), OFF_QB // B_WIDTH))
    kc = pl.BlockSpec((BLOCK, B_KV_WIDTH), lambda n: (qi(n), OFF_KB // B_KV_WIDTH))
    kp = pl.BlockSpec((BLOCK, B_KV_WIDTH), lambda n: (jnp.maximum(qi(n) - 1, 0), OFF_KB // B_KV_WIDTH))
    vc = pl.BlockSpec((BLOCK, B_KV_WIDTH), lambda n: (qi(n), OFF_VB // B_KV_WIDTH))
    vp = pl.BlockSpec((BLOCK, B_KV_WIDTH), lambda n: (jnp.maximum(qi(n) - 1, 0), OFF_VB // B_KV_WIDTH))
    small = [pl.BlockSpec((1, B_HEAD_DIM), lambda n: (0, 0)), pl.BlockSpec((1, B_HEAD_DIM), lambda n: (0, 0)),
             pl.BlockSpec((1, B_GROUP * B_KV_HEADS), lambda n: (0, 0))]
    return [q, kp, kc, vp, vc] + small


def _swa_fwd(proj, q_gain, k_gain, sinks, plan=None):
    T = proj.shape[0]
    nb = T // BLOCK

    def body(q_ref, kp_ref, kc_ref, vp_ref, vc_ref, qg_ref, kg_ref, sk_ref, o_ref):
        mask = _swa_mask(pl.program_id(0) > 0)

        def head(j):
            c = yield from _swa_head_fwd(j, q_ref, kp_ref, kc_ref, vp_ref, vc_ref, qg_ref[...], kg_ref[...], sk_ref, mask)
            yield
            o = _nn(c["prob"], c["vcat"])
            yield
            for g in range(B_GROUP):
                o_ref[:, pl.ds((j * B_GROUP + g) * B_HEAD_DIM, B_HEAD_DIM)] = o[g * BLOCK:(g + 1) * BLOCK].astype(BF16)

        _lockstep([head(j) for j in range(B_KV_HEADS)])

    return _pcall(
        body, plan=plan, name="swa_fwd", grid=(nb,),
        in_specs=_swa_in_specs(nb, nb - 1),
        out_specs=pl.BlockSpec((BLOCK, B_WIDTH), lambda n: (n, 0)),
        out_shape=jax.ShapeDtypeStruct((T, B_WIDTH), BF16),
        compiler_params=_params(("parallel",)),
    )(proj, proj, proj, proj, proj, q_gain, k_gain, sinks)


def _swa_bwd(proj, q_gain, k_gain, sinks, do, plan=None):
    T = proj.shape[0]
    nb = T // BLOCK
    scale = B_HEAD_DIM ** -0.5

    def body(q_ref, kp_ref, kc_ref, vp_ref, vc_ref, qg_ref, kg_ref, sk_ref, do_ref,
             dq_ref, dkv_ref, sm_ref, ck, cv):
        n = pl.program_id(0)

        @pl.when(n == 0)
        def _():
            ck[...] = jnp.zeros_like(ck)
            cv[...] = jnp.zeros_like(cv)
            sm_ref[...] = jnp.zeros_like(sm_ref)

        @pl.when(n < nb)
        def _():
            mask = _swa_mask(n > 0)
            qg, kg = qg_ref[...], kg_ref[...]
            lane = lax.broadcasted_iota(jnp.int32, (1, BLOCK), 1)
            def head(j):
                hs = slice(j * B_HEAD_DIM, (j + 1) * B_HEAD_DIM)
                vs = slice(B_KV_WIDTH + j * B_HEAD_DIM, B_KV_WIDTH + (j + 1) * B_HEAD_DIM)
                c = yield from _swa_head_fwd(j, q_ref, kp_ref, kc_ref, vp_ref, vc_ref, qg, kg, sk_ref, mask)
                d_out = jnp.concatenate(
                    [do_ref[:, pl.ds((j * B_GROUP + g) * B_HEAD_DIM, B_HEAD_DIM)] for g in range(B_GROUP)], axis=0)
                prob = c["prob"]
                yield
                out = _nn(prob, c["vcat"])
                d_prob = _nt(d_out, c["vcat"])
                dv = _tn(prob, d_out)
                yield
                delta = jnp.sum(d_out * out, axis=-1, keepdims=True)
                ds = prob * (d_prob - delta)
                d_sink = -c["p_sink"] * delta
                yield
                dqh = _nn(ds, c["kh"]) * scale
                dkh = _tn(ds, c["qh"]) * scale
                yield
                dq, dqg = _head_norm_bwd(dqh, c["qn"], c["qr"], qg)
                dk, dkg = _head_norm_bwd(dkh, c["kn"], c["kr"], kg)
                d_sinks = jnp.zeros((1, BLOCK), F32)
                for g in range(B_GROUP):
                    dq_ref[:, pl.ds((j * B_GROUP + g) * B_HEAD_DIM, B_HEAD_DIM)] = dq[g * BLOCK:(g + 1) * BLOCK].astype(BF16)
                    tot = jnp.sum(d_sink[g * BLOCK:(g + 1) * BLOCK], axis=0, keepdims=True)
                    d_sinks = d_sinks + jnp.where(lane == j * B_GROUP + g, tot, 0.0)
                dkv_ref[:, hs] = (ck[:, hs] + dk[0:BLOCK]).astype(BF16)
                dkv_ref[:, vs] = (cv[:, hs] + dv[0:BLOCK]).astype(BF16)
                ck[:, hs] = dk[BLOCK:2 * BLOCK]
                cv[:, hs] = dv[BLOCK:2 * BLOCK]
                return dqg, dkg, d_sinks

            small = _lockstep([head(j) for j in range(B_KV_HEADS)])
            sm_ref[0:1, 0:B_HEAD_DIM] += small[0][0] + small[1][0] + small[2][0] + small[3][0]
            sm_ref[1:2, 0:B_HEAD_DIM] += small[0][1] + small[1][1] + small[2][1] + small[3][1]
            sm_ref[2:3, :] += small[0][2] + small[1][2] + small[2][2] + small[3][2]

        @pl.when(n == nb)
        def _():
            dkv_ref[:, 0:B_KV_WIDTH] = ck[...].astype(BF16)
            dkv_ref[:, B_KV_WIDTH:2 * B_KV_WIDTH] = cv[...].astype(BF16)

    return _pcall(
        body, plan=plan, name="swa_bwd", grid=(nb + 1,),
        in_specs=_swa_in_specs(nb, nb - 1) + [pl.BlockSpec((BLOCK, B_WIDTH), lambda n: (jnp.minimum(n, nb - 1), 0))],
        out_specs=[pl.BlockSpec((BLOCK, B_WIDTH), lambda n: (jnp.minimum(n, nb - 1), 0)),
                   pl.BlockSpec((BLOCK, 2 * B_KV_WIDTH), lambda n: (jnp.maximum(n - 1, 0), 0)),
                   pl.BlockSpec((8, BLOCK), lambda n: (0, 0))],
        out_shape=[jax.ShapeDtypeStruct((T, B_WIDTH), BF16), jax.ShapeDtypeStruct((T, 2 * B_KV_WIDTH), BF16),
                   jax.ShapeDtypeStruct((8, BLOCK), F32)],
        scratch_shapes=[pltpu.VMEM((BLOCK, B_KV_WIDTH), F32), pltpu.VMEM((BLOCK, B_KV_WIDTH), F32)],
        compiler_params=_params(("arbitrary",)),
    )(proj, proj, proj, proj, proj, q_gain, k_gain, sinks, do)


W_IN, W_A, W_B, W_OUT, W_MI, W_MO = range(6)
DEEP_TILES = dict(bm=512, bn=512, bk=1 << 14)


def _local_step(x, target, mod8, norm1_gain, norm2_gain, lb_logits, o_gain, q_gain, k_gain, sinks, shards, c_arr, chip_arr):
    relu2 = lambda u: (u, jnp.square(jnp.maximum(u, 0.0)))
    pair, half = {}, {}

    def exchange(ws, grads):
        return _sibling_exchange_plan([_grad_view(g, w) for w, g in zip(ws, grads)])

    def pair_sums(ws, grads, others):
        for w, g, o in zip(ws, grads, others):
            pair[w] = _pair_sum(_grad_view(g, w), o, c_arr, f"pair_sum{w}")

    def sum_slots(ws, slots):
        for w, s in zip(ws, slots):
            half[w] = _sum_slots(pair[w], s, w, chip_arr, f"sum_slots{w}")

    part_in = _cast_into_full({W_IN: shards[W_IN]}, "cast_w_in")[W_IN]
    h, (part_in,) = _norm1_fwd(x, norm1_gain, mod8, plan=_gather_plan({W_IN: part_in}, part="near"))
    parts, (w_in,) = _cast_into_full({w: shards[w] for w in range(1, N_W)}, "cast_rest",
                                     plan=_gather_plan({W_IN: part_in}, pass_at=(0.8,), part="far"))
    proj, (w_mi,) = _mm(h, w_in, name="mm_proj", bn=512, plan=_gather_plan({W_MI: parts[W_MI]}, pass_at=(0.47, 0.72)))
    (o_a, states), (w_a, w_b) = _hgrn_fwd(
        proj, lb_logits, o_gain, plan=_gather_plan({w: parts[w] for w in (W_A, W_B)}, pass_at=(0.4, 0.65)))
    o_b, (w_out,) = _swa_fwd(proj, q_gain, k_gain, sinks, plan=_gather_plan({W_OUT: parts[W_OUT]}, pass_at=(0.3, 0.5)))
    ya = _mm(o_a, w_a, name="mm_branch_a")
    gate_cols = (OFF_GATE_A // MERGE_BC, OFF_GATE_B // MERGE_BC)
    yb, merged = _mm(o_b, w_b, name="mm_branch_b", bn=MERGE_BC, out_dtypes=(F32, BF16),
                     extras=(proj, proj, ya), extra_cols=gate_cols + (0,),
                     epi=lambda acc, ga, gb, ya_: (acc, _sig(ga) * ya_ + _sig(gb) * acc))
    mo = _mm(merged, w_out, name="mm_out")
    x1, h2 = _res_norm2_fwd(x, mo, norm2_gain, mod8)
    (u, act), (w_mo,) = _mm(h2, w_mi, name="mm_mlp_in", out_dtypes=(F32, BF16), epi=relu2,
                            plan=_gather_plan({W_MO: parts[W_MO]}, pass_at=(0.6, 0.9)))
    mlp = _mm(act, w_mo, name="mm_mlp_out", **DEEP_TILES)
    dy, dmlp, st_loss = _loss_bwd(x1, mlp, target, mod8)
    g_mo = _mm(act, dmlp, name="mm_g_mlp_out", ta=True, bn=512)
    du, others = _mm(dmlp, w_mo, name="mm_d_act", tb=True, out_dtypes=(BF16,), extras=(u,),
                     epi=lambda acc, uu: (acc * (2.0 * jnp.maximum(uu, 0.0)),), plan=exchange([W_MO], [g_mo]))
    pair_sums([W_MO], [g_mo], others)
    near, far = (0, 1), (2,)
    g_mi, (part,) = _mm(h2, du, name="mm_g_mlp_in", ta=True, bn=512,
                        plan=_chip_exchange_plan({W_MO: pair[W_MO]}, near))
    dh2, res = _mm(du, w_mi, name="mm_d_h2", tb=True, **DEEP_TILES,
                   plan=_join(_chip_exchange_plan({W_MO: pair[W_MO]}, far, {W_MO: part}), exchange([W_MI], [g_mi])))
    sum_slots([W_MO], res[:1])
    pair_sums([W_MI], [g_mi], res[1:])
    dx1, dmo, st_n2 = _norm2_bwd(dh2, x1, dy, mo, norm2_gain, mod8)
    def merge_bwd(dm, ga, gb, ya_, yb_):
        sa, sb = _sig(ga), _sig(gb)
        return dm * sa, dm * sb, dm * ya_ * sa * (1.0 - sa), dm * yb_ * sb * (1.0 - sb)

    (dya, dyb, dga, dgb), (part,) = _mm(dmo, w_out, name="mm_d_merged", tb=True, bn=MERGE_BC, out_dtypes=(BF16,) * 4,
                                        extras=(proj, proj, ya, yb), extra_cols=gate_cols + (0, 0), epi=merge_bwd,
                                        plan=_chip_exchange_plan({W_MI: pair[W_MI]}, near))
    g_out = _mm(merged, dmo, name="mm_g_out", ta=True, bn=512)
    do_a = _mm(dya, w_a, name="mm_d_oa", tb=True)
    g_a = _mm(o_a, dya, name="mm_g_branch_a", ta=True, bn=512)
    do_b = _mm(dyb, w_b, name="mm_d_ob", tb=True)
    g_b = _mm(o_b, dyb, name="mm_g_branch_b", ta=True, bn=512)
    mid = [W_A, W_B, W_OUT]
    (dqb, dkvb, st_swa), res = _swa_bwd(
        proj, q_gain, k_gain, sinks, do_b,
        plan=_join(_chip_exchange_plan({W_MI: pair[W_MI]}, far, {W_MI: part}), exchange(mid, [g_a, g_b, g_out])))
    sum_slots([W_MI], res[:1])
    pair_sums(mid, [g_a, g_b, g_out], res[1:])
    (dqa, dfa, dia, dgga, d_lb, d_og), parts_mid = _hgrn_bwd(
        proj, lb_logits, o_gain, states, do_a, plan=_chip_exchange_plan({w: pair[w] for w in mid}, near))
    dproj = jnp.concatenate([dqa, dfa, dia, dgga, dqb, dkvb, dga, dgb], axis=1)
    hr = D_MODEL // 2
    h_send = lax.dynamic_slice(h, (0, (1 - c_arr[0]) * hr), (h.shape[0], hr))
    h_own = lax.dynamic_slice(h, (0, c_arr[0] * hr), (h.shape[0], hr))
    done = [W_A, W_B, W_OUT, W_MI, W_MO]
    g_send, slots_mid = _mm(h_send, dproj, name="mm_g_in_send", ta=True, bn=512,
                            plan=_chip_exchange_plan({w: pair[w] for w in mid}, far, dict(zip(mid, parts_mid))))
    sum_slots(mid, slots_mid)
    g_own, res = _mm(h_own, dproj, name="mm_g_in_own", ta=True, bn=512,
                     plan=_sibling_share_plan([g_send] + [half[w] for w in done]))
    g_other, theirs = res[0], dict(zip(done, res[1:]))
    pair[W_IN] = _add_bf16(g_own, g_other, "pair_sum0")[None]
    dh, slots_in = _mm(dproj, w_in, name="mm_d_h", tb=True, **DEEP_TILES, plan=_chip_exchange_plan({W_IN: pair[W_IN]}))
    sum_slots([W_IN], slots_in)
    grad_x, st_n1 = _norm1_bwd(dh, x, dx1, norm1_gain, mod8)
    (theirs[W_IN],) = _run_plan(_sibling_share_plan([half[W_IN]]), "sibling_share_w_in")
    stats = dict(loss=st_loss, n2=st_n2, n1=st_n1, d_lb=d_lb, d_og=d_og, swa=st_swa)
    return grad_x, [half[w] for w in range(N_W)], [theirs[w] for w in range(N_W)], stats


def _ew_rows(rows, cols):
    br = 8
    while br * 2 <= rows and br * 2 * cols * 4 <= (1 << 20) and rows % (br * 2) == 0:
        br *= 2
    return br


CAST_STEPS = 16


def _cast_into_full(shards, name, plan=None):
    ws = sorted(shards)
    in_specs, out_specs, out_shape = [], [], []
    for w in ws:
        sr, sc = shards[w].shape
        R, C, by_col = W_SHAPES[w]
        br = sr // CAST_STEPS
        assert br * CAST_STEPS == sr and br % 16 == 0, (w, sr)

        def out_map(i, by_col=by_col):
            chip = 2 * lax.axis_index("x") + lax.axis_index("y")
            return (i, chip) if by_col else (chip * CAST_STEPS + i, 0)

        in_specs.append(pl.BlockSpec((br, sc), lambda i: (i, 0)))
        out_specs.append(pl.BlockSpec((br, sc), out_map))
        out_shape.append(jax.ShapeDtypeStruct((R, C), BF16))

    def body(*refs):
        for w_ref, o_ref in zip(refs[:len(ws)], refs[len(ws):]):
            o_ref[...] = w_ref[...].astype(BF16)

    res = _pcall(body, plan=plan, name=name, grid=(CAST_STEPS,), in_specs=in_specs, out_specs=out_specs,
                 out_shape=out_shape, compiler_params=_params(("arbitrary",)))(*[shards[w] for w in ws])
    if plan is None:
        return dict(zip(ws, res))
    return dict(zip(ws, res[0])), res[1]


def _adamw_math(w, g, m, v):
    m = ADAM_B1 * m + (1.0 - ADAM_B1) * g
    v = ADAM_B2 * v + (1.0 - ADAM_B2) * (g * g)
    m_hat = m / (1.0 - ADAM_B1 ** ADAM_STEP)
    v_hat = v / (1.0 - ADAM_B2 ** ADAM_STEP)
    delta = -ADAM_LR * (m_hat / (jnp.sqrt(v_hat) + ADAM_EPS) + ADAM_WD * w)
    return delta, m, v


def _adamw(w, g, m, v, name):
    R, C = w.shape
    br = _ew_rows(R, C)
    spec = pl.BlockSpec((br, C), lambda i: (i, 0))

    def body(w_ref, g_ref, m_ref, v_ref, d_ref, nm_ref, nv_ref):
        d_ref[...], nm_ref[...], nv_ref[...] = _adamw_math(w_ref[...], g_ref[...], m_ref[...], v_ref[...])

    sh = jax.ShapeDtypeStruct((R, C), F32)
    return _pcall(body, name=name, grid=(R // br,), in_specs=[spec] * 4, out_specs=[spec] * 3, out_shape=[sh] * 3,
                  compiler_params=_params(("parallel",)))(w, g, m, v)


def _add_bf16(a, b, name):
    R, C = a.shape
    br = _ew_rows(R, C)
    spec = pl.BlockSpec((br, C), lambda i: (i, 0))

    def body(a_ref, b_ref, o_ref):
        o_ref[...] = (a_ref[...] + b_ref[...]).astype(BF16)

    return _pcall(body, name=name, grid=(R // br,), in_specs=[spec, spec], out_specs=spec,
                  out_shape=jax.ShapeDtypeStruct((R, C), BF16), compiler_params=_params(("parallel",)))(a, b)


def _adamw_halves(w, own, other, m, v, c_arr, name):
    R, C = w.shape
    hr = R // 2
    br = _ew_rows(hr, C)
    nb = hr // br
    full = pl.BlockSpec((br, C), lambda h, i, c_ref: (h * nb + i, 0))
    half = pl.BlockSpec((br, C), lambda h, i, c_ref: (i, 0))

    def body(c_ref, w_ref, own_ref, oth_ref, m_ref, v_ref, g_ref, d_ref, nm_ref, nv_ref):
        g = jnp.where(pl.program_id(0) == c_ref[0], own_ref[...], oth_ref[...])
        g_ref[...] = g
        d_ref[...], nm_ref[...], nv_ref[...] = _adamw_math(w_ref[...], g, m_ref[...], v_ref[...])

    sh = jax.ShapeDtypeStruct((R, C), F32)
    return _pcall(
        body, name=name,
        grid_spec=pltpu.PrefetchScalarGridSpec(
            num_scalar_prefetch=1, grid=(2, nb), in_specs=[full, half, half, full, full], out_specs=[full] * 4),
        out_shape=[sh] * 4, compiler_params=_params(("parallel", "parallel")))(c_arr, w, own, other, m, v)


def _ada_grad_adamw(c_t, dmod, w, m, v):
    R, C = w.shape
    br = _ew_rows(R, C)
    spec = pl.BlockSpec((br, C), lambda i: (i, 0))

    def body(c_ref, dm_ref, w_ref, m_ref, v_ref, g_ref, d_ref, nm_ref, nv_ref):
        cv = c_ref[...]
        sc = cv * _sig(cv)
        g = sc[:, 0:1] * dm_ref[0:1, :]
        for b in range(1, N_DEV):
            g = g + sc[:, b:b + 1] * dm_ref[b:b + 1, :]
        g_ref[...] = g
        d_ref[...], nm_ref[...], nv_ref[...] = _adamw_math(w_ref[...], g, m_ref[...], v_ref[...])

    sh = jax.ShapeDtypeStruct((R, C), F32)
    return _pcall(
        body, name="ada_grad_adamw", grid=(R // br,),
        in_specs=[pl.BlockSpec((br, N_DEV), lambda i: (i, 0)), pl.BlockSpec((N_DEV, C), lambda i: (0, 0)), spec, spec, spec],
        out_specs=[spec] * 4, out_shape=[sh] * 4, compiler_params=_params(("parallel",)))(c_t, dmod, w, m, v)


SMALL_ROWS = 16


def _small_sum(small_all, lb_logits):
    def body(s_ref, lbl_ref, o_ref):
        acc = s_ref[0:SMALL_ROWS, :]
        for d in range(1, N_DEV):
            acc = acc + s_ref[d * SMALL_ROWS:(d + 1) * SMALL_ROWS, :]
        o_ref[...] = acc
        z = lbl_ref[...]
        e = jnp.exp(z - jnp.max(z, axis=0, keepdims=True))
        p0 = e[0:1, :] / (e[0:1, :] + e[1:2, :])
        dz = acc[8:9, 0:A_WIDTH] * p0 * (1.0 - p0)
        o_ref[8:9, 0:A_WIDTH] = dz
        o_ref[10:11, 0:A_WIDTH] = -dz

    return _pcall(body, name="small_sum", out_shape=jax.ShapeDtypeStruct((SMALL_ROWS, D_MODEL), F32),
                  in_specs=[pl.BlockSpec(memory_space=pltpu.VMEM)] * 2, out_specs=pl.BlockSpec(memory_space=pltpu.VMEM),
                  compiler_params=_params())(small_all, lb_logits)


RELATIONS = ((1, 0), (0, 1), (1, 1))
ANY = pl.BlockSpec(memory_space=pl.ANY)


def _place():
    x, y, c = lax.axis_index("x"), lax.axis_index("y"), lax.axis_index("c")
    return x, y, c


def _allgather_small(x_shard, name):
    m_per, n = x_shard.shape

    def body(x_ref, out_ref, send_sems, recv_sems, local_sem):
        x, y, c = _place()
        me, sibling = (x, y, c), (x, y, 1 - c)
        chips = [(1 - x, y), (x, 1 - y), (1 - x, 1 - y)]

        def rows(px, py, pc):
            return out_ref.at[pl.ds((4 * px + 2 * py + pc) * m_per, m_per), :]

        def copy(k, block, to, src=None):
            return pltpu.make_async_remote_copy(
                src_ref=rows(*block) if src is None else src, dst_ref=rows(*block),
                send_sem=send_sems.at[k], recv_sem=recv_sems.at[k], device_id=to, device_id_type=MESH)

        mine = pltpu.make_async_copy(x_ref, rows(*me), local_sem)
        mine.start()
        first = [copy(0, me, sibling, src=x_ref)]
        first += [copy(1 + j, me, (*chip, c), src=x_ref) for j, chip in enumerate(chips)]
        for cp in first:
            cp.start()
        passed = [copy(4 + j, (*chip, c), sibling) for j, chip in enumerate(chips)]
        for j, chip in enumerate(chips):
            copy(1 + j, (*chip, c), me).wait_recv()
            passed[j].start()
        copy(0, sibling, me).wait_recv()
        for j, chip in enumerate(chips):
            copy(4 + j, (*chip, 1 - c), me).wait_recv()
        for cp in first + passed:
            cp.wait_send()
        mine.wait()

    return _pcall(
        body, name=name, out_shape=jax.ShapeDtypeStruct((N_DEV * m_per, n), x_shard.dtype),
        in_specs=[pl.BlockSpec(memory_space=pltpu.VMEM)], out_specs=pl.BlockSpec(memory_space=pltpu.VMEM),
        scratch_shapes=[pltpu.SemaphoreType.DMA((7,)), pltpu.SemaphoreType.DMA((7,)), pltpu.SemaphoreType.DMA],
        compiler_params=_params(),
    )(x_shard)


W_SHAPES = ((D_MODEL, IN_WIDTH, True), (A_WIDTH, D_MODEL, True), (B_WIDTH, D_MODEL, True),
            (D_MODEL, D_MODEL, False), (D_MODEL, MLP_HIDDEN, True), (MLP_HIDDEN, D_MODEL, False))
N_W = len(W_SHAPES)


def _shard_shape(w):
    R, C, by_col = W_SHAPES[w]
    return (R, C // N_CHIPS) if by_col else (R // N_CHIPS, C)


def _half_shape(w):
    sr, sc = _shard_shape(w)
    return sr // 2, sc


def _region(full_ref, w, chip, half, quarter=None):
    sr, sc = _shard_shape(w)
    by_col = W_SHAPES[w][2]
    r0, c0 = (0, chip * sc) if by_col else (chip * sr, 0)
    r0, rows = r0 + half * (sr // 2), sr // 2
    if quarter is not None:
        r0, rows = r0 + quarter * (rows // 2), rows // 2
    return full_ref.at[pl.ds(r0, rows), pl.ds(c0, sc)]


def _on_device(fn):
    x, y, c = _place()
    me = 4 * x + 2 * y + c
    for d in range(N_DEV):
        @pl.when(me == d)
        def _(d=d):
            fn(x, y, c, d)


GATHER_COPIES = (
    (0, 0, None, "x"), (0, 0, None, "y"),
    (1, 2, 0, "y"), (1, 1, 1, "x"),
    (1, 2, None, "s"), (1, 1, None, "s"),
    (2, 3, 0, "s"), (2, 3, 1, "s"),
)
PEER_FLIP = {"x": 2, "y": 1, "s": 0}


GATHER_STAGES = {
    None: (((), (0, 1), ()), ((0, 1), (2, 3, 4, 5), ()), ((2, 3), (6, 7), ()), ((4, 5, 6, 7), (), tuple(range(8)))),
    "near": (((), (0, 1), ()), ((0, 1), (), (0, 1))),
    "far": (((), (2, 3, 4, 5), ()), ((2, 3), (6, 7), ()), ((4, 5, 6, 7), (), (2, 3, 4, 5, 6, 7))),
}


def _gather_plan(partials, pass_at=(0.5, 0.75), part=None):
    ws = sorted(partials)
    n_t = len(GATHER_COPIES)
    jobs = [(i, w) for i, w in enumerate(ws)]

    def copy(pi, po, ps, x, y, c, d, i, w, t, landing):
        chip, dc = d >> 1, d & 1
        stage, flip, quarter, to = GATHER_COPIES[t]
        if landing:
            peer_chip = chip ^ PEER_FLIP[to]
            part = _region(po[i], w, peer_chip ^ flip, (1 - dc) if to == "s" else dc, quarter)
            src = part
        else:
            part = _region(po[i], w, chip ^ flip, dc, quarter)
            here = flip != 0 and (part_of is None or stage == 2)
            src = part if here else _region(pi[i], w, chip ^ flip, dc, quarter)
        target = {"x": (x ^ 1, y, c), "y": (x, y ^ 1, c), "s": (x, y, 1 - c)}[to]
        return pltpu.make_async_remote_copy(
            src_ref=src, dst_ref=part, send_sem=ps[0].at[i * n_t + t], recv_sem=ps[1].at[i * n_t + t],
            device_id=target, device_id_type=MESH)

    part_of = part

    def stage(landed, started, sent):
        def run(pi, po, ps):
            def on(x, y, c, d):
                for i, w in jobs:
                    for t in landed:
                        copy(pi, po, ps, x, y, c, d, i, w, t, True).wait_recv()
                for i, w in jobs:
                    for t in started:
                        copy(pi, po, ps, x, y, c, d, i, w, t, False).start()
                for i, w in jobs:
                    for t in sent:
                        copy(pi, po, ps, x, y, c, d, i, w, t, False).wait_send()
            _on_device(on)
        return run

    stages = [stage(*st) for st in GATHER_STAGES[part]]
    mid_at = tuple(pass_at) if part is None else tuple(pass_at)[:len(stages) - 2]
    return _Plan([partials[w] for w in ws], [jax.ShapeDtypeStruct(W_SHAPES[w][:2], BF16) for w in ws],
                 [pltpu.SemaphoreType.DMA((n_t * len(ws),)) for _ in range(2)], stages,
                 {i: i for i in range(len(ws))}, mid_at=mid_at)


def _grad_view(g, w):
    R, C, by_col = W_SHAPES[w]
    return g.reshape(1, 2, R // 2, C) if by_col else g.reshape(N_CHIPS, 2, R // N_CHIPS // 2, C)


def _start_wait_plan(ins, outs, n_copies, copies):
    def start(pi, po, ps):
        for cp in copies(pi, po, ps):
            cp.start()

    def finish(pi, po, ps):
        for cp in copies(pi, po, ps):
            cp.wait()

    return _Plan(ins, outs, [pltpu.SemaphoreType.DMA((n_copies,)), pltpu.SemaphoreType.DMA((n_copies,))], [start, finish])


def _sibling_exchange_plan(g4s):
    pieces = [(i, p) for i, g in enumerate(g4s) for p in range(g.shape[0])]

    def copies(pi, po, ps):
        x, y, c = _place()
        return [pltpu.make_async_remote_copy(
            src_ref=pi[i].at[p, 1 - c], dst_ref=po[i].at[p], send_sem=ps[0].at[n], recv_sem=ps[1].at[n],
            device_id=(x, y, 1 - c), device_id_type=MESH) for n, (i, p) in enumerate(pieces)]

    return _start_wait_plan(list(g4s), [jax.ShapeDtypeStruct((g.shape[0],) + g.shape[2:], F32) for g in g4s],
                            len(pieces), copies)


def _pair_sum(g4, other, c_arr, name):
    P, _, hr, C = g4.shape
    br = _ew_rows(hr, C)

    def body(c_ref, g_ref, o_ref, p_ref):
        p_ref[...] = (g_ref[...] + o_ref[...]).astype(BF16)

    return _pcall(
        body, name=name,
        grid_spec=pltpu.PrefetchScalarGridSpec(
            num_scalar_prefetch=1, grid=(P, hr // br),
            in_specs=[pl.BlockSpec((None, None, br, C), lambda p, i, c_ref: (p, c_ref[0], i, 0)),
                      pl.BlockSpec((None, br, C), lambda p, i, c_ref: (p, i, 0))],
            out_specs=pl.BlockSpec((None, br, C), lambda p, i, c_ref: (p, i, 0))),
        out_shape=jax.ShapeDtypeStruct((P, hr, C), BF16),
        compiler_params=_params(("parallel", "parallel")),
    )(c_arr, g4, other)


def _pair_part(p_ref, w, chip):
    sr, sc = _shard_shape(w)
    return p_ref.at[0, :, pl.ds(chip * sc, sc)] if W_SHAPES[w][2] else p_ref.at[chip]


def _chip_exchange_plan(pairs, rels=(0, 1, 2), into=None):
    ws = sorted(pairs)
    n = len(ws)

    def stage(wait):
        def run(pi, po, ps):
            def on(x, y, c, d):
                for i, w in enumerate(ws):
                    for k, (rx, ry) in enumerate(RELATIONS):
                        if k not in rels:
                            continue
                        cp = pltpu.make_async_remote_copy(
                            src_ref=_pair_part(pi[i], w, (d >> 1) ^ (2 * rx + ry)), dst_ref=po[i].at[k],
                            send_sem=ps[0].at[i * 3 + k], recv_sem=ps[1].at[i * 3 + k],
                            device_id=(x ^ rx, y ^ ry, c), device_id_type=MESH)
                        if wait:
                            cp.wait()
                        else:
                            cp.start()
            _on_device(on)
        return run

    ins = [pairs[w] for w in ws] + ([into[w] for w in ws] if into else [])
    return _Plan(ins, [jax.ShapeDtypeStruct((3,) + _half_shape(w), BF16) for w in ws],
                 [pltpu.SemaphoreType.DMA((3 * n,)), pltpu.SemaphoreType.DMA((3 * n,))],
                 [stage(False), stage(True)], {n + i: i for i in range(n)} if into else None)


def _sum_slots(pair, slots, w, chip_arr, name):
    _, hr, C = slots.shape
    br = _ew_rows(hr, C)
    own_map = (lambda i, chip: (0, i, chip[0])) if W_SHAPES[w][2] else (lambda i, chip: (chip[0], i, 0))

    def body(chip_ref, p_ref, s_ref, o_ref):
        acc = p_ref[...].astype(F32)
        for k in range(3):
            acc = acc + s_ref[k].astype(F32)
        o_ref[...] = acc

    return _pcall(
        body, name=name,
        grid_spec=pltpu.PrefetchScalarGridSpec(
            num_scalar_prefetch=1, grid=(hr // br,),
            in_specs=[pl.BlockSpec((None, br, C), own_map), pl.BlockSpec((3, br, C), lambda i, chip: (0, i, 0))],
            out_specs=pl.BlockSpec((br, C), lambda i, chip: (i, 0))),
        out_shape=jax.ShapeDtypeStruct((hr, C), F32), compiler_params=_params(("parallel",)),
    )(chip_arr, pair, slots)


def _sibling_share_plan(halves):
    def copies(pi, po, ps):
        x, y, c = _place()
        return [pltpu.make_async_remote_copy(
            src_ref=pi[i], dst_ref=po[i], send_sem=ps[0].at[i], recv_sem=ps[1].at[i],
            device_id=(x, y, 1 - c), device_id_type=MESH) for i in range(len(halves))]

    return _start_wait_plan(list(halves), [jax.ShapeDtypeStruct(h.shape, F32) for h in halves], len(halves), copies)


def _pad_lanes(v, width=D_MODEL):
    return jnp.pad(v, ((0, 0), (0, width - v.shape[1])))


def _pack_small(b_ada, norm1, norm2, lb, o_gain, q_gain, k_gain, sinks):
    rows = [b_ada.reshape(N_MOD, D_MODEL), norm1, norm2, jnp.concatenate([lb[0:1], o_gain], axis=1),
            _pad_lanes(jnp.concatenate([q_gain, k_gain, sinks], axis=1)), _pad_lanes(lb[1:2]),
            jnp.zeros((SMALL_ROWS - 11, D_MODEL), F32)]
    return jnp.concatenate(rows, axis=0)


def _unpack_small(p):
    return (p[0:6].reshape(1, N_MOD * D_MODEL), p[6:7], p[7:8],
            jnp.concatenate([p[8:9, 0:A_WIDTH], p[10:11, 0:A_WIDTH]], axis=0), p[8:9, A_WIDTH:],
            p[9:10, 0:64], p[9:10, 64:128], p[9:10, 128:144])


def kernel(x, c, w_ada, b_ada, norm1_gain, w_in, lb_logits, hgrn_o_gain, q_norm_gain, k_norm_gain, sinks, w_branch_a, w_branch_b, w_out, norm2_gain, w_mlp_in, w_mlp_out, loss_target, m_w_ada, m_b_ada, m_norm1_gain, m_w_in, m_lb_logits, m_hgrn_o_gain, m_q_norm_gain, m_k_norm_gain, m_sinks, m_w_branch_a, m_w_branch_b, m_w_out, m_norm2_gain, m_w_mlp_in, m_w_mlp_out, v_w_ada, v_b_ada, v_norm1_gain, v_w_in, v_lb_logits, v_hgrn_o_gain, v_q_norm_gain, v_k_norm_gain, v_sinks, v_w_branch_a, v_w_branch_b, v_w_out, v_norm2_gain, v_w_mlp_in, v_w_mlp_out):
    xi, yi, ci = _place()
    chip = 2 * xi + yi
    me = 4 * xi + 2 * yi + ci
    ada_cols = w_ada.shape[2]

    c_all = _allgather_small(jnp.broadcast_to(c, (8, D_MODEL)), "gather_c").reshape(N_DEV, 8, D_MODEL)[:, 0]
    b_cols = lax.dynamic_slice(b_ada, (0, chip * ada_cols), (1, ada_cols))
    mod_part = _ada_fwd(c_all, w_ada[0], b_cols)
    mod_all = _allgather_small(mod_part, "gather_mod").reshape(N_CHIPS, 2, N_DEV, ada_cols)[:, 0]
    mod_mine = lax.dynamic_index_in_dim(mod_all, me, axis=1, keepdims=False).reshape(N_MOD, D_MODEL)
    mod8 = jnp.concatenate([mod_mine, jnp.zeros((2, D_MODEL), F32)], axis=0)

    shards = (w_in[0], w_branch_a[0], w_branch_b[0], w_out[0], w_mlp_in[0], w_mlp_out[0])
    chip_arr = chip.astype(jnp.int32).reshape(1)
    c_arr = ci.astype(jnp.int32).reshape(1)

    grad_x, halves, theirs, st = _local_step(x[0], loss_target[0], mod8, norm1_gain, norm2_gain, lb_logits, hgrn_o_gain,
                                             q_norm_gain, k_norm_gain, sinks, shards, c_arr, chip_arr)
    loss = lax.psum(0.5 * jnp.sum(st["loss"][0]) / D_MODEL, ("x", "y", "c"))
    moments = ((m_w_in, v_w_in), (m_w_branch_a, v_w_branch_a), (m_w_branch_b, v_w_branch_b), (m_w_out, v_w_out),
               (m_w_mlp_in, v_w_mlp_in), (m_w_mlp_out, v_w_mlp_out))
    big = [_adamw_halves(shards[w], halves[w], theirs[w], moments[w][0][0], moments[w][1][0], c_arr, f"adamw{w}")
           for w in range(N_W)]

    swa = st["swa"]
    small = jnp.concatenate([
        st["n1"][1:2], st["n1"][0:1], st["n2"][3:4], st["n2"][1:2], st["n2"][0:1], st["loss"][1:2],
        st["n1"][2:3], st["n2"][2:3], jnp.concatenate([st["d_lb"][0:1], st["d_og"][0:1]], axis=1),
        _pad_lanes(jnp.concatenate([swa[0:1, 0:64], swa[1:2, 0:64], swa[2:3, 0:16]], axis=1)),
        jnp.zeros((SMALL_ROWS - 10, D_MODEL), F32)], axis=0)
    small_all = _allgather_small(small, "gather_small")
    g_small = _small_sum(small_all, lb_logits)
    small_w = (b_ada, norm1_gain, norm2_gain, lb_logits, hgrn_o_gain, q_norm_gain, k_norm_gain, sinks)
    small_m = (m_b_ada, m_norm1_gain, m_norm2_gain, m_lb_logits, m_hgrn_o_gain, m_q_norm_gain, m_k_norm_gain, m_sinks)
    small_v = (v_b_ada, v_norm1_gain, v_norm2_gain, v_lb_logits, v_hgrn_o_gain, v_q_norm_gain, v_k_norm_gain, v_sinks)
    sm = [_unpack_small(t) for t in
          (g_small,) + tuple(_adamw(_pack_small(*small_w), g_small, _pack_small(*small_m), _pack_small(*small_v),
                                    "adamw_small"))]
    g_b, g_n1, g_n2, g_lb, g_og, g_qg, g_kg, g_sk = ([t[i] for t in sm] for i in range(8))

    dmod_all = small_all.reshape(N_DEV, SMALL_ROWS, D_MODEL)[:, 0:N_MOD].reshape(N_DEV, N_MOD * D_MODEL)
    dmod_cols = lax.dynamic_slice(dmod_all, (0, chip * ada_cols), (N_DEV, ada_cols))
    ada = _ada_grad_adamw(c_all.T, dmod_cols, w_ada[0], m_w_ada[0], v_w_ada[0])

    def ordered(k):
        lead = lambda a: a[None]
        return (lead(ada[k]), g_b[k], g_n1[k], lead(big[0][k]), g_lb[k], g_og[k], g_qg[k], g_kg[k], g_sk[k],
                lead(big[1][k]), lead(big[2][k]), lead(big[3][k]), g_n2[k], lead(big[4][k]), lead(big[5][k]))

    return (loss, grad_x[None]) + ordered(0) + ordered(1) + ordered(2) + ordered(3)
```

```python
import functools

import jax
import jax.numpy as jnp
from jax import lax
from jax.experimental import pallas as pl
from jax.experimental.pallas import tpu as pltpu

F32 = jnp.float32
BF16 = jnp.bfloat16
HIGHEST = lax.Precision.HIGHEST
MESH = pl.DeviceIdType.MESH

D_MODEL = 2048
A_WIDTH = 1024
A_HEADS = 8
A_HEAD_DIM = 128
A_CHUNK = 64
B_WIDTH = 1024
B_HEAD_DIM = 64
B_GROUP = 4
B_KV_HEADS = 4
B_KV_WIDTH = 256
BLOCK = 128
MLP_HIDDEN = 8192
IN_WIDTH = 9728
N_MOD = 6
EPS = 1e-6
N_CHIPS = 4
N_DEV = 8

OFF_QA, OFF_FA, OFF_IA, OFF_GA = 0, 1024, 2048, 3072
OFF_QB, OFF_KB, OFF_VB = 4096, 5120, 5376
OFF_GATE_A, OFF_GATE_B = 5632, 7680

ADAM_LR = 0.001
ADAM_B1 = 0.9
ADAM_B2 = 0.999
ADAM_EPS = 1e-08
ADAM_WD = 0.01
ADAM_STEP = 10

VMEM_LIMIT_V7X = 48 * 1024 * 1024
NEG_BIG = -1e30


def _params(sem=None, vmem=VMEM_LIMIT_V7X):
    return pltpu.CompilerParams(dimension_semantics=sem, vmem_limit_bytes=vmem)


class _Plan:
    def __init__(self, ins, outs, sems, stages, aliases=None, mid_at=()):
        self.ins, self.outs, self.sems, self.stages, self.aliases = ins, outs, sems, stages, aliases or {}
        self.mid_at = tuple(mid_at)
        assert len(self.mid_at) == len(stages) - 2


def _join(a, b):
    assert len(a.stages) == 2 and len(b.stages) == 2
    ni, no, ns = len(a.ins), len(a.outs), len(a.sems)

    def stage(k):
        def run(pi, po, ps):
            a.stages[k](pi[:ni], po[:no], ps[:ns])
            b.stages[k](pi[ni:], po[no:], ps[ns:])
        return run

    aliases = dict(a.aliases)
    aliases.update({ni + i: no + o for i, o in b.aliases.items()})
    return _Plan(a.ins + b.ins, a.outs + b.outs, a.sems + b.sems, [stage(0), stage(1)], aliases)


def _pcall(body, plan=None, **kw):
    if plan is None:
        return pl.pallas_call(body, **kw)
    grid = kw["grid"]
    single = not isinstance(kw["out_specs"], (list, tuple))
    in_specs = list(kw["in_specs"])
    out_specs = [kw["out_specs"]] if single else list(kw["out_specs"])
    out_shape = [kw["out_shape"]] if single else list(kw["out_shape"])
    scratch = list(kw.get("scratch_shapes", ()))
    n_in, n_out, n_scr = len(in_specs), len(out_specs), len(scratch)
    n_pi, n_po = len(plan.ins), len(plan.outs)
    total = 1
    for g in grid:
        total *= g
    n_st = len(plan.stages)

    def wrapped(*refs):
        o0 = n_in + n_pi
        s0 = o0 + n_out + n_po
        pi, po, ps = refs[n_in:o0], refs[o0 + n_out:s0], refs[s0 + n_scr:]
        lin = 0
        for d, g in enumerate(grid):
            lin = lin * g + pl.program_id(d)
        for si, frac in enumerate((0.0,) + plan.mid_at):
            @pl.when(lin == int(frac * (total - 1)))
            def _(si=si):
                plan.stages[si](pi, po, ps)
        body(*refs[:n_in], *refs[o0:o0 + n_out], *refs[s0:s0 + n_scr])

        @pl.when(lin == total - 1)
        def _():
            plan.stages[-1](pi, po, ps)

    any_spec = pl.BlockSpec(memory_space=pl.ANY)
    call = pl.pallas_call(
        wrapped, name=kw["name"], grid=grid, in_specs=in_specs + [any_spec] * n_pi,
        out_specs=out_specs + [any_spec] * n_po, out_shape=out_shape + list(plan.outs),
        scratch_shapes=scratch + list(plan.sems),
        input_output_aliases={n_in + i: n_out + o for i, o in plan.aliases.items()},
        compiler_params=_params(("arbitrary",) * len(grid)))

    def run(*args):
        res = call(*args, *plan.ins)
        outs = list(res[:n_out])
        return (outs[0] if single else outs), list(res[n_out:])

    return run


def _run_plan(plan, name):
    return _pcall(lambda: None, plan=plan, name=name, grid=(1,), in_specs=[], out_specs=[], out_shape=[])()[1]


def _sig(x):
    return 1.0 / (1.0 + jnp.exp(-x))


def _nn(a, b):
    return lax.dot_general(a.astype(BF16), b.astype(BF16), (((1,), (0,)), ((), ())), preferred_element_type=F32)


def _nt(a, b):
    return lax.dot_general(a.astype(BF16), b.astype(BF16), (((1,), (1,)), ((), ())), preferred_element_type=F32)


def _tn(a, b):
    return lax.dot_general(a.astype(BF16), b.astype(BF16), (((0,), (0,)), ((), ())), preferred_element_type=F32)


def _mm(a, b, *, name, ta=False, tb=False, bm=1024, bn=1024, bk=2048, out_dtypes=(F32,), epi=None, extras=(),
        extra_cols=None, plan=None, a_blocks=None):
    if ta:
        K, M = a.shape
        bk = K
        if a_blocks is not None:
            M = a_blocks[0] * bm
    else:
        M, K = a.shape
    if tb:
        N, K2 = b.shape
    else:
        K2, N = b.shape
    bm, bn, bk = min(bm, M), min(bn, N), min(bk, K)
    assert K == K2 and M % bm == 0 and N % bn == 0 and K % bk == 0, (name, a.shape, b.shape)
    nk = K // bk
    a_col = a_blocks[1] if a_blocks is not None else (lambda i: i)
    a_spec = pl.BlockSpec((bk, bm), lambda i, j, k: (k, a_col(i))) if ta else pl.BlockSpec((bm, bk), lambda i, j, k: (i, k))
    b_spec = pl.BlockSpec((bn, bk), lambda i, j, k: (j, k)) if tb else pl.BlockSpec((bk, bn), lambda i, j, k: (k, j))
    t_spec = pl.BlockSpec((bm, bn), lambda i, j, k: (i, j))
    extra_cols = extra_cols or (0,) * len(extras)
    e_specs = [pl.BlockSpec((bm, bn), lambda i, j, k, off=off: (i, off + j)) for off in extra_cols]
    dims = (((1,), (1 if tb else 0,)), ((), ()))
    n_e, n_o = len(extras), len(out_dtypes)

    def body(*refs):
        a_ref, b_ref = refs[0], refs[1]
        e_refs = refs[2:2 + n_e]
        o_refs = refs[2 + n_e:2 + n_e + n_o]

        def finish(acc):
            outs = (acc,) if epi is None else epi(acc, *[e[...] for e in e_refs])
            for o_ref, o in zip(o_refs, outs):
                o_ref[...] = o.astype(o_ref.dtype)

        if ta:
            at_ref = refs[-1]

            @pl.when(pl.program_id(1) == 0)
            def _():
                at_ref[...] = a_ref[...].T

            lhs = at_ref[...]
        else:
            lhs = a_ref[...].astype(BF16)
        part = lax.dot_general(lhs, b_ref[...].astype(BF16), dims, preferred_element_type=F32)
        if nk == 1:
            finish(part)
        else:
            acc_ref = refs[-1]
            k = pl.program_id(2)

            @pl.when(k == 0)
            def _():
                acc_ref[...] = part

            @pl.when(k > 0)
            def _():
                acc_ref[...] += part

            @pl.when(k == nk - 1)
            def _():
                finish(acc_ref[...])

    if ta:
        assert a.dtype == BF16 and nk == 1
        scratch = [pltpu.VMEM((bm, bk), BF16)]
    else:
        scratch = [pltpu.VMEM((bm, bn), F32)] if nk > 1 else []
    out = _pcall(
        body, plan=plan, name=name, grid=(M // bm, N // bn, nk),
        in_specs=[a_spec, b_spec] + e_specs,
        out_specs=[t_spec] * n_o,
        out_shape=[jax.ShapeDtypeStruct((M, N), dt) for dt in out_dtypes],
        scratch_shapes=scratch,
        compiler_params=_params(("parallel", "arbitrary", "arbitrary")),
    )(a, b, *extras)
    if plan is not None:
        return (out[0][0] if n_o == 1 else out[0]), out[1]
    return out[0] if n_o == 1 else out


def _ada_fwd(c_all, w_ada, b_cols):
    n = w_ada.shape[1]
    bn = 512

    def body(c_ref, w_ref, b_ref, o_ref):
        cv = c_ref[...]
        sc = cv * _sig(cv)
        o_ref[...] = jnp.dot(sc, w_ref[...], precision=HIGHEST, preferred_element_type=F32) + b_ref[...]

    return _pcall(
        body, name="ada_fwd", grid=(n // bn,),
        in_specs=[pl.BlockSpec((N_DEV, D_MODEL), lambda j: (0, 0)), pl.BlockSpec((D_MODEL, bn), lambda j: (0, j)),
                  pl.BlockSpec((1, bn), lambda j: (0, j))],
        out_specs=pl.BlockSpec((N_DEV, bn), lambda j: (0, j)),
        out_shape=jax.ShapeDtypeStruct((N_DEV, n), F32),
        compiler_params=_params(("parallel",)),
    )(c_all, w_ada, b_cols)


ROWS_EW = 256


def _rms_fwd_math(x, gain, scale, shift):
    rstd = lax.rsqrt(jnp.mean(x * x, axis=-1, keepdims=True) + EPS)
    xhat = x * rstd
    n = xhat * gain
    return n * (1.0 + scale) + shift, xhat, n, rstd


def _rms_bwd_math(dh, xhat, n, rstd, gain, scale):
    dn = dh * (1.0 + scale)
    dxhat = dn * gain
    dx = rstd * (dxhat - xhat * jnp.mean(dxhat * xhat, axis=-1, keepdims=True))
    d_scale = jnp.sum(dh * n, axis=0, keepdims=True)
    d_shift = jnp.sum(dh, axis=0, keepdims=True)
    d_gain = jnp.sum(dn * xhat, axis=0, keepdims=True)
    return dx, d_scale, d_shift, d_gain


def _row_spec(w=D_MODEL, br=ROWS_EW):
    return pl.BlockSpec((br, w), lambda i: (i, 0))


def _vec_spec(r=8, w=D_MODEL):
    return pl.BlockSpec((r, w), lambda i: (0, 0))


def _norm1_fwd(x, gain, mod8, plan=None):
    T = x.shape[0]

    def body(x_ref, g_ref, m_ref, h_ref):
        h, _, _, _ = _rms_fwd_math(x_ref[...], g_ref[...], m_ref[1:2, :], m_ref[0:1, :])
        h_ref[...] = h.astype(BF16)

    return _pcall(
        body, plan=plan, name="norm1_fwd", grid=(T // ROWS_EW,),
        in_specs=[_row_spec(), _vec_spec(1), _vec_spec()],
        out_specs=_row_spec(), out_shape=jax.ShapeDtypeStruct((T, D_MODEL), BF16),
        compiler_params=_params(("parallel",)),
    )(x, gain, mod8)


def _res_norm2_fwd(x, mo, gain, mod8):
    T = x.shape[0]

    def body(x_ref, mo_ref, g_ref, m_ref, x1_ref, h_ref):
        x1 = x_ref[...] + m_ref[2:3, :] * mo_ref[...]
        x1_ref[...] = x1
        h, _, _, _ = _rms_fwd_math(x1, g_ref[...], m_ref[4:5, :], m_ref[3:4, :])
        h_ref[...] = h.astype(BF16)

    return _pcall(
        body, name="res_norm2_fwd", grid=(T // ROWS_EW,),
        in_specs=[_row_spec(), _row_spec(), _vec_spec(1), _vec_spec()],
        out_specs=[_row_spec(), _row_spec()],
        out_shape=[jax.ShapeDtypeStruct((T, D_MODEL), F32), jax.ShapeDtypeStruct((T, D_MODEL), BF16)],
        compiler_params=_params(("parallel",)),
    )(x, mo, gain, mod8)


def _loss_bwd(x1, mlp, target, mod8):
    T = x1.shape[0]

    def body(x1_ref, mlp_ref, t_ref, m_ref, dy_ref, dmlp_ref, st_ref):
        i = pl.program_id(0)
        gate = m_ref[5:6, :]
        mlp_v = mlp_ref[...]
        err = x1_ref[...] + gate * mlp_v - t_ref[...]
        dy = err * (1.0 / D_MODEL)
        dy_ref[...] = dy
        dmlp_ref[...] = (dy * gate).astype(BF16)

        @pl.when(i == 0)
        def _():
            st_ref[...] = jnp.zeros_like(st_ref)

        st_ref[0:1, :] += jnp.sum(err * err, axis=0, keepdims=True)
        st_ref[1:2, :] += jnp.sum(dy * mlp_v, axis=0, keepdims=True)

    return _pcall(
        body, name="loss_bwd", grid=(T // ROWS_EW,),
        in_specs=[_row_spec(), _row_spec(), _row_spec(), _vec_spec()],
        out_specs=[_row_spec(), _row_spec(), _vec_spec()],
        out_shape=[jax.ShapeDtypeStruct((T, D_MODEL), F32), jax.ShapeDtypeStruct((T, D_MODEL), BF16),
                   jax.ShapeDtypeStruct((8, D_MODEL), F32)],
        compiler_params=_params(("arbitrary",)),
    )(x1, mlp, target, mod8)


def _norm2_bwd(dh2, x1, dy, mo, gain, mod8):
    T = x1.shape[0]

    def body(dh_ref, x1_ref, dy_ref, mo_ref, g_ref, m_ref, dx1_ref, dmo_ref, st_ref):
        i = pl.program_id(0)
        gain_v, scale = g_ref[...], m_ref[4:5, :]
        _, xhat, n, rstd = _rms_fwd_math(x1_ref[...], gain_v, scale, m_ref[3:4, :])
        dx, d_scale, d_shift, d_gain = _rms_bwd_math(dh_ref[...], xhat, n, rstd, gain_v, scale)
        dx1 = dy_ref[...] + dx
        dx1_ref[...] = dx1
        dmo_ref[...] = (dx1 * m_ref[2:3, :]).astype(BF16)

        @pl.when(i == 0)
        def _():
            st_ref[...] = jnp.zeros_like(st_ref)

        st_ref[0:1, :] += d_scale
        st_ref[1:2, :] += d_shift
        st_ref[2:3, :] += d_gain
        st_ref[3:4, :] += jnp.sum(dx1 * mo_ref[...], axis=0, keepdims=True)

    return _pcall(
        body, name="norm2_bwd", grid=(T // ROWS_EW,),
        in_specs=[_row_spec(), _row_spec(), _row_spec(), _row_spec(), _vec_spec(1), _vec_spec()],
        out_specs=[_row_spec(), _row_spec(), _vec_spec()],
        out_shape=[jax.ShapeDtypeStruct((T, D_MODEL), F32), jax.ShapeDtypeStruct((T, D_MODEL), BF16),
                   jax.ShapeDtypeStruct((8, D_MODEL), F32)],
        compiler_params=_params(("arbitrary",)),
    )(dh2, x1, dy, mo, gain, mod8)


def _norm1_bwd(dh, x, dx1, gain, mod8):
    T = x.shape[0]

    def body(dh_ref, x_ref, dx1_ref, g_ref, m_ref, dx_ref, st_ref):
        i = pl.program_id(0)
        gain_v, scale = g_ref[...], m_ref[1:2, :]
        _, xhat, n, rstd = _rms_fwd_math(x_ref[...], gain_v, scale, m_ref[0:1, :])
        dx, d_scale, d_shift, d_gain = _rms_bwd_math(dh_ref[...], xhat, n, rstd, gain_v, scale)
        dx_ref[...] = dx1_ref[...] + dx

        @pl.when(i == 0)
        def _():
            st_ref[...] = jnp.zeros_like(st_ref)

        st_ref[0:1, :] += d_scale
        st_ref[1:2, :] += d_shift
        st_ref[2:3, :] += d_gain

    return _pcall(
        body, name="norm1_bwd", grid=(T // ROWS_EW,),
        in_specs=[_row_spec(), _row_spec(), _row_spec(), _vec_spec(1), _vec_spec()],
        out_specs=[_row_spec(), _vec_spec()],
        out_shape=[jax.ShapeDtypeStruct((T, D_MODEL), F32), jax.ShapeDtypeStruct((8, D_MODEL), F32)],
        compiler_params=_params(("arbitrary",)),
    )(dh, x, dx1, gain, mod8)


MERGE_BC = 512


def _hgrn_rows(T):
    return 512 if T >= 1024 else 128


def _lower_bound(lbl):
    e = jnp.exp(lbl - jnp.max(lbl, axis=0, keepdims=True))
    return e[0:1, :] / (e[0:1, :] + e[1:2, :])


def _chunk_sum_matrix(rows, backward):
    shift = A_CHUNK.bit_length() - 1
    r = lax.broadcasted_iota(jnp.int32, (rows, rows), 0)
    c = lax.broadcasted_iota(jnp.int32, (rows, rows), 1)
    same = jnp.right_shift(r, shift) == jnp.right_shift(c, shift)
    return (same & ((r <= c) if backward else (r >= c))).astype(BF16)


def _chunk_sums(m, x):
    n = x.shape[1]
    hi = x.astype(BF16)
    rest = x - hi.astype(F32)
    mid = rest.astype(BF16)
    lo = (rest - mid.astype(F32)).astype(BF16)
    y = jnp.dot(m, jnp.concatenate([hi, mid, lo], axis=1), preferred_element_type=F32)
    return y[:, 0:n] + y[:, n:2 * n] + y[:, 2 * n:3 * n]


def _hgrn_block_pre(q, fl, lb, m_fwd):
    sg = _sig(fl)
    f = lb + (1.0 - lb) * sg
    sq = _sig(q)
    return dict(sg=sg, f=f, k=1.0 - f, sq=sq, qf=q * sq, b=_chunk_sums(m_fwd, jnp.log(f)))


def _hgrn_chunk_local(pre, r):
    C = A_CHUNK
    qf, k, b = pre["qf"][r], pre["k"][r], pre["b"][r]
    causal = lax.broadcasted_iota(jnp.int32, (C, C), 0) >= lax.broadcasted_iota(jnp.int32, (C, C), 1)
    bm = b[C // 2 - 1:C // 2, :]
    bl = b[C - 1:C, :]
    e_q, e_k = jnp.exp(b - bm), jnp.exp(bm - b)
    e_b, e_l = jnp.exp(b), jnp.exp(bl - b)
    qd, kd = qf * e_q, k * e_k
    qe, ke = qf * e_b, k * e_l
    att = jnp.where(causal, _nt(qd, kd), 0.0)
    return dict(causal=causal, e_q=e_q, e_k=e_k, e_b=e_b, e_l=e_l, qd=qd, kd=kd, qe=qe, ke=ke, att=att, dec=jnp.exp(bl))


def _hgrn_chunk_fwd(pre, r, v, st):
    c = _hgrn_chunk_local(pre, r)
    c["o"] = _nn(c["att"], v) + _nt(c["qe"], st)
    return c


def _lockstep(gens):
    out = [None] * len(gens)
    live = list(enumerate(gens))
    while live:
        still = []
        for i, g in live:
            try:
                next(g)
                still.append((i, g))
            except StopIteration as done:
                out[i] = done.value
        live = still
    return out


HGRN_HEADS_PER_STEP = 4


def _hgrn_fwd(proj, lb_logits, o_gain, plan=None):
    T = proj.shape[0]
    BR = _hgrn_rows(T)
    cps = BR // A_CHUNK
    K, NH = A_HEAD_DIM, HGRN_HEADS_PER_STEP
    W = NH * K

    def col(off):
        return pl.BlockSpec((BR, W), lambda h, cb: (cb, off // W + h))

    def body(q_ref, f_ref, i_ref, g_ref, lbl_ref, og_ref, o_ref, s_ref, st):
        @pl.when(pl.program_id(1) == 0)
        def _():
            st[...] = jnp.zeros_like(st)

        lb_all = _lower_bound(lbl_ref[...])
        m_fwd = _chunk_sum_matrix(BR, False)
        pre = [_hgrn_block_pre(q_ref[:, n * K:(n + 1) * K], f_ref[:, n * K:(n + 1) * K], lb_all[:, n * K:(n + 1) * K], m_fwd)
               for n in range(NH)]
        def local(n, ci):
            r, hs = slice(ci * A_CHUNK, (ci + 1) * A_CHUNK), slice(n * K, (n + 1) * K)
            v = i_ref[r, hs]
            c = _hgrn_chunk_local(pre[n], r)
            yield
            return dict(o=_nn(c["att"], v), ds=_tn(v, c["ke"]), qe=c["qe"], dec=c["dec"])

        def chain(n, loc):
            hs = slice(n * K, (n + 1) * K)
            state = st[n]
            for ci, p in enumerate(loc):
                r = slice(ci * A_CHUNK, (ci + 1) * A_CHUNK)
                s_ref[n, ci] = state
                o = p["o"] + _nt(p["qe"], state)
                state = state * p["dec"] + p["ds"]
                yield
                on = o * lax.rsqrt(jnp.mean(o * o, axis=-1, keepdims=True) + EPS)
                g = g_ref[r, hs]
                o_ref[r, hs] = (on * og_ref[:, hs] * (g * _sig(g))).astype(BF16)
            st[n] = state

        loc = _lockstep([local(n, ci) for n in range(NH) for ci in range(cps)])
        _lockstep([chain(n, loc[n * cps:(n + 1) * cps]) for n in range(NH)])

    return _pcall(
        body, plan=plan, name="hgrn_fwd", grid=(A_HEADS // NH, T // BR),
        in_specs=[col(OFF_QA), col(OFF_FA), col(OFF_IA), col(OFF_GA),
                  pl.BlockSpec((2, W), lambda h, cb: (0, h)), pl.BlockSpec((1, W), lambda h, cb: (0, h))],
        out_specs=[pl.BlockSpec((BR, W), lambda h, cb: (cb, h)),
                   pl.BlockSpec((NH, cps, K, K), lambda h, cb: (h, cb, 0, 0))],
        out_shape=[jax.ShapeDtypeStruct((T, A_WIDTH), BF16),
                   jax.ShapeDtypeStruct((A_HEADS, T // A_CHUNK, K, K), F32)],
        scratch_shapes=[pltpu.VMEM((NH, K, K), F32)],
        compiler_params=_params(("parallel", "arbitrary")),
    )(proj, proj, proj, proj, lb_logits, o_gain)


def _hgrn_bwd(proj, lb_logits, o_gain, states, do, plan=None):
    T = proj.shape[0]
    BR = _hgrn_rows(T)
    cps = BR // A_CHUNK
    ncb = T // BR
    K, C, NH = A_HEAD_DIM, A_CHUNK, HGRN_HEADS_PER_STEP
    W = NH * K

    def col(off):
        return pl.BlockSpec((BR, W), lambda h, cb: (ncb - 1 - cb, off // W + h))

    def body(q_ref, f_ref, i_ref, g_ref, lbl_ref, og_ref, s_ref, do_ref,
             dq_ref, df_ref, di_ref, dg_ref, dlb_ref, dog_ref, dst):
        @pl.when(pl.program_id(1) == 0)
        def _():
            dst[...] = jnp.zeros_like(dst)
            dlb_ref[...] = jnp.zeros_like(dlb_ref)
            dog_ref[...] = jnp.zeros_like(dog_ref)

        lb_all = _lower_bound(lbl_ref[...])
        row = lax.broadcasted_iota(jnp.int32, (C, K), 0)
        m_fwd, m_bwd = _chunk_sum_matrix(BR, False), _chunk_sum_matrix(BR, True)
        pre = [_hgrn_block_pre(q_ref[:, n * K:(n + 1) * K], f_ref[:, n * K:(n + 1) * K], lb_all[:, n * K:(n + 1) * K], m_fwd)
               for n in range(NH)]
        def local(n, ci):
            r, hs = slice(ci * C, (ci + 1) * C), slice(n * K, (n + 1) * K)
            gain = og_ref[:, hs]
            st = s_ref[n, ci]
            v = i_ref[r, hs]
            q = q_ref[r, hs]
            c = _hgrn_chunk_fwd(pre[n], r, v, st)
            yield
            o = c["o"]
            rn = lax.rsqrt(jnp.mean(o * o, axis=-1, keepdims=True) + EPS)
            on = o * rn
            g = g_ref[r, hs]
            sgg = _sig(g)
            dy = do_ref[r, hs]
            d_ong = dy * (g * sgg)
            dg_ref[r, hs] = (dy * (on * gain) * (sgg * (1.0 + g * (1.0 - sgg)))).astype(BF16)
            d_on = d_ong * gain
            d_o = rn * (d_on - on * jnp.mean(d_on * on, axis=-1, keepdims=True))
            datt = jnp.where(c["causal"], _nt(d_o, v), 0.0)
            dqe = _nn(d_o, st)
            yield
            dqd = _nn(datt, c["kd"])
            dkd = _tn(datt, c["qd"])
            dv = _tn(c["att"], d_o)
            ds = _tn(d_o, c["qe"])
            yield
            t_q, t_k = dqd * c["qd"], dkd * c["kd"]
            sq = pre[n]["sq"][r]
            dq_ref[r, hs] = ((dqd * c["e_q"] + dqe * c["e_b"]) * (sq * (1.0 + q * (1.0 - sq)))).astype(BF16)
            return dict(v=v, st=st, ke=c["ke"], e_l=c["e_l"], dec=c["dec"], dv=dv, ds=ds, dk=dkd * c["e_k"],
                        db=t_q - t_k + dqe * c["qe"], dbm=jnp.sum(t_k - t_q, axis=0, keepdims=True),
                        d_og=jnp.sum(d_ong * on, axis=0, keepdims=True))

        def chain(n, loc):
            hs = slice(n * K, (n + 1) * K)
            dst_next = dst[n]
            db_of, dk_of = [None] * cps, [None] * cps
            for ci in reversed(range(cps)):
                p = loc[ci]
                di_ref[ci * C:(ci + 1) * C, hs] = (p["dv"] + _nt(p["ke"], dst_next)).astype(BF16)
                dke = _nn(p["v"], dst_next)
                yield
                t_l = dke * p["ke"]
                dbl = jnp.sum(t_l, axis=0, keepdims=True) + jnp.sum(dst_next * p["st"], axis=0, keepdims=True) * p["dec"]
                db_of[ci] = p["db"] - t_l + jnp.where(row == C // 2 - 1, p["dbm"], 0.0) + jnp.where(row == C - 1, dbl, 0.0)
                dk_of[ci] = p["dk"] + dke * p["e_l"]
                dst_next = dst_next * p["dec"] + p["ds"]
            dst[n] = dst_next
            return db_of, dk_of

        loc = _lockstep([local(n, ci) for n in range(NH) for ci in range(cps)])
        loc = [loc[n * cps:(n + 1) * cps] for n in range(NH)]
        chains = _lockstep([chain(n, loc[n]) for n in range(NH)])
        for n in range(NH):
            hs = slice(n * K, (n + 1) * K)
            db_of, dk_of = chains[n]
            d_og = loc[n][0]["d_og"]
            for p in loc[n][1:]:
                d_og = d_og + p["d_og"]
            dog_ref[0:1, hs] += d_og
            lb, sg = lb_all[:, hs], pre[n]["sg"]
            dlf = _chunk_sums(m_bwd, jnp.concatenate(db_of, axis=0))
            df = dlf / pre[n]["f"] - jnp.concatenate(dk_of, axis=0)
            df_ref[:, hs] = (df * (1.0 - lb) * sg * (1.0 - sg)).astype(BF16)
            dlb_ref[0:1, hs] += jnp.sum(df * (1.0 - sg), axis=0, keepdims=True)

    ocol = pl.BlockSpec((BR, W), lambda h, cb: (ncb - 1 - cb, h))
    vec = pl.BlockSpec((8, W), lambda h, cb: (0, h))
    return _pcall(
        body, plan=plan, name="hgrn_bwd", grid=(A_HEADS // NH, ncb),
        in_specs=[col(OFF_QA), col(OFF_FA), col(OFF_IA), col(OFF_GA),
                  pl.BlockSpec((2, W), lambda h, cb: (0, h)), pl.BlockSpec((1, W), lambda h, cb: (0, h)),
                  pl.BlockSpec((NH, cps, K, K), lambda h, cb: (h, ncb - 1 - cb, 0, 0)),
                  pl.BlockSpec((BR, W), lambda h, cb: (ncb - 1 - cb, h))],
        out_specs=[ocol, ocol, ocol, ocol, vec, vec],
        out_shape=[jax.ShapeDtypeStruct((T, A_WIDTH), BF16)] * 4 + [jax.ShapeDtypeStruct((8, A_WIDTH), F32)] * 2,
        scratch_shapes=[pltpu.VMEM((NH, K, K), F32)],
        compiler_params=_params(("parallel", "arbitrary")),
    )(proj, proj, proj, proj, lb_logits, o_gain, states, do)


def _head_norm(x):
    r = lax.rsqrt(jnp.mean(x * x, axis=-1, keepdims=True) + EPS)
    return x * r, r


def _head_norm_bwd(dy, xn, r, gain):
    dxn = dy * gain
    return r * (dxn - xn * jnp.mean(dxn * xn, axis=-1, keepdims=True)), jnp.sum(dy * xn, axis=0, keepdims=True)


def _swa_mask(has_prev):
    rows = B_GROUP * BLOCK
    r = lax.broadcasted_iota(jnp.int32, (rows, 2 * BLOCK), 0) % BLOCK
    c = lax.broadcasted_iota(jnp.int32, (rows, 2 * BLOCK), 1)
    rel = r + BLOCK - c
    return (rel >= 0) & (rel < BLOCK) & ((c >= BLOCK) | has_prev)


def _swa_head_fwd(j, q_ref, kp_ref, kc_ref, vp_ref, vc_ref, qg, kg, sk_ref, mask):
    hs = slice(j * B_HEAD_DIM, (j + 1) * B_HEAD_DIM)
    kcat = jnp.concatenate([kp_ref[:, hs], kc_ref[:, hs]], axis=0)
    vcat = jnp.concatenate([vp_ref[:, hs], vc_ref[:, hs]], axis=0)
    qs = jnp.concatenate([q_ref[:, pl.ds((j * B_GROUP + g) * B_HEAD_DIM, B_HEAD_DIM)] for g in range(B_GROUP)], axis=0)
    kn, kr = _head_norm(kcat)
    qn, qr = _head_norm(qs)
    kh, qh = kn * kg, qn * qg
    yield
    s = jnp.where(mask, _nt(qh, kh) * (B_HEAD_DIM ** -0.5), NEG_BIG)
    yield
    sink = jnp.concatenate(
        [jnp.broadcast_to(sk_ref[0:1, pl.ds(j * B_GROUP + g, 1)], (BLOCK, 1)) for g in range(B_GROUP)], axis=0)
    m = jnp.maximum(jnp.max(s, axis=-1, keepdims=True), sink)
    p = jnp.exp(s - m)
    e_sink = jnp.exp(sink - m)
    inv = 1.0 / (jnp.sum(p, axis=-1, keepdims=True) + e_sink)
    prob = p * inv
    return dict(vcat=vcat, kn=kn, kr=kr, qn=qn, qr=qr, kh=kh, qh=qh, prob=prob, p_sink=e_sink * inv)


def _swa_in_specs(nb, last):
    def qi(n):
        return jnp.minimum(n, last)

    q = pl.BlockSpec((BLOCK, B_WIDTH), lambda n: (qi(n), OFF_QB // B_WIDTH))
    kc = pl.BlockSpec((BLOCK, B_KV_WIDTH), lambda n: (qi(n), OFF_KB // B_KV_WIDTH))
    kp = pl.BlockSpec((BLOCK, B_KV_WIDTH), lambda n: (jnp.maximum(qi(n) - 1, 0), OFF_KB // B_KV_WIDTH))
    vc = pl.BlockSpec((BLOCK, B_KV_WIDTH), lambda n: (qi(n), OFF_VB // B_KV_WIDTH))
    vp = pl.BlockSpec((BLOCK, B_KV_WIDTH), lambda n: (jnp.maximum(qi(n) - 1, 0), OFF_VB // B_KV_WIDTH))
    small = [pl.BlockSpec((1, B_HEAD_DIM), lambda n: (0, 0)), pl.BlockSpec((1, B_HEAD_DIM), lambda n: (0, 0)),
             pl.BlockSpec((1, B_GROUP * B_KV_HEADS), lambda n: (0, 0))]
    return [q, kp, kc, vp, vc] + small


def _swa_fwd(proj, q_gain, k_gain, sinks, plan=None):
    T = proj.shape[0]
    nb = T // BLOCK

    def body(q_ref, kp_ref, kc_ref, vp_ref, vc_ref, qg_ref, kg_ref, sk_ref, o_ref):
        mask = _swa_mask(pl.program_id(0) > 0)

        def head(j):
            c = yield from _swa_head_fwd(j, q_ref, kp_ref, kc_ref, vp_ref, vc_ref, qg_ref[...], kg_ref[...], sk_ref, mask)
            yield
            o = _nn(c["prob"], c["vcat"])
            yield
            for g in range(B_GROUP):
                o_ref[:, pl.ds((j * B_GROUP + g) * B_HEAD_DIM, B_HEAD_DIM)] = o[g * BLOCK:(g + 1) * BLOCK].astype(BF16)

        _lockstep([head(j) for j in range(B_KV_HEADS)])

    return _pcall(
        body, plan=plan, name="swa_fwd", grid=(nb,),
        in_specs=_swa_in_specs(nb, nb - 1),
        out_specs=pl.BlockSpec((BLOCK, B_WIDTH), lambda n: (n, 0)),
        out_shape=jax.ShapeDtypeStruct((T, B_WIDTH), BF16),
        compiler_params=_params(("parallel",)),
    )(proj, proj, proj, proj, proj, q_gain, k_gain, sinks)


def _swa_bwd(proj, q_gain, k_gain, sinks, do, plan=None):
    T = proj.shape[0]
    nb = T // BLOCK
    scale = B_HEAD_DIM ** -0.5

    def body(q_ref, kp_ref, kc_ref, vp_ref, vc_ref, qg_ref, kg_ref, sk_ref, do_ref,
             dq_ref, dkv_ref, sm_ref, ck, cv):
        n = pl.program_id(0)

        @pl.when(n == 0)
        def _():
            ck[...] = jnp.zeros_like(ck)
            cv[...] = jnp.zeros_like(cv)
            sm_ref[...] = jnp.zeros_like(sm_ref)

        @pl.when(n < nb)
        def _():
            mask = _swa_mask(n > 0)
            qg, kg = qg_ref[...], kg_ref[...]
            lane = lax.broadcasted_iota(jnp.int32, (1, BLOCK), 1)
            def head(j):
                hs = slice(j * B_HEAD_DIM, (j + 1) * B_HEAD_DIM)
                vs = slice(B_KV_WIDTH + j * B_HEAD_DIM, B_KV_WIDTH + (j + 1) * B_HEAD_DIM)
                c = yield from _swa_head_fwd(j, q_ref, kp_ref, kc_ref, vp_ref, vc_ref, qg, kg, sk_ref, mask)
                d_out = jnp.concatenate(
                    [do_ref[:, pl.ds((j * B_GROUP + g) * B_HEAD_DIM, B_HEAD_DIM)] for g in range(B_GROUP)], axis=0)
                prob = c["prob"]
                yield
                out = _nn(prob, c["vcat"])
                d_prob = _nt(d_out, c["vcat"])
                dv = _tn(prob, d_out)
                yield
                delta = jnp.sum(d_out * out, axis=-1, keepdims=True)
                ds = prob * (d_prob - delta)
                d_sink = -c["p_sink"] * delta
                yield
                dqh = _nn(ds, c["kh"]) * scale
                dkh = _tn(ds, c["qh"]) * scale
                yield
                dq, dqg = _head_norm_bwd(dqh, c["qn"], c["qr"], qg)
                dk, dkg = _head_norm_bwd(dkh, c["kn"], c["kr"], kg)
                d_sinks = jnp.zeros((1, BLOCK), F32)
                for g in range(B_GROUP):
                    dq_ref[:, pl.ds((j * B_GROUP + g) * B_HEAD_DIM, B_HEAD_DIM)] = dq[g * BLOCK:(g + 1) * BLOCK].astype(BF16)
                    tot = jnp.sum(d_sink[g * BLOCK:(g + 1) * BLOCK], axis=0, keepdims=True)
                    d_sinks = d_sinks + jnp.where(lane == j * B_GROUP + g, tot, 0.0)
                dkv_ref[:, hs] = (ck[:, hs] + dk[0:BLOCK]).astype(BF16)
                dkv_ref[:, vs] = (cv[:, hs] + dv[0:BLOCK]).astype(BF16)
                ck[:, hs] = dk[BLOCK:2 * BLOCK]
                cv[:, hs] = dv[BLOCK:2 * BLOCK]
                return dqg, dkg, d_sinks

            small = _lockstep([head(j) for j in range(B_KV_HEADS)])
            sm_ref[0:1, 0:B_HEAD_DIM] += small[0][0] + small[1][0] + small[2][0] + small[3][0]
            sm_ref[1:2, 0:B_HEAD_DIM] += small[0][1] + small[1][1] + small[2][1] + small[3][1]
            sm_ref[2:3, :] += small[0][2] + small[1][2] + small[2][2] + small[3][2]

        @pl.when(n == nb)
        def _():
            dkv_ref[:, 0:B_KV_WIDTH] = ck[...].astype(BF16)
            dkv_ref[:, B_KV_WIDTH:2 * B_KV_WIDTH] = cv[...].astype(BF16)

    return _pcall(
        body, plan=plan, name="swa_bwd", grid=(nb + 1,),
        in_specs=_swa_in_specs(nb, nb - 1) + [pl.BlockSpec((BLOCK, B_WIDTH), lambda n: (jnp.minimum(n, nb - 1), 0))],
        out_specs=[pl.BlockSpec((BLOCK, B_WIDTH), lambda n: (jnp.minimum(n, nb - 1), 0)),
                   pl.BlockSpec((BLOCK, 2 * B_KV_WIDTH), lambda n: (jnp.maximum(n - 1, 0), 0)),
                   pl.BlockSpec((8, BLOCK), lambda n: (0, 0))],
        out_shape=[jax.ShapeDtypeStruct((T, B_WIDTH), BF16), jax.ShapeDtypeStruct((T, 2 * B_KV_WIDTH), BF16),
                   jax.ShapeDtypeStruct((8, BLOCK), F32)],
        scratch_shapes=[pltpu.VMEM((BLOCK, B_KV_WIDTH), F32), pltpu.VMEM((BLOCK, B_KV_WIDTH), F32)],
        compiler_params=_params(("arbitrary",)),
    )(proj, proj, proj, proj, proj, q_gain, k_gain, sinks, do)


W_IN, W_A, W_B, W_OUT, W_MI, W_MO = range(6)


def _local_step(x, target, mod8, norm1_gain, norm2_gain, lb_logits, o_gain, q_gain, k_gain, sinks, shards, c_arr, chip_arr):
    relu2 = lambda u: (u, jnp.square(jnp.maximum(u, 0.0)))
    pair, half = {}, {}

    def exchange(ws, grads):
        return _sibling_exchange_plan([_grad_view(g, w) for w, g in zip(ws, grads)])

    def pair_sums(ws, grads, others):
        for w, g, o in zip(ws, grads, others):
            pair[w] = _pair_sum(_grad_view(g, w), o, c_arr, f"pair_sum{w}")

    def sum_slots(ws, slots):
        for w, s in zip(ws, slots):
            half[w] = _sum_slots(pair[w], s, w, chip_arr, f"sum_slots{w}")

    part_in = _cast_into_full({W_IN: shards[W_IN]}, "cast_w_in")[W_IN]
    h, (part_in,) = _norm1_fwd(x, norm1_gain, mod8, plan=_gather_plan({W_IN: part_in}, part="near"))
    parts, (w_in,) = _cast_into_full({w: shards[w] for w in range(1, N_W)}, "cast_rest",
                                     plan=_gather_plan({W_IN: part_in}, pass_at=(0.8,), part="far"))
    proj, (w_mi,) = _mm(h, w_in, name="mm_proj", bn=512, plan=_gather_plan({W_MI: parts[W_MI]}, pass_at=(0.47, 0.72)))
    (o_a, states), (w_a, w_b) = _hgrn_fwd(
        proj, lb_logits, o_gain, plan=_gather_plan({w: parts[w] for w in (W_A, W_B)}, pass_at=(0.4, 0.65)))
    o_b, (w_out,) = _swa_fwd(proj, q_gain, k_gain, sinks, plan=_gather_plan({W_OUT: parts[W_OUT]}, pass_at=(0.3, 0.5)))
    ya = _mm(o_a, w_a, name="mm_branch_a")
    gate_cols = (OFF_GATE_A // MERGE_BC, OFF_GATE_B // MERGE_BC)
    yb, merged = _mm(o_b, w_b, name="mm_branch_b", bn=MERGE_BC, out_dtypes=(F32, BF16),
                     extras=(proj, proj, ya), extra_cols=gate_cols + (0,),
                     epi=lambda acc, ga, gb, ya_: (acc, _sig(ga) * ya_ + _sig(gb) * acc))
    mo = _mm(merged, w_out, name="mm_out")
    x1, h2 = _res_norm2_fwd(x, mo, norm2_gain, mod8)
    (u, act), (w_mo,) = _mm(h2, w_mi, name="mm_mlp_in", out_dtypes=(F32, BF16), epi=relu2,
                            plan=_gather_plan({W_MO: parts[W_MO]}, pass_at=(0.6, 0.9)))
    mlp = _mm(act, w_mo, name="mm_mlp_out")
    dy, dmlp, st_loss = _loss_bwd(x1, mlp, target, mod8)
    def half_blocks(w, own):
        def block(i):
            return 2 * i + (lax.axis_index("c") if own else 1 - lax.axis_index("c"))
        return (1 if W_SHAPES[w][2] else N_CHIPS), block

    def pair_of(w, lhs, rhs, other, name):
        hr, cols = _half_shape(w)
        p = _mm(lhs, rhs, name=name, ta=True, bn=512, a_blocks=half_blocks(w, True), out_dtypes=(BF16,),
                extras=(other,), epi=lambda acc, o: (acc + o,))
        return p.reshape(-1, hr, W_SHAPES[w][1])

    near, far = (0, 1), (2,)
    g_send = _mm(act, dmlp, name="mm_g_mlp_out_send", ta=True, bn=512, a_blocks=half_blocks(W_MO, False))
    du, (g_other,) = _mm(dmlp, w_mo, name="mm_d_act", tb=True, out_dtypes=(BF16,), extras=(u,),
                         epi=lambda acc, uu: (acc * (2.0 * jnp.maximum(uu, 0.0)),), plan=_sibling_share_plan([g_send]))
    pair[W_MO] = pair_of(W_MO, act, dmlp, g_other, "mm_g_mlp_out_own")
    g_send, (part,) = _mm(h2, du, name="mm_g_mlp_in_send", ta=True, bn=512, a_blocks=half_blocks(W_MI, False),
                          plan=_chip_exchange_plan({W_MO: pair[W_MO]}, near))
    dh2, res = _mm(du, w_mi, name="mm_d_h2", tb=True,
                   plan=_join(_chip_exchange_plan({W_MO: pair[W_MO]}, far, {W_MO: part}), _sibling_share_plan([g_send])))
    sum_slots([W_MO], res[:1])
    pair[W_MI] = pair_of(W_MI, h2, du, res[1], "mm_g_mlp_in_own")
    dx1, dmo, st_n2 = _norm2_bwd(dh2, x1, dy, mo, norm2_gain, mod8)
    def merge_bwd(dm, ga, gb, ya_, yb_):
        sa, sb = _sig(ga), _sig(gb)
        return dm * sa, dm * sb, dm * ya_ * sa * (1.0 - sa), dm * yb_ * sb * (1.0 - sb)

    (dya, dyb, dga, dgb), (part,) = _mm(dmo, w_out, name="mm_d_merged", tb=True, bn=MERGE_BC, out_dtypes=(BF16,) * 4,
                                        extras=(proj, proj, ya, yb), extra_cols=gate_cols + (0, 0), epi=merge_bwd,
                                        plan=_chip_exchange_plan({W_MI: pair[W_MI]}, near))
    g_out = _mm(merged, dmo, name="mm_g_out", ta=True, bn=512)
    do_a = _mm(dya, w_a, name="mm_d_oa", tb=True)
    g_a = _mm(o_a, dya, name="mm_g_branch_a", ta=True, bn=512)
    do_b = _mm(dyb, w_b, name="mm_d_ob", tb=True)
    g_b = _mm(o_b, dyb, name="mm_g_branch_b", ta=True, bn=512)
    mid = [W_A, W_B, W_OUT]
    (dqb, dkvb, st_swa), res = _swa_bwd(
        proj, q_gain, k_gain, sinks, do_b,
        plan=_join(_chip_exchange_plan({W_MI: pair[W_MI]}, far, {W_MI: part}), exchange(mid, [g_a, g_b, g_out])))
    sum_slots([W_MI], res[:1])
    pair_sums(mid, [g_a, g_b, g_out], res[1:])
    (dqa, dfa, dia, dgga, d_lb, d_og), parts_mid = _hgrn_bwd(
        proj, lb_logits, o_gain, states, do_a, plan=_chip_exchange_plan({w: pair[w] for w in mid}, near))
    dproj = jnp.concatenate([dqa, dfa, dia, dgga, dqb, dkvb, dga, dgb], axis=1)
    done = [W_A, W_B, W_OUT, W_MI, W_MO]
    g_send, slots_mid = _mm(h, dproj, name="mm_g_in_send", ta=True, bn=512, a_blocks=half_blocks(W_IN, False),
                            plan=_chip_exchange_plan({w: pair[w] for w in mid}, far, dict(zip(mid, parts_mid))))
    sum_slots(mid, slots_mid)
    g_own, res = _mm(h, dproj, name="mm_g_in_own", ta=True, bn=512, a_blocks=half_blocks(W_IN, True),
                     plan=_sibling_share_plan([g_send] + [half[w] for w in done]))
    g_other, theirs = res[0], dict(zip(done, res[1:]))
    pair[W_IN] = _add_bf16(g_own, g_other, "pair_sum0")[None]
    dh, slots_in = _mm(dproj, w_in, name="mm_d_h", tb=True, bk=2432, plan=_chip_exchange_plan({W_IN: pair[W_IN]}))
    sum_slots([W_IN], slots_in)
    grad_x, st_n1 = _norm1_bwd(dh, x, dx1, norm1_gain, mod8)
    (theirs[W_IN],) = _run_plan(_sibling_share_plan([half[W_IN]]), "sibling_share_w_in")
    stats = dict(loss=st_loss, n2=st_n2, n1=st_n1, d_lb=d_lb, d_og=d_og, swa=st_swa)
    return grad_x, [half[w] for w in range(N_W)], [theirs[w] for w in range(N_W)], stats


def _ew_rows(rows, cols):
    br = 8
    while br * 2 <= rows and br * 2 * cols * 4 <= (1 << 20) and rows % (br * 2) == 0:
        br *= 2
    return br


CAST_STEPS = 16


def _cast_into_full(shards, name, plan=None):
    ws = sorted(shards)
    in_specs, out_specs, out_shape = [], [], []
    for w in ws:
        sr, sc = shards[w].shape
        R, C, by_col = W_SHAPES[w]
        br = sr // CAST_STEPS
        assert br * CAST_STEPS == sr and br % 16 == 0, (w, sr)

        def out_map(i, by_col=by_col):
            chip = 2 * lax.axis_index("x") + lax.axis_index("y")
            return (i, chip) if by_col else (chip * CAST_STEPS + i, 0)

        in_specs.append(pl.BlockSpec((br, sc), lambda i: (i, 0)))
        out_specs.append(pl.BlockSpec((br, sc), out_map))
        out_shape.append(jax.ShapeDtypeStruct((R, C), BF16))

    def body(*refs):
        for w_ref, o_ref in zip(refs[:len(ws)], refs[len(ws):]):
            o_ref[...] = w_ref[...].astype(BF16)

    res = _pcall(body, plan=plan, name=name, grid=(CAST_STEPS,), in_specs=in_specs, out_specs=out_specs,
                 out_shape=out_shape, compiler_params=_params(("arbitrary",)))(*[shards[w] for w in ws])
    if plan is None:
        return dict(zip(ws, res))
    return dict(zip(ws, res[0])), res[1]


def _adamw_math(w, g, m, v):
    m = ADAM_B1 * m + (1.0 - ADAM_B1) * g
    v = ADAM_B2 * v + (1.0 - ADAM_B2) * (g * g)
    m_hat = m / (1.0 - ADAM_B1 ** ADAM_STEP)
    v_hat = v / (1.0 - ADAM_B2 ** ADAM_STEP)
    delta = -ADAM_LR * (m_hat / (jnp.sqrt(v_hat) + ADAM_EPS) + ADAM_WD * w)
    return delta, m, v


def _adamw(w, g, m, v, name):
    R, C = w.shape
    br = _ew_rows(R, C)
    spec = pl.BlockSpec((br, C), lambda i: (i, 0))

    def body(w_ref, g_ref, m_ref, v_ref, d_ref, nm_ref, nv_ref):
        d_ref[...], nm_ref[...], nv_ref[...] = _adamw_math(w_ref[...], g_ref[...], m_ref[...], v_ref[...])

    sh = jax.ShapeDtypeStruct((R, C), F32)
    return _pcall(body, name=name, grid=(R // br,), in_specs=[spec] * 4, out_specs=[spec] * 3, out_shape=[sh] * 3,
                  compiler_params=_params(("parallel",)))(w, g, m, v)


def _add_bf16(a, b, name):
    R, C = a.shape
    br = _ew_rows(R, C)
    spec = pl.BlockSpec((br, C), lambda i: (i, 0))

    def body(a_ref, b_ref, o_ref):
        o_ref[...] = (a_ref[...] + b_ref[...]).astype(BF16)

    return _pcall(body, name=name, grid=(R // br,), in_specs=[spec, spec], out_specs=spec,
                  out_shape=jax.ShapeDtypeStruct((R, C), BF16), compiler_params=_params(("parallel",)))(a, b)


def _adamw_halves(w, own, other, m, v, c_arr, name):
    R, C = w.shape
    hr = R // 2
    br = _ew_rows(hr, C)
    nb = hr // br
    full = pl.BlockSpec((br, C), lambda h, i, c_ref: (h * nb + i, 0))
    half = pl.BlockSpec((br, C), lambda h, i, c_ref: (i, 0))

    def body(c_ref, w_ref, own_ref, oth_ref, m_ref, v_ref, g_ref, d_ref, nm_ref, nv_ref):
        g = jnp.where(pl.program_id(0) == c_ref[0], own_ref[...], oth_ref[...])
        g_ref[...] = g
        d_ref[...], nm_ref[...], nv_ref[...] = _adamw_math(w_ref[...], g, m_ref[...], v_ref[...])

    sh = jax.ShapeDtypeStruct((R, C), F32)
    return _pcall(
        body, name=name,
        grid_spec=pltpu.PrefetchScalarGridSpec(
            num_scalar_prefetch=1, grid=(2, nb), in_specs=[full, half, half, full, full], out_specs=[full] * 4),
        out_shape=[sh] * 4, compiler_params=_params(("parallel", "parallel")))(c_arr, w, own, other, m, v)


def _ada_grad_adamw(c_t, dmod, w, m, v):
    R, C = w.shape
    br = _ew_rows(R, C)
    spec = pl.BlockSpec((br, C), lambda i: (i, 0))

    def body(c_ref, dm_ref, w_ref, m_ref, v_ref, g_ref, d_ref, nm_ref, nv_ref):
        cv = c_ref[...]
        sc = cv * _sig(cv)
        g = sc[:, 0:1] * dm_ref[0:1, :]
        for b in range(1, N_DEV):
            g = g + sc[:, b:b + 1] * dm_ref[b:b + 1, :]
        g_ref[...] = g
        d_ref[...], nm_ref[...], nv_ref[...] = _adamw_math(w_ref[...], g, m_ref[...], v_ref[...])

    sh = jax.ShapeDtypeStruct((R, C), F32)
    return _pcall(
        body, name="ada_grad_adamw", grid=(R // br,),
        in_specs=[pl.BlockSpec((br, N_DEV), lambda i: (i, 0)), pl.BlockSpec((N_DEV, C), lambda i: (0, 0)), spec, spec, spec],
        out_specs=[spec] * 4, out_shape=[sh] * 4, compiler_params=_params(("parallel",)))(c_t, dmod, w, m, v)


SMALL_ROWS = 16


def _small_sum(small_all, lb_logits):
    def body(s_ref, lbl_ref, o_ref):
        acc = s_ref[0:SMALL_ROWS, :]
        for d in range(1, N_DEV):
            acc = acc + s_ref[d * SMALL_ROWS:(d + 1) * SMALL_ROWS, :]
        o_ref[...] = acc
        z = lbl_ref[...]
        e = jnp.exp(z - jnp.max(z, axis=0, keepdims=True))
        p0 = e[0:1, :] / (e[0:1, :] + e[1:2, :])
        dz = acc[8:9, 0:A_WIDTH] * p0 * (1.0 - p0)
        o_ref[8:9, 0:A_WIDTH] = dz
        o_ref[10:11, 0:A_WIDTH] = -dz

    return _pcall(body, name="small_sum", out_shape=jax.ShapeDtypeStruct((SMALL_ROWS, D_MODEL), F32),
                  in_specs=[pl.BlockSpec(memory_space=pltpu.VMEM)] * 2, out_specs=pl.BlockSpec(memory_space=pltpu.VMEM),
                  compiler_params=_params())(small_all, lb_logits)


RELATIONS = ((1, 0), (0, 1), (1, 1))
ANY = pl.BlockSpec(memory_space=pl.ANY)


def _place():
    x, y, c = lax.axis_index("x"), lax.axis_index("y"), lax.axis_index("c")
    return x, y, c


def _allgather_small(x_shard, name):
    m_per, n = x_shard.shape

    def body(x_ref, out_ref, send_sems, recv_sems, local_sem):
        x, y, c = _place()
        me, sibling = (x, y, c), (x, y, 1 - c)
        chips = [(1 - x, y), (x, 1 - y), (1 - x, 1 - y)]

        def rows(px, py, pc):
            return out_ref.at[pl.ds((4 * px + 2 * py + pc) * m_per, m_per), :]

        def copy(k, block, to, src=None):
            return pltpu.make_async_remote_copy(
                src_ref=rows(*block) if src is None else src, dst_ref=rows(*block),
                send_sem=send_sems.at[k], recv_sem=recv_sems.at[k], device_id=to, device_id_type=MESH)

        mine = pltpu.make_async_copy(x_ref, rows(*me), local_sem)
        mine.start()
        first = [copy(0, me, sibling, src=x_ref)]
        first += [copy(1 + j, me, (*chip, c), src=x_ref) for j, chip in enumerate(chips)]
        for cp in first:
            cp.start()
        passed = [copy(4 + j, (*chip, c), sibling) for j, chip in enumerate(chips)]
        for j, chip in enumerate(chips):
            copy(1 + j, (*chip, c), me).wait_recv()
            passed[j].start()
        copy(0, sibling, me).wait_recv()
        for j, chip in enumerate(chips):
            copy(4 + j, (*chip, 1 - c), me).wait_recv()
        for cp in first + passed:
            cp.wait_send()
        mine.wait()

    return _pcall(
        body, name=name, out_shape=jax.ShapeDtypeStruct((N_DEV * m_per, n), x_shard.dtype),
        in_specs=[pl.BlockSpec(memory_space=pltpu.VMEM)], out_specs=pl.BlockSpec(memory_space=pltpu.VMEM),
        scratch_shapes=[pltpu.SemaphoreType.DMA((7,)), pltpu.SemaphoreType.DMA((7,)), pltpu.SemaphoreType.DMA],
        compiler_params=_params(),
    )(x_shard)


W_SHAPES = ((D_MODEL, IN_WIDTH, True), (A_WIDTH, D_MODEL, True), (B_WIDTH, D_MODEL, True),
            (D_MODEL, D_MODEL, False), (D_MODEL, MLP_HIDDEN, True), (MLP_HIDDEN, D_MODEL, False))
N_W = len(W_SHAPES)


def _shard_shape(w):
    R, C, by_col = W_SHAPES[w]
    return (R, C // N_CHIPS) if by_col else (R // N_CHIPS, C)


def _half_shape(w):
    sr, sc = _shard_shape(w)
    return sr // 2, sc


def _region(full_ref, w, chip, half, quarter=None):
    sr, sc = _shard_shape(w)
    by_col = W_SHAPES[w][2]
    r0, c0 = (0, chip * sc) if by_col else (chip * sr, 0)
    r0, rows = r0 + half * (sr // 2), sr // 2
    if quarter is not None:
        r0, rows = r0 + quarter * (rows // 2), rows // 2
    return full_ref.at[pl.ds(r0, rows), pl.ds(c0, sc)]


def _on_device(fn):
    x, y, c = _place()
    me = 4 * x + 2 * y + c
    for d in range(N_DEV):
        @pl.when(me == d)
        def _(d=d):
            fn(x, y, c, d)


GATHER_COPIES = (
    (0, 0, None, "x"), (0, 0, None, "y"),
    (1, 2, 0, "y"), (1, 1, 1, "x"),
    (1, 2, None, "s"), (1, 1, None, "s"),
    (2, 3, 0, "s"), (2, 3, 1, "s"),
)
PEER_FLIP = {"x": 2, "y": 1, "s": 0}


GATHER_STAGES = {
    None: (((), (0, 1), ()), ((0, 1), (2, 3, 4, 5), ()), ((2, 3), (6, 7), ()), ((4, 5, 6, 7), (), tuple(range(8)))),
    "near": (((), (0, 1), ()), ((0, 1), (), (0, 1))),
    "far": (((), (2, 3, 4, 5), ()), ((2, 3), (6, 7), ()), ((4, 5, 6, 7), (), (2, 3, 4, 5, 6, 7))),
}


def _gather_plan(partials, pass_at=(0.5, 0.75), part=None):
    ws = sorted(partials)
    n_t = len(GATHER_COPIES)
    jobs = [(i, w) for i, w in enumerate(ws)]

    def copy(pi, po, ps, x, y, c, d, i, w, t, landing):
        chip, dc = d >> 1, d & 1
        stage, flip, quarter, to = GATHER_COPIES[t]
        if landing:
            peer_chip = chip ^ PEER_FLIP[to]
            part = _region(po[i], w, peer_chip ^ flip, (1 - dc) if to == "s" else dc, quarter)
            src = part
        else:
            part = _region(po[i], w, chip ^ flip, dc, quarter)
            here = flip != 0 and (part_of is None or stage == 2)
            src = part if here else _region(pi[i], w, chip ^ flip, dc, quarter)
        target = {"x": (x ^ 1, y, c), "y": (x, y ^ 1, c), "s": (x, y, 1 - c)}[to]
        return pltpu.make_async_remote_copy(
            src_ref=src, dst_ref=part, send_sem=ps[0].at[i * n_t + t], recv_sem=ps[1].at[i * n_t + t],
            device_id=target, device_id_type=MESH)

    part_of = part

    def stage(landed, started, sent):
        def run(pi, po, ps):
            def on(x, y, c, d):
                for i, w in jobs:
                    for t in landed:
                        copy(pi, po, ps, x, y, c, d, i, w, t, True).wait_recv()
                for i, w in jobs:
                    for t in started:
                        copy(pi, po, ps, x, y, c, d, i, w, t, False).start()
                for i, w in jobs:
                    for t in sent:
                        copy(pi, po, ps, x, y, c, d, i, w, t, False).wait_send()
            _on_device(on)
        return run

    stages = [stage(*st) for st in GATHER_STAGES[part]]
    mid_at = tuple(pass_at) if part is None else tuple(pass_at)[:len(stages) - 2]
    return _Plan([partials[w] for w in ws], [jax.ShapeDtypeStruct(W_SHAPES[w][:2], BF16) for w in ws],
                 [pltpu.SemaphoreType.DMA((n_t * len(ws),)) for _ in range(2)], stages,
                 {i: i for i in range(len(ws))}, mid_at=mid_at)


def _grad_view(g, w):
    R, C, by_col = W_SHAPES[w]
    return g.reshape(1, 2, R // 2, C) if by_col else g.reshape(N_CHIPS, 2, R // N_CHIPS // 2, C)


def _start_wait_plan(ins, outs, n_copies, copies):
    def start(pi, po, ps):
        for cp in copies(pi, po, ps):
            cp.start()

    def finish(pi, po, ps):
        for cp in copies(pi, po, ps):
            cp.wait()

    return _Plan(ins, outs, [pltpu.SemaphoreType.DMA((n_copies,)), pltpu.SemaphoreType.DMA((n_copies,))], [start, finish])


def _sibling_exchange_plan(g4s):
    pieces = [(i, p) for i, g in enumerate(g4s) for p in range(g.shape[0])]

    def copies(pi, po, ps):
        x, y, c = _place()
        return [pltpu.make_async_remote_copy(
            src_ref=pi[i].at[p, 1 - c], dst_ref=po[i].at[p], send_sem=ps[0].at[n], recv_sem=ps[1].at[n],
            device_id=(x, y, 1 - c), device_id_type=MESH) for n, (i, p) in enumerate(pieces)]

    return _start_wait_plan(list(g4s), [jax.ShapeDtypeStruct((g.shape[0],) + g.shape[2:], F32) for g in g4s],
                            len(pieces), copies)


def _pair_sum(g4, other, c_arr, name):
    P, _, hr, C = g4.shape
    br = _ew_rows(hr, C)

    def body(c_ref, g_ref, o_ref, p_ref):
        p_ref[...] = (g_ref[...] + o_ref[...]).astype(BF16)

    return _pcall(
        body, name=name,
        grid_spec=pltpu.PrefetchScalarGridSpec(
            num_scalar_prefetch=1, grid=(P, hr // br),
            in_specs=[pl.BlockSpec((None, None, br, C), lambda p, i, c_ref: (p, c_ref[0], i, 0)),
                      pl.BlockSpec((None, br, C), lambda p, i, c_ref: (p, i, 0))],
            out_specs=pl.BlockSpec((None, br, C), lambda p, i, c_ref: (p, i, 0))),
        out_shape=jax.ShapeDtypeStruct((P, hr, C), BF16),
        compiler_params=_params(("parallel", "parallel")),
    )(c_arr, g4, other)


def _pair_part(p_ref, w, chip):
    sr, sc = _shard_shape(w)
    return p_ref.at[0, :, pl.ds(chip * sc, sc)] if W_SHAPES[w][2] else p_ref.at[chip]


def _chip_exchange_plan(pairs, rels=(0, 1, 2), into=None):
    ws = sorted(pairs)
    n = len(ws)

    def stage(wait):
        def run(pi, po, ps):
            def on(x, y, c, d):
                for i, w in enumerate(ws):
                    for k, (rx, ry) in enumerate(RELATIONS):
                        if k not in rels:
                            continue
                        cp = pltpu.make_async_remote_copy(
                            src_ref=_pair_part(pi[i], w, (d >> 1) ^ (2 * rx + ry)), dst_ref=po[i].at[k],
                            send_sem=ps[0].at[i * 3 + k], recv_sem=ps[1].at[i * 3 + k],
                            device_id=(x ^ rx, y ^ ry, c), device_id_type=MESH)
                        if wait:
                            cp.wait()
                        else:
                            cp.start()
            _on_device(on)
        return run

    ins = [pairs[w] for w in ws] + ([into[w] for w in ws] if into else [])
    return _Plan(ins, [jax.ShapeDtypeStruct((3,) + _half_shape(w), BF16) for w in ws],
                 [pltpu.SemaphoreType.DMA((3 * n,)), pltpu.SemaphoreType.DMA((3 * n,))],
                 [stage(False), stage(True)], {n + i: i for i in range(n)} if into else None)


def _sum_slots(pair, slots, w, chip_arr, name):
    _, hr, C = slots.shape
    br = _ew_rows(hr, C)
    own_map = (lambda i, chip: (0, i, chip[0])) if W_SHAPES[w][2] else (lambda i, chip: (chip[0], i, 0))

    def body(chip_ref, p_ref, s_ref, o_ref):
        acc = p_ref[...].astype(F32)
        for k in range(3):
            acc = acc + s_ref[k].astype(F32)
        o_ref[...] = acc

    return _pcall(
        body, name=name,
        grid_spec=pltpu.PrefetchScalarGridSpec(
            num_scalar_prefetch=1, grid=(hr // br,),
            in_specs=[pl.BlockSpec((None, br, C), own_map), pl.BlockSpec((3, br, C), lambda i, chip: (0, i, 0))],
            out_specs=pl.BlockSpec((br, C), lambda i, chip: (i, 0))),
        out_shape=jax.ShapeDtypeStruct((hr, C), F32), compiler_params=_params(("parallel",)),
    )(chip_arr, pair, slots)


def _sibling_share_plan(halves):
    def copies(pi, po, ps):
        x, y, c = _place()
        return [pltpu.make_async_remote_copy(
            src_ref=pi[i], dst_ref=po[i], send_sem=ps[0].at[i], recv_sem=ps[1].at[i],
            device_id=(x, y, 1 - c), device_id_type=MESH) for i in range(len(halves))]

    return _start_wait_plan(list(halves), [jax.ShapeDtypeStruct(h.shape, F32) for h in halves], len(halves), copies)


def _pad_lanes(v, width=D_MODEL):
    return jnp.pad(v, ((0, 0), (0, width - v.shape[1])))


def _pack_small(b_ada, norm1, norm2, lb, o_gain, q_gain, k_gain, sinks):
    rows = [b_ada.reshape(N_MOD, D_MODEL), norm1, norm2, jnp.concatenate([lb[0:1], o_gain], axis=1),
            _pad_lanes(jnp.concatenate([q_gain, k_gain, sinks], axis=1)), _pad_lanes(lb[1:2]),
            jnp.zeros((SMALL_ROWS - 11, D_MODEL), F32)]
    return jnp.concatenate(rows, axis=0)


def _unpack_small(p):
    return (p[0:6].reshape(1, N_MOD * D_MODEL), p[6:7], p[7:8],
            jnp.concatenate([p[8:9, 0:A_WIDTH], p[10:11, 0:A_WIDTH]], axis=0), p[8:9, A_WIDTH:],
            p[9:10, 0:64], p[9:10, 64:128], p[9:10, 128:144])


def kernel(x, c, w_ada, b_ada, norm1_gain, w_in, lb_logits, hgrn_o_gain, q_norm_gain, k_norm_gain, sinks, w_branch_a, w_branch_b, w_out, norm2_gain, w_mlp_in, w_mlp_out, loss_target, m_w_ada, m_b_ada, m_norm1_gain, m_w_in, m_lb_logits, m_hgrn_o_gain, m_q_norm_gain, m_k_norm_gain, m_sinks, m_w_branch_a, m_w_branch_b, m_w_out, m_norm2_gain, m_w_mlp_in, m_w_mlp_out, v_w_ada, v_b_ada, v_norm1_gain, v_w_in, v_lb_logits, v_hgrn_o_gain, v_q_norm_gain, v_k_norm_gain, v_sinks, v_w_branch_a, v_w_branch_b, v_w_out, v_norm2_gain, v_w_mlp_in, v_w_mlp_out):
    xi, yi, ci = _place()
    chip = 2 * xi + yi
    me = 4 * xi + 2 * yi + ci
    ada_cols = w_ada.shape[2]

    c_all = _allgather_small(jnp.broadcast_to(c, (8, D_MODEL)), "gather_c").reshape(N_DEV, 8, D_MODEL)[:, 0]
    b_cols = lax.dynamic_slice(b_ada, (0, chip * ada_cols), (1, ada_cols))
    mod_part = _ada_fwd(c_all, w_ada[0], b_cols)
    mod_all = _allgather_small(mod_part, "gather_mod").reshape(N_CHIPS, 2, N_DEV, ada_cols)[:, 0]
    mod_mine = lax.dynamic_index_in_dim(mod_all, me, axis=1, keepdims=False).reshape(N_MOD, D_MODEL)
    mod8 = jnp.concatenate([mod_mine, jnp.zeros((2, D_MODEL), F32)], axis=0)

    shards = (w_in[0], w_branch_a[0], w_branch_b[0], w_out[0], w_mlp_in[0], w_mlp_out[0])
    chip_arr = chip.astype(jnp.int32).reshape(1)
    c_arr = ci.astype(jnp.int32).reshape(1)

    grad_x, halves, theirs, st = _local_step(x[0], loss_target[0], mod8, norm1_gain, norm2_gain, lb_logits, hgrn_o_gain,
                                             q_norm_gain, k_norm_gain, sinks, shards, c_arr, chip_arr)
    loss = lax.psum(0.5 * jnp.sum(st["loss"][0]) / D_MODEL, ("x", "y", "c"))
    moments = ((m_w_in, v_w_in), (m_w_branch_a, v_w_branch_a), (m_w_branch_b, v_w_branch_b), (m_w_out, v_w_out),
               (m_w_mlp_in, v_w_mlp_in), (m_w_mlp_out, v_w_mlp_out))
    big = [_adamw_halves(shards[w], halves[w], theirs[w], moments[w][0][0], moments[w][1][0], c_arr, f"adamw{w}")
           for w in range(N_W)]

    swa = st["swa"]
    small = jnp.concatenate([
        st["n1"][1:2], st["n1"][0:1], st["n2"][3:4], st["n2"][1:2], st["n2"][0:1], st["loss"][1:2],
        st["n1"][2:3], st["n2"][2:3], jnp.concatenate([st["d_lb"][0:1], st["d_og"][0:1]], axis=1),
        _pad_lanes(jnp.concatenate([swa[0:1, 0:64], swa[1:2, 0:64], swa[2:3, 0:16]], axis=1)),
        jnp.zeros((SMALL_ROWS - 10, D_MODEL), F32)], axis=0)
    small_all = _allgather_small(small, "gather_small")
    g_small = _small_sum(small_all, lb_logits)
    small_w = (b_ada, norm1_gain, norm2_gain, lb_logits, hgrn_o_gain, q_norm_gain, k_norm_gain, sinks)
    small_m = (m_b_ada, m_norm1_gain, m_norm2_gain, m_lb_logits, m_hgrn_o_gain, m_q_norm_gain, m_k_norm_gain, m_sinks)
    small_v = (v_b_ada, v_norm1_gain, v_norm2_gain, v_lb_logits, v_hgrn_o_gain, v_q_norm_gain, v_k_norm_gain, v_sinks)
    sm = [_unpack_small(t) for t in
          (g_small,) + tuple(_adamw(_pack_small(*small_w), g_small, _pack_small(*small_m), _pack_small(*small_v),
                                    "adamw_small"))]
    g_b, g_n1, g_n2, g_lb, g_og, g_qg, g_kg, g_sk = ([t[i] for t in sm] for i in range(8))

    dmod_all = small_all.reshape(N_DEV, SMALL_ROWS, D_MODEL)[:, 0:N_MOD].reshape(N_DEV, N_MOD * D_MODEL)
    dmod_cols = lax.dynamic_slice(dmod_all, (0, chip * ada_cols), (N_DEV, ada_cols))
    ada = _ada_grad_adamw(c_all.T, dmod_cols, w_ada[0], m_w_ada[0], v_w_ada[0])

    def ordered(k):
        lead = lambda a: a[None]
        return (lead(ada[k]), g_b[k], g_n1[k], lead(big[0][k]), g_lb[k], g_og[k], g_qg[k], g_kg[k], g_sk[k],
                lead(big[1][k]), lead(big[2][k]), lead(big[3][k]), g_n2[k], lead(big[4][k]), lead(big[5][k]))

    return (loss, grad_x[None]) + ordered(0) + ordered(1) + ordered(2) + ordered(3)
```

```python
import functools

import jax
import jax.numpy as jnp
from jax import lax
from jax.experimental import pallas as pl
from jax.experimental.pallas import tpu as pltpu

F32 = jnp.float32
BF16 = jnp.bfloat16
HIGHEST = lax.Precision.HIGHEST
MESH = pl.DeviceIdType.MESH

D_MODEL = 2048
A_WIDTH = 1024
A_HEADS = 8
A_HEAD_DIM = 128
A_CHUNK = 64
B_WIDTH = 1024
B_HEAD_DIM = 64
B_GROUP = 4
B_KV_HEADS = 4
B_KV_WIDTH = 256
BLOCK = 128
MLP_HIDDEN = 8192
IN_WIDTH = 9728
N_MOD = 6
EPS = 1e-6
N_CHIPS = 4
N_DEV = 8

OFF_QA, OFF_FA, OFF_IA, OFF_GA = 0, 1024, 2048, 3072
OFF_QB, OFF_KB, OFF_VB = 4096, 5120, 5376
OFF_GATE_A, OFF_GATE_B = 5632, 7680

ADAM_LR = 0.001
ADAM_B1 = 0.9
ADAM_B2 = 0.999
ADAM_EPS = 1e-08
ADAM_WD = 0.01
ADAM_STEP = 10

VMEM_LIMIT_V7X = 48 * 1024 * 1024
NEG_BIG = -1e30


def _params(sem=None, vmem=VMEM_LIMIT_V7X):
    return pltpu.CompilerParams(dimension_semantics=sem, vmem_limit_bytes=vmem)


class _Plan:
    def __init__(self, ins, outs, sems, stages, aliases=None, mid_at=()):
        self.ins, self.outs, self.sems, self.stages, self.aliases = ins, outs, sems, stages, aliases or {}
        self.mid_at = tuple(mid_at)
        assert len(self.mid_at) == len(stages) - 2


def _join(a, b):
    assert len(a.stages) == 2 and len(b.stages) == 2
    ni, no, ns = len(a.ins), len(a.outs), len(a.sems)

    def stage(k):
        def run(pi, po, ps):
            a.stages[k](pi[:ni], po[:no], ps[:ns])
            b.stages[k](pi[ni:], po[no:], ps[ns:])
        return run

    aliases = dict(a.aliases)
    aliases.update({ni + i: no + o for i, o in b.aliases.items()})
    return _Plan(a.ins + b.ins, a.outs + b.outs, a.sems + b.sems, [stage(0), stage(1)], aliases)


def _pcall(body, plan=None, **kw):
    if plan is None:
        return pl.pallas_call(body, **kw)
    grid = kw["grid"]
    single = not isinstance(kw["out_specs"], (list, tuple))
    in_specs = list(kw["in_specs"])
    out_specs = [kw["out_specs"]] if single else list(kw["out_specs"])
    out_shape = [kw["out_shape"]] if single else list(kw["out_shape"])
    scratch = list(kw.get("scratch_shapes", ()))
    n_in, n_out, n_scr = len(in_specs), len(out_specs), len(scratch)
    n_pi, n_po = len(plan.ins), len(plan.outs)
    total = 1
    for g in grid:
        total *= g
    n_st = len(plan.stages)

    def wrapped(*refs):
        o0 = n_in + n_pi
        s0 = o0 + n_out + n_po
        pi, po, ps = refs[n_in:o0], refs[o0 + n_out:s0], refs[s0 + n_scr:]
        lin = 0
        for d, g in enumerate(grid):
            lin = lin * g + pl.program_id(d)
        for si, frac in enumerate((0.0,) + plan.mid_at):
            @pl.when(lin == int(frac * (total - 1)))
            def _(si=si):
                plan.stages[si](pi, po, ps)
        body(*refs[:n_in], *refs[o0:o0 + n_out], *refs[s0:s0 + n_scr])

        @pl.when(lin == total - 1)
        def _():
            plan.stages[-1](pi, po, ps)

    any_spec = pl.BlockSpec(memory_space=pl.ANY)
    call = pl.pallas_call(
        wrapped, name=kw["name"], grid=grid, in_specs=in_specs + [any_spec] * n_pi,
        out_specs=out_specs + [any_spec] * n_po, out_shape=out_shape + list(plan.outs),
        scratch_shapes=scratch + list(plan.sems),
        input_output_aliases={n_in + i: n_out + o for i, o in plan.aliases.items()},
        compiler_params=_params(("arbitrary",) * len(grid)))

    def run(*args):
        res = call(*args, *plan.ins)
        outs = list(res[:n_out])
        return (outs[0] if single else outs), list(res[n_out:])

    return run


def _run_plan(plan, name):
    return _pcall(lambda: None, plan=plan, name=name, grid=(1,), in_specs=[], out_specs=[], out_shape=[])()[1]


def _sig(x):
    return 1.0 / (1.0 + jnp.exp(-x))


def _nn(a, b):
    return lax.dot_general(a.astype(BF16), b.astype(BF16), (((1,), (0,)), ((), ())), preferred_element_type=F32)


def _nt(a, b):
    return lax.dot_general(a.astype(BF16), b.astype(BF16), (((1,), (1,)), ((), ())), preferred_element_type=F32)


def _tn(a, b):
    return lax.dot_general(a.astype(BF16), b.astype(BF16), (((0,), (0,)), ((), ())), preferred_element_type=F32)


def _mm(a, b, *, name, ta=False, tb=False, bm=1024, bn=1024, bk=2048, out_dtypes=(F32,), epi=None, extras=(),
        extra_cols=None, plan=None, a_blocks=None, row_extras=(), n_stats=0):
    if ta:
        K, M = a.shape
        bk = K
        if a_blocks is not None:
            M = a_blocks[0] * bm
    else:
        M, K = a.shape
    if tb:
        N, K2 = b.shape
    else:
        K2, N = b.shape
    bm, bn, bk = min(bm, M), min(bn, N), min(bk, K)
    assert K == K2 and M % bm == 0 and N % bn == 0 and K % bk == 0, (name, a.shape, b.shape)
    nk = K // bk
    a_col = a_blocks[1] if a_blocks is not None else (lambda i: i)
    a_spec = pl.BlockSpec((bk, bm), lambda i, j, k: (k, a_col(i))) if ta else pl.BlockSpec((bm, bk), lambda i, j, k: (i, k))
    b_spec = pl.BlockSpec((bn, bk), lambda i, j, k: (j, k)) if tb else pl.BlockSpec((bk, bn), lambda i, j, k: (k, j))
    t_spec = pl.BlockSpec((bm, bn), lambda i, j, k: (i, j))
    extra_cols = extra_cols or (0,) * len(extras)
    e_specs = [pl.BlockSpec((bm, bn), lambda i, j, k, off=off: (i, off + j)) for off in extra_cols]
    e_specs += [pl.BlockSpec((8, bn), lambda i, j, k: (0, j)) for _ in row_extras]
    dims = (((1,), (1 if tb else 0,)), ((), ()))
    n_e, n_o = len(extras) + len(row_extras), len(out_dtypes)
    stat_spec = pl.BlockSpec((8, bn), lambda i, j, k: (i, j))

    def body(*refs):
        a_ref, b_ref = refs[0], refs[1]
        e_refs = refs[2:2 + n_e]
        o_refs = refs[2 + n_e:2 + n_e + n_o]

        def finish(acc):
            outs = (acc,) if epi is None else epi(acc, *[e[...] for e in e_refs])
            for o_ref, o in zip(o_refs, outs):
                o_ref[...] = o.astype(o_ref.dtype)

        if ta:
            at_ref = refs[-1]

            @pl.when(pl.program_id(1) == 0)
            def _():
                at_ref[...] = a_ref[...].T

            lhs = at_ref[...]
        else:
            lhs = a_ref[...].astype(BF16)
        part = lax.dot_general(lhs, b_ref[...].astype(BF16), dims, preferred_element_type=F32)
        if nk == 1:
            finish(part)
        else:
            acc_ref = refs[-1]
            k = pl.program_id(2)

            @pl.when(k == 0)
            def _():
                acc_ref[...] = part

            @pl.when(k > 0)
            def _():
                acc_ref[...] += part

            @pl.when(k == nk - 1)
            def _():
                finish(acc_ref[...])

    if ta:
        assert a.dtype == BF16 and nk == 1
        scratch = [pltpu.VMEM((bm, bk), BF16)]
    else:
        scratch = [pltpu.VMEM((bm, bn), F32)] if nk > 1 else []
    out = _pcall(
        body, plan=plan, name=name, grid=(M // bm, N // bn, nk),
        in_specs=[a_spec, b_spec] + e_specs,
        out_specs=[t_spec] * (n_o - n_stats) + [stat_spec] * n_stats,
        out_shape=[jax.ShapeDtypeStruct((M, N), dt) for dt in out_dtypes[:n_o - n_stats]]
        + [jax.ShapeDtypeStruct((8 * (M // bm), N), F32)] * n_stats,
        scratch_shapes=scratch,
        compiler_params=_params(("parallel", "arbitrary", "arbitrary")),
    )(a, b, *extras, *row_extras)
    if plan is not None:
        return (out[0][0] if n_o == 1 else out[0]), out[1]
    return out[0] if n_o == 1 else out


def _ada_fwd(c_all, w_ada, b_cols):
    n = w_ada.shape[1]
    bn = 512

    def body(c_ref, w_ref, b_ref, o_ref):
        cv = c_ref[...]
        sc = cv * _sig(cv)
        o_ref[...] = jnp.dot(sc, w_ref[...], precision=HIGHEST, preferred_element_type=F32) + b_ref[...]

    return _pcall(
        body, name="ada_fwd", grid=(n // bn,),
        in_specs=[pl.BlockSpec((N_DEV, D_MODEL), lambda j: (0, 0)), pl.BlockSpec((D_MODEL, bn), lambda j: (0, j)),
                  pl.BlockSpec((1, bn), lambda j: (0, j))],
        out_specs=pl.BlockSpec((N_DEV, bn), lambda j: (0, j)),
        out_shape=jax.ShapeDtypeStruct((N_DEV, n), F32),
        compiler_params=_params(("parallel",)),
    )(c_all, w_ada, b_cols)


ROWS_EW = 256


def _rms_fwd_math(x, gain, scale, shift):
    rstd = lax.rsqrt(jnp.mean(x * x, axis=-1, keepdims=True) + EPS)
    xhat = x * rstd
    n = xhat * gain
    return n * (1.0 + scale) + shift, xhat, n, rstd


def _rms_bwd_math(dh, xhat, n, rstd, gain, scale):
    dn = dh * (1.0 + scale)
    dxhat = dn * gain
    dx = rstd * (dxhat - xhat * jnp.mean(dxhat * xhat, axis=-1, keepdims=True))
    d_scale = jnp.sum(dh * n, axis=0, keepdims=True)
    d_shift = jnp.sum(dh, axis=0, keepdims=True)
    d_gain = jnp.sum(dn * xhat, axis=0, keepdims=True)
    return dx, d_scale, d_shift, d_gain


def _row_spec(w=D_MODEL, br=ROWS_EW):
    return pl.BlockSpec((br, w), lambda i: (i, 0))


def _vec_spec(r=8, w=D_MODEL):
    return pl.BlockSpec((r, w), lambda i: (0, 0))


def _norm1_fwd(x, gain, mod8, plan=None):
    T = x.shape[0]

    def body(x_ref, g_ref, m_ref, h_ref):
        h, _, _, _ = _rms_fwd_math(x_ref[...], g_ref[...], m_ref[1:2, :], m_ref[0:1, :])
        h_ref[...] = h.astype(BF16)

    return _pcall(
        body, plan=plan, name="norm1_fwd", grid=(T // ROWS_EW,),
        in_specs=[_row_spec(), _vec_spec(1), _vec_spec()],
        out_specs=_row_spec(), out_shape=jax.ShapeDtypeStruct((T, D_MODEL), BF16),
        compiler_params=_params(("parallel",)),
    )(x, gain, mod8)


def _res_norm2_fwd(x, mo, gain, mod8):
    T = x.shape[0]

    def body(x_ref, mo_ref, g_ref, m_ref, x1_ref, h_ref):
        x1 = x_ref[...] + m_ref[2:3, :] * mo_ref[...]
        x1_ref[...] = x1
        h, _, _, _ = _rms_fwd_math(x1, g_ref[...], m_ref[4:5, :], m_ref[3:4, :])
        h_ref[...] = h.astype(BF16)

    return _pcall(
        body, name="res_norm2_fwd", grid=(T // ROWS_EW,),
        in_specs=[_row_spec(), _row_spec(), _vec_spec(1), _vec_spec()],
        out_specs=[_row_spec(), _row_spec()],
        out_shape=[jax.ShapeDtypeStruct((T, D_MODEL), F32), jax.ShapeDtypeStruct((T, D_MODEL), BF16)],
        compiler_params=_params(("parallel",)),
    )(x, mo, gain, mod8)


def _loss_head(mlp, x1, target, mod):
    gate = mod[5:6, :]
    err = x1 + gate * mlp - target
    dy = err * (1.0 / D_MODEL)
    row = lax.broadcasted_iota(jnp.int32, (8, mlp.shape[1]), 0)
    stats = jnp.where(row == 0, jnp.sum(err * err, axis=0, keepdims=True),
                      jnp.where(row == 1, jnp.sum(dy * mlp, axis=0, keepdims=True), 0.0))
    return dy, dy * gate, stats


def _norm2_bwd(dh2, x1, dy, mo, gain, mod8):
    T = x1.shape[0]

    def body(dh_ref, x1_ref, dy_ref, mo_ref, g_ref, m_ref, dx1_ref, dmo_ref, st_ref):
        i = pl.program_id(0)
        gain_v, scale = g_ref[...], m_ref[4:5, :]
        _, xhat, n, rstd = _rms_fwd_math(x1_ref[...], gain_v, scale, m_ref[3:4, :])
        dx, d_scale, d_shift, d_gain = _rms_bwd_math(dh_ref[...], xhat, n, rstd, gain_v, scale)
        dx1 = dy_ref[...] + dx
        dx1_ref[...] = dx1
        dmo_ref[...] = (dx1 * m_ref[2:3, :]).astype(BF16)

        @pl.when(i == 0)
        def _():
            st_ref[...] = jnp.zeros_like(st_ref)

        st_ref[0:1, :] += d_scale
        st_ref[1:2, :] += d_shift
        st_ref[2:3, :] += d_gain
        st_ref[3:4, :] += jnp.sum(dx1 * mo_ref[...], axis=0, keepdims=True)

    return _pcall(
        body, name="norm2_bwd", grid=(T // ROWS_EW,),
        in_specs=[_row_spec(), _row_spec(), _row_spec(), _row_spec(), _vec_spec(1), _vec_spec()],
        out_specs=[_row_spec(), _row_spec(), _vec_spec()],
        out_shape=[jax.ShapeDtypeStruct((T, D_MODEL), F32), jax.ShapeDtypeStruct((T, D_MODEL), BF16),
                   jax.ShapeDtypeStruct((8, D_MODEL), F32)],
        compiler_params=_params(("arbitrary",)),
    )(dh2, x1, dy, mo, gain, mod8)


def _norm1_bwd(dh, x, dx1, gain, mod8):
    T = x.shape[0]

    def body(dh_ref, x_ref, dx1_ref, g_ref, m_ref, dx_ref, st_ref):
        i = pl.program_id(0)
        gain_v, scale = g_ref[...], m_ref[1:2, :]
        _, xhat, n, rstd = _rms_fwd_math(x_ref[...], gain_v, scale, m_ref[0:1, :])
        dx, d_scale, d_shift, d_gain = _rms_bwd_math(dh_ref[...], xhat, n, rstd, gain_v, scale)
        dx_ref[...] = dx1_ref[...] + dx

        @pl.when(i == 0)
        def _():
            st_ref[...] = jnp.zeros_like(st_ref)

        st_ref[0:1, :] += d_scale
        st_ref[1:2, :] += d_shift
        st_ref[2:3, :] += d_gain

    return _pcall(
        body, name="norm1_bwd", grid=(T // ROWS_EW,),
        in_specs=[_row_spec(), _row_spec(), _row_spec(), _vec_spec(1), _vec_spec()],
        out_specs=[_row_spec(), _vec_spec()],
        out_shape=[jax.ShapeDtypeStruct((T, D_MODEL), F32), jax.ShapeDtypeStruct((8, D_MODEL), F32)],
        compiler_params=_params(("arbitrary",)),
    )(dh, x, dx1, gain, mod8)


MERGE_BC = 512


def _hgrn_rows(T):
    return 512 if T >= 1024 else 128


def _lower_bound(lbl):
    e = jnp.exp(lbl - jnp.max(lbl, axis=0, keepdims=True))
    return e[0:1, :] / (e[0:1, :] + e[1:2, :])


def _chunk_sum_matrix(rows, backward):
    shift = A_CHUNK.bit_length() - 1
    r = lax.broadcasted_iota(jnp.int32, (rows, rows), 0)
    c = lax.broadcasted_iota(jnp.int32, (rows, rows), 1)
    same = jnp.right_shift(r, shift) == jnp.right_shift(c, shift)
    return (same & ((r <= c) if backward else (r >= c))).astype(BF16)


def _chunk_sums(m, x):
    n = x.shape[1]
    hi = x.astype(BF16)
    rest = x - hi.astype(F32)
    mid = rest.astype(BF16)
    lo = (rest - mid.astype(F32)).astype(BF16)
    y = jnp.dot(m, jnp.concatenate([hi, mid, lo], axis=1), preferred_element_type=F32)
    return y[:, 0:n] + y[:, n:2 * n] + y[:, 2 * n:3 * n]


def _hgrn_block_pre(q, fl, lb, m_fwd):
    sg = _sig(fl)
    f = lb + (1.0 - lb) * sg
    sq = _sig(q)
    return dict(sg=sg, f=f, k=1.0 - f, sq=sq, qf=q * sq, b=_chunk_sums(m_fwd, jnp.log(f)))


def _hgrn_chunk_local(pre, r):
    C = A_CHUNK
    qf, k, b = pre["qf"][r], pre["k"][r], pre["b"][r]
    causal = lax.broadcasted_iota(jnp.int32, (C, C), 0) >= lax.broadcasted_iota(jnp.int32, (C, C), 1)
    bm = b[C // 2 - 1:C // 2, :]
    bl = b[C - 1:C, :]
    e_q, e_k = jnp.exp(b - bm), jnp.exp(bm - b)
    e_b, e_l = jnp.exp(b), jnp.exp(bl - b)
    qd, kd = qf * e_q, k * e_k
    qe, ke = qf * e_b, k * e_l
    att = jnp.where(causal, _nt(qd, kd), 0.0)
    return dict(causal=causal, e_q=e_q, e_k=e_k, e_b=e_b, e_l=e_l, qd=qd, kd=kd, qe=qe, ke=ke, att=att, dec=jnp.exp(bl))


def _hgrn_chunk_fwd(pre, r, v, st):
    c = _hgrn_chunk_local(pre, r)
    c["o"] = _nn(c["att"], v) + _nt(c["qe"], st)
    return c


def _lockstep(gens):
    out = [None] * len(gens)
    live = list(enumerate(gens))
    while live:
        still = []
        for i, g in live:
            try:
                next(g)
                still.append((i, g))
            except StopIteration as done:
                out[i] = done.value
        live = still
    return out


HGRN_HEADS_PER_STEP = 4


def _hgrn_fwd(proj, lb_logits, o_gain, plan=None):
    T = proj.shape[0]
    BR = _hgrn_rows(T)
    cps = BR // A_CHUNK
    K, NH = A_HEAD_DIM, HGRN_HEADS_PER_STEP
    W = NH * K

    def col(off):
        return pl.BlockSpec((BR, W), lambda h, cb: (cb, off // W + h))

    def body(q_ref, f_ref, i_ref, g_ref, lbl_ref, og_ref, o_ref, s_ref, st):
        @pl.when(pl.program_id(1) == 0)
        def _():
            st[...] = jnp.zeros_like(st)

        lb_all = _lower_bound(lbl_ref[...])
        m_fwd = _chunk_sum_matrix(BR, False)
        pre = [_hgrn_block_pre(q_ref[:, n * K:(n + 1) * K], f_ref[:, n * K:(n + 1) * K], lb_all[:, n * K:(n + 1) * K], m_fwd)
               for n in range(NH)]
        def local(n, ci):
            r, hs = slice(ci * A_CHUNK, (ci + 1) * A_CHUNK), slice(n * K, (n + 1) * K)
            v = i_ref[r, hs]
            c = _hgrn_chunk_local(pre[n], r)
            yield
            return dict(o=_nn(c["att"], v), ds=_tn(v, c["ke"]), qe=c["qe"], dec=c["dec"])

        def chain(n, loc):
            hs = slice(n * K, (n + 1) * K)
            state = st[n]
            for ci, p in enumerate(loc):
                r = slice(ci * A_CHUNK, (ci + 1) * A_CHUNK)
                s_ref[n, ci] = state
                o = p["o"] + _nt(p["qe"], state)
                state = state * p["dec"] + p["ds"]
                yield
                on = o * lax.rsqrt(jnp.mean(o * o, axis=-1, keepdims=True) + EPS)
                g = g_ref[r, hs]
                o_ref[r, hs] = (on * og_ref[:, hs] * (g * _sig(g))).astype(BF16)
            st[n] = state

        loc = _lockstep([local(n, ci) for n in range(NH) for ci in range(cps)])
        _lockstep([chain(n, loc[n * cps:(n + 1) * cps]) for n in range(NH)])

    return _pcall(
        body, plan=plan, name="hgrn_fwd", grid=(A_HEADS // NH, T // BR),
        in_specs=[col(OFF_QA), col(OFF_FA), col(OFF_IA), col(OFF_GA),
                  pl.BlockSpec((2, W), lambda h, cb: (0, h)), pl.BlockSpec((1, W), lambda h, cb: (0, h))],
        out_specs=[pl.BlockSpec((BR, W), lambda h, cb: (cb, h)),
                   pl.BlockSpec((NH, cps, K, K), lambda h, cb: (h, cb, 0, 0))],
        out_shape=[jax.ShapeDtypeStruct((T, A_WIDTH), BF16),
                   jax.ShapeDtypeStruct((A_HEADS, T // A_CHUNK, K, K), F32)],
        scratch_shapes=[pltpu.VMEM((NH, K, K), F32)],
        compiler_params=_params(("parallel", "arbitrary")),
    )(proj, proj, proj, proj, lb_logits, o_gain)


def _hgrn_bwd(proj, lb_logits, o_gain, states, do, plan=None):
    T = proj.shape[0]
    BR = _hgrn_rows(T)
    cps = BR // A_CHUNK
    ncb = T // BR
    K, C, NH = A_HEAD_DIM, A_CHUNK, HGRN_HEADS_PER_STEP
    W = NH * K

    def col(off):
        return pl.BlockSpec((BR, W), lambda h, cb: (ncb - 1 - cb, off // W + h))

    def body(q_ref, f_ref, i_ref, g_ref, lbl_ref, og_ref, s_ref, do_ref,
             dq_ref, df_ref, di_ref, dg_ref, dlb_ref, dog_ref, dst):
        @pl.when(pl.program_id(1) == 0)
        def _():
            dst[...] = jnp.zeros_like(dst)
            dlb_ref[...] = jnp.zeros_like(dlb_ref)
            dog_ref[...] = jnp.zeros_like(dog_ref)

        lb_all = _lower_bound(lbl_ref[...])
        row = lax.broadcasted_iota(jnp.int32, (C, K), 0)
        m_fwd, m_bwd = _chunk_sum_matrix(BR, False), _chunk_sum_matrix(BR, True)
        pre = [_hgrn_block_pre(q_ref[:, n * K:(n + 1) * K], f_ref[:, n * K:(n + 1) * K], lb_all[:, n * K:(n + 1) * K], m_fwd)
               for n in range(NH)]
        def local(n, ci):
            r, hs = slice(ci * C, (ci + 1) * C), slice(n * K, (n + 1) * K)
            gain = og_ref[:, hs]
            st = s_ref[n, ci]
            v = i_ref[r, hs]
            q = q_ref[r, hs]
            c = _hgrn_chunk_fwd(pre[n], r, v, st)
            yield
            o = c["o"]
            rn = lax.rsqrt(jnp.mean(o * o, axis=-1, keepdims=True) + EPS)
            on = o * rn
            g = g_ref[r, hs]
            sgg = _sig(g)
            dy = do_ref[r, hs]
            d_ong = dy * (g * sgg)
            dg_ref[r, hs] = (dy * (on * gain) * (sgg * (1.0 + g * (1.0 - sgg)))).astype(BF16)
            d_on = d_ong * gain
            d_o = rn * (d_on - on * jnp.mean(d_on * on, axis=-1, keepdims=True))
            datt = jnp.where(c["causal"], _nt(d_o, v), 0.0)
            dqe = _nn(d_o, st)
            yield
            dqd = _nn(datt, c["kd"])
            dkd = _tn(datt, c["qd"])
            dv = _tn(c["att"], d_o)
            ds = _tn(d_o, c["qe"])
            yield
            t_q, t_k = dqd * c["qd"], dkd * c["kd"]
            sq = pre[n]["sq"][r]
            dq_ref[r, hs] = ((dqd * c["e_q"] + dqe * c["e_b"]) * (sq * (1.0 + q * (1.0 - sq)))).astype(BF16)
            return dict(v=v, st=st, ke=c["ke"], e_l=c["e_l"], dec=c["dec"], dv=dv, ds=ds, dk=dkd * c["e_k"],
                        db=t_q - t_k + dqe * c["qe"], dbm=jnp.sum(t_k - t_q, axis=0, keepdims=True),
                        d_og=jnp.sum(d_ong * on, axis=0, keepdims=True))

        def chain(n, loc):
            hs = slice(n * K, (n + 1) * K)
            dst_next = dst[n]
            db_of, dk_of = [None] * cps, [None] * cps
            for ci in reversed(range(cps)):
                p = loc[ci]
                di_ref[ci * C:(ci + 1) * C, hs] = (p["dv"] + _nt(p["ke"], dst_next)).astype(BF16)
                dke = _nn(p["v"], dst_next)
                yield
                t_l = dke * p["ke"]
                dbl = jnp.sum(t_l, axis=0, keepdims=True) + jnp.sum(dst_next * p["st"], axis=0, keepdims=True) * p["dec"]
                db_of[ci] = p["db"] - t_l + jnp.where(row == C // 2 - 1, p["dbm"], 0.0) + jnp.where(row == C - 1, dbl, 0.0)
                dk_of[ci] = p["dk"] + dke * p["e_l"]
                dst_next = dst_next * p["dec"] + p["ds"]
            dst[n] = dst_next
            return db_of, dk_of

        loc = _lockstep([local(n, ci) for n in range(NH) for ci in range(cps)])
        loc = [loc[n * cps:(n + 1) * cps] for n in range(NH)]
        chains = _lockstep([chain(n, loc[n]) for n in range(NH)])
        for n in range(NH):
            hs = slice(n * K, (n + 1) * K)
            db_of, dk_of = chains[n]
            d_og = loc[n][0]["d_og"]
            for p in loc[n][1:]:
                d_og = d_og + p["d_og"]
            dog_ref[0:1, hs] += d_og
            lb, sg = lb_all[:, hs], pre[n]["sg"]
            dlf = _chunk_sums(m_bwd, jnp.concatenate(db_of, axis=0))
            df = dlf / pre[n]["f"] - jnp.concatenate(dk_of, axis=0)
            df_ref[:, hs] = (df * (1.0 - lb) * sg * (1.0 - sg)).astype(BF16)
            dlb_ref[0:1, hs] += jnp.sum(df * (1.0 - sg), axis=0, keepdims=True)

    ocol = pl.BlockSpec((BR, W), lambda h, cb: (ncb - 1 - cb, h))
    vec = pl.BlockSpec((8, W), lambda h, cb: (0, h))
    return _pcall(
        body, plan=plan, name="hgrn_bwd", grid=(A_HEADS // NH, ncb),
        in_specs=[col(OFF_QA), col(OFF_FA), col(OFF_IA), col(OFF_GA),
                  pl.BlockSpec((2, W), lambda h, cb: (0, h)), pl.BlockSpec((1, W), lambda h, cb: (0, h)),
                  pl.BlockSpec((NH, cps, K, K), lambda h, cb: (h, ncb - 1 - cb, 0, 0)),
                  pl.BlockSpec((BR, W), lambda h, cb: (ncb - 1 - cb, h))],
        out_specs=[ocol, ocol, ocol, ocol, vec, vec],
        out_shape=[jax.ShapeDtypeStruct((T, A_WIDTH), BF16)] * 4 + [jax.ShapeDtypeStruct((8, A_WIDTH), F32)] * 2,
        scratch_shapes=[pltpu.VMEM((NH, K, K), F32)],
        compiler_params=_params(("parallel", "arbitrary")),
    )(proj, proj, proj, proj, lb_logits, o_gain, states, do)


def _head_norm(x):
    r = lax.rsqrt(jnp.mean(x * x, axis=-1, keepdims=True) + EPS)
    return x * r, r


def _head_norm_bwd(dy, xn, r, gain):
    dxn = dy * gain
    return r * (dxn - xn * jnp.mean(dxn * xn, axis=-1, keepdims=True)), jnp.sum(dy * xn, axis=0, keepdims=True)


def _swa_mask(has_prev):
    rows = B_GROUP * BLOCK
    r = lax.broadcasted_iota(jnp.int32, (rows, 2 * BLOCK), 0) % BLOCK
    c = lax.broadcasted_iota(jnp.int32, (rows, 2 * BLOCK), 1)
    rel = r + BLOCK - c
    return (rel >= 0) & (rel < BLOCK) & ((c >= BLOCK) | has_prev)


def _swa_head_fwd(j, q_ref, kp_ref, kc_ref, vp_ref, vc_ref, qg, kg, sk_ref, mask):
    hs = slice(j * B_HEAD_DIM, (j + 1) * B_HEAD_DIM)
    kcat = jnp.concatenate([kp_ref[:, hs], kc_ref[:, hs]], axis=0)
    vcat = jnp.concatenate([vp_ref[:, hs], vc_ref[:, hs]], axis=0)
    qs = jnp.concatenate([q_ref[:, pl.ds((j * B_GROUP + g) * B_HEAD_DIM, B_HEAD_DIM)] for g in range(B_GROUP)], axis=0)
    kn, kr = _head_norm(kcat)
    qn, qr = _head_norm(qs)
    kh, qh = kn * kg, qn * qg
    yield
    s = jnp.where(mask, _nt(qh, kh) * (B_HEAD_DIM ** -0.5), NEG_BIG)
    yield
    sink = jnp.concatenate(
        [jnp.broadcast_to(sk_ref[0:1, pl.ds(j * B_GROUP + g, 1)], (BLOCK, 1)) for g in range(B_GROUP)], axis=0)
    m = jnp.maximum(jnp.max(s, axis=-1, keepdims=True), sink)
    p = jnp.exp(s - m)
    e_sink = jnp.exp(sink - m)
    inv = 1.0 / (jnp.sum(p, axis=-1, keepdims=True) + e_sink)
    prob = p * inv
    return dict(vcat=vcat, kn=kn, kr=kr, qn=qn, qr=qr, kh=kh, qh=qh, prob=prob, p_sink=e_sink * inv)


def _swa_in_specs(nb, last):
    def qi(n):
        return jnp.minimum(n, last)

    q = pl.BlockSpec((BLOCK, B_WIDTH), lambda n: (qi(n), OFF_QB // B_WIDTH))
    kc = pl.BlockSpec((BLOCK, B_KV_WIDTH), lambda n: (qi(n), OFF_KB // B_KV_WIDTH))
    kp = pl.BlockSpec((BLOCK, B_KV_WIDTH), lambda n: (jnp.maximum(qi(n) - 1, 0), OFF_KB // B_KV_WIDTH))
    vc = pl.BlockSpec((BLOCK, B_KV_WIDTH), lambda n: (qi(n), OFF_VB // B_KV_WIDTH))
    vp = pl.BlockSpec((BLOCK, B_KV_WIDTH), lambda n: (jnp.maximum(qi(n) - 1, 0), OFF_VB // B_KV_WIDTH))
    small = [pl.BlockSpec((1, B_HEAD_DIM), lambda n: (0, 0)), pl.BlockSpec((1, B_HEAD_DIM), lambda n: (0, 0)),
             pl.BlockSpec((1, B_GROUP * B_KV_HEADS), lambda n: (0, 0))]
    return [q, kp, kc, vp, vc] + small


def _swa_fwd(proj, q_gain, k_gain, sinks, plan=None):
    T = proj.shape[0]
    nb = T // BLOCK

    def body(q_ref, kp_ref, kc_ref, vp_ref, vc_ref, qg_ref, kg_ref, sk_ref, o_ref):
        mask = _swa_mask(pl.program_id(0) > 0)

        def head(j):
            c = yield from _swa_head_fwd(j, q_ref, kp_ref, kc_ref, vp_ref, vc_ref, qg_ref[...], kg_ref[...], sk_ref, mask)
            yield
            o = _nn(c["prob"], c["vcat"])
            yield
            for g in range(B_GROUP):
                o_ref[:, pl.ds((j * B_GROUP + g) * B_HEAD_DIM, B_HEAD_DIM)] = o[g * BLOCK:(g + 1) * BLOCK].astype(BF16)

        _lockstep([head(j) for j in range(B_KV_HEADS)])

    return _pcall(
        body, plan=plan, name="swa_fwd", grid=(nb,),
        in_specs=_swa_in_specs(nb, nb - 1),
        out_specs=pl.BlockSpec((BLOCK, B_WIDTH), lambda n: (n, 0)),
        out_shape=jax.ShapeDtypeStruct((T, B_WIDTH), BF16),
        compiler_params=_params(("parallel",)),
    )(proj, proj, proj, proj, proj, q_gain, k_gain, sinks)


def _swa_bwd(proj, q_gain, k_gain, sinks, do, plan=None):
    T = proj.shape[0]
    nb = T // BLOCK
    scale = B_HEAD_DIM ** -0.5

    def body(q_ref, kp_ref, kc_ref, vp_ref, vc_ref, qg_ref, kg_ref, sk_ref, do_ref,
             dq_ref, dkv_ref, sm_ref, ck, cv):
        n = pl.program_id(0)

        @pl.when(n == 0)
        def _():
            ck[...] = jnp.zeros_like(ck)
            cv[...] = jnp.zeros_like(cv)
            sm_ref[...] = jnp.zeros_like(sm_ref)

        @pl.when(n < nb)
        def _():
            mask = _swa_mask(n > 0)
            qg, kg = qg_ref[...], kg_ref[...]
            lane = lax.broadcasted_iota(jnp.int32, (1, BLOCK), 1)
            def head(j):
                hs = slice(j * B_HEAD_DIM, (j + 1) * B_HEAD_DIM)
                vs = slice(B_KV_WIDTH + j * B_HEAD_DIM, B_KV_WIDTH + (j + 1) * B_HEAD_DIM)
                c = yield from _swa_head_fwd(j, q_ref, kp_ref, kc_ref, vp_ref, vc_ref, qg, kg, sk_ref, mask)
                d_out = jnp.concatenate(
                    [do_ref[:, pl.ds((j * B_GROUP + g) * B_HEAD_DIM, B_HEAD_DIM)] for g in range(B_GROUP)], axis=0)
                prob = c["prob"]
                yield
                out = _nn(prob, c["vcat"])
                d_prob = _nt(d_out, c["vcat"])
                dv = _tn(prob, d_out)
                yield
                delta = jnp.sum(d_out * out, axis=-1, keepdims=True)
                ds = prob * (d_prob - delta)
                d_sink = -c["p_sink"] * delta
                yield
                dqh = _nn(ds, c["kh"]) * scale
                dkh = _tn(ds, c["qh"]) * scale
                yield
                dq, dqg = _head_norm_bwd(dqh, c["qn"], c["qr"], qg)
                dk, dkg = _head_norm_bwd(dkh, c["kn"], c["kr"], kg)
                d_sinks = jnp.zeros((1, BLOCK), F32)
                for g in range(B_GROUP):
                    dq_ref[:, pl.ds((j * B_GROUP + g) * B_HEAD_DIM, B_HEAD_DIM)] = dq[g * BLOCK:(g + 1) * BLOCK].astype(BF16)
                    tot = jnp.sum(d_sink[g * BLOCK:(g + 1) * BLOCK], axis=0, keepdims=True)
                    d_sinks = d_sinks + jnp.where(lane == j * B_GROUP + g, tot, 0.0)
                dkv_ref[:, hs] = (ck[:, hs] + dk[0:BLOCK]).astype(BF16)
                dkv_ref[:, vs] = (cv[:, hs] + dv[0:BLOCK]).astype(BF16)
                ck[:, hs] = dk[BLOCK:2 * BLOCK]
                cv[:, hs] = dv[BLOCK:2 * BLOCK]
                return dqg, dkg, d_sinks

            small = _lockstep([head(j) for j in range(B_KV_HEADS)])
            sm_ref[0:1, 0:B_HEAD_DIM] += small[0][0] + small[1][0] + small[2][0] + small[3][0]
            sm_ref[1:2, 0:B_HEAD_DIM] += small[0][1] + small[1][1] + small[2][1] + small[3][1]
            sm_ref[2:3, :] += small[0][2] + small[1][2] + small[2][2] + small[3][2]

        @pl.when(n == nb)
        def _():
            dkv_ref[:, 0:B_KV_WIDTH] = ck[...].astype(BF16)
            dkv_ref[:, B_KV_WIDTH:2 * B_KV_WIDTH] = cv[...].astype(BF16)

    return _pcall(
        body, plan=plan, name="swa_bwd", grid=(nb + 1,),
        in_specs=_swa_in_specs(nb, nb - 1) + [pl.BlockSpec((BLOCK, B_WIDTH), lambda n: (jnp.minimum(n, nb - 1), 0))],
        out_specs=[pl.BlockSpec((BLOCK, B_WIDTH), lambda n: (jnp.minimum(n, nb - 1), 0)),
                   pl.BlockSpec((BLOCK, 2 * B_KV_WIDTH), lambda n: (jnp.maximum(n - 1, 0), 0)),
                   pl.BlockSpec((8, BLOCK), lambda n: (0, 0))],
        out_shape=[jax.ShapeDtypeStruct((T, B_WIDTH), BF16), jax.ShapeDtypeStruct((T, 2 * B_KV_WIDTH), BF16),
                   jax.ShapeDtypeStruct((8, BLOCK), F32)],
        scratch_shapes=[pltpu.VMEM((BLOCK, B_KV_WIDTH), F32), pltpu.VMEM((BLOCK, B_KV_WIDTH), F32)],
        compiler_params=_params(("arbitrary",)),
    )(proj, proj, proj, proj, proj, q_gain, k_gain, sinks, do)


W_IN, W_A, W_B, W_OUT, W_MI, W_MO = range(6)


def _local_step(x, target, mod8, norm1_gain, norm2_gain, lb_logits, o_gain, q_gain, k_gain, sinks, shards, c_arr, chip_arr):
    relu2 = lambda u: (u, jnp.square(jnp.maximum(u, 0.0)))
    pair, half = {}, {}

    def exchange(ws, grads):
        return _sibling_exchange_plan([_grad_view(g, w) for w, g in zip(ws, grads)])

    def pair_sums(ws, grads, others):
        for w, g, o in zip(ws, grads, others):
            pair[w] = _pair_sum(_grad_view(g, w), o, c_arr, f"pair_sum{w}")

    def sum_slots(ws, slots):
        for w, s in zip(ws, slots):
            half[w] = _sum_slots(pair[w], s, w, chip_arr, f"sum_slots{w}")

    part_in = _cast_into_full({W_IN: shards[W_IN]}, "cast_w_in")[W_IN]
    h, (part_in,) = _norm1_fwd(x, norm1_gain, mod8, plan=_gather_plan({W_IN: part_in}, part="near"))
    parts, (w_in,) = _cast_into_full({w: shards[w] for w in range(1, N_W)}, "cast_rest",
                                     plan=_gather_plan({W_IN: part_in}, pass_at=(0.8,), part="far"))
    proj, (w_mi,) = _mm(h, w_in, name="mm_proj", bn=512, plan=_gather_plan({W_MI: parts[W_MI]}, pass_at=(0.47, 0.72)))
    (o_a, states), (w_a, w_b) = _hgrn_fwd(
        proj, lb_logits, o_gain, plan=_gather_plan({w: parts[w] for w in (W_A, W_B)}, pass_at=(0.4, 0.65)))
    o_b, (w_out,) = _swa_fwd(proj, q_gain, k_gain, sinks, plan=_gather_plan({W_OUT: parts[W_OUT]}, pass_at=(0.3, 0.5)))
    ya = _mm(o_a, w_a, name="mm_branch_a")
    gate_cols = (OFF_GATE_A // MERGE_BC, OFF_GATE_B // MERGE_BC)
    yb, merged = _mm(o_b, w_b, name="mm_branch_b", bn=MERGE_BC, out_dtypes=(F32, BF16),
                     extras=(proj, proj, ya), extra_cols=gate_cols + (0,),
                     epi=lambda acc, ga, gb, ya_: (acc, _sig(ga) * ya_ + _sig(gb) * acc))
    mo = _mm(merged, w_out, name="mm_out")
    x1, h2 = _res_norm2_fwd(x, mo, norm2_gain, mod8)
    (u, act), (w_mo,) = _mm(h2, w_mi, name="mm_mlp_in", out_dtypes=(F32, BF16), epi=relu2,
                            plan=_gather_plan({W_MO: parts[W_MO]}, pass_at=(0.6, 0.9)))
    dy, dmlp, st_loss = _mm(act, w_mo, name="mm_mlp_out", bn=512, out_dtypes=(F32, BF16, F32), n_stats=1,
                            extras=(x1, target), row_extras=(mod8,), epi=_loss_head)
    st_loss = st_loss.reshape(-1, 8, D_MODEL).sum(axis=0)
    def half_blocks(w, own):
        def block(i):
            return 2 * i + (lax.axis_index("c") if own else 1 - lax.axis_index("c"))
        return (1 if W_SHAPES[w][2] else N_CHIPS), block

    def pair_of(w, lhs, rhs, other, name):
        hr, cols = _half_shape(w)
        p = _mm(lhs, rhs, name=name, ta=True, bn=512, a_blocks=half_blocks(w, True), out_dtypes=(BF16,),
                extras=(other,), epi=lambda acc, o: (acc + o,))
        return p.reshape(-1, hr, W_SHAPES[w][1])

    near, far = (0, 1), (2,)
    g_send = _mm(act, dmlp, name="mm_g_mlp_out_send", ta=True, bn=512, a_blocks=half_blocks(W_MO, False))
    du, (g_other,) = _mm(dmlp, w_mo, name="mm_d_act", tb=True, out_dtypes=(BF16,), extras=(u,),
                         epi=lambda acc, uu: (acc * (2.0 * jnp.maximum(uu, 0.0)),), plan=_sibling_share_plan([g_send]))
    pair[W_MO] = pair_of(W_MO, act, dmlp, g_other, "mm_g_mlp_out_own")
    g_send, (part,) = _mm(h2, du, name="mm_g_mlp_in_send", ta=True, bn=512, a_blocks=half_blocks(W_MI, False),
                          plan=_chip_exchange_plan({W_MO: pair[W_MO]}, near))
    dh2, res = _mm(du, w_mi, name="mm_d_h2", tb=True,
                   plan=_join(_chip_exchange_plan({W_MO: pair[W_MO]}, far, {W_MO: part}), _sibling_share_plan([g_send])))
    sum_slots([W_MO], res[:1])
    pair[W_MI] = pair_of(W_MI, h2, du, res[1], "mm_g_mlp_in_own")
    dx1, dmo, st_n2 = _norm2_bwd(dh2, x1, dy, mo, norm2_gain, mod8)
    def merge_bwd(dm, ga, gb, ya_, yb_):
        sa, sb = _sig(ga), _sig(gb)
        return dm * sa, dm * sb, dm * ya_ * sa * (1.0 - sa), dm * yb_ * sb * (1.0 - sb)

    (dya, dyb, dga, dgb), (part,) = _mm(dmo, w_out, name="mm_d_merged", tb=True, bn=MERGE_BC, out_dtypes=(BF16,) * 4,
                                        extras=(proj, proj, ya, yb), extra_cols=gate_cols + (0, 0), epi=merge_bwd,
                                        plan=_chip_exchange_plan({W_MI: pair[W_MI]}, near))
    g_out = _mm(merged, dmo, name="mm_g_out", ta=True, bn=512)
    do_a = _mm(dya, w_a, name="mm_d_oa", tb=True)
    g_a = _mm(o_a, dya, name="mm_g_branch_a", ta=True, bn=512)
    do_b = _mm(dyb, w_b, name="mm_d_ob", tb=True)
    g_b = _mm(o_b, dyb, name="mm_g_branch_b", ta=True, bn=512)
    mid = [W_A, W_B, W_OUT]
    (dqb, dkvb, st_swa), res = _swa_bwd(
        proj, q_gain, k_gain, sinks, do_b,
        plan=_join(_chip_exchange_plan({W_MI: pair[W_MI]}, far, {W_MI: part}), exchange(mid, [g_a, g_b, g_out])))
    sum_slots([W_MI], res[:1])
    pair_sums(mid, [g_a, g_b, g_out], res[1:])
    (dqa, dfa, dia, dgga, d_lb, d_og), parts_mid = _hgrn_bwd(
        proj, lb_logits, o_gain, states, do_a, plan=_chip_exchange_plan({w: pair[w] for w in mid}, near))
    dproj = jnp.concatenate([dqa, dfa, dia, dgga, dqb, dkvb, dga, dgb], axis=1)
    done = [W_A, W_B, W_OUT, W_MI, W_MO]
    g_send, slots_mid = _mm(h, dproj, name="mm_g_in_send", ta=True, bn=512, a_blocks=half_blocks(W_IN, False),
                            plan=_chip_exchange_plan({w: pair[w] for w in mid}, far, dict(zip(mid, parts_mid))))
    sum_slots(mid, slots_mid)
    g_own, res = _mm(h, dproj, name="mm_g_in_own", ta=True, bn=512, a_blocks=half_blocks(W_IN, True),
                     plan=_sibling_share_plan([g_send] + [half[w] for w in done]))
    g_other, theirs = res[0], dict(zip(done, res[1:]))
    pair[W_IN] = _add_bf16(g_own, g_other, "pair_sum0")[None]
    dh, slots_in = _mm(dproj, w_in, name="mm_d_h", tb=True, bk=2432, plan=_chip_exchange_plan({W_IN: pair[W_IN]}))
    sum_slots([W_IN], slots_in)
    grad_x, st_n1 = _norm1_bwd(dh, x, dx1, norm1_gain, mod8)
    (theirs[W_IN],) = _run_plan(_sibling_share_plan([half[W_IN]]), "sibling_share_w_in")
    stats = dict(loss=st_loss, n2=st_n2, n1=st_n1, d_lb=d_lb, d_og=d_og, swa=st_swa)
    return grad_x, [half[w] for w in range(N_W)], [theirs[w] for w in range(N_W)], stats


def _ew_rows(rows, cols):
    br = 8
    while br * 2 <= rows and br * 2 * cols * 4 <= (1 << 20) and rows % (br * 2) == 0:
        br *= 2
    return br


CAST_STEPS = 16


def _cast_into_full(shards, name, plan=None):
    ws = sorted(shards)
    in_specs, out_specs, out_shape = [], [], []
    for w in ws:
        sr, sc = shards[w].shape
        R, C, by_col = W_SHAPES[w]
        br = sr // CAST_STEPS
        assert br * CAST_STEPS == sr and br % 16 == 0, (w, sr)

        def out_map(i, by_col=by_col):
            chip = 2 * lax.axis_index("x") + lax.axis_index("y")
            return (i, chip) if by_col else (chip * CAST_STEPS + i, 0)

        in_specs.append(pl.BlockSpec((br, sc), lambda i: (i, 0)))
        out_specs.append(pl.BlockSpec((br, sc), out_map))
        out_shape.append(jax.ShapeDtypeStruct((R, C), BF16))

    def body(*refs):
        for w_ref, o_ref in zip(refs[:len(ws)], refs[len(ws):]):
            o_ref[...] = w_ref[...].astype(BF16)

    res = _pcall(body, plan=plan, name=name, grid=(CAST_STEPS,), in_specs=in_specs, out_specs=out_specs,
                 out_shape=out_shape, compiler_params=_params(("arbitrary",)))(*[shards[w] for w in ws])
    if plan is None:
        return dict(zip(ws, res))
    return dict(zip(ws, res[0])), res[1]


def _adamw_math(w, g, m, v):
    m = ADAM_B1 * m + (1.0 - ADAM_B1) * g
    v = ADAM_B2 * v + (1.0 - ADAM_B2) * (g * g)
    m_hat = m / (1.0 - ADAM_B1 ** ADAM_STEP)
    v_hat = v / (1.0 - ADAM_B2 ** ADAM_STEP)
    delta = -ADAM_LR * (m_hat / (jnp.sqrt(v_hat) + ADAM_EPS) + ADAM_WD * w)
    return delta, m, v


def _adamw(w, g, m, v, name):
    R, C = w.shape
    br = _ew_rows(R, C)
    spec = pl.BlockSpec((br, C), lambda i: (i, 0))

    def body(w_ref, g_ref, m_ref, v_ref, d_ref, nm_ref, nv_ref):
        d_ref[...], nm_ref[...], nv_ref[...] = _adamw_math(w_ref[...], g_ref[...], m_ref[...], v_ref[...])

    sh = jax.ShapeDtypeStruct((R, C), F32)
    return _pcall(body, name=name, grid=(R // br,), in_specs=[spec] * 4, out_specs=[spec] * 3, out_shape=[sh] * 3,
                  compiler_params=_params(("parallel",)))(w, g, m, v)


def _add_bf16(a, b, name):
    R, C = a.shape
    br = _ew_rows(R, C)
    spec = pl.BlockSpec((br, C), lambda i: (i, 0))

    def body(a_ref, b_ref, o_ref):
        o_ref[...] = (a_ref[...] + b_ref[...]).astype(BF16)

    return _pcall(body, name=name, grid=(R // br,), in_specs=[spec, spec], out_specs=spec,
                  out_shape=jax.ShapeDtypeStruct((R, C), BF16), compiler_params=_params(("parallel",)))(a, b)


def _adamw_halves(w, own, other, m, v, c_arr, name):
    R, C = w.shape
    hr = R // 2
    br = _ew_rows(hr, C)
    nb = hr // br
    full = pl.BlockSpec((br, C), lambda h, i, c_ref: (h * nb + i, 0))
    half = pl.BlockSpec((br, C), lambda h, i, c_ref: (i, 0))

    def body(c_ref, w_ref, own_ref, oth_ref, m_ref, v_ref, g_ref, d_ref, nm_ref, nv_ref):
        g = jnp.where(pl.program_id(0) == c_ref[0], own_ref[...], oth_ref[...])
        g_ref[...] = g
        d_ref[...], nm_ref[...], nv_ref[...] = _adamw_math(w_ref[...], g, m_ref[...], v_ref[...])

    sh = jax.ShapeDtypeStruct((R, C), F32)
    return _pcall(
        body, name=name,
        grid_spec=pltpu.PrefetchScalarGridSpec(
            num_scalar_prefetch=1, grid=(2, nb), in_specs=[full, half, half, full, full], out_specs=[full] * 4),
        out_shape=[sh] * 4, compiler_params=_params(("parallel", "parallel")))(c_arr, w, own, other, m, v)


def _ada_grad_adamw(c_t, dmod, w, m, v):
    R, C = w.shape
    br = _ew_rows(R, C)
    spec = pl.BlockSpec((br, C), lambda i: (i, 0))

    def body(c_ref, dm_ref, w_ref, m_ref, v_ref, g_ref, d_ref, nm_ref, nv_ref):
        cv = c_ref[...]
        sc = cv * _sig(cv)
        g = sc[:, 0:1] * dm_ref[0:1, :]
        for b in range(1, N_DEV):
            g = g + sc[:, b:b + 1] * dm_ref[b:b + 1, :]
        g_ref[...] = g
        d_ref[...], nm_ref[...], nv_ref[...] = _adamw_math(w_ref[...], g, m_ref[...], v_ref[...])

    sh = jax.ShapeDtypeStruct((R, C), F32)
    return _pcall(
        body, name="ada_grad_adamw", grid=(R // br,),
        in_specs=[pl.BlockSpec((br, N_DEV), lambda i: (i, 0)), pl.BlockSpec((N_DEV, C), lambda i: (0, 0)), spec, spec, spec],
        out_specs=[spec] * 4, out_shape=[sh] * 4, compiler_params=_params(("parallel",)))(c_t, dmod, w, m, v)


SMALL_ROWS = 16


def _small_sum(small_all, lb_logits):
    def body(s_ref, lbl_ref, o_ref):
        acc = s_ref[0:SMALL_ROWS, :]
        for d in range(1, N_DEV):
            acc = acc + s_ref[d * SMALL_ROWS:(d + 1) * SMALL_ROWS, :]
        o_ref[...] = acc
        z = lbl_ref[...]
        e = jnp.exp(z - jnp.max(z, axis=0, keepdims=True))
        p0 = e[0:1, :] / (e[0:1, :] + e[1:2, :])
        dz = acc[8:9, 0:A_WIDTH] * p0 * (1.0 - p0)
        o_ref[8:9, 0:A_WIDTH] = dz
        o_ref[10:11, 0:A_WIDTH] = -dz

    return _pcall(body, name="small_sum", out_shape=jax.ShapeDtypeStruct((SMALL_ROWS, D_MODEL), F32),
                  in_specs=[pl.BlockSpec(memory_space=pltpu.VMEM)] * 2, out_specs=pl.BlockSpec(memory_space=pltpu.VMEM),
                  compiler_params=_params())(small_all, lb_logits)


RELATIONS = ((1, 0), (0, 1), (1, 1))
ANY = pl.BlockSpec(memory_space=pl.ANY)


def _place():
    x, y, c = lax.axis_index("x"), lax.axis_index("y"), lax.axis_index("c")
    return x, y, c


def _allgather_small(x_shard, name):
    m_per, n = x_shard.shape

    def body(x_ref, out_ref, send_sems, recv_sems, local_sem):
        x, y, c = _place()
        me, sibling = (x, y, c), (x, y, 1 - c)
        chips = [(1 - x, y), (x, 1 - y), (1 - x, 1 - y)]

        def rows(px, py, pc):
            return out_ref.at[pl.ds((4 * px + 2 * py + pc) * m_per, m_per), :]

        def copy(k, block, to, src=None):
            return pltpu.make_async_remote_copy(
                src_ref=rows(*block) if src is None else src, dst_ref=rows(*block),
                send_sem=send_sems.at[k], recv_sem=recv_sems.at[k], device_id=to, device_id_type=MESH)

        mine = pltpu.make_async_copy(x_ref, rows(*me), local_sem)
        mine.start()
        first = [copy(0, me, sibling, src=x_ref)]
        first += [copy(1 + j, me, (*chip, c), src=x_ref) for j, chip in enumerate(chips)]
        for cp in first:
            cp.start()
        passed = [copy(4 + j, (*chip, c), sibling) for j, chip in enumerate(chips)]
        for j, chip in enumerate(chips):
            copy(1 + j, (*chip, c), me).wait_recv()
            passed[j].start()
        copy(0, sibling, me).wait_recv()
        for j, chip in enumerate(chips):
            copy(4 + j, (*chip, 1 - c), me).wait_recv()
        for cp in first + passed:
            cp.wait_send()
        mine.wait()

    return _pcall(
        body, name=name, out_shape=jax.ShapeDtypeStruct((N_DEV * m_per, n), x_shard.dtype),
        in_specs=[pl.BlockSpec(memory_space=pltpu.VMEM)], out_specs=pl.BlockSpec(memory_space=pltpu.VMEM),
        scratch_shapes=[pltpu.SemaphoreType.DMA((7,)), pltpu.SemaphoreType.DMA((7,)), pltpu.SemaphoreType.DMA],
        compiler_params=_params(),
    )(x_shard)


W_SHAPES = ((D_MODEL, IN_WIDTH, True), (A_WIDTH, D_MODEL, True), (B_WIDTH, D_MODEL, True),
            (D_MODEL, D_MODEL, False), (D_MODEL, MLP_HIDDEN, True), (MLP_HIDDEN, D_MODEL, False))
N_W = len(W_SHAPES)


def _shard_shape(w):
    R, C, by_col = W_SHAPES[w]
    return (R, C // N_CHIPS) if by_col else (R // N_CHIPS, C)


def _half_shape(w):
    sr, sc = _shard_shape(w)
    return sr // 2, sc


def _region(full_ref, w, chip, half, quarter=None):
    sr, sc = _shard_shape(w)
    by_col = W_SHAPES[w][2]
    r0, c0 = (0, chip * sc) if by_col else (chip * sr, 0)
    r0, rows = r0 + half * (sr // 2), sr // 2
    if quarter is not None:
        r0, rows = r0 + quarter * (rows // 2), rows // 2
    return full_ref.at[pl.ds(r0, rows), pl.ds(c0, sc)]


def _on_device(fn):
    x, y, c = _place()
    me = 4 * x + 2 * y + c
    for d in range(N_DEV):
        @pl.when(me == d)
        def _(d=d):
            fn(x, y, c, d)


GATHER_COPIES = (
    (0, 0, None, "x"), (0, 0, None, "y"),
    (1, 2, 0, "y"), (1, 1, 1, "x"),
    (1, 2, None, "s"), (1, 1, None, "s"),
    (2, 3, 0, "s"), (2, 3, 1, "s"),
)
PEER_FLIP = {"x": 2, "y": 1, "s": 0}


GATHER_STAGES = {
    None: (((), (0, 1), ()), ((0, 1), (2, 3, 4, 5), ()), ((2, 3), (6, 7), ()), ((4, 5, 6, 7), (), tuple(range(8)))),
    "near": (((), (0, 1), ()), ((0, 1), (), (0, 1))),
    "far": (((), (2, 3, 4, 5), ()), ((2, 3), (6, 7), ()), ((4, 5, 6, 7), (), (2, 3, 4, 5, 6, 7))),
}


def _gather_plan(partials, pass_at=(0.5, 0.75), part=None):
    ws = sorted(partials)
    n_t = len(GATHER_COPIES)
    jobs = [(i, w) for i, w in enumerate(ws)]

    def copy(pi, po, ps, x, y, c, d, i, w, t, landing):
        chip, dc = d >> 1, d & 1
        stage, flip, quarter, to = GATHER_COPIES[t]
        if landing:
            peer_chip = chip ^ PEER_FLIP[to]
            part = _region(po[i], w, peer_chip ^ flip, (1 - dc) if to == "s" else dc, quarter)
            src = part
        else:
            part = _region(po[i], w, chip ^ flip, dc, quarter)
            here = flip != 0 and (part_of is None or stage == 2)
            src = part if here else _region(pi[i], w, chip ^ flip, dc, quarter)
        target = {"x": (x ^ 1, y, c), "y": (x, y ^ 1, c), "s": (x, y, 1 - c)}[to]
        return pltpu.make_async_remote_copy(
            src_ref=src, dst_ref=part, send_sem=ps[0].at[i * n_t + t], recv_sem=ps[1].at[i * n_t + t],
            device_id=target, device_id_type=MESH)

    part_of = part

    def stage(landed, started, sent):
        def run(pi, po, ps):
            def on(x, y, c, d):
                for i, w in jobs:
                    for t in landed:
                        copy(pi, po, ps, x, y, c, d, i, w, t, True).wait_recv()
                for i, w in jobs:
                    for t in started:
                        copy(pi, po, ps, x, y, c, d, i, w, t, False).start()
                for i, w in jobs:
                    for t in sent:
                        copy(pi, po, ps, x, y, c, d, i, w, t, False).wait_send()
            _on_device(on)
        return run

    stages = [stage(*st) for st in GATHER_STAGES[part]]
    mid_at = tuple(pass_at) if part is None else tuple(pass_at)[:len(stages) - 2]
    return _Plan([partials[w] for w in ws], [jax.ShapeDtypeStruct(W_SHAPES[w][:2], BF16) for w in ws],
                 [pltpu.SemaphoreType.DMA((n_t * len(ws),)) for _ in range(2)], stages,
                 {i: i for i in range(len(ws))}, mid_at=mid_at)


def _grad_view(g, w):
    R, C, by_col = W_SHAPES[w]
    return g.reshape(1, 2, R // 2, C) if by_col else g.reshape(N_CHIPS, 2, R // N_CHIPS // 2, C)


def _start_wait_plan(ins, outs, n_copies, copies):
    def start(pi, po, ps):
        for cp in copies(pi, po, ps):
            cp.start()

    def finish(pi, po, ps):
        for cp in copies(pi, po, ps):
            cp.wait()

    return _Plan(ins, outs, [pltpu.SemaphoreType.DMA((n_copies,)), pltpu.SemaphoreType.DMA((n_copies,))], [start, finish])


def _sibling_exchange_plan(g4s):
    pieces = [(i, p) for i, g in enumerate(g4s) for p in range(g.shape[0])]

    def copies(pi, po, ps):
        x, y, c = _place()
        return [pltpu.make_async_remote_copy(
            src_ref=pi[i].at[p, 1 - c], dst_ref=po[i].at[p], send_sem=ps[0].at[n], recv_sem=ps[1].at[n],
            device_id=(x, y, 1 - c), device_id_type=MESH) for n, (i, p) in enumerate(pieces)]

    return _start_wait_plan(list(g4s), [jax.ShapeDtypeStruct((g.shape[0],) + g.shape[2:], F32) for g in g4s],
                            len(pieces), copies)


def _pair_sum(g4, other, c_arr, name):
    P, _, hr, C = g4.shape
    br = _ew_rows(hr, C)

    def body(c_ref, g_ref, o_ref, p_ref):
        p_ref[...] = (g_ref[...] + o_ref[...]).astype(BF16)

    return _pcall(
        body, name=name,
        grid_spec=pltpu.PrefetchScalarGridSpec(
            num_scalar_prefetch=1, grid=(P, hr // br),
            in_specs=[pl.BlockSpec((None, None, br, C), lambda p, i, c_ref: (p, c_ref[0], i, 0)),
                      pl.BlockSpec((None, br, C), lambda p, i, c_ref: (p, i, 0))],
            out_specs=pl.BlockSpec((None, br, C), lambda p, i, c_ref: (p, i, 0))),
        out_shape=jax.ShapeDtypeStruct((P, hr, C), BF16),
        compiler_params=_params(("parallel", "parallel")),
    )(c_arr, g4, other)


def _pair_part(p_ref, w, chip):
    sr, sc = _shard_shape(w)
    return p_ref.at[0, :, pl.ds(chip * sc, sc)] if W_SHAPES[w][2] else p_ref.at[chip]


def _chip_exchange_plan(pairs, rels=(0, 1, 2), into=None):
    ws = sorted(pairs)
    n = len(ws)

    def stage(wait):
        def run(pi, po, ps):
            def on(x, y, c, d):
                for i, w in enumerate(ws):
                    for k, (rx, ry) in enumerate(RELATIONS):
                        if k not in rels:
                            continue
                        cp = pltpu.make_async_remote_copy(
                            src_ref=_pair_part(pi[i], w, (d >> 1) ^ (2 * rx + ry)), dst_ref=po[i].at[k],
                            send_sem=ps[0].at[i * 3 + k], recv_sem=ps[1].at[i * 3 + k],
                            device_id=(x ^ rx, y ^ ry, c), device_id_type=MESH)
                        if wait:
                            cp.wait()
                        else:
                            cp.start()
            _on_device(on)
        return run

    ins = [pairs[w] for w in ws] + ([into[w] for w in ws] if into else [])
    return _Plan(ins, [jax.ShapeDtypeStruct((3,) + _half_shape(w), BF16) for w in ws],
                 [pltpu.SemaphoreType.DMA((3 * n,)), pltpu.SemaphoreType.DMA((3 * n,))],
                 [stage(False), stage(True)], {n + i: i for i in range(n)} if into else None)


def _sum_slots(pair, slots, w, chip_arr, name):
    _, hr, C = slots.shape
    br = _ew_rows(hr, C)
    own_map = (lambda i, chip: (0, i, chip[0])) if W_SHAPES[w][2] else (lambda i, chip: (chip[0], i, 0))

    def body(chip_ref, p_ref, s_ref, o_ref):
        acc = p_ref[...].astype(F32)
        for k in range(3):
            acc = acc + s_ref[k].astype(F32)
        o_ref[...] = acc

    return _pcall(
        body, name=name,
        grid_spec=pltpu.PrefetchScalarGridSpec(
            num_scalar_prefetch=1, grid=(hr // br,),
            in_specs=[pl.BlockSpec((None, br, C), own_map), pl.BlockSpec((3, br, C), lambda i, chip: (0, i, 0))],
            out_specs=pl.BlockSpec((br, C), lambda i, chip: (i, 0))),
        out_shape=jax.ShapeDtypeStruct((hr, C), F32), compiler_params=_params(("parallel",)),
    )(chip_arr, pair, slots)


def _sibling_share_plan(halves):
    def copies(pi, po, ps):
        x, y, c = _place()
        return [pltpu.make_async_remote_copy(
            src_ref=pi[i], dst_ref=po[i], send_sem=ps[0].at[i], recv_sem=ps[1].at[i],
            device_id=(x, y, 1 - c), device_id_type=MESH) for i in range(len(halves))]

    return _start_wait_plan(list(halves), [jax.ShapeDtypeStruct(h.shape, F32) for h in halves], len(halves), copies)


def _pad_lanes(v, width=D_MODEL):
    return jnp.pad(v, ((0, 0), (0, width - v.shape[1])))


def _pack_small(b_ada, norm1, norm2, lb, o_gain, q_gain, k_gain, sinks):
    rows = [b_ada.reshape(N_MOD, D_MODEL), norm1, norm2, jnp.concatenate([lb[0:1], o_gain], axis=1),
            _pad_lanes(jnp.concatenate([q_gain, k_gain, sinks], axis=1)), _pad_lanes(lb[1:2]),
            jnp.zeros((SMALL_ROWS - 11, D_MODEL), F32)]
    return jnp.concatenate(rows, axis=0)


def _unpack_small(p):
    return (p[0:6].reshape(1, N_MOD * D_MODEL), p[6:7], p[7:8],
            jnp.concatenate([p[8:9, 0:A_WIDTH], p[10:11, 0:A_WIDTH]], axis=0), p[8:9, A_WIDTH:],
            p[9:10, 0:64], p[9:10, 64:128], p[9:10, 128:144])


def kernel(x, c, w_ada, b_ada, norm1_gain, w_in, lb_logits, hgrn_o_gain, q_norm_gain, k_norm_gain, sinks, w_branch_a, w_branch_b, w_out, norm2_gain, w_mlp_in, w_mlp_out, loss_target, m_w_ada, m_b_ada, m_norm1_gain, m_w_in, m_lb_logits, m_hgrn_o_gain, m_q_norm_gain, m_k_norm_gain, m_sinks, m_w_branch_a, m_w_branch_b, m_w_out, m_norm2_gain, m_w_mlp_in, m_w_mlp_out, v_w_ada, v_b_ada, v_norm1_gain, v_w_in, v_lb_logits, v_hgrn_o_gain, v_q_norm_gain, v_k_norm_gain, v_sinks, v_w_branch_a, v_w_branch_b, v_w_out, v_norm2_gain, v_w_mlp_in, v_w_mlp_out):
    xi, yi, ci = _place()
    chip = 2 * xi + yi
    me = 4 * xi + 2 * yi + ci
    ada_cols = w_ada.shape[2]

    c_all = _allgather_small(jnp.broadcast_to(c, (8, D_MODEL)), "gather_c").reshape(N_DEV, 8, D_MODEL)[:, 0]
    b_cols = lax.dynamic_slice(b_ada, (0, chip * ada_cols), (1, ada_cols))
    mod_part = _ada_fwd(c_all, w_ada[0], b_cols)
    mod_all = _allgather_small(mod_part, "gather_mod").reshape(N_CHIPS, 2, N_DEV, ada_cols)[:, 0]
    mod_mine = lax.dynamic_index_in_dim(mod_all, me, axis=1, keepdims=False).reshape(N_MOD, D_MODEL)
    mod8 = jnp.concatenate([mod_mine, jnp.zeros((2, D_MODEL), F32)], axis=0)

    shards = (w_in[0], w_branch_a[0], w_branch_b[0], w_out[0], w_mlp_in[0], w_mlp_out[0])
    chip_arr = chip.astype(jnp.int32).reshape(1)
    c_arr = ci.astype(jnp.int32).reshape(1)

    grad_x, halves, theirs, st = _local_step(x[0], loss_target[0], mod8, norm1_gain, norm2_gain, lb_logits, hgrn_o_gain,
                                             q_norm_gain, k_norm_gain, sinks, shards, c_arr, chip_arr)
    loss = lax.psum(0.5 * jnp.sum(st["loss"][0]) / D_MODEL, ("x", "y", "c"))
    moments = ((m_w_in, v_w_in), (m_w_branch_a, v_w_branch_a), (m_w_branch_b, v_w_branch_b), (m_w_out, v_w_out),
               (m_w_mlp_in, v_w_mlp_in), (m_w_mlp_out, v_w_mlp_out))
    big = [_adamw_halves(shards[w], halves[w], theirs[w], moments[w][0][0], moments[w][1][0], c_arr, f"adamw{w}")
           for w in range(N_W)]

    swa = st["swa"]
    small = jnp.concatenate([
        st["n1"][1:2], st["n1"][0:1], st["n2"][3:4], st["n2"][1:2], st["n2"][0:1], st["loss"][1:2],
        st["n1"][2:3], st["n2"][2:3], jnp.concatenate([st["d_lb"][0:1], st["d_og"][0:1]], axis=1),
        _pad_lanes(jnp.concatenate([swa[0:1, 0:64], swa[1:2, 0:64], swa[2:3, 0:16]], axis=1)),
        jnp.zeros((SMALL_ROWS - 10, D_MODEL), F32)], axis=0)
    small_all = _allgather_small(small, "gather_small")
    g_small = _small_sum(small_all, lb_logits)
    small_w = (b_ada, norm1_gain, norm2_gain, lb_logits, hgrn_o_gain, q_norm_gain, k_norm_gain, sinks)
    small_m = (m_b_ada, m_norm1_gain, m_norm2_gain, m_lb_logits, m_hgrn_o_gain, m_q_norm_gain, m_k_norm_gain, m_sinks)
    small_v = (v_b_ada, v_norm1_gain, v_norm2_gain, v_lb_logits, v_hgrn_o_gain, v_q_norm_gain, v_k_norm_gain, v_sinks)
    sm = [_unpack_small(t) for t in
          (g_small,) + tuple(_adamw(_pack_small(*small_w), g_small, _pack_small(*small_m), _pack_small(*small_v),
                                    "adamw_small"))]
    g_b, g_n1, g_n2, g_lb, g_og, g_qg, g_kg, g_sk = ([t[i] for t in sm] for i in range(8))

    dmod_all = small_all.reshape(N_DEV, SMALL_ROWS, D_MODEL)[:, 0:N_MOD].reshape(N_DEV, N_MOD * D_MODEL)
    dmod_cols = lax.dynamic_slice(dmod_all, (0, chip * ada_cols), (N_DEV, ada_cols))
    ada = _ada_grad_adamw(c_all.T, dmod_cols, w_ada[0], m_w_ada[0], v_w_ada[0])

    def ordered(k):
        lead = lambda a: a[None]
        return (lead(ada[k]), g_b[k], g_n1[k], lead(big[0][k]), g_lb[k], g_og[k], g_qg[k], g_kg[k], g_sk[k],
                lead(big[1][k]), lead(big[2][k]), lead(big[3][k]), g_n2[k], lead(big[4][k]), lead(big[5][k]))

    return (loss, grad_x[None]) + ordered(0) + ordered(1) + ordered(2) + ordered(3)
```

```python
import functools

import jax
import jax.numpy as jnp
from jax import lax
from jax.experimental import pallas as pl
from jax.experimental.pallas import tpu as pltpu

F32 = jnp.float32
BF16 = jnp.bfloat16
HIGHEST = lax.Precision.HIGHEST
MESH = pl.DeviceIdType.MESH

D_MODEL = 2048
A_WIDTH = 1024
A_HEADS = 8
A_HEAD_DIM = 128
A_CHUNK = 64
B_WIDTH = 1024
B_HEAD_DIM = 64
B_GROUP = 4
B_KV_HEADS = 4
B_KV_WIDTH = 256
BLOCK = 128
MLP_HIDDEN = 8192
IN_WIDTH = 9728
N_MOD = 6
EPS = 1e-6
N_CHIPS = 4
N_DEV = 8

OFF_QA, OFF_FA, OFF_IA, OFF_GA = 0, 1024, 2048, 3072
OFF_QB, OFF_KB, OFF_VB = 4096, 5120, 5376
OFF_GATE_A, OFF_GATE_B = 5632, 7680

ADAM_LR = 0.001
ADAM_B1 = 0.9
ADAM_B2 = 0.999
ADAM_EPS = 1e-08
ADAM_WD = 0.01
ADAM_STEP = 10

VMEM_LIMIT_V7X = 48 * 1024 * 1024
NEG_BIG = -1e30


def _params(sem=None, vmem=VMEM_LIMIT_V7X):
    return pltpu.CompilerParams(dimension_semantics=sem, vmem_limit_bytes=vmem)


class _Plan:
    def __init__(self, ins, outs, sems, stages, aliases=None, mid_at=()):
        self.ins, self.outs, self.sems, self.stages, self.aliases = ins, outs, sems, stages, aliases or {}
        self.mid_at = tuple(mid_at)
        assert len(self.mid_at) == len(stages) - 2


def _join(a, b):
    assert len(a.stages) == 2 and len(b.stages) == 2
    ni, no, ns = len(a.ins), len(a.outs), len(a.sems)

    def stage(k):
        def run(pi, po, ps):
            a.stages[k](pi[:ni], po[:no], ps[:ns])
            b.stages[k](pi[ni:], po[no:], ps[ns:])
        return run

    aliases = dict(a.aliases)
    aliases.update({ni + i: no + o for i, o in b.aliases.items()})
    return _Plan(a.ins + b.ins, a.outs + b.outs, a.sems + b.sems, [stage(0), stage(1)], aliases)


def _pcall(body, plan=None, **kw):
    if plan is None:
        return pl.pallas_call(body, **kw)
    grid = kw["grid"]
    single = not isinstance(kw["out_specs"], (list, tuple))
    in_specs = list(kw["in_specs"])
    out_specs = [kw["out_specs"]] if single else list(kw["out_specs"])
    out_shape = [kw["out_shape"]] if single else list(kw["out_shape"])
    scratch = list(kw.get("scratch_shapes", ()))
    n_in, n_out, n_scr = len(in_specs), len(out_specs), len(scratch)
    n_pi, n_po = len(plan.ins), len(plan.outs)
    total = 1
    for g in grid:
        total *= g
    n_st = len(plan.stages)

    def wrapped(*refs):
        o0 = n_in + n_pi
        s0 = o0 + n_out + n_po
        pi, po, ps = refs[n_in:o0], refs[o0 + n_out:s0], refs[s0 + n_scr:]
        lin = 0
        for d, g in enumerate(grid):
            lin = lin * g + pl.program_id(d)
        for si, frac in enumerate((0.0,) + plan.mid_at):
            @pl.when(lin == int(frac * (total - 1)))
            def _(si=si):
                plan.stages[si](pi, po, ps)
        body(*refs[:n_in], *refs[o0:o0 + n_out], *refs[s0:s0 + n_scr])

        @pl.when(lin == total - 1)
        def _():
            plan.stages[-1](pi, po, ps)

    any_spec = pl.BlockSpec(memory_space=pl.ANY)
    call = pl.pallas_call(
        wrapped, name=kw["name"], grid=grid, in_specs=in_specs + [any_spec] * n_pi,
        out_specs=out_specs + [any_spec] * n_po, out_shape=out_shape + list(plan.outs),
        scratch_shapes=scratch + list(plan.sems),
        input_output_aliases={n_in + i: n_out + o for i, o in plan.aliases.items()},
        compiler_params=_params(("arbitrary",) * len(grid)))

    def run(*args):
        res = call(*args, *plan.ins)
        outs = list(res[:n_out])
        return (outs[0] if single else outs), list(res[n_out:])

    return run


def _run_plan(plan, name):
    return _pcall(lambda: None, plan=plan, name=name, grid=(1,), in_specs=[], out_specs=[], out_shape=[])()[1]


def _sig(x):
    return 1.0 / (1.0 + jnp.exp(-x))


def _nn(a, b):
    return lax.dot_general(a.astype(BF16), b.astype(BF16), (((1,), (0,)), ((), ())), preferred_element_type=F32)


def _nt(a, b):
    return lax.dot_general(a.astype(BF16), b.astype(BF16), (((1,), (1,)), ((), ())), preferred_element_type=F32)


def _tn(a, b):
    return lax.dot_general(a.astype(BF16), b.astype(BF16), (((0,), (0,)), ((), ())), preferred_element_type=F32)


def _mm(a, b, *, name, ta=False, tb=False, bm=1024, bn=1024, bk=2048, out_dtypes=(F32,), epi=None, extras=(),
        extra_cols=None, plan=None, a_blocks=None, row_extras=(), n_stats=0):
    if ta:
        K, M = a.shape
        bk = K
        if a_blocks is not None:
            M = a_blocks[0] * bm
    else:
        M, K = a.shape
    if tb:
        N, K2 = b.shape
    else:
        K2, N = b.shape
    bm, bn, bk = min(bm, M), min(bn, N), min(bk, K)
    assert K == K2 and M % bm == 0 and N % bn == 0 and K % bk == 0, (name, a.shape, b.shape)
    nk = K // bk
    a_col = a_blocks[1] if a_blocks is not None else (lambda i: i)
    a_spec = pl.BlockSpec((bk, bm), lambda i, j, k: (k, a_col(i))) if ta else pl.BlockSpec((bm, bk), lambda i, j, k: (i, k))
    b_spec = pl.BlockSpec((bn, bk), lambda i, j, k: (j, k)) if tb else pl.BlockSpec((bk, bn), lambda i, j, k: (k, j))
    t_spec = pl.BlockSpec((bm, bn), lambda i, j, k: (i, j))
    extra_cols = extra_cols or (0,) * len(extras)
    e_specs = [pl.BlockSpec((bm, bn), lambda i, j, k, off=off: (i, off + j)) for off in extra_cols]
    e_specs += [pl.BlockSpec((8, bn), lambda i, j, k: (0, j)) for _ in row_extras]
    dims = (((1,), (1 if tb else 0,)), ((), ()))
    n_e, n_o = len(extras) + len(row_extras), len(out_dtypes)
    stat_spec = pl.BlockSpec((8, bn), lambda i, j, k: (i, j))

    def body(*refs):
        a_ref, b_ref = refs[0], refs[1]
        e_refs = refs[2:2 + n_e]
        o_refs = refs[2 + n_e:2 + n_e + n_o]

        def finish(acc):
            outs = (acc,) if epi is None else epi(acc, *[e[...] for e in e_refs])
            for o_ref, o in zip(o_refs, outs):
                o_ref[...] = o.astype(o_ref.dtype)

        if ta:
            at_ref = refs[-1]

            @pl.when(pl.program_id(1) == 0)
            def _():
                at_ref[...] = a_ref[...].T

            lhs = at_ref[...]
        else:
            lhs = a_ref[...].astype(BF16)
        part = lax.dot_general(lhs, b_ref[...].astype(BF16), dims, preferred_element_type=F32)
        if nk == 1:
            finish(part)
        else:
            acc_ref = refs[-1]
            k = pl.program_id(2)

            @pl.when(k == 0)
            def _():
                acc_ref[...] = part

            @pl.when(k > 0)
            def _():
                acc_ref[...] += part

            @pl.when(k == nk - 1)
            def _():
                finish(acc_ref[...])

    if ta:
        assert a.dtype == BF16 and nk == 1
        scratch = [pltpu.VMEM((bm, bk), BF16)]
    else:
        scratch = [pltpu.VMEM((bm, bn), F32)] if nk > 1 else []
    out = _pcall(
        body, plan=plan, name=name, grid=(M // bm, N // bn, nk),
        in_specs=[a_spec, b_spec] + e_specs,
        out_specs=[t_spec] * (n_o - n_stats) + [stat_spec] * n_stats,
        out_shape=[jax.ShapeDtypeStruct((M, N), dt) for dt in out_dtypes[:n_o - n_stats]]
        + [jax.ShapeDtypeStruct((8 * (M // bm), N), F32)] * n_stats,
        scratch_shapes=scratch,
        compiler_params=_params(("parallel", "arbitrary", "arbitrary")),
    )(a, b, *extras, *row_extras)
    if plan is not None:
        return (out[0][0] if n_o == 1 else out[0]), out[1]
    return out[0] if n_o == 1 else out


def _ada_fwd(c_all, w_ada, b_cols):
    n = w_ada.shape[1]
    bn = 512

    def body(c_ref, w_ref, b_ref, o_ref):
        cv = c_ref[...]
        sc = cv * _sig(cv)
        o_ref[...] = jnp.dot(sc, w_ref[...], precision=HIGHEST, preferred_element_type=F32) + b_ref[...]

    return _pcall(
        body, name="ada_fwd", grid=(n // bn,),
        in_specs=[pl.BlockSpec((N_DEV, D_MODEL), lambda j: (0, 0)), pl.BlockSpec((D_MODEL, bn), lambda j: (0, j)),
                  pl.BlockSpec((1, bn), lambda j: (0, j))],
        out_specs=pl.BlockSpec((N_DEV, bn), lambda j: (0, j)),
        out_shape=jax.ShapeDtypeStruct((N_DEV, n), F32),
        compiler_params=_params(("parallel",)),
    )(c_all, w_ada, b_cols)


ROWS_EW = 256


def _rms_fwd_math(x, gain, scale, shift):
    rstd = lax.rsqrt(jnp.mean(x * x, axis=-1, keepdims=True) + EPS)
    xhat = x * rstd
    n = xhat * gain
    return n * (1.0 + scale) + shift, xhat, n, rstd


def _rms_bwd_math(dh, xhat, n, rstd, gain, scale):
    dn = dh * (1.0 + scale)
    dxhat = dn * gain
    dx = rstd * (dxhat - xhat * jnp.mean(dxhat * xhat, axis=-1, keepdims=True))
    d_scale = jnp.sum(dh * n, axis=0, keepdims=True)
    d_shift = jnp.sum(dh, axis=0, keepdims=True)
    d_gain = jnp.sum(dn * xhat, axis=0, keepdims=True)
    return dx, d_scale, d_shift, d_gain


def _row_spec(w=D_MODEL, br=ROWS_EW):
    return pl.BlockSpec((br, w), lambda i: (i, 0))


def _vec_spec(r=8, w=D_MODEL):
    return pl.BlockSpec((r, w), lambda i: (0, 0))


def _norm1_fwd(x, gain, mod8, plan=None):
    T = x.shape[0]

    def body(x_ref, g_ref, m_ref, h_ref):
        h, _, _, _ = _rms_fwd_math(x_ref[...], g_ref[...], m_ref[1:2, :], m_ref[0:1, :])
        h_ref[...] = h.astype(BF16)

    return _pcall(
        body, plan=plan, name="norm1_fwd", grid=(T // ROWS_EW,),
        in_specs=[_row_spec(), _vec_spec(1), _vec_spec()],
        out_specs=_row_spec(), out_shape=jax.ShapeDtypeStruct((T, D_MODEL), BF16),
        compiler_params=_params(("parallel",)),
    )(x, gain, mod8)


def _res_norm2_fwd(x, mo, gain, mod8):
    T = x.shape[0]

    def body(x_ref, mo_ref, g_ref, m_ref, x1_ref, h_ref):
        x1 = x_ref[...] + m_ref[2:3, :] * mo_ref[...]
        x1_ref[...] = x1
        h, _, _, _ = _rms_fwd_math(x1, g_ref[...], m_ref[4:5, :], m_ref[3:4, :])
        h_ref[...] = h.astype(BF16)

    return _pcall(
        body, name="res_norm2_fwd", grid=(T // ROWS_EW,),
        in_specs=[_row_spec(), _row_spec(), _vec_spec(1), _vec_spec()],
        out_specs=[_row_spec(), _row_spec()],
        out_shape=[jax.ShapeDtypeStruct((T, D_MODEL), F32), jax.ShapeDtypeStruct((T, D_MODEL), BF16)],
        compiler_params=_params(("parallel",)),
    )(x, mo, gain, mod8)


def _loss_head(mlp, x1, target, mod):
    gate = mod[5:6, :]
    err = x1 + gate * mlp - target
    dy = err * (1.0 / D_MODEL)
    row = lax.broadcasted_iota(jnp.int32, (8, mlp.shape[1]), 0)
    stats = jnp.where(row == 0, jnp.sum(err * err, axis=0, keepdims=True),
                      jnp.where(row == 1, jnp.sum(dy * mlp, axis=0, keepdims=True), 0.0))
    return dy, dy * gate, stats


def _norm2_bwd(dh2, x1, dy, mo, gain, mod8):
    T = x1.shape[0]

    def body(dh_ref, x1_ref, dy_ref, mo_ref, g_ref, m_ref, dx1_ref, dmo_ref, st_ref):
        i = pl.program_id(0)
        gain_v, scale = g_ref[...], m_ref[4:5, :]
        _, xhat, n, rstd = _rms_fwd_math(x1_ref[...], gain_v, scale, m_ref[3:4, :])
        dx, d_scale, d_shift, d_gain = _rms_bwd_math(dh_ref[...], xhat, n, rstd, gain_v, scale)
        dx1 = dy_ref[...] + dx
        dx1_ref[...] = dx1
        dmo_ref[...] = (dx1 * m_ref[2:3, :]).astype(BF16)

        @pl.when(i == 0)
        def _():
            st_ref[...] = jnp.zeros_like(st_ref)

        st_ref[0:1, :] += d_scale
        st_ref[1:2, :] += d_shift
        st_ref[2:3, :] += d_gain
        st_ref[3:4, :] += jnp.sum(dx1 * mo_ref[...], axis=0, keepdims=True)

    return _pcall(
        body, name="norm2_bwd", grid=(T // ROWS_EW,),
        in_specs=[_row_spec(), _row_spec(), _row_spec(), _row_spec(), _vec_spec(1), _vec_spec()],
        out_specs=[_row_spec(), _row_spec(), _vec_spec()],
        out_shape=[jax.ShapeDtypeStruct((T, D_MODEL), F32), jax.ShapeDtypeStruct((T, D_MODEL), BF16),
                   jax.ShapeDtypeStruct((8, D_MODEL), F32)],
        compiler_params=_params(("arbitrary",)),
    )(dh2, x1, dy, mo, gain, mod8)


def _norm1_bwd(dh, x, dx1, gain, mod8):
    T = x.shape[0]

    def body(dh_ref, x_ref, dx1_ref, g_ref, m_ref, dx_ref, st_ref):
        i = pl.program_id(0)
        gain_v, scale = g_ref[...], m_ref[1:2, :]
        _, xhat, n, rstd = _rms_fwd_math(x_ref[...], gain_v, scale, m_ref[0:1, :])
        dx, d_scale, d_shift, d_gain = _rms_bwd_math(dh_ref[...], xhat, n, rstd, gain_v, scale)
        dx_ref[...] = dx1_ref[...] + dx

        @pl.when(i == 0)
        def _():
            st_ref[...] = jnp.zeros_like(st_ref)

        st_ref[0:1, :] += d_scale
        st_ref[1:2, :] += d_shift
        st_ref[2:3, :] += d_gain

    return _pcall(
        body, name="norm1_bwd", grid=(T // ROWS_EW,),
        in_specs=[_row_spec(), _row_spec(), _row_spec(), _vec_spec(1), _vec_spec()],
        out_specs=[_row_spec(), _vec_spec()],
        out_shape=[jax.ShapeDtypeStruct((T, D_MODEL), F32), jax.ShapeDtypeStruct((8, D_MODEL), F32)],
        compiler_params=_params(("arbitrary",)),
    )(dh, x, dx1, gain, mod8)


MERGE_BC = 512


def _hgrn_rows(T):
    return 512 if T >= 1024 else 128


def _lower_bound(lbl):
    e = jnp.exp(lbl - jnp.max(lbl, axis=0, keepdims=True))
    return e[0:1, :] / (e[0:1, :] + e[1:2, :])


def _chunk_sum_matrix(rows, backward):
    shift = A_CHUNK.bit_length() - 1
    r = lax.broadcasted_iota(jnp.int32, (rows, rows), 0)
    c = lax.broadcasted_iota(jnp.int32, (rows, rows), 1)
    same = jnp.right_shift(r, shift) == jnp.right_shift(c, shift)
    return (same & ((r <= c) if backward else (r >= c))).astype(BF16)


def _chunk_sums(m, x):
    n = x.shape[1]
    hi = x.astype(BF16)
    rest = x - hi.astype(F32)
    mid = rest.astype(BF16)
    lo = (rest - mid.astype(F32)).astype(BF16)
    y = jnp.dot(m, jnp.concatenate([hi, mid, lo], axis=1), preferred_element_type=F32)
    return y[:, 0:n] + y[:, n:2 * n] + y[:, 2 * n:3 * n]


def _hgrn_block_pre(q, fl, lb, m_fwd):
    sg = _sig(fl)
    f = lb + (1.0 - lb) * sg
    sq = _sig(q)
    return dict(sg=sg, f=f, k=1.0 - f, sq=sq, qf=q * sq, b=_chunk_sums(m_fwd, jnp.log(f)))


def _hgrn_chunk_local(pre, r):
    C = A_CHUNK
    qf, k, b = pre["qf"][r], pre["k"][r], pre["b"][r]
    causal = lax.broadcasted_iota(jnp.int32, (C, C), 0) >= lax.broadcasted_iota(jnp.int32, (C, C), 1)
    bm = b[C // 2 - 1:C // 2, :]
    bl = b[C - 1:C, :]
    e_q, e_k = jnp.exp(b - bm), jnp.exp(bm - b)
    e_b, e_l = jnp.exp(b), jnp.exp(bl - b)
    qd, kd = qf * e_q, k * e_k
    qe, ke = qf * e_b, k * e_l
    att = jnp.where(causal, _nt(qd, kd), 0.0)
    return dict(causal=causal, e_q=e_q, e_k=e_k, e_b=e_b, e_l=e_l, qd=qd, kd=kd, qe=qe, ke=ke, att=att, dec=jnp.exp(bl))


def _hgrn_chunk_fwd(pre, r, v, st):
    c = _hgrn_chunk_local(pre, r)
    c["o"] = _nn(c["att"], v) + _nt(c["qe"], st)
    return c


def _lockstep(gens):
    out = [None] * len(gens)
    live = list(enumerate(gens))
    while live:
        still = []
        for i, g in live:
            try:
                next(g)
                still.append((i, g))
            except StopIteration as done:
                out[i] = done.value
        live = still
    return out


HGRN_HEADS_PER_STEP = 4


def _hgrn_fwd(proj, lb_logits, o_gain, plan=None):
    T = proj.shape[0]
    BR = _hgrn_rows(T)
    cps = BR // A_CHUNK
    K, NH = A_HEAD_DIM, HGRN_HEADS_PER_STEP
    W = NH * K

    def col(off):
        return pl.BlockSpec((BR, W), lambda h, cb: (cb, off // W + h))

    def body(q_ref, f_ref, i_ref, g_ref, lbl_ref, og_ref, o_ref, s_ref, st):
        @pl.when(pl.program_id(1) == 0)
        def _():
            st[...] = jnp.zeros_like(st)

        lb_all = _lower_bound(lbl_ref[...])
        m_fwd = _chunk_sum_matrix(BR, False)
        pre = [_hgrn_block_pre(q_ref[:, n * K:(n + 1) * K], f_ref[:, n * K:(n + 1) * K], lb_all[:, n * K:(n + 1) * K], m_fwd)
               for n in range(NH)]
        def local(n, ci):
            r, hs = slice(ci * A_CHUNK, (ci + 1) * A_CHUNK), slice(n * K, (n + 1) * K)
            v = i_ref[r, hs]
            c = _hgrn_chunk_local(pre[n], r)
            yield
            return dict(o=_nn(c["att"], v), ds=_tn(v, c["ke"]), qe=c["qe"], dec=c["dec"])

        def chain(n, loc):
            hs = slice(n * K, (n + 1) * K)
            state = st[n]
            for ci, p in enumerate(loc):
                r = slice(ci * A_CHUNK, (ci + 1) * A_CHUNK)
                s_ref[n, ci] = state
                o = p["o"] + _nt(p["qe"], state)
                state = state * p["dec"] + p["ds"]
                yield
                on = o * lax.rsqrt(jnp.mean(o * o, axis=-1, keepdims=True) + EPS)
                g = g_ref[r, hs]
                o_ref[r, hs] = (on * og_ref[:, hs] * (g * _sig(g))).astype(BF16)
            st[n] = state

        loc = _lockstep([local(n, ci) for n in range(NH) for ci in range(cps)])
        _lockstep([chain(n, loc[n * cps:(n + 1) * cps]) for n in range(NH)])

    return _pcall(
        body, plan=plan, name="hgrn_fwd", grid=(A_HEADS // NH, T // BR),
        in_specs=[col(OFF_QA), col(OFF_FA), col(OFF_IA), col(OFF_GA),
                  pl.BlockSpec((2, W), lambda h, cb: (0, h)), pl.BlockSpec((1, W), lambda h, cb: (0, h))],
        out_specs=[pl.BlockSpec((BR, W), lambda h, cb: (cb, h)),
                   pl.BlockSpec((NH, cps, K, K), lambda h, cb: (h, cb, 0, 0))],
        out_shape=[jax.ShapeDtypeStruct((T, A_WIDTH), BF16),
                   jax.ShapeDtypeStruct((A_HEADS, T // A_CHUNK, K, K), F32)],
        scratch_shapes=[pltpu.VMEM((NH, K, K), F32)],
        compiler_params=_params(("parallel", "arbitrary")),
    )(proj, proj, proj, proj, lb_logits, o_gain)


def _hgrn_bwd(proj, lb_logits, o_gain, states, do, plan=None):
    T = proj.shape[0]
    BR = _hgrn_rows(T)
    cps = BR // A_CHUNK
    ncb = T // BR
    K, C, NH = A_HEAD_DIM, A_CHUNK, HGRN_HEADS_PER_STEP
    W = NH * K

    def col(off):
        return pl.BlockSpec((BR, W), lambda h, cb: (ncb - 1 - cb, off // W + h))

    def body(q_ref, f_ref, i_ref, g_ref, lbl_ref, og_ref, s_ref, do_ref,
             dq_ref, df_ref, di_ref, dg_ref, dlb_ref, dog_ref, dst):
        @pl.when(pl.program_id(1) == 0)
        def _():
            dst[...] = jnp.zeros_like(dst)
            dlb_ref[...] = jnp.zeros_like(dlb_ref)
            dog_ref[...] = jnp.zeros_like(dog_ref)

        lb_all = _lower_bound(lbl_ref[...])
        row = lax.broadcasted_iota(jnp.int32, (C, K), 0)
        m_fwd, m_bwd = _chunk_sum_matrix(BR, False), _chunk_sum_matrix(BR, True)
        pre = [_hgrn_block_pre(q_ref[:, n * K:(n + 1) * K], f_ref[:, n * K:(n + 1) * K], lb_all[:, n * K:(n + 1) * K], m_fwd)
               for n in range(NH)]
        def local(n, ci):
            r, hs = slice(ci * C, (ci + 1) * C), slice(n * K, (n + 1) * K)
            gain = og_ref[:, hs]
            st = s_ref[n, ci]
            v = i_ref[r, hs]
            q = q_ref[r, hs]
            c = _hgrn_chunk_fwd(pre[n], r, v, st)
            yield
            o = c["o"]
            rn = lax.rsqrt(jnp.mean(o * o, axis=-1, keepdims=True) + EPS)
            on = o * rn
            g = g_ref[r, hs]
            sgg = _sig(g)
            dy = do_ref[r, hs]
            d_ong = dy * (g * sgg)
            dg_ref[r, hs] = (dy * (on * gain) * (sgg * (1.0 + g * (1.0 - sgg)))).astype(BF16)
            d_on = d_ong * gain
            d_o = rn * (d_on - on * jnp.mean(d_on * on, axis=-1, keepdims=True))
            datt = jnp.where(c["causal"], _nt(d_o, v), 0.0)
            dqe = _nn(d_o, st)
            yield
            dqd = _nn(datt, c["kd"])
            dkd = _tn(datt, c["qd"])
            dv = _tn(c["att"], d_o)
            ds = _tn(d_o, c["qe"])
            yield
            t_q, t_k = dqd * c["qd"], dkd * c["kd"]
            sq = pre[n]["sq"][r]
            dq_ref[r, hs] = ((dqd * c["e_q"] + dqe * c["e_b"]) * (sq * (1.0 + q * (1.0 - sq)))).astype(BF16)
            return dict(v=v, st=st, ke=c["ke"], e_l=c["e_l"], dec=c["dec"], dv=dv, ds=ds, dk=dkd * c["e_k"],
                        db=t_q - t_k + dqe * c["qe"], dbm=jnp.sum(t_k - t_q, axis=0, keepdims=True),
                        d_og=jnp.sum(d_ong * on, axis=0, keepdims=True))

        def chain(n, loc):
            hs = slice(n * K, (n + 1) * K)
            dst_next = dst[n]
            db_of, dk_of = [None] * cps, [None] * cps
            for ci in reversed(range(cps)):
                p = loc[ci]
                di_ref[ci * C:(ci + 1) * C, hs] = (p["dv"] + _nt(p["ke"], dst_next)).astype(BF16)
                dke = _nn(p["v"], dst_next)
                yield
                t_l = dke * p["ke"]
                dbl = jnp.sum(t_l, axis=0, keepdims=True) + jnp.sum(dst_next * p["st"], axis=0, keepdims=True) * p["dec"]
                db_of[ci] = p["db"] - t_l + jnp.where(row == C // 2 - 1, p["dbm"], 0.0) + jnp.where(row == C - 1, dbl, 0.0)
                dk_of[ci] = p["dk"] + dke * p["e_l"]
                dst_next = dst_next * p["dec"] + p["ds"]
            dst[n] = dst_next
            return db_of, dk_of

        loc = _lockstep([local(n, ci) for n in range(NH) for ci in range(cps)])
        loc = [loc[n * cps:(n + 1) * cps] for n in range(NH)]
        chains = _lockstep([chain(n, loc[n]) for n in range(NH)])
        for n in range(NH):
            hs = slice(n * K, (n + 1) * K)
            db_of, dk_of = chains[n]
            d_og = loc[n][0]["d_og"]
            for p in loc[n][1:]:
                d_og = d_og + p["d_og"]
            dog_ref[0:1, hs] += d_og
            lb, sg = lb_all[:, hs], pre[n]["sg"]
            dlf = _chunk_sums(m_bwd, jnp.concatenate(db_of, axis=0))
            df = dlf / pre[n]["f"] - jnp.concatenate(dk_of, axis=0)
            df_ref[:, hs] = (df * (1.0 - lb) * sg * (1.0 - sg)).astype(BF16)
            dlb_ref[0:1, hs] += jnp.sum(df * (1.0 - sg), axis=0, keepdims=True)

    ocol = pl.BlockSpec((BR, W), lambda h, cb: (ncb - 1 - cb, h))
    vec = pl.BlockSpec((8, W), lambda h, cb: (0, h))
    return _pcall(
        body, plan=plan, name="hgrn_bwd", grid=(A_HEADS // NH, ncb),
        in_specs=[col(OFF_QA), col(OFF_FA), col(OFF_IA), col(OFF_GA),
                  pl.BlockSpec((2, W), lambda h, cb: (0, h)), pl.BlockSpec((1, W), lambda h, cb: (0, h)),
                  pl.BlockSpec((NH, cps, K, K), lambda h, cb: (h, ncb - 1 - cb, 0, 0)),
                  pl.BlockSpec((BR, W), lambda h, cb: (ncb - 1 - cb, h))],
        out_specs=[ocol, ocol, ocol, ocol, vec, vec],
        out_shape=[jax.ShapeDtypeStruct((T, A_WIDTH), BF16)] * 4 + [jax.ShapeDtypeStruct((8, A_WIDTH), F32)] * 2,
        scratch_shapes=[pltpu.VMEM((NH, K, K), F32)],
        compiler_params=_params(("parallel", "arbitrary")),
    )(proj, proj, proj, proj, lb_logits, o_gain, states, do)


def _head_norm(x):
    r = lax.rsqrt(jnp.mean(x * x, axis=-1, keepdims=True) + EPS)
    return x * r, r


def _head_norm_bwd(dy, xn, r, gain):
    dxn = dy * gain
    return r * (dxn - xn * jnp.mean(dxn * xn, axis=-1, keepdims=True)), jnp.sum(dy * xn, axis=0, keepdims=True)


def _swa_mask(has_prev):
    rows = B_GROUP * BLOCK
    r = lax.broadcasted_iota(jnp.int32, (rows, 2 * BLOCK), 0) % BLOCK
    c = lax.broadcasted_iota(jnp.int32, (rows, 2 * BLOCK), 1)
    rel = r + BLOCK - c
    return (rel >= 0) & (rel < BLOCK) & ((c >= BLOCK) | has_prev)


def _swa_head_fwd(j, q_ref, kp_ref, kc_ref, vp_ref, vc_ref, qg, kg, sk_ref, mask):
    hs = slice(j * B_HEAD_DIM, (j + 1) * B_HEAD_DIM)
    kcat = jnp.concatenate([kp_ref[:, hs], kc_ref[:, hs]], axis=0)
    vcat = jnp.concatenate([vp_ref[:, hs], vc_ref[:, hs]], axis=0)
    qs = jnp.concatenate([q_ref[:, pl.ds((j * B_GROUP + g) * B_HEAD_DIM, B_HEAD_DIM)] for g in range(B_GROUP)], axis=0)
    kn, kr = _head_norm(kcat)
    qn, qr = _head_norm(qs)
    kh, qh = kn * kg, qn * qg
    yield
    s = jnp.where(mask, _nt(qh, kh) * (B_HEAD_DIM ** -0.5), NEG_BIG)
    yield
    sink = jnp.concatenate(
        [jnp.broadcast_to(sk_ref[0:1, pl.ds(j * B_GROUP + g, 1)], (BLOCK, 1)) for g in range(B_GROUP)], axis=0)
    m = jnp.maximum(jnp.max(s, axis=-1, keepdims=True), sink)
    p = jnp.exp(s - m)
    e_sink = jnp.exp(sink - m)
    inv = 1.0 / (jnp.sum(p, axis=-1, keepdims=True) + e_sink)
    prob = p * inv
    return dict(vcat=vcat, kn=kn, kr=kr, qn=qn, qr=qr, kh=kh, qh=qh, prob=prob, p_sink=e_sink * inv)


def _swa_in_specs(nb, last):
    def qi(n):
        return jnp.minimum(n, last)

    q = pl.BlockSpec((BLOCK, B_WIDTH), lambda n: (qi(n), OFF_QB // B_WIDTH))
    kc = pl.BlockSpec((BLOCK, B_KV_WIDTH), lambda n: (qi(n), OFF_KB // B_KV_WIDTH))
    kp = pl.BlockSpec((BLOCK, B_KV_WIDTH), lambda n: (jnp.maximum(qi(n) - 1, 0), OFF_KB // B_KV_WIDTH))
    vc = pl.BlockSpec((BLOCK, B_KV_WIDTH), lambda n: (qi(n), OFF_VB // B_KV_WIDTH))
    vp = pl.BlockSpec((BLOCK, B_KV_WIDTH), lambda n: (jnp.maximum(qi(n) - 1, 0), OFF_VB // B_KV_WIDTH))
    small = [pl.BlockSpec((1, B_HEAD_DIM), lambda n: (0, 0)), pl.BlockSpec((1, B_HEAD_DIM), lambda n: (0, 0)),
             pl.BlockSpec((1, B_GROUP * B_KV_HEADS), lambda n: (0, 0))]
    return [q, kp, kc, vp, vc] + small


def _swa_fwd(proj, q_gain, k_gain, sinks, plan=None):
    T = proj.shape[0]
    nb = T // BLOCK

    def body(q_ref, kp_ref, kc_ref, vp_ref, vc_ref, qg_ref, kg_ref, sk_ref, o_ref):
        mask = _swa_mask(pl.program_id(0) > 0)

        def head(j):
            c = yield from _swa_head_fwd(j, q_ref, kp_ref, kc_ref, vp_ref, vc_ref, qg_ref[...], kg_ref[...], sk_ref, mask)
            yield
            o = _nn(c["prob"], c["vcat"])
            yield
            for g in range(B_GROUP):
                o_ref[:, pl.ds((j * B_GROUP + g) * B_HEAD_DIM, B_HEAD_DIM)] = o[g * BLOCK:(g + 1) * BLOCK].astype(BF16)

        _lockstep([head(j) for j in range(B_KV_HEADS)])

    return _pcall(
        body, plan=plan, name="swa_fwd", grid=(nb,),
        in_specs=_swa_in_specs(nb, nb - 1),
        out_specs=pl.BlockSpec((BLOCK, B_WIDTH), lambda n: (n, 0)),
        out_shape=jax.ShapeDtypeStruct((T, B_WIDTH), BF16),
        compiler_params=_params(("parallel",)),
    )(proj, proj, proj, proj, proj, q_gain, k_gain, sinks)


def _swa_bwd(proj, q_gain, k_gain, sinks, do, plan=None):
    T = proj.shape[0]
    nb = T // BLOCK
    scale = B_HEAD_DIM ** -0.5

    def body(q_ref, kp_ref, kc_ref, vp_ref, vc_ref, qg_ref, kg_ref, sk_ref, do_ref,
             dq_ref, dkv_ref, sm_ref, ck, cv):
        n = pl.program_id(0)

        @pl.when(n == 0)
        def _():
            ck[...] = jnp.zeros_like(ck)
            cv[...] = jnp.zeros_like(cv)
            sm_ref[...] = jnp.zeros_like(sm_ref)

        @pl.when(n < nb)
        def _():
            mask = _swa_mask(n > 0)
            qg, kg = qg_ref[...], kg_ref[...]
            lane = lax.broadcasted_iota(jnp.int32, (1, BLOCK), 1)
            def head(j):
                hs = slice(j * B_HEAD_DIM, (j + 1) * B_HEAD_DIM)
                vs = slice(B_KV_WIDTH + j * B_HEAD_DIM, B_KV_WIDTH + (j + 1) * B_HEAD_DIM)
                c = yield from _swa_head_fwd(j, q_ref, kp_ref, kc_ref, vp_ref, vc_ref, qg, kg, sk_ref, mask)
                d_out = jnp.concatenate(
                    [do_ref[:, pl.ds((j * B_GROUP + g) * B_HEAD_DIM, B_HEAD_DIM)] for g in range(B_GROUP)], axis=0)
                prob = c["prob"]
                yield
                out = _nn(prob, c["vcat"])
                d_prob = _nt(d_out, c["vcat"])
                dv = _tn(prob, d_out)
                yield
                delta = jnp.sum(d_out * out, axis=-1, keepdims=True)
                ds = prob * (d_prob - delta)
                d_sink = -c["p_sink"] * delta
                yield
                dqh = _nn(ds, c["kh"]) * scale
                dkh = _tn(ds, c["qh"]) * scale
                yield
                dq, dqg = _head_norm_bwd(dqh, c["qn"], c["qr"], qg)
                dk, dkg = _head_norm_bwd(dkh, c["kn"], c["kr"], kg)
                d_sinks = jnp.zeros((1, BLOCK), F32)
                for g in range(B_GROUP):
                    dq_ref[:, pl.ds((j * B_GROUP + g) * B_HEAD_DIM, B_HEAD_DIM)] = dq[g * BLOCK:(g + 1) * BLOCK].astype(BF16)
                    tot = jnp.sum(d_sink[g * BLOCK:(g + 1) * BLOCK], axis=0, keepdims=True)
                    d_sinks = d_sinks + jnp.where(lane == j * B_GROUP + g, tot, 0.0)
                dkv_ref[:, hs] = (ck[:, hs] + dk[0:BLOCK]).astype(BF16)
                dkv_ref[:, vs] = (cv[:, hs] + dv[0:BLOCK]).astype(BF16)
                ck[:, hs] = dk[BLOCK:2 * BLOCK]
                cv[:, hs] = dv[BLOCK:2 * BLOCK]
                return dqg, dkg, d_sinks

            small = _lockstep([head(j) for j in range(B_KV_HEADS)])
            sm_ref[0:1, 0:B_HEAD_DIM] += small[0][0] + small[1][0] + small[2][0] + small[3][0]
            sm_ref[1:2, 0:B_HEAD_DIM] += small[0][1] + small[1][1] + small[2][1] + small[3][1]
            sm_ref[2:3, :] += small[0][2] + small[1][2] + small[2][2] + small[3][2]

        @pl.when(n == nb)
        def _():
            dkv_ref[:, 0:B_KV_WIDTH] = ck[...].astype(BF16)
            dkv_ref[:, B_KV_WIDTH:2 * B_KV_WIDTH] = cv[...].astype(BF16)

    return _pcall(
        body, plan=plan, name="swa_bwd", grid=(nb + 1,),
        in_specs=_swa_in_specs(nb, nb - 1) + [pl.BlockSpec((BLOCK, B_WIDTH), lambda n: (jnp.minimum(n, nb - 1), 0))],
        out_specs=[pl.BlockSpec((BLOCK, B_WIDTH), lambda n: (jnp.minimum(n, nb - 1), 0)),
                   pl.BlockSpec((BLOCK, 2 * B_KV_WIDTH), lambda n: (jnp.maximum(n - 1, 0), 0)),
                   pl.BlockSpec((8, BLOCK), lambda n: (0, 0))],
        out_shape=[jax.ShapeDtypeStruct((T, B_WIDTH), BF16), jax.ShapeDtypeStruct((T, 2 * B_KV_WIDTH), BF16),
                   jax.ShapeDtypeStruct((8, BLOCK), F32)],
        scratch_shapes=[pltpu.VMEM((BLOCK, B_KV_WIDTH), F32), pltpu.VMEM((BLOCK, B_KV_WIDTH), F32)],
        compiler_params=_params(("arbitrary",)),
    )(proj, proj, proj, proj, proj, q_gain, k_gain, sinks, do)


W_IN, W_A, W_B, W_OUT, W_MI, W_MO = range(6)


def _local_step(x, target, mod8, norm1_gain, norm2_gain, lb_logits, o_gain, q_gain, k_gain, sinks, shards, c_arr, chip_arr):
    relu2 = lambda u: (u, jnp.square(jnp.maximum(u, 0.0)))
    pair, half = {}, {}

    def exchange(ws, grads):
        return _sibling_exchange_plan([_grad_view(g, w) for w, g in zip(ws, grads)])

    def pair_sums(ws, grads, others):
        for w, g, o in zip(ws, grads, others):
            pair[w] = _pair_sum(_grad_view(g, w), o, c_arr, f"pair_sum{w}")

    def sum_slots(ws, slots):
        for w, s in zip(ws, slots):
            half[w] = _sum_slots(pair[w], s, w, chip_arr, f"sum_slots{w}")

    part_in = _cast_into_full({W_IN: shards[W_IN]}, "cast_w_in")[W_IN]
    h, (part_in,) = _norm1_fwd(x, norm1_gain, mod8, plan=_gather_plan({W_IN: part_in}, part="near"))
    parts, (w_in,) = _cast_into_full({w: shards[w] for w in range(1, N_W)}, "cast_rest",
                                     plan=_gather_plan({W_IN: part_in}, pass_at=(0.8,), part="far"))
    proj, (w_mi,) = _mm(h, w_in, name="mm_proj", bn=512, plan=_gather_plan({W_MI: parts[W_MI]}, pass_at=(0.47, 0.72)))
    (o_a, states), (w_a, w_b) = _hgrn_fwd(
        proj, lb_logits, o_gain, plan=_gather_plan({w: parts[w] for w in (W_A, W_B)}, pass_at=(0.4, 0.65)))
    o_b, (w_out,) = _swa_fwd(proj, q_gain, k_gain, sinks, plan=_gather_plan({W_OUT: parts[W_OUT]}, pass_at=(0.3, 0.5)))
    ya = _mm(o_a, w_a, name="mm_branch_a")
    gate_cols = (OFF_GATE_A // MERGE_BC, OFF_GATE_B // MERGE_BC)
    yb, merged = _mm(o_b, w_b, name="mm_branch_b", bn=MERGE_BC, out_dtypes=(F32, BF16),
                     extras=(proj, proj, ya), extra_cols=gate_cols + (0,),
                     epi=lambda acc, ga, gb, ya_: (acc, _sig(ga) * ya_ + _sig(gb) * acc))
    mo = _mm(merged, w_out, name="mm_out")
    x1, h2 = _res_norm2_fwd(x, mo, norm2_gain, mod8)
    (u, act), (w_mo,) = _mm(h2, w_mi, name="mm_mlp_in", out_dtypes=(F32, BF16), epi=relu2,
                            plan=_gather_plan({W_MO: parts[W_MO]}, pass_at=(0.6, 0.9)))
    dy, dmlp, st_loss = _mm(act, w_mo, name="mm_mlp_out", bm=512, out_dtypes=(F32, BF16, F32), n_stats=1,
                            extras=(x1, target), row_extras=(mod8,), epi=_loss_head)
    st_loss = st_loss.reshape(-1, 8, D_MODEL).sum(axis=0)
    def half_blocks(w, own):
        def block(i):
            return 2 * i + (lax.axis_index("c") if own else 1 - lax.axis_index("c"))
        return (1 if W_SHAPES[w][2] else N_CHIPS), block

    def pair_of(w, lhs, rhs, other, name):
        hr, cols = _half_shape(w)
        p = _mm(lhs, rhs, name=name, ta=True, bn=512, a_blocks=half_blocks(w, True), out_dtypes=(BF16,),
                extras=(other,), epi=lambda acc, o: (acc + o,))
        return p.reshape(-1, hr, W_SHAPES[w][1])

    near, far = (0, 1), (2,)
    g_send = _mm(act, dmlp, name="mm_g_mlp_out_send", ta=True, bn=512, a_blocks=half_blocks(W_MO, False))
    du, (g_other,) = _mm(dmlp, w_mo, name="mm_d_act", tb=True, out_dtypes=(BF16,), extras=(u,),
                         epi=lambda acc, uu: (acc * (2.0 * jnp.maximum(uu, 0.0)),), plan=_sibling_share_plan([g_send]))
    pair[W_MO] = pair_of(W_MO, act, dmlp, g_other, "mm_g_mlp_out_own")
    g_send, (part,) = _mm(h2, du, name="mm_g_mlp_in_send", ta=True, bn=512, a_blocks=half_blocks(W_MI, False),
                          plan=_chip_exchange_plan({W_MO: pair[W_MO]}, near))
    dh2, res = _mm(du, w_mi, name="mm_d_h2", tb=True,
                   plan=_join(_chip_exchange_plan({W_MO: pair[W_MO]}, far, {W_MO: part}), _sibling_share_plan([g_send])))
    sum_slots([W_MO], res[:1])
    pair[W_MI] = pair_of(W_MI, h2, du, res[1], "mm_g_mlp_in_own")
    dx1, dmo, st_n2 = _norm2_bwd(dh2, x1, dy, mo, norm2_gain, mod8)
    def merge_bwd(dm, ga, gb, ya_, yb_):
        sa, sb = _sig(ga), _sig(gb)
        return dm * sa, dm * sb, dm * ya_ * sa * (1.0 - sa), dm * yb_ * sb * (1.0 - sb)

    dya, dyb, dga, dgb = _mm(dmo, w_out, name="mm_d_merged", tb=True, bn=MERGE_BC, out_dtypes=(BF16,) * 4,
                             extras=(proj, proj, ya, yb), extra_cols=gate_cols + (0, 0), epi=merge_bwd)
    g_out = _mm(merged, dmo, name="mm_g_out", ta=True, bn=512)
    do_a = _mm(dya, w_a, name="mm_d_oa", tb=True)
    g_a = _mm(o_a, dya, name="mm_g_branch_a", ta=True, bn=512)
    do_b = _mm(dyb, w_b, name="mm_d_ob", tb=True)
    g_b = _mm(o_b, dyb, name="mm_g_branch_b", ta=True, bn=512)
    mid = [W_A, W_B, W_OUT]
    (dqb, dkvb, st_swa), res = _swa_bwd(
        proj, q_gain, k_gain, sinks, do_b,
        plan=_join(_chip_exchange_plan({W_MI: pair[W_MI]}), exchange(mid, [g_a, g_b, g_out])))
    sum_slots([W_MI], res[:1])
    pair_sums(mid, [g_a, g_b, g_out], res[1:])
    (dqa, dfa, dia, dgga, d_lb, d_og), slots_mid = _hgrn_bwd(
        proj, lb_logits, o_gain, states, do_a, plan=_chip_exchange_plan({w: pair[w] for w in mid}))
    sum_slots(mid, slots_mid)
    dproj = jnp.concatenate([dqa, dfa, dia, dgga, dqb, dkvb, dga, dgb], axis=1)
    done = [W_A, W_B, W_OUT, W_MI, W_MO]
    g_send = _mm(h, dproj, name="mm_g_in_send", ta=True, bn=512, a_blocks=half_blocks(W_IN, False))
    g_own, res = _mm(h, dproj, name="mm_g_in_own", ta=True, bn=512, a_blocks=half_blocks(W_IN, True),
                     plan=_sibling_share_plan([g_send] + [half[w] for w in done]))
    g_other, theirs = res[0], dict(zip(done, res[1:]))
    pair[W_IN] = _add_bf16(g_own, g_other, "pair_sum0")[None]
    dh, slots_in = _mm(dproj, w_in, name="mm_d_h", tb=True, bk=2432, plan=_chip_exchange_plan({W_IN: pair[W_IN]}))
    sum_slots([W_IN], slots_in)
    grad_x, st_n1 = _norm1_bwd(dh, x, dx1, norm1_gain, mod8)
    (theirs[W_IN],) = _run_plan(_sibling_share_plan([half[W_IN]]), "sibling_share_w_in")
    stats = dict(loss=st_loss, n2=st_n2, n1=st_n1, d_lb=d_lb, d_og=d_og, swa=st_swa)
    return grad_x, [half[w] for w in range(N_W)], [theirs[w] for w in range(N_W)], stats


def _ew_rows(rows, cols):
    br = 8
    while br * 2 <= rows and br * 2 * cols * 4 <= (1 << 20) and rows % (br * 2) == 0:
        br *= 2
    return br


CAST_STEPS = 16


def _cast_into_full(shards, name, plan=None):
    ws = sorted(shards)
    in_specs, out_specs, out_shape = [], [], []
    for w in ws:
        sr, sc = shards[w].shape
        R, C, by_col = W_SHAPES[w]
        br = sr // CAST_STEPS
        assert br * CAST_STEPS == sr and br % 16 == 0, (w, sr)

        def out_map(i, by_col=by_col):
            chip = 2 * lax.axis_index("x") + lax.axis_index("y")
            return (i, chip) if by_col else (chip * CAST_STEPS + i, 0)

        in_specs.append(pl.BlockSpec((br, sc), lambda i: (i, 0)))
        out_specs.append(pl.BlockSpec((br, sc), out_map))
        out_shape.append(jax.ShapeDtypeStruct((R, C), BF16))

    def body(*refs):
        for w_ref, o_ref in zip(refs[:len(ws)], refs[len(ws):]):
            o_ref[...] = w_ref[...].astype(BF16)

    res = _pcall(body, plan=plan, name=name, grid=(CAST_STEPS,), in_specs=in_specs, out_specs=out_specs,
                 out_shape=out_shape, compiler_params=_params(("arbitrary",)))(*[shards[w] for w in ws])
    if plan is None:
        return dict(zip(ws, res))
    return dict(zip(ws, res[0])), res[1]


def _adamw_math(w, g, m, v):
    m = ADAM_B1 * m + (1.0 - ADAM_B1) * g
    v = ADAM_B2 * v + (1.0 - ADAM_B2) * (g * g)
    m_hat = m / (1.0 - ADAM_B1 ** ADAM_STEP)
    v_hat = v / (1.0 - ADAM_B2 ** ADAM_STEP)
    delta = -ADAM_LR * (m_hat / (jnp.sqrt(v_hat) + ADAM_EPS) + ADAM_WD * w)
    return delta, m, v


def _adamw(w, g, m, v, name):
    R, C = w.shape
    br = _ew_rows(R, C)
    spec = pl.BlockSpec((br, C), lambda i: (i, 0))

    def body(w_ref, g_ref, m_ref, v_ref, d_ref, nm_ref, nv_ref):
        d_ref[...], nm_ref[...], nv_ref[...] = _adamw_math(w_ref[...], g_ref[...], m_ref[...], v_ref[...])

    sh = jax.ShapeDtypeStruct((R, C), F32)
    return _pcall(body, name=name, grid=(R // br,), in_specs=[spec] * 4, out_specs=[spec] * 3, out_shape=[sh] * 3,
                  compiler_params=_params(("parallel",)))(w, g, m, v)


def _add_bf16(a, b, name):
    R, C = a.shape
    br = _ew_rows(R, C)
    spec = pl.BlockSpec((br, C), lambda i: (i, 0))

    def body(a_ref, b_ref, o_ref):
        o_ref[...] = (a_ref[...] + b_ref[...]).astype(BF16)

    return _pcall(body, name=name, grid=(R // br,), in_specs=[spec, spec], out_specs=spec,
                  out_shape=jax.ShapeDtypeStruct((R, C), BF16), compiler_params=_params(("parallel",)))(a, b)


def _adamw_halves(w, own, other, m, v, c_arr, name):
    R, C = w.shape
    hr = R // 2
    br = _ew_rows(hr, C)
    nb = hr // br
    full = pl.BlockSpec((br, C), lambda h, i, c_ref: (h * nb + i, 0))
    half = pl.BlockSpec((br, C), lambda h, i, c_ref: (i, 0))

    def body(c_ref, w_ref, own_ref, oth_ref, m_ref, v_ref, g_ref, d_ref, nm_ref, nv_ref):
        g = jnp.where(pl.program_id(0) == c_ref[0], own_ref[...], oth_ref[...])
        g_ref[...] = g
        d_ref[...], nm_ref[...], nv_ref[...] = _adamw_math(w_ref[...], g, m_ref[...], v_ref[...])

    sh = jax.ShapeDtypeStruct((R, C), F32)
    return _pcall(
        body, name=name,
        grid_spec=pltpu.PrefetchScalarGridSpec(
            num_scalar_prefetch=1, grid=(2, nb), in_specs=[full, half, half, full, full], out_specs=[full] * 4),
        out_shape=[sh] * 4, compiler_params=_params(("parallel", "parallel")))(c_arr, w, own, other, m, v)


def _ada_grad_adamw(c_t, dmod, w, m, v):
    R, C = w.shape
    br = _ew_rows(R, C)
    spec = pl.BlockSpec((br, C), lambda i: (i, 0))

    def body(c_ref, dm_ref, w_ref, m_ref, v_ref, g_ref, d_ref, nm_ref, nv_ref):
        cv = c_ref[...]
        sc = cv * _sig(cv)
        g = sc[:, 0:1] * dm_ref[0:1, :]
        for b in range(1, N_DEV):
            g = g + sc[:, b:b + 1] * dm_ref[b:b + 1, :]
        g_ref[...] = g
        d_ref[...], nm_ref[...], nv_ref[...] = _adamw_math(w_ref[...], g, m_ref[...], v_ref[...])

    sh = jax.ShapeDtypeStruct((R, C), F32)
    return _pcall(
        body, name="ada_grad_adamw", grid=(R // br,),
        in_specs=[pl.BlockSpec((br, N_DEV), lambda i: (i, 0)), pl.BlockSpec((N_DEV, C), lambda i: (0, 0)), spec, spec, spec],
        out_specs=[spec] * 4, out_shape=[sh] * 4, compiler_params=_params(("parallel",)))(c_t, dmod, w, m, v)


SMALL_ROWS = 16


def _small_sum(small_all, lb_logits):
    def body(s_ref, lbl_ref, o_ref):
        acc = s_ref[0:SMALL_ROWS, :]
        for d in range(1, N_DEV):
            acc = acc + s_ref[d * SMALL_ROWS:(d + 1) * SMALL_ROWS, :]
        o_ref[...] = acc
        z = lbl_ref[...]
        e = jnp.exp(z - jnp.max(z, axis=0, keepdims=True))
        p0 = e[0:1, :] / (e[0:1, :] + e[1:2, :])
        dz = acc[8:9, 0:A_WIDTH] * p0 * (1.0 - p0)
        o_ref[8:9, 0:A_WIDTH] = dz
        o_ref[10:11, 0:A_WIDTH] = -dz

    return _pcall(body, name="small_sum", out_shape=jax.ShapeDtypeStruct((SMALL_ROWS, D_MODEL), F32),
                  in_specs=[pl.BlockSpec(memory_space=pltpu.VMEM)] * 2, out_specs=pl.BlockSpec(memory_space=pltpu.VMEM),
                  compiler_params=_params())(small_all, lb_logits)


RELATIONS = ((1, 0), (0, 1), (1, 1))
ANY = pl.BlockSpec(memory_space=pl.ANY)


def _place():
    x, y, c = lax.axis_index("x"), lax.axis_index("y"), lax.axis_index("c")
    return x, y, c


def _allgather_small(x_shard, name):
    m_per, n = x_shard.shape

    def body(x_ref, out_ref, send_sems, recv_sems, local_sem):
        x, y, c = _place()
        me, sibling = (x, y, c), (x, y, 1 - c)
        chips = [(1 - x, y), (x, 1 - y), (1 - x, 1 - y)]

        def rows(px, py, pc):
            return out_ref.at[pl.ds((4 * px + 2 * py + pc) * m_per, m_per), :]

        def copy(k, block, to, src=None):
            return pltpu.make_async_remote_copy(
                src_ref=rows(*block) if src is None else src, dst_ref=rows(*block),
                send_sem=send_sems.at[k], recv_sem=recv_sems.at[k], device_id=to, device_id_type=MESH)

        mine = pltpu.make_async_copy(x_ref, rows(*me), local_sem)
        mine.start()
        first = [copy(0, me, sibling, src=x_ref)]
        first += [copy(1 + j, me, (*chip, c), src=x_ref) for j, chip in enumerate(chips)]
        for cp in first:
            cp.start()
        passed = [copy(4 + j, (*chip, c), sibling) for j, chip in enumerate(chips)]
        for j, chip in enumerate(chips):
            copy(1 + j, (*chip, c), me).wait_recv()
            passed[j].start()
        copy(0, sibling, me).wait_recv()
        for j, chip in enumerate(chips):
            copy(4 + j, (*chip, 1 - c), me).wait_recv()
        for cp in first + passed:
            cp.wait_send()
        mine.wait()

    return _pcall(
        body, name=name, out_shape=jax.ShapeDtypeStruct((N_DEV * m_per, n), x_shard.dtype),
        in_specs=[pl.BlockSpec(memory_space=pltpu.VMEM)], out_specs=pl.BlockSpec(memory_space=pltpu.VMEM),
        scratch_shapes=[pltpu.SemaphoreType.DMA((7,)), pltpu.SemaphoreType.DMA((7,)), pltpu.SemaphoreType.DMA],
        compiler_params=_params(),
    )(x_shard)


W_SHAPES = ((D_MODEL, IN_WIDTH, True), (A_WIDTH, D_MODEL, True), (B_WIDTH, D_MODEL, True),
            (D_MODEL, D_MODEL, False), (D_MODEL, MLP_HIDDEN, True), (MLP_HIDDEN, D_MODEL, False))
N_W = len(W_SHAPES)


def _shard_shape(w):
    R, C, by_col = W_SHAPES[w]
    return (R, C // N_CHIPS) if by_col else (R // N_CHIPS, C)


def _half_shape(w):
    sr, sc = _shard_shape(w)
    return sr // 2, sc


def _region(full_ref, w, chip, half, quarter=None):
    sr, sc = _shard_shape(w)
    by_col = W_SHAPES[w][2]
    r0, c0 = (0, chip * sc) if by_col else (chip * sr, 0)
    r0, rows = r0 + half * (sr // 2), sr // 2
    if quarter is not None:
        r0, rows = r0 + quarter * (rows // 2), rows // 2
    return full_ref.at[pl.ds(r0, rows), pl.ds(c0, sc)]


def _on_device(fn):
    x, y, c = _place()
    me = 4 * x + 2 * y + c
    for d in range(N_DEV):
        @pl.when(me == d)
        def _(d=d):
            fn(x, y, c, d)


GATHER_COPIES = (
    (0, 0, None, "x"), (0, 0, None, "y"),
    (1, 2, 0, "y"), (1, 1, 1, "x"),
    (1, 2, None, "s"), (1, 1, None, "s"),
    (2, 3, 0, "s"), (2, 3, 1, "s"),
)
PEER_FLIP = {"x": 2, "y": 1, "s": 0}


GATHER_STAGES = {
    None: (((), (0, 1), ()), ((0, 1), (2, 3, 4, 5), ()), ((2, 3), (6, 7), ()), ((4, 5, 6, 7), (), tuple(range(8)))),
    "near": (((), (0, 1), ()), ((0, 1), (), (0, 1))),
    "far": (((), (2, 3, 4, 5), ()), ((2, 3), (6, 7), ()), ((4, 5, 6, 7), (), (2, 3, 4, 5, 6, 7))),
}


def _gather_plan(partials, pass_at=(0.5, 0.75), part=None):
    ws = sorted(partials)
    n_t = len(GATHER_COPIES)
    jobs = [(i, w) for i, w in enumerate(ws)]

    def copy(pi, po, ps, x, y, c, d, i, w, t, landing):
        chip, dc = d >> 1, d & 1
        stage, flip, quarter, to = GATHER_COPIES[t]
        if landing:
            peer_chip = chip ^ PEER_FLIP[to]
            part = _region(po[i], w, peer_chip ^ flip, (1 - dc) if to == "s" else dc, quarter)
            src = part
        else:
            part = _region(po[i], w, chip ^ flip, dc, quarter)
            here = flip != 0 and (part_of is None or stage == 2)
            src = part if here else _region(pi[i], w, chip ^ flip, dc, quarter)
        target = {"x": (x ^ 1, y, c), "y": (x, y ^ 1, c), "s": (x, y, 1 - c)}[to]
        return pltpu.make_async_remote_copy(
            src_ref=src, dst_ref=part, send_sem=ps[0].at[i * n_t + t], recv_sem=ps[1].at[i * n_t + t],
            device_id=target, device_id_type=MESH)

    part_of = part

    def stage(landed, started, sent):
        def run(pi, po, ps):
            def on(x, y, c, d):
                for i, w in jobs:
                    for t in landed:
                        copy(pi, po, ps, x, y, c, d, i, w, t, True).wait_recv()
                for i, w in jobs:
                    for t in started:
                        copy(pi, po, ps, x, y, c, d, i, w, t, False).start()
                for i, w in jobs:
                    for t in sent:
                        copy(pi, po, ps, x, y, c, d, i, w, t, False).wait_send()
            _on_device(on)
        return run

    stages = [stage(*st) for st in GATHER_STAGES[part]]
    mid_at = tuple(pass_at) if part is None else tuple(pass_at)[:len(stages) - 2]
    return _Plan([partials[w] for w in ws], [jax.ShapeDtypeStruct(W_SHAPES[w][:2], BF16) for w in ws],
                 [pltpu.SemaphoreType.DMA((n_t * len(ws),)) for _ in range(2)], stages,
                 {i: i for i in range(len(ws))}, mid_at=mid_at)


def _grad_view(g, w):
    R, C, by_col = W_SHAPES[w]
    return g.reshape(1, 2, R // 2, C) if by_col else g.reshape(N_CHIPS, 2, R // N_CHIPS // 2, C)


def _start_wait_plan(ins, outs, n_copies, copies):
    def start(pi, po, ps):
        for cp in copies(pi, po, ps):
            cp.start()

    def finish(pi, po, ps):
        for cp in copies(pi, po, ps):
            cp.wait()

    return _Plan(ins, outs, [pltpu.SemaphoreType.DMA((n_copies,)), pltpu.SemaphoreType.DMA((n_copies,))], [start, finish])


def _sibling_exchange_plan(g4s):
    pieces = [(i, p) for i, g in enumerate(g4s) for p in range(g.shape[0])]

    def copies(pi, po, ps):
        x, y, c = _place()
        return [pltpu.make_async_remote_copy(
            src_ref=pi[i].at[p, 1 - c], dst_ref=po[i].at[p], send_sem=ps[0].at[n], recv_sem=ps[1].at[n],
            device_id=(x, y, 1 - c), device_id_type=MESH) for n, (i, p) in enumerate(pieces)]

    return _start_wait_plan(list(g4s), [jax.ShapeDtypeStruct((g.shape[0],) + g.shape[2:], F32) for g in g4s],
                            len(pieces), copies)


def _pair_sum(g4, other, c_arr, name):
    P, _, hr, C = g4.shape
    br = _ew_rows(hr, C)

    def body(c_ref, g_ref, o_ref, p_ref):
        p_ref[...] = (g_ref[...] + o_ref[...]).astype(BF16)

    return _pcall(
        body, name=name,
        grid_spec=pltpu.PrefetchScalarGridSpec(
            num_scalar_prefetch=1, grid=(P, hr // br),
            in_specs=[pl.BlockSpec((None, None, br, C), lambda p, i, c_ref: (p, c_ref[0], i, 0)),
                      pl.BlockSpec((None, br, C), lambda p, i, c_ref: (p, i, 0))],
            out_specs=pl.BlockSpec((None, br, C), lambda p, i, c_ref: (p, i, 0))),
        out_shape=jax.ShapeDtypeStruct((P, hr, C), BF16),
        compiler_params=_params(("parallel", "parallel")),
    )(c_arr, g4, other)


def _pair_part(p_ref, w, chip):
    sr, sc = _shard_shape(w)
    return p_ref.at[0, :, pl.ds(chip * sc, sc)] if W_SHAPES[w][2] else p_ref.at[chip]


def _chip_exchange_plan(pairs, rels=(0, 1, 2), into=None):
    ws = sorted(pairs)
    n = len(ws)

    def stage(wait):
        def run(pi, po, ps):
            def on(x, y, c, d):
                for i, w in enumerate(ws):
                    for k, (rx, ry) in enumerate(RELATIONS):
                        if k not in rels:
                            continue
                        cp = pltpu.make_async_remote_copy(
                            src_ref=_pair_part(pi[i], w, (d >> 1) ^ (2 * rx + ry)), dst_ref=po[i].at[k],
                            send_sem=ps[0].at[i * 3 + k], recv_sem=ps[1].at[i * 3 + k],
                            device_id=(x ^ rx, y ^ ry, c), device_id_type=MESH)
                        if wait:
                            cp.wait()
                        else:
                            cp.start()
            _on_device(on)
        return run

    ins = [pairs[w] for w in ws] + ([into[w] for w in ws] if into else [])
    return _Plan(ins, [jax.ShapeDtypeStruct((3,) + _half_shape(w), BF16) for w in ws],
                 [pltpu.SemaphoreType.DMA((3 * n,)), pltpu.SemaphoreType.DMA((3 * n,))],
                 [stage(False), stage(True)], {n + i: i for i in range(n)} if into else None)


def _sum_slots(pair, slots, w, chip_arr, name):
    _, hr, C = slots.shape
    br = _ew_rows(hr, C)
    own_map = (lambda i, chip: (0, i, chip[0])) if W_SHAPES[w][2] else (lambda i, chip: (chip[0], i, 0))

    def body(chip_ref, p_ref, s_ref, o_ref):
        acc = p_ref[...].astype(F32)
        for k in range(3):
            acc = acc + s_ref[k].astype(F32)
        o_ref[...] = acc

    return _pcall(
        body, name=name,
        grid_spec=pltpu.PrefetchScalarGridSpec(
            num_scalar_prefetch=1, grid=(hr // br,),
            in_specs=[pl.BlockSpec((None, br, C), own_map), pl.BlockSpec((3, br, C), lambda i, chip: (0, i, 0))],
            out_specs=pl.BlockSpec((br, C), lambda i, chip: (i, 0))),
        out_shape=jax.ShapeDtypeStruct((hr, C), F32), compiler_params=_params(("parallel",)),
    )(chip_arr, pair, slots)


def _sibling_share_plan(halves):
    def copies(pi, po, ps):
        x, y, c = _place()
        return [pltpu.make_async_remote_copy(
            src_ref=pi[i], dst_ref=po[i], send_sem=ps[0].at[i], recv_sem=ps[1].at[i],
            device_id=(x, y, 1 - c), device_id_type=MESH) for i in range(len(halves))]

    return _start_wait_plan(list(halves), [jax.ShapeDtypeStruct(h.shape, F32) for h in halves], len(halves), copies)


def _pad_lanes(v, width=D_MODEL):
    return jnp.pad(v, ((0, 0), (0, width - v.shape[1])))


def _pack_small(b_ada, norm1, norm2, lb, o_gain, q_gain, k_gain, sinks):
    rows = [b_ada.reshape(N_MOD, D_MODEL), norm1, norm2, jnp.concatenate([lb[0:1], o_gain], axis=1),
            _pad_lanes(jnp.concatenate([q_gain, k_gain, sinks], axis=1)), _pad_lanes(lb[1:2]),
            jnp.zeros((SMALL_ROWS - 11, D_MODEL), F32)]
    return jnp.concatenate(rows, axis=0)


def _unpack_small(p):
    return (p[0:6].reshape(1, N_MOD * D_MODEL), p[6:7], p[7:8],
            jnp.concatenate([p[8:9, 0:A_WIDTH], p[10:11, 0:A_WIDTH]], axis=0), p[8:9, A_WIDTH:],
            p[9:10, 0:64], p[9:10, 64:128], p[9:10, 128:144])


def kernel(x, c, w_ada, b_ada, norm1_gain, w_in, lb_logits, hgrn_o_gain, q_norm_gain, k_norm_gain, sinks, w_branch_a, w_branch_b, w_out, norm2_gain, w_mlp_in, w_mlp_out, loss_target, m_w_ada, m_b_ada, m_norm1_gain, m_w_in, m_lb_logits, m_hgrn_o_gain, m_q_norm_gain, m_k_norm_gain, m_sinks, m_w_branch_a, m_w_branch_b, m_w_out, m_norm2_gain, m_w_mlp_in, m_w_mlp_out, v_w_ada, v_b_ada, v_norm1_gain, v_w_in, v_lb_logits, v_hgrn_o_gain, v_q_norm_gain, v_k_norm_gain, v_sinks, v_w_branch_a, v_w_branch_b, v_w_out, v_norm2_gain, v_w_mlp_in, v_w_mlp_out):
    xi, yi, ci = _place()
    chip = 2 * xi + yi
    me = 4 * xi + 2 * yi + ci
    ada_cols = w_ada.shape[2]

    c_all = _allgather_small(jnp.broadcast_to(c, (8, D_MODEL)), "gather_c").reshape(N_DEV, 8, D_MODEL)[:, 0]
    b_cols = lax.dynamic_slice(b_ada, (0, chip * ada_cols), (1, ada_cols))
    mod_part = _ada_fwd(c_all, w_ada[0], b_cols)
    mod_all = _allgather_small(mod_part, "gather_mod").reshape(N_CHIPS, 2, N_DEV, ada_cols)[:, 0]
    mod_mine = lax.dynamic_index_in_dim(mod_all, me, axis=1, keepdims=False).reshape(N_MOD, D_MODEL)
    mod8 = jnp.concatenate([mod_mine, jnp.zeros((2, D_MODEL), F32)], axis=0)

    shards = (w_in[0], w_branch_a[0], w_branch_b[0], w_out[0], w_mlp_in[0], w_mlp_out[0])
    chip_arr = chip.astype(jnp.int32).reshape(1)
    c_arr = ci.astype(jnp.int32).reshape(1)

    grad_x, halves, theirs, st = _local_step(x[0], loss_target[0], mod8, norm1_gain, norm2_gain, lb_logits, hgrn_o_gain,
                                             q_norm_gain, k_norm_gain, sinks, shards, c_arr, chip_arr)
    loss = lax.psum(0.5 * jnp.sum(st["loss"][0]) / D_MODEL, ("x", "y", "c"))
    moments = ((m_w_in, v_w_in), (m_w_branch_a, v_w_branch_a), (m_w_branch_b, v_w_branch_b), (m_w_out, v_w_out),
               (m_w_mlp_in, v_w_mlp_in), (m_w_mlp_out, v_w_mlp_out))
    big = [_adamw_halves(shards[w], halves[w], theirs[w], moments[w][0][0], moments[w][1][0], c_arr, f"adamw{w}")
           for w in range(N_W)]

    swa = st["swa"]
    small = jnp.concatenate([
        st["n1"][1:2], st["n1"][0:1], st["n2"][3:4], st["n2"][1:2], st["n2"][0:1], st["loss"][1:2],
        st["n1"][2:3], st["n2"][2:3], jnp.concatenate([st["d_lb"][0:1], st["d_og"][0:1]], axis=1),
        _pad_lanes(jnp.concatenate([swa[0:1, 0:64], swa[1:2, 0:64], swa[2:3, 0:16]], axis=1)),
        jnp.zeros((SMALL_ROWS - 10, D_MODEL), F32)], axis=0)
    small_all = _allgather_small(small, "gather_small")
    g_small = _small_sum(small_all, lb_logits)
    small_w = (b_ada, norm1_gain, norm2_gain, lb_logits, hgrn_o_gain, q_norm_gain, k_norm_gain, sinks)
    small_m = (m_b_ada, m_norm1_gain, m_norm2_gain, m_lb_logits, m_hgrn_o_gain, m_q_norm_gain, m_k_norm_gain, m_sinks)
    small_v = (v_b_ada, v_norm1_gain, v_norm2_gain, v_lb_logits, v_hgrn_o_gain, v_q_norm_gain, v_k_norm_gain, v_sinks)
    sm = [_unpack_small(t) for t in
          (g_small,) + tuple(_adamw(_pack_small(*small_w), g_small, _pack_small(*small_m), _pack_small(*small_v),
                                    "adamw_small"))]
    g_b, g_n1, g_n2, g_lb, g_og, g_qg, g_kg, g_sk = ([t[i] for t in sm] for i in range(8))

    dmod_all = small_all.reshape(N_DEV, SMALL_ROWS, D_MODEL)[:, 0:N_MOD].reshape(N_DEV, N_MOD * D_MODEL)
    dmod_cols = lax.dynamic_slice(dmod_all, (0, chip * ada_cols), (N_DEV, ada_cols))
    ada = _ada_grad_adamw(c_all.T, dmod_cols, w_ada[0], m_w_ada[0], v_w_ada[0])

    def ordered(k):
        lead = lambda a: a[None]
        return (lead(ada[k]), g_b[k], g_n1[k], lead(big[0][k]), g_lb[k], g_og[k], g_qg[k], g_kg[k], g_sk[k],
                lead(big[1][k]), lead(big[2][k]), lead(big[3][k]), g_n2[k], lead(big[4][k]), lead(big[5][k]))

    return (loss, grad_x[None]) + ordered(0) + ordered(1) + ordered(2) + ordered(3)
```

```python
import jax
import jax.numpy as jnp
from jax import lax
from jax.experimental import pallas as pl
from jax.experimental.pallas import tpu as pltpu

F32 = jnp.float32
BF16 = jnp.bfloat16
HIGHEST = lax.Precision.HIGHEST
MESH = pl.DeviceIdType.MESH

D_MODEL = 2048
A_WIDTH = 1024
A_HEADS = 8
A_HEAD_DIM = 128
A_CHUNK = 64
B_WIDTH = 1024
B_HEAD_DIM = 64
B_GROUP = 4
B_KV_HEADS = 4
B_KV_WIDTH = 256
BLOCK = 128
MLP_HIDDEN = 8192
IN_WIDTH = 9728
N_MOD = 6
EPS = 1e-6
N_CHIPS = 4
N_DEV = 8

OFF_QA, OFF_FA, OFF_IA, OFF_GA = 0, 1024, 2048, 3072
OFF_QB, OFF_KB, OFF_VB = 4096, 5120, 5376
OFF_GATE_A, OFF_GATE_B = 5632, 7680

ADAM_LR = 0.001
ADAM_B1 = 0.9
ADAM_B2 = 0.999
ADAM_EPS = 1e-08
ADAM_WD = 0.01
ADAM_STEP = 10

VMEM_LIMIT_V7X = 48 * 1024 * 1024
NEG_BIG = -1e30


def _params(sem=None, vmem=VMEM_LIMIT_V7X):
    return pltpu.CompilerParams(dimension_semantics=sem, vmem_limit_bytes=vmem)


class _Plan:
    def __init__(self, ins, outs, sems, stages, aliases=None, mid_at=()):
        self.ins, self.outs, self.sems, self.stages, self.aliases = ins, outs, sems, stages, aliases or {}
        self.mid_at = tuple(mid_at)
        assert len(self.mid_at) == len(stages) - 2


def _join(a, b):
    assert len(a.stages) == 2 and len(b.stages) == 2
    ni, no, ns = len(a.ins), len(a.outs), len(a.sems)

    def stage(k):
        def run(pi, po, ps):
            a.stages[k](pi[:ni], po[:no], ps[:ns])
            b.stages[k](pi[ni:], po[no:], ps[ns:])
        return run

    aliases = dict(a.aliases)
    aliases.update({ni + i: no + o for i, o in b.aliases.items()})
    return _Plan(a.ins + b.ins, a.outs + b.outs, a.sems + b.sems, [stage(0), stage(1)], aliases)


def _pcall(body, plan=None, **kw):
    if plan is None:
        return pl.pallas_call(body, **kw)
    grid = kw["grid"]
    single = not isinstance(kw["out_specs"], (list, tuple))
    in_specs = list(kw["in_specs"])
    out_specs = [kw["out_specs"]] if single else list(kw["out_specs"])
    out_shape = [kw["out_shape"]] if single else list(kw["out_shape"])
    scratch = list(kw.get("scratch_shapes", ()))
    n_in, n_out, n_scr = len(in_specs), len(out_specs), len(scratch)
    n_pi, n_po = len(plan.ins), len(plan.outs)
    total = 1
    for g in grid:
        total *= g
    n_st = len(plan.stages)

    def wrapped(*refs):
        o0 = n_in + n_pi
        s0 = o0 + n_out + n_po
        pi, po, ps = refs[n_in:o0], refs[o0 + n_out:s0], refs[s0 + n_scr:]
        lin = 0
        for d, g in enumerate(grid):
            lin = lin * g + pl.program_id(d)
        for si, frac in enumerate((0.0,) + plan.mid_at):
            @pl.when(lin == int(frac * (total - 1)))
            def _(si=si):
                plan.stages[si](pi, po, ps)
        body(*refs[:n_in], *refs[o0:o0 + n_out], *refs[s0:s0 + n_scr])

        @pl.when(lin == total - 1)
        def _():
            plan.stages[-1](pi, po, ps)

    any_spec = pl.BlockSpec(memory_space=pl.ANY)
    call = pl.pallas_call(
        wrapped, name=kw["name"], grid=grid, in_specs=in_specs + [any_spec] * n_pi,
        out_specs=out_specs + [any_spec] * n_po, out_shape=out_shape + list(plan.outs),
        scratch_shapes=scratch + list(plan.sems),
        input_output_aliases={n_in + i: n_out + o for i, o in plan.aliases.items()},
        compiler_params=_params(("arbitrary",) * len(grid)))

    def run(*args):
        res = call(*args, *plan.ins)
        outs = list(res[:n_out])
        return (outs[0] if single else outs), list(res[n_out:])

    return run


def _run_plan(plan, name):
    return _pcall(lambda: None, plan=plan, name=name, grid=(1,), in_specs=[], out_specs=[], out_shape=[])()[1]


def _sig(x):
    return 1.0 / (1.0 + jnp.exp(-x))


def _nn(a, b):
    return lax.dot_general(a.astype(BF16), b.astype(BF16), (((1,), (0,)), ((), ())), preferred_element_type=F32)


def _nt(a, b):
    return lax.dot_general(a.astype(BF16), b.astype(BF16), (((1,), (1,)), ((), ())), preferred_element_type=F32)


def _tn(a, b):
    return lax.dot_general(a.astype(BF16), b.astype(BF16), (((0,), (0,)), ((), ())), preferred_element_type=F32)


def _mm(a, b, *, name, ta=False, tb=False, bm=1024, bn=1024, bk=2048, out_dtypes=(F32,), epi=None, extras=(),
        extra_cols=None, plan=None, a_blocks=None, row_extras=(), n_stats=0):
    if ta:
        K, M = a.shape
        bk = K
        if a_blocks is not None:
            M = a_blocks[0] * bm
    else:
        M, K = a.shape
    if tb:
        N, K2 = b.shape
    else:
        K2, N = b.shape
    bm, bn, bk = min(bm, M), min(bn, N), min(bk, K)
    assert K == K2 and M % bm == 0 and N % bn == 0 and K % bk == 0, (name, a.shape, b.shape)
    nk = K // bk
    a_col = a_blocks[1] if a_blocks is not None else (lambda i: i)
    a_spec = pl.BlockSpec((bk, bm), lambda i, j, k: (k, a_col(i))) if ta else pl.BlockSpec((bm, bk), lambda i, j, k: (i, k))
    b_spec = pl.BlockSpec((bn, bk), lambda i, j, k: (j, k)) if tb else pl.BlockSpec((bk, bn), lambda i, j, k: (k, j))
    t_spec = pl.BlockSpec((bm, bn), lambda i, j, k: (i, j))
    extra_cols = extra_cols or (0,) * len(extras)
    e_specs = [pl.BlockSpec((bm, bn), lambda i, j, k, off=off: (i, off + j)) for off in extra_cols]
    e_specs += [pl.BlockSpec((8, bn), lambda i, j, k: (0, j)) for _ in row_extras]
    dims = (((1,), (1 if tb else 0,)), ((), ()))
    n_e, n_o = len(extras) + len(row_extras), len(out_dtypes)
    stat_spec = pl.BlockSpec((8, bn), lambda i, j, k: (i, j))

    def body(*refs):
        a_ref, b_ref = refs[0], refs[1]
        e_refs = refs[2:2 + n_e]
        o_refs = refs[2 + n_e:2 + n_e + n_o]

        def finish(acc):
            outs = (acc,) if epi is None else epi(acc, *[e[...] for e in e_refs])
            for o_ref, o in zip(o_refs, outs):
                o_ref[...] = o.astype(o_ref.dtype)

        if ta:
            at_ref = refs[-1]

            @pl.when(pl.program_id(1) == 0)
            def _():
                at_ref[...] = a_ref[...].T

            lhs = at_ref[...]
        else:
            lhs = a_ref[...].astype(BF16)
        part = lax.dot_general(lhs, b_ref[...].astype(BF16), dims, preferred_element_type=F32)
        if nk == 1:
            finish(part)
        else:
            acc_ref = refs[-1]
            k = pl.program_id(2)

            @pl.when(k == 0)
            def _():
                acc_ref[...] = part

            @pl.when(k > 0)
            def _():
                acc_ref[...] += part

            @pl.when(k == nk - 1)
            def _():
                finish(acc_ref[...])

    if ta:
        assert a.dtype == BF16 and nk == 1
        scratch = [pltpu.VMEM((bm, bk), BF16)]
    else:
        scratch = [pltpu.VMEM((bm, bn), F32)] if nk > 1 else []
    out = _pcall(
        body, plan=plan, name=name, grid=(M // bm, N // bn, nk),
        in_specs=[a_spec, b_spec] + e_specs,
        out_specs=[t_spec] * (n_o - n_stats) + [stat_spec] * n_stats,
        out_shape=[jax.ShapeDtypeStruct((M, N), dt) for dt in out_dtypes[:n_o - n_stats]]
        + [jax.ShapeDtypeStruct((8 * (M // bm), N), F32)] * n_stats,
        scratch_shapes=scratch,
        compiler_params=_params(("parallel", "arbitrary", "arbitrary")),
    )(a, b, *extras, *row_extras)
    if plan is not None:
        return (out[0][0] if n_o == 1 else out[0]), out[1]
    return out[0] if n_o == 1 else out


def _ada_fwd(c_all, w_ada, b_cols):
    n = w_ada.shape[1]
    bn = 512

    def body(c_ref, w_ref, b_ref, o_ref):
        cv = c_ref[...]
        sc = cv * _sig(cv)
        o_ref[...] = jnp.dot(sc, w_ref[...], precision=HIGHEST, preferred_element_type=F32) + b_ref[...]

    return _pcall(
        body, name="ada_fwd", grid=(n // bn,),
        in_specs=[pl.BlockSpec((N_DEV, D_MODEL), lambda j: (0, 0)), pl.BlockSpec((D_MODEL, bn), lambda j: (0, j)),
                  pl.BlockSpec((1, bn), lambda j: (0, j))],
        out_specs=pl.BlockSpec((N_DEV, bn), lambda j: (0, j)),
        out_shape=jax.ShapeDtypeStruct((N_DEV, n), F32),
        compiler_params=_params(("parallel",)),
    )(c_all, w_ada, b_cols)


ROWS_EW = 256


def _rms_fwd_math(x, gain, scale, shift):
    rstd = lax.rsqrt(jnp.mean(x * x, axis=-1, keepdims=True) + EPS)
    xhat = x * rstd
    n = xhat * gain
    return n * (1.0 + scale) + shift, xhat, n, rstd


def _rms_bwd_math(dh, xhat, n, rstd, gain, scale):
    dn = dh * (1.0 + scale)
    dxhat = dn * gain
    dx = rstd * (dxhat - xhat * jnp.mean(dxhat * xhat, axis=-1, keepdims=True))
    d_scale = jnp.sum(dh * n, axis=0, keepdims=True)
    d_shift = jnp.sum(dh, axis=0, keepdims=True)
    d_gain = jnp.sum(dn * xhat, axis=0, keepdims=True)
    return dx, d_scale, d_shift, d_gain


def _row_spec(w=D_MODEL, br=ROWS_EW):
    return pl.BlockSpec((br, w), lambda i: (i, 0))


def _vec_spec(r=8, w=D_MODEL):
    return pl.BlockSpec((r, w), lambda i: (0, 0))


def _norm1_fwd(x, gain, mod8, plan=None):
    T = x.shape[0]

    def body(x_ref, g_ref, m_ref, h_ref):
        h, _, _, _ = _rms_fwd_math(x_ref[...], g_ref[...], m_ref[1:2, :], m_ref[0:1, :])
        h_ref[...] = h.astype(BF16)

    return _pcall(
        body, plan=plan, name="norm1_fwd", grid=(T // ROWS_EW,),
        in_specs=[_row_spec(), _vec_spec(1), _vec_spec()],
        out_specs=_row_spec(), out_shape=jax.ShapeDtypeStruct((T, D_MODEL), BF16),
        compiler_params=_params(("parallel",)),
    )(x, gain, mod8)


def _res_norm2_fwd(x, mo, gain, mod8):
    T = x.shape[0]

    def body(x_ref, mo_ref, g_ref, m_ref, x1_ref, h_ref):
        x1 = x_ref[...] + m_ref[2:3, :] * mo_ref[...]
        x1_ref[...] = x1
        h, _, _, _ = _rms_fwd_math(x1, g_ref[...], m_ref[4:5, :], m_ref[3:4, :])
        h_ref[...] = h.astype(BF16)

    return _pcall(
        body, name="res_norm2_fwd", grid=(T // ROWS_EW,),
        in_specs=[_row_spec(), _row_spec(), _vec_spec(1), _vec_spec()],
        out_specs=[_row_spec(), _row_spec()],
        out_shape=[jax.ShapeDtypeStruct((T, D_MODEL), F32), jax.ShapeDtypeStruct((T, D_MODEL), BF16)],
        compiler_params=_params(("parallel",)),
    )(x, mo, gain, mod8)


def _loss_head(mlp, x1, target, mod):
    gate = mod[5:6, :]
    err = x1 + gate * mlp - target
    dy = err * (1.0 / D_MODEL)
    row = lax.broadcasted_iota(jnp.int32, (8, mlp.shape[1]), 0)
    stats = jnp.where(row == 0, jnp.sum(err * err, axis=0, keepdims=True),
                      jnp.where(row == 1, jnp.sum(dy * mlp, axis=0, keepdims=True), 0.0))
    return dy, dy * gate, stats


def _norm2_bwd(dh2, x1, dy, mo, gain, mod8):
    T = x1.shape[0]

    def body(dh_ref, x1_ref, dy_ref, mo_ref, g_ref, m_ref, dx1_ref, dmo_ref, st_ref):
        i = pl.program_id(0)
        gain_v, scale = g_ref[...], m_ref[4:5, :]
        _, xhat, n, rstd = _rms_fwd_math(x1_ref[...], gain_v, scale, m_ref[3:4, :])
        dx, d_scale, d_shift, d_gain = _rms_bwd_math(dh_ref[...], xhat, n, rstd, gain_v, scale)
        dx1 = dy_ref[...] + dx
        dx1_ref[...] = dx1
        dmo_ref[...] = (dx1 * m_ref[2:3, :]).astype(BF16)

        @pl.when(i == 0)
        def _():
            st_ref[...] = jnp.zeros_like(st_ref)

        st_ref[0:1, :] += d_scale
        st_ref[1:2, :] += d_shift
        st_ref[2:3, :] += d_gain
        st_ref[3:4, :] += jnp.sum(dx1 * mo_ref[...], axis=0, keepdims=True)

    return _pcall(
        body, name="norm2_bwd", grid=(T // ROWS_EW,),
        in_specs=[_row_spec(), _row_spec(), _row_spec(), _row_spec(), _vec_spec(1), _vec_spec()],
        out_specs=[_row_spec(), _row_spec(), _vec_spec()],
        out_shape=[jax.ShapeDtypeStruct((T, D_MODEL), F32), jax.ShapeDtypeStruct((T, D_MODEL), BF16),
                   jax.ShapeDtypeStruct((8, D_MODEL), F32)],
        compiler_params=_params(("arbitrary",)),
    )(dh2, x1, dy, mo, gain, mod8)


def _norm1_bwd(dh, x, dx1, gain, mod8):
    T = x.shape[0]

    def body(dh_ref, x_ref, dx1_ref, g_ref, m_ref, dx_ref, st_ref):
        i = pl.program_id(0)
        gain_v, scale = g_ref[...], m_ref[1:2, :]
        _, xhat, n, rstd = _rms_fwd_math(x_ref[...], gain_v, scale, m_ref[0:1, :])
        dx, d_scale, d_shift, d_gain = _rms_bwd_math(dh_ref[...], xhat, n, rstd, gain_v, scale)
        dx_ref[...] = dx1_ref[...] + dx

        @pl.when(i == 0)
        def _():
            st_ref[...] = jnp.zeros_like(st_ref)

        st_ref[0:1, :] += d_scale
        st_ref[1:2, :] += d_shift
        st_ref[2:3, :] += d_gain

    return _pcall(
        body, name="norm1_bwd", grid=(T // ROWS_EW,),
        in_specs=[_row_spec(), _row_spec(), _row_spec(), _vec_spec(1), _vec_spec()],
        out_specs=[_row_spec(), _vec_spec()],
        out_shape=[jax.ShapeDtypeStruct((T, D_MODEL), F32), jax.ShapeDtypeStruct((8, D_MODEL), F32)],
        compiler_params=_params(("arbitrary",)),
    )(dh, x, dx1, gain, mod8)


MERGE_BC = 512


def _hgrn_rows(T):
    return 512 if T >= 1024 else 128


def _lower_bound(lbl):
    e = jnp.exp(lbl - jnp.max(lbl, axis=0, keepdims=True))
    return e[0:1, :] / (e[0:1, :] + e[1:2, :])


def _chunk_sum_matrix(rows, backward):
    shift = A_CHUNK.bit_length() - 1
    r = lax.broadcasted_iota(jnp.int32, (rows, rows), 0)
    c = lax.broadcasted_iota(jnp.int32, (rows, rows), 1)
    same = jnp.right_shift(r, shift) == jnp.right_shift(c, shift)
    return (same & ((r <= c) if backward else (r >= c))).astype(BF16)


def _chunk_sums(m, x):
    n = x.shape[1]
    hi = x.astype(BF16)
    rest = x - hi.astype(F32)
    mid = rest.astype(BF16)
    lo = (rest - mid.astype(F32)).astype(BF16)
    y = jnp.dot(m, jnp.concatenate([hi, mid, lo], axis=1), preferred_element_type=F32)
    return y[:, 0:n] + y[:, n:2 * n] + y[:, 2 * n:3 * n]


def _hgrn_block_pre(q, fl, lb, m_fwd):
    sg = _sig(fl)
    f = lb + (1.0 - lb) * sg
    sq = _sig(q)
    return dict(sg=sg, f=f, k=1.0 - f, sq=sq, qf=q * sq, b=_chunk_sums(m_fwd, jnp.log(f)))


def _hgrn_chunk_local(pre, r):
    C = A_CHUNK
    qf, k, b = pre["qf"][r], pre["k"][r], pre["b"][r]
    causal = lax.broadcasted_iota(jnp.int32, (C, C), 0) >= lax.broadcasted_iota(jnp.int32, (C, C), 1)
    bm = b[C // 2 - 1:C // 2, :]
    bl = b[C - 1:C, :]
    e_q, e_k = jnp.exp(b - bm), jnp.exp(bm - b)
    e_b, e_l = jnp.exp(b), jnp.exp(bl - b)
    qd, kd = qf * e_q, k * e_k
    qe, ke = qf * e_b, k * e_l
    att = jnp.where(causal, _nt(qd, kd), 0.0)
    return dict(causal=causal, e_q=e_q, e_k=e_k, e_b=e_b, e_l=e_l, qd=qd, kd=kd, qe=qe, ke=ke, att=att, dec=jnp.exp(bl))


def _hgrn_chunk_fwd(pre, r, v, st):
    c = _hgrn_chunk_local(pre, r)
    c["o"] = _nn(c["att"], v) + _nt(c["qe"], st)
    return c


def _lockstep(gens):
    out = [None] * len(gens)
    live = list(enumerate(gens))
    while live:
        still = []
        for i, g in live:
            try:
                next(g)
                still.append((i, g))
            except StopIteration as done:
                out[i] = done.value
        live = still
    return out


HGRN_HEADS_PER_STEP = 4


def _hgrn_fwd(proj, lb_logits, o_gain, plan=None):
    T = proj.shape[0]
    BR = _hgrn_rows(T)
    cps = BR // A_CHUNK
    K, NH = A_HEAD_DIM, HGRN_HEADS_PER_STEP
    W = NH * K

    def col(off):
        return pl.BlockSpec((BR, W), lambda h, cb: (cb, off // W + h))

    def body(q_ref, f_ref, i_ref, g_ref, lbl_ref, og_ref, o_ref, s_ref, st):
        @pl.when(pl.program_id(1) == 0)
        def _():
            st[...] = jnp.zeros_like(st)

        lb_all = _lower_bound(lbl_ref[...])
        m_fwd = _chunk_sum_matrix(BR, False)
        pre = [_hgrn_block_pre(q_ref[:, n * K:(n + 1) * K], f_ref[:, n * K:(n + 1) * K], lb_all[:, n * K:(n + 1) * K], m_fwd)
               for n in range(NH)]
        def local(n, ci):
            r, hs = slice(ci * A_CHUNK, (ci + 1) * A_CHUNK), slice(n * K, (n + 1) * K)
            v = i_ref[r, hs]
            c = _hgrn_chunk_local(pre[n], r)
            yield
            return dict(o=_nn(c["att"], v), ds=_tn(v, c["ke"]), qe=c["qe"], dec=c["dec"])

        def chain(n, loc):
            hs = slice(n * K, (n + 1) * K)
            state = st[n]
            for ci, p in enumerate(loc):
                r = slice(ci * A_CHUNK, (ci + 1) * A_CHUNK)
                s_ref[n, ci] = state
                o = p["o"] + _nt(p["qe"], state)
                state = state * p["dec"] + p["ds"]
                yield
                on = o * lax.rsqrt(jnp.mean(o * o, axis=-1, keepdims=True) + EPS)
                g = g_ref[r, hs]
                o_ref[r, hs] = (on * og_ref[:, hs] * (g * _sig(g))).astype(BF16)
            st[n] = state

        loc = _lockstep([local(n, ci) for n in range(NH) for ci in range(cps)])
        _lockstep([chain(n, loc[n * cps:(n + 1) * cps]) for n in range(NH)])

    return _pcall(
        body, plan=plan, name="hgrn_fwd", grid=(A_HEADS // NH, T // BR),
        in_specs=[col(OFF_QA), col(OFF_FA), col(OFF_IA), col(OFF_GA),
                  pl.BlockSpec((2, W), lambda h, cb: (0, h)), pl.BlockSpec((1, W), lambda h, cb: (0, h))],
        out_specs=[pl.BlockSpec((BR, W), lambda h, cb: (cb, h)),
                   pl.BlockSpec((NH, cps, K, K), lambda h, cb: (h, cb, 0, 0))],
        out_shape=[jax.ShapeDtypeStruct((T, A_WIDTH), BF16),
                   jax.ShapeDtypeStruct((A_HEADS, T // A_CHUNK, K, K), F32)],
        scratch_shapes=[pltpu.VMEM((NH, K, K), F32)],
        compiler_params=_params(("parallel", "arbitrary")),
    )(proj, proj, proj, proj, lb_logits, o_gain)


def _hgrn_bwd(proj, lb_logits, o_gain, states, do, plan=None):
    T = proj.shape[0]
    BR = _hgrn_rows(T)
    cps = BR // A_CHUNK
    ncb = T // BR
    K, C, NH = A_HEAD_DIM, A_CHUNK, HGRN_HEADS_PER_STEP
    W = NH * K

    def col(off):
        return pl.BlockSpec((BR, W), lambda h, cb: (ncb - 1 - cb, off // W + h))

    def body(q_ref, f_ref, i_ref, g_ref, lbl_ref, og_ref, s_ref, do_ref,
             dq_ref, df_ref, di_ref, dg_ref, dlb_ref, dog_ref, dst):
        @pl.when(pl.program_id(1) == 0)
        def _():
            dst[...] = jnp.zeros_like(dst)
            dlb_ref[...] = jnp.zeros_like(dlb_ref)
            dog_ref[...] = jnp.zeros_like(dog_ref)

        lb_all = _lower_bound(lbl_ref[...])
        row = lax.broadcasted_iota(jnp.int32, (C, K), 0)
        m_fwd, m_bwd = _chunk_sum_matrix(BR, False), _chunk_sum_matrix(BR, True)
        pre = [_hgrn_block_pre(q_ref[:, n * K:(n + 1) * K], f_ref[:, n * K:(n + 1) * K], lb_all[:, n * K:(n + 1) * K], m_fwd)
               for n in range(NH)]
        def local(n, ci):
            r, hs = slice(ci * C, (ci + 1) * C), slice(n * K, (n + 1) * K)
            gain = og_ref[:, hs]
            st = s_ref[n, ci]
            v = i_ref[r, hs]
            q = q_ref[r, hs]
            c = _hgrn_chunk_fwd(pre[n], r, v, st)
            yield
            o = c["o"]
            rn = lax.rsqrt(jnp.mean(o * o, axis=-1, keepdims=True) + EPS)
            on = o * rn
            g = g_ref[r, hs]
            sgg = _sig(g)
            dy = do_ref[r, hs]
            d_ong = dy * (g * sgg)
            dg_ref[r, hs] = (dy * (on * gain) * (sgg * (1.0 + g * (1.0 - sgg)))).astype(BF16)
            d_on = d_ong * gain
            d_o = rn * (d_on - on * jnp.mean(d_on * on, axis=-1, keepdims=True))
            datt = jnp.where(c["causal"], _nt(d_o, v), 0.0)
            dqe = _nn(d_o, st)
            yield
            dqd = _nn(datt, c["kd"])
            dkd = _tn(datt, c["qd"])
            dv = _tn(c["att"], d_o)
            ds = _tn(d_o, c["qe"])
            yield
            t_q, t_k = dqd * c["qd"], dkd * c["kd"]
            sq = pre[n]["sq"][r]
            dq_ref[r, hs] = ((dqd * c["e_q"] + dqe * c["e_b"]) * (sq * (1.0 + q * (1.0 - sq)))).astype(BF16)
            return dict(v=v, st=st, ke=c["ke"], e_l=c["e_l"], dec=c["dec"], dv=dv, ds=ds, dk=dkd * c["e_k"],
                        db=t_q - t_k + dqe * c["qe"], dbm=jnp.sum(t_k - t_q, axis=0, keepdims=True),
                        d_og=jnp.sum(d_ong * on, axis=0, keepdims=True))

        def chain(n, loc):
            hs = slice(n * K, (n + 1) * K)
            dst_next = dst[n]
            db_of, dk_of = [None] * cps, [None] * cps
            for ci in reversed(range(cps)):
                p = loc[ci]
                di_ref[ci * C:(ci + 1) * C, hs] = (p["dv"] + _nt(p["ke"], dst_next)).astype(BF16)
                dke = _nn(p["v"], dst_next)
                yield
                t_l = dke * p["ke"]
                dbl = jnp.sum(t_l, axis=0, keepdims=True) + jnp.sum(dst_next * p["st"], axis=0, keepdims=True) * p["dec"]
                db_of[ci] = p["db"] - t_l + jnp.where(row == C // 2 - 1, p["dbm"], 0.0) + jnp.where(row == C - 1, dbl, 0.0)
                dk_of[ci] = p["dk"] + dke * p["e_l"]
                dst_next = dst_next * p["dec"] + p["ds"]
            dst[n] = dst_next
            return db_of, dk_of

        loc = _lockstep([local(n, ci) for n in range(NH) for ci in range(cps)])
        loc = [loc[n * cps:(n + 1) * cps] for n in range(NH)]
        chains = _lockstep([chain(n, loc[n]) for n in range(NH)])
        for n in range(NH):
            hs = slice(n * K, (n + 1) * K)
            db_of, dk_of = chains[n]
            d_og = loc[n][0]["d_og"]
            for p in loc[n][1:]:
                d_og = d_og + p["d_og"]
            dog_ref[0:1, hs] += d_og
            lb, sg = lb_all[:, hs], pre[n]["sg"]
            dlf = _chunk_sums(m_bwd, jnp.concatenate(db_of, axis=0))
            df = dlf / pre[n]["f"] - jnp.concatenate(dk_of, axis=0)
            df_ref[:, hs] = (df * (1.0 - lb) * sg * (1.0 - sg)).astype(BF16)
            dlb_ref[0:1, hs] += jnp.sum(df * (1.0 - sg), axis=0, keepdims=True)

    ocol = pl.BlockSpec((BR, W), lambda h, cb: (ncb - 1 - cb, h))
    vec = pl.BlockSpec((8, W), lambda h, cb: (0, h))
    return _pcall(
        body, plan=plan, name="hgrn_bwd", grid=(A_HEADS // NH, ncb),
        in_specs=[col(OFF_QA), col(OFF_FA), col(OFF_IA), col(OFF_GA),
                  pl.BlockSpec((2, W), lambda h, cb: (0, h)), pl.BlockSpec((1, W), lambda h, cb: (0, h)),
                  pl.BlockSpec((NH, cps, K, K), lambda h, cb: (h, ncb - 1 - cb, 0, 0)),
                  pl.BlockSpec((BR, W), lambda h, cb: (ncb - 1 - cb, h))],
        out_specs=[ocol, ocol, ocol, ocol, vec, vec],
        out_shape=[jax.ShapeDtypeStruct((T, A_WIDTH), BF16)] * 4 + [jax.ShapeDtypeStruct((8, A_WIDTH), F32)] * 2,
        scratch_shapes=[pltpu.VMEM((NH, K, K), F32)],
        compiler_params=_params(("parallel", "arbitrary")),
    )(proj, proj, proj, proj, lb_logits, o_gain, states, do)


def _head_norm(x):
    r = lax.rsqrt(jnp.mean(x * x, axis=-1, keepdims=True) + EPS)
    return x * r, r


def _head_norm_bwd(dy, xn, r, gain):
    dxn = dy * gain
    return r * (dxn - xn * jnp.mean(dxn * xn, axis=-1, keepdims=True)), jnp.sum(dy * xn, axis=0, keepdims=True)


def _swa_mask(has_prev):
    rows = B_GROUP * BLOCK
    r = lax.broadcasted_iota(jnp.int32, (rows, 2 * BLOCK), 0) % BLOCK
    c = lax.broadcasted_iota(jnp.int32, (rows, 2 * BLOCK), 1)
    rel = r + BLOCK - c
    return (rel >= 0) & (rel < BLOCK) & ((c >= BLOCK) | has_prev)


def _swa_head_fwd(j, q_ref, kp_ref, kc_ref, vp_ref, vc_ref, qg, kg, sk_ref, mask):
    hs = slice(j * B_HEAD_DIM, (j + 1) * B_HEAD_DIM)
    kcat = jnp.concatenate([kp_ref[:, hs], kc_ref[:, hs]], axis=0)
    vcat = jnp.concatenate([vp_ref[:, hs], vc_ref[:, hs]], axis=0)
    qs = jnp.concatenate([q_ref[:, pl.ds((j * B_GROUP + g) * B_HEAD_DIM, B_HEAD_DIM)] for g in range(B_GROUP)], axis=0)
    kn, kr = _head_norm(kcat)
    qn, qr = _head_norm(qs)
    kh, qh = kn * kg, qn * qg
    yield
    s = jnp.where(mask, _nt(qh, kh) * (B_HEAD_DIM ** -0.5), NEG_BIG)
    yield
    sink = jnp.concatenate(
        [jnp.broadcast_to(sk_ref[0:1, pl.ds(j * B_GROUP + g, 1)], (BLOCK, 1)) for g in range(B_GROUP)], axis=0)
    m = jnp.maximum(jnp.max(s, axis=-1, keepdims=True), sink)
    p = jnp.exp(s - m)
    e_sink = jnp.exp(sink - m)
    inv = 1.0 / (jnp.sum(p, axis=-1, keepdims=True) + e_sink)
    prob = p * inv
    return dict(vcat=vcat, kn=kn, kr=kr, qn=qn, qr=qr, kh=kh, qh=qh, prob=prob, p_sink=e_sink * inv)


def _swa_in_specs(nb, last):
    def qi(n):
        return jnp.minimum(n, last)

    q = pl.BlockSpec((BLOCK, B_WIDTH), lambda n: (qi(n), OFF_QB // B_WIDTH))
    kc = pl.BlockSpec((BLOCK, B_KV_WIDTH), lambda n: (qi(n), OFF_KB // B_KV_WIDTH))
    kp = pl.BlockSpec((BLOCK, B_KV_WIDTH), lambda n: (jnp.maximum(qi(n) - 1, 0), OFF_KB // B_KV_WIDTH))
    vc = pl.BlockSpec((BLOCK, B_KV_WIDTH), lambda n: (qi(n), OFF_VB // B_KV_WIDTH))
    vp = pl.BlockSpec((BLOCK, B_KV_WIDTH), lambda n: (jnp.maximum(qi(n) - 1, 0), OFF_VB // B_KV_WIDTH))
    small = [pl.BlockSpec((1, B_HEAD_DIM), lambda n: (0, 0)), pl.BlockSpec((1, B_HEAD_DIM), lambda n: (0, 0)),
             pl.BlockSpec((1, B_GROUP * B_KV_HEADS), lambda n: (0, 0))]
    return [q, kp, kc, vp, vc] + small


def _swa_fwd(proj, q_gain, k_gain, sinks, plan=None):
    T = proj.shape[0]
    nb = T // BLOCK

    def body(q_ref, kp_ref, kc_ref, vp_ref, vc_ref, qg_ref, kg_ref, sk_ref, o_ref):
        mask = _swa_mask(pl.program_id(0) > 0)

        def head(j):
            c = yield from _swa_head_fwd(j, q_ref, kp_ref, kc_ref, vp_ref, vc_ref, qg_ref[...], kg_ref[...], sk_ref, mask)
            yield
            o = _nn(c["prob"], c["vcat"])
            yield
            for g in range(B_GROUP):
                o_ref[:, pl.ds((j * B_GROUP + g) * B_HEAD_DIM, B_HEAD_DIM)] = o[g * BLOCK:(g + 1) * BLOCK].astype(BF16)

        _lockstep([head(j) for j in range(B_KV_HEADS)])

    return _pcall(
        body, plan=plan, name="swa_fwd", grid=(nb,),
        in_specs=_swa_in_specs(nb, nb - 1),
        out_specs=pl.BlockSpec((BLOCK, B_WIDTH), lambda n: (n, 0)),
        out_shape=jax.ShapeDtypeStruct((T, B_WIDTH), BF16),
        compiler_params=_params(("parallel",)),
    )(proj, proj, proj, proj, proj, q_gain, k_gain, sinks)


def _swa_bwd(proj, q_gain, k_gain, sinks, do, plan=None):
    T = proj.shape[0]
    nb = T // BLOCK
    scale = B_HEAD_DIM ** -0.5

    def body(q_ref, kp_ref, kc_ref, vp_ref, vc_ref, qg_ref, kg_ref, sk_ref, do_ref,
             dq_ref, dkv_ref, sm_ref, ck, cv):
        n = pl.program_id(0)

        @pl.when(n == 0)
        def _():
            ck[...] = jnp.zeros_like(ck)
            cv[...] = jnp.zeros_like(cv)
            sm_ref[...] = jnp.zeros_like(sm_ref)

        @pl.when(n < nb)
        def _():
            mask = _swa_mask(n > 0)
            qg, kg = qg_ref[...], kg_ref[...]
            lane = lax.broadcasted_iota(jnp.int32, (1, BLOCK), 1)
            def head(j):
                hs = slice(j * B_HEAD_DIM, (j + 1) * B_HEAD_DIM)
                vs = slice(B_KV_WIDTH + j * B_HEAD_DIM, B_KV_WIDTH + (j + 1) * B_HEAD_DIM)
                c = yield from _swa_head_fwd(j, q_ref, kp_ref, kc_ref, vp_ref, vc_ref, qg, kg, sk_ref, mask)
                d_out = jnp.concatenate(
                    [do_ref[:, pl.ds((j * B_GROUP + g) * B_HEAD_DIM, B_HEAD_DIM)] for g in range(B_GROUP)], axis=0)
                prob = c["prob"]
                yield
                out = _nn(prob, c["vcat"])
                d_prob = _nt(d_out, c["vcat"])
                dv = _tn(prob, d_out)
                yield
                delta = jnp.sum(d_out * out, axis=-1, keepdims=True)
                ds = prob * (d_prob - delta)
                d_sink = -c["p_sink"] * delta
                yield
                dqh = _nn(ds, c["kh"]) * scale
                dkh = _tn(ds, c["qh"]) * scale
                yield
                dq, dqg = _head_norm_bwd(dqh, c["qn"], c["qr"], qg)
                dk, dkg = _head_norm_bwd(dkh, c["kn"], c["kr"], kg)
                d_sinks = jnp.zeros((1, BLOCK), F32)
                for g in range(B_GROUP):
                    dq_ref[:, pl.ds((j * B_GROUP + g) * B_HEAD_DIM, B_HEAD_DIM)] = dq[g * BLOCK:(g + 1) * BLOCK].astype(BF16)
                    tot = jnp.sum(d_sink[g * BLOCK:(g + 1) * BLOCK], axis=0, keepdims=True)
                    d_sinks = d_sinks + jnp.where(lane == j * B_GROUP + g, tot, 0.0)
                dkv_ref[:, hs] = (ck[:, hs] + dk[0:BLOCK]).astype(BF16)
                dkv_ref[:, vs] = (cv[:, hs] + dv[0:BLOCK]).astype(BF16)
                ck[:, hs] = dk[BLOCK:2 * BLOCK]
                cv[:, hs] = dv[BLOCK:2 * BLOCK]
                return dqg, dkg, d_sinks

            small = _lockstep([head(j) for j in range(B_KV_HEADS)])
            sm_ref[0:1, 0:B_HEAD_DIM] += small[0][0] + small[1][0] + small[2][0] + small[3][0]
            sm_ref[1:2, 0:B_HEAD_DIM] += small[0][1] + small[1][1] + small[2][1] + small[3][1]
            sm_ref[2:3, :] += small[0][2] + small[1][2] + small[2][2] + small[3][2]

        @pl.when(n == nb)
        def _():
            dkv_ref[:, 0:B_KV_WIDTH] = ck[...].astype(BF16)
            dkv_ref[:, B_KV_WIDTH:2 * B_KV_WIDTH] = cv[...].astype(BF16)

    return _pcall(
        body, plan=plan, name="swa_bwd", grid=(nb + 1,),
        in_specs=_swa_in_specs(nb, nb - 1) + [pl.BlockSpec((BLOCK, B_WIDTH), lambda n: (jnp.minimum(n, nb - 1), 0))],
        out_specs=[pl.BlockSpec((BLOCK, B_WIDTH), lambda n: (jnp.minimum(n, nb - 1), 0)),
                   pl.BlockSpec((BLOCK, 2 * B_KV_WIDTH), lambda n: (jnp.maximum(n - 1, 0), 0)),
                   pl.BlockSpec((8, BLOCK), lambda n: (0, 0))],
        out_shape=[jax.ShapeDtypeStruct((T, B_WIDTH), BF16), jax.ShapeDtypeStruct((T, 2 * B_KV_WIDTH), BF16),
                   jax.ShapeDtypeStruct((8, BLOCK), F32)],
        scratch_shapes=[pltpu.VMEM((BLOCK, B_KV_WIDTH), F32), pltpu.VMEM((BLOCK, B_KV_WIDTH), F32)],
        compiler_params=_params(("arbitrary",)),
    )(proj, proj, proj, proj, proj, q_gain, k_gain, sinks, do)


W_IN, W_A, W_B, W_OUT, W_MI, W_MO = range(6)


def _local_step(x, target, mod8, norm1_gain, norm2_gain, lb_logits, o_gain, q_gain, k_gain, sinks, shards, c_arr, chip_arr):
    relu2 = lambda u: (u, jnp.square(jnp.maximum(u, 0.0)))
    pair, half = {}, {}

    def exchange(ws, grads):
        return _sibling_exchange_plan([_grad_view(g, w) for w, g in zip(ws, grads)])

    def pair_sums(ws, grads, others):
        for w, g, o in zip(ws, grads, others):
            pair[w] = _pair_sum(_grad_view(g, w), o, c_arr, f"pair_sum{w}")

    def sum_slots(ws, slots):
        for w, s in zip(ws, slots):
            half[w] = _sum_slots(pair[w], s, w, chip_arr, f"sum_slots{w}")

    part_in = _cast_into_full({W_IN: shards[W_IN]}, "cast_w_in")[W_IN]
    h, (part_in,) = _norm1_fwd(x, norm1_gain, mod8, plan=_gather_plan({W_IN: part_in}, part="near"))
    parts, (w_in,) = _cast_into_full({w: shards[w] for w in range(1, N_W)}, "cast_rest",
                                     plan=_gather_plan({W_IN: part_in}, pass_at=(0.8,), part="far"))
    proj, (w_mi,) = _mm(h, w_in, name="mm_proj", bn=512, plan=_gather_plan({W_MI: parts[W_MI]}, pass_at=(0.47, 0.72)))
    (o_a, states), (w_a, w_b) = _hgrn_fwd(
        proj, lb_logits, o_gain, plan=_gather_plan({w: parts[w] for w in (W_A, W_B)}, pass_at=(0.4, 0.65)))
    o_b, (w_out,) = _swa_fwd(proj, q_gain, k_gain, sinks, plan=_gather_plan({W_OUT: parts[W_OUT]}, pass_at=(0.3, 0.5)))
    ya = _mm(o_a, w_a, name="mm_branch_a")
    gate_cols = (OFF_GATE_A // MERGE_BC, OFF_GATE_B // MERGE_BC)
    yb, merged = _mm(o_b, w_b, name="mm_branch_b", bn=MERGE_BC, out_dtypes=(F32, BF16),
                     extras=(proj, proj, ya), extra_cols=gate_cols + (0,),
                     epi=lambda acc, ga, gb, ya_: (acc, _sig(ga) * ya_ + _sig(gb) * acc))
    mo = _mm(merged, w_out, name="mm_out")
    x1, h2 = _res_norm2_fwd(x, mo, norm2_gain, mod8)
    (u, act), (w_mo,) = _mm(h2, w_mi, name="mm_mlp_in", out_dtypes=(F32, BF16), epi=relu2,
                            plan=_gather_plan({W_MO: parts[W_MO]}, pass_at=(0.6, 0.9)))
    dy, dmlp, st_loss = _mm(act, w_mo, name="mm_mlp_out", bm=512, out_dtypes=(F32, BF16, F32), n_stats=1,
                            extras=(x1, target), row_extras=(mod8,), epi=_loss_head)
    st_loss = st_loss.reshape(-1, 8, D_MODEL).sum(axis=0)
    def half_blocks(w, own):
        def block(i):
            return 2 * i + (lax.axis_index("c") if own else 1 - lax.axis_index("c"))
        return (1 if W_SHAPES[w][2] else N_CHIPS), block

    def pair_of(w, lhs, rhs, other, name):
        hr, cols = _half_shape(w)
        p = _mm(lhs, rhs, name=name, ta=True, bn=512, a_blocks=half_blocks(w, True), out_dtypes=(BF16,),
                extras=(other,), epi=lambda acc, o: (acc + o,))
        return p.reshape(-1, hr, W_SHAPES[w][1])

    near, far = (0, 1), (2,)
    g_send = _mm(act, dmlp, name="mm_g_mlp_out_send", ta=True, bn=512, a_blocks=half_blocks(W_MO, False))
    du, (g_other,) = _mm(dmlp, w_mo, name="mm_d_act", tb=True, out_dtypes=(BF16,), extras=(u,),
                         epi=lambda acc, uu: (acc * (2.0 * jnp.maximum(uu, 0.0)),), plan=_sibling_share_plan([g_send]))
    pair[W_MO] = pair_of(W_MO, act, dmlp, g_other, "mm_g_mlp_out_own")
    g_send, (part,) = _mm(h2, du, name="mm_g_mlp_in_send", ta=True, bn=512, a_blocks=half_blocks(W_MI, False),
                          plan=_chip_exchange_plan({W_MO: pair[W_MO]}, near))
    dh2, res = _mm(du, w_mi, name="mm_d_h2", tb=True,
                   plan=_join(_chip_exchange_plan({W_MO: pair[W_MO]}, far, {W_MO: part}), _sibling_share_plan([g_send])))
    sum_slots([W_MO], res[:1])
    pair[W_MI] = pair_of(W_MI, h2, du, res[1], "mm_g_mlp_in_own")
    dx1, dmo, st_n2 = _norm2_bwd(dh2, x1, dy, mo, norm2_gain, mod8)
    def merge_bwd(dm, ga, gb, ya_, yb_):
        sa, sb = _sig(ga), _sig(gb)
        return dm * sa, dm * sb, dm * ya_ * sa * (1.0 - sa), dm * yb_ * sb * (1.0 - sb)

    dya, dyb, dga, dgb = _mm(dmo, w_out, name="mm_d_merged", tb=True, bn=MERGE_BC, out_dtypes=(BF16,) * 4,
                             extras=(proj, proj, ya, yb), extra_cols=gate_cols + (0, 0), epi=merge_bwd)
    g_out = _mm(merged, dmo, name="mm_g_out", ta=True, bn=512)
    do_a = _mm(dya, w_a, name="mm_d_oa", tb=True)
    g_a = _mm(o_a, dya, name="mm_g_branch_a", ta=True, bn=512)
    do_b = _mm(dyb, w_b, name="mm_d_ob", tb=True)
    g_b = _mm(o_b, dyb, name="mm_g_branch_b", ta=True, bn=512)
    mid = [W_A, W_B, W_OUT]
    (dqb, dkvb, st_swa), res = _swa_bwd(
        proj, q_gain, k_gain, sinks, do_b,
        plan=_join(_chip_exchange_plan({W_MI: pair[W_MI]}), exchange(mid, [g_a, g_b, g_out])))
    sum_slots([W_MI], res[:1])
    pair_sums(mid, [g_a, g_b, g_out], res[1:])
    (dqa, dfa, dia, dgga, d_lb, d_og), slots_mid = _hgrn_bwd(
        proj, lb_logits, o_gain, states, do_a, plan=_chip_exchange_plan({w: pair[w] for w in mid}))
    sum_slots(mid, slots_mid)
    dproj = jnp.concatenate([dqa, dfa, dia, dgga, dqb, dkvb, dga, dgb], axis=1)
    done = [W_A, W_B, W_OUT, W_MI, W_MO]
    g_send = _mm(h, dproj, name="mm_g_in_send", ta=True, bn=512, a_blocks=half_blocks(W_IN, False))
    g_own, res = _mm(h, dproj, name="mm_g_in_own", ta=True, bn=512, a_blocks=half_blocks(W_IN, True),
                     plan=_sibling_share_plan([g_send] + [half[w] for w in done]))
    g_other, theirs = res[0], dict(zip(done, res[1:]))
    pair[W_IN] = _add_bf16(g_own, g_other, "pair_sum0")[None]
    dh, slots_in = _mm(dproj, w_in, name="mm_d_h", tb=True, bk=2432, plan=_chip_exchange_plan({W_IN: pair[W_IN]}))
    sum_slots([W_IN], slots_in)
    grad_x, st_n1 = _norm1_bwd(dh, x, dx1, norm1_gain, mod8)
    (theirs[W_IN],) = _run_plan(_sibling_share_plan([half[W_IN]]), "sibling_share_w_in")
    stats = dict(loss=st_loss, n2=st_n2, n1=st_n1, d_lb=d_lb, d_og=d_og, swa=st_swa)
    return grad_x, [half[w] for w in range(N_W)], [theirs[w] for w in range(N_W)], stats


EW_BLOCK_BYTES = 2 << 20


def _ew_rows(rows, cols):
    br = 8
    while br * 2 <= rows and br * 2 * cols * 4 <= EW_BLOCK_BYTES and rows % (br * 2) == 0:
        br *= 2
    return br


CAST_STEPS = 16


def _cast_into_full(shards, name, plan=None):
    ws = sorted(shards)
    in_specs, out_specs, out_shape = [], [], []
    for w in ws:
        sr, sc = shards[w].shape
        R, C, by_col = W_SHAPES[w]
        br = sr // CAST_STEPS
        assert br * CAST_STEPS == sr and br % 16 == 0, (w, sr)

        def out_map(i, by_col=by_col):
            chip = 2 * lax.axis_index("x") + lax.axis_index("y")
            return (i, chip) if by_col else (chip * CAST_STEPS + i, 0)

        in_specs.append(pl.BlockSpec((br, sc), lambda i: (i, 0)))
        out_specs.append(pl.BlockSpec((br, sc), out_map))
        out_shape.append(jax.ShapeDtypeStruct((R, C), BF16))

    def body(*refs):
        for w_ref, o_ref in zip(refs[:len(ws)], refs[len(ws):]):
            o_ref[...] = w_ref[...].astype(BF16)

    res = _pcall(body, plan=plan, name=name, grid=(CAST_STEPS,), in_specs=in_specs, out_specs=out_specs,
                 out_shape=out_shape, compiler_params=_params(("arbitrary",)))(*[shards[w] for w in ws])
    if plan is None:
        return dict(zip(ws, res))
    return dict(zip(ws, res[0])), res[1]


def _adamw_math(w, g, m, v):
    m = ADAM_B1 * m + (1.0 - ADAM_B1) * g
    v = ADAM_B2 * v + (1.0 - ADAM_B2) * (g * g)
    m_hat = m / (1.0 - ADAM_B1 ** ADAM_STEP)
    v_hat = v / (1.0 - ADAM_B2 ** ADAM_STEP)
    delta = -ADAM_LR * (m_hat / (jnp.sqrt(v_hat) + ADAM_EPS) + ADAM_WD * w)
    return delta, m, v


def _adamw(w, g, m, v, name):
    R, C = w.shape
    br = _ew_rows(R, C)
    spec = pl.BlockSpec((br, C), lambda i: (i, 0))

    def body(w_ref, g_ref, m_ref, v_ref, d_ref, nm_ref, nv_ref):
        d_ref[...], nm_ref[...], nv_ref[...] = _adamw_math(w_ref[...], g_ref[...], m_ref[...], v_ref[...])

    sh = jax.ShapeDtypeStruct((R, C), F32)
    return _pcall(body, name=name, grid=(R // br,), in_specs=[spec] * 4, out_specs=[spec] * 3, out_shape=[sh] * 3,
                  compiler_params=_params(("parallel",)))(w, g, m, v)


def _add_bf16(a, b, name):
    R, C = a.shape
    br = _ew_rows(R, C)
    spec = pl.BlockSpec((br, C), lambda i: (i, 0))

    def body(a_ref, b_ref, o_ref):
        o_ref[...] = (a_ref[...] + b_ref[...]).astype(BF16)

    return _pcall(body, name=name, grid=(R // br,), in_specs=[spec, spec], out_specs=spec,
                  out_shape=jax.ShapeDtypeStruct((R, C), BF16), compiler_params=_params(("parallel",)))(a, b)


def _adamw_halves(w, own, other, m, v, c_arr, name):
    R, C = w.shape
    hr = R // 2
    br = _ew_rows(hr, C)
    nb = hr // br
    full = pl.BlockSpec((br, C), lambda h, i, c_ref: (h * nb + i, 0))
    half = pl.BlockSpec((br, C), lambda h, i, c_ref: (i, 0))

    def body(c_ref, w_ref, own_ref, oth_ref, m_ref, v_ref, g_ref, d_ref, nm_ref, nv_ref):
        g = jnp.where(pl.program_id(0) == c_ref[0], own_ref[...], oth_ref[...])
        g_ref[...] = g
        d_ref[...], nm_ref[...], nv_ref[...] = _adamw_math(w_ref[...], g, m_ref[...], v_ref[...])

    sh = jax.ShapeDtypeStruct((R, C), F32)
    return _pcall(
        body, name=name,
        grid_spec=pltpu.PrefetchScalarGridSpec(
            num_scalar_prefetch=1, grid=(2, nb), in_specs=[full, half, half, full, full], out_specs=[full] * 4),
        out_shape=[sh] * 4, compiler_params=_params(("parallel", "parallel")))(c_arr, w, own, other, m, v)


def _ada_grad_adamw(c_t, dmod, w, m, v):
    R, C = w.shape
    br = _ew_rows(R, C)
    spec = pl.BlockSpec((br, C), lambda i: (i, 0))

    def body(c_ref, dm_ref, w_ref, m_ref, v_ref, g_ref, d_ref, nm_ref, nv_ref):
        cv = c_ref[...]
        sc = cv * _sig(cv)
        g = sc[:, 0:1] * dm_ref[0:1, :]
        for b in range(1, N_DEV):
            g = g + sc[:, b:b + 1] * dm_ref[b:b + 1, :]
        g_ref[...] = g
        d_ref[...], nm_ref[...], nv_ref[...] = _adamw_math(w_ref[...], g, m_ref[...], v_ref[...])

    sh = jax.ShapeDtypeStruct((R, C), F32)
    return _pcall(
        body, name="ada_grad_adamw", grid=(R // br,),
        in_specs=[pl.BlockSpec((br, N_DEV), lambda i: (i, 0)), pl.BlockSpec((N_DEV, C), lambda i: (0, 0)), spec, spec, spec],
        out_specs=[spec] * 4, out_shape=[sh] * 4, compiler_params=_params(("parallel",)))(c_t, dmod, w, m, v)


SMALL_ROWS = 16


def _small_sum(small_all, lb_logits):
    def body(s_ref, lbl_ref, o_ref):
        acc = s_ref[0:SMALL_ROWS, :]
        for d in range(1, N_DEV):
            acc = acc + s_ref[d * SMALL_ROWS:(d + 1) * SMALL_ROWS, :]
        o_ref[...] = acc
        z = lbl_ref[...]
        e = jnp.exp(z - jnp.max(z, axis=0, keepdims=True))
        p0 = e[0:1, :] / (e[0:1, :] + e[1:2, :])
        dz = acc[8:9, 0:A_WIDTH] * p0 * (1.0 - p0)
        o_ref[8:9, 0:A_WIDTH] = dz
        o_ref[10:11, 0:A_WIDTH] = -dz

    return _pcall(body, name="small_sum", out_shape=jax.ShapeDtypeStruct((SMALL_ROWS, D_MODEL), F32),
                  in_specs=[pl.BlockSpec(memory_space=pltpu.VMEM)] * 2, out_specs=pl.BlockSpec(memory_space=pltpu.VMEM),
                  compiler_params=_params())(small_all, lb_logits)


RELATIONS = ((1, 0), (0, 1), (1, 1))
ANY = pl.BlockSpec(memory_space=pl.ANY)


def _place():
    x, y, c = lax.axis_index("x"), lax.axis_index("y"), lax.axis_index("c")
    return x, y, c


def _allgather_small(x_shard, name):
    m_per, n = x_shard.shape

    def body(x_ref, out_ref, send_sems, recv_sems, local_sem):
        x, y, c = _place()
        me, sibling = (x, y, c), (x, y, 1 - c)
        chips = [(1 - x, y), (x, 1 - y), (1 - x, 1 - y)]

        def rows(px, py, pc):
            return out_ref.at[pl.ds((4 * px + 2 * py + pc) * m_per, m_per), :]

        def copy(k, block, to, src=None):
            return pltpu.make_async_remote_copy(
                src_ref=rows(*block) if src is None else src, dst_ref=rows(*block),
                send_sem=send_sems.at[k], recv_sem=recv_sems.at[k], device_id=to, device_id_type=MESH)

        mine = pltpu.make_async_copy(x_ref, rows(*me), local_sem)
        mine.start()
        first = [copy(0, me, sibling, src=x_ref)]
        first += [copy(1 + j, me, (*chip, c), src=x_ref) for j, chip in enumerate(chips)]
        for cp in first:
            cp.start()
        passed = [copy(4 + j, (*chip, c), sibling) for j, chip in enumerate(chips)]
        for j, chip in enumerate(chips):
            copy(1 + j, (*chip, c), me).wait_recv()
            passed[j].start()
        copy(0, sibling, me).wait_recv()
        for j, chip in enumerate(chips):
            copy(4 + j, (*chip, 1 - c), me).wait_recv()
        for cp in first + passed:
            cp.wait_send()
        mine.wait()

    return _pcall(
        body, name=name, out_shape=jax.ShapeDtypeStruct((N_DEV * m_per, n), x_shard.dtype),
        in_specs=[pl.BlockSpec(memory_space=pltpu.VMEM)], out_specs=pl.BlockSpec(memory_space=pltpu.VMEM),
        scratch_shapes=[pltpu.SemaphoreType.DMA((7,)), pltpu.SemaphoreType.DMA((7,)), pltpu.SemaphoreType.DMA],
        compiler_params=_params(),
    )(x_shard)


W_SHAPES = ((D_MODEL, IN_WIDTH, True), (A_WIDTH, D_MODEL, True), (B_WIDTH, D_MODEL, True),
            (D_MODEL, D_MODEL, False), (D_MODEL, MLP_HIDDEN, True), (MLP_HIDDEN, D_MODEL, False))
N_W = len(W_SHAPES)


def _shard_shape(w):
    R, C, by_col = W_SHAPES[w]
    return (R, C // N_CHIPS) if by_col else (R // N_CHIPS, C)


def _half_shape(w):
    sr, sc = _shard_shape(w)
    return sr // 2, sc


def _region(full_ref, w, chip, half, quarter=None):
    sr, sc = _shard_shape(w)
    by_col = W_SHAPES[w][2]
    r0, c0 = (0, chip * sc) if by_col else (chip * sr, 0)
    r0, rows = r0 + half * (sr // 2), sr // 2
    if quarter is not None:
        r0, rows = r0 + quarter * (rows // 2), rows // 2
    return full_ref.at[pl.ds(r0, rows), pl.ds(c0, sc)]


def _on_device(fn):
    x, y, c = _place()
    me = 4 * x + 2 * y + c
    for d in range(N_DEV):
        @pl.when(me == d)
        def _(d=d):
            fn(x, y, c, d)


GATHER_COPIES = (
    (0, 0, None, "x"), (0, 0, None, "y"),
    (1, 2, 0, "y"), (1, 1, 1, "x"),
    (1, 2, None, "s"), (1, 1, None, "s"),
    (2, 3, 0, "s"), (2, 3, 1, "s"),
)
PEER_FLIP = {"x": 2, "y": 1, "s": 0}


GATHER_STAGES = {
    None: (((), (0, 1), ()), ((0, 1), (2, 3, 4, 5), ()), ((2, 3), (6, 7), ()), ((4, 5, 6, 7), (), tuple(range(8)))),
    "near": (((), (0, 1), ()), ((0, 1), (), (0, 1))),
    "far": (((), (2, 3, 4, 5), ()), ((2, 3), (6, 7), ()), ((4, 5, 6, 7), (), (2, 3, 4, 5, 6, 7))),
}


def _gather_plan(partials, pass_at=(0.5, 0.75), part=None):
    ws = sorted(partials)
    n_t = len(GATHER_COPIES)
    jobs = [(i, w) for i, w in enumerate(ws)]

    def copy(pi, po, ps, x, y, c, d, i, w, t, landing):
        chip, dc = d >> 1, d & 1
        stage, flip, quarter, to = GATHER_COPIES[t]
        if landing:
            peer_chip = chip ^ PEER_FLIP[to]
            part = _region(po[i], w, peer_chip ^ flip, (1 - dc) if to == "s" else dc, quarter)
            src = part
        else:
            part = _region(po[i], w, chip ^ flip, dc, quarter)
            here = flip != 0 and (part_of is None or stage == 2)
            src = part if here else _region(pi[i], w, chip ^ flip, dc, quarter)
        target = {"x": (x ^ 1, y, c), "y": (x, y ^ 1, c), "s": (x, y, 1 - c)}[to]
        return pltpu.make_async_remote_copy(
            src_ref=src, dst_ref=part, send_sem=ps[0].at[i * n_t + t], recv_sem=ps[1].at[i * n_t + t],
            device_id=target, device_id_type=MESH)

    part_of = part

    def stage(landed, started, sent):
        def run(pi, po, ps):
            def on(x, y, c, d):
                for i, w in jobs:
                    for t in landed:
                        copy(pi, po, ps, x, y, c, d, i, w, t, True).wait_recv()
                for i, w in jobs:
                    for t in started:
                        copy(pi, po, ps, x, y, c, d, i, w, t, False).start()
                for i, w in jobs:
                    for t in sent:
                        copy(pi, po, ps, x, y, c, d, i, w, t, False).wait_send()
            _on_device(on)
        return run

    stages = [stage(*st) for st in GATHER_STAGES[part]]
    mid_at = tuple(pass_at) if part is None else tuple(pass_at)[:len(stages) - 2]
    return _Plan([partials[w] for w in ws], [jax.ShapeDtypeStruct(W_SHAPES[w][:2], BF16) for w in ws],
                 [pltpu.SemaphoreType.DMA((n_t * len(ws),)) for _ in range(2)], stages,
                 {i: i for i in range(len(ws))}, mid_at=mid_at)


def _grad_view(g, w):
    R, C, by_col = W_SHAPES[w]
    return g.reshape(1, 2, R // 2, C) if by_col else g.reshape(N_CHIPS, 2, R // N_CHIPS // 2, C)


def _start_wait_plan(ins, outs, n_copies, copies):
    def start(pi, po, ps):
        for cp in copies(pi, po, ps):
            cp.start()

    def finish(pi, po, ps):
        for cp in copies(pi, po, ps):
            cp.wait()

    return _Plan(ins, outs, [pltpu.SemaphoreType.DMA((n_copies,)), pltpu.SemaphoreType.DMA((n_copies,))], [start, finish])


def _sibling_exchange_plan(g4s):
    pieces = [(i, p) for i, g in enumerate(g4s) for p in range(g.shape[0])]

    def copies(pi, po, ps):
        x, y, c = _place()
        return [pltpu.make_async_remote_copy(
            src_ref=pi[i].at[p, 1 - c], dst_ref=po[i].at[p], send_sem=ps[0].at[n], recv_sem=ps[1].at[n],
            device_id=(x, y, 1 - c), device_id_type=MESH) for n, (i, p) in enumerate(pieces)]

    return _start_wait_plan(list(g4s), [jax.ShapeDtypeStruct((g.shape[0],) + g.shape[2:], F32) for g in g4s],
                            len(pieces), copies)


def _pair_sum(g4, other, c_arr, name):
    P, _, hr, C = g4.shape
    br = _ew_rows(hr, C)

    def body(c_ref, g_ref, o_ref, p_ref):
        p_ref[...] = (g_ref[...] + o_ref[...]).astype(BF16)

    return _pcall(
        body, name=name,
        grid_spec=pltpu.PrefetchScalarGridSpec(
            num_scalar_prefetch=1, grid=(P, hr // br),
            in_specs=[pl.BlockSpec((None, None, br, C), lambda p, i, c_ref: (p, c_ref[0], i, 0)),
                      pl.BlockSpec((None, br, C), lambda p, i, c_ref: (p, i, 0))],
            out_specs=pl.BlockSpec((None, br, C), lambda p, i, c_ref: (p, i, 0))),
        out_shape=jax.ShapeDtypeStruct((P, hr, C), BF16),
        compiler_params=_params(("parallel", "parallel")),
    )(c_arr, g4, other)


def _pair_part(p_ref, w, chip):
    sr, sc = _shard_shape(w)
    return p_ref.at[0, :, pl.ds(chip * sc, sc)] if W_SHAPES[w][2] else p_ref.at[chip]


def _chip_exchange_plan(pairs, rels=(0, 1, 2), into=None):
    ws = sorted(pairs)
    n = len(ws)

    def stage(wait):
        def run(pi, po, ps):
            def on(x, y, c, d):
                for i, w in enumerate(ws):
                    for k, (rx, ry) in enumerate(RELATIONS):
                        if k not in rels:
                            continue
                        cp = pltpu.make_async_remote_copy(
                            src_ref=_pair_part(pi[i], w, (d >> 1) ^ (2 * rx + ry)), dst_ref=po[i].at[k],
                            send_sem=ps[0].at[i * 3 + k], recv_sem=ps[1].at[i * 3 + k],
                            device_id=(x ^ rx, y ^ ry, c), device_id_type=MESH)
                        if wait:
                            cp.wait()
                        else:
                            cp.start()
            _on_device(on)
        return run

    ins = [pairs[w] for w in ws] + ([into[w] for w in ws] if into else [])
    return _Plan(ins, [jax.ShapeDtypeStruct((3,) + _half_shape(w), BF16) for w in ws],
                 [pltpu.SemaphoreType.DMA((3 * n,)), pltpu.SemaphoreType.DMA((3 * n,))],
                 [stage(False), stage(True)], {n + i: i for i in range(n)} if into else None)


def _sum_slots(pair, slots, w, chip_arr, name):
    _, hr, C = slots.shape
    br = _ew_rows(hr, C)
    own_map = (lambda i, chip: (0, i, chip[0])) if W_SHAPES[w][2] else (lambda i, chip: (chip[0], i, 0))

    def body(chip_ref, p_ref, s_ref, o_ref):
        acc = p_ref[...].astype(F32)
        for k in range(3):
            acc = acc + s_ref[k].astype(F32)
        o_ref[...] = acc

    return _pcall(
        body, name=name,
        grid_spec=pltpu.PrefetchScalarGridSpec(
            num_scalar_prefetch=1, grid=(hr // br,),
            in_specs=[pl.BlockSpec((None, br, C), own_map), pl.BlockSpec((3, br, C), lambda i, chip: (0, i, 0))],
            out_specs=pl.BlockSpec((br, C), lambda i, chip: (i, 0))),
        out_shape=jax.ShapeDtypeStruct((hr, C), F32), compiler_params=_params(("parallel",)),
    )(chip_arr, pair, slots)


def _sibling_share_plan(halves):
    def copies(pi, po, ps):
        x, y, c = _place()
        return [pltpu.make_async_remote_copy(
            src_ref=pi[i], dst_ref=po[i], send_sem=ps[0].at[i], recv_sem=ps[1].at[i],
            device_id=(x, y, 1 - c), device_id_type=MESH) for i in range(len(halves))]

    return _start_wait_plan(list(halves), [jax.ShapeDtypeStruct(h.shape, F32) for h in halves], len(halves), copies)


def _pad_lanes(v, width=D_MODEL):
    return jnp.pad(v, ((0, 0), (0, width - v.shape[1])))


def _pack_small(b_ada, norm1, norm2, lb, o_gain, q_gain, k_gain, sinks):
    rows = [b_ada.reshape(N_MOD, D_MODEL), norm1, norm2, jnp.concatenate([lb[0:1], o_gain], axis=1),
            _pad_lanes(jnp.concatenate([q_gain, k_gain, sinks], axis=1)), _pad_lanes(lb[1:2]),
            jnp.zeros((SMALL_ROWS - 11, D_MODEL), F32)]
    return jnp.concatenate(rows, axis=0)


def _unpack_small(p):
    return (p[0:6].reshape(1, N_MOD * D_MODEL), p[6:7], p[7:8],
            jnp.concatenate([p[8:9, 0:A_WIDTH], p[10:11, 0:A_WIDTH]], axis=0), p[8:9, A_WIDTH:],
            p[9:10, 0:64], p[9:10, 64:128], p[9:10, 128:144])


def kernel(x, c, w_ada, b_ada, norm1_gain, w_in, lb_logits, hgrn_o_gain, q_norm_gain, k_norm_gain, sinks, w_branch_a, w_branch_b, w_out, norm2_gain, w_mlp_in, w_mlp_out, loss_target, m_w_ada, m_b_ada, m_norm1_gain, m_w_in, m_lb_logits, m_hgrn_o_gain, m_q_norm_gain, m_k_norm_gain, m_sinks, m_w_branch_a, m_w_branch_b, m_w_out, m_norm2_gain, m_w_mlp_in, m_w_mlp_out, v_w_ada, v_b_ada, v_norm1_gain, v_w_in, v_lb_logits, v_hgrn_o_gain, v_q_norm_gain, v_k_norm_gain, v_sinks, v_w_branch_a, v_w_branch_b, v_w_out, v_norm2_gain, v_w_mlp_in, v_w_mlp_out):
    xi, yi, ci = _place()
    chip = 2 * xi + yi
    me = 4 * xi + 2 * yi + ci
    ada_cols = w_ada.shape[2]

    c_all = _allgather_small(jnp.broadcast_to(c, (8, D_MODEL)), "gather_c").reshape(N_DEV, 8, D_MODEL)[:, 0]
    b_cols = lax.dynamic_slice(b_ada, (0, chip * ada_cols), (1, ada_cols))
    mod_part = _ada_fwd(c_all, w_ada[0], b_cols)
    mod_all = _allgather_small(mod_part, "gather_mod").reshape(N_CHIPS, 2, N_DEV, ada_cols)[:, 0]
    mod_mine = lax.dynamic_index_in_dim(mod_all, me, axis=1, keepdims=False).reshape(N_MOD, D_MODEL)
    mod8 = jnp.concatenate([mod_mine, jnp.zeros((2, D_MODEL), F32)], axis=0)

    shards = (w_in[0], w_branch_a[0], w_branch_b[0], w_out[0], w_mlp_in[0], w_mlp_out[0])
    chip_arr = chip.astype(jnp.int32).reshape(1)
    c_arr = ci.astype(jnp.int32).reshape(1)

    grad_x, halves, theirs, st = _local_step(x[0], loss_target[0], mod8, norm1_gain, norm2_gain, lb_logits, hgrn_o_gain,
                                             q_norm_gain, k_norm_gain, sinks, shards, c_arr, chip_arr)
    loss = lax.psum(0.5 * jnp.sum(st["loss"][0]) / D_MODEL, ("x", "y", "c"))
    moments = ((m_w_in, v_w_in), (m_w_branch_a, v_w_branch_a), (m_w_branch_b, v_w_branch_b), (m_w_out, v_w_out),
               (m_w_mlp_in, v_w_mlp_in), (m_w_mlp_out, v_w_mlp_out))
    big = [_adamw_halves(shards[w], halves[w], theirs[w], moments[w][0][0], moments[w][1][0], c_arr, f"adamw{w}")
           for w in range(N_W)]

    swa = st["swa"]
    small = jnp.concatenate([
        st["n1"][1:2], st["n1"][0:1], st["n2"][3:4], st["n2"][1:2], st["n2"][0:1], st["loss"][1:2],
        st["n1"][2:3], st["n2"][2:3], jnp.concatenate([st["d_lb"][0:1], st["d_og"][0:1]], axis=1),
        _pad_lanes(jnp.concatenate([swa[0:1, 0:64], swa[1:2, 0:64], swa[2:3, 0:16]], axis=1)),
        jnp.zeros((SMALL_ROWS - 10, D_MODEL), F32)], axis=0)
    small_all = _allgather_small(small, "gather_small")
    g_small = _small_sum(small_all, lb_logits)
    small_w = (b_ada, norm1_gain, norm2_gain, lb_logits, hgrn_o_gain, q_norm_gain, k_norm_gain, sinks)
    small_m = (m_b_ada, m_norm1_gain, m_norm2_gain, m_lb_logits, m_hgrn_o_gain, m_q_norm_gain, m_k_norm_gain, m_sinks)
    small_v = (v_b_ada, v_norm1_gain, v_norm2_gain, v_lb_logits, v_hgrn_o_gain, v_q_norm_gain, v_k_norm_gain, v_sinks)
    sm = [_unpack_small(t) for t in
          (g_small,) + tuple(_adamw(_pack_small(*small_w), g_small, _pack_small(*small_m), _pack_small(*small_v),
                                    "adamw_small"))]
    g_b, g_n1, g_n2, g_lb, g_og, g_qg, g_kg, g_sk = ([t[i] for t in sm] for i in range(8))

    dmod_all = small_all.reshape(N_DEV, SMALL_ROWS, D_MODEL)[:, 0:N_MOD].reshape(N_DEV, N_MOD * D_MODEL)
    dmod_cols = lax.dynamic_slice(dmod_all, (0, chip * ada_cols), (N_DEV, ada_cols))
    ada = _ada_grad_adamw(c_all.T, dmod_cols, w_ada[0], m_w_ada[0], v_w_ada[0])

    def ordered(k):
        lead = lambda a: a[None]
        return (lead(ada[k]), g_b[k], g_n1[k], lead(big[0][k]), g_lb[k], g_og[k], g_qg[k], g_kg[k], g_sk[k],
                lead(big[1][k]), lead(big[2][k]), lead(big[3][k]), g_n2[k], lead(big[4][k]), lead(big[5][k]))

    return (loss, grad_x[None]) + ordered(0) + ordered(1) + ordered(2) + ordered(3)
```

```python
import jax
import jax.numpy as jnp
from jax import lax
from jax.experimental import pallas as pl
from jax.experimental.pallas import tpu as pltpu

F32 = jnp.float32
BF16 = jnp.bfloat16
HIGHEST = lax.Precision.HIGHEST
MESH = pl.DeviceIdType.MESH

D_MODEL = 2048
A_WIDTH = 1024
A_HEADS = 8
A_HEAD_DIM = 128
A_CHUNK = 64
B_WIDTH = 1024
B_HEAD_DIM = 64
B_GROUP = 4
B_KV_HEADS = 4
B_KV_WIDTH = 256
BLOCK = 128
MLP_HIDDEN = 8192
IN_WIDTH = 9728
N_MOD = 6
EPS = 1e-6
N_CHIPS = 4
N_DEV = 8

OFF_QA, OFF_FA, OFF_IA, OFF_GA = 0, 1024, 2048, 3072
OFF_QB, OFF_KB, OFF_VB = 4096, 5120, 5376
OFF_GATE_A, OFF_GATE_B = 5632, 7680

ADAM_LR = 0.001
ADAM_B1 = 0.9
ADAM_B2 = 0.999
ADAM_EPS = 1e-08
ADAM_WD = 0.01
ADAM_STEP = 10

VMEM_LIMIT_V7X = 48 * 1024 * 1024
NEG_BIG = -1e30


def _params(sem=None, vmem=VMEM_LIMIT_V7X):
    return pltpu.CompilerParams(dimension_semantics=sem, vmem_limit_bytes=vmem)


class _Plan:
    def __init__(self, ins, outs, sems, stages, aliases=None, mid_at=()):
        self.ins, self.outs, self.sems, self.stages, self.aliases = ins, outs, sems, stages, aliases or {}
        self.mid_at = tuple(mid_at)
        assert len(self.mid_at) == len(stages) - 2


def _join(a, b):
    assert len(a.stages) == 2 and len(b.stages) == 2
    ni, no, ns = len(a.ins), len(a.outs), len(a.sems)

    def stage(k):
        def run(pi, po, ps):
            a.stages[k](pi[:ni], po[:no], ps[:ns])
            b.stages[k](pi[ni:], po[no:], ps[ns:])
        return run

    aliases = dict(a.aliases)
    aliases.update({ni + i: no + o for i, o in b.aliases.items()})
    return _Plan(a.ins + b.ins, a.outs + b.outs, a.sems + b.sems, [stage(0), stage(1)], aliases)


def _pcall(body, plan=None, **kw):
    if plan is None:
        return pl.pallas_call(body, **kw)
    grid = kw["grid"]
    single = not isinstance(kw["out_specs"], (list, tuple))
    in_specs = list(kw["in_specs"])
    out_specs = [kw["out_specs"]] if single else list(kw["out_specs"])
    out_shape = [kw["out_shape"]] if single else list(kw["out_shape"])
    scratch = list(kw.get("scratch_shapes", ()))
    n_in, n_out, n_scr = len(in_specs), len(out_specs), len(scratch)
    n_pi, n_po = len(plan.ins), len(plan.outs)
    total = 1
    for g in grid:
        total *= g
    n_st = len(plan.stages)

    def wrapped(*refs):
        o0 = n_in + n_pi
        s0 = o0 + n_out + n_po
        pi, po, ps = refs[n_in:o0], refs[o0 + n_out:s0], refs[s0 + n_scr:]
        lin = 0
        for d, g in enumerate(grid):
            lin = lin * g + pl.program_id(d)
        for si, frac in enumerate((0.0,) + plan.mid_at):
            @pl.when(lin == int(frac * (total - 1)))
            def _(si=si):
                plan.stages[si](pi, po, ps)
        body(*refs[:n_in], *refs[o0:o0 + n_out], *refs[s0:s0 + n_scr])

        @pl.when(lin == total - 1)
        def _():
            plan.stages[-1](pi, po, ps)

    any_spec = pl.BlockSpec(memory_space=pl.ANY)
    call = pl.pallas_call(
        wrapped, name=kw["name"], grid=grid, in_specs=in_specs + [any_spec] * n_pi,
        out_specs=out_specs + [any_spec] * n_po, out_shape=out_shape + list(plan.outs),
        scratch_shapes=scratch + list(plan.sems),
        input_output_aliases={n_in + i: n_out + o for i, o in plan.aliases.items()},
        compiler_params=_params(("arbitrary",) * len(grid)))

    def run(*args):
        res = call(*args, *plan.ins)
        outs = list(res[:n_out])
        return (outs[0] if single else outs), list(res[n_out:])

    return run


def _run_plan(plan, name):
    return _pcall(lambda: None, plan=plan, name=name, grid=(1,), in_specs=[], out_specs=[], out_shape=[])()[1]


def _sig(x):
    return 1.0 / (1.0 + jnp.exp(-x))


def _nn(a, b):
    return lax.dot_general(a.astype(BF16), b.astype(BF16), (((1,), (0,)), ((), ())), preferred_element_type=F32)


def _nt(a, b):
    return lax.dot_general(a.astype(BF16), b.astype(BF16), (((1,), (1,)), ((), ())), preferred_element_type=F32)


def _tn(a, b):
    return lax.dot_general(a.astype(BF16), b.astype(BF16), (((0,), (0,)), ((), ())), preferred_element_type=F32)


def _mm(a, b, *, name, ta=False, tb=False, bm=1024, bn=1024, bk=2048, out_dtypes=(F32,), epi=None, extras=(),
        extra_cols=None, plan=None, a_blocks=None, row_extras=(), n_stats=0):
    if ta:
        K, M = a.shape
        bk = K
        if a_blocks is not None:
            M = a_blocks[0] * bm
    else:
        M, K = a.shape
    if tb:
        N, K2 = b.shape
    else:
        K2, N = b.shape
    bm, bn, bk = min(bm, M), min(bn, N), min(bk, K)
    assert K == K2 and M % bm == 0 and N % bn == 0 and K % bk == 0, (name, a.shape, b.shape)
    nk = K // bk
    a_col = a_blocks[1] if a_blocks is not None else (lambda i: i)
    a_spec = pl.BlockSpec((bk, bm), lambda i, j, k: (k, a_col(i))) if ta else pl.BlockSpec((bm, bk), lambda i, j, k: (i, k))
    b_spec = pl.BlockSpec((bn, bk), lambda i, j, k: (j, k)) if tb else pl.BlockSpec((bk, bn), lambda i, j, k: (k, j))
    t_spec = pl.BlockSpec((bm, bn), lambda i, j, k: (i, j))
    extra_cols = extra_cols or (0,) * len(extras)
    e_specs = [pl.BlockSpec((bm, bn), lambda i, j, k, off=off: (i, off + j)) for off in extra_cols]
    e_specs += [pl.BlockSpec((8, bn), lambda i, j, k: (0, j)) for _ in row_extras]
    dims = (((1,), (1 if tb else 0,)), ((), ()))
    n_e, n_o = len(extras) + len(row_extras), len(out_dtypes)
    stat_spec = pl.BlockSpec((8, bn), lambda i, j, k: (i, j))

    def body(*refs):
        a_ref, b_ref = refs[0], refs[1]
        e_refs = refs[2:2 + n_e]
        o_refs = refs[2 + n_e:2 + n_e + n_o]

        def finish(acc):
            outs = (acc,) if epi is None else epi(acc, *[e[...] for e in e_refs])
            for o_ref, o in zip(o_refs, outs):
                o_ref[...] = o.astype(o_ref.dtype)

        if ta:
            at_ref = refs[-1]

            @pl.when(pl.program_id(1) == 0)
            def _():
                at_ref[...] = a_ref[...].T

            lhs = at_ref[...]
        else:
            lhs = a_ref[...].astype(BF16)
        part = lax.dot_general(lhs, b_ref[...].astype(BF16), dims, preferred_element_type=F32)
        if nk == 1:
            finish(part)
        else:
            acc_ref = refs[-1]
            k = pl.program_id(2)

            @pl.when(k == 0)
            def _():
                acc_ref[...] = part

            @pl.when(k > 0)
            def _():
                acc_ref[...] += part

            @pl.when(k == nk - 1)
            def _():
                finish(acc_ref[...])

    if ta:
        assert a.dtype == BF16 and nk == 1
        scratch = [pltpu.VMEM((bm, bk), BF16)]
    else:
        scratch = [pltpu.VMEM((bm, bn), F32)] if nk > 1 else []
    out = _pcall(
        body, plan=plan, name=name, grid=(M // bm, N // bn, nk),
        in_specs=[a_spec, b_spec] + e_specs,
        out_specs=[t_spec] * (n_o - n_stats) + [stat_spec] * n_stats,
        out_shape=[jax.ShapeDtypeStruct((M, N), dt) for dt in out_dtypes[:n_o - n_stats]]
        + [jax.ShapeDtypeStruct((8 * (M // bm), N), F32)] * n_stats,
        scratch_shapes=scratch,
        compiler_params=_params(("parallel", "arbitrary", "arbitrary")),
    )(a, b, *extras, *row_extras)
    if plan is not None:
        return (out[0][0] if n_o == 1 else out[0]), out[1]
    return out[0] if n_o == 1 else out


def _ada_fwd(c_all, w_ada, b_cols):
    n = w_ada.shape[1]
    bn = 512

    def body(c_ref, w_ref, b_ref, o_ref):
        cv = c_ref[...]
        sc = cv * _sig(cv)
        o_ref[...] = jnp.dot(sc, w_ref[...], precision=HIGHEST, preferred_element_type=F32) + b_ref[...]

    return _pcall(
        body, name="ada_fwd", grid=(n // bn,),
        in_specs=[pl.BlockSpec((N_DEV, D_MODEL), lambda j: (0, 0)), pl.BlockSpec((D_MODEL, bn), lambda j: (0, j)),
                  pl.BlockSpec((1, bn), lambda j: (0, j))],
        out_specs=pl.BlockSpec((N_DEV, bn), lambda j: (0, j)),
        out_shape=jax.ShapeDtypeStruct((N_DEV, n), F32),
        compiler_params=_params(("parallel",)),
    )(c_all, w_ada, b_cols)


ROWS_EW = 256


def _rms_fwd_math(x, gain, scale, shift):
    rstd = lax.rsqrt(jnp.mean(x * x, axis=-1, keepdims=True) + EPS)
    xhat = x * rstd
    n = xhat * gain
    return n * (1.0 + scale) + shift, xhat, n, rstd


def _rms_bwd_math(dh, xhat, n, rstd, gain, scale):
    dn = dh * (1.0 + scale)
    dxhat = dn * gain
    dx = rstd * (dxhat - xhat * jnp.mean(dxhat * xhat, axis=-1, keepdims=True))
    d_scale = jnp.sum(dh * n, axis=0, keepdims=True)
    d_shift = jnp.sum(dh, axis=0, keepdims=True)
    d_gain = jnp.sum(dn * xhat, axis=0, keepdims=True)
    return dx, d_scale, d_shift, d_gain


def _row_spec(w=D_MODEL, br=ROWS_EW):
    return pl.BlockSpec((br, w), lambda i: (i, 0))


def _vec_spec(r=8, w=D_MODEL):
    return pl.BlockSpec((r, w), lambda i: (0, 0))


def _norm1_fwd(x, gain, mod8, plan=None):
    T = x.shape[0]

    def body(x_ref, g_ref, m_ref, h_ref):
        h, _, _, _ = _rms_fwd_math(x_ref[...], g_ref[...], m_ref[1:2, :], m_ref[0:1, :])
        h_ref[...] = h.astype(BF16)

    return _pcall(
        body, plan=plan, name="norm1_fwd", grid=(T // ROWS_EW,),
        in_specs=[_row_spec(), _vec_spec(1), _vec_spec()],
        out_specs=_row_spec(), out_shape=jax.ShapeDtypeStruct((T, D_MODEL), BF16),
        compiler_params=_params(("parallel",)),
    )(x, gain, mod8)


def _res_norm2_fwd(x, mo, gain, mod8):
    T = x.shape[0]
    br = ROWS_EW

    def body(x_ref, mo_ref, g_ref, m_ref, x1_ref, h_ref):
        x1 = x_ref[...] + m_ref[2:3, :] * mo_ref[...]
        x1_ref[...] = x1
        h, _, _, _ = _rms_fwd_math(x1, g_ref[...], m_ref[4:5, :], m_ref[3:4, :])
        h_ref[...] = h.astype(BF16)

    return _pcall(
        body, name="res_norm2_fwd", grid=(T // br,),
        in_specs=[_row_spec(br=br), _row_spec(br=br), _vec_spec(1), _vec_spec()],
        out_specs=[_row_spec(br=br), _row_spec(br=br)],
        out_shape=[jax.ShapeDtypeStruct((T, D_MODEL), F32), jax.ShapeDtypeStruct((T, D_MODEL), BF16)],
        compiler_params=_params(("parallel",)),
    )(x, mo, gain, mod8)


def _loss_head(mlp, x1, target, mod):
    gate = mod[5:6, :]
    err = x1 + gate * mlp - target
    dy = err * (1.0 / D_MODEL)
    row = lax.broadcasted_iota(jnp.int32, (8, mlp.shape[1]), 0)
    stats = jnp.where(row == 0, jnp.sum(err * err, axis=0, keepdims=True),
                      jnp.where(row == 1, jnp.sum(dy * mlp, axis=0, keepdims=True), 0.0))
    return dy, dy * gate, stats


def _norm2_bwd(dh2, x1, dy, mo, gain, mod8):
    T = x1.shape[0]

    def body(dh_ref, x1_ref, dy_ref, mo_ref, g_ref, m_ref, dx1_ref, dmo_ref, st_ref):
        i = pl.program_id(0)
        gain_v, scale = g_ref[...], m_ref[4:5, :]
        _, xhat, n, rstd = _rms_fwd_math(x1_ref[...], gain_v, scale, m_ref[3:4, :])
        dx, d_scale, d_shift, d_gain = _rms_bwd_math(dh_ref[...], xhat, n, rstd, gain_v, scale)
        dx1 = dy_ref[...] + dx
        dx1_ref[...] = dx1
        dmo_ref[...] = (dx1 * m_ref[2:3, :]).astype(BF16)

        @pl.when(i == 0)
        def _():
            st_ref[...] = jnp.zeros_like(st_ref)

        st_ref[0:1, :] += d_scale
        st_ref[1:2, :] += d_shift
        st_ref[2:3, :] += d_gain
        st_ref[3:4, :] += jnp.sum(dx1 * mo_ref[...], axis=0, keepdims=True)

    return _pcall(
        body, name="norm2_bwd", grid=(T // ROWS_EW,),
        in_specs=[_row_spec(), _row_spec(), _row_spec(), _row_spec(), _vec_spec(1), _vec_spec()],
        out_specs=[_row_spec(), _row_spec(), _vec_spec()],
        out_shape=[jax.ShapeDtypeStruct((T, D_MODEL), F32), jax.ShapeDtypeStruct((T, D_MODEL), BF16),
                   jax.ShapeDtypeStruct((8, D_MODEL), F32)],
        compiler_params=_params(("arbitrary",)),
    )(dh2, x1, dy, mo, gain, mod8)


def _norm1_bwd(dh, x, dx1, gain, mod8):
    T = x.shape[0]
    br = ROWS_EW

    def body(dh_ref, x_ref, dx1_ref, g_ref, m_ref, dx_ref, st_ref):
        i = pl.program_id(0)
        gain_v, scale = g_ref[...], m_ref[1:2, :]
        _, xhat, n, rstd = _rms_fwd_math(x_ref[...], gain_v, scale, m_ref[0:1, :])
        dx, d_scale, d_shift, d_gain = _rms_bwd_math(dh_ref[...], xhat, n, rstd, gain_v, scale)
        dx_ref[...] = dx1_ref[...] + dx

        @pl.when(i == 0)
        def _():
            st_ref[...] = jnp.zeros_like(st_ref)

        st_ref[0:1, :] += d_scale
        st_ref[1:2, :] += d_shift
        st_ref[2:3, :] += d_gain

    return _pcall(
        body, name="norm1_bwd", grid=(T // br,),
        in_specs=[_row_spec(br=br), _row_spec(br=br), _row_spec(br=br), _vec_spec(1), _vec_spec()],
        out_specs=[_row_spec(br=br), _vec_spec()],
        out_shape=[jax.ShapeDtypeStruct((T, D_MODEL), F32), jax.ShapeDtypeStruct((8, D_MODEL), F32)],
        compiler_params=_params(("arbitrary",)),
    )(dh, x, dx1, gain, mod8)


MERGE_BC = 512


def _hgrn_rows(T):
    return 512 if T >= 1024 else 128


def _lower_bound(lbl):
    e = jnp.exp(lbl - jnp.max(lbl, axis=0, keepdims=True))
    return e[0:1, :] / (e[0:1, :] + e[1:2, :])


def _chunk_sum_matrix(rows, backward):
    shift = A_CHUNK.bit_length() - 1
    r = lax.broadcasted_iota(jnp.int32, (rows, rows), 0)
    c = lax.broadcasted_iota(jnp.int32, (rows, rows), 1)
    same = jnp.right_shift(r, shift) == jnp.right_shift(c, shift)
    return (same & ((r <= c) if backward else (r >= c))).astype(BF16)


def _chunk_sums(m, x):
    n = x.shape[1]
    hi = x.astype(BF16)
    rest = x - hi.astype(F32)
    mid = rest.astype(BF16)
    lo = (rest - mid.astype(F32)).astype(BF16)
    y = jnp.dot(m, jnp.concatenate([hi, mid, lo], axis=1), preferred_element_type=F32)
    return y[:, 0:n] + y[:, n:2 * n] + y[:, 2 * n:3 * n]


def _hgrn_block_pre(q, fl, lb, m_fwd):
    sg = _sig(fl)
    f = lb + (1.0 - lb) * sg
    sq = _sig(q)
    return dict(sg=sg, f=f, k=1.0 - f, sq=sq, qf=q * sq, b=_chunk_sums(m_fwd, jnp.log(f)))


def _hgrn_chunk_local(pre, r):
    C = A_CHUNK
    qf, k, b = pre["qf"][r], pre["k"][r], pre["b"][r]
    causal = lax.broadcasted_iota(jnp.int32, (C, C), 0) >= lax.broadcasted_iota(jnp.int32, (C, C), 1)
    bm = b[C // 2 - 1:C // 2, :]
    bl = b[C - 1:C, :]
    e_q, e_k = jnp.exp(b - bm), jnp.exp(bm - b)
    e_b, e_l = jnp.exp(b), jnp.exp(bl - b)
    qd, kd = qf * e_q, k * e_k
    qe, ke = qf * e_b, k * e_l
    att = jnp.where(causal, _nt(qd, kd), 0.0)
    return dict(causal=causal, e_q=e_q, e_k=e_k, e_b=e_b, e_l=e_l, qd=qd, kd=kd, qe=qe, ke=ke, att=att, dec=jnp.exp(bl))


def _hgrn_chunk_fwd(pre, r, v, st):
    c = _hgrn_chunk_local(pre, r)
    c["o"] = _nn(c["att"], v) + _nt(c["qe"], st)
    return c


def _lockstep(gens):
    out = [None] * len(gens)
    live = list(enumerate(gens))
    while live:
        still = []
        for i, g in live:
            try:
                next(g)
                still.append((i, g))
            except StopIteration as done:
                out[i] = done.value
        live = still
    return out


HGRN_HEADS_PER_STEP = 4


def _hgrn_fwd(proj, lb_logits, o_gain, plan=None):
    T = proj.shape[0]
    BR = _hgrn_rows(T)
    cps = BR // A_CHUNK
    K, NH = A_HEAD_DIM, HGRN_HEADS_PER_STEP
    W = NH * K

    def col(off):
        return pl.BlockSpec((BR, W), lambda h, cb: (cb, off // W + h))

    def body(q_ref, f_ref, i_ref, g_ref, lbl_ref, og_ref, o_ref, s_ref, st):
        @pl.when(pl.program_id(1) == 0)
        def _():
            st[...] = jnp.zeros_like(st)

        lb_all = _lower_bound(lbl_ref[...])
        m_fwd = _chunk_sum_matrix(BR, False)
        pre = [_hgrn_block_pre(q_ref[:, n * K:(n + 1) * K], f_ref[:, n * K:(n + 1) * K], lb_all[:, n * K:(n + 1) * K], m_fwd)
               for n in range(NH)]
        def local(n, ci):
            r, hs = slice(ci * A_CHUNK, (ci + 1) * A_CHUNK), slice(n * K, (n + 1) * K)
            v = i_ref[r, hs]
            c = _hgrn_chunk_local(pre[n], r)
            yield
            return dict(o=_nn(c["att"], v), ds=_tn(v, c["ke"]), qe=c["qe"], dec=c["dec"])

        def chain(n, loc):
            hs = slice(n * K, (n + 1) * K)
            state = st[n]
            for ci, p in enumerate(loc):
                r = slice(ci * A_CHUNK, (ci + 1) * A_CHUNK)
                s_ref[n, ci] = state
                o = p["o"] + _nt(p["qe"], state)
                state = state * p["dec"] + p["ds"]
                yield
                on = o * lax.rsqrt(jnp.mean(o * o, axis=-1, keepdims=True) + EPS)
                g = g_ref[r, hs]
                o_ref[r, hs] = (on * og_ref[:, hs] * (g * _sig(g))).astype(BF16)
            st[n] = state

        loc = _lockstep([local(n, ci) for n in range(NH) for ci in range(cps)])
        _lockstep([chain(n, loc[n * cps:(n + 1) * cps]) for n in range(NH)])

    return _pcall(
        body, plan=plan, name="hgrn_fwd", grid=(A_HEADS // NH, T // BR),
        in_specs=[col(OFF_QA), col(OFF_FA), col(OFF_IA), col(OFF_GA),
                  pl.BlockSpec((2, W), lambda h, cb: (0, h)), pl.BlockSpec((1, W), lambda h, cb: (0, h))],
        out_specs=[pl.BlockSpec((BR, W), lambda h, cb: (cb, h)),
                   pl.BlockSpec((NH, cps, K, K), lambda h, cb: (h, cb, 0, 0))],
        out_shape=[jax.ShapeDtypeStruct((T, A_WIDTH), BF16),
                   jax.ShapeDtypeStruct((A_HEADS, T // A_CHUNK, K, K), F32)],
        scratch_shapes=[pltpu.VMEM((NH, K, K), F32)],
        compiler_params=_params(("parallel", "arbitrary")),
    )(proj, proj, proj, proj, lb_logits, o_gain)


def _hgrn_bwd(proj, lb_logits, o_gain, states, do, plan=None):
    T = proj.shape[0]
    BR = _hgrn_rows(T)
    cps = BR // A_CHUNK
    ncb = T // BR
    K, C, NH = A_HEAD_DIM, A_CHUNK, HGRN_HEADS_PER_STEP
    W = NH * K

    def col(off):
        return pl.BlockSpec((BR, W), lambda h, cb: (ncb - 1 - cb, off // W + h))

    def body(q_ref, f_ref, i_ref, g_ref, lbl_ref, og_ref, s_ref, do_ref,
             dq_ref, df_ref, di_ref, dg_ref, dlb_ref, dog_ref, dst):
        @pl.when(pl.program_id(1) == 0)
        def _():
            dst[...] = jnp.zeros_like(dst)
            dlb_ref[...] = jnp.zeros_like(dlb_ref)
            dog_ref[...] = jnp.zeros_like(dog_ref)

        lb_all = _lower_bound(lbl_ref[...])
        row = lax.broadcasted_iota(jnp.int32, (C, K), 0)
        m_fwd, m_bwd = _chunk_sum_matrix(BR, False), _chunk_sum_matrix(BR, True)
        pre = [_hgrn_block_pre(q_ref[:, n * K:(n + 1) * K], f_ref[:, n * K:(n + 1) * K], lb_all[:, n * K:(n + 1) * K], m_fwd)
               for n in range(NH)]
        def local(n, ci):
            r, hs = slice(ci * C, (ci + 1) * C), slice(n * K, (n + 1) * K)
            gain = og_ref[:, hs]
            st = s_ref[n, ci]
            v = i_ref[r, hs]
            q = q_ref[r, hs]
            c = _hgrn_chunk_fwd(pre[n], r, v, st)
            yield
            o = c["o"]
            rn = lax.rsqrt(jnp.mean(o * o, axis=-1, keepdims=True) + EPS)
            on = o * rn
            g = g_ref[r, hs]
            sgg = _sig(g)
            dy = do_ref[r, hs]
            d_ong = dy * (g * sgg)
            dg_ref[r, hs] = (dy * (on * gain) * (sgg * (1.0 + g * (1.0 - sgg)))).astype(BF16)
            d_on = d_ong * gain
            d_o = rn * (d_on - on * jnp.mean(d_on * on, axis=-1, keepdims=True))
            datt = jnp.where(c["causal"], _nt(d_o, v), 0.0)
            dqe = _nn(d_o, st)
            yield
            dqd = _nn(datt, c["kd"])
            dkd = _tn(datt, c["qd"])
            dv = _tn(c["att"], d_o)
            ds = _tn(d_o, c["qe"])
            yield
            t_q, t_k = dqd * c["qd"], dkd * c["kd"]
            sq = pre[n]["sq"][r]
            dq_ref[r, hs] = ((dqd * c["e_q"] + dqe * c["e_b"]) * (sq * (1.0 + q * (1.0 - sq)))).astype(BF16)
            return dict(v=v, st=st, ke=c["ke"], e_l=c["e_l"], dec=c["dec"], dv=dv, ds=ds, dk=dkd * c["e_k"],
                        db=t_q - t_k + dqe * c["qe"], dbm=jnp.sum(t_k - t_q, axis=0, keepdims=True),
                        d_og=jnp.sum(d_ong * on, axis=0, keepdims=True))

        def chain(n, loc):
            hs = slice(n * K, (n + 1) * K)
            dst_next = dst[n]
            db_of, dk_of = [None] * cps, [None] * cps
            for ci in reversed(range(cps)):
                p = loc[ci]
                di_ref[ci * C:(ci + 1) * C, hs] = (p["dv"] + _nt(p["ke"], dst_next)).astype(BF16)
                dke = _nn(p["v"], dst_next)
                yield
                t_l = dke * p["ke"]
                dbl = jnp.sum(t_l, axis=0, keepdims=True) + jnp.sum(dst_next * p["st"], axis=0, keepdims=True) * p["dec"]
                db_of[ci] = p["db"] - t_l + jnp.where(row == C // 2 - 1, p["dbm"], 0.0) + jnp.where(row == C - 1, dbl, 0.0)
                dk_of[ci] = p["dk"] + dke * p["e_l"]
                dst_next = dst_next * p["dec"] + p["ds"]
            dst[n] = dst_next
            return db_of, dk_of

        loc = _lockstep([local(n, ci) for n in range(NH) for ci in range(cps)])
        loc = [loc[n * cps:(n + 1) * cps] for n in range(NH)]
        chains = _lockstep([chain(n, loc[n]) for n in range(NH)])
        for n in range(NH):
            hs = slice(n * K, (n + 1) * K)
            db_of, dk_of = chains[n]
            d_og = loc[n][0]["d_og"]
            for p in loc[n][1:]:
                d_og = d_og + p["d_og"]
            dog_ref[0:1, hs] += d_og
            lb, sg = lb_all[:, hs], pre[n]["sg"]
            dlf = _chunk_sums(m_bwd, jnp.concatenate(db_of, axis=0))
            df = dlf / pre[n]["f"] - jnp.concatenate(dk_of, axis=0)
            df_ref[:, hs] = (df * (1.0 - lb) * sg * (1.0 - sg)).astype(BF16)
            dlb_ref[0:1, hs] += jnp.sum(df * (1.0 - sg), axis=0, keepdims=True)

    ocol = pl.BlockSpec((BR, W), lambda h, cb: (ncb - 1 - cb, h))
    vec = pl.BlockSpec((8, W), lambda h, cb: (0, h))
    return _pcall(
        body, plan=plan, name="hgrn_bwd", grid=(A_HEADS // NH, ncb),
        in_specs=[col(OFF_QA), col(OFF_FA), col(OFF_IA), col(OFF_GA),
                  pl.BlockSpec((2, W), lambda h, cb: (0, h)), pl.BlockSpec((1, W), lambda h, cb: (0, h)),
                  pl.BlockSpec((NH, cps, K, K), lambda h, cb: (h, ncb - 1 - cb, 0, 0)),
                  pl.BlockSpec((BR, W), lambda h, cb: (ncb - 1 - cb, h))],
        out_specs=[ocol, ocol, ocol, ocol, vec, vec],
        out_shape=[jax.ShapeDtypeStruct((T, A_WIDTH), BF16)] * 4 + [jax.ShapeDtypeStruct((8, A_WIDTH), F32)] * 2,
        scratch_shapes=[pltpu.VMEM((NH, K, K), F32)],
        compiler_params=_params(("parallel", "arbitrary")),
    )(proj, proj, proj, proj, lb_logits, o_gain, states, do)


def _head_norm(x):
    r = lax.rsqrt(jnp.mean(x * x, axis=-1, keepdims=True) + EPS)
    return x * r, r


def _head_norm_bwd(dy, xn, r, gain):
    dxn = dy * gain
    return r * (dxn - xn * jnp.mean(dxn * xn, axis=-1, keepdims=True)), jnp.sum(dy * xn, axis=0, keepdims=True)


def _swa_mask(has_prev):
    rows = B_GROUP * BLOCK
    r = lax.broadcasted_iota(jnp.int32, (rows, 2 * BLOCK), 0) % BLOCK
    c = lax.broadcasted_iota(jnp.int32, (rows, 2 * BLOCK), 1)
    rel = r + BLOCK - c
    return (rel >= 0) & (rel < BLOCK) & ((c >= BLOCK) | has_prev)


def _swa_head_fwd(j, q_ref, kp_ref, kc_ref, vp_ref, vc_ref, qg, kg, sk_ref, mask):
    hs = slice(j * B_HEAD_DIM, (j + 1) * B_HEAD_DIM)
    kcat = jnp.concatenate([kp_ref[:, hs], kc_ref[:, hs]], axis=0)
    vcat = jnp.concatenate([vp_ref[:, hs], vc_ref[:, hs]], axis=0)
    qs = jnp.concatenate([q_ref[:, pl.ds((j * B_GROUP + g) * B_HEAD_DIM, B_HEAD_DIM)] for g in range(B_GROUP)], axis=0)
    kn, kr = _head_norm(kcat)
    qn, qr = _head_norm(qs)
    kh, qh = kn * kg, qn * qg
    yield
    s = jnp.where(mask, _nt(qh, kh) * (B_HEAD_DIM ** -0.5), NEG_BIG)
    yield
    sink = jnp.concatenate(
        [jnp.broadcast_to(sk_ref[0:1, pl.ds(j * B_GROUP + g, 1)], (BLOCK, 1)) for g in range(B_GROUP)], axis=0)
    m = jnp.maximum(jnp.max(s, axis=-1, keepdims=True), sink)
    p = jnp.exp(s - m)
    e_sink = jnp.exp(sink - m)
    inv = 1.0 / (jnp.sum(p, axis=-1, keepdims=True) + e_sink)
    prob = p * inv
    return dict(vcat=vcat, kn=kn, kr=kr, qn=qn, qr=qr, kh=kh, qh=qh, prob=prob, p_sink=e_sink * inv)


def _swa_in_specs(nb, last):
    def qi(n):
        return jnp.minimum(n, last)

    q = pl.BlockSpec((BLOCK, B_WIDTH), lambda n: (qi(n), OFF_QB // B_WIDTH))
    kc = pl.BlockSpec((BLOCK, B_KV_WIDTH), lambda n: (qi(n), OFF_KB // B_KV_WIDTH))
    kp = pl.BlockSpec((BLOCK, B_KV_WIDTH), lambda n: (jnp.maximum(qi(n) - 1, 0), OFF_KB // B_KV_WIDTH))
    vc = pl.BlockSpec((BLOCK, B_KV_WIDTH), lambda n: (qi(n), OFF_VB // B_KV_WIDTH))
    vp = pl.BlockSpec((BLOCK, B_KV_WIDTH), lambda n: (jnp.maximum(qi(n) - 1, 0), OFF_VB // B_KV_WIDTH))
    small = [pl.BlockSpec((1, B_HEAD_DIM), lambda n: (0, 0)), pl.BlockSpec((1, B_HEAD_DIM), lambda n: (0, 0)),
             pl.BlockSpec((1, B_GROUP * B_KV_HEADS), lambda n: (0, 0))]
    return [q, kp, kc, vp, vc] + small


def _swa_fwd(proj, q_gain, k_gain, sinks, plan=None):
    T = proj.shape[0]
    nb = T // BLOCK

    def body(q_ref, kp_ref, kc_ref, vp_ref, vc_ref, qg_ref, kg_ref, sk_ref, o_ref):
        mask = _swa_mask(pl.program_id(0) > 0)

        def head(j):
            c = yield from _swa_head_fwd(j, q_ref, kp_ref, kc_ref, vp_ref, vc_ref, qg_ref[...], kg_ref[...], sk_ref, mask)
            yield
            o = _nn(c["prob"], c["vcat"])
            yield
            for g in range(B_GROUP):
                o_ref[:, pl.ds((j * B_GROUP + g) * B_HEAD_DIM, B_HEAD_DIM)] = o[g * BLOCK:(g + 1) * BLOCK].astype(BF16)

        _lockstep([head(j) for j in range(B_KV_HEADS)])

    return _pcall(
        body, plan=plan, name="swa_fwd", grid=(nb,),
        in_specs=_swa_in_specs(nb, nb - 1),
        out_specs=pl.BlockSpec((BLOCK, B_WIDTH), lambda n: (n, 0)),
        out_shape=jax.ShapeDtypeStruct((T, B_WIDTH), BF16),
        compiler_params=_params(("parallel",)),
    )(proj, proj, proj, proj, proj, q_gain, k_gain, sinks)


def _swa_bwd(proj, q_gain, k_gain, sinks, do, plan=None):
    T = proj.shape[0]
    nb = T // BLOCK
    scale = B_HEAD_DIM ** -0.5

    def body(q_ref, kp_ref, kc_ref, vp_ref, vc_ref, qg_ref, kg_ref, sk_ref, do_ref,
             dq_ref, dkv_ref, sm_ref, ck, cv):
        n = pl.program_id(0)

        @pl.when(n == 0)
        def _():
            ck[...] = jnp.zeros_like(ck)
            cv[...] = jnp.zeros_like(cv)
            sm_ref[...] = jnp.zeros_like(sm_ref)

        @pl.when(n < nb)
        def _():
            mask = _swa_mask(n > 0)
            qg, kg = qg_ref[...], kg_ref[...]
            lane = lax.broadcasted_iota(jnp.int32, (1, BLOCK), 1)
            def head(j):
                hs = slice(j * B_HEAD_DIM, (j + 1) * B_HEAD_DIM)
                vs = slice(B_KV_WIDTH + j * B_HEAD_DIM, B_KV_WIDTH + (j + 1) * B_HEAD_DIM)
                c = yield from _swa_head_fwd(j, q_ref, kp_ref, kc_ref, vp_ref, vc_ref, qg, kg, sk_ref, mask)
                d_out = jnp.concatenate(
                    [do_ref[:, pl.ds((j * B_GROUP + g) * B_HEAD_DIM, B_HEAD_DIM)] for g in range(B_GROUP)], axis=0)
                prob = c["prob"]
                yield
                out = _nn(prob, c["vcat"])
                d_prob = _nt(d_out, c["vcat"])
                dv = _tn(prob, d_out)
                yield
                delta = jnp.sum(d_out * out, axis=-1, keepdims=True)
                ds = prob * (d_prob - delta)
                d_sink = -c["p_sink"] * delta
                yield
                dqh = _nn(ds, c["kh"]) * scale
                dkh = _tn(ds, c["qh"]) * scale
                yield
                dq, dqg = _head_norm_bwd(dqh, c["qn"], c["qr"], qg)
                dk, dkg = _head_norm_bwd(dkh, c["kn"], c["kr"], kg)
                d_sinks = jnp.zeros((1, BLOCK), F32)
                for g in range(B_GROUP):
                    dq_ref[:, pl.ds((j * B_GROUP + g) * B_HEAD_DIM, B_HEAD_DIM)] = dq[g * BLOCK:(g + 1) * BLOCK].astype(BF16)
                    tot = jnp.sum(d_sink[g * BLOCK:(g + 1) * BLOCK], axis=0, keepdims=True)
                    d_sinks = d_sinks + jnp.where(lane == j * B_GROUP + g, tot, 0.0)
                dkv_ref[:, hs] = (ck[:, hs] + dk[0:BLOCK]).astype(BF16)
                dkv_ref[:, vs] = (cv[:, hs] + dv[0:BLOCK]).astype(BF16)
                ck[:, hs] = dk[BLOCK:2 * BLOCK]
                cv[:, hs] = dv[BLOCK:2 * BLOCK]
                return dqg, dkg, d_sinks

            small = _lockstep([head(j) for j in range(B_KV_HEADS)])
            sm_ref[0:1, 0:B_HEAD_DIM] += small[0][0] + small[1][0] + small[2][0] + small[3][0]
            sm_ref[1:2, 0:B_HEAD_DIM] += small[0][1] + small[1][1] + small[2][1] + small[3][1]
            sm_ref[2:3, :] += small[0][2] + small[1][2] + small[2][2] + small[3][2]

        @pl.when(n == nb)
        def _():
            dkv_ref[:, 0:B_KV_WIDTH] = ck[...].astype(BF16)
            dkv_ref[:, B_KV_WIDTH:2 * B_KV_WIDTH] = cv[...].astype(BF16)

    return _pcall(
        body, plan=plan, name="swa_bwd", grid=(nb + 1,),
        in_specs=_swa_in_specs(nb, nb - 1) + [pl.BlockSpec((BLOCK, B_WIDTH), lambda n: (jnp.minimum(n, nb - 1), 0))],
        out_specs=[pl.BlockSpec((BLOCK, B_WIDTH), lambda n: (jnp.minimum(n, nb - 1), 0)),
                   pl.BlockSpec((BLOCK, 2 * B_KV_WIDTH), lambda n: (jnp.maximum(n - 1, 0), 0)),
                   pl.BlockSpec((8, BLOCK), lambda n: (0, 0))],
        out_shape=[jax.ShapeDtypeStruct((T, B_WIDTH), BF16), jax.ShapeDtypeStruct((T, 2 * B_KV_WIDTH), BF16),
                   jax.ShapeDtypeStruct((8, BLOCK), F32)],
        scratch_shapes=[pltpu.VMEM((BLOCK, B_KV_WIDTH), F32), pltpu.VMEM((BLOCK, B_KV_WIDTH), F32)],
        compiler_params=_params(("arbitrary",)),
    )(proj, proj, proj, proj, proj, q_gain, k_gain, sinks, do)


W_IN, W_A, W_B, W_OUT, W_MI, W_MO = range(6)


def _local_step(x, target, mod8, norm1_gain, norm2_gain, lb_logits, o_gain, q_gain, k_gain, sinks, shards, c_arr, chip_arr):
    relu2 = lambda u: (u, jnp.square(jnp.maximum(u, 0.0)))
    pair, half = {}, {}

    def exchange(ws, grads):
        return _sibling_exchange_plan([_grad_view(g, w) for w, g in zip(ws, grads)])

    def pair_sums(ws, grads, others):
        for w, g, o in zip(ws, grads, others):
            pair[w] = _pair_sum(_grad_view(g, w), o, c_arr, f"pair_sum{w}")

    def sum_slots(ws, slots):
        for w, s in zip(ws, slots):
            half[w] = _sum_slots(pair[w], s, w, chip_arr, f"sum_slots{w}")

    part_in = _cast_into_full({W_IN: shards[W_IN]}, "cast_w_in")[W_IN]
    h, (part_in,) = _norm1_fwd(x, norm1_gain, mod8, plan=_gather_plan({W_IN: part_in}, part="near"))
    parts, (w_in,) = _cast_into_full({w: shards[w] for w in range(1, N_W)}, "cast_rest",
                                     plan=_gather_plan({W_IN: part_in}, pass_at=(0.8,), part="far"))
    proj, (w_mi,) = _mm(h, w_in, name="mm_proj", bn=512, plan=_gather_plan({W_MI: parts[W_MI]}, pass_at=(0.47, 0.72)))
    (o_a, states), (w_a, w_b) = _hgrn_fwd(
        proj, lb_logits, o_gain, plan=_gather_plan({w: parts[w] for w in (W_A, W_B)}, pass_at=(0.4, 0.65)))
    o_b, (w_out,) = _swa_fwd(proj, q_gain, k_gain, sinks, plan=_gather_plan({W_OUT: parts[W_OUT]}, pass_at=(0.3, 0.5)))
    ya = _mm(o_a, w_a, name="mm_branch_a")
    gate_cols = (OFF_GATE_A // MERGE_BC, OFF_GATE_B // MERGE_BC)
    yb, merged = _mm(o_b, w_b, name="mm_branch_b", bn=MERGE_BC, out_dtypes=(F32, BF16),
                     extras=(proj, proj, ya), extra_cols=gate_cols + (0,),
                     epi=lambda acc, ga, gb, ya_: (acc, _sig(ga) * ya_ + _sig(gb) * acc))
    mo = _mm(merged, w_out, name="mm_out")
    x1, h2 = _res_norm2_fwd(x, mo, norm2_gain, mod8)
    (u, act), (w_mo,) = _mm(h2, w_mi, name="mm_mlp_in", out_dtypes=(F32, BF16), epi=relu2,
                            plan=_gather_plan({W_MO: parts[W_MO]}, pass_at=(0.6, 0.9)))
    dy, dmlp, st_loss = _mm(act, w_mo, name="mm_mlp_out", bm=512, out_dtypes=(F32, BF16, F32), n_stats=1,
                            extras=(x1, target), row_extras=(mod8,), epi=_loss_head)
    st_loss = st_loss.reshape(-1, 8, D_MODEL).sum(axis=0)
    def half_blocks(w, own):
        def block(i):
            return 2 * i + (lax.axis_index("c") if own else 1 - lax.axis_index("c"))
        return (1 if W_SHAPES[w][2] else N_CHIPS), block

    def pair_of(w, lhs, rhs, other, name):
        hr, cols = _half_shape(w)
        p = _mm(lhs, rhs, name=name, ta=True, bn=512, a_blocks=half_blocks(w, True), out_dtypes=(BF16,),
                extras=(other,), epi=lambda acc, o: (acc + o,))
        return p.reshape(-1, hr, W_SHAPES[w][1])

    near, far = (0, 1), (2,)
    g_send = _mm(act, dmlp, name="mm_g_mlp_out_send", ta=True, bn=512, a_blocks=half_blocks(W_MO, False))
    du, (g_other,) = _mm(dmlp, w_mo, name="mm_d_act", tb=True, out_dtypes=(BF16,), extras=(u,),
                         epi=lambda acc, uu: (acc * (2.0 * jnp.maximum(uu, 0.0)),), plan=_sibling_share_plan([g_send]))
    pair[W_MO] = pair_of(W_MO, act, dmlp, g_other, "mm_g_mlp_out_own")
    g_send, (part,) = _mm(h2, du, name="mm_g_mlp_in_send", ta=True, bn=512, a_blocks=half_blocks(W_MI, False),
                          plan=_chip_exchange_plan({W_MO: pair[W_MO]}, near))
    dh2, res = _mm(du, w_mi, name="mm_d_h2", tb=True,
                   plan=_join(_chip_exchange_plan({W_MO: pair[W_MO]}, far, {W_MO: part}), _sibling_share_plan([g_send])))
    sum_slots([W_MO], res[:1])
    pair[W_MI] = pair_of(W_MI, h2, du, res[1], "mm_g_mlp_in_own")
    dx1, dmo, st_n2 = _norm2_bwd(dh2, x1, dy, mo, norm2_gain, mod8)
    def merge_bwd(dm, ga, gb, ya_, yb_):
        sa, sb = _sig(ga), _sig(gb)
        return dm * sa, dm * sb, dm * ya_ * sa * (1.0 - sa), dm * yb_ * sb * (1.0 - sb)

    dya, dyb, dga, dgb = _mm(dmo, w_out, name="mm_d_merged", tb=True, bn=MERGE_BC, out_dtypes=(BF16,) * 4,
                             extras=(proj, proj, ya, yb), extra_cols=gate_cols + (0, 0), epi=merge_bwd)
    g_out = _mm(merged, dmo, name="mm_g_out", ta=True, bn=512)
    do_a = _mm(dya, w_a, name="mm_d_oa", tb=True)
    g_a = _mm(o_a, dya, name="mm_g_branch_a", ta=True, bn=512)
    do_b = _mm(dyb, w_b, name="mm_d_ob", tb=True)
    g_b = _mm(o_b, dyb, name="mm_g_branch_b", ta=True, bn=512)
    mid = [W_A, W_B, W_OUT]
    (dqb, dkvb, st_swa), res = _swa_bwd(
        proj, q_gain, k_gain, sinks, do_b,
        plan=_join(_chip_exchange_plan({W_MI: pair[W_MI]}), exchange(mid, [g_a, g_b, g_out])))
    sum_slots([W_MI], res[:1])
    pair_sums(mid, [g_a, g_b, g_out], res[1:])
    (dqa, dfa, dia, dgga, d_lb, d_og), slots_mid = _hgrn_bwd(
        proj, lb_logits, o_gain, states, do_a, plan=_chip_exchange_plan({w: pair[w] for w in mid}))
    sum_slots(mid, slots_mid)
    dproj = jnp.concatenate([dqa, dfa, dia, dgga, dqb, dkvb, dga, dgb], axis=1)
    done = [W_A, W_B, W_OUT, W_MI, W_MO]
    g_send = _mm(h, dproj, name="mm_g_in_send", ta=True, bn=512, a_blocks=half_blocks(W_IN, False))
    g_own, res = _mm(h, dproj, name="mm_g_in_own", ta=True, bn=512, a_blocks=half_blocks(W_IN, True),
                     plan=_sibling_share_plan([g_send] + [half[w] for w in done]))
    g_other, theirs = res[0], dict(zip(done, res[1:]))
    pair[W_IN] = _add_bf16(g_own, g_other, "pair_sum0")[None]
    dh, slots_in = _mm(dproj, w_in, name="mm_d_h", tb=True, bk=2432, plan=_chip_exchange_plan({W_IN: pair[W_IN]}))
    sum_slots([W_IN], slots_in)
    grad_x, st_n1 = _norm1_bwd(dh, x, dx1, norm1_gain, mod8)
    (theirs[W_IN],) = _run_plan(_sibling_share_plan([half[W_IN]]), "sibling_share_w_in")
    stats = dict(loss=st_loss, n2=st_n2, n1=st_n1, d_lb=d_lb, d_og=d_og, swa=st_swa)
    return grad_x, [half[w] for w in range(N_W)], [theirs[w] for w in range(N_W)], stats


EW_VMEM_BYTES = 40 << 20


def _ew_rows(rows, cols, streams):
    br = 8
    while br * 2 * 4 <= rows and br * 2 * cols * 4 * 2 * streams <= EW_VMEM_BYTES and rows % (br * 2) == 0:
        br *= 2
    return br


CAST_STEPS = 16


def _cast_into_full(shards, name, plan=None):
    ws = sorted(shards)
    in_specs, out_specs, out_shape = [], [], []
    for w in ws:
        sr, sc = shards[w].shape
        R, C, by_col = W_SHAPES[w]
        br = sr // CAST_STEPS
        assert br * CAST_STEPS == sr and br % 16 == 0, (w, sr)

        def out_map(i, by_col=by_col):
            chip = 2 * lax.axis_index("x") + lax.axis_index("y")
            return (i, chip) if by_col else (chip * CAST_STEPS + i, 0)

        in_specs.append(pl.BlockSpec((br, sc), lambda i: (i, 0)))
        out_specs.append(pl.BlockSpec((br, sc), out_map))
        out_shape.append(jax.ShapeDtypeStruct((R, C), BF16))

    def body(*refs):
        for w_ref, o_ref in zip(refs[:len(ws)], refs[len(ws):]):
            o_ref[...] = w_ref[...].astype(BF16)

    res = _pcall(body, plan=plan, name=name, grid=(CAST_STEPS,), in_specs=in_specs, out_specs=out_specs,
                 out_shape=out_shape, compiler_params=_params(("arbitrary",)))(*[shards[w] for w in ws])
    if plan is None:
        return dict(zip(ws, res))
    return dict(zip(ws, res[0])), res[1]


def _adamw_math(w, g, m, v):
    m = ADAM_B1 * m + (1.0 - ADAM_B1) * g
    v = ADAM_B2 * v + (1.0 - ADAM_B2) * (g * g)
    m_hat = m / (1.0 - ADAM_B1 ** ADAM_STEP)
    v_hat = v / (1.0 - ADAM_B2 ** ADAM_STEP)
    delta = -ADAM_LR * (m_hat / (jnp.sqrt(v_hat) + ADAM_EPS) + ADAM_WD * w)
    return delta, m, v


def _adamw(w, g, m, v, name):
    R, C = w.shape
    br = _ew_rows(R, C, 7)
    spec = pl.BlockSpec((br, C), lambda i: (i, 0))

    def body(w_ref, g_ref, m_ref, v_ref, d_ref, nm_ref, nv_ref):
        d_ref[...], nm_ref[...], nv_ref[...] = _adamw_math(w_ref[...], g_ref[...], m_ref[...], v_ref[...])

    sh = jax.ShapeDtypeStruct((R, C), F32)
    return _pcall(body, name=name, grid=(R // br,), in_specs=[spec] * 4, out_specs=[spec] * 3, out_shape=[sh] * 3,
                  compiler_params=_params(("parallel",)))(w, g, m, v)


def _add_bf16(a, b, name):
    R, C = a.shape
    br = _ew_rows(R, C, 2.5)
    spec = pl.BlockSpec((br, C), lambda i: (i, 0))

    def body(a_ref, b_ref, o_ref):
        o_ref[...] = (a_ref[...] + b_ref[...]).astype(BF16)

    return _pcall(body, name=name, grid=(R // br,), in_specs=[spec, spec], out_specs=spec,
                  out_shape=jax.ShapeDtypeStruct((R, C), BF16), compiler_params=_params(("parallel",)))(a, b)


def _adamw_halves(w, own, other, m, v, c_arr, name):
    R, C = w.shape
    hr = R // 2
    br = _ew_rows(hr, C, 9)
    nb = hr // br
    full = pl.BlockSpec((br, C), lambda h, i, c_ref: (h * nb + i, 0))
    half = pl.BlockSpec((br, C), lambda h, i, c_ref: (i, 0))

    def body(c_ref, w_ref, own_ref, oth_ref, m_ref, v_ref, g_ref, d_ref, nm_ref, nv_ref):
        g = jnp.where(pl.program_id(0) == c_ref[0], own_ref[...], oth_ref[...])
        g_ref[...] = g
        d_ref[...], nm_ref[...], nv_ref[...] = _adamw_math(w_ref[...], g, m_ref[...], v_ref[...])

    sh = jax.ShapeDtypeStruct((R, C), F32)
    return _pcall(
        body, name=name,
        grid_spec=pltpu.PrefetchScalarGridSpec(
            num_scalar_prefetch=1, grid=(2, nb), in_specs=[full, half, half, full, full], out_specs=[full] * 4),
        out_shape=[sh] * 4, compiler_params=_params(("parallel", "parallel")))(c_arr, w, own, other, m, v)


def _ada_grad_adamw(c_t, dmod, w, m, v):
    R, C = w.shape
    br = _ew_rows(R, C, 8)
    spec = pl.BlockSpec((br, C), lambda i: (i, 0))

    def body(c_ref, dm_ref, w_ref, m_ref, v_ref, g_ref, d_ref, nm_ref, nv_ref):
        cv = c_ref[...]
        sc = cv * _sig(cv)
        g = sc[:, 0:1] * dm_ref[0:1, :]
        for b in range(1, N_DEV):
            g = g + sc[:, b:b + 1] * dm_ref[b:b + 1, :]
        g_ref[...] = g
        d_ref[...], nm_ref[...], nv_ref[...] = _adamw_math(w_ref[...], g, m_ref[...], v_ref[...])

    sh = jax.ShapeDtypeStruct((R, C), F32)
    return _pcall(
        body, name="ada_grad_adamw", grid=(R // br,),
        in_specs=[pl.BlockSpec((br, N_DEV), lambda i: (i, 0)), pl.BlockSpec((N_DEV, C), lambda i: (0, 0)), spec, spec, spec],
        out_specs=[spec] * 4, out_shape=[sh] * 4, compiler_params=_params(("parallel",)))(c_t, dmod, w, m, v)


SMALL_ROWS = 16


def _small_sum(small_all, lb_logits):
    def body(s_ref, lbl_ref, o_ref):
        acc = s_ref[0:SMALL_ROWS, :]
        for d in range(1, N_DEV):
            acc = acc + s_ref[d * SMALL_ROWS:(d + 1) * SMALL_ROWS, :]
        o_ref[...] = acc
        z = lbl_ref[...]
        e = jnp.exp(z - jnp.max(z, axis=0, keepdims=True))
        p0 = e[0:1, :] / (e[0:1, :] + e[1:2, :])
        dz = acc[8:9, 0:A_WIDTH] * p0 * (1.0 - p0)
        o_ref[8:9, 0:A_WIDTH] = dz
        o_ref[10:11, 0:A_WIDTH] = -dz

    return _pcall(body, name="small_sum", out_shape=jax.ShapeDtypeStruct((SMALL_ROWS, D_MODEL), F32),
                  in_specs=[pl.BlockSpec(memory_space=pltpu.VMEM)] * 2, out_specs=pl.BlockSpec(memory_space=pltpu.VMEM),
                  compiler_params=_params())(small_all, lb_logits)


RELATIONS = ((1, 0), (0, 1), (1, 1))
ANY = pl.BlockSpec(memory_space=pl.ANY)


def _place():
    x, y, c = lax.axis_index("x"), lax.axis_index("y"), lax.axis_index("c")
    return x, y, c


def _allgather_small(x_shard, name):
    m_per, n = x_shard.shape

    def body(x_ref, out_ref, send_sems, recv_sems, local_sem):
        x, y, c = _place()
        me, sibling = (x, y, c), (x, y, 1 - c)
        chips = [(1 - x, y), (x, 1 - y), (1 - x, 1 - y)]

        def rows(px, py, pc):
            return out_ref.at[pl.ds((4 * px + 2 * py + pc) * m_per, m_per), :]

        def copy(k, block, to, src=None):
            return pltpu.make_async_remote_copy(
                src_ref=rows(*block) if src is None else src, dst_ref=rows(*block),
                send_sem=send_sems.at[k], recv_sem=recv_sems.at[k], device_id=to, device_id_type=MESH)

        mine = pltpu.make_async_copy(x_ref, rows(*me), local_sem)
        mine.start()
        first = [copy(0, me, sibling, src=x_ref)]
        first += [copy(1 + j, me, (*chip, c), src=x_ref) for j, chip in enumerate(chips)]
        for cp in first:
            cp.start()
        passed = [copy(4 + j, (*chip, c), sibling) for j, chip in enumerate(chips)]
        for j, chip in enumerate(chips):
            copy(1 + j, (*chip, c), me).wait_recv()
            passed[j].start()
        copy(0, sibling, me).wait_recv()
        for j, chip in enumerate(chips):
            copy(4 + j, (*chip, 1 - c), me).wait_recv()
        for cp in first + passed:
            cp.wait_send()
        mine.wait()

    return _pcall(
        body, name=name, out_shape=jax.ShapeDtypeStruct((N_DEV * m_per, n), x_shard.dtype),
        in_specs=[pl.BlockSpec(memory_space=pltpu.VMEM)], out_specs=pl.BlockSpec(memory_space=pltpu.VMEM),
        scratch_shapes=[pltpu.SemaphoreType.DMA((7,)), pltpu.SemaphoreType.DMA((7,)), pltpu.SemaphoreType.DMA],
        compiler_params=_params(),
    )(x_shard)


W_SHAPES = ((D_MODEL, IN_WIDTH, True), (A_WIDTH, D_MODEL, True), (B_WIDTH, D_MODEL, True),
            (D_MODEL, D_MODEL, False), (D_MODEL, MLP_HIDDEN, True), (MLP_HIDDEN, D_MODEL, False))
N_W = len(W_SHAPES)


def _shard_shape(w):
    R, C, by_col = W_SHAPES[w]
    return (R, C // N_CHIPS) if by_col else (R // N_CHIPS, C)


def _half_shape(w):
    sr, sc = _shard_shape(w)
    return sr // 2, sc


def _region(full_ref, w, chip, half, quarter=None):
    sr, sc = _shard_shape(w)
    by_col = W_SHAPES[w][2]
    r0, c0 = (0, chip * sc) if by_col else (chip * sr, 0)
    r0, rows = r0 + half * (sr // 2), sr // 2
    if quarter is not None:
        r0, rows = r0 + quarter * (rows // 2), rows // 2
    return full_ref.at[pl.ds(r0, rows), pl.ds(c0, sc)]


def _on_device(fn):
    x, y, c = _place()
    me = 4 * x + 2 * y + c
    for d in range(N_DEV):
        @pl.when(me == d)
        def _(d=d):
            fn(x, y, c, d)


GATHER_COPIES = (
    (0, 0, None, "x"), (0, 0, None, "y"),
    (1, 2, 0, "y"), (1, 1, 1, "x"),
    (1, 2, None, "s"), (1, 1, None, "s"),
    (2, 3, 0, "s"), (2, 3, 1, "s"),
)
PEER_FLIP = {"x": 2, "y": 1, "s": 0}


GATHER_STAGES = {
    None: (((), (0, 1), ()), ((0, 1), (2, 3, 4, 5), ()), ((2, 3), (6, 7), ()), ((4, 5, 6, 7), (), tuple(range(8)))),
    "near": (((), (0, 1), ()), ((0, 1), (), (0, 1))),
    "far": (((), (2, 3, 4, 5), ()), ((2, 3), (6, 7), ()), ((4, 5, 6, 7), (), (2, 3, 4, 5, 6, 7))),
}


def _gather_plan(partials, pass_at=(0.5, 0.75), part=None):
    ws = sorted(partials)
    n_t = len(GATHER_COPIES)
    jobs = [(i, w) for i, w in enumerate(ws)]

    def copy(pi, po, ps, x, y, c, d, i, w, t, landing):
        chip, dc = d >> 1, d & 1
        stage, flip, quarter, to = GATHER_COPIES[t]
        if landing:
            peer_chip = chip ^ PEER_FLIP[to]
            part = _region(po[i], w, peer_chip ^ flip, (1 - dc) if to == "s" else dc, quarter)
            src = part
        else:
            part = _region(po[i], w, chip ^ flip, dc, quarter)
            here = flip != 0 and (part_of is None or stage == 2)
            src = part if here else _region(pi[i], w, chip ^ flip, dc, quarter)
        target = {"x": (x ^ 1, y, c), "y": (x, y ^ 1, c), "s": (x, y, 1 - c)}[to]
        return pltpu.make_async_remote_copy(
            src_ref=src, dst_ref=part, send_sem=ps[0].at[i * n_t + t], recv_sem=ps[1].at[i * n_t + t],
            device_id=target, device_id_type=MESH)

    part_of = part

    def stage(landed, started, sent):
        def run(pi, po, ps):
            def on(x, y, c, d):
                for i, w in jobs:
                    for t in landed:
                        copy(pi, po, ps, x, y, c, d, i, w, t, True).wait_recv()
                for i, w in jobs:
                    for t in started:
                        copy(pi, po, ps, x, y, c, d, i, w, t, False).start()
                for i, w in jobs:
                    for t in sent:
                        copy(pi, po, ps, x, y, c, d, i, w, t, False).wait_send()
            _on_device(on)
        return run

    stages = [stage(*st) for st in GATHER_STAGES[part]]
    mid_at = tuple(pass_at) if part is None else tuple(pass_at)[:len(stages) - 2]
    return _Plan([partials[w] for w in ws], [jax.ShapeDtypeStruct(W_SHAPES[w][:2], BF16) for w in ws],
                 [pltpu.SemaphoreType.DMA((n_t * len(ws),)) for _ in range(2)], stages,
                 {i: i for i in range(len(ws))}, mid_at=mid_at)


def _grad_view(g, w):
    R, C, by_col = W_SHAPES[w]
    return g.reshape(1, 2, R // 2, C) if by_col else g.reshape(N_CHIPS, 2, R // N_CHIPS // 2, C)


def _start_wait_plan(ins, outs, n_copies, copies):
    def start(pi, po, ps):
        for cp in copies(pi, po, ps):
            cp.start()

    def finish(pi, po, ps):
        for cp in copies(pi, po, ps):
            cp.wait()

    return _Plan(ins, outs, [pltpu.SemaphoreType.DMA((n_copies,)), pltpu.SemaphoreType.DMA((n_copies,))], [start, finish])


def _sibling_exchange_plan(g4s):
    pieces = [(i, p) for i, g in enumerate(g4s) for p in range(g.shape[0])]

    def copies(pi, po, ps):
        x, y, c = _place()
        return [pltpu.make_async_remote_copy(
            src_ref=pi[i].at[p, 1 - c], dst_ref=po[i].at[p], send_sem=ps[0].at[n], recv_sem=ps[1].at[n],
            device_id=(x, y, 1 - c), device_id_type=MESH) for n, (i, p) in enumerate(pieces)]

    return _start_wait_plan(list(g4s), [jax.ShapeDtypeStruct((g.shape[0],) + g.shape[2:], F32) for g in g4s],
                            len(pieces), copies)


def _pair_sum(g4, other, c_arr, name):
    P, _, hr, C = g4.shape
    br = _ew_rows(hr, C, 2.5)

    def body(c_ref, g_ref, o_ref, p_ref):
        p_ref[...] = (g_ref[...] + o_ref[...]).astype(BF16)

    return _pcall(
        body, name=name,
        grid_spec=pltpu.PrefetchScalarGridSpec(
            num_scalar_prefetch=1, grid=(P, hr // br),
            in_specs=[pl.BlockSpec((None, None, br, C), lambda p, i, c_ref: (p, c_ref[0], i, 0)),
                      pl.BlockSpec((None, br, C), lambda p, i, c_ref: (p, i, 0))],
            out_specs=pl.BlockSpec((None, br, C), lambda p, i, c_ref: (p, i, 0))),
        out_shape=jax.ShapeDtypeStruct((P, hr, C), BF16),
        compiler_params=_params(("parallel", "parallel")),
    )(c_arr, g4, other)


def _pair_part(p_ref, w, chip):
    sr, sc = _shard_shape(w)
    return p_ref.at[0, :, pl.ds(chip * sc, sc)] if W_SHAPES[w][2] else p_ref.at[chip]


def _chip_exchange_plan(pairs, rels=(0, 1, 2), into=None):
    ws = sorted(pairs)
    n = len(ws)

    def stage(wait):
        def run(pi, po, ps):
            def on(x, y, c, d):
                for i, w in enumerate(ws):
                    for k, (rx, ry) in enumerate(RELATIONS):
                        if k not in rels:
                            continue
                        cp = pltpu.make_async_remote_copy(
                            src_ref=_pair_part(pi[i], w, (d >> 1) ^ (2 * rx + ry)), dst_ref=po[i].at[k],
                            send_sem=ps[0].at[i * 3 + k], recv_sem=ps[1].at[i * 3 + k],
                            device_id=(x ^ rx, y ^ ry, c), device_id_type=MESH)
                        if wait:
                            cp.wait()
                        else:
                            cp.start()
            _on_device(on)
        return run

    ins = [pairs[w] for w in ws] + ([into[w] for w in ws] if into else [])
    return _Plan(ins, [jax.ShapeDtypeStruct((3,) + _half_shape(w), BF16) for w in ws],
                 [pltpu.SemaphoreType.DMA((3 * n,)), pltpu.SemaphoreType.DMA((3 * n,))],
                 [stage(False), stage(True)], {n + i: i for i in range(n)} if into else None)


def _sum_slots(pair, slots, w, chip_arr, name):
    _, hr, C = slots.shape
    br = _ew_rows(hr, C, 3)
    own_map = (lambda i, chip: (0, i, chip[0])) if W_SHAPES[w][2] else (lambda i, chip: (chip[0], i, 0))

    def body(chip_ref, p_ref, s_ref, o_ref):
        acc = p_ref[...].astype(F32)
        for k in range(3):
            acc = acc + s_ref[k].astype(F32)
        o_ref[...] = acc

    return _pcall(
        body, name=name,
        grid_spec=pltpu.PrefetchScalarGridSpec(
            num_scalar_prefetch=1, grid=(hr // br,),
            in_specs=[pl.BlockSpec((None, br, C), own_map), pl.BlockSpec((3, br, C), lambda i, chip: (0, i, 0))],
            out_specs=pl.BlockSpec((br, C), lambda i, chip: (i, 0))),
        out_shape=jax.ShapeDtypeStruct((hr, C), F32), compiler_params=_params(("parallel",)),
    )(chip_arr, pair, slots)


def _sibling_share_plan(halves):
    def copies(pi, po, ps):
        x, y, c = _place()
        return [pltpu.make_async_remote_copy(
            src_ref=pi[i], dst_ref=po[i], send_sem=ps[0].at[i], recv_sem=ps[1].at[i],
            device_id=(x, y, 1 - c), device_id_type=MESH) for i in range(len(halves))]

    return _start_wait_plan(list(halves), [jax.ShapeDtypeStruct(h.shape, F32) for h in halves], len(halves), copies)


def _pad_lanes(v, width=D_MODEL):
    return jnp.pad(v, ((0, 0), (0, width - v.shape[1])))


def _pack_small(b_ada, norm1, norm2, lb, o_gain, q_gain, k_gain, sinks):
    rows = [b_ada.reshape(N_MOD, D_MODEL), norm1, norm2, jnp.concatenate([lb[0:1], o_gain], axis=1),
            _pad_lanes(jnp.concatenate([q_gain, k_gain, sinks], axis=1)), _pad_lanes(lb[1:2]),
            jnp.zeros((SMALL_ROWS - 11, D_MODEL), F32)]
    return jnp.concatenate(rows, axis=0)


def _unpack_small(p):
    return (p[0:6].reshape(1, N_MOD * D_MODEL), p[6:7], p[7:8],
            jnp.concatenate([p[8:9, 0:A_WIDTH], p[10:11, 0:A_WIDTH]], axis=0), p[8:9, A_WIDTH:],
            p[9:10, 0:64], p[9:10, 64:128], p[9:10, 128:144])


def kernel(x, c, w_ada, b_ada, norm1_gain, w_in, lb_logits, hgrn_o_gain, q_norm_gain, k_norm_gain, sinks, w_branch_a, w_branch_b, w_out, norm2_gain, w_mlp_in, w_mlp_out, loss_target, m_w_ada, m_b_ada, m_norm1_gain, m_w_in, m_lb_logits, m_hgrn_o_gain, m_q_norm_gain, m_k_norm_gain, m_sinks, m_w_branch_a, m_w_branch_b, m_w_out, m_norm2_gain, m_w_mlp_in, m_w_mlp_out, v_w_ada, v_b_ada, v_norm1_gain, v_w_in, v_lb_logits, v_hgrn_o_gain, v_q_norm_gain, v_k_norm_gain, v_sinks, v_w_branch_a, v_w_branch_b, v_w_out, v_norm2_gain, v_w_mlp_in, v_w_mlp_out):
    xi, yi, ci = _place()
    chip = 2 * xi + yi
    me = 4 * xi + 2 * yi + ci
    ada_cols = w_ada.shape[2]

    c_all = _allgather_small(jnp.broadcast_to(c, (8, D_MODEL)), "gather_c").reshape(N_DEV, 8, D_MODEL)[:, 0]
    b_cols = lax.dynamic_slice(b_ada, (0, chip * ada_cols), (1, ada_cols))
    mod_part = _ada_fwd(c_all, w_ada[0], b_cols)
    mod_all = _allgather_small(mod_part, "gather_mod").reshape(N_CHIPS, 2, N_DEV, ada_cols)[:, 0]
    mod_mine = lax.dynamic_index_in_dim(mod_all, me, axis=1, keepdims=False).reshape(N_MOD, D_MODEL)
    mod8 = jnp.concatenate([mod_mine, jnp.zeros((2, D_MODEL), F32)], axis=0)

    shards = (w_in[0], w_branch_a[0], w_branch_b[0], w_out[0], w_mlp_in[0], w_mlp_out[0])
    chip_arr = chip.astype(jnp.int32).reshape(1)
    c_arr = ci.astype(jnp.int32).reshape(1)

    grad_x, halves, theirs, st = _local_step(x[0], loss_target[0], mod8, norm1_gain, norm2_gain, lb_logits, hgrn_o_gain,
                                             q_norm_gain, k_norm_gain, sinks, shards, c_arr, chip_arr)
    loss = lax.psum(0.5 * jnp.sum(st["loss"][0]) / D_MODEL, ("x", "y", "c"))
    moments = ((m_w_in, v_w_in), (m_w_branch_a, v_w_branch_a), (m_w_branch_b, v_w_branch_b), (m_w_out, v_w_out),
               (m_w_mlp_in, v_w_mlp_in), (m_w_mlp_out, v_w_mlp_out))
    big = [_adamw_halves(shards[w], halves[w], theirs[w], moments[w][0][0], moments[w][1][0], c_arr, f"adamw{w}")
           for w in range(N_W)]

    swa = st["swa"]
    small = jnp.concatenate([
        st["n1"][1:2], st["n1"][0:1], st["n2"][3:4], st["n2"][1:2], st["n2"][0:1], st["loss"][1:2],
        st["n1"][2:3], st["n2"][2:3], jnp.concatenate([st["d_lb"][0:1], st["d_og"][0:1]], axis=1),
        _pad_lanes(jnp.concatenate([swa[0:1, 0:64], swa[1:2, 0:64], swa[2:3, 0:16]], axis=1)),
        jnp.zeros((SMALL_ROWS - 10, D_MODEL), F32)], axis=0)
    small_all = _allgather_small(small, "gather_small")
    g_small = _small_sum(small_all, lb_logits)
    small_w = (b_ada, norm1_gain, norm2_gain, lb_logits, hgrn_o_gain, q_norm_gain, k_norm_gain, sinks)
    small_m = (m_b_ada, m_norm1_gain, m_norm2_gain, m_lb_logits, m_hgrn_o_gain, m_q_norm_gain, m_k_norm_gain, m_sinks)
    small_v = (v_b_ada, v_norm1_gain, v_norm2_gain, v_lb_logits, v_hgrn_o_gain, v_q_norm_gain, v_k_norm_gain, v_sinks)
    sm = [_unpack_small(t) for t in
          (g_small,) + tuple(_adamw(_pack_small(*small_w), g_small, _pack_small(*small_m), _pack_small(*small_v),
                                    "adamw_small"))]
    g_b, g_n1, g_n2, g_lb, g_og, g_qg, g_kg, g_sk = ([t[i] for t in sm] for i in range(8))

    dmod_all = small_all.reshape(N_DEV, SMALL_ROWS, D_MODEL)[:, 0:N_MOD].reshape(N_DEV, N_MOD * D_MODEL)
    dmod_cols = lax.dynamic_slice(dmod_all, (0, chip * ada_cols), (N_DEV, ada_cols))
    ada = _ada_grad_adamw(c_all.T, dmod_cols, w_ada[0], m_w_ada[0], v_w_ada[0])

    def ordered(k):
        lead = lambda a: a[None]
        return (lead(ada[k]), g_b[k], g_n1[k], lead(big[0][k]), g_lb[k], g_og[k], g_qg[k], g_kg[k], g_sk[k],
                lead(big[1][k]), lead(big[2][k]), lead(big[3][k]), g_n2[k], lead(big[4][k]), lead(big[5][k]))

    return (loss, grad_x[None]) + ordered(0) + ordered(1) + ordered(2) + ordered(3)
```

```python
import jax
import jax.numpy as jnp
from jax import lax
from jax.experimental import pallas as pl
from jax.experimental.pallas import tpu as pltpu

F32 = jnp.float32
BF16 = jnp.bfloat16
HIGHEST = lax.Precision.HIGHEST
MESH = pl.DeviceIdType.MESH

D_MODEL = 2048
A_WIDTH = 1024
A_HEADS = 8
A_HEAD_DIM = 128
A_CHUNK = 64
B_WIDTH = 1024
B_HEAD_DIM = 64
B_GROUP = 4
B_KV_HEADS = 4
B_KV_WIDTH = 256
BLOCK = 128
MLP_HIDDEN = 8192
IN_WIDTH = 9728
N_MOD = 6
EPS = 1e-6
N_CHIPS = 4
N_DEV = 8

OFF_QA, OFF_FA, OFF_IA, OFF_GA = 0, 1024, 2048, 3072
OFF_QB, OFF_KB, OFF_VB = 4096, 5120, 5376
OFF_GATE_A, OFF_GATE_B = 5632, 7680

ADAM_LR = 0.001
ADAM_B1 = 0.9
ADAM_B2 = 0.999
ADAM_EPS = 1e-08
ADAM_WD = 0.01
ADAM_STEP = 10

VMEM_LIMIT_V7X = 48 * 1024 * 1024
NEG_BIG = -1e30


def _params(sem=None, vmem=VMEM_LIMIT_V7X):
    return pltpu.CompilerParams(dimension_semantics=sem, vmem_limit_bytes=vmem)


class _Plan:
    def __init__(self, ins, outs, sems, stages, aliases=None, mid_at=()):
        self.ins, self.outs, self.sems, self.stages, self.aliases = ins, outs, sems, stages, aliases or {}
        self.mid_at = tuple(mid_at)
        assert len(self.mid_at) == len(stages) - 2


def _join(a, b):
    assert len(a.stages) == 2 and len(b.stages) == 2
    ni, no, ns = len(a.ins), len(a.outs), len(a.sems)

    def stage(k):
        def run(pi, po, ps):
            a.stages[k](pi[:ni], po[:no], ps[:ns])
            b.stages[k](pi[ni:], po[no:], ps[ns:])
        return run

    aliases = dict(a.aliases)
    aliases.update({ni + i: no + o for i, o in b.aliases.items()})
    return _Plan(a.ins + b.ins, a.outs + b.outs, a.sems + b.sems, [stage(0), stage(1)], aliases)


def _pcall(body, plan=None, **kw):
    if plan is None:
        return pl.pallas_call(body, **kw)
    grid = kw["grid"]
    single = not isinstance(kw["out_specs"], (list, tuple))
    in_specs = list(kw["in_specs"])
    out_specs = [kw["out_specs"]] if single else list(kw["out_specs"])
    out_shape = [kw["out_shape"]] if single else list(kw["out_shape"])
    scratch = list(kw.get("scratch_shapes", ()))
    n_in, n_out, n_scr = len(in_specs), len(out_specs), len(scratch)
    n_pi, n_po = len(plan.ins), len(plan.outs)
    total = 1
    for g in grid:
        total *= g
    n_st = len(plan.stages)

    def wrapped(*refs):
        o0 = n_in + n_pi
        s0 = o0 + n_out + n_po
        pi, po, ps = refs[n_in:o0], refs[o0 + n_out:s0], refs[s0 + n_scr:]
        lin = 0
        for d, g in enumerate(grid):
            lin = lin * g + pl.program_id(d)
        for si, frac in enumerate((0.0,) + plan.mid_at):
            @pl.when(lin == int(frac * (total - 1)))
            def _(si=si):
                plan.stages[si](pi, po, ps)
        body(*refs[:n_in], *refs[o0:o0 + n_out], *refs[s0:s0 + n_scr])

        @pl.when(lin == total - 1)
        def _():
            plan.stages[-1](pi, po, ps)

    any_spec = pl.BlockSpec(memory_space=pl.ANY)
    call = pl.pallas_call(
        wrapped, name=kw["name"], grid=grid, in_specs=in_specs + [any_spec] * n_pi,
        out_specs=out_specs + [any_spec] * n_po, out_shape=out_shape + list(plan.outs),
        scratch_shapes=scratch + list(plan.sems),
        input_output_aliases={n_in + i: n_out + o for i, o in plan.aliases.items()},
        compiler_params=_params(("arbitrary",) * len(grid)))

    def run(*args):
        res = call(*args, *plan.ins)
        outs = list(res[:n_out])
        return (outs[0] if single else outs), list(res[n_out:])

    return run


def _run_plan(plan, name):
    return _pcall(lambda: None, plan=plan, name=name, grid=(1,), in_specs=[], out_specs=[], out_shape=[])()[1]


def _sig(x):
    return 1.0 / (1.0 + jnp.exp(-x))


def _nn(a, b):
    return lax.dot_general(a.astype(BF16), b.astype(BF16), (((1,), (0,)), ((), ())), preferred_element_type=F32)


def _nt(a, b):
    return lax.dot_general(a.astype(BF16), b.astype(BF16), (((1,), (1,)), ((), ())), preferred_element_type=F32)


def _tn(a, b):
    return lax.dot_general(a.astype(BF16), b.astype(BF16), (((0,), (0,)), ((), ())), preferred_element_type=F32)


def _mm(a, b, *, name, ta=False, tb=False, bm=1024, bn=1024, bk=2048, out_dtypes=(F32,), epi=None, extras=(),
        extra_cols=None, plan=None, a_blocks=None, row_extras=(), n_stats=0):
    if ta:
        K, M = a.shape
        bk = K
        if a_blocks is not None:
            M = a_blocks[0] * bm
    else:
        M, K = a.shape
    if tb:
        N, K2 = b.shape
    else:
        K2, N = b.shape
    bm, bn, bk = min(bm, M), min(bn, N), min(bk, K)
    assert K == K2 and M % bm == 0 and N % bn == 0 and K % bk == 0, (name, a.shape, b.shape)
    nk = K // bk
    a_col = a_blocks[1] if a_blocks is not None else (lambda i: i)
    a_spec = pl.BlockSpec((bk, bm), lambda i, j, k: (k, a_col(i))) if ta else pl.BlockSpec((bm, bk), lambda i, j, k: (i, k))
    b_spec = pl.BlockSpec((bn, bk), lambda i, j, k: (j, k)) if tb else pl.BlockSpec((bk, bn), lambda i, j, k: (k, j))
    t_spec = pl.BlockSpec((bm, bn), lambda i, j, k: (i, j))
    extra_cols = extra_cols or (0,) * len(extras)
    e_specs = [pl.BlockSpec((bm, bn), lambda i, j, k, off=off: (i, off + j)) for off in extra_cols]
    e_specs += [pl.BlockSpec((8, bn), lambda i, j, k: (0, j)) for _ in row_extras]
    dims = (((1,), (1 if tb else 0,)), ((), ()))
    n_e, n_o = len(extras) + len(row_extras), len(out_dtypes)
    stat_spec = pl.BlockSpec((8, bn), lambda i, j, k: (i, j))

    def body(*refs):
        a_ref, b_ref = refs[0], refs[1]
        e_refs = refs[2:2 + n_e]
        o_refs = refs[2 + n_e:2 + n_e + n_o]

        def finish(acc):
            outs = (acc,) if epi is None else epi(acc, *[e[...] for e in e_refs])
            for o_ref, o in zip(o_refs, outs):
                o_ref[...] = o.astype(o_ref.dtype)

        if ta:
            at_ref = refs[-1]

            @pl.when(pl.program_id(1) == 0)
            def _():
                at_ref[...] = a_ref[...].T

            lhs = at_ref[...]
        else:
            lhs = a_ref[...].astype(BF16)
        part = lax.dot_general(lhs, b_ref[...].astype(BF16), dims, preferred_element_type=F32)
        if nk == 1:
            finish(part)
        else:
            acc_ref = refs[-1]
            k = pl.program_id(2)

            @pl.when(k == 0)
            def _():
                acc_ref[...] = part

            @pl.when(k > 0)
            def _():
                acc_ref[...] += part

            @pl.when(k == nk - 1)
            def _():
                finish(acc_ref[...])

    if ta:
        assert a.dtype == BF16 and nk == 1
        scratch = [pltpu.VMEM((bm, bk), BF16)]
    else:
        scratch = [pltpu.VMEM((bm, bn), F32)] if nk > 1 else []
    out = _pcall(
        body, plan=plan, name=name, grid=(M // bm, N // bn, nk),
        in_specs=[a_spec, b_spec] + e_specs,
        out_specs=[t_spec] * (n_o - n_stats) + [stat_spec] * n_stats,
        out_shape=[jax.ShapeDtypeStruct((M, N), dt) for dt in out_dtypes[:n_o - n_stats]]
        + [jax.ShapeDtypeStruct((8 * (M // bm), N), F32)] * n_stats,
        scratch_shapes=scratch,
        compiler_params=_params(("parallel", "arbitrary", "arbitrary")),
    )(a, b, *extras, *row_extras)
    if plan is not None:
        return (out[0][0] if n_o == 1 else out[0]), out[1]
    return out[0] if n_o == 1 else out


def _ada_fwd(c_all, w_ada, b_cols):
    n = w_ada.shape[1]
    bn = 512

    def body(c_ref, w_ref, b_ref, o_ref):
        cv = c_ref[...]
        sc = cv * _sig(cv)
        o_ref[...] = jnp.dot(sc, w_ref[...], precision=HIGHEST, preferred_element_type=F32) + b_ref[...]

    return _pcall(
        body, name="ada_fwd", grid=(n // bn,),
        in_specs=[pl.BlockSpec((N_DEV, D_MODEL), lambda j: (0, 0)), pl.BlockSpec((D_MODEL, bn), lambda j: (0, j)),
                  pl.BlockSpec((1, bn), lambda j: (0, j))],
        out_specs=pl.BlockSpec((N_DEV, bn), lambda j: (0, j)),
        out_shape=jax.ShapeDtypeStruct((N_DEV, n), F32),
        compiler_params=_params(("parallel",)),
    )(c_all, w_ada, b_cols)


ROWS_EW = 256


def _rms_fwd_math(x, gain, scale, shift):
    rstd = lax.rsqrt(jnp.mean(x * x, axis=-1, keepdims=True) + EPS)
    xhat = x * rstd
    n = xhat * gain
    return n * (1.0 + scale) + shift, xhat, n, rstd


def _rms_bwd_math(dh, xhat, n, rstd, gain, scale):
    dn = dh * (1.0 + scale)
    dxhat = dn * gain
    dx = rstd * (dxhat - xhat * jnp.mean(dxhat * xhat, axis=-1, keepdims=True))
    d_scale = jnp.sum(dh * n, axis=0, keepdims=True)
    d_shift = jnp.sum(dh, axis=0, keepdims=True)
    d_gain = jnp.sum(dn * xhat, axis=0, keepdims=True)
    return dx, d_scale, d_shift, d_gain


def _row_spec(w=D_MODEL, br=ROWS_EW):
    return pl.BlockSpec((br, w), lambda i: (i, 0))


def _vec_spec(r=8, w=D_MODEL):
    return pl.BlockSpec((r, w), lambda i: (0, 0))


def _norm1_fwd(x, gain, mod8, plan=None):
    T = x.shape[0]

    def body(x_ref, g_ref, m_ref, h_ref):
        h, _, _, _ = _rms_fwd_math(x_ref[...], g_ref[...], m_ref[1:2, :], m_ref[0:1, :])
        h_ref[...] = h.astype(BF16)

    return _pcall(
        body, plan=plan, name="norm1_fwd", grid=(T // ROWS_EW,),
        in_specs=[_row_spec(), _vec_spec(1), _vec_spec()],
        out_specs=_row_spec(), out_shape=jax.ShapeDtypeStruct((T, D_MODEL), BF16),
        compiler_params=_params(("parallel",)),
    )(x, gain, mod8)


def _res_norm2_fwd(x, mo, gain, mod8):
    T = x.shape[0]
    br = ROWS_EW

    def body(x_ref, mo_ref, g_ref, m_ref, x1_ref, h_ref):
        x1 = x_ref[...] + m_ref[2:3, :] * mo_ref[...]
        x1_ref[...] = x1
        h, _, _, _ = _rms_fwd_math(x1, g_ref[...], m_ref[4:5, :], m_ref[3:4, :])
        h_ref[...] = h.astype(BF16)

    return _pcall(
        body, name="res_norm2_fwd", grid=(T // br,),
        in_specs=[_row_spec(br=br), _row_spec(br=br), _vec_spec(1), _vec_spec()],
        out_specs=[_row_spec(br=br), _row_spec(br=br)],
        out_shape=[jax.ShapeDtypeStruct((T, D_MODEL), F32), jax.ShapeDtypeStruct((T, D_MODEL), BF16)],
        compiler_params=_params(("parallel",)),
    )(x, mo, gain, mod8)


def _loss_head(mlp, x1, target, mod):
    gate = mod[5:6, :]
    err = x1 + gate * mlp - target
    dy = err * (1.0 / D_MODEL)
    row = lax.broadcasted_iota(jnp.int32, (8, mlp.shape[1]), 0)
    stats = jnp.where(row == 0, jnp.sum(err * err, axis=0, keepdims=True),
                      jnp.where(row == 1, jnp.sum(dy * mlp, axis=0, keepdims=True), 0.0))
    return dy, dy * gate, stats


def _norm2_bwd(dh2, x1, dy, mo, gain, mod8):
    T = x1.shape[0]

    def body(dh_ref, x1_ref, dy_ref, mo_ref, g_ref, m_ref, dx1_ref, dmo_ref, st_ref):
        i = pl.program_id(0)
        gain_v, scale = g_ref[...], m_ref[4:5, :]
        _, xhat, n, rstd = _rms_fwd_math(x1_ref[...], gain_v, scale, m_ref[3:4, :])
        dx, d_scale, d_shift, d_gain = _rms_bwd_math(dh_ref[...], xhat, n, rstd, gain_v, scale)
        dx1 = dy_ref[...] + dx
        dx1_ref[...] = dx1
        dmo_ref[...] = (dx1 * m_ref[2:3, :]).astype(BF16)

        @pl.when(i == 0)
        def _():
            st_ref[...] = jnp.zeros_like(st_ref)

        st_ref[0:1, :] += d_scale
        st_ref[1:2, :] += d_shift
        st_ref[2:3, :] += d_gain
        st_ref[3:4, :] += jnp.sum(dx1 * mo_ref[...], axis=0, keepdims=True)

    return _pcall(
        body, name="norm2_bwd", grid=(T // ROWS_EW,),
        in_specs=[_row_spec(), _row_spec(), _row_spec(), _row_spec(), _vec_spec(1), _vec_spec()],
        out_specs=[_row_spec(), _row_spec(), _vec_spec()],
        out_shape=[jax.ShapeDtypeStruct((T, D_MODEL), F32), jax.ShapeDtypeStruct((T, D_MODEL), BF16),
                   jax.ShapeDtypeStruct((8, D_MODEL), F32)],
        compiler_params=_params(("arbitrary",)),
    )(dh2, x1, dy, mo, gain, mod8)


def _norm1_bwd(dh, x, dx1, gain, mod8):
    T = x.shape[0]
    br = ROWS_EW

    def body(dh_ref, x_ref, dx1_ref, g_ref, m_ref, dx_ref, st_ref):
        i = pl.program_id(0)
        gain_v, scale = g_ref[...], m_ref[1:2, :]
        _, xhat, n, rstd = _rms_fwd_math(x_ref[...], gain_v, scale, m_ref[0:1, :])
        dx, d_scale, d_shift, d_gain = _rms_bwd_math(dh_ref[...], xhat, n, rstd, gain_v, scale)
        dx_ref[...] = dx1_ref[...] + dx

        @pl.when(i == 0)
        def _():
            st_ref[...] = jnp.zeros_like(st_ref)

        st_ref[0:1, :] += d_scale
        st_ref[1:2, :] += d_shift
        st_ref[2:3, :] += d_gain

    return _pcall(
        body, name="norm1_bwd", grid=(T // br,),
        in_specs=[_row_spec(br=br), _row_spec(br=br), _row_spec(br=br), _vec_spec(1), _vec_spec()],
        out_specs=[_row_spec(br=br), _vec_spec()],
        out_shape=[jax.ShapeDtypeStruct((T, D_MODEL), F32), jax.ShapeDtypeStruct((8, D_MODEL), F32)],
        compiler_params=_params(("arbitrary",)),
    )(dh, x, dx1, gain, mod8)


MERGE_BC = 512


def _hgrn_rows(T):
    return 512 if T >= 1024 else 128


def _lower_bound(lbl):
    e = jnp.exp(lbl - jnp.max(lbl, axis=0, keepdims=True))
    return e[0:1, :] / (e[0:1, :] + e[1:2, :])


def _chunk_sum_matrix(rows, backward):
    shift = A_CHUNK.bit_length() - 1
    r = lax.broadcasted_iota(jnp.int32, (rows, rows), 0)
    c = lax.broadcasted_iota(jnp.int32, (rows, rows), 1)
    same = jnp.right_shift(r, shift) == jnp.right_shift(c, shift)
    return (same & ((r <= c) if backward else (r >= c))).astype(BF16)


def _chunk_sums(m, x):
    n = x.shape[1]
    hi = x.astype(BF16)
    rest = x - hi.astype(F32)
    mid = rest.astype(BF16)
    lo = (rest - mid.astype(F32)).astype(BF16)
    y = jnp.dot(m, jnp.concatenate([hi, mid, lo], axis=1), preferred_element_type=F32)
    return y[:, 0:n] + y[:, n:2 * n] + y[:, 2 * n:3 * n]


def _hgrn_block_pre(q, fl, lb, m_fwd):
    sg = _sig(fl)
    f = lb + (1.0 - lb) * sg
    sq = _sig(q)
    return dict(sg=sg, f=f, k=1.0 - f, sq=sq, qf=q * sq, b=_chunk_sums(m_fwd, jnp.log(f)))


def _hgrn_chunk_local(pre, r):
    C = A_CHUNK
    qf, k, b = pre["qf"][r], pre["k"][r], pre["b"][r]
    causal = lax.broadcasted_iota(jnp.int32, (C, C), 0) >= lax.broadcasted_iota(jnp.int32, (C, C), 1)
    bm = b[C // 2 - 1:C // 2, :]
    bl = b[C - 1:C, :]
    e_q, e_k = jnp.exp(b - bm), jnp.exp(bm - b)
    e_b, e_l = jnp.exp(b), jnp.exp(bl - b)
    qd, kd = qf * e_q, k * e_k
    qe, ke = qf * e_b, k * e_l
    att = jnp.where(causal, _nt(qd, kd), 0.0)
    return dict(causal=causal, e_q=e_q, e_k=e_k, e_b=e_b, e_l=e_l, qd=qd, kd=kd, qe=qe, ke=ke, att=att, dec=jnp.exp(bl))


def _hgrn_chunk_fwd(pre, r, v, st):
    c = _hgrn_chunk_local(pre, r)
    c["o"] = _nn(c["att"], v) + _nt(c["qe"], st)
    return c


def _lockstep(gens):
    out = [None] * len(gens)
    live = list(enumerate(gens))
    while live:
        still = []
        for i, g in live:
            try:
                next(g)
                still.append((i, g))
            except StopIteration as done:
                out[i] = done.value
        live = still
    return out


HGRN_HEADS_PER_STEP = 4


def _hgrn_fwd(proj, lb_logits, o_gain, plan=None):
    T = proj.shape[0]
    BR = _hgrn_rows(T)
    cps = BR // A_CHUNK
    K, NH = A_HEAD_DIM, HGRN_HEADS_PER_STEP
    W = NH * K

    def col(off):
        return pl.BlockSpec((BR, W), lambda h, cb: (cb, off // W + h))

    def body(q_ref, f_ref, i_ref, g_ref, lbl_ref, og_ref, o_ref, s_ref, st):
        @pl.when(pl.program_id(1) == 0)
        def _():
            st[...] = jnp.zeros_like(st)

        lb_all = _lower_bound(lbl_ref[...])
        m_fwd = _chunk_sum_matrix(BR, False)
        pre = [_hgrn_block_pre(q_ref[:, n * K:(n + 1) * K], f_ref[:, n * K:(n + 1) * K], lb_all[:, n * K:(n + 1) * K], m_fwd)
               for n in range(NH)]
        def local(n, ci):
            r, hs = slice(ci * A_CHUNK, (ci + 1) * A_CHUNK), slice(n * K, (n + 1) * K)
            v = i_ref[r, hs]
            c = _hgrn_chunk_local(pre[n], r)
            yield
            return dict(o=_nn(c["att"], v), ds=_tn(v, c["ke"]), qe=c["qe"], dec=c["dec"])

        def chain(n, loc):
            hs = slice(n * K, (n + 1) * K)
            state = st[n]
            for ci, p in enumerate(loc):
                r = slice(ci * A_CHUNK, (ci + 1) * A_CHUNK)
                s_ref[n, ci] = state
                o = p["o"] + _nt(p["qe"], state)
                state = state * p["dec"] + p["ds"]
                yield
                on = o * lax.rsqrt(jnp.mean(o * o, axis=-1, keepdims=True) + EPS)
                g = g_ref[r, hs]
                o_ref[r, hs] = (on * og_ref[:, hs] * (g * _sig(g))).astype(BF16)
            st[n] = state

        loc = _lockstep([local(n, ci) for n in range(NH) for ci in range(cps)])
        _lockstep([chain(n, loc[n * cps:(n + 1) * cps]) for n in range(NH)])

    return _pcall(
        body, plan=plan, name="hgrn_fwd", grid=(A_HEADS // NH, T // BR),
        in_specs=[col(OFF_QA), col(OFF_FA), col(OFF_IA), col(OFF_GA),
                  pl.BlockSpec((2, W), lambda h, cb: (0, h)), pl.BlockSpec((1, W), lambda h, cb: (0, h))],
        out_specs=[pl.BlockSpec((BR, W), lambda h, cb: (cb, h)),
                   pl.BlockSpec((NH, cps, K, K), lambda h, cb: (h, cb, 0, 0))],
        out_shape=[jax.ShapeDtypeStruct((T, A_WIDTH), BF16),
                   jax.ShapeDtypeStruct((A_HEADS, T // A_CHUNK, K, K), F32)],
        scratch_shapes=[pltpu.VMEM((NH, K, K), F32)],
        compiler_params=_params(("parallel", "arbitrary")),
    )(proj, proj, proj, proj, lb_logits, o_gain)


def _hgrn_bwd(proj, lb_logits, o_gain, states, do, plan=None):
    T = proj.shape[0]
    BR = _hgrn_rows(T)
    cps = BR // A_CHUNK
    ncb = T // BR
    K, C, NH = A_HEAD_DIM, A_CHUNK, HGRN_HEADS_PER_STEP
    W = NH * K

    def col(off):
        return pl.BlockSpec((BR, W), lambda h, cb: (ncb - 1 - cb, off // W + h))

    def body(q_ref, f_ref, i_ref, g_ref, lbl_ref, og_ref, s_ref, do_ref,
             dq_ref, df_ref, di_ref, dg_ref, dlb_ref, dog_ref, dst):
        @pl.when(pl.program_id(1) == 0)
        def _():
            dst[...] = jnp.zeros_like(dst)
            dlb_ref[...] = jnp.zeros_like(dlb_ref)
            dog_ref[...] = jnp.zeros_like(dog_ref)

        lb_all = _lower_bound(lbl_ref[...])
        row = lax.broadcasted_iota(jnp.int32, (C, K), 0)
        m_fwd, m_bwd = _chunk_sum_matrix(BR, False), _chunk_sum_matrix(BR, True)
        pre = [_hgrn_block_pre(q_ref[:, n * K:(n + 1) * K], f_ref[:, n * K:(n + 1) * K], lb_all[:, n * K:(n + 1) * K], m_fwd)
               for n in range(NH)]
        def local(n, ci):
            r, hs = slice(ci * C, (ci + 1) * C), slice(n * K, (n + 1) * K)
            gain = og_ref[:, hs]
            st = s_ref[n, ci]
            v = i_ref[r, hs]
            q = q_ref[r, hs]
            c = _hgrn_chunk_fwd(pre[n], r, v, st)
            yield
            o = c["o"]
            rn = lax.rsqrt(jnp.mean(o * o, axis=-1, keepdims=True) + EPS)
            on = o * rn
            g = g_ref[r, hs]
            sgg = _sig(g)
            dy = do_ref[r, hs]
            d_ong = dy * (g * sgg)
            dg_ref[r, hs] = (dy * (on * gain) * (sgg * (1.0 + g * (1.0 - sgg)))).astype(BF16)
            d_on = d_ong * gain
            d_o = rn * (d_on - on * jnp.mean(d_on * on, axis=-1, keepdims=True))
            datt = jnp.where(c["causal"], _nt(d_o, v), 0.0)
            dqe = _nn(d_o, st)
            yield
            dqd = _nn(datt, c["kd"])
            dkd = _tn(datt, c["qd"])
            dv = _tn(c["att"], d_o)
            ds = _tn(d_o, c["qe"])
            yield
            t_q, t_k = dqd * c["qd"], dkd * c["kd"]
            sq = pre[n]["sq"][r]
            dq_ref[r, hs] = ((dqd * c["e_q"] + dqe * c["e_b"]) * (sq * (1.0 + q * (1.0 - sq)))).astype(BF16)
            return dict(v=v, st=st, ke=c["ke"], e_l=c["e_l"], dec=c["dec"], dv=dv, ds=ds, dk=dkd * c["e_k"],
                        db=t_q - t_k + dqe * c["qe"], dbm=jnp.sum(t_k - t_q, axis=0, keepdims=True),
                        d_og=jnp.sum(d_ong * on, axis=0, keepdims=True))

        def chain(n, loc):
            hs = slice(n * K, (n + 1) * K)
            dst_next = dst[n]
            db_of, dk_of = [None] * cps, [None] * cps
            for ci in reversed(range(cps)):
                p = loc[ci]
                di_ref[ci * C:(ci + 1) * C, hs] = (p["dv"] + _nt(p["ke"], dst_next)).astype(BF16)
                dke = _nn(p["v"], dst_next)
                yield
                t_l = dke * p["ke"]
                dbl = jnp.sum(t_l, axis=0, keepdims=True) + jnp.sum(dst_next * p["st"], axis=0, keepdims=True) * p["dec"]
                db_of[ci] = p["db"] - t_l + jnp.where(row == C // 2 - 1, p["dbm"], 0.0) + jnp.where(row == C - 1, dbl, 0.0)
                dk_of[ci] = p["dk"] + dke * p["e_l"]
                dst_next = dst_next * p["dec"] + p["ds"]
            dst[n] = dst_next
            return db_of, dk_of

        loc = _lockstep([local(n, ci) for n in range(NH) for ci in range(cps)])
        loc = [loc[n * cps:(n + 1) * cps] for n in range(NH)]
        chains = _lockstep([chain(n, loc[n]) for n in range(NH)])
        for n in range(NH):
            hs = slice(n * K, (n + 1) * K)
            db_of, dk_of = chains[n]
            d_og = loc[n][0]["d_og"]
            for p in loc[n][1:]:
                d_og = d_og + p["d_og"]
            dog_ref[0:1, hs] += d_og
            lb, sg = lb_all[:, hs], pre[n]["sg"]
            dlf = _chunk_sums(m_bwd, jnp.concatenate(db_of, axis=0))
            df = dlf / pre[n]["f"] - jnp.concatenate(dk_of, axis=0)
            df_ref[:, hs] = (df * (1.0 - lb) * sg * (1.0 - sg)).astype(BF16)
            dlb_ref[0:1, hs] += jnp.sum(df * (1.0 - sg), axis=0, keepdims=True)

    ocol = pl.BlockSpec((BR, W), lambda h, cb: (ncb - 1 - cb, h))
    vec = pl.BlockSpec((8, W), lambda h, cb: (0, h))
    return _pcall(
        body, plan=plan, name="hgrn_bwd", grid=(A_HEADS // NH, ncb),
        in_specs=[col(OFF_QA), col(OFF_FA), col(OFF_IA), col(OFF_GA),
                  pl.BlockSpec((2, W), lambda h, cb: (0, h)), pl.BlockSpec((1, W), lambda h, cb: (0, h)),
                  pl.BlockSpec((NH, cps, K, K), lambda h, cb: (h, ncb - 1 - cb, 0, 0)),
                  pl.BlockSpec((BR, W), lambda h, cb: (ncb - 1 - cb, h))],
        out_specs=[ocol, ocol, ocol, ocol, vec, vec],
        out_shape=[jax.ShapeDtypeStruct((T, A_WIDTH), BF16)] * 4 + [jax.ShapeDtypeStruct((8, A_WIDTH), F32)] * 2,
        scratch_shapes=[pltpu.VMEM((NH, K, K), F32)],
        compiler_params=_params(("parallel", "arbitrary")),
    )(proj, proj, proj, proj, lb_logits, o_gain, states, do)


def _head_norm(x):
    r = lax.rsqrt(jnp.mean(x * x, axis=-1, keepdims=True) + EPS)
    return x * r, r


def _head_norm_bwd(dy, xn, r, gain):
    dxn = dy * gain
    return r * (dxn - xn * jnp.mean(dxn * xn, axis=-1, keepdims=True)), jnp.sum(dy * xn, axis=0, keepdims=True)


def _swa_mask(has_prev):
    rows = B_GROUP * BLOCK
    r = lax.broadcasted_iota(jnp.int32, (rows, 2 * BLOCK), 0) % BLOCK
    c = lax.broadcasted_iota(jnp.int32, (rows, 2 * BLOCK), 1)
    rel = r + BLOCK - c
    return (rel >= 0) & (rel < BLOCK) & ((c >= BLOCK) | has_prev)


def _swa_head_fwd(j, q_ref, kp_ref, kc_ref, vp_ref, vc_ref, qg, kg, sk_ref, mask):
    hs = slice(j * B_HEAD_DIM, (j + 1) * B_HEAD_DIM)
    kcat = jnp.concatenate([kp_ref[:, hs], kc_ref[:, hs]], axis=0)
    vcat = jnp.concatenate([vp_ref[:, hs], vc_ref[:, hs]], axis=0)
    qs = jnp.concatenate([q_ref[:, pl.ds((j * B_GROUP + g) * B_HEAD_DIM, B_HEAD_DIM)] for g in range(B_GROUP)], axis=0)
    kn, kr = _head_norm(kcat)
    qn, qr = _head_norm(qs)
    kh, qh = kn * kg, qn * qg
    yield
    s = jnp.where(mask, _nt(qh, kh) * (B_HEAD_DIM ** -0.5), NEG_BIG)
    yield
    sink = jnp.concatenate(
        [jnp.broadcast_to(sk_ref[0:1, pl.ds(j * B_GROUP + g, 1)], (BLOCK, 1)) for g in range(B_GROUP)], axis=0)
    m = jnp.maximum(jnp.max(s, axis=-1, keepdims=True), sink)
    p = jnp.exp(s - m)
    e_sink = jnp.exp(sink - m)
    inv = 1.0 / (jnp.sum(p, axis=-1, keepdims=True) + e_sink)
    prob = p * inv
    return dict(vcat=vcat, kn=kn, kr=kr, qn=qn, qr=qr, kh=kh, qh=qh, prob=prob, p_sink=e_sink * inv)


def _swa_in_specs(nb, last):
    def qi(n):
        return jnp.minimum(n, last)

    q = pl.BlockSpec((BLOCK, B_WIDTH), lambda n: (qi(n), OFF_QB // B_WIDTH))
    kc = pl.BlockSpec((BLOCK, B_KV_WIDTH), lambda n: (qi(n), OFF_KB // B_KV_WIDTH))
    kp = pl.BlockSpec((BLOCK, B_KV_WIDTH), lambda n: (jnp.maximum(qi(n) - 1, 0), OFF_KB // B_KV_WIDTH))
    vc = pl.BlockSpec((BLOCK, B_KV_WIDTH), lambda n: (qi(n), OFF_VB // B_KV_WIDTH))
    vp = pl.BlockSpec((BLOCK, B_KV_WIDTH), lambda n: (jnp.maximum(qi(n) - 1, 0), OFF_VB // B_KV_WIDTH))
    small = [pl.BlockSpec((1, B_HEAD_DIM), lambda n: (0, 0)), pl.BlockSpec((1, B_HEAD_DIM), lambda n: (0, 0)),
             pl.BlockSpec((1, B_GROUP * B_KV_HEADS), lambda n: (0, 0))]
    return [q, kp, kc, vp, vc] + small


def _swa_fwd(proj, q_gain, k_gain, sinks, plan=None):
    T = proj.shape[0]
    nb = T // BLOCK

    def body(q_ref, kp_ref, kc_ref, vp_ref, vc_ref, qg_ref, kg_ref, sk_ref, o_ref):
        mask = _swa_mask(pl.program_id(0) > 0)

        def head(j):
            c = yield from _swa_head_fwd(j, q_ref, kp_ref, kc_ref, vp_ref, vc_ref, qg_ref[...], kg_ref[...], sk_ref, mask)
            yield
            o = _nn(c["prob"], c["vcat"])
            yield
            for g in range(B_GROUP):
                o_ref[:, pl.ds((j * B_GROUP + g) * B_HEAD_DIM, B_HEAD_DIM)] = o[g * BLOCK:(g + 1) * BLOCK].astype(BF16)

        _lockstep([head(j) for j in range(B_KV_HEADS)])

    return _pcall(
        body, plan=plan, name="swa_fwd", grid=(nb,),
        in_specs=_swa_in_specs(nb, nb - 1),
        out_specs=pl.BlockSpec((BLOCK, B_WIDTH), lambda n: (n, 0)),
        out_shape=jax.ShapeDtypeStruct((T, B_WIDTH), BF16),
        compiler_params=_params(("parallel",)),
    )(proj, proj, proj, proj, proj, q_gain, k_gain, sinks)


def _swa_bwd(proj, q_gain, k_gain, sinks, do, plan=None):
    T = proj.shape[0]
    nb = T // BLOCK
    scale = B_HEAD_DIM ** -0.5

    def body(q_ref, kp_ref, kc_ref, vp_ref, vc_ref, qg_ref, kg_ref, sk_ref, do_ref,
             dq_ref, dkv_ref, sm_ref, ck, cv):
        n = pl.program_id(0)

        @pl.when(n == 0)
        def _():
            ck[...] = jnp.zeros_like(ck)
            cv[...] = jnp.zeros_like(cv)
            sm_ref[...] = jnp.zeros_like(sm_ref)

        @pl.when(n < nb)
        def _():
            mask = _swa_mask(n > 0)
            qg, kg = qg_ref[...], kg_ref[...]
            lane = lax.broadcasted_iota(jnp.int32, (1, BLOCK), 1)
            def head(j):
                hs = slice(j * B_HEAD_DIM, (j + 1) * B_HEAD_DIM)
                vs = slice(B_KV_WIDTH + j * B_HEAD_DIM, B_KV_WIDTH + (j + 1) * B_HEAD_DIM)
                c = yield from _swa_head_fwd(j, q_ref, kp_ref, kc_ref, vp_ref, vc_ref, qg, kg, sk_ref, mask)
                d_out = jnp.concatenate(
                    [do_ref[:, pl.ds((j * B_GROUP + g) * B_HEAD_DIM, B_HEAD_DIM)] for g in range(B_GROUP)], axis=0)
                prob = c["prob"]
                yield
                out = _nn(prob, c["vcat"])
                d_prob = _nt(d_out, c["vcat"])
                dv = _tn(prob, d_out)
                yield
                delta = jnp.sum(d_out * out, axis=-1, keepdims=True)
                ds = prob * (d_prob - delta)
                d_sink = -c["p_sink"] * delta
                yield
                dqh = _nn(ds, c["kh"]) * scale
                dkh = _tn(ds, c["qh"]) * scale
                yield
                dq, dqg = _head_norm_bwd(dqh, c["qn"], c["qr"], qg)
                dk, dkg = _head_norm_bwd(dkh, c["kn"], c["kr"], kg)
                d_sinks = jnp.zeros((1, BLOCK), F32)
                for g in range(B_GROUP):
                    dq_ref[:, pl.ds((j * B_GROUP + g) * B_HEAD_DIM, B_HEAD_DIM)] = dq[g * BLOCK:(g + 1) * BLOCK].astype(BF16)
                    tot = jnp.sum(d_sink[g * BLOCK:(g + 1) * BLOCK], axis=0, keepdims=True)
                    d_sinks = d_sinks + jnp.where(lane == j * B_GROUP + g, tot, 0.0)
                dkv_ref[:, hs] = (ck[:, hs] + dk[0:BLOCK]).astype(BF16)
                dkv_ref[:, vs] = (cv[:, hs] + dv[0:BLOCK]).astype(BF16)
                ck[:, hs] = dk[BLOCK:2 * BLOCK]
                cv[:, hs] = dv[BLOCK:2 * BLOCK]
                return dqg, dkg, d_sinks

            small = _lockstep([head(j) for j in range(B_KV_HEADS)])
            sm_ref[0:1, 0:B_HEAD_DIM] += small[0][0] + small[1][0] + small[2][0] + small[3][0]
            sm_ref[1:2, 0:B_HEAD_DIM] += small[0][1] + small[1][1] + small[2][1] + small[3][1]
            sm_ref[2:3, :] += small[0][2] + small[1][2] + small[2][2] + small[3][2]

        @pl.when(n == nb)
        def _():
            dkv_ref[:, 0:B_KV_WIDTH] = ck[...].astype(BF16)
            dkv_ref[:, B_KV_WIDTH:2 * B_KV_WIDTH] = cv[...].astype(BF16)

    return _pcall(
        body, plan=plan, name="swa_bwd", grid=(nb + 1,),
        in_specs=_swa_in_specs(nb, nb - 1) + [pl.BlockSpec((BLOCK, B_WIDTH), lambda n: (jnp.minimum(n, nb - 1), 0))],
        out_specs=[pl.BlockSpec((BLOCK, B_WIDTH), lambda n: (jnp.minimum(n, nb - 1), 0)),
                   pl.BlockSpec((BLOCK, 2 * B_KV_WIDTH), lambda n: (jnp.maximum(n - 1, 0), 0)),
                   pl.BlockSpec((8, BLOCK), lambda n: (0, 0))],
        out_shape=[jax.ShapeDtypeStruct((T, B_WIDTH), BF16), jax.ShapeDtypeStruct((T, 2 * B_KV_WIDTH), BF16),
                   jax.ShapeDtypeStruct((8, BLOCK), F32)],
        scratch_shapes=[pltpu.VMEM((BLOCK, B_KV_WIDTH), F32), pltpu.VMEM((BLOCK, B_KV_WIDTH), F32)],
        compiler_params=_params(("arbitrary",)),
    )(proj, proj, proj, proj, proj, q_gain, k_gain, sinks, do)


W_IN, W_A, W_B, W_OUT, W_MI, W_MO = range(6)


def _local_step(x, target, mod8, norm1_gain, norm2_gain, lb_logits, o_gain, q_gain, k_gain, sinks, shards, c_arr, chip_arr):
    relu2 = lambda u: (u, jnp.square(jnp.maximum(u, 0.0)))
    pair, half = {}, {}

    def exchange(ws, grads):
        return _sibling_exchange_plan([_grad_view(g, w) for w, g in zip(ws, grads)])

    def pair_sums(ws, grads, others):
        for w, g, o in zip(ws, grads, others):
            pair[w] = _pair_sum(_grad_view(g, w), o, c_arr, f"pair_sum{w}")

    def sum_slots(ws, slots):
        for w, s in zip(ws, slots):
            half[w] = _sum_slots(pair[w], s, w, chip_arr, f"sum_slots{w}")

    part_in = _cast_into_full({W_IN: shards[W_IN]}, "cast_w_in")[W_IN]
    h, (part_in,) = _norm1_fwd(x, norm1_gain, mod8, plan=_gather_plan({W_IN: part_in}, part="near"))
    parts, (w_in,) = _cast_into_full({w: shards[w] for w in range(1, N_W)}, "cast_rest",
                                     plan=_gather_plan({W_IN: part_in}, pass_at=(0.97,), part="far"))
    proj, (w_mi,) = _mm(h, w_in, name="mm_proj", bn=512, plan=_gather_plan({W_MI: parts[W_MI]}, pass_at=(0.47, 0.72)))
    (o_a, states), (w_a, w_b) = _hgrn_fwd(
        proj, lb_logits, o_gain, plan=_gather_plan({w: parts[w] for w in (W_A, W_B)}, pass_at=(0.4, 0.65)))
    o_b, (w_out,) = _swa_fwd(proj, q_gain, k_gain, sinks, plan=_gather_plan({W_OUT: parts[W_OUT]}, pass_at=(0.3, 0.5)))
    ya = _mm(o_a, w_a, name="mm_branch_a")
    gate_cols = (OFF_GATE_A // MERGE_BC, OFF_GATE_B // MERGE_BC)
    yb, merged = _mm(o_b, w_b, name="mm_branch_b", bn=MERGE_BC, out_dtypes=(F32, BF16),
                     extras=(proj, proj, ya), extra_cols=gate_cols + (0,),
                     epi=lambda acc, ga, gb, ya_: (acc, _sig(ga) * ya_ + _sig(gb) * acc))
    mo = _mm(merged, w_out, name="mm_out")
    x1, h2 = _res_norm2_fwd(x, mo, norm2_gain, mod8)
    (u, act), (w_mo,) = _mm(h2, w_mi, name="mm_mlp_in", out_dtypes=(F32, BF16), epi=relu2,
                            plan=_gather_plan({W_MO: parts[W_MO]}, pass_at=(0.6, 0.9)))
    dy, dmlp, st_loss = _mm(act, w_mo, name="mm_mlp_out", bm=512, out_dtypes=(F32, BF16, F32), n_stats=1,
                            extras=(x1, target), row_extras=(mod8,), epi=_loss_head)
    st_loss = st_loss.reshape(-1, 8, D_MODEL).sum(axis=0)
    def half_blocks(w, own):
        def block(i):
            return 2 * i + (lax.axis_index("c") if own else 1 - lax.axis_index("c"))
        return (1 if W_SHAPES[w][2] else N_CHIPS), block

    def pair_of(w, lhs, rhs, other, name):
        hr, cols = _half_shape(w)
        p = _mm(lhs, rhs, name=name, ta=True, bn=512, a_blocks=half_blocks(w, True), out_dtypes=(BF16,),
                extras=(other,), epi=lambda acc, o: (acc + o,))
        return p.reshape(-1, hr, W_SHAPES[w][1])

    near, far = (0, 1), (2,)
    g_send = _mm(act, dmlp, name="mm_g_mlp_out_send", ta=True, bn=512, a_blocks=half_blocks(W_MO, False))
    du, (g_other,) = _mm(dmlp, w_mo, name="mm_d_act", tb=True, out_dtypes=(BF16,), extras=(u,),
                         epi=lambda acc, uu: (acc * (2.0 * jnp.maximum(uu, 0.0)),), plan=_sibling_share_plan([g_send]))
    pair[W_MO] = pair_of(W_MO, act, dmlp, g_other, "mm_g_mlp_out_own")
    g_send, (part,) = _mm(h2, du, name="mm_g_mlp_in_send", ta=True, bn=512, a_blocks=half_blocks(W_MI, False),
                          plan=_chip_exchange_plan({W_MO: pair[W_MO]}, near))
    dh2, res = _mm(du, w_mi, name="mm_d_h2", tb=True,
                   plan=_join(_chip_exchange_plan({W_MO: pair[W_MO]}, far, {W_MO: part}), _sibling_share_plan([g_send])))
    sum_slots([W_MO], res[:1])
    pair[W_MI] = pair_of(W_MI, h2, du, res[1], "mm_g_mlp_in_own")
    dx1, dmo, st_n2 = _norm2_bwd(dh2, x1, dy, mo, norm2_gain, mod8)
    def merge_bwd(dm, ga, gb, ya_, yb_):
        sa, sb = _sig(ga), _sig(gb)
        return dm * sa, dm * sb, dm * ya_ * sa * (1.0 - sa), dm * yb_ * sb * (1.0 - sb)

    dya, dyb, dga, dgb = _mm(dmo, w_out, name="mm_d_merged", tb=True, bn=MERGE_BC, out_dtypes=(BF16,) * 4,
                             extras=(proj, proj, ya, yb), extra_cols=gate_cols + (0, 0), epi=merge_bwd)
    g_out = _mm(merged, dmo, name="mm_g_out", ta=True, bn=512)
    do_a = _mm(dya, w_a, name="mm_d_oa", tb=True)
    g_a = _mm(o_a, dya, name="mm_g_branch_a", ta=True, bn=512)
    do_b = _mm(dyb, w_b, name="mm_d_ob", tb=True)
    g_b = _mm(o_b, dyb, name="mm_g_branch_b", ta=True, bn=512)
    mid = [W_A, W_B, W_OUT]
    (dqb, dkvb, st_swa), res = _swa_bwd(
        proj, q_gain, k_gain, sinks, do_b,
        plan=_join(_chip_exchange_plan({W_MI: pair[W_MI]}), exchange(mid, [g_a, g_b, g_out])))
    sum_slots([W_MI], res[:1])
    pair_sums(mid, [g_a, g_b, g_out], res[1:])
    (dqa, dfa, dia, dgga, d_lb, d_og), slots_mid = _hgrn_bwd(
        proj, lb_logits, o_gain, states, do_a, plan=_chip_exchange_plan({w: pair[w] for w in mid}))
    sum_slots(mid, slots_mid)
    dproj = jnp.concatenate([dqa, dfa, dia, dgga, dqb, dkvb, dga, dgb], axis=1)
    done = [W_A, W_B, W_OUT, W_MI, W_MO]
    g_send = _mm(h, dproj, name="mm_g_in_send", ta=True, bn=512, a_blocks=half_blocks(W_IN, False))
    g_own, res = _mm(h, dproj, name="mm_g_in_own", ta=True, bn=512, a_blocks=half_blocks(W_IN, True),
                     plan=_sibling_share_plan([g_send] + [half[w] for w in done]))
    g_other, theirs = res[0], dict(zip(done, res[1:]))
    pair[W_IN] = _add_bf16(g_own, g_other, "pair_sum0")[None]
    dh, slots_in = _mm(dproj, w_in, name="mm_d_h", tb=True, bk=2432, plan=_chip_exchange_plan({W_IN: pair[W_IN]}))
    sum_slots([W_IN], slots_in)
    grad_x, st_n1 = _norm1_bwd(dh, x, dx1, norm1_gain, mod8)
    (theirs[W_IN],) = _run_plan(_sibling_share_plan([half[W_IN]]), "sibling_share_w_in")
    stats = dict(loss=st_loss, n2=st_n2, n1=st_n1, d_lb=d_lb, d_og=d_og, swa=st_swa)
    return grad_x, [half[w] for w in range(N_W)], [theirs[w] for w in range(N_W)], stats


EW_VMEM_BYTES = 40 << 20


def _ew_rows(rows, cols, streams):
    br = 8
    while br * 2 * 4 <= rows and br * 2 * cols * 4 * 2 * streams <= EW_VMEM_BYTES and rows % (br * 2) == 0:
        br *= 2
    return br


CAST_STEPS = 16


def _cast_into_full(shards, name, plan=None):
    ws = sorted(shards)
    in_specs, out_specs, out_shape = [], [], []
    for w in ws:
        sr, sc = shards[w].shape
        R, C, by_col = W_SHAPES[w]
        br = sr // CAST_STEPS
        assert br * CAST_STEPS == sr and br % 16 == 0, (w, sr)

        def out_map(i, by_col=by_col):
            chip = 2 * lax.axis_index("x") + lax.axis_index("y")
            return (i, chip) if by_col else (chip * CAST_STEPS + i, 0)

        in_specs.append(pl.BlockSpec((br, sc), lambda i: (i, 0)))
        out_specs.append(pl.BlockSpec((br, sc), out_map))
        out_shape.append(jax.ShapeDtypeStruct((R, C), BF16))

    def body(*refs):
        for w_ref, o_ref in zip(refs[:len(ws)], refs[len(ws):]):
            o_ref[...] = w_ref[...].astype(BF16)

    res = _pcall(body, plan=plan, name=name, grid=(CAST_STEPS,), in_specs=in_specs, out_specs=out_specs,
                 out_shape=out_shape, compiler_params=_params(("arbitrary",)))(*[shards[w] for w in ws])
    if plan is None:
        return dict(zip(ws, res))
    return dict(zip(ws, res[0])), res[1]


def _adamw_math(w, g, m, v):
    m = ADAM_B1 * m + (1.0 - ADAM_B1) * g
    v = ADAM_B2 * v + (1.0 - ADAM_B2) * (g * g)
    m_hat = m / (1.0 - ADAM_B1 ** ADAM_STEP)
    v_hat = v / (1.0 - ADAM_B2 ** ADAM_STEP)
    delta = -ADAM_LR * (m_hat / (jnp.sqrt(v_hat) + ADAM_EPS) + ADAM_WD * w)
    return delta, m, v


def _adamw(w, g, m, v, name):
    R, C = w.shape
    br = _ew_rows(R, C, 7)
    spec = pl.BlockSpec((br, C), lambda i: (i, 0))

    def body(w_ref, g_ref, m_ref, v_ref, d_ref, nm_ref, nv_ref):
        d_ref[...], nm_ref[...], nv_ref[...] = _adamw_math(w_ref[...], g_ref[...], m_ref[...], v_ref[...])

    sh = jax.ShapeDtypeStruct((R, C), F32)
    return _pcall(body, name=name, grid=(R // br,), in_specs=[spec] * 4, out_specs=[spec] * 3, out_shape=[sh] * 3,
                  compiler_params=_params(("parallel",)))(w, g, m, v)


def _add_bf16(a, b, name):
    R, C = a.shape
    br = _ew_rows(R, C, 2.5)
    spec = pl.BlockSpec((br, C), lambda i: (i, 0))

    def body(a_ref, b_ref, o_ref):
        o_ref[...] = (a_ref[...] + b_ref[...]).astype(BF16)

    return _pcall(body, name=name, grid=(R // br,), in_specs=[spec, spec], out_specs=spec,
                  out_shape=jax.ShapeDtypeStruct((R, C), BF16), compiler_params=_params(("parallel",)))(a, b)


def _adamw_halves(w, own, other, m, v, c_arr, name):
    R, C = w.shape
    hr = R // 2
    br = _ew_rows(hr, C, 9)
    nb = hr // br
    full = pl.BlockSpec((br, C), lambda h, i, c_ref: (h * nb + i, 0))
    half = pl.BlockSpec((br, C), lambda h, i, c_ref: (i, 0))

    def body(c_ref, w_ref, own_ref, oth_ref, m_ref, v_ref, g_ref, d_ref, nm_ref, nv_ref):
        g = jnp.where(pl.program_id(0) == c_ref[0], own_ref[...], oth_ref[...])
        g_ref[...] = g
        d_ref[...], nm_ref[...], nv_ref[...] = _adamw_math(w_ref[...], g, m_ref[...], v_ref[...])

    sh = jax.ShapeDtypeStruct((R, C), F32)
    return _pcall(
        body, name=name,
        grid_spec=pltpu.PrefetchScalarGridSpec(
            num_scalar_prefetch=1, grid=(2, nb), in_specs=[full, half, half, full, full], out_specs=[full] * 4),
        out_shape=[sh] * 4, compiler_params=_params(("parallel", "parallel")))(c_arr, w, own, other, m, v)


def _ada_grad_adamw(c_t, dmod, w, m, v):
    R, C = w.shape
    br = _ew_rows(R, C, 8)
    spec = pl.BlockSpec((br, C), lambda i: (i, 0))

    def body(c_ref, dm_ref, w_ref, m_ref, v_ref, g_ref, d_ref, nm_ref, nv_ref):
        cv = c_ref[...]
        sc = cv * _sig(cv)
        g = sc[:, 0:1] * dm_ref[0:1, :]
        for b in range(1, N_DEV):
            g = g + sc[:, b:b + 1] * dm_ref[b:b + 1, :]
        g_ref[...] = g
        d_ref[...], nm_ref[...], nv_ref[...] = _adamw_math(w_ref[...], g, m_ref[...], v_ref[...])

    sh = jax.ShapeDtypeStruct((R, C), F32)
    return _pcall(
        body, name="ada_grad_adamw", grid=(R // br,),
        in_specs=[pl.BlockSpec((br, N_DEV), lambda i: (i, 0)), pl.BlockSpec((N_DEV, C), lambda i: (0, 0)), spec, spec, spec],
        out_specs=[spec] * 4, out_shape=[sh] * 4, compiler_params=_params(("parallel",)))(c_t, dmod, w, m, v)


SMALL_ROWS = 16


def _small_sum(small_all, lb_logits):
    def body(s_ref, lbl_ref, o_ref):
        acc = s_ref[0:SMALL_ROWS, :]
        for d in range(1, N_DEV):
            acc = acc + s_ref[d * SMALL_ROWS:(d + 1) * SMALL_ROWS, :]
        o_ref[...] = acc
        z = lbl_ref[...]
        e = jnp.exp(z - jnp.max(z, axis=0, keepdims=True))
        p0 = e[0:1, :] / (e[0:1, :] + e[1:2, :])
        dz = acc[8:9, 0:A_WIDTH] * p0 * (1.0 - p0)
        o_ref[8:9, 0:A_WIDTH] = dz
        o_ref[10:11, 0:A_WIDTH] = -dz

    return _pcall(body, name="small_sum", out_shape=jax.ShapeDtypeStruct((SMALL_ROWS, D_MODEL), F32),
                  in_specs=[pl.BlockSpec(memory_space=pltpu.VMEM)] * 2, out_specs=pl.BlockSpec(memory_space=pltpu.VMEM),
                  compiler_params=_params())(small_all, lb_logits)


RELATIONS = ((1, 0), (0, 1), (1, 1))
ANY = pl.BlockSpec(memory_space=pl.ANY)


def _place():
    x, y, c = lax.axis_index("x"), lax.axis_index("y"), lax.axis_index("c")
    return x, y, c


def _allgather_small(x_shard, name):
    m_per, n = x_shard.shape

    def body(x_ref, out_ref, send_sems, recv_sems, local_sem):
        x, y, c = _place()
        me, sibling = (x, y, c), (x, y, 1 - c)
        chips = [(1 - x, y), (x, 1 - y), (1 - x, 1 - y)]

        def rows(px, py, pc):
            return out_ref.at[pl.ds((4 * px + 2 * py + pc) * m_per, m_per), :]

        def copy(k, block, to, src=None):
            return pltpu.make_async_remote_copy(
                src_ref=rows(*block) if src is None else src, dst_ref=rows(*block),
                send_sem=send_sems.at[k], recv_sem=recv_sems.at[k], device_id=to, device_id_type=MESH)

        mine = pltpu.make_async_copy(x_ref, rows(*me), local_sem)
        mine.start()
        first = [copy(0, me, sibling, src=x_ref)]
        first += [copy(1 + j, me, (*chip, c), src=x_ref) for j, chip in enumerate(chips)]
        for cp in first:
            cp.start()
        passed = [copy(4 + j, (*chip, c), sibling) for j, chip in enumerate(chips)]
        for j, chip in enumerate(chips):
            copy(1 + j, (*chip, c), me).wait_recv()
            passed[j].start()
        copy(0, sibling, me).wait_recv()
        for j, chip in enumerate(chips):
            copy(4 + j, (*chip, 1 - c), me).wait_recv()
        for cp in first + passed:
            cp.wait_send()
        mine.wait()

    return _pcall(
        body, name=name, out_shape=jax.ShapeDtypeStruct((N_DEV * m_per, n), x_shard.dtype),
        in_specs=[pl.BlockSpec(memory_space=pltpu.VMEM)], out_specs=pl.BlockSpec(memory_space=pltpu.VMEM),
        scratch_shapes=[pltpu.SemaphoreType.DMA((7,)), pltpu.SemaphoreType.DMA((7,)), pltpu.SemaphoreType.DMA],
        compiler_params=_params(),
    )(x_shard)


W_SHAPES = ((D_MODEL, IN_WIDTH, True), (A_WIDTH, D_MODEL, True), (B_WIDTH, D_MODEL, True),
            (D_MODEL, D_MODEL, False), (D_MODEL, MLP_HIDDEN, True), (MLP_HIDDEN, D_MODEL, False))
N_W = len(W_SHAPES)


def _shard_shape(w):
    R, C, by_col = W_SHAPES[w]
    return (R, C // N_CHIPS) if by_col else (R // N_CHIPS, C)


def _half_shape(w):
    sr, sc = _shard_shape(w)
    return sr // 2, sc


def _region(full_ref, w, chip, half, quarter=None):
    sr, sc = _shard_shape(w)
    by_col = W_SHAPES[w][2]
    r0, c0 = (0, chip * sc) if by_col else (chip * sr, 0)
    r0, rows = r0 + half * (sr // 2), sr // 2
    if quarter is not None:
        r0, rows = r0 + quarter * (rows // 2), rows // 2
    return full_ref.at[pl.ds(r0, rows), pl.ds(c0, sc)]


def _on_device(fn):
    x, y, c = _place()
    me = 4 * x + 2 * y + c
    for d in range(N_DEV):
        @pl.when(me == d)
        def _(d=d):
            fn(x, y, c, d)


GATHER_COPIES = (
    (0, 0, None, "x"), (0, 0, None, "y"),
    (1, 2, 0, "y"), (1, 1, 1, "x"),
    (1, 2, None, "s"), (1, 1, None, "s"),
    (2, 3, 0, "s"), (2, 3, 1, "s"),
)
PEER_FLIP = {"x": 2, "y": 1, "s": 0}


GATHER_STAGES = {
    None: (((), (0, 1), ()), ((0, 1), (2, 3, 4, 5), ()), ((2, 3), (6, 7), ()), ((4, 5, 6, 7), (), tuple(range(8)))),
    "near": (((), (0, 1), ()), ((0, 1), (), (0, 1))),
    "far": (((), (2, 3, 4, 5), ()), ((2, 3), (6, 7), ()), ((4, 5, 6, 7), (), (2, 3, 4, 5, 6, 7))),
}


def _gather_plan(partials, pass_at=(0.5, 0.75), part=None):
    ws = sorted(partials)
    n_t = len(GATHER_COPIES)
    jobs = [(i, w) for i, w in enumerate(ws)]

    def copy(pi, po, ps, x, y, c, d, i, w, t, landing):
        chip, dc = d >> 1, d & 1
        stage, flip, quarter, to = GATHER_COPIES[t]
        if landing:
            peer_chip = chip ^ PEER_FLIP[to]
            part = _region(po[i], w, peer_chip ^ flip, (1 - dc) if to == "s" else dc, quarter)
            src = part
        else:
            part = _region(po[i], w, chip ^ flip, dc, quarter)
            here = flip != 0 and (part_of is None or stage == 2)
            src = part if here else _region(pi[i], w, chip ^ flip, dc, quarter)
        target = {"x": (x ^ 1, y, c), "y": (x, y ^ 1, c), "s": (x, y, 1 - c)}[to]
        return pltpu.make_async_remote_copy(
            src_ref=src, dst_ref=part, send_sem=ps[0].at[i * n_t + t], recv_sem=ps[1].at[i * n_t + t],
            device_id=target, device_id_type=MESH)

    part_of = part

    def stage(landed, started, sent):
        def run(pi, po, ps):
            def on(x, y, c, d):
                for i, w in jobs:
                    for t in landed:
                        copy(pi, po, ps, x, y, c, d, i, w, t, True).wait_recv()
                for i, w in jobs:
                    for t in started:
                        copy(pi, po, ps, x, y, c, d, i, w, t, False).start()
                for i, w in jobs:
                    for t in sent:
                        copy(pi, po, ps, x, y, c, d, i, w, t, False).wait_send()
            _on_device(on)
        return run

    stages = [stage(*st) for st in GATHER_STAGES[part]]
    mid_at = tuple(pass_at) if part is None else tuple(pass_at)[:len(stages) - 2]
    return _Plan([partials[w] for w in ws], [jax.ShapeDtypeStruct(W_SHAPES[w][:2], BF16) for w in ws],
                 [pltpu.SemaphoreType.DMA((n_t * len(ws),)) for _ in range(2)], stages,
                 {i: i for i in range(len(ws))}, mid_at=mid_at)


def _grad_view(g, w):
    R, C, by_col = W_SHAPES[w]
    return g.reshape(1, 2, R // 2, C) if by_col else g.reshape(N_CHIPS, 2, R // N_CHIPS // 2, C)


def _start_wait_plan(ins, outs, n_copies, copies):
    def start(pi, po, ps):
        for cp in copies(pi, po, ps):
            cp.start()

    def finish(pi, po, ps):
        for cp in copies(pi, po, ps):
            cp.wait()

    return _Plan(ins, outs, [pltpu.SemaphoreType.DMA((n_copies,)), pltpu.SemaphoreType.DMA((n_copies,))], [start, finish])


def _sibling_exchange_plan(g4s):
    pieces = [(i, p) for i, g in enumerate(g4s) for p in range(g.shape[0])]

    def copies(pi, po, ps):
        x, y, c = _place()
        return [pltpu.make_async_remote_copy(
            src_ref=pi[i].at[p, 1 - c], dst_ref=po[i].at[p], send_sem=ps[0].at[n], recv_sem=ps[1].at[n],
            device_id=(x, y, 1 - c), device_id_type=MESH) for n, (i, p) in enumerate(pieces)]

    return _start_wait_plan(list(g4s), [jax.ShapeDtypeStruct((g.shape[0],) + g.shape[2:], F32) for g in g4s],
                            len(pieces), copies)


def _pair_sum(g4, other, c_arr, name):
    P, _, hr, C = g4.shape
    br = _ew_rows(hr, C, 2.5)

    def body(c_ref, g_ref, o_ref, p_ref):
        p_ref[...] = (g_ref[...] + o_ref[...]).astype(BF16)

    return _pcall(
        body, name=name,
        grid_spec=pltpu.PrefetchScalarGridSpec(
            num_scalar_prefetch=1, grid=(P, hr // br),
            in_specs=[pl.BlockSpec((None, None, br, C), lambda p, i, c_ref: (p, c_ref[0], i, 0)),
                      pl.BlockSpec((None, br, C), lambda p, i, c_ref: (p, i, 0))],
            out_specs=pl.BlockSpec((None, br, C), lambda p, i, c_ref: (p, i, 0))),
        out_shape=jax.ShapeDtypeStruct((P, hr, C), BF16),
        compiler_params=_params(("parallel", "parallel")),
    )(c_arr, g4, other)


def _pair_part(p_ref, w, chip):
    sr, sc = _shard_shape(w)
    return p_ref.at[0, :, pl.ds(chip * sc, sc)] if W_SHAPES[w][2] else p_ref.at[chip]


def _chip_exchange_plan(pairs, rels=(0, 1, 2), into=None):
    ws = sorted(pairs)
    n = len(ws)

    def stage(wait):
        def run(pi, po, ps):
            def on(x, y, c, d):
                for i, w in enumerate(ws):
                    for k, (rx, ry) in enumerate(RELATIONS):
                        if k not in rels:
                            continue
                        cp = pltpu.make_async_remote_copy(
                            src_ref=_pair_part(pi[i], w, (d >> 1) ^ (2 * rx + ry)), dst_ref=po[i].at[k],
                            send_sem=ps[0].at[i * 3 + k], recv_sem=ps[1].at[i * 3 + k],
                            device_id=(x ^ rx, y ^ ry, c), device_id_type=MESH)
                        if wait:
                            cp.wait()
                        else:
                            cp.start()
            _on_device(on)
        return run

    ins = [pairs[w] for w in ws] + ([into[w] for w in ws] if into else [])
    return _Plan(ins, [jax.ShapeDtypeStruct((3,) + _half_shape(w), BF16) for w in ws],
                 [pltpu.SemaphoreType.DMA((3 * n,)), pltpu.SemaphoreType.DMA((3 * n,))],
                 [stage(False), stage(True)], {n + i: i for i in range(n)} if into else None)


def _sum_slots(pair, slots, w, chip_arr, name):
    _, hr, C = slots.shape
    br = _ew_rows(hr, C, 3)
    own_map = (lambda i, chip: (0, i, chip[0])) if W_SHAPES[w][2] else (lambda i, chip: (chip[0], i, 0))

    def body(chip_ref, p_ref, s_ref, o_ref):
        acc = p_ref[...].astype(F32)
        for k in range(3):
            acc = acc + s_ref[k].astype(F32)
        o_ref[...] = acc

    return _pcall(
        body, name=name,
        grid_spec=pltpu.PrefetchScalarGridSpec(
            num_scalar_prefetch=1, grid=(hr // br,),
            in_specs=[pl.BlockSpec((None, br, C), own_map), pl.BlockSpec((3, br, C), lambda i, chip: (0, i, 0))],
            out_specs=pl.BlockSpec((br, C), lambda i, chip: (i, 0))),
        out_shape=jax.ShapeDtypeStruct((hr, C), F32), compiler_params=_params(("parallel",)),
    )(chip_arr, pair, slots)


def _sibling_share_plan(halves):
    def copies(pi, po, ps):
        x, y, c = _place()
        return [pltpu.make_async_remote_copy(
            src_ref=pi[i], dst_ref=po[i], send_sem=ps[0].at[i], recv_sem=ps[1].at[i],
            device_id=(x, y, 1 - c), device_id_type=MESH) for i in range(len(halves))]

    return _start_wait_plan(list(halves), [jax.ShapeDtypeStruct(h.shape, F32) for h in halves], len(halves), copies)


def _pad_lanes(v, width=D_MODEL):
    return jnp.pad(v, ((0, 0), (0, width - v.shape[1])))


def _pack_small(b_ada, norm1, norm2, lb, o_gain, q_gain, k_gain, sinks):
    rows = [b_ada.reshape(N_MOD, D_MODEL), norm1, norm2, jnp.concatenate([lb[0:1], o_gain], axis=1),
            _pad_lanes(jnp.concatenate([q_gain, k_gain, sinks], axis=1)), _pad_lanes(lb[1:2]),
            jnp.zeros((SMALL_ROWS - 11, D_MODEL), F32)]
    return jnp.concatenate(rows, axis=0)


def _unpack_small(p):
    return (p[0:6].reshape(1, N_MOD * D_MODEL), p[6:7], p[7:8],
            jnp.concatenate([p[8:9, 0:A_WIDTH], p[10:11, 0:A_WIDTH]], axis=0), p[8:9, A_WIDTH:],
            p[9:10, 0:64], p[9:10, 64:128], p[9:10, 128:144])


def kernel(x, c, w_ada, b_ada, norm1_gain, w_in, lb_logits, hgrn_o_gain, q_norm_gain, k_norm_gain, sinks, w_branch_a, w_branch_b, w_out, norm2_gain, w_mlp_in, w_mlp_out, loss_target, m_w_ada, m_b_ada, m_norm1_gain, m_w_in, m_lb_logits, m_hgrn_o_gain, m_q_norm_gain, m_k_norm_gain, m_sinks, m_w_branch_a, m_w_branch_b, m_w_out, m_norm2_gain, m_w_mlp_in, m_w_mlp_out, v_w_ada, v_b_ada, v_norm1_gain, v_w_in, v_lb_logits, v_hgrn_o_gain, v_q_norm_gain, v_k_norm_gain, v_sinks, v_w_branch_a, v_w_branch_b, v_w_out, v_norm2_gain, v_w_mlp_in, v_w_mlp_out):
    xi, yi, ci = _place()
    chip = 2 * xi + yi
    me = 4 * xi + 2 * yi + ci
    ada_cols = w_ada.shape[2]

    c_all = _allgather_small(jnp.broadcast_to(c, (8, D_MODEL)), "gather_c").reshape(N_DEV, 8, D_MODEL)[:, 0]
    b_cols = lax.dynamic_slice(b_ada, (0, chip * ada_cols), (1, ada_cols))
    mod_part = _ada_fwd(c_all, w_ada[0], b_cols)
    mod_all = _allgather_small(mod_part, "gather_mod").reshape(N_CHIPS, 2, N_DEV, ada_cols)[:, 0]
    mod_mine = lax.dynamic_index_in_dim(mod_all, me, axis=1, keepdims=False).reshape(N_MOD, D_MODEL)
    mod8 = jnp.concatenate([mod_mine, jnp.zeros((2, D_MODEL), F32)], axis=0)

    shards = (w_in[0], w_branch_a[0], w_branch_b[0], w_out[0], w_mlp_in[0], w_mlp_out[0])
    chip_arr = chip.astype(jnp.int32).reshape(1)
    c_arr = ci.astype(jnp.int32).reshape(1)

    grad_x, halves, theirs, st = _local_step(x[0], loss_target[0], mod8, norm1_gain, norm2_gain, lb_logits, hgrn_o_gain,
                                             q_norm_gain, k_norm_gain, sinks, shards, c_arr, chip_arr)
    loss = lax.psum(0.5 * jnp.sum(st["loss"][0]) / D_MODEL, ("x", "y", "c"))
    moments = ((m_w_in, v_w_in), (m_w_branch_a, v_w_branch_a), (m_w_branch_b, v_w_branch_b), (m_w_out, v_w_out),
               (m_w_mlp_in, v_w_mlp_in), (m_w_mlp_out, v_w_mlp_out))
    big = [_adamw_halves(shards[w], halves[w], theirs[w], moments[w][0][0], moments[w][1][0], c_arr, f"adamw{w}")
           for w in range(N_W)]

    swa = st["swa"]
    small = jnp.concatenate([
        st["n1"][1:2], st["n1"][0:1], st["n2"][3:4], st["n2"][1:2], st["n2"][0:1], st["loss"][1:2],
        st["n1"][2:3], st["n2"][2:3], jnp.concatenate([st["d_lb"][0:1], st["d_og"][0:1]], axis=1),
        _pad_lanes(jnp.concatenate([swa[0:1, 0:64], swa[1:2, 0:64], swa[2:3, 0:16]], axis=1)),
        jnp.zeros((SMALL_ROWS - 10, D_MODEL), F32)], axis=0)
    small_all = _allgather_small(small, "gather_small")
    g_small = _small_sum(small_all, lb_logits)
    small_w = (b_ada, norm1_gain, norm2_gain, lb_logits, hgrn_o_gain, q_norm_gain, k_norm_gain, sinks)
    small_m = (m_b_ada, m_norm1_gain, m_norm2_gain, m_lb_logits, m_hgrn_o_gain, m_q_norm_gain, m_k_norm_gain, m_sinks)
    small_v = (v_b_ada, v_norm1_gain, v_norm2_gain, v_lb_logits, v_hgrn_o_gain, v_q_norm_gain, v_k_norm_gain, v_sinks)
    sm = [_unpack_small(t) for t in
          (g_small,) + tuple(_adamw(_pack_small(*small_w), g_small, _pack_small(*small_m), _pack_small(*small_v),
                                    "adamw_small"))]
    g_b, g_n1, g_n2, g_lb, g_og, g_qg, g_kg, g_sk = ([t[i] for t in sm] for i in range(8))

    dmod_all = small_all.reshape(N_DEV, SMALL_ROWS, D_MODEL)[:, 0:N_MOD].reshape(N_DEV, N_MOD * D_MODEL)
    dmod_cols = lax.dynamic_slice(dmod_all, (0, chip * ada_cols), (N_DEV, ada_cols))
    ada = _ada_grad_adamw(c_all.T, dmod_cols, w_ada[0], m_w_ada[0], v_w_ada[0])

    def ordered(k):
        lead = lambda a: a[None]
        return (lead(ada[k]), g_b[k], g_n1[k], lead(big[0][k]), g_lb[k], g_og[k], g_qg[k], g_kg[k], g_sk[k],
                lead(big[1][k]), lead(big[2][k]), lead(big[3][k]), g_n2[k], lead(big[4][k]), lead(big[5][k]))

    return (loss, grad_x[None]) + ordered(0) + ordered(1) + ordered(2) + ordered(3)
```

```python
import jax
import jax.numpy as jnp
from jax import lax
from jax.experimental import pallas as pl
from jax.experimental.pallas import tpu as pltpu

F32 = jnp.float32
BF16 = jnp.bfloat16
HIGHEST = lax.Precision.HIGHEST
MESH = pl.DeviceIdType.MESH

D_MODEL = 2048
A_WIDTH = 1024
A_HEADS = 8
A_HEAD_DIM = 128
A_CHUNK = 64
B_WIDTH = 1024
B_HEAD_DIM = 64
B_GROUP = 4
B_KV_HEADS = 4
B_KV_WIDTH = 256
BLOCK = 128
MLP_HIDDEN = 8192
IN_WIDTH = 9728
N_MOD = 6
EPS = 1e-6
N_CHIPS = 4
N_DEV = 8

OFF_QA, OFF_FA, OFF_IA, OFF_GA = 0, 1024, 2048, 3072
OFF_QB, OFF_KB, OFF_VB = 4096, 5120, 5376
OFF_GATE_A, OFF_GATE_B = 5632, 7680

ADAM_LR = 0.001
ADAM_B1 = 0.9
ADAM_B2 = 0.999
ADAM_EPS = 1e-08
ADAM_WD = 0.01
ADAM_STEP = 10

VMEM_LIMIT_V7X = 48 * 1024 * 1024
NEG_BIG = -1e30


def _params(sem=None, vmem=VMEM_LIMIT_V7X):
    return pltpu.CompilerParams(dimension_semantics=sem, vmem_limit_bytes=vmem)


class _Plan:
    def __init__(self, ins, outs, sems, stages, aliases=None, mid_at=()):
        self.ins, self.outs, self.sems, self.stages, self.aliases = ins, outs, sems, stages, aliases or {}
        self.mid_at = tuple(mid_at)
        assert len(self.mid_at) == len(stages) - 2


def _join(a, b):
    assert len(a.stages) == 2 and len(b.stages) == 2
    ni, no, ns = len(a.ins), len(a.outs), len(a.sems)

    def stage(k):
        def run(pi, po, ps):
            a.stages[k](pi[:ni], po[:no], ps[:ns])
            b.stages[k](pi[ni:], po[no:], ps[ns:])
        return run

    aliases = dict(a.aliases)
    aliases.update({ni + i: no + o for i, o in b.aliases.items()})
    return _Plan(a.ins + b.ins, a.outs + b.outs, a.sems + b.sems, [stage(0), stage(1)], aliases)


def _pcall(body, plan=None, **kw):
    if plan is None:
        return pl.pallas_call(body, **kw)
    grid = kw["grid"]
    single = not isinstance(kw["out_specs"], (list, tuple))
    in_specs = list(kw["in_specs"])
    out_specs = [kw["out_specs"]] if single else list(kw["out_specs"])
    out_shape = [kw["out_shape"]] if single else list(kw["out_shape"])
    scratch = list(kw.get("scratch_shapes", ()))
    n_in, n_out, n_scr = len(in_specs), len(out_specs), len(scratch)
    n_pi, n_po = len(plan.ins), len(plan.outs)
    total = 1
    for g in grid:
        total *= g
    n_st = len(plan.stages)

    def wrapped(*refs):
        o0 = n_in + n_pi
        s0 = o0 + n_out + n_po
        pi, po, ps = refs[n_in:o0], refs[o0 + n_out:s0], refs[s0 + n_scr:]
        lin = 0
        for d, g in enumerate(grid):
            lin = lin * g + pl.program_id(d)
        for si, frac in enumerate((0.0,) + plan.mid_at):
            @pl.when(lin == int(frac * (total - 1)))
            def _(si=si):
                plan.stages[si](pi, po, ps)
        body(*refs[:n_in], *refs[o0:o0 + n_out], *refs[s0:s0 + n_scr])

        @pl.when(lin == total - 1)
        def _():
            plan.stages[-1](pi, po, ps)

    any_spec = pl.BlockSpec(memory_space=pl.ANY)
    call = pl.pallas_call(
        wrapped, name=kw["name"], grid=grid, in_specs=in_specs + [any_spec] * n_pi,
        out_specs=out_specs + [any_spec] * n_po, out_shape=out_shape + list(plan.outs),
        scratch_shapes=scratch + list(plan.sems),
        input_output_aliases={n_in + i: n_out + o for i, o in plan.aliases.items()},
        compiler_params=_params(("arbitrary",) * len(grid)))

    def run(*args):
        res = call(*args, *plan.ins)
        outs = list(res[:n_out])
        return (outs[0] if single else outs), list(res[n_out:])

    return run


def _run_plan(plan, name):
    return _pcall(lambda: None, plan=plan, name=name, grid=(1,), in_specs=[], out_specs=[], out_shape=[])()[1]


def _sig(x):
    return 1.0 / (1.0 + jnp.exp(-x))


def _nn(a, b):
    return lax.dot_general(a.astype(BF16), b.astype(BF16), (((1,), (0,)), ((), ())), preferred_element_type=F32)


def _nt(a, b):
    return lax.dot_general(a.astype(BF16), b.astype(BF16), (((1,), (1,)), ((), ())), preferred_element_type=F32)


def _tn(a, b):
    return lax.dot_general(a.astype(BF16), b.astype(BF16), (((0,), (0,)), ((), ())), preferred_element_type=F32)


def _mm(a, b, *, name, ta=False, tb=False, bm=1024, bn=1024, bk=2048, out_dtypes=(F32,), epi=None, extras=(),
        extra_cols=None, plan=None, a_blocks=None, row_extras=(), n_stats=0):
    if ta:
        K, M = a.shape
        bk = K
        if a_blocks is not None:
            M = a_blocks[0] * bm
    else:
        M, K = a.shape
    if tb:
        N, K2 = b.shape
    else:
        K2, N = b.shape
    bm, bn, bk = min(bm, M), min(bn, N), min(bk, K)
    assert K == K2 and M % bm == 0 and N % bn == 0 and K % bk == 0, (name, a.shape, b.shape)
    nk = K // bk
    a_col = a_blocks[1] if a_blocks is not None else (lambda i: i)
    a_spec = pl.BlockSpec((bk, bm), lambda i, j, k: (k, a_col(i))) if ta else pl.BlockSpec((bm, bk), lambda i, j, k: (i, k))
    b_spec = pl.BlockSpec((bn, bk), lambda i, j, k: (j, k)) if tb else pl.BlockSpec((bk, bn), lambda i, j, k: (k, j))
    t_spec = pl.BlockSpec((bm, bn), lambda i, j, k: (i, j))
    extra_cols = extra_cols or (0,) * len(extras)
    e_specs = [pl.BlockSpec((bm, bn), lambda i, j, k, off=off: (i, off + j)) for off in extra_cols]
    e_specs += [pl.BlockSpec((8, bn), lambda i, j, k: (0, j)) for _ in row_extras]
    dims = (((1,), (1 if tb else 0,)), ((), ()))
    n_e, n_o = len(extras) + len(row_extras), len(out_dtypes)
    stat_spec = pl.BlockSpec((8, bn), lambda i, j, k: (i, j))

    def body(*refs):
        a_ref, b_ref = refs[0], refs[1]
        e_refs = refs[2:2 + n_e]
        o_refs = refs[2 + n_e:2 + n_e + n_o]

        def finish(acc):
            outs = (acc,) if epi is None else epi(acc, *[e[...] for e in e_refs])
            for o_ref, o in zip(o_refs, outs):
                o_ref[...] = o.astype(o_ref.dtype)

        if ta:
            at_ref = refs[-1]

            @pl.when(pl.program_id(1) == 0)
            def _():
                at_ref[...] = a_ref[...].T

            lhs = at_ref[...]
        else:
            lhs = a_ref[...].astype(BF16)
        part = lax.dot_general(lhs, b_ref[...].astype(BF16), dims, preferred_element_type=F32)
        if nk == 1:
            finish(part)
        else:
            acc_ref = refs[-1]
            k = pl.program_id(2)

            @pl.when(k == 0)
            def _():
                acc_ref[...] = part

            @pl.when(k > 0)
            def _():
                acc_ref[...] += part

            @pl.when(k == nk - 1)
            def _():
                finish(acc_ref[...])

    if ta:
        assert a.dtype == BF16 and nk == 1
        scratch = [pltpu.VMEM((bm, bk), BF16)]
    else:
        scratch = [pltpu.VMEM((bm, bn), F32)] if nk > 1 else []
    out = _pcall(
        body, plan=plan, name=name, grid=(M // bm, N // bn, nk),
        in_specs=[a_spec, b_spec] + e_specs,
        out_specs=[t_spec] * (n_o - n_stats) + [stat_spec] * n_stats,
        out_shape=[jax.ShapeDtypeStruct((M, N), dt) for dt in out_dtypes[:n_o - n_stats]]
        + [jax.ShapeDtypeStruct((8 * (M // bm), N), F32)] * n_stats,
        scratch_shapes=scratch,
        compiler_params=_params(("parallel", "arbitrary", "arbitrary")),
    )(a, b, *extras, *row_extras)
    if plan is not None:
        return (out[0][0] if n_o == 1 else out[0]), out[1]
    return out[0] if n_o == 1 else out


def _ada_fwd(c_all, w_ada, b_cols):
    n = w_ada.shape[1]
    bn = 512

    def body(c_ref, w_ref, b_ref, o_ref):
        cv = c_ref[...]
        sc = cv * _sig(cv)
        o_ref[...] = jnp.dot(sc, w_ref[...], precision=HIGHEST, preferred_element_type=F32) + b_ref[...]

    return _pcall(
        body, name="ada_fwd", grid=(n // bn,),
        in_specs=[pl.BlockSpec((N_DEV, D_MODEL), lambda j: (0, 0)), pl.BlockSpec((D_MODEL, bn), lambda j: (0, j)),
                  pl.BlockSpec((1, bn), lambda j: (0, j))],
        out_specs=pl.BlockSpec((N_DEV, bn), lambda j: (0, j)),
        out_shape=jax.ShapeDtypeStruct((N_DEV, n), F32),
        compiler_params=_params(("parallel",)),
    )(c_all, w_ada, b_cols)


ROWS_EW = 256


def _rms_fwd_math(x, gain, scale, shift):
    rstd = lax.rsqrt(jnp.mean(x * x, axis=-1, keepdims=True) + EPS)
    xhat = x * rstd
    n = xhat * gain
    return n * (1.0 + scale) + shift, xhat, n, rstd


def _rms_bwd_math(dh, xhat, n, rstd, gain, scale):
    dn = dh * (1.0 + scale)
    dxhat = dn * gain
    dx = rstd * (dxhat - xhat * jnp.mean(dxhat * xhat, axis=-1, keepdims=True))
    d_scale = jnp.sum(dh * n, axis=0, keepdims=True)
    d_shift = jnp.sum(dh, axis=0, keepdims=True)
    d_gain = jnp.sum(dn * xhat, axis=0, keepdims=True)
    return dx, d_scale, d_shift, d_gain


def _row_spec(w=D_MODEL, br=ROWS_EW):
    return pl.BlockSpec((br, w), lambda i: (i, 0))


def _vec_spec(r=8, w=D_MODEL):
    return pl.BlockSpec((r, w), lambda i: (0, 0))


def _norm1_fwd(x, gain, mod8, plan=None):
    T = x.shape[0]

    def body(x_ref, g_ref, m_ref, h_ref):
        h, _, _, _ = _rms_fwd_math(x_ref[...], g_ref[...], m_ref[1:2, :], m_ref[0:1, :])
        h_ref[...] = h.astype(BF16)

    return _pcall(
        body, plan=plan, name="norm1_fwd", grid=(T // ROWS_EW,),
        in_specs=[_row_spec(), _vec_spec(1), _vec_spec()],
        out_specs=_row_spec(), out_shape=jax.ShapeDtypeStruct((T, D_MODEL), BF16),
        compiler_params=_params(("parallel",)),
    )(x, gain, mod8)


def _res_norm2_fwd(x, mo, gain, mod8):
    T = x.shape[0]
    br = ROWS_EW

    def body(x_ref, mo_ref, g_ref, m_ref, x1_ref, h_ref):
        x1 = x_ref[...] + m_ref[2:3, :] * mo_ref[...]
        x1_ref[...] = x1
        h, _, _, _ = _rms_fwd_math(x1, g_ref[...], m_ref[4:5, :], m_ref[3:4, :])
        h_ref[...] = h.astype(BF16)

    return _pcall(
        body, name="res_norm2_fwd", grid=(T // br,),
        in_specs=[_row_spec(br=br), _row_spec(br=br), _vec_spec(1), _vec_spec()],
        out_specs=[_row_spec(br=br), _row_spec(br=br)],
        out_shape=[jax.ShapeDtypeStruct((T, D_MODEL), F32), jax.ShapeDtypeStruct((T, D_MODEL), BF16)],
        compiler_params=_params(("parallel",)),
    )(x, mo, gain, mod8)


def _loss_head(mlp, x1, target, mod):
    gate = mod[5:6, :]
    err = x1 + gate * mlp - target
    dy = err * (1.0 / D_MODEL)
    row = lax.broadcasted_iota(jnp.int32, (8, mlp.shape[1]), 0)
    stats = jnp.where(row == 0, jnp.sum(err * err, axis=0, keepdims=True),
                      jnp.where(row == 1, jnp.sum(dy * mlp, axis=0, keepdims=True), 0.0))
    return dy, dy * gate, stats


def _norm2_bwd(dh2, x1, dy, mo, gain, mod8):
    T = x1.shape[0]

    def body(dh_ref, x1_ref, dy_ref, mo_ref, g_ref, m_ref, dx1_ref, dmo_ref, st_ref):
        i = pl.program_id(0)
        gain_v, scale = g_ref[...], m_ref[4:5, :]
        _, xhat, n, rstd = _rms_fwd_math(x1_ref[...], gain_v, scale, m_ref[3:4, :])
        dx, d_scale, d_shift, d_gain = _rms_bwd_math(dh_ref[...], xhat, n, rstd, gain_v, scale)
        dx1 = dy_ref[...] + dx
        dx1_ref[...] = dx1
        dmo_ref[...] = (dx1 * m_ref[2:3, :]).astype(BF16)

        @pl.when(i == 0)
        def _():
            st_ref[...] = jnp.zeros_like(st_ref)

        st_ref[0:1, :] += d_scale
        st_ref[1:2, :] += d_shift
        st_ref[2:3, :] += d_gain
        st_ref[3:4, :] += jnp.sum(dx1 * mo_ref[...], axis=0, keepdims=True)

    return _pcall(
        body, name="norm2_bwd", grid=(T // ROWS_EW,),
        in_specs=[_row_spec(), _row_spec(), _row_spec(), _row_spec(), _vec_spec(1), _vec_spec()],
        out_specs=[_row_spec(), _row_spec(), _vec_spec()],
        out_shape=[jax.ShapeDtypeStruct((T, D_MODEL), F32), jax.ShapeDtypeStruct((T, D_MODEL), BF16),
                   jax.ShapeDtypeStruct((8, D_MODEL), F32)],
        compiler_params=_params(("arbitrary",)),
    )(dh2, x1, dy, mo, gain, mod8)


def _norm1_bwd(dh, x, dx1, gain, mod8):
    T = x.shape[0]
    br = ROWS_EW

    def body(dh_ref, x_ref, dx1_ref, g_ref, m_ref, dx_ref, st_ref):
        i = pl.program_id(0)
        gain_v, scale = g_ref[...], m_ref[1:2, :]
        _, xhat, n, rstd = _rms_fwd_math(x_ref[...], gain_v, scale, m_ref[0:1, :])
        dx, d_scale, d_shift, d_gain = _rms_bwd_math(dh_ref[...], xhat, n, rstd, gain_v, scale)
        dx_ref[...] = dx1_ref[...] + dx

        @pl.when(i == 0)
        def _():
            st_ref[...] = jnp.zeros_like(st_ref)

        st_ref[0:1, :] += d_scale
        st_ref[1:2, :] += d_shift
        st_ref[2:3, :] += d_gain

    return _pcall(
        body, name="norm1_bwd", grid=(T // br,),
        in_specs=[_row_spec(br=br), _row_spec(br=br), _row_spec(br=br), _vec_spec(1), _vec_spec()],
        out_specs=[_row_spec(br=br), _vec_spec()],
        out_shape=[jax.ShapeDtypeStruct((T, D_MODEL), F32), jax.ShapeDtypeStruct((8, D_MODEL), F32)],
        compiler_params=_params(("arbitrary",)),
    )(dh, x, dx1, gain, mod8)


MERGE_BC = 512


def _hgrn_rows(T):
    return 512 if T >= 1024 else 128


def _lower_bound(lbl):
    e = jnp.exp(lbl - jnp.max(lbl, axis=0, keepdims=True))
    return e[0:1, :] / (e[0:1, :] + e[1:2, :])


def _chunk_sum_matrix(rows, backward):
    shift = A_CHUNK.bit_length() - 1
    r = lax.broadcasted_iota(jnp.int32, (rows, rows), 0)
    c = lax.broadcasted_iota(jnp.int32, (rows, rows), 1)
    same = jnp.right_shift(r, shift) == jnp.right_shift(c, shift)
    return (same & ((r <= c) if backward else (r >= c))).astype(BF16)


def _chunk_sums(m, x):
    n = x.shape[1]
    hi = x.astype(BF16)
    rest = x - hi.astype(F32)
    mid = rest.astype(BF16)
    lo = (rest - mid.astype(F32)).astype(BF16)
    y = jnp.dot(m, jnp.concatenate([hi, mid, lo], axis=1), preferred_element_type=F32)
    return y[:, 0:n] + y[:, n:2 * n] + y[:, 2 * n:3 * n]


def _hgrn_block_pre(q, fl, lb, m_fwd):
    sg = _sig(fl)
    f = lb + (1.0 - lb) * sg
    sq = _sig(q)
    return dict(sg=sg, f=f, k=1.0 - f, sq=sq, qf=q * sq, b=_chunk_sums(m_fwd, jnp.log(f)))


def _hgrn_chunk_local(pre, r):
    C = A_CHUNK
    qf, k, b = pre["qf"][r], pre["k"][r], pre["b"][r]
    causal = lax.broadcasted_iota(jnp.int32, (C, C), 0) >= lax.broadcasted_iota(jnp.int32, (C, C), 1)
    bm = b[C // 2 - 1:C // 2, :]
    bl = b[C - 1:C, :]
    e_q, e_k = jnp.exp(b - bm), jnp.exp(bm - b)
    e_b, e_l = jnp.exp(b), jnp.exp(bl - b)
    qd, kd = qf * e_q, k * e_k
    qe, ke = qf * e_b, k * e_l
    att = jnp.where(causal, _nt(qd, kd), 0.0)
    return dict(causal=causal, e_q=e_q, e_k=e_k, e_b=e_b, e_l=e_l, qd=qd, kd=kd, qe=qe, ke=ke, att=att, dec=jnp.exp(bl))


def _hgrn_chunk_fwd(pre, r, v, st):
    c = _hgrn_chunk_local(pre, r)
    c["o"] = _nn(c["att"], v) + _nt(c["qe"], st)
    return c


def _lockstep(gens):
    out = [None] * len(gens)
    live = list(enumerate(gens))
    while live:
        still = []
        for i, g in live:
            try:
                next(g)
                still.append((i, g))
            except StopIteration as done:
                out[i] = done.value
        live = still
    return out


HGRN_HEADS_PER_STEP = 4


def _hgrn_fwd(proj, lb_logits, o_gain, plan=None):
    T = proj.shape[0]
    BR = _hgrn_rows(T)
    cps = BR // A_CHUNK
    K, NH = A_HEAD_DIM, HGRN_HEADS_PER_STEP
    W = NH * K

    def col(off):
        return pl.BlockSpec((BR, W), lambda h, cb: (cb, off // W + h))

    def body(q_ref, f_ref, i_ref, g_ref, lbl_ref, og_ref, o_ref, s_ref, st):
        @pl.when(pl.program_id(1) == 0)
        def _():
            st[...] = jnp.zeros_like(st)

        lb_all = _lower_bound(lbl_ref[...])
        m_fwd = _chunk_sum_matrix(BR, False)
        pre = [_hgrn_block_pre(q_ref[:, n * K:(n + 1) * K], f_ref[:, n * K:(n + 1) * K], lb_all[:, n * K:(n + 1) * K], m_fwd)
               for n in range(NH)]
        def local(n, ci):
            r, hs = slice(ci * A_CHUNK, (ci + 1) * A_CHUNK), slice(n * K, (n + 1) * K)
            v = i_ref[r, hs]
            c = _hgrn_chunk_local(pre[n], r)
            yield
            return dict(o=_nn(c["att"], v), ds=_tn(v, c["ke"]), qe=c["qe"], dec=c["dec"])

        def chain(n, loc):
            hs = slice(n * K, (n + 1) * K)
            state = st[n]
            for ci, p in enumerate(loc):
                r = slice(ci * A_CHUNK, (ci + 1) * A_CHUNK)
                s_ref[n, ci] = state
                o = p["o"] + _nt(p["qe"], state)
                state = state * p["dec"] + p["ds"]
                yield
                on = o * lax.rsqrt(jnp.mean(o * o, axis=-1, keepdims=True) + EPS)
                g = g_ref[r, hs]
                o_ref[r, hs] = (on * og_ref[:, hs] * (g * _sig(g))).astype(BF16)
            st[n] = state

        loc = _lockstep([local(n, ci) for n in range(NH) for ci in range(cps)])
        _lockstep([chain(n, loc[n * cps:(n + 1) * cps]) for n in range(NH)])

    return _pcall(
        body, plan=plan, name="hgrn_fwd", grid=(A_HEADS // NH, T // BR),
        in_specs=[col(OFF_QA), col(OFF_FA), col(OFF_IA), col(OFF_GA),
                  pl.BlockSpec((2, W), lambda h, cb: (0, h)), pl.BlockSpec((1, W), lambda h, cb: (0, h))],
        out_specs=[pl.BlockSpec((BR, W), lambda h, cb: (cb, h)),
                   pl.BlockSpec((NH, cps, K, K), lambda h, cb: (h, cb, 0, 0))],
        out_shape=[jax.ShapeDtypeStruct((T, A_WIDTH), BF16),
                   jax.ShapeDtypeStruct((A_HEADS, T // A_CHUNK, K, K), F32)],
        scratch_shapes=[pltpu.VMEM((NH, K, K), F32)],
        compiler_params=_params(("parallel", "arbitrary")),
    )(proj, proj, proj, proj, lb_logits, o_gain)


def _hgrn_bwd(proj, lb_logits, o_gain, states, do, plan=None):
    T = proj.shape[0]
    BR = _hgrn_rows(T)
    cps = BR // A_CHUNK
    ncb = T // BR
    K, C, NH = A_HEAD_DIM, A_CHUNK, HGRN_HEADS_PER_STEP
    W = NH * K

    def col(off):
        return pl.BlockSpec((BR, W), lambda h, cb: (ncb - 1 - cb, off // W + h))

    def body(q_ref, f_ref, i_ref, g_ref, lbl_ref, og_ref, s_ref, do_ref,
             dq_ref, df_ref, di_ref, dg_ref, dlb_ref, dog_ref, dst):
        @pl.when(pl.program_id(1) == 0)
        def _():
            dst[...] = jnp.zeros_like(dst)
            dlb_ref[...] = jnp.zeros_like(dlb_ref)
            dog_ref[...] = jnp.zeros_like(dog_ref)

        lb_all = _lower_bound(lbl_ref[...])
        row = lax.broadcasted_iota(jnp.int32, (C, K), 0)
        m_fwd, m_bwd = _chunk_sum_matrix(BR, False), _chunk_sum_matrix(BR, True)
        pre = [_hgrn_block_pre(q_ref[:, n * K:(n + 1) * K], f_ref[:, n * K:(n + 1) * K], lb_all[:, n * K:(n + 1) * K], m_fwd)
               for n in range(NH)]
        def local(n, ci):
            r, hs = slice(ci * C, (ci + 1) * C), slice(n * K, (n + 1) * K)
            gain = og_ref[:, hs]
            st = s_ref[n, ci]
            v = i_ref[r, hs]
            q = q_ref[r, hs]
            c = _hgrn_chunk_fwd(pre[n], r, v, st)
            yield
            o = c["o"]
            rn = lax.rsqrt(jnp.mean(o * o, axis=-1, keepdims=True) + EPS)
            on = o * rn
            g = g_ref[r, hs]
            sgg = _sig(g)
            dy = do_ref[r, hs]
            d_ong = dy * (g * sgg)
            dg_ref[r, hs] = (dy * (on * gain) * (sgg * (1.0 + g * (1.0 - sgg)))).astype(BF16)
            d_on = d_ong * gain
            d_o = rn * (d_on - on * jnp.mean(d_on * on, axis=-1, keepdims=True))
            datt = jnp.where(c["causal"], _nt(d_o, v), 0.0)
            dqe = _nn(d_o, st)
            yield
            dqd = _nn(datt, c["kd"])
            dkd = _tn(datt, c["qd"])
            dv = _tn(c["att"], d_o)
            ds = _tn(d_o, c["qe"])
            yield
            t_q, t_k = dqd * c["qd"], dkd * c["kd"]
            sq = pre[n]["sq"][r]
            dq_ref[r, hs] = ((dqd * c["e_q"] + dqe * c["e_b"]) * (sq * (1.0 + q * (1.0 - sq)))).astype(BF16)
            return dict(v=v, st=st, ke=c["ke"], e_l=c["e_l"], dec=c["dec"], dv=dv, ds=ds, dk=dkd * c["e_k"],
                        db=t_q - t_k + dqe * c["qe"], dbm=jnp.sum(t_k - t_q, axis=0, keepdims=True),
                        d_og=jnp.sum(d_ong * on, axis=0, keepdims=True))

        def chain(n, loc):
            hs = slice(n * K, (n + 1) * K)
            dst_next = dst[n]
            db_of, dk_of = [None] * cps, [None] * cps
            for ci in reversed(range(cps)):
                p = loc[ci]
                di_ref[ci * C:(ci + 1) * C, hs] = (p["dv"] + _nt(p["ke"], dst_next)).astype(BF16)
                dke = _nn(p["v"], dst_next)
                yield
                t_l = dke * p["ke"]
                dbl = jnp.sum(t_l, axis=0, keepdims=True) + jnp.sum(dst_next * p["st"], axis=0, keepdims=True) * p["dec"]
                db_of[ci] = p["db"] - t_l + jnp.where(row == C // 2 - 1, p["dbm"], 0.0) + jnp.where(row == C - 1, dbl, 0.0)
                dk_of[ci] = p["dk"] + dke * p["e_l"]
                dst_next = dst_next * p["dec"] + p["ds"]
            dst[n] = dst_next
            return db_of, dk_of

        loc = _lockstep([local(n, ci) for n in range(NH) for ci in range(cps)])
        loc = [loc[n * cps:(n + 1) * cps] for n in range(NH)]
        chains = _lockstep([chain(n, loc[n]) for n in range(NH)])
        for n in range(NH):
            hs = slice(n * K, (n + 1) * K)
            db_of, dk_of = chains[n]
            d_og = loc[n][0]["d_og"]
            for p in loc[n][1:]:
                d_og = d_og + p["d_og"]
            dog_ref[0:1, hs] += d_og
            lb, sg = lb_all[:, hs], pre[n]["sg"]
            dlf = _chunk_sums(m_bwd, jnp.concatenate(db_of, axis=0))
            df = dlf / pre[n]["f"] - jnp.concatenate(dk_of, axis=0)
            df_ref[:, hs] = (df * (1.0 - lb) * sg * (1.0 - sg)).astype(BF16)
            dlb_ref[0:1, hs] += jnp.sum(df * (1.0 - sg), axis=0, keepdims=True)

    ocol = pl.BlockSpec((BR, W), lambda h, cb: (ncb - 1 - cb, h))
    vec = pl.BlockSpec((8, W), lambda h, cb: (0, h))
    return _pcall(
        body, plan=plan, name="hgrn_bwd", grid=(A_HEADS // NH, ncb),
        in_specs=[col(OFF_QA), col(OFF_FA), col(OFF_IA), col(OFF_GA),
                  pl.BlockSpec((2, W), lambda h, cb: (0, h)), pl.BlockSpec((1, W), lambda h, cb: (0, h)),
                  pl.BlockSpec((NH, cps, K, K), lambda h, cb: (h, ncb - 1 - cb, 0, 0)),
                  pl.BlockSpec((BR, W), lambda h, cb: (ncb - 1 - cb, h))],
        out_specs=[ocol, ocol, ocol, ocol, vec, vec],
        out_shape=[jax.ShapeDtypeStruct((T, A_WIDTH), BF16)] * 4 + [jax.ShapeDtypeStruct((8, A_WIDTH), F32)] * 2,
        scratch_shapes=[pltpu.VMEM((NH, K, K), F32)],
        compiler_params=_params(("parallel", "arbitrary")),
    )(proj, proj, proj, proj, lb_logits, o_gain, states, do)


def _head_norm(x):
    r = lax.rsqrt(jnp.mean(x * x, axis=-1, keepdims=True) + EPS)
    return x * r, r


def _head_norm_bwd(dy, xn, r, gain):
    dxn = dy * gain
    return r * (dxn - xn * jnp.mean(dxn * xn, axis=-1, keepdims=True)), jnp.sum(dy * xn, axis=0, keepdims=True)


def _swa_mask(has_prev):
    rows = B_GROUP * BLOCK
    r = lax.broadcasted_iota(jnp.int32, (rows, 2 * BLOCK), 0) % BLOCK
    c = lax.broadcasted_iota(jnp.int32, (rows, 2 * BLOCK), 1)
    rel = r + BLOCK - c
    return (rel >= 0) & (rel < BLOCK) & ((c >= BLOCK) | has_prev)


def _swa_head_fwd(j, q_ref, kp_ref, kc_ref, vp_ref, vc_ref, qg, kg, sk_ref, mask):
    hs = slice(j * B_HEAD_DIM, (j + 1) * B_HEAD_DIM)
    kcat = jnp.concatenate([kp_ref[:, hs], kc_ref[:, hs]], axis=0)
    vcat = jnp.concatenate([vp_ref[:, hs], vc_ref[:, hs]], axis=0)
    qs = jnp.concatenate([q_ref[:, pl.ds((j * B_GROUP + g) * B_HEAD_DIM, B_HEAD_DIM)] for g in range(B_GROUP)], axis=0)
    kn, kr = _head_norm(kcat)
    qn, qr = _head_norm(qs)
    kh, qh = kn * kg, qn * qg
    yield
    s = jnp.where(mask, _nt(qh, kh) * (B_HEAD_DIM ** -0.5), NEG_BIG)
    yield
    sink = jnp.concatenate(
        [jnp.broadcast_to(sk_ref[0:1, pl.ds(j * B_GROUP + g, 1)], (BLOCK, 1)) for g in range(B_GROUP)], axis=0)
    m = jnp.maximum(jnp.max(s, axis=-1, keepdims=True), sink)
    p = jnp.exp(s - m)
    e_sink = jnp.exp(sink - m)
    inv = 1.0 / (jnp.sum(p, axis=-1, keepdims=True) + e_sink)
    prob = p * inv
    return dict(vcat=vcat, kn=kn, kr=kr, qn=qn, qr=qr, kh=kh, qh=qh, prob=prob, p_sink=e_sink * inv)


def _swa_in_specs(nb, last):
    def qi(n):
        return jnp.minimum(n, last)

    q = pl.BlockSpec((BLOCK, B_WIDTH), lambda n: (qi(n), OFF_QB // B_WIDTH))
    kc = pl.BlockSpec((BLOCK, B_KV_WIDTH), lambda n: (qi(n), OFF_KB // B_KV_WIDTH))
    kp = pl.BlockSpec((BLOCK, B_KV_WIDTH), lambda n: (jnp.maximum(qi(n) - 1, 0), OFF_KB // B_KV_WIDTH))
    vc = pl.BlockSpec((BLOCK, B_KV_WIDTH), lambda n: (qi(n), OFF_VB // B_KV_WIDTH))
    vp = pl.BlockSpec((BLOCK, B_KV_WIDTH), lambda n: (jnp.maximum(qi(n) - 1, 0), OFF_VB // B_KV_WIDTH))
    small = [pl.BlockSpec((1, B_HEAD_DIM), lambda n: (0, 0)), pl.BlockSpec((1, B_HEAD_DIM), lambda n: (0, 0)),
             pl.BlockSpec((1, B_GROUP * B_KV_HEADS), lambda n: (0, 0))]
    return [q, kp, kc, vp, vc] + small


def _swa_fwd(proj, q_gain, k_gain, sinks, plan=None):
    T = proj.shape[0]
    nb = T // BLOCK

    def body(q_ref, kp_ref, kc_ref, vp_ref, vc_ref, qg_ref, kg_ref, sk_ref, o_ref):
        mask = _swa_mask(pl.program_id(0) > 0)

        def head(j):
            c = yield from _swa_head_fwd(j, q_ref, kp_ref, kc_ref, vp_ref, vc_ref, qg_ref[...], kg_ref[...], sk_ref, mask)
            yield
            o = _nn(c["prob"], c["vcat"])
            yield
            for g in range(B_GROUP):
                o_ref[:, pl.ds((j * B_GROUP + g) * B_HEAD_DIM, B_HEAD_DIM)] = o[g * BLOCK:(g + 1) * BLOCK].astype(BF16)

        _lockstep([head(j) for j in range(B_KV_HEADS)])

    return _pcall(
        body, plan=plan, name="swa_fwd", grid=(nb,),
        in_specs=_swa_in_specs(nb, nb - 1),
        out_specs=pl.BlockSpec((BLOCK, B_WIDTH), lambda n: (n, 0)),
        out_shape=jax.ShapeDtypeStruct((T, B_WIDTH), BF16),
        compiler_params=_params(("parallel",)),
    )(proj, proj, proj, proj, proj, q_gain, k_gain, sinks)


def _swa_bwd(proj, q_gain, k_gain, sinks, do, plan=None):
    T = proj.shape[0]
    nb = T // BLOCK
    scale = B_HEAD_DIM ** -0.5

    def body(q_ref, kp_ref, kc_ref, vp_ref, vc_ref, qg_ref, kg_ref, sk_ref, do_ref,
             dq_ref, dkv_ref, sm_ref, ck, cv):
        n = pl.program_id(0)

        @pl.when(n == 0)
        def _():
            ck[...] = jnp.zeros_like(ck)
            cv[...] = jnp.zeros_like(cv)
            sm_ref[...] = jnp.zeros_like(sm_ref)

        @pl.when(n < nb)
        def _():
            mask = _swa_mask(n > 0)
            qg, kg = qg_ref[...], kg_ref[...]
            lane = lax.broadcasted_iota(jnp.int32, (1, BLOCK), 1)
            def head(j):
                hs = slice(j * B_HEAD_DIM, (j + 1) * B_HEAD_DIM)
                vs = slice(B_KV_WIDTH + j * B_HEAD_DIM, B_KV_WIDTH + (j + 1) * B_HEAD_DIM)
                c = yield from _swa_head_fwd(j, q_ref, kp_ref, kc_ref, vp_ref, vc_ref, qg, kg, sk_ref, mask)
                d_out = jnp.concatenate(
                    [do_ref[:, pl.ds((j * B_GROUP + g) * B_HEAD_DIM, B_HEAD_DIM)] for g in range(B_GROUP)], axis=0)
                prob = c["prob"]
                yield
                out = _nn(prob, c["vcat"])
                d_prob = _nt(d_out, c["vcat"])
                dv = _tn(prob, d_out)
                yield
                delta = jnp.sum(d_out * out, axis=-1, keepdims=True)
                ds = prob * (d_prob - delta)
                d_sink = -c["p_sink"] * delta
                yield
                dqh = _nn(ds, c["kh"]) * scale
                dkh = _tn(ds, c["qh"]) * scale
                yield
                dq, dqg = _head_norm_bwd(dqh, c["qn"], c["qr"], qg)
                dk, dkg = _head_norm_bwd(dkh, c["kn"], c["kr"], kg)
                d_sinks = jnp.zeros((1, BLOCK), F32)
                for g in range(B_GROUP):
                    dq_ref[:, pl.ds((j * B_GROUP + g) * B_HEAD_DIM, B_HEAD_DIM)] = dq[g * BLOCK:(g + 1) * BLOCK].astype(BF16)
                    tot = jnp.sum(d_sink[g * BLOCK:(g + 1) * BLOCK], axis=0, keepdims=True)
                    d_sinks = d_sinks + jnp.where(lane == j * B_GROUP + g, tot, 0.0)
                dkv_ref[:, hs] = (ck[:, hs] + dk[0:BLOCK]).astype(BF16)
                dkv_ref[:, vs] = (cv[:, hs] + dv[0:BLOCK]).astype(BF16)
                ck[:, hs] = dk[BLOCK:2 * BLOCK]
                cv[:, hs] = dv[BLOCK:2 * BLOCK]
                return dqg, dkg, d_sinks

            small = _lockstep([head(j) for j in range(B_KV_HEADS)])
            sm_ref[0:1, 0:B_HEAD_DIM] += small[0][0] + small[1][0] + small[2][0] + small[3][0]
            sm_ref[1:2, 0:B_HEAD_DIM] += small[0][1] + small[1][1] + small[2][1] + small[3][1]
            sm_ref[2:3, :] += small[0][2] + small[1][2] + small[2][2] + small[3][2]

        @pl.when(n == nb)
        def _():
            dkv_ref[:, 0:B_KV_WIDTH] = ck[...].astype(BF16)
            dkv_ref[:, B_KV_WIDTH:2 * B_KV_WIDTH] = cv[...].astype(BF16)

    return _pcall(
        body, plan=plan, name="swa_bwd", grid=(nb + 1,),
        in_specs=_swa_in_specs(nb, nb - 1) + [pl.BlockSpec((BLOCK, B_WIDTH), lambda n: (jnp.minimum(n, nb - 1), 0))],
        out_specs=[pl.BlockSpec((BLOCK, B_WIDTH), lambda n: (jnp.minimum(n, nb - 1), 0)),
                   pl.BlockSpec((BLOCK, 2 * B_KV_WIDTH), lambda n: (jnp.maximum(n - 1, 0), 0)),
                   pl.BlockSpec((8, BLOCK), lambda n: (0, 0))],
        out_shape=[jax.ShapeDtypeStruct((T, B_WIDTH), BF16), jax.ShapeDtypeStruct((T, 2 * B_KV_WIDTH), BF16),
                   jax.ShapeDtypeStruct((8, BLOCK), F32)],
        scratch_shapes=[pltpu.VMEM((BLOCK, B_KV_WIDTH), F32), pltpu.VMEM((BLOCK, B_KV_WIDTH), F32)],
        compiler_params=_params(("arbitrary",)),
    )(proj, proj, proj, proj, proj, q_gain, k_gain, sinks, do)


W_IN, W_A, W_B, W_OUT, W_MI, W_MO = range(6)


def _local_step(x, target, mod8, norm1_gain, norm2_gain, lb_logits, o_gain, q_gain, k_gain, sinks, shards, c_arr, chip_arr):
    relu2 = lambda u: (u, jnp.square(jnp.maximum(u, 0.0)))
    pair, half = {}, {}

    def exchange(ws, grads):
        return _sibling_exchange_plan([_grad_view(g, w) for w, g in zip(ws, grads)])

    def pair_sums(ws, grads, others):
        for w, g, o in zip(ws, grads, others):
            pair[w] = _pair_sum(_grad_view(g, w), o, c_arr, f"pair_sum{w}")

    def sum_slots(ws, slots):
        for w, s in zip(ws, slots):
            half[w] = _sum_slots(pair[w], s, w, chip_arr, f"sum_slots{w}")

    part_in = _cast_into_full({W_IN: shards[W_IN]}, "cast_w_in")[W_IN]
    h, (part_in,) = _norm1_fwd(x, norm1_gain, mod8, plan=_gather_plan({W_IN: part_in}, part="near"))
    parts, (w_in,) = _cast_into_full({w: shards[w] for w in range(1, N_W)}, "cast_rest",
                                     plan=_gather_plan({W_IN: part_in}, pass_at=(0.97,), part="far"))
    proj, (w_mi,) = _mm(h, w_in, name="mm_proj", bn=512, plan=_gather_plan({W_MI: parts[W_MI]}, pass_at=(0.47, 0.72)))
    (o_a, states), (w_a, w_b) = _hgrn_fwd(
        proj, lb_logits, o_gain, plan=_gather_plan({w: parts[w] for w in (W_A, W_B)}, pass_at=(0.4, 0.65)))
    o_b, (w_out,) = _swa_fwd(proj, q_gain, k_gain, sinks, plan=_gather_plan({W_OUT: parts[W_OUT]}, pass_at=(0.3, 0.5)))
    ya = _mm(o_a, w_a, name="mm_branch_a")
    gate_cols = (OFF_GATE_A // MERGE_BC, OFF_GATE_B // MERGE_BC)
    yb, merged = _mm(o_b, w_b, name="mm_branch_b", bn=MERGE_BC, out_dtypes=(F32, BF16),
                     extras=(proj, proj, ya), extra_cols=gate_cols + (0,),
                     epi=lambda acc, ga, gb, ya_: (acc, _sig(ga) * ya_ + _sig(gb) * acc))
    mo = _mm(merged, w_out, name="mm_out")
    x1, h2 = _res_norm2_fwd(x, mo, norm2_gain, mod8)
    (u, act), (w_mo,) = _mm(h2, w_mi, name="mm_mlp_in", out_dtypes=(F32, BF16), epi=relu2,
                            plan=_gather_plan({W_MO: parts[W_MO]}, pass_at=(0.6, 0.9)))
    dy, dmlp, st_loss = _mm(act, w_mo, name="mm_mlp_out", bm=512, out_dtypes=(F32, BF16, F32), n_stats=1,
                            extras=(x1, target), row_extras=(mod8,), epi=_loss_head)
    st_loss = st_loss.reshape(-1, 8, D_MODEL).sum(axis=0)
    def half_blocks(w, own):
        def block(i):
            return 2 * i + (lax.axis_index("c") if own else 1 - lax.axis_index("c"))
        return (1 if W_SHAPES[w][2] else N_CHIPS), block

    def pair_of(w, lhs, rhs, other, name):
        hr, cols = _half_shape(w)
        p = _mm(lhs, rhs, name=name, ta=True, bn=512, a_blocks=half_blocks(w, True), out_dtypes=(BF16,),
                extras=(other,), epi=lambda acc, o: (acc + o,))
        return p.reshape(-1, hr, W_SHAPES[w][1])

    near, far = (0, 1), (2,)
    g_send = _mm(act, dmlp, name="mm_g_mlp_out_send", ta=True, bn=512, a_blocks=half_blocks(W_MO, False))
    du, (g_other,) = _mm(dmlp, w_mo, name="mm_d_act", tb=True, out_dtypes=(BF16,), extras=(u,),
                         epi=lambda acc, uu: (acc * (2.0 * jnp.maximum(uu, 0.0)),), plan=_sibling_share_plan([g_send]))
    pair[W_MO] = pair_of(W_MO, act, dmlp, g_other, "mm_g_mlp_out_own")
    g_send, (part,) = _mm(h2, du, name="mm_g_mlp_in_send", ta=True, bn=512, a_blocks=half_blocks(W_MI, False),
                          plan=_chip_exchange_plan({W_MO: pair[W_MO]}, near))
    dh2, res = _mm(du, w_mi, name="mm_d_h2", tb=True,
                   plan=_join(_chip_exchange_plan({W_MO: pair[W_MO]}, far, {W_MO: part}), _sibling_share_plan([g_send])))
    sum_slots([W_MO], res[:1])
    pair[W_MI] = pair_of(W_MI, h2, du, res[1], "mm_g_mlp_in_own")
    dx1, dmo, st_n2 = _norm2_bwd(dh2, x1, dy, mo, norm2_gain, mod8)
    def merge_bwd(dm, ga, gb, ya_, yb_):
        sa, sb = _sig(ga), _sig(gb)
        return dm * sa, dm * sb, dm * ya_ * sa * (1.0 - sa), dm * yb_ * sb * (1.0 - sb)

    dya, dyb, dga, dgb = _mm(dmo, w_out, name="mm_d_merged", tb=True, bn=MERGE_BC, out_dtypes=(BF16,) * 4,
                             extras=(proj, proj, ya, yb), extra_cols=gate_cols + (0, 0), epi=merge_bwd)
    g_out = _mm(merged, dmo, name="mm_g_out", ta=True, bn=512)
    do_a = _mm(dya, w_a, name="mm_d_oa", tb=True)
    g_a = _mm(o_a, dya, name="mm_g_branch_a", ta=True, bn=512)
    do_b = _mm(dyb, w_b, name="mm_d_ob", tb=True)
    g_b = _mm(o_b, dyb, name="mm_g_branch_b", ta=True, bn=512)
    mid = [W_A, W_B, W_OUT]
    (dqb, dkvb, st_swa), res = _swa_bwd(
        proj, q_gain, k_gain, sinks, do_b,
        plan=_join(_chip_exchange_plan({W_MI: pair[W_MI]}), exchange(mid, [g_a, g_b, g_out])))
    sum_slots([W_MI], res[:1])
    pair_sums(mid, [g_a, g_b, g_out], res[1:])
    (dqa, dfa, dia, dgga, d_lb, d_og), slots_mid = _hgrn_bwd(
        proj, lb_logits, o_gain, states, do_a, plan=_chip_exchange_plan({w: pair[w] for w in mid}))
    sum_slots(mid, slots_mid)
    dproj = jnp.concatenate([dqa, dfa, dia, dgga, dqb, dkvb, dga, dgb], axis=1)
    done = [W_A, W_B, W_OUT, W_MI, W_MO]
    g_send = _mm(h, dproj, name="mm_g_in_send", ta=True, bn=512, a_blocks=half_blocks(W_IN, False))
    g_own, res = _mm(h, dproj, name="mm_g_in_own", ta=True, bn=512, a_blocks=half_blocks(W_IN, True),
                     plan=_sibling_share_plan([g_send] + [half[w] for w in done]))
    g_other, theirs = res[0], dict(zip(done, res[1:]))
    pair[W_IN] = _add_bf16(g_own, g_other, "pair_sum0")[None]
    dh, slots_in = _mm(dproj, w_in, name="mm_d_h", tb=True, bk=2432, plan=_chip_exchange_plan({W_IN: pair[W_IN]}))
    sum_slots([W_IN], slots_in)
    grad_x, st_n1 = _norm1_bwd(dh, x, dx1, norm1_gain, mod8)
    (theirs[W_IN],) = _run_plan(_sibling_share_plan([half[W_IN]]), "sibling_share_w_in")
    stats = dict(loss=st_loss, n2=st_n2, n1=st_n1, d_lb=d_lb, d_og=d_og, swa=st_swa)
    return grad_x, [half[w] for w in range(N_W)], [theirs[w] for w in range(N_W)], stats


EW_VMEM_BYTES = 40 << 20


def _ew_rows(rows, cols, streams):
    br = 8
    while br * 2 * 4 <= rows and br * 2 * cols * 4 * 2 * streams <= EW_VMEM_BYTES and rows % (br * 2) == 0:
        br *= 2
    return br


CAST_STEPS = 16


def _cast_into_full(shards, name, plan=None):
    ws = sorted(shards)
    in_specs, out_specs, out_shape = [], [], []
    for w in ws:
        sr, sc = shards[w].shape
        R, C, by_col = W_SHAPES[w]
        br = sr // CAST_STEPS
        assert br * CAST_STEPS == sr and br % 16 == 0, (w, sr)

        def out_map(i, by_col=by_col):
            chip = 2 * lax.axis_index("x") + lax.axis_index("y")
            return (i, chip) if by_col else (chip * CAST_STEPS + i, 0)

        in_specs.append(pl.BlockSpec((br, sc), lambda i: (i, 0)))
        out_specs.append(pl.BlockSpec((br, sc), out_map))
        out_shape.append(jax.ShapeDtypeStruct((R, C), BF16))

    def body(*refs):
        for w_ref, o_ref in zip(refs[:len(ws)], refs[len(ws):]):
            o_ref[...] = w_ref[...].astype(BF16)

    res = _pcall(body, plan=plan, name=name, grid=(CAST_STEPS,), in_specs=in_specs, out_specs=out_specs,
                 out_shape=out_shape, compiler_params=_params(("arbitrary",)))(*[shards[w] for w in ws])
    if plan is None:
        return dict(zip(ws, res))
    return dict(zip(ws, res[0])), res[1]


def _adamw_math(w, g, m, v):
    m = ADAM_B1 * m + (1.0 - ADAM_B1) * g
    v = ADAM_B2 * v + (1.0 - ADAM_B2) * (g * g)
    m_hat = m / (1.0 - ADAM_B1 ** ADAM_STEP)
    v_hat = v / (1.0 - ADAM_B2 ** ADAM_STEP)
    delta = -ADAM_LR * (m_hat / (jnp.sqrt(v_hat) + ADAM_EPS) + ADAM_WD * w)
    return delta, m, v


def _adamw(w, g, m, v, name):
    R, C = w.shape
    br = _ew_rows(R, C, 7)
    spec = pl.BlockSpec((br, C), lambda i: (i, 0))

    def body(w_ref, g_ref, m_ref, v_ref, d_ref, nm_ref, nv_ref):
        d_ref[...], nm_ref[...], nv_ref[...] = _adamw_math(w_ref[...], g_ref[...], m_ref[...], v_ref[...])

    sh = jax.ShapeDtypeStruct((R, C), F32)
    return _pcall(body, name=name, grid=(R // br,), in_specs=[spec] * 4, out_specs=[spec] * 3, out_shape=[sh] * 3,
                  compiler_params=_params(("parallel",)))(w, g, m, v)


def _add_bf16(a, b, name):
    R, C = a.shape
    br = _ew_rows(R, C, 2.5)
    spec = pl.BlockSpec((br, C), lambda i: (i, 0))

    def body(a_ref, b_ref, o_ref):
        o_ref[...] = (a_ref[...] + b_ref[...]).astype(BF16)

    return _pcall(body, name=name, grid=(R // br,), in_specs=[spec, spec], out_specs=spec,
                  out_shape=jax.ShapeDtypeStruct((R, C), BF16), compiler_params=_params(("parallel",)))(a, b)


def _adamw_halves(w, own, other, m, v, c_arr, name):
    R, C = w.shape
    hr = R // 2
    br = _ew_rows(hr, C, 9)
    nb = hr // br
    full = pl.BlockSpec((br, C), lambda h, i, c_ref: (h * nb + i, 0))

    def own_map(h, i, c_ref):
        return jnp.where(h == c_ref[0], i, jnp.where(c_ref[0] == 0, nb - 1, 0)), 0

    def other_map(h, i, c_ref):
        return jnp.where(h != c_ref[0], i, jnp.where(c_ref[0] == 0, 0, nb - 1)), 0

    def body(c_ref, w_ref, own_ref, oth_ref, m_ref, v_ref, g_ref, d_ref, nm_ref, nv_ref):
        g = jnp.where(pl.program_id(0) == c_ref[0], own_ref[...], oth_ref[...])
        g_ref[...] = g
        d_ref[...], nm_ref[...], nv_ref[...] = _adamw_math(w_ref[...], g, m_ref[...], v_ref[...])

    sh = jax.ShapeDtypeStruct((R, C), F32)
    return _pcall(
        body, name=name,
        grid_spec=pltpu.PrefetchScalarGridSpec(
            num_scalar_prefetch=1, grid=(2, nb),
            in_specs=[full, pl.BlockSpec((br, C), own_map), pl.BlockSpec((br, C), other_map), full, full],
            out_specs=[full] * 4),
        out_shape=[sh] * 4, compiler_params=_params(("arbitrary", "arbitrary")))(c_arr, w, own, other, m, v)


def _ada_grad_adamw(c_t, dmod, w, m, v):
    R, C = w.shape
    br = _ew_rows(R, C, 8)
    spec = pl.BlockSpec((br, C), lambda i: (i, 0))

    def body(c_ref, dm_ref, w_ref, m_ref, v_ref, g_ref, d_ref, nm_ref, nv_ref):
        cv = c_ref[...]
        sc = cv * _sig(cv)
        g = sc[:, 0:1] * dm_ref[0:1, :]
        for b in range(1, N_DEV):
            g = g + sc[:, b:b + 1] * dm_ref[b:b + 1, :]
        g_ref[...] = g
        d_ref[...], nm_ref[...], nv_ref[...] = _adamw_math(w_ref[...], g, m_ref[...], v_ref[...])

    sh = jax.ShapeDtypeStruct((R, C), F32)
    return _pcall(
        body, name="ada_grad_adamw", grid=(R // br,),
        in_specs=[pl.BlockSpec((br, N_DEV), lambda i: (i, 0)), pl.BlockSpec((N_DEV, C), lambda i: (0, 0)), spec, spec, spec],
        out_specs=[spec] * 4, out_shape=[sh] * 4, compiler_params=_params(("parallel",)))(c_t, dmod, w, m, v)


SMALL_ROWS = 16


def _small_sum(small_all, lb_logits):
    def body(s_ref, lbl_ref, o_ref):
        acc = s_ref[0:SMALL_ROWS, :]
        for d in range(1, N_DEV):
            acc = acc + s_ref[d * SMALL_ROWS:(d + 1) * SMALL_ROWS, :]
        o_ref[...] = acc
        z = lbl_ref[...]
        e = jnp.exp(z - jnp.max(z, axis=0, keepdims=True))
        p0 = e[0:1, :] / (e[0:1, :] + e[1:2, :])
        dz = acc[8:9, 0:A_WIDTH] * p0 * (1.0 - p0)
        o_ref[8:9, 0:A_WIDTH] = dz
        o_ref[10:11, 0:A_WIDTH] = -dz

    return _pcall(body, name="small_sum", out_shape=jax.ShapeDtypeStruct((SMALL_ROWS, D_MODEL), F32),
                  in_specs=[pl.BlockSpec(memory_space=pltpu.VMEM)] * 2, out_specs=pl.BlockSpec(memory_space=pltpu.VMEM),
                  compiler_params=_params())(small_all, lb_logits)


RELATIONS = ((1, 0), (0, 1), (1, 1))
ANY = pl.BlockSpec(memory_space=pl.ANY)


def _place():
    x, y, c = lax.axis_index("x"), lax.axis_index("y"), lax.axis_index("c")
    return x, y, c


def _allgather_small(x_shard, name):
    m_per, n = x_shard.shape

    def body(x_ref, out_ref, send_sems, recv_sems, local_sem):
        x, y, c = _place()
        me, sibling = (x, y, c), (x, y, 1 - c)
        chips = [(1 - x, y), (x, 1 - y), (1 - x, 1 - y)]

        def rows(px, py, pc):
            return out_ref.at[pl.ds((4 * px + 2 * py + pc) * m_per, m_per), :]

        def copy(k, block, to, src=None):
            return pltpu.make_async_remote_copy(
                src_ref=rows(*block) if src is None else src, dst_ref=rows(*block),
                send_sem=send_sems.at[k], recv_sem=recv_sems.at[k], device_id=to, device_id_type=MESH)

        mine = pltpu.make_async_copy(x_ref, rows(*me), local_sem)
        mine.start()
        first = [copy(0, me, sibling, src=x_ref)]
        first += [copy(1 + j, me, (*chip, c), src=x_ref) for j, chip in enumerate(chips)]
        for cp in first:
            cp.start()
        passed = [copy(4 + j, (*chip, c), sibling) for j, chip in enumerate(chips)]
        for j, chip in enumerate(chips):
            copy(1 + j, (*chip, c), me).wait_recv()
            passed[j].start()
        copy(0, sibling, me).wait_recv()
        for j, chip in enumerate(chips):
            copy(4 + j, (*chip, 1 - c), me).wait_recv()
        for cp in first + passed:
            cp.wait_send()
        mine.wait()

    return _pcall(
        body, name=name, out_shape=jax.ShapeDtypeStruct((N_DEV * m_per, n), x_shard.dtype),
        in_specs=[pl.BlockSpec(memory_space=pltpu.VMEM)], out_specs=pl.BlockSpec(memory_space=pltpu.VMEM),
        scratch_shapes=[pltpu.SemaphoreType.DMA((7,)), pltpu.SemaphoreType.DMA((7,)), pltpu.SemaphoreType.DMA],
        compiler_params=_params(),
    )(x_shard)


W_SHAPES = ((D_MODEL, IN_WIDTH, True), (A_WIDTH, D_MODEL, True), (B_WIDTH, D_MODEL, True),
            (D_MODEL, D_MODEL, False), (D_MODEL, MLP_HIDDEN, True), (MLP_HIDDEN, D_MODEL, False))
N_W = len(W_SHAPES)


def _shard_shape(w):
    R, C, by_col = W_SHAPES[w]
    return (R, C // N_CHIPS) if by_col else (R // N_CHIPS, C)


def _half_shape(w):
    sr, sc = _shard_shape(w)
    return sr // 2, sc


def _region(full_ref, w, chip, half, quarter=None):
    sr, sc = _shard_shape(w)
    by_col = W_SHAPES[w][2]
    r0, c0 = (0, chip * sc) if by_col else (chip * sr, 0)
    r0, rows = r0 + half * (sr // 2), sr // 2
    if quarter is not None:
        r0, rows = r0 + quarter * (rows // 2), rows // 2
    return full_ref.at[pl.ds(r0, rows), pl.ds(c0, sc)]


def _on_device(fn):
    x, y, c = _place()
    me = 4 * x + 2 * y + c
    for d in range(N_DEV):
        @pl.when(me == d)
        def _(d=d):
            fn(x, y, c, d)


GATHER_COPIES = (
    (0, 0, None, "x"), (0, 0, None, "y"),
    (1, 2, 0, "y"), (1, 1, 1, "x"),
    (1, 2, None, "s"), (1, 1, None, "s"),
    (2, 3, 0, "s"), (2, 3, 1, "s"),
)
PEER_FLIP = {"x": 2, "y": 1, "s": 0}


GATHER_STAGES = {
    None: (((), (0, 1), ()), ((0, 1), (2, 3, 4, 5), ()), ((2, 3), (6, 7), ()), ((4, 5, 6, 7), (), tuple(range(8)))),
    "near": (((), (0, 1), ()), ((0, 1), (), (0, 1))),
    "far": (((), (2, 3, 4, 5), ()), ((2, 3), (6, 7), ()), ((4, 5, 6, 7), (), (2, 3, 4, 5, 6, 7))),
}


def _gather_plan(partials, pass_at=(0.5, 0.75), part=None):
    ws = sorted(partials)
    n_t = len(GATHER_COPIES)
    jobs = [(i, w) for i, w in enumerate(ws)]

    def copy(pi, po, ps, x, y, c, d, i, w, t, landing):
        chip, dc = d >> 1, d & 1
        stage, flip, quarter, to = GATHER_COPIES[t]
        if landing:
            peer_chip = chip ^ PEER_FLIP[to]
            part = _region(po[i], w, peer_chip ^ flip, (1 - dc) if to == "s" else dc, quarter)
            src = part
        else:
            part = _region(po[i], w, chip ^ flip, dc, quarter)
            here = flip != 0 and (part_of is None or stage == 2)
            src = part if here else _region(pi[i], w, chip ^ flip, dc, quarter)
        target = {"x": (x ^ 1, y, c), "y": (x, y ^ 1, c), "s": (x, y, 1 - c)}[to]
        return pltpu.make_async_remote_copy(
            src_ref=src, dst_ref=part, send_sem=ps[0].at[i * n_t + t], recv_sem=ps[1].at[i * n_t + t],
            device_id=target, device_id_type=MESH)

    part_of = part

    def stage(landed, started, sent):
        def run(pi, po, ps):
            def on(x, y, c, d):
                for i, w in jobs:
                    for t in landed:
                        copy(pi, po, ps, x, y, c, d, i, w, t, True).wait_recv()
                for i, w in jobs:
                    for t in started:
                        copy(pi, po, ps, x, y, c, d, i, w, t, False).start()
                for i, w in jobs:
                    for t in sent:
                        copy(pi, po, ps, x, y, c, d, i, w, t, False).wait_send()
            _on_device(on)
        return run

    stages = [stage(*st) for st in GATHER_STAGES[part]]
    mid_at = tuple(pass_at) if part is None else tuple(pass_at)[:len(stages) - 2]
    return _Plan([partials[w] for w in ws], [jax.ShapeDtypeStruct(W_SHAPES[w][:2], BF16) for w in ws],
                 [pltpu.SemaphoreType.DMA((n_t * len(ws),)) for _ in range(2)], stages,
                 {i: i for i in range(len(ws))}, mid_at=mid_at)


def _grad_view(g, w):
    R, C, by_col = W_SHAPES[w]
    return g.reshape(1, 2, R // 2, C) if by_col else g.reshape(N_CHIPS, 2, R // N_CHIPS // 2, C)


def _start_wait_plan(ins, outs, n_copies, copies):
    def start(pi, po, ps):
        for cp in copies(pi, po, ps):
            cp.start()

    def finish(pi, po, ps):
        for cp in copies(pi, po, ps):
            cp.wait()

    return _Plan(ins, outs, [pltpu.SemaphoreType.DMA((n_copies,)), pltpu.SemaphoreType.DMA((n_copies,))], [start, finish])


def _sibling_exchange_plan(g4s):
    pieces = [(i, p) for i, g in enumerate(g4s) for p in range(g.shape[0])]

    def copies(pi, po, ps):
        x, y, c = _place()
        return [pltpu.make_async_remote_copy(
            src_ref=pi[i].at[p, 1 - c], dst_ref=po[i].at[p], send_sem=ps[0].at[n], recv_sem=ps[1].at[n],
            device_id=(x, y, 1 - c), device_id_type=MESH) for n, (i, p) in enumerate(pieces)]

    return _start_wait_plan(list(g4s), [jax.ShapeDtypeStruct((g.shape[0],) + g.shape[2:], F32) for g in g4s],
                            len(pieces), copies)


def _pair_sum(g4, other, c_arr, name):
    P, _, hr, C = g4.shape
    br = _ew_rows(hr, C, 2.5)

    def body(c_ref, g_ref, o_ref, p_ref):
        p_ref[...] = (g_ref[...] + o_ref[...]).astype(BF16)

    return _pcall(
        body, name=name,
        grid_spec=pltpu.PrefetchScalarGridSpec(
            num_scalar_prefetch=1, grid=(P, hr // br),
            in_specs=[pl.BlockSpec((None, None, br, C), lambda p, i, c_ref: (p, c_ref[0], i, 0)),
                      pl.BlockSpec((None, br, C), lambda p, i, c_ref: (p, i, 0))],
            out_specs=pl.BlockSpec((None, br, C), lambda p, i, c_ref: (p, i, 0))),
        out_shape=jax.ShapeDtypeStruct((P, hr, C), BF16),
        compiler_params=_params(("parallel", "parallel")),
    )(c_arr, g4, other)


def _pair_part(p_ref, w, chip):
    sr, sc = _shard_shape(w)
    return p_ref.at[0, :, pl.ds(chip * sc, sc)] if W_SHAPES[w][2] else p_ref.at[chip]


def _chip_exchange_plan(pairs, rels=(0, 1, 2), into=None):
    ws = sorted(pairs)
    n = len(ws)

    def stage(wait):
        def run(pi, po, ps):
            def on(x, y, c, d):
                for i, w in enumerate(ws):
                    for k, (rx, ry) in enumerate(RELATIONS):
                        if k not in rels:
                            continue
                        cp = pltpu.make_async_remote_copy(
                            src_ref=_pair_part(pi[i], w, (d >> 1) ^ (2 * rx + ry)), dst_ref=po[i].at[k],
                            send_sem=ps[0].at[i * 3 + k], recv_sem=ps[1].at[i * 3 + k],
                            device_id=(x ^ rx, y ^ ry, c), device_id_type=MESH)
                        if wait:
                            cp.wait()
                        else:
                            cp.start()
            _on_device(on)
        return run

    ins = [pairs[w] for w in ws] + ([into[w] for w in ws] if into else [])
    return _Plan(ins, [jax.ShapeDtypeStruct((3,) + _half_shape(w), BF16) for w in ws],
                 [pltpu.SemaphoreType.DMA((3 * n,)), pltpu.SemaphoreType.DMA((3 * n,))],
                 [stage(False), stage(True)], {n + i: i for i in range(n)} if into else None)


def _sum_slots(pair, slots, w, chip_arr, name):
    _, hr, C = slots.shape
    br = _ew_rows(hr, C, 3)
    own_map = (lambda i, chip: (0, i, chip[0])) if W_SHAPES[w][2] else (lambda i, chip: (chip[0], i, 0))

    def body(chip_ref, p_ref, s_ref, o_ref):
        acc = p_ref[...].astype(F32)
        for k in range(3):
            acc = acc + s_ref[k].astype(F32)
        o_ref[...] = acc

    return _pcall(
        body, name=name,
        grid_spec=pltpu.PrefetchScalarGridSpec(
            num_scalar_prefetch=1, grid=(hr // br,),
            in_specs=[pl.BlockSpec((None, br, C), own_map), pl.BlockSpec((3, br, C), lambda i, chip: (0, i, 0))],
            out_specs=pl.BlockSpec((br, C), lambda i, chip: (i, 0))),
        out_shape=jax.ShapeDtypeStruct((hr, C), F32), compiler_params=_params(("parallel",)),
    )(chip_arr, pair, slots)


def _sibling_share_plan(halves):
    def copies(pi, po, ps):
        x, y, c = _place()
        return [pltpu.make_async_remote_copy(
            src_ref=pi[i], dst_ref=po[i], send_sem=ps[0].at[i], recv_sem=ps[1].at[i],
            device_id=(x, y, 1 - c), device_id_type=MESH) for i in range(len(halves))]

    return _start_wait_plan(list(halves), [jax.ShapeDtypeStruct(h.shape, F32) for h in halves], len(halves), copies)


def _pad_lanes(v, width=D_MODEL):
    return jnp.pad(v, ((0, 0), (0, width - v.shape[1])))


def _pack_small(b_ada, norm1, norm2, lb, o_gain, q_gain, k_gain, sinks):
    rows = [b_ada.reshape(N_MOD, D_MODEL), norm1, norm2, jnp.concatenate([lb[0:1], o_gain], axis=1),
            _pad_lanes(jnp.concatenate([q_gain, k_gain, sinks], axis=1)), _pad_lanes(lb[1:2]),
            jnp.zeros((SMALL_ROWS - 11, D_MODEL), F32)]
    return jnp.concatenate(rows, axis=0)


def _unpack_small(p):
    return (p[0:6].reshape(1, N_MOD * D_MODEL), p[6:7], p[7:8],
            jnp.concatenate([p[8:9, 0:A_WIDTH], p[10:11, 0:A_WIDTH]], axis=0), p[8:9, A_WIDTH:],
            p[9:10, 0:64], p[9:10, 64:128], p[9:10, 128:144])


def kernel(x, c, w_ada, b_ada, norm1_gain, w_in, lb_logits, hgrn_o_gain, q_norm_gain, k_norm_gain, sinks, w_branch_a, w_branch_b, w_out, norm2_gain, w_mlp_in, w_mlp_out, loss_target, m_w_ada, m_b_ada, m_norm1_gain, m_w_in, m_lb_logits, m_hgrn_o_gain, m_q_norm_gain, m_k_norm_gain, m_sinks, m_w_branch_a, m_w_branch_b, m_w_out, m_norm2_gain, m_w_mlp_in, m_w_mlp_out, v_w_ada, v_b_ada, v_norm1_gain, v_w_in, v_lb_logits, v_hgrn_o_gain, v_q_norm_gain, v_k_norm_gain, v_sinks, v_w_branch_a, v_w_branch_b, v_w_out, v_norm2_gain, v_w_mlp_in, v_w_mlp_out):
    xi, yi, ci = _place()
    chip = 2 * xi + yi
    me = 4 * xi + 2 * yi + ci
    ada_cols = w_ada.shape[2]

    c_all = _allgather_small(jnp.broadcast_to(c, (8, D_MODEL)), "gather_c").reshape(N_DEV, 8, D_MODEL)[:, 0]
    b_cols = lax.dynamic_slice(b_ada, (0, chip * ada_cols), (1, ada_cols))
    mod_part = _ada_fwd(c_all, w_ada[0], b_cols)
    mod_all = _allgather_small(mod_part, "gather_mod").reshape(N_CHIPS, 2, N_DEV, ada_cols)[:, 0]
    mod_mine = lax.dynamic_index_in_dim(mod_all, me, axis=1, keepdims=False).reshape(N_MOD, D_MODEL)
    mod8 = jnp.concatenate([mod_mine, jnp.zeros((2, D_MODEL), F32)], axis=0)

    shards = (w_in[0], w_branch_a[0], w_branch_b[0], w_out[0], w_mlp_in[0], w_mlp_out[0])
    chip_arr = chip.astype(jnp.int32).reshape(1)
    c_arr = ci.astype(jnp.int32).reshape(1)

    grad_x, halves, theirs, st = _local_step(x[0], loss_target[0], mod8, norm1_gain, norm2_gain, lb_logits, hgrn_o_gain,
                                             q_norm_gain, k_norm_gain, sinks, shards, c_arr, chip_arr)
    loss = lax.psum(0.5 * jnp.sum(st["loss"][0]) / D_MODEL, ("x", "y", "c"))
    moments = ((m_w_in, v_w_in), (m_w_branch_a, v_w_branch_a), (m_w_branch_b, v_w_branch_b), (m_w_out, v_w_out),
               (m_w_mlp_in, v_w_mlp_in), (m_w_mlp_out, v_w_mlp_out))
    big = [_adamw_halves(shards[w], halves[w], theirs[w], moments[w][0][0], moments[w][1][0], c_arr, f"adamw{w}")
           for w in range(N_W)]

    swa = st["swa"]
    small = jnp.concatenate([
        st["n1"][1:2], st["n1"][0:1], st["n2"][3:4], st["n2"][1:2], st["n2"][0:1], st["loss"][1:2],
        st["n1"][2:3], st["n2"][2:3], jnp.concatenate([st["d_lb"][0:1], st["d_og"][0:1]], axis=1),
        _pad_lanes(jnp.concatenate([swa[0:1, 0:64], swa[1:2, 0:64], swa[2:3, 0:16]], axis=1)),
        jnp.zeros((SMALL_ROWS - 10, D_MODEL), F32)], axis=0)
    small_all = _allgather_small(small, "gather_small")
    g_small = _small_sum(small_all, lb_logits)
    small_w = (b_ada, norm1_gain, norm2_gain, lb_logits, hgrn_o_gain, q_norm_gain, k_norm_gain, sinks)
    small_m = (m_b_ada, m_norm1_gain, m_norm2_gain, m_lb_logits, m_hgrn_o_gain, m_q_norm_gain, m_k_norm_gain, m_sinks)
    small_v = (v_b_ada, v_norm1_gain, v_norm2_gain, v_lb_logits, v_hgrn_o_gain, v_q_norm_gain, v_k_norm_gain, v_sinks)
    sm = [_unpack_small(t) for t in
          (g_small,) + tuple(_adamw(_pack_small(*small_w), g_small, _pack_small(*small_m), _pack_small(*small_v),
                                    "adamw_small"))]
    g_b, g_n1, g_n2, g_lb, g_og, g_qg, g_kg, g_sk = ([t[i] for t in sm] for i in range(8))

    dmod_all = small_all.reshape(N_DEV, SMALL_ROWS, D_MODEL)[:, 0:N_MOD].reshape(N_DEV, N_MOD * D_MODEL)
    dmod_cols = lax.dynamic_slice(dmod_all, (0, chip * ada_cols), (N_DEV, ada_cols))
    ada = _ada_grad_adamw(c_all.T, dmod_cols, w_ada[0], m_w_ada[0], v_w_ada[0])

    def ordered(k):
        lead = lambda a: a[None]
        return (lead(ada[k]), g_b[k], g_n1[k], lead(big[0][k]), g_lb[k], g_og[k], g_qg[k], g_kg[k], g_sk[k],
                lead(big[1][k]), lead(big[2][k]), lead(big[3][k]), g_n2[k], lead(big[4][k]), lead(big[5][k]))

    return (loss, grad_x[None]) + ordered(0) + ordered(1) + ordered(2) + ordered(3)
```

```python
import jax
import jax.numpy as jnp
from jax import lax
from jax.experimental import pallas as pl
from jax.experimental.pallas import tpu as pltpu

F32 = jnp.float32
BF16 = jnp.bfloat16
HIGHEST = lax.Precision.HIGHEST
MESH = pl.DeviceIdType.MESH

D_MODEL = 2048
A_WIDTH = 1024
A_HEADS = 8
A_HEAD_DIM = 128
A_CHUNK = 64
B_WIDTH = 1024
B_HEAD_DIM = 64
B_GROUP = 4
B_KV_HEADS = 4
B_KV_WIDTH = 256
BLOCK = 128
MLP_HIDDEN = 8192
IN_WIDTH = 9728
N_MOD = 6
EPS = 1e-6
N_CHIPS = 4
N_DEV = 8

OFF_QA, OFF_FA, OFF_IA, OFF_GA = 0, 1024, 2048, 3072
OFF_QB, OFF_KB, OFF_VB = 4096, 5120, 5376
OFF_GATE_A, OFF_GATE_B = 5632, 7680

ADAM_LR = 0.001
ADAM_B1 = 0.9
ADAM_B2 = 0.999
ADAM_EPS = 1e-08
ADAM_WD = 0.01
ADAM_STEP = 10

VMEM_LIMIT_V7X = 48 * 1024 * 1024
NEG_BIG = -1e30


def _params(sem=None, vmem=VMEM_LIMIT_V7X):
    return pltpu.CompilerParams(dimension_semantics=sem, vmem_limit_bytes=vmem)


class _Plan:
    def __init__(self, ins, outs, sems, stages, aliases=None, mid_at=()):
        self.ins, self.outs, self.sems, self.stages, self.aliases = ins, outs, sems, stages, aliases or {}
        self.mid_at = tuple(mid_at)
        assert len(self.mid_at) == len(stages) - 2


def _join(a, b):
    assert len(a.stages) == 2 and len(b.stages) == 2
    ni, no, ns = len(a.ins), len(a.outs), len(a.sems)

    def stage(k):
        def run(pi, po, ps):
            a.stages[k](pi[:ni], po[:no], ps[:ns])
            b.stages[k](pi[ni:], po[no:], ps[ns:])
        return run

    aliases = dict(a.aliases)
    aliases.update({ni + i: no + o for i, o in b.aliases.items()})
    return _Plan(a.ins + b.ins, a.outs + b.outs, a.sems + b.sems, [stage(0), stage(1)], aliases)


def _pcall(body, plan=None, **kw):
    if plan is None:
        return pl.pallas_call(body, **kw)
    grid = kw["grid"]
    single = not isinstance(kw["out_specs"], (list, tuple))
    in_specs = list(kw["in_specs"])
    out_specs = [kw["out_specs"]] if single else list(kw["out_specs"])
    out_shape = [kw["out_shape"]] if single else list(kw["out_shape"])
    scratch = list(kw.get("scratch_shapes", ()))
    n_in, n_out, n_scr = len(in_specs), len(out_specs), len(scratch)
    n_pi, n_po = len(plan.ins), len(plan.outs)
    total = 1
    for g in grid:
        total *= g
    n_st = len(plan.stages)

    def wrapped(*refs):
        o0 = n_in + n_pi
        s0 = o0 + n_out + n_po
        pi, po, ps = refs[n_in:o0], refs[o0 + n_out:s0], refs[s0 + n_scr:]
        lin = 0
        for d, g in enumerate(grid):
            lin = lin * g + pl.program_id(d)
        for si, frac in enumerate((0.0,) + plan.mid_at):
            @pl.when(lin == int(frac * (total - 1)))
            def _(si=si):
                plan.stages[si](pi, po, ps)
        body(*refs[:n_in], *refs[o0:o0 + n_out], *refs[s0:s0 + n_scr])

        @pl.when(lin == total - 1)
        def _():
            plan.stages[-1](pi, po, ps)

    any_spec = pl.BlockSpec(memory_space=pl.ANY)
    call = pl.pallas_call(
        wrapped, name=kw["name"], grid=grid, in_specs=in_specs + [any_spec] * n_pi,
        out_specs=out_specs + [any_spec] * n_po, out_shape=out_shape + list(plan.outs),
        scratch_shapes=scratch + list(plan.sems),
        input_output_aliases={n_in + i: n_out + o for i, o in plan.aliases.items()},
        compiler_params=_params(("arbitrary",) * len(grid)))

    def run(*args):
        res = call(*args, *plan.ins)
        outs = list(res[:n_out])
        return (outs[0] if single else outs), list(res[n_out:])

    return run


def _run_plan(plan, name):
    return _pcall(lambda: None, plan=plan, name=name, grid=(1,), in_specs=[], out_specs=[], out_shape=[])()[1]


def _sig(x):
    return 1.0 / (1.0 + jnp.exp(-x))


def _nn(a, b):
    return lax.dot_general(a.astype(BF16), b.astype(BF16), (((1,), (0,)), ((), ())), preferred_element_type=F32)


def _nt(a, b):
    return lax.dot_general(a.astype(BF16), b.astype(BF16), (((1,), (1,)), ((), ())), preferred_element_type=F32)


def _tn(a, b):
    return lax.dot_general(a.astype(BF16), b.astype(BF16), (((0,), (0,)), ((), ())), preferred_element_type=F32)


def _mm(a, b, *, name, ta=False, tb=False, bm=1024, bn=1024, bk=2048, out_dtypes=(F32,), epi=None, extras=(),
        extra_cols=None, plan=None, a_blocks=None, row_extras=(), n_stats=0):
    if ta:
        K, M = a.shape
        bk = K
        if a_blocks is not None:
            M = a_blocks[0] * bm
    else:
        M, K = a.shape
    if tb:
        N, K2 = b.shape
    else:
        K2, N = b.shape
    bm, bn, bk = min(bm, M), min(bn, N), min(bk, K)
    assert K == K2 and M % bm == 0 and N % bn == 0 and K % bk == 0, (name, a.shape, b.shape)
    nk = K // bk
    a_col = a_blocks[1] if a_blocks is not None else (lambda i: i)
    a_spec = pl.BlockSpec((bk, bm), lambda i, j, k: (k, a_col(i))) if ta else pl.BlockSpec((bm, bk), lambda i, j, k: (i, k))
    b_spec = pl.BlockSpec((bn, bk), lambda i, j, k: (j, k)) if tb else pl.BlockSpec((bk, bn), lambda i, j, k: (k, j))
    t_spec = pl.BlockSpec((bm, bn), lambda i, j, k: (i, j))
    extra_cols = extra_cols or (0,) * len(extras)
    e_specs = [pl.BlockSpec((bm, bn), lambda i, j, k, off=off: (i, off + j)) for off in extra_cols]
    e_specs += [pl.BlockSpec((8, bn), lambda i, j, k: (0, j)) for _ in row_extras]
    dims = (((1,), (1 if tb else 0,)), ((), ()))
    n_e, n_o = len(extras) + len(row_extras), len(out_dtypes)
    stat_spec = pl.BlockSpec((8, bn), lambda i, j, k: (i, j))

    def body(*refs):
        a_ref, b_ref = refs[0], refs[1]
        e_refs = refs[2:2 + n_e]
        o_refs = refs[2 + n_e:2 + n_e + n_o]

        def finish(acc):
            outs = (acc,) if epi is None else epi(acc, *[e[...] for e in e_refs])
            for o_ref, o in zip(o_refs, outs):
                o_ref[...] = o.astype(o_ref.dtype)

        if ta:
            at_ref = refs[-1]

            @pl.when(pl.program_id(1) == 0)
            def _():
                at_ref[...] = a_ref[...].T

            lhs = at_ref[...]
        else:
            lhs = a_ref[...].astype(BF16)
        part = lax.dot_general(lhs, b_ref[...].astype(BF16), dims, preferred_element_type=F32)
        if nk == 1:
            finish(part)
        else:
            acc_ref = refs[-1]
            k = pl.program_id(2)

            @pl.when(k == 0)
            def _():
                acc_ref[...] = part

            @pl.when(k > 0)
            def _():
                acc_ref[...] += part

            @pl.when(k == nk - 1)
            def _():
                finish(acc_ref[...])

    if ta:
        assert a.dtype == BF16 and nk == 1
        scratch = [pltpu.VMEM((bm, bk), BF16)]
    else:
        scratch = [pltpu.VMEM((bm, bn), F32)] if nk > 1 else []
    out = _pcall(
        body, plan=plan, name=name, grid=(M // bm, N // bn, nk),
        in_specs=[a_spec, b_spec] + e_specs,
        out_specs=[t_spec] * (n_o - n_stats) + [stat_spec] * n_stats,
        out_shape=[jax.ShapeDtypeStruct((M, N), dt) for dt in out_dtypes[:n_o - n_stats]]
        + [jax.ShapeDtypeStruct((8 * (M // bm), N), F32)] * n_stats,
        scratch_shapes=scratch,
        compiler_params=_params(("parallel", "arbitrary", "arbitrary")),
    )(a, b, *extras, *row_extras)
    if plan is not None:
        return (out[0][0] if n_o == 1 else out[0]), out[1]
    return out[0] if n_o == 1 else out


def _ada_fwd(c_all, w_ada, b_cols):
    n = w_ada.shape[1]
    bn = 512

    def body(c_ref, w_ref, b_ref, o_ref):
        cv = c_ref[...]
        sc = cv * _sig(cv)
        o_ref[...] = jnp.dot(sc, w_ref[...], precision=HIGHEST, preferred_element_type=F32) + b_ref[...]

    return _pcall(
        body, name="ada_fwd", grid=(n // bn,),
        in_specs=[pl.BlockSpec((N_DEV, D_MODEL), lambda j: (0, 0)), pl.BlockSpec((D_MODEL, bn), lambda j: (0, j)),
                  pl.BlockSpec((1, bn), lambda j: (0, j))],
        out_specs=pl.BlockSpec((N_DEV, bn), lambda j: (0, j)),
        out_shape=jax.ShapeDtypeStruct((N_DEV, n), F32),
        compiler_params=_params(("parallel",)),
    )(c_all, w_ada, b_cols)


ROWS_EW = 256


def _rms_fwd_math(x, gain, scale, shift):
    rstd = lax.rsqrt(jnp.mean(x * x, axis=-1, keepdims=True) + EPS)
    xhat = x * rstd
    n = xhat * gain
    return n * (1.0 + scale) + shift, xhat, n, rstd


def _rms_bwd_math(dh, xhat, n, rstd, gain, scale):
    dn = dh * (1.0 + scale)
    dxhat = dn * gain
    dx = rstd * (dxhat - xhat * jnp.mean(dxhat * xhat, axis=-1, keepdims=True))
    d_scale = jnp.sum(dh * n, axis=0, keepdims=True)
    d_shift = jnp.sum(dh, axis=0, keepdims=True)
    d_gain = jnp.sum(dn * xhat, axis=0, keepdims=True)
    return dx, d_scale, d_shift, d_gain


def _row_spec(w=D_MODEL, br=ROWS_EW):
    return pl.BlockSpec((br, w), lambda i: (i, 0))


def _vec_spec(r=8, w=D_MODEL):
    return pl.BlockSpec((r, w), lambda i: (0, 0))


def _norm1_fwd(x, gain, mod8, plan=None):
    T = x.shape[0]

    def body(x_ref, g_ref, m_ref, h_ref):
        h, _, _, _ = _rms_fwd_math(x_ref[...], g_ref[...], m_ref[1:2, :], m_ref[0:1, :])
        h_ref[...] = h.astype(BF16)

    return _pcall(
        body, plan=plan, name="norm1_fwd", grid=(T // ROWS_EW,),
        in_specs=[_row_spec(), _vec_spec(1), _vec_spec()],
        out_specs=_row_spec(), out_shape=jax.ShapeDtypeStruct((T, D_MODEL), BF16),
        compiler_params=_params(("parallel",)),
    )(x, gain, mod8)


def _res_norm2_fwd(x, mo, gain, mod8):
    T = x.shape[0]
    br = ROWS_EW

    def body(x_ref, mo_ref, g_ref, m_ref, x1_ref, h_ref):
        x1 = x_ref[...] + m_ref[2:3, :] * mo_ref[...]
        x1_ref[...] = x1
        h, _, _, _ = _rms_fwd_math(x1, g_ref[...], m_ref[4:5, :], m_ref[3:4, :])
        h_ref[...] = h.astype(BF16)

    return _pcall(
        body, name="res_norm2_fwd", grid=(T // br,),
        in_specs=[_row_spec(br=br), _row_spec(br=br), _vec_spec(1), _vec_spec()],
        out_specs=[_row_spec(br=br), _row_spec(br=br)],
        out_shape=[jax.ShapeDtypeStruct((T, D_MODEL), F32), jax.ShapeDtypeStruct((T, D_MODEL), BF16)],
        compiler_params=_params(("parallel",)),
    )(x, mo, gain, mod8)


def _loss_head(mlp, x1, target, mod):
    gate = mod[5:6, :]
    err = x1 + gate * mlp - target
    dy = err * (1.0 / D_MODEL)
    row = lax.broadcasted_iota(jnp.int32, (8, mlp.shape[1]), 0)
    stats = jnp.where(row == 0, jnp.sum(err * err, axis=0, keepdims=True),
                      jnp.where(row == 1, jnp.sum(dy * mlp, axis=0, keepdims=True), 0.0))
    return dy, dy * gate, stats


def _norm2_bwd(dh2, x1, dy, mo, gain, mod8):
    T = x1.shape[0]

    def body(dh_ref, x1_ref, dy_ref, mo_ref, g_ref, m_ref, dx1_ref, dmo_ref, st_ref):
        i = pl.program_id(0)
        gain_v, scale = g_ref[...], m_ref[4:5, :]
        _, xhat, n, rstd = _rms_fwd_math(x1_ref[...], gain_v, scale, m_ref[3:4, :])
        dx, d_scale, d_shift, d_gain = _rms_bwd_math(dh_ref[...], xhat, n, rstd, gain_v, scale)
        dx1 = dy_ref[...] + dx
        dx1_ref[...] = dx1
        dmo_ref[...] = (dx1 * m_ref[2:3, :]).astype(BF16)

        @pl.when(i == 0)
        def _():
            st_ref[...] = jnp.zeros_like(st_ref)

        st_ref[0:1, :] += d_scale
        st_ref[1:2, :] += d_shift
        st_ref[2:3, :] += d_gain
        st_ref[3:4, :] += jnp.sum(dx1 * mo_ref[...], axis=0, keepdims=True)

    return _pcall(
        body, name="norm2_bwd", grid=(T // ROWS_EW,),
        in_specs=[_row_spec(), _row_spec(), _row_spec(), _row_spec(), _vec_spec(1), _vec_spec()],
        out_specs=[_row_spec(), _row_spec(), _vec_spec()],
        out_shape=[jax.ShapeDtypeStruct((T, D_MODEL), F32), jax.ShapeDtypeStruct((T, D_MODEL), BF16),
                   jax.ShapeDtypeStruct((8, D_MODEL), F32)],
        compiler_params=_params(("arbitrary",)),
    )(dh2, x1, dy, mo, gain, mod8)


def _norm1_bwd(dh, x, dx1, gain, mod8):
    T = x.shape[0]
    br = ROWS_EW

    def body(dh_ref, x_ref, dx1_ref, g_ref, m_ref, dx_ref, st_ref):
        i = pl.program_id(0)
        gain_v, scale = g_ref[...], m_ref[1:2, :]
        _, xhat, n, rstd = _rms_fwd_math(x_ref[...], gain_v, scale, m_ref[0:1, :])
        dx, d_scale, d_shift, d_gain = _rms_bwd_math(dh_ref[...], xhat, n, rstd, gain_v, scale)
        dx_ref[...] = dx1_ref[...] + dx

        @pl.when(i == 0)
        def _():
            st_ref[...] = jnp.zeros_like(st_ref)

        st_ref[0:1, :] += d_scale
        st_ref[1:2, :] += d_shift
        st_ref[2:3, :] += d_gain

    return _pcall(
        body, name="norm1_bwd", grid=(T // br,),
        in_specs=[_row_spec(br=br), _row_spec(br=br), _row_spec(br=br), _vec_spec(1), _vec_spec()],
        out_specs=[_row_spec(br=br), _vec_spec()],
        out_shape=[jax.ShapeDtypeStruct((T, D_MODEL), F32), jax.ShapeDtypeStruct((8, D_MODEL), F32)],
        compiler_params=_params(("arbitrary",)),
    )(dh, x, dx1, gain, mod8)


MERGE_BC = 512


def _hgrn_rows(T):
    return 512 if T >= 1024 else 128


def _lower_bound(lbl):
    e = jnp.exp(lbl - jnp.max(lbl, axis=0, keepdims=True))
    return e[0:1, :] / (e[0:1, :] + e[1:2, :])


def _chunk_sum_matrix(rows, backward):
    shift = A_CHUNK.bit_length() - 1
    r = lax.broadcasted_iota(jnp.int32, (rows, rows), 0)
    c = lax.broadcasted_iota(jnp.int32, (rows, rows), 1)
    same = jnp.right_shift(r, shift) == jnp.right_shift(c, shift)
    return (same & ((r <= c) if backward else (r >= c))).astype(BF16)


def _chunk_sums(m, x):
    n = x.shape[1]
    hi = x.astype(BF16)
    rest = x - hi.astype(F32)
    mid = rest.astype(BF16)
    lo = (rest - mid.astype(F32)).astype(BF16)
    y = jnp.dot(m, jnp.concatenate([hi, mid, lo], axis=1), preferred_element_type=F32)
    return y[:, 0:n] + y[:, n:2 * n] + y[:, 2 * n:3 * n]


def _hgrn_block_pre(q, fl, lb, m_fwd):
    sg = _sig(fl)
    f = lb + (1.0 - lb) * sg
    sq = _sig(q)
    return dict(sg=sg, f=f, k=1.0 - f, sq=sq, qf=q * sq, b=_chunk_sums(m_fwd, jnp.log(f)))


def _hgrn_chunk_local(pre, r):
    C = A_CHUNK
    qf, k, b = pre["qf"][r], pre["k"][r], pre["b"][r]
    causal = lax.broadcasted_iota(jnp.int32, (C, C), 0) >= lax.broadcasted_iota(jnp.int32, (C, C), 1)
    bm = b[C // 2 - 1:C // 2, :]
    bl = b[C - 1:C, :]
    e_q, e_k = jnp.exp(b - bm), jnp.exp(bm - b)
    e_b, e_l = jnp.exp(b), jnp.exp(bl - b)
    qd, kd = qf * e_q, k * e_k
    qe, ke = qf * e_b, k * e_l
    att = jnp.where(causal, _nt(qd, kd), 0.0)
    return dict(causal=causal, e_q=e_q, e_k=e_k, e_b=e_b, e_l=e_l, qd=qd, kd=kd, qe=qe, ke=ke, att=att, dec=jnp.exp(bl))


def _hgrn_chunk_fwd(pre, r, v, st):
    c = _hgrn_chunk_local(pre, r)
    c["o"] = _nn(c["att"], v) + _nt(c["qe"], st)
    return c


def _lockstep(gens):
    out = [None] * len(gens)
    live = list(enumerate(gens))
    while live:
        still = []
        for i, g in live:
            try:
                next(g)
                still.append((i, g))
            except StopIteration as done:
                out[i] = done.value
        live = still
    return out


HGRN_HEADS_PER_STEP = 4


def _hgrn_fwd(proj, lb_logits, o_gain, plan=None):
    T = proj.shape[0]
    BR = _hgrn_rows(T)
    cps = BR // A_CHUNK
    K, NH = A_HEAD_DIM, HGRN_HEADS_PER_STEP
    W = NH * K

    def col(off):
        return pl.BlockSpec((BR, W), lambda h, cb: (cb, off // W + h))

    def body(q_ref, f_ref, i_ref, g_ref, lbl_ref, og_ref, o_ref, s_ref, st):
        @pl.when(pl.program_id(1) == 0)
        def _():
            st[...] = jnp.zeros_like(st)

        lb_all = _lower_bound(lbl_ref[...])
        m_fwd = _chunk_sum_matrix(BR, False)
        pre = [_hgrn_block_pre(q_ref[:, n * K:(n + 1) * K], f_ref[:, n * K:(n + 1) * K], lb_all[:, n * K:(n + 1) * K], m_fwd)
               for n in range(NH)]
        def local(n, ci):
            r, hs = slice(ci * A_CHUNK, (ci + 1) * A_CHUNK), slice(n * K, (n + 1) * K)
            v = i_ref[r, hs]
            c = _hgrn_chunk_local(pre[n], r)
            yield
            return dict(o=_nn(c["att"], v), ds=_tn(v, c["ke"]), qe=c["qe"], dec=c["dec"])

        def chain(n, loc):
            hs = slice(n * K, (n + 1) * K)
            state = st[n]
            for ci, p in enumerate(loc):
                r = slice(ci * A_CHUNK, (ci + 1) * A_CHUNK)
                s_ref[n, ci] = state
                o = p["o"] + _nt(p["qe"], state)
                state = state * p["dec"] + p["ds"]
                yield
                on = o * lax.rsqrt(jnp.mean(o * o, axis=-1, keepdims=True) + EPS)
                g = g_ref[r, hs]
                o_ref[r, hs] = (on * og_ref[:, hs] * (g * _sig(g))).astype(BF16)
            st[n] = state

        loc = _lockstep([local(n, ci) for n in range(NH) for ci in range(cps)])
        _lockstep([chain(n, loc[n * cps:(n + 1) * cps]) for n in range(NH)])

    return _pcall(
        body, plan=plan, name="hgrn_fwd", grid=(A_HEADS // NH, T // BR),
        in_specs=[col(OFF_QA), col(OFF_FA), col(OFF_IA), col(OFF_GA),
                  pl.BlockSpec((2, W), lambda h, cb: (0, h)), pl.BlockSpec((1, W), lambda h, cb: (0, h))],
        out_specs=[pl.BlockSpec((BR, W), lambda h, cb: (cb, h)),
                   pl.BlockSpec((NH, cps, K, K), lambda h, cb: (h, cb, 0, 0))],
        out_shape=[jax.ShapeDtypeStruct((T, A_WIDTH), BF16),
                   jax.ShapeDtypeStruct((A_HEADS, T // A_CHUNK, K, K), F32)],
        scratch_shapes=[pltpu.VMEM((NH, K, K), F32)],
        compiler_params=_params(("parallel", "arbitrary")),
    )(proj, proj, proj, proj, lb_logits, o_gain)


def _hgrn_bwd(proj, lb_logits, o_gain, states, do, plan=None):
    T = proj.shape[0]
    BR = _hgrn_rows(T)
    cps = BR // A_CHUNK
    ncb = T // BR
    K, C, NH = A_HEAD_DIM, A_CHUNK, HGRN_HEADS_PER_STEP
    W = NH * K

    def col(off):
        return pl.BlockSpec((BR, W), lambda h, cb: (ncb - 1 - cb, off // W + h))

    def body(q_ref, f_ref, i_ref, g_ref, lbl_ref, og_ref, s_ref, do_ref,
             dq_ref, df_ref, di_ref, dg_ref, dlb_ref, dog_ref, dst):
        @pl.when(pl.program_id(1) == 0)
        def _():
            dst[...] = jnp.zeros_like(dst)
            dlb_ref[...] = jnp.zeros_like(dlb_ref)
            dog_ref[...] = jnp.zeros_like(dog_ref)

        lb_all = _lower_bound(lbl_ref[...])
        row = lax.broadcasted_iota(jnp.int32, (C, K), 0)
        m_fwd, m_bwd = _chunk_sum_matrix(BR, False), _chunk_sum_matrix(BR, True)
        pre = [_hgrn_block_pre(q_ref[:, n * K:(n + 1) * K], f_ref[:, n * K:(n + 1) * K], lb_all[:, n * K:(n + 1) * K], m_fwd)
               for n in range(NH)]
        def local(n, ci):
            r, hs = slice(ci * C, (ci + 1) * C), slice(n * K, (n + 1) * K)
            gain = og_ref[:, hs]
            st = s_ref[n, ci]
            v = i_ref[r, hs]
            q = q_ref[r, hs]
            c = _hgrn_chunk_fwd(pre[n], r, v, st)
            yield
            o = c["o"]
            rn = lax.rsqrt(jnp.mean(o * o, axis=-1, keepdims=True) + EPS)
            on = o * rn
            g = g_ref[r, hs]
            sgg = _sig(g)
            dy = do_ref[r, hs]
            d_ong = dy * (g * sgg)
            dg_ref[r, hs] = (dy * (on * gain) * (sgg * (1.0 + g * (1.0 - sgg)))).astype(BF16)
            d_on = d_ong * gain
            d_o = rn * (d_on - on * jnp.mean(d_on * on, axis=-1, keepdims=True))
            datt = jnp.where(c["causal"], _nt(d_o, v), 0.0)
            dqe = _nn(d_o, st)
            yield
            dqd = _nn(datt, c["kd"])
            dkd = _tn(datt, c["qd"])
            dv = _tn(c["att"], d_o)
            ds = _tn(d_o, c["qe"])
            yield
            t_q, t_k = dqd * c["qd"], dkd * c["kd"]
            sq = pre[n]["sq"][r]
            dq_ref[r, hs] = ((dqd * c["e_q"] + dqe * c["e_b"]) * (sq * (1.0 + q * (1.0 - sq)))).astype(BF16)
            return dict(v=v, st=st, ke=c["ke"], e_l=c["e_l"], dec=c["dec"], dv=dv, ds=ds, dk=dkd * c["e_k"],
                        db=t_q - t_k + dqe * c["qe"], dbm=jnp.sum(t_k - t_q, axis=0, keepdims=True),
                        d_og=jnp.sum(d_ong * on, axis=0, keepdims=True))

        def chain(n, loc):
            hs = slice(n * K, (n + 1) * K)
            dst_next = dst[n]
            db_of, dk_of = [None] * cps, [None] * cps
            for ci in reversed(range(cps)):
                p = loc[ci]
                di_ref[ci * C:(ci + 1) * C, hs] = (p["dv"] + _nt(p["ke"], dst_next)).astype(BF16)
                dke = _nn(p["v"], dst_next)
                yield
                t_l = dke * p["ke"]
                dbl = jnp.sum(t_l, axis=0, keepdims=True) + jnp.sum(dst_next * p["st"], axis=0, keepdims=True) * p["dec"]
                db_of[ci] = p["db"] - t_l + jnp.where(row == C // 2 - 1, p["dbm"], 0.0) + jnp.where(row == C - 1, dbl, 0.0)
                dk_of[ci] = p["dk"] + dke * p["e_l"]
                dst_next = dst_next * p["dec"] + p["ds"]
            dst[n] = dst_next
            return db_of, dk_of

        loc = _lockstep([local(n, ci) for n in range(NH) for ci in range(cps)])
        loc = [loc[n * cps:(n + 1) * cps] for n in range(NH)]
        chains = _lockstep([chain(n, loc[n]) for n in range(NH)])
        for n in range(NH):
            hs = slice(n * K, (n + 1) * K)
            db_of, dk_of = chains[n]
            d_og = loc[n][0]["d_og"]
            for p in loc[n][1:]:
                d_og = d_og + p["d_og"]
            dog_ref[0:1, hs] += d_og
            lb, sg = lb_all[:, hs], pre[n]["sg"]
            dlf = _chunk_sums(m_bwd, jnp.concatenate(db_of, axis=0))
            df = dlf / pre[n]["f"] - jnp.concatenate(dk_of, axis=0)
            df_ref[:, hs] = (df * (1.0 - lb) * sg * (1.0 - sg)).astype(BF16)
            dlb_ref[0:1, hs] += jnp.sum(df * (1.0 - sg), axis=0, keepdims=True)

    ocol = pl.BlockSpec((BR, W), lambda h, cb: (ncb - 1 - cb, h))
    vec = pl.BlockSpec((8, W), lambda h, cb: (0, h))
    return _pcall(
        body, plan=plan, name="hgrn_bwd", grid=(A_HEADS // NH, ncb),
        in_specs=[col(OFF_QA), col(OFF_FA), col(OFF_IA), col(OFF_GA),
                  pl.BlockSpec((2, W), lambda h, cb: (0, h)), pl.BlockSpec((1, W), lambda h, cb: (0, h)),
                  pl.BlockSpec((NH, cps, K, K), lambda h, cb: (h, ncb - 1 - cb, 0, 0)),
                  pl.BlockSpec((BR, W), lambda h, cb: (ncb - 1 - cb, h))],
        out_specs=[ocol, ocol, ocol, ocol, vec, vec],
        out_shape=[jax.ShapeDtypeStruct((T, A_WIDTH), BF16)] * 4 + [jax.ShapeDtypeStruct((8, A_WIDTH), F32)] * 2,
        scratch_shapes=[pltpu.VMEM((NH, K, K), F32)],
        compiler_params=_params(("parallel", "arbitrary")),
    )(proj, proj, proj, proj, lb_logits, o_gain, states, do)


def _head_norm(x):
    r = lax.rsqrt(jnp.mean(x * x, axis=-1, keepdims=True) + EPS)
    return x * r, r


def _head_norm_bwd(dy, xn, r, gain):
    dxn = dy * gain
    return r * (dxn - xn * jnp.mean(dxn * xn, axis=-1, keepdims=True)), jnp.sum(dy * xn, axis=0, keepdims=True)


def _swa_mask(has_prev):
    rows = B_GROUP * BLOCK
    r = lax.broadcasted_iota(jnp.int32, (rows, 2 * BLOCK), 0) % BLOCK
    c = lax.broadcasted_iota(jnp.int32, (rows, 2 * BLOCK), 1)
    rel = r + BLOCK - c
    return (rel >= 0) & (rel < BLOCK) & ((c >= BLOCK) | has_prev)


def _swa_head_fwd(j, q_ref, kp_ref, kc_ref, vp_ref, vc_ref, qg, kg, sk_ref, mask):
    hs = slice(j * B_HEAD_DIM, (j + 1) * B_HEAD_DIM)
    kcat = jnp.concatenate([kp_ref[:, hs], kc_ref[:, hs]], axis=0)
    vcat = jnp.concatenate([vp_ref[:, hs], vc_ref[:, hs]], axis=0)
    qs = jnp.concatenate([q_ref[:, pl.ds((j * B_GROUP + g) * B_HEAD_DIM, B_HEAD_DIM)] for g in range(B_GROUP)], axis=0)
    kn, kr = _head_norm(kcat)
    qn, qr = _head_norm(qs)
    kh, qh = kn * kg, qn * qg
    yield
    s = jnp.where(mask, _nt(qh, kh) * (B_HEAD_DIM ** -0.5), NEG_BIG)
    yield
    sink = jnp.concatenate(
        [jnp.broadcast_to(sk_ref[0:1, pl.ds(j * B_GROUP + g, 1)], (BLOCK, 1)) for g in range(B_GROUP)], axis=0)
    m = jnp.maximum(jnp.max(s, axis=-1, keepdims=True), sink)
    p = jnp.exp(s - m)
    e_sink = jnp.exp(sink - m)
    inv = 1.0 / (jnp.sum(p, axis=-1, keepdims=True) + e_sink)
    prob = p * inv
    return dict(vcat=vcat, kn=kn, kr=kr, qn=qn, qr=qr, kh=kh, qh=qh, prob=prob, p_sink=e_sink * inv)


def _swa_in_specs(nb, last):
    def qi(n):
        return jnp.minimum(n, last)

    q = pl.BlockSpec((BLOCK, B_WIDTH), lambda n: (qi(n), OFF_QB // B_WIDTH))
    kc = pl.BlockSpec((BLOCK, B_KV_WIDTH), lambda n: (qi(n), OFF_KB // B_KV_WIDTH))
    kp = pl.BlockSpec((BLOCK, B_KV_WIDTH), lambda n: (jnp.maximum(qi(n) - 1, 0), OFF_KB // B_KV_WIDTH))
    vc = pl.BlockSpec((BLOCK, B_KV_WIDTH), lambda n: (qi(n), OFF_VB // B_KV_WIDTH))
    vp = pl.BlockSpec((BLOCK, B_KV_WIDTH), lambda n: (jnp.maximum(qi(n) - 1, 0), OFF_VB // B_KV_WIDTH))
    small = [pl.BlockSpec((1, B_HEAD_DIM), lambda n: (0, 0)), pl.BlockSpec((1, B_HEAD_DIM), lambda n: (0, 0)),
             pl.BlockSpec((1, B_GROUP * B_KV_HEADS), lambda n: (0, 0))]
    return [q, kp, kc, vp, vc] + small


def _swa_fwd(proj, q_gain, k_gain, sinks, plan=None):
    T = proj.shape[0]
    nb = T // BLOCK

    def body(q_ref, kp_ref, kc_ref, vp_ref, vc_ref, qg_ref, kg_ref, sk_ref, o_ref):
        mask = _swa_mask(pl.program_id(0) > 0)

        def head(j):
            c = yield from _swa_head_fwd(j, q_ref, kp_ref, kc_ref, vp_ref, vc_ref, qg_ref[...], kg_ref[...], sk_ref, mask)
            yield
            o = _nn(c["prob"], c["vcat"])
            yield
            for g in range(B_GROUP):
                o_ref[:, pl.ds((j * B_GROUP + g) * B_HEAD_DIM, B_HEAD_DIM)] = o[g * BLOCK:(g + 1) * BLOCK].astype(BF16)

        _lockstep([head(j) for j in range(B_KV_HEADS)])

    return _pcall(
        body, plan=plan, name="swa_fwd", grid=(nb,),
        in_specs=_swa_in_specs(nb, nb - 1),
        out_specs=pl.BlockSpec((BLOCK, B_WIDTH), lambda n: (n, 0)),
        out_shape=jax.ShapeDtypeStruct((T, B_WIDTH), BF16),
        compiler_params=_params(("parallel",)),
    )(proj, proj, proj, proj, proj, q_gain, k_gain, sinks)


def _swa_bwd(proj, q_gain, k_gain, sinks, do, plan=None):
    T = proj.shape[0]
    nb = T // BLOCK
    scale = B_HEAD_DIM ** -0.5

    def body(q_ref, kp_ref, kc_ref, vp_ref, vc_ref, qg_ref, kg_ref, sk_ref, do_ref,
             dq_ref, dkv_ref, sm_ref, ck, cv):
        n = pl.program_id(0)

        @pl.when(n == 0)
        def _():
            ck[...] = jnp.zeros_like(ck)
            cv[...] = jnp.zeros_like(cv)
            sm_ref[...] = jnp.zeros_like(sm_ref)

        @pl.when(n < nb)
        def _():
            mask = _swa_mask(n > 0)
            qg, kg = qg_ref[...], kg_ref[...]
            lane = lax.broadcasted_iota(jnp.int32, (1, BLOCK), 1)
            def head(j):
                hs = slice(j * B_HEAD_DIM, (j + 1) * B_HEAD_DIM)
                vs = slice(B_KV_WIDTH + j * B_HEAD_DIM, B_KV_WIDTH + (j + 1) * B_HEAD_DIM)
                c = yield from _swa_head_fwd(j, q_ref, kp_ref, kc_ref, vp_ref, vc_ref, qg, kg, sk_ref, mask)
                d_out = jnp.concatenate(
                    [do_ref[:, pl.ds((j * B_GROUP + g) * B_HEAD_DIM, B_HEAD_DIM)] for g in range(B_GROUP)], axis=0)
                prob = c["prob"]
                yield
                out = _nn(prob, c["vcat"])
                d_prob = _nt(d_out, c["vcat"])
                dv = _tn(prob, d_out)
                yield
                delta = jnp.sum(d_out * out, axis=-1, keepdims=True)
                ds = prob * (d_prob - delta)
                d_sink = -c["p_sink"] * delta
                yield
                dqh = _nn(ds, c["kh"]) * scale
                dkh = _tn(ds, c["qh"]) * scale
                yield
                dq, dqg = _head_norm_bwd(dqh, c["qn"], c["qr"], qg)
                dk, dkg = _head_norm_bwd(dkh, c["kn"], c["kr"], kg)
                d_sinks = jnp.zeros((1, BLOCK), F32)
                for g in range(B_GROUP):
                    dq_ref[:, pl.ds((j * B_GROUP + g) * B_HEAD_DIM, B_HEAD_DIM)] = dq[g * BLOCK:(g + 1) * BLOCK].astype(BF16)
                    tot = jnp.sum(d_sink[g * BLOCK:(g + 1) * BLOCK], axis=0, keepdims=True)
                    d_sinks = d_sinks + jnp.where(lane == j * B_GROUP + g, tot, 0.0)
                dkv_ref[:, hs] = (ck[:, hs] + dk[0:BLOCK]).astype(BF16)
                dkv_ref[:, vs] = (cv[:, hs] + dv[0:BLOCK]).astype(BF16)
                ck[:, hs] = dk[BLOCK:2 * BLOCK]
                cv[:, hs] = dv[BLOCK:2 * BLOCK]
                return dqg, dkg, d_sinks

            small = _lockstep([head(j) for j in range(B_KV_HEADS)])
            sm_ref[0:1, 0:B_HEAD_DIM] += small[0][0] + small[1][0] + small[2][0] + small[3][0]
            sm_ref[1:2, 0:B_HEAD_DIM] += small[0][1] + small[1][1] + small[2][1] + small[3][1]
            sm_ref[2:3, :] += small[0][2] + small[1][2] + small[2][2] + small[3][2]

        @pl.when(n == nb)
        def _():
            dkv_ref[:, 0:B_KV_WIDTH] = ck[...].astype(BF16)
            dkv_ref[:, B_KV_WIDTH:2 * B_KV_WIDTH] = cv[...].astype(BF16)

    return _pcall(
        body, plan=plan, name="swa_bwd", grid=(nb + 1,),
        in_specs=_swa_in_specs(nb, nb - 1) + [pl.BlockSpec((BLOCK, B_WIDTH), lambda n: (jnp.minimum(n, nb - 1), 0))],
        out_specs=[pl.BlockSpec((BLOCK, B_WIDTH), lambda n: (jnp.minimum(n, nb - 1), 0)),
                   pl.BlockSpec((BLOCK, 2 * B_KV_WIDTH), lambda n: (jnp.maximum(n - 1, 0), 0)),
                   pl.BlockSpec((8, BLOCK), lambda n: (0, 0))],
        out_shape=[jax.ShapeDtypeStruct((T, B_WIDTH), BF16), jax.ShapeDtypeStruct((T, 2 * B_KV_WIDTH), BF16),
                   jax.ShapeDtypeStruct((8, BLOCK), F32)],
        scratch_shapes=[pltpu.VMEM((BLOCK, B_KV_WIDTH), F32), pltpu.VMEM((BLOCK, B_KV_WIDTH), F32)],
        compiler_params=_params(("arbitrary",)),
    )(proj, proj, proj, proj, proj, q_gain, k_gain, sinks, do)


W_IN, W_A, W_B, W_OUT, W_MI, W_MO = range(6)


def _local_step(x, target, mod8, norm1_gain, norm2_gain, lb_logits, o_gain, q_gain, k_gain, sinks, shards, c_arr, chip_arr):
    relu2 = lambda u: (u, jnp.square(jnp.maximum(u, 0.0)))
    pair, half = {}, {}

    def exchange(ws, grads):
        return _sibling_exchange_plan([_grad_view(g, w) for w, g in zip(ws, grads)])

    def pair_sums(ws, grads, others):
        for w, g, o in zip(ws, grads, others):
            pair[w] = _pair_sum(_grad_view(g, w), o, c_arr, f"pair_sum{w}")

    def sum_slots(ws, slots):
        for w, s in zip(ws, slots):
            half[w] = _sum_slots(pair[w], s, w, chip_arr, f"sum_slots{w}")

    part_in = _cast_into_full({W_IN: shards[W_IN]}, "cast_w_in")[W_IN]
    h, (part_in,) = _norm1_fwd(x, norm1_gain, mod8, plan=_gather_plan({W_IN: part_in}, part="near"))
    parts, (w_in,) = _cast_into_full({w: shards[w] for w in range(1, N_W)}, "cast_rest",
                                     plan=_gather_plan({W_IN: part_in}, pass_at=(0.97,), part="far"))
    proj, (w_mi,) = _mm(h, w_in, name="mm_proj", bn=512, plan=_gather_plan({W_MI: parts[W_MI]}, pass_at=(0.47, 0.72)))
    (o_a, states), (w_a, w_b) = _hgrn_fwd(
        proj, lb_logits, o_gain, plan=_gather_plan({w: parts[w] for w in (W_A, W_B)}, pass_at=(0.4, 0.65)))
    o_b, (w_out,) = _swa_fwd(proj, q_gain, k_gain, sinks, plan=_gather_plan({W_OUT: parts[W_OUT]}, pass_at=(0.3, 0.5)))
    ya = _mm(o_a, w_a, name="mm_branch_a")
    gate_cols = (OFF_GATE_A // MERGE_BC, OFF_GATE_B // MERGE_BC)
    yb, merged = _mm(o_b, w_b, name="mm_branch_b", bn=MERGE_BC, out_dtypes=(F32, BF16),
                     extras=(proj, proj, ya), extra_cols=gate_cols + (0,),
                     epi=lambda acc, ga, gb, ya_: (acc, _sig(ga) * ya_ + _sig(gb) * acc))
    mo = _mm(merged, w_out, name="mm_out")
    x1, h2 = _res_norm2_fwd(x, mo, norm2_gain, mod8)
    (u, act), (w_mo,) = _mm(h2, w_mi, name="mm_mlp_in", out_dtypes=(F32, BF16), epi=relu2,
                            plan=_gather_plan({W_MO: parts[W_MO]}, pass_at=(0.6, 0.9)))
    dy, dmlp, st_loss = _mm(act, w_mo, name="mm_mlp_out", bm=512, out_dtypes=(F32, BF16, F32), n_stats=1,
                            extras=(x1, target), row_extras=(mod8,), epi=_loss_head)
    st_loss = st_loss.reshape(-1, 8, D_MODEL).sum(axis=0)
    def half_blocks(w, own):
        def block(i):
            return 2 * i + (lax.axis_index("c") if own else 1 - lax.axis_index("c"))
        return (1 if W_SHAPES[w][2] else N_CHIPS), block

    def pair_of(w, lhs, rhs, other, name):
        hr, cols = _half_shape(w)
        p = _mm(lhs, rhs, name=name, ta=True, bn=512, a_blocks=half_blocks(w, True), out_dtypes=(BF16,),
                extras=(other,), epi=lambda acc, o: (acc + o,))
        return p.reshape(-1, hr, W_SHAPES[w][1])

    near, far = (0, 1), (2,)
    g_send = _mm(act, dmlp, name="mm_g_mlp_out_send", ta=True, bn=512, a_blocks=half_blocks(W_MO, False))
    du, (g_other,) = _mm(dmlp, w_mo, name="mm_d_act", tb=True, out_dtypes=(BF16,), extras=(u,),
                         epi=lambda acc, uu: (acc * (2.0 * jnp.maximum(uu, 0.0)),), plan=_sibling_share_plan([g_send]))
    pair[W_MO] = pair_of(W_MO, act, dmlp, g_other, "mm_g_mlp_out_own")
    g_send, (part,) = _mm(h2, du, name="mm_g_mlp_in_send", ta=True, bn=512, a_blocks=half_blocks(W_MI, False),
                          plan=_chip_exchange_plan({W_MO: pair[W_MO]}, near))
    dh2, res = _mm(du, w_mi, name="mm_d_h2", tb=True,
                   plan=_join(_chip_exchange_plan({W_MO: pair[W_MO]}, far, {W_MO: part}), _sibling_share_plan([g_send])))
    sum_slots([W_MO], res[:1])
    pair[W_MI] = pair_of(W_MI, h2, du, res[1], "mm_g_mlp_in_own")
    dx1, dmo, st_n2 = _norm2_bwd(dh2, x1, dy, mo, norm2_gain, mod8)
    def merge_bwd(dm, ga, gb, ya_, yb_):
        sa, sb = _sig(ga), _sig(gb)
        return dm * sa, dm * sb, dm * ya_ * sa * (1.0 - sa), dm * yb_ * sb * (1.0 - sb)

    dya, dyb, dga, dgb = _mm(dmo, w_out, name="mm_d_merged", tb=True, bn=MERGE_BC, out_dtypes=(BF16,) * 4,
                             extras=(proj, proj, ya, yb), extra_cols=gate_cols + (0, 0), epi=merge_bwd)
    g_out = _mm(merged, dmo, name="mm_g_out", ta=True, bn=512)
    do_a = _mm(dya, w_a, name="mm_d_oa", tb=True)
    g_a = _mm(o_a, dya, name="mm_g_branch_a", ta=True, bn=512)
    do_b = _mm(dyb, w_b, name="mm_d_ob", tb=True)
    g_b = _mm(o_b, dyb, name="mm_g_branch_b", ta=True, bn=512)
    mid = [W_A, W_B, W_OUT]
    (dqb, dkvb, st_swa), res = _swa_bwd(
        proj, q_gain, k_gain, sinks, do_b,
        plan=_join(_chip_exchange_plan({W_MI: pair[W_MI]}), exchange(mid, [g_a, g_b, g_out])))
    sum_slots([W_MI], res[:1])
    pair_sums(mid, [g_a, g_b, g_out], res[1:])
    (dqa, dfa, dia, dgga, d_lb, d_og), slots_mid = _hgrn_bwd(
        proj, lb_logits, o_gain, states, do_a, plan=_chip_exchange_plan({w: pair[w] for w in mid}))
    sum_slots(mid, slots_mid)
    dproj = jnp.concatenate([dqa, dfa, dia, dgga, dqb, dkvb, dga, dgb], axis=1)
    done = [W_A, W_B, W_OUT, W_MI, W_MO]
    g_send, res = _mm(h, dproj, name="mm_g_in_send", ta=True, bn=512, a_blocks=half_blocks(W_IN, False),
                      plan=_sibling_share_plan([half[w] for w in done]))
    theirs = dict(zip(done, res))
    g_own, (g_other,) = _mm(h, dproj, name="mm_g_in_own", ta=True, bn=512, a_blocks=half_blocks(W_IN, True),
                            plan=_sibling_share_plan([g_send]))
    pair[W_IN] = _add_bf16(g_own, g_other, "pair_sum0")[None]
    dh, slots_in = _mm(dproj, w_in, name="mm_d_h", tb=True, bk=2432, plan=_chip_exchange_plan({W_IN: pair[W_IN]}))
    sum_slots([W_IN], slots_in)
    grad_x, st_n1 = _norm1_bwd(dh, x, dx1, norm1_gain, mod8)
    (theirs[W_IN],) = _run_plan(_sibling_share_plan([half[W_IN]]), "sibling_share_w_in")
    stats = dict(loss=st_loss, n2=st_n2, n1=st_n1, d_lb=d_lb, d_og=d_og, swa=st_swa)
    return grad_x, [half[w] for w in range(N_W)], [theirs[w] for w in range(N_W)], stats


EW_VMEM_BYTES = 40 << 20


def _ew_rows(rows, cols, streams):
    br = 8
    while br * 2 * 4 <= rows and br * 2 * cols * 4 * 2 * streams <= EW_VMEM_BYTES and rows % (br * 2) == 0:
        br *= 2
    return br


CAST_STEPS = 16


def _cast_into_full(shards, name, plan=None):
    ws = sorted(shards)
    in_specs, out_specs, out_shape = [], [], []
    for w in ws:
        sr, sc = shards[w].shape
        R, C, by_col = W_SHAPES[w]
        br = sr // CAST_STEPS
        assert br * CAST_STEPS == sr and br % 16 == 0, (w, sr)

        def out_map(i, by_col=by_col):
            chip = 2 * lax.axis_index("x") + lax.axis_index("y")
            return (i, chip) if by_col else (chip * CAST_STEPS + i, 0)

        in_specs.append(pl.BlockSpec((br, sc), lambda i: (i, 0)))
        out_specs.append(pl.BlockSpec((br, sc), out_map))
        out_shape.append(jax.ShapeDtypeStruct((R, C), BF16))

    def body(*refs):
        for w_ref, o_ref in zip(refs[:len(ws)], refs[len(ws):]):
            o_ref[...] = w_ref[...].astype(BF16)

    res = _pcall(body, plan=plan, name=name, grid=(CAST_STEPS,), in_specs=in_specs, out_specs=out_specs,
                 out_shape=out_shape, compiler_params=_params(("arbitrary",)))(*[shards[w] for w in ws])
    if plan is None:
        return dict(zip(ws, res))
    return dict(zip(ws, res[0])), res[1]


def _adamw_math(w, g, m, v):
    m = ADAM_B1 * m + (1.0 - ADAM_B1) * g
    v = ADAM_B2 * v + (1.0 - ADAM_B2) * (g * g)
    m_hat = m / (1.0 - ADAM_B1 ** ADAM_STEP)
    v_hat = v / (1.0 - ADAM_B2 ** ADAM_STEP)
    delta = -ADAM_LR * (m_hat / (jnp.sqrt(v_hat) + ADAM_EPS) + ADAM_WD * w)
    return delta, m, v


def _adamw(w, g, m, v, name):
    R, C = w.shape
    br = _ew_rows(R, C, 7)
    spec = pl.BlockSpec((br, C), lambda i: (i, 0))

    def body(w_ref, g_ref, m_ref, v_ref, d_ref, nm_ref, nv_ref):
        d_ref[...], nm_ref[...], nv_ref[...] = _adamw_math(w_ref[...], g_ref[...], m_ref[...], v_ref[...])

    sh = jax.ShapeDtypeStruct((R, C), F32)
    return _pcall(body, name=name, grid=(R // br,), in_specs=[spec] * 4, out_specs=[spec] * 3, out_shape=[sh] * 3,
                  compiler_params=_params(("parallel",)))(w, g, m, v)


def _add_bf16(a, b, name):
    R, C = a.shape
    br = _ew_rows(R, C, 2.5)
    spec = pl.BlockSpec((br, C), lambda i: (i, 0))

    def body(a_ref, b_ref, o_ref):
        o_ref[...] = (a_ref[...] + b_ref[...]).astype(BF16)

    return _pcall(body, name=name, grid=(R // br,), in_specs=[spec, spec], out_specs=spec,
                  out_shape=jax.ShapeDtypeStruct((R, C), BF16), compiler_params=_params(("parallel",)))(a, b)


def _adamw_halves(w, own, other, m, v, c_arr, name):
    R, C = w.shape
    hr = R // 2
    br = _ew_rows(hr, C, 9)
    nb = hr // br
    full = pl.BlockSpec((br, C), lambda h, i, c_ref: (h * nb + i, 0))

    def own_map(h, i, c_ref):
        return jnp.where(h == c_ref[0], i, jnp.where(c_ref[0] == 0, nb - 1, 0)), 0

    def other_map(h, i, c_ref):
        return jnp.where(h != c_ref[0], i, jnp.where(c_ref[0] == 0, 0, nb - 1)), 0

    def body(c_ref, w_ref, own_ref, oth_ref, m_ref, v_ref, g_ref, d_ref, nm_ref, nv_ref):
        g = jnp.where(pl.program_id(0) == c_ref[0], own_ref[...], oth_ref[...])
        g_ref[...] = g
        d_ref[...], nm_ref[...], nv_ref[...] = _adamw_math(w_ref[...], g, m_ref[...], v_ref[...])

    sh = jax.ShapeDtypeStruct((R, C), F32)
    return _pcall(
        body, name=name,
        grid_spec=pltpu.PrefetchScalarGridSpec(
            num_scalar_prefetch=1, grid=(2, nb),
            in_specs=[full, pl.BlockSpec((br, C), own_map), pl.BlockSpec((br, C), other_map), full, full],
            out_specs=[full] * 4),
        out_shape=[sh] * 4, compiler_params=_params(("arbitrary", "arbitrary")))(c_arr, w, own, other, m, v)


def _ada_grad_adamw(c_t, dmod, w, m, v):
    R, C = w.shape
    br = _ew_rows(R, C, 8)
    spec = pl.BlockSpec((br, C), lambda i: (i, 0))

    def body(c_ref, dm_ref, w_ref, m_ref, v_ref, g_ref, d_ref, nm_ref, nv_ref):
        cv = c_ref[...]
        sc = cv * _sig(cv)
        g = sc[:, 0:1] * dm_ref[0:1, :]
        for b in range(1, N_DEV):
            g = g + sc[:, b:b + 1] * dm_ref[b:b + 1, :]
        g_ref[...] = g
        d_ref[...], nm_ref[...], nv_ref[...] = _adamw_math(w_ref[...], g, m_ref[...], v_ref[...])

    sh = jax.ShapeDtypeStruct((R, C), F32)
    return _pcall(
        body, name="ada_grad_adamw", grid=(R // br,),
        in_specs=[pl.BlockSpec((br, N_DEV), lambda i: (i, 0)), pl.BlockSpec((N_DEV, C), lambda i: (0, 0)), spec, spec, spec],
        out_specs=[spec] * 4, out_shape=[sh] * 4, compiler_params=_params(("parallel",)))(c_t, dmod, w, m, v)


SMALL_ROWS = 16


def _small_sum(small_all, lb_logits):
    def body(s_ref, lbl_ref, o_ref):
        acc = s_ref[0:SMALL_ROWS, :]
        for d in range(1, N_DEV):
            acc = acc + s_ref[d * SMALL_ROWS:(d + 1) * SMALL_ROWS, :]
        o_ref[...] = acc
        z = lbl_ref[...]
        e = jnp.exp(z - jnp.max(z, axis=0, keepdims=True))
        p0 = e[0:1, :] / (e[0:1, :] + e[1:2, :])
        dz = acc[8:9, 0:A_WIDTH] * p0 * (1.0 - p0)
        o_ref[8:9, 0:A_WIDTH] = dz
        o_ref[10:11, 0:A_WIDTH] = -dz

    return _pcall(body, name="small_sum", out_shape=jax.ShapeDtypeStruct((SMALL_ROWS, D_MODEL), F32),
                  in_specs=[pl.BlockSpec(memory_space=pltpu.VMEM)] * 2, out_specs=pl.BlockSpec(memory_space=pltpu.VMEM),
                  compiler_params=_params())(small_all, lb_logits)


RELATIONS = ((1, 0), (0, 1), (1, 1))
ANY = pl.BlockSpec(memory_space=pl.ANY)


def _place():
    x, y, c = lax.axis_index("x"), lax.axis_index("y"), lax.axis_index("c")
    return x, y, c


def _allgather_small(x_shard, name):
    m_per, n = x_shard.shape

    def body(x_ref, out_ref, send_sems, recv_sems, local_sem):
        x, y, c = _place()
        me, sibling = (x, y, c), (x, y, 1 - c)
        chips = [(1 - x, y), (x, 1 - y), (1 - x, 1 - y)]

        def rows(px, py, pc):
            return out_ref.at[pl.ds((4 * px + 2 * py + pc) * m_per, m_per), :]

        def copy(k, block, to, src=None):
            return pltpu.make_async_remote_copy(
                src_ref=rows(*block) if src is None else src, dst_ref=rows(*block),
                send_sem=send_sems.at[k], recv_sem=recv_sems.at[k], device_id=to, device_id_type=MESH)

        mine = pltpu.make_async_copy(x_ref, rows(*me), local_sem)
        mine.start()
        first = [copy(0, me, sibling, src=x_ref)]
        first += [copy(1 + j, me, (*chip, c), src=x_ref) for j, chip in enumerate(chips)]
        for cp in first:
            cp.start()
        passed = [copy(4 + j, (*chip, c), sibling) for j, chip in enumerate(chips)]
        for j, chip in enumerate(chips):
            copy(1 + j, (*chip, c), me).wait_recv()
            passed[j].start()
        copy(0, sibling, me).wait_recv()
        for j, chip in enumerate(chips):
            copy(4 + j, (*chip, 1 - c), me).wait_recv()
        for cp in first + passed:
            cp.wait_send()
        mine.wait()

    return _pcall(
        body, name=name, out_shape=jax.ShapeDtypeStruct((N_DEV * m_per, n), x_shard.dtype),
        in_specs=[pl.BlockSpec(memory_space=pltpu.VMEM)], out_specs=pl.BlockSpec(memory_space=pltpu.VMEM),
        scratch_shapes=[pltpu.SemaphoreType.DMA((7,)), pltpu.SemaphoreType.DMA((7,)), pltpu.SemaphoreType.DMA],
        compiler_params=_params(),
    )(x_shard)


W_SHAPES = ((D_MODEL, IN_WIDTH, True), (A_WIDTH, D_MODEL, True), (B_WIDTH, D_MODEL, True),
            (D_MODEL, D_MODEL, False), (D_MODEL, MLP_HIDDEN, True), (MLP_HIDDEN, D_MODEL, False))
N_W = len(W_SHAPES)


def _shard_shape(w):
    R, C, by_col = W_SHAPES[w]
    return (R, C // N_CHIPS) if by_col else (R // N_CHIPS, C)


def _half_shape(w):
    sr, sc = _shard_shape(w)
    return sr // 2, sc


def _region(full_ref, w, chip, half, quarter=None):
    sr, sc = _shard_shape(w)
    by_col = W_SHAPES[w][2]
    r0, c0 = (0, chip * sc) if by_col else (chip * sr, 0)
    r0, rows = r0 + half * (sr // 2), sr // 2
    if quarter is not None:
        r0, rows = r0 + quarter * (rows // 2), rows // 2
    return full_ref.at[pl.ds(r0, rows), pl.ds(c0, sc)]


def _on_device(fn):
    x, y, c = _place()
    me = 4 * x + 2 * y + c
    for d in range(N_DEV):
        @pl.when(me == d)
        def _(d=d):
            fn(x, y, c, d)


GATHER_COPIES = (
    (0, 0, None, "x"), (0, 0, None, "y"),
    (1, 2, 0, "y"), (1, 1, 1, "x"),
    (1, 2, None, "s"), (1, 1, None, "s"),
    (2, 3, 0, "s"), (2, 3, 1, "s"),
)
PEER_FLIP = {"x": 2, "y": 1, "s": 0}


GATHER_STAGES = {
    None: (((), (0, 1), ()), ((0, 1), (2, 3, 4, 5), ()), ((2, 3), (6, 7), ()), ((4, 5, 6, 7), (), tuple(range(8)))),
    "near": (((), (0, 1), ()), ((0, 1), (), (0, 1))),
    "far": (((), (2, 3, 4, 5), ()), ((2, 3), (6, 7), ()), ((4, 5, 6, 7), (), (2, 3, 4, 5, 6, 7))),
}


def _gather_plan(partials, pass_at=(0.5, 0.75), part=None):
    ws = sorted(partials)
    n_t = len(GATHER_COPIES)
    jobs = [(i, w) for i, w in enumerate(ws)]

    def copy(pi, po, ps, x, y, c, d, i, w, t, landing):
        chip, dc = d >> 1, d & 1
        stage, flip, quarter, to = GATHER_COPIES[t]
        if landing:
            peer_chip = chip ^ PEER_FLIP[to]
            part = _region(po[i], w, peer_chip ^ flip, (1 - dc) if to == "s" else dc, quarter)
            src = part
        else:
            part = _region(po[i], w, chip ^ flip, dc, quarter)
            here = flip != 0 and (part_of is None or stage == 2)
            src = part if here else _region(pi[i], w, chip ^ flip, dc, quarter)
        target = {"x": (x ^ 1, y, c), "y": (x, y ^ 1, c), "s": (x, y, 1 - c)}[to]
        return pltpu.make_async_remote_copy(
            src_ref=src, dst_ref=part, send_sem=ps[0].at[i * n_t + t], recv_sem=ps[1].at[i * n_t + t],
            device_id=target, device_id_type=MESH)

    part_of = part

    def stage(landed, started, sent):
        def run(pi, po, ps):
            def on(x, y, c, d):
                for i, w in jobs:
                    for t in landed:
                        copy(pi, po, ps, x, y, c, d, i, w, t, True).wait_recv()
                for i, w in jobs:
                    for t in started:
                        copy(pi, po, ps, x, y, c, d, i, w, t, False).start()
                for i, w in jobs:
                    for t in sent:
                        copy(pi, po, ps, x, y, c, d, i, w, t, False).wait_send()
            _on_device(on)
        return run

    stages = [stage(*st) for st in GATHER_STAGES[part]]
    mid_at = tuple(pass_at) if part is None else tuple(pass_at)[:len(stages) - 2]
    return _Plan([partials[w] for w in ws], [jax.ShapeDtypeStruct(W_SHAPES[w][:2], BF16) for w in ws],
                 [pltpu.SemaphoreType.DMA((n_t * len(ws),)) for _ in range(2)], stages,
                 {i: i for i in range(len(ws))}, mid_at=mid_at)


def _grad_view(g, w):
    R, C, by_col = W_SHAPES[w]
    return g.reshape(1, 2, R // 2, C) if by_col else g.reshape(N_CHIPS, 2, R // N_CHIPS // 2, C)


def _start_wait_plan(ins, outs, n_copies, copies):
    def start(pi, po, ps):
        for cp in copies(pi, po, ps):
            cp.start()

    def finish(pi, po, ps):
        for cp in copies(pi, po, ps):
            cp.wait()

    return _Plan(ins, outs, [pltpu.SemaphoreType.DMA((n_copies,)), pltpu.SemaphoreType.DMA((n_copies,))], [start, finish])


def _sibling_exchange_plan(g4s):
    pieces = [(i, p) for i, g in enumerate(g4s) for p in range(g.shape[0])]

    def copies(pi, po, ps):
        x, y, c = _place()
        return [pltpu.make_async_remote_copy(
            src_ref=pi[i].at[p, 1 - c], dst_ref=po[i].at[p], send_sem=ps[0].at[n], recv_sem=ps[1].at[n],
            device_id=(x, y, 1 - c), device_id_type=MESH) for n, (i, p) in enumerate(pieces)]

    return _start_wait_plan(list(g4s), [jax.ShapeDtypeStruct((g.shape[0],) + g.shape[2:], F32) for g in g4s],
                            len(pieces), copies)


def _pair_sum(g4, other, c_arr, name):
    P, _, hr, C = g4.shape
    br = _ew_rows(hr, C, 2.5)

    def body(c_ref, g_ref, o_ref, p_ref):
        p_ref[...] = (g_ref[...] + o_ref[...]).astype(BF16)

    return _pcall(
        body, name=name,
        grid_spec=pltpu.PrefetchScalarGridSpec(
            num_scalar_prefetch=1, grid=(P, hr // br),
            in_specs=[pl.BlockSpec((None, None, br, C), lambda p, i, c_ref: (p, c_ref[0], i, 0)),
                      pl.BlockSpec((None, br, C), lambda p, i, c_ref: (p, i, 0))],
            out_specs=pl.BlockSpec((None, br, C), lambda p, i, c_ref: (p, i, 0))),
        out_shape=jax.ShapeDtypeStruct((P, hr, C), BF16),
        compiler_params=_params(("parallel", "parallel")),
    )(c_arr, g4, other)


def _pair_part(p_ref, w, chip):
    sr, sc = _shard_shape(w)
    return p_ref.at[0, :, pl.ds(chip * sc, sc)] if W_SHAPES[w][2] else p_ref.at[chip]


def _chip_exchange_plan(pairs, rels=(0, 1, 2), into=None):
    ws = sorted(pairs)
    n = len(ws)

    def stage(wait):
        def run(pi, po, ps):
            def on(x, y, c, d):
                for i, w in enumerate(ws):
                    for k, (rx, ry) in enumerate(RELATIONS):
                        if k not in rels:
                            continue
                        cp = pltpu.make_async_remote_copy(
                            src_ref=_pair_part(pi[i], w, (d >> 1) ^ (2 * rx + ry)), dst_ref=po[i].at[k],
                            send_sem=ps[0].at[i * 3 + k], recv_sem=ps[1].at[i * 3 + k],
                            device_id=(x ^ rx, y ^ ry, c), device_id_type=MESH)
                        if wait:
                            cp.wait()
                        else:
                            cp.start()
            _on_device(on)
        return run

    ins = [pairs[w] for w in ws] + ([into[w] for w in ws] if into else [])
    return _Plan(ins, [jax.ShapeDtypeStruct((3,) + _half_shape(w), BF16) for w in ws],
                 [pltpu.SemaphoreType.DMA((3 * n,)), pltpu.SemaphoreType.DMA((3 * n,))],
                 [stage(False), stage(True)], {n + i: i for i in range(n)} if into else None)


def _sum_slots(pair, slots, w, chip_arr, name):
    _, hr, C = slots.shape
    br = _ew_rows(hr, C, 3)
    own_map = (lambda i, chip: (0, i, chip[0])) if W_SHAPES[w][2] else (lambda i, chip: (chip[0], i, 0))

    def body(chip_ref, p_ref, s_ref, o_ref):
        acc = p_ref[...].astype(F32)
        for k in range(3):
            acc = acc + s_ref[k].astype(F32)
        o_ref[...] = acc

    return _pcall(
        body, name=name,
        grid_spec=pltpu.PrefetchScalarGridSpec(
            num_scalar_prefetch=1, grid=(hr // br,),
            in_specs=[pl.BlockSpec((None, br, C), own_map), pl.BlockSpec((3, br, C), lambda i, chip: (0, i, 0))],
            out_specs=pl.BlockSpec((br, C), lambda i, chip: (i, 0))),
        out_shape=jax.ShapeDtypeStruct((hr, C), F32), compiler_params=_params(("parallel",)),
    )(chip_arr, pair, slots)


def _sibling_share_plan(halves):
    def copies(pi, po, ps):
        x, y, c = _place()
        return [pltpu.make_async_remote_copy(
            src_ref=pi[i], dst_ref=po[i], send_sem=ps[0].at[i], recv_sem=ps[1].at[i],
            device_id=(x, y, 1 - c), device_id_type=MESH) for i in range(len(halves))]

    return _start_wait_plan(list(halves), [jax.ShapeDtypeStruct(h.shape, F32) for h in halves], len(halves), copies)


def _pad_lanes(v, width=D_MODEL):
    return jnp.pad(v, ((0, 0), (0, width - v.shape[1])))


def _pack_small(b_ada, norm1, norm2, lb, o_gain, q_gain, k_gain, sinks):
    rows = [b_ada.reshape(N_MOD, D_MODEL), norm1, norm2, jnp.concatenate([lb[0:1], o_gain], axis=1),
            _pad_lanes(jnp.concatenate([q_gain, k_gain, sinks], axis=1)), _pad_lanes(lb[1:2]),
            jnp.zeros((SMALL_ROWS - 11, D_MODEL), F32)]
    return jnp.concatenate(rows, axis=0)


def _unpack_small(p):
    return (p[0:6].reshape(1, N_MOD * D_MODEL), p[6:7], p[7:8],
            jnp.concatenate([p[8:9, 0:A_WIDTH], p[10:11, 0:A_WIDTH]], axis=0), p[8:9, A_WIDTH:],
            p[9:10, 0:64], p[9:10, 64:128], p[9:10, 128:144])


def kernel(x, c, w_ada, b_ada, norm1_gain, w_in, lb_logits, hgrn_o_gain, q_norm_gain, k_norm_gain, sinks, w_branch_a, w_branch_b, w_out, norm2_gain, w_mlp_in, w_mlp_out, loss_target, m_w_ada, m_b_ada, m_norm1_gain, m_w_in, m_lb_logits, m_hgrn_o_gain, m_q_norm_gain, m_k_norm_gain, m_sinks, m_w_branch_a, m_w_branch_b, m_w_out, m_norm2_gain, m_w_mlp_in, m_w_mlp_out, v_w_ada, v_b_ada, v_norm1_gain, v_w_in, v_lb_logits, v_hgrn_o_gain, v_q_norm_gain, v_k_norm_gain, v_sinks, v_w_branch_a, v_w_branch_b, v_w_out, v_norm2_gain, v_w_mlp_in, v_w_mlp_out):
    xi, yi, ci = _place()
    chip = 2 * xi + yi
    me = 4 * xi + 2 * yi + ci
    ada_cols = w_ada.shape[2]

    c_all = _allgather_small(jnp.broadcast_to(c, (8, D_MODEL)), "gather_c").reshape(N_DEV, 8, D_MODEL)[:, 0]
    b_cols = lax.dynamic_slice(b_ada, (0, chip * ada_cols), (1, ada_cols))
    mod_part = _ada_fwd(c_all, w_ada[0], b_cols)
    mod_all = _allgather_small(mod_part, "gather_mod").reshape(N_CHIPS, 2, N_DEV, ada_cols)[:, 0]
    mod_mine = lax.dynamic_index_in_dim(mod_all, me, axis=1, keepdims=False).reshape(N_MOD, D_MODEL)
    mod8 = jnp.concatenate([mod_mine, jnp.zeros((2, D_MODEL), F32)], axis=0)

    shards = (w_in[0], w_branch_a[0], w_branch_b[0], w_out[0], w_mlp_in[0], w_mlp_out[0])
    chip_arr = chip.astype(jnp.int32).reshape(1)
    c_arr = ci.astype(jnp.int32).reshape(1)

    grad_x, halves, theirs, st = _local_step(x[0], loss_target[0], mod8, norm1_gain, norm2_gain, lb_logits, hgrn_o_gain,
                                             q_norm_gain, k_norm_gain, sinks, shards, c_arr, chip_arr)
    loss = lax.psum(0.5 * jnp.sum(st["loss"][0]) / D_MODEL, ("x", "y", "c"))
    moments = ((m_w_in, v_w_in), (m_w_branch_a, v_w_branch_a), (m_w_branch_b, v_w_branch_b), (m_w_out, v_w_out),
               (m_w_mlp_in, v_w_mlp_in), (m_w_mlp_out, v_w_mlp_out))
    big = [_adamw_halves(shards[w], halves[w], theirs[w], moments[w][0][0], moments[w][1][0], c_arr, f"adamw{w}")
           for w in range(N_W)]

    swa = st["swa"]
    small = jnp.concatenate([
        st["n1"][1:2], st["n1"][0:1], st["n2"][3:4], st["n2"][1:2], st["n2"][0:1], st["loss"][1:2],
        st["n1"][2:3], st["n2"][2:3], jnp.concatenate([st["d_lb"][0:1], st["d_og"][0:1]], axis=1),
        _pad_lanes(jnp.concatenate([swa[0:1, 0:64], swa[1:2, 0:64], swa[2:3, 0:16]], axis=1)),
        jnp.zeros((SMALL_ROWS - 10, D_MODEL), F32)], axis=0)
    small_all = _allgather_small(small, "gather_small")
    g_small = _small_sum(small_all, lb_logits)
    small_w = (b_ada, norm1_gain, norm2_gain, lb_logits, hgrn_o_gain, q_norm_gain, k_norm_gain, sinks)
    small_m = (m_b_ada, m_norm1_gain, m_norm2_gain, m_lb_logits, m_hgrn_o_gain, m_q_norm_gain, m_k_norm_gain, m_sinks)
    small_v = (v_b_ada, v_norm1_gain, v_norm2_gain, v_lb_logits, v_hgrn_o_gain, v_q_norm_gain, v_k_norm_gain, v_sinks)
    sm = [_unpack_small(t) for t in
          (g_small,) + tuple(_adamw(_pack_small(*small_w), g_small, _pack_small(*small_m), _pack_small(*small_v),
                                    "adamw_small"))]
    g_b, g_n1, g_n2, g_lb, g_og, g_qg, g_kg, g_sk = ([t[i] for t in sm] for i in range(8))

    dmod_all = small_all.reshape(N_DEV, SMALL_ROWS, D_MODEL)[:, 0:N_MOD].reshape(N_DEV, N_MOD * D_MODEL)
    dmod_cols = lax.dynamic_slice(dmod_all, (0, chip * ada_cols), (N_DEV, ada_cols))
    ada = _ada_grad_adamw(c_all.T, dmod_cols, w_ada[0], m_w_ada[0], v_w_ada[0])

    def ordered(k):
        lead = lambda a: a[None]
        return (lead(ada[k]), g_b[k], g_n1[k], lead(big[0][k]), g_lb[k], g_og[k], g_qg[k], g_kg[k], g_sk[k],
                lead(big[1][k]), lead(big[2][k]), lead(big[3][k]), g_n2[k], lead(big[4][k]), lead(big[5][k]))

    return (loss, grad_x[None]) + ordered(0) + ordered(1) + ordered(2) + ordered(3)
```

```python
import jax
import jax.numpy as jnp
from jax import lax
from jax.experimental import pallas as pl
from jax.experimental.pallas import tpu as pltpu

F32 = jnp.float32
BF16 = jnp.bfloat16
HIGHEST = lax.Precision.HIGHEST
MESH = pl.DeviceIdType.MESH

D_MODEL = 2048
A_WIDTH = 1024
A_HEADS = 8
A_HEAD_DIM = 128
A_CHUNK = 64
B_WIDTH = 1024
B_HEAD_DIM = 64
B_GROUP = 4
B_KV_HEADS = 4
B_KV_WIDTH = 256
BLOCK = 128
MLP_HIDDEN = 8192
IN_WIDTH = 9728
N_MOD = 6
EPS = 1e-6
N_CHIPS = 4
N_DEV = 8

OFF_QA, OFF_FA, OFF_IA, OFF_GA = 0, 1024, 2048, 3072
OFF_QB, OFF_KB, OFF_VB = 4096, 5120, 5376
OFF_GATE_A, OFF_GATE_B = 5632, 7680

ADAM_LR = 0.001
ADAM_B1 = 0.9
ADAM_B2 = 0.999
ADAM_EPS = 1e-08
ADAM_WD = 0.01
ADAM_STEP = 10

VMEM_LIMIT_V7X = 48 * 1024 * 1024
NEG_BIG = -1e30


def _params(sem=None, vmem=VMEM_LIMIT_V7X):
    return pltpu.CompilerParams(dimension_semantics=sem, vmem_limit_bytes=vmem)


class _Plan:
    def __init__(self, ins, outs, sems, stages, aliases=None, mid_at=()):
        self.ins, self.outs, self.sems, self.stages, self.aliases = ins, outs, sems, stages, aliases or {}
        self.mid_at = tuple(mid_at)
        assert len(self.mid_at) == len(stages) - 2


def _join(a, b):
    assert len(a.stages) == 2 and len(b.stages) == 2
    ni, no, ns = len(a.ins), len(a.outs), len(a.sems)

    def stage(k):
        def run(pi, po, ps):
            a.stages[k](pi[:ni], po[:no], ps[:ns])
            b.stages[k](pi[ni:], po[no:], ps[ns:])
        return run

    aliases = dict(a.aliases)
    aliases.update({ni + i: no + o for i, o in b.aliases.items()})
    return _Plan(a.ins + b.ins, a.outs + b.outs, a.sems + b.sems, [stage(0), stage(1)], aliases)


def _pcall(body, plan=None, **kw):
    if plan is None:
        return pl.pallas_call(body, **kw)
    grid = kw["grid"]
    single = not isinstance(kw["out_specs"], (list, tuple))
    in_specs = list(kw["in_specs"])
    out_specs = [kw["out_specs"]] if single else list(kw["out_specs"])
    out_shape = [kw["out_shape"]] if single else list(kw["out_shape"])
    scratch = list(kw.get("scratch_shapes", ()))
    n_in, n_out, n_scr = len(in_specs), len(out_specs), len(scratch)
    n_pi, n_po = len(plan.ins), len(plan.outs)
    total = 1
    for g in grid:
        total *= g
    n_st = len(plan.stages)

    def wrapped(*refs):
        o0 = n_in + n_pi
        s0 = o0 + n_out + n_po
        pi, po, ps = refs[n_in:o0], refs[o0 + n_out:s0], refs[s0 + n_scr:]
        lin = 0
        for d, g in enumerate(grid):
            lin = lin * g + pl.program_id(d)
        for si, frac in enumerate((0.0,) + plan.mid_at):
            @pl.when(lin == int(frac * (total - 1)))
            def _(si=si):
                plan.stages[si](pi, po, ps)
        body(*refs[:n_in], *refs[o0:o0 + n_out], *refs[s0:s0 + n_scr])

        @pl.when(lin == total - 1)
        def _():
            plan.stages[-1](pi, po, ps)

    any_spec = pl.BlockSpec(memory_space=pl.ANY)
    call = pl.pallas_call(
        wrapped, name=kw["name"], grid=grid, in_specs=in_specs + [any_spec] * n_pi,
        out_specs=out_specs + [any_spec] * n_po, out_shape=out_shape + list(plan.outs),
        scratch_shapes=scratch + list(plan.sems),
        input_output_aliases={n_in + i: n_out + o for i, o in plan.aliases.items()},
        compiler_params=_params(("arbitrary",) * len(grid)))

    def run(*args):
        res = call(*args, *plan.ins)
        outs = list(res[:n_out])
        return (outs[0] if single else outs), list(res[n_out:])

    return run


def _run_plan(plan, name):
    return _pcall(lambda: None, plan=plan, name=name, grid=(1,), in_specs=[], out_specs=[], out_shape=[])()[1]


def _sig(x):
    return 1.0 / (1.0 + jnp.exp(-x))


def _nn(a, b):
    return lax.dot_general(a.astype(BF16), b.astype(BF16), (((1,), (0,)), ((), ())), preferred_element_type=F32)


def _nt(a, b):
    return lax.dot_general(a.astype(BF16), b.astype(BF16), (((1,), (1,)), ((), ())), preferred_element_type=F32)


def _tn(a, b):
    return lax.dot_general(a.astype(BF16), b.astype(BF16), (((0,), (0,)), ((), ())), preferred_element_type=F32)


def _mm(a, b, *, name, ta=False, tb=False, bm=1024, bn=1024, bk=2048, out_dtypes=(F32,), epi=None, extras=(),
        extra_cols=None, plan=None, a_blocks=None, row_extras=(), n_stats=0):
    if ta:
        K, M = a.shape
        bk = K
        if a_blocks is not None:
            M = a_blocks[0] * bm
    else:
        M, K = a.shape
    if tb:
        N, K2 = b.shape
    else:
        K2, N = b.shape
    bm, bn, bk = min(bm, M), min(bn, N), min(bk, K)
    assert K == K2 and M % bm == 0 and N % bn == 0 and K % bk == 0, (name, a.shape, b.shape)
    nk = K // bk
    a_col = a_blocks[1] if a_blocks is not None else (lambda i: i)
    a_spec = pl.BlockSpec((bk, bm), lambda i, j, k: (k, a_col(i))) if ta else pl.BlockSpec((bm, bk), lambda i, j, k: (i, k))
    b_spec = pl.BlockSpec((bn, bk), lambda i, j, k: (j, k)) if tb else pl.BlockSpec((bk, bn), lambda i, j, k: (k, j))
    t_spec = pl.BlockSpec((bm, bn), lambda i, j, k: (i, j))
    extra_cols = extra_cols or (0,) * len(extras)
    e_specs = [pl.BlockSpec((bm, bn), lambda i, j, k, off=off: (i, off + j)) for off in extra_cols]
    e_specs += [pl.BlockSpec((8, bn), lambda i, j, k: (0, j)) for _ in row_extras]
    dims = (((1,), (1 if tb else 0,)), ((), ()))
    n_e, n_o = len(extras) + len(row_extras), len(out_dtypes)
    stat_spec = pl.BlockSpec((8, bn), lambda i, j, k: (i, j))

    def body(*refs):
        a_ref, b_ref = refs[0], refs[1]
        e_refs = refs[2:2 + n_e]
        o_refs = refs[2 + n_e:2 + n_e + n_o]

        def finish(acc):
            outs = (acc,) if epi is None else epi(acc, *[e[...] for e in e_refs])
            for o_ref, o in zip(o_refs, outs):
                o_ref[...] = o.astype(o_ref.dtype)

        if ta:
            at_ref = refs[-1]

            @pl.when(pl.program_id(1) == 0)
            def _():
                at_ref[...] = a_ref[...].T

            lhs = at_ref[...]
        else:
            lhs = a_ref[...].astype(BF16)
        part = lax.dot_general(lhs, b_ref[...].astype(BF16), dims, preferred_element_type=F32)
        if nk == 1:
            finish(part)
        else:
            acc_ref = refs[-1]
            k = pl.program_id(2)

            @pl.when(k == 0)
            def _():
                acc_ref[...] = part

            @pl.when(k > 0)
            def _():
                acc_ref[...] += part

            @pl.when(k == nk - 1)
            def _():
                finish(acc_ref[...])

    if ta:
        assert a.dtype == BF16 and nk == 1
        scratch = [pltpu.VMEM((bm, bk), BF16)]
    else:
        scratch = [pltpu.VMEM((bm, bn), F32)] if nk > 1 else []
    out = _pcall(
        body, plan=plan, name=name, grid=(M // bm, N // bn, nk),
        in_specs=[a_spec, b_spec] + e_specs,
        out_specs=[t_spec] * (n_o - n_stats) + [stat_spec] * n_stats,
        out_shape=[jax.ShapeDtypeStruct((M, N), dt) for dt in out_dtypes[:n_o - n_stats]]
        + [jax.ShapeDtypeStruct((8 * (M // bm), N), F32)] * n_stats,
        scratch_shapes=scratch,
        compiler_params=_params(("parallel", "arbitrary", "arbitrary")),
    )(a, b, *extras, *row_extras)
    if plan is not None:
        return (out[0][0] if n_o == 1 else out[0]), out[1]
    return out[0] if n_o == 1 else out


def _ada_fwd(c_all, w_ada, b_cols):
    n = w_ada.shape[1]
    bn = 512

    def body(c_ref, w_ref, b_ref, o_ref):
        cv = c_ref[...]
        sc = cv * _sig(cv)
        o_ref[...] = jnp.dot(sc, w_ref[...], precision=HIGHEST, preferred_element_type=F32) + b_ref[...]

    return _pcall(
        body, name="ada_fwd", grid=(n // bn,),
        in_specs=[pl.BlockSpec((N_DEV, D_MODEL), lambda j: (0, 0)), pl.BlockSpec((D_MODEL, bn), lambda j: (0, j)),
                  pl.BlockSpec((1, bn), lambda j: (0, j))],
        out_specs=pl.BlockSpec((N_DEV, bn), lambda j: (0, j)),
        out_shape=jax.ShapeDtypeStruct((N_DEV, n), F32),
        compiler_params=_params(("parallel",)),
    )(c_all, w_ada, b_cols)


ROWS_EW = 256


def _rms_fwd_math(x, gain, scale, shift):
    rstd = lax.rsqrt(jnp.mean(x * x, axis=-1, keepdims=True) + EPS)
    xhat = x * rstd
    n = xhat * gain
    return n * (1.0 + scale) + shift, xhat, n, rstd


def _rms_bwd_math(dh, xhat, n, rstd, gain, scale):
    dn = dh * (1.0 + scale)
    dxhat = dn * gain
    dx = rstd * (dxhat - xhat * jnp.mean(dxhat * xhat, axis=-1, keepdims=True))
    d_scale = jnp.sum(dh * n, axis=0, keepdims=True)
    d_shift = jnp.sum(dh, axis=0, keepdims=True)
    d_gain = jnp.sum(dn * xhat, axis=0, keepdims=True)
    return dx, d_scale, d_shift, d_gain


def _row_spec(w=D_MODEL, br=ROWS_EW):
    return pl.BlockSpec((br, w), lambda i: (i, 0))


def _vec_spec(r=8, w=D_MODEL):
    return pl.BlockSpec((r, w), lambda i: (0, 0))


def _norm1_fwd(x, gain, mod8, plan=None):
    T = x.shape[0]

    def body(x_ref, g_ref, m_ref, h_ref):
        h, _, _, _ = _rms_fwd_math(x_ref[...], g_ref[...], m_ref[1:2, :], m_ref[0:1, :])
        h_ref[...] = h.astype(BF16)

    return _pcall(
        body, plan=plan, name="norm1_fwd", grid=(T // ROWS_EW,),
        in_specs=[_row_spec(), _vec_spec(1), _vec_spec()],
        out_specs=_row_spec(), out_shape=jax.ShapeDtypeStruct((T, D_MODEL), BF16),
        compiler_params=_params(("parallel",)),
    )(x, gain, mod8)


def _res_norm2_fwd(x, mo, gain, mod8):
    T = x.shape[0]
    br = ROWS_EW

    def body(x_ref, mo_ref, g_ref, m_ref, x1_ref, h_ref):
        x1 = x_ref[...] + m_ref[2:3, :] * mo_ref[...]
        x1_ref[...] = x1
        h, _, _, _ = _rms_fwd_math(x1, g_ref[...], m_ref[4:5, :], m_ref[3:4, :])
        h_ref[...] = h.astype(BF16)

    return _pcall(
        body, name="res_norm2_fwd", grid=(T // br,),
        in_specs=[_row_spec(br=br), _row_spec(br=br), _vec_spec(1), _vec_spec()],
        out_specs=[_row_spec(br=br), _row_spec(br=br)],
        out_shape=[jax.ShapeDtypeStruct((T, D_MODEL), F32), jax.ShapeDtypeStruct((T, D_MODEL), BF16)],
        compiler_params=_params(("parallel",)),
    )(x, mo, gain, mod8)


def _loss_head(mlp, x1, target, mod):
    gate = mod[5:6, :]
    err = x1 + gate * mlp - target
    dy = err * (1.0 / D_MODEL)
    row = lax.broadcasted_iota(jnp.int32, (8, mlp.shape[1]), 0)
    stats = jnp.where(row == 0, jnp.sum(err * err, axis=0, keepdims=True),
                      jnp.where(row == 1, jnp.sum(dy * mlp, axis=0, keepdims=True), 0.0))
    return dy, dy * gate, stats


def _norm2_bwd(dh2, x1, dy, mo, gain, mod8):
    T = x1.shape[0]

    def body(dh_ref, x1_ref, dy_ref, mo_ref, g_ref, m_ref, dx1_ref, dmo_ref, st_ref):
        i = pl.program_id(0)
        gain_v, scale = g_ref[...], m_ref[4:5, :]
        _, xhat, n, rstd = _rms_fwd_math(x1_ref[...], gain_v, scale, m_ref[3:4, :])
        dx, d_scale, d_shift, d_gain = _rms_bwd_math(dh_ref[...], xhat, n, rstd, gain_v, scale)
        dx1 = dy_ref[...] + dx
        dx1_ref[...] = dx1
        dmo_ref[...] = (dx1 * m_ref[2:3, :]).astype(BF16)

        @pl.when(i == 0)
        def _():
            st_ref[...] = jnp.zeros_like(st_ref)

        st_ref[0:1, :] += d_scale
        st_ref[1:2, :] += d_shift
        st_ref[2:3, :] += d_gain
        st_ref[3:4, :] += jnp.sum(dx1 * mo_ref[...], axis=0, keepdims=True)

    return _pcall(
        body, name="norm2_bwd", grid=(T // ROWS_EW,),
        in_specs=[_row_spec(), _row_spec(), _row_spec(), _row_spec(), _vec_spec(1), _vec_spec()],
        out_specs=[_row_spec(), _row_spec(), _vec_spec()],
        out_shape=[jax.ShapeDtypeStruct((T, D_MODEL), F32), jax.ShapeDtypeStruct((T, D_MODEL), BF16),
                   jax.ShapeDtypeStruct((8, D_MODEL), F32)],
        compiler_params=_params(("arbitrary",)),
    )(dh2, x1, dy, mo, gain, mod8)


def _norm1_bwd(dh, x, dx1, gain, mod8):
    T = x.shape[0]
    br = ROWS_EW

    def body(dh_ref, x_ref, dx1_ref, g_ref, m_ref, dx_ref, st_ref):
        i = pl.program_id(0)
        gain_v, scale = g_ref[...], m_ref[1:2, :]
        _, xhat, n, rstd = _rms_fwd_math(x_ref[...], gain_v, scale, m_ref[0:1, :])
        dx, d_scale, d_shift, d_gain = _rms_bwd_math(dh_ref[...], xhat, n, rstd, gain_v, scale)
        dx_ref[...] = dx1_ref[...] + dx

        @pl.when(i == 0)
        def _():
            st_ref[...] = jnp.zeros_like(st_ref)

        st_ref[0:1, :] += d_scale
        st_ref[1:2, :] += d_shift
        st_ref[2:3, :] += d_gain

    return _pcall(
        body, name="norm1_bwd", grid=(T // br,),
        in_specs=[_row_spec(br=br), _row_spec(br=br), _row_spec(br=br), _vec_spec(1), _vec_spec()],
        out_specs=[_row_spec(br=br), _vec_spec()],
        out_shape=[jax.ShapeDtypeStruct((T, D_MODEL), F32), jax.ShapeDtypeStruct((8, D_MODEL), F32)],
        compiler_params=_params(("arbitrary",)),
    )(dh, x, dx1, gain, mod8)


MERGE_BC = 512


def _hgrn_rows(T):
    return 512 if T >= 1024 else 128


def _lower_bound(lbl):
    e = jnp.exp(lbl - jnp.max(lbl, axis=0, keepdims=True))
    return e[0:1, :] / (e[0:1, :] + e[1:2, :])


def _chunk_sum_matrix(rows, backward):
    shift = A_CHUNK.bit_length() - 1
    r = lax.broadcasted_iota(jnp.int32, (rows, rows), 0)
    c = lax.broadcasted_iota(jnp.int32, (rows, rows), 1)
    same = jnp.right_shift(r, shift) == jnp.right_shift(c, shift)
    return (same & ((r <= c) if backward else (r >= c))).astype(BF16)


def _chunk_sums(m, x):
    n = x.shape[1]
    hi = x.astype(BF16)
    rest = x - hi.astype(F32)
    mid = rest.astype(BF16)
    lo = (rest - mid.astype(F32)).astype(BF16)
    y = jnp.dot(m, jnp.concatenate([hi, mid, lo], axis=1), preferred_element_type=F32)
    return y[:, 0:n] + y[:, n:2 * n] + y[:, 2 * n:3 * n]


def _hgrn_block_pre(q, fl, lb, m_fwd):
    sg = _sig(fl)
    f = lb + (1.0 - lb) * sg
    sq = _sig(q)
    return dict(sg=sg, f=f, k=1.0 - f, sq=sq, qf=q * sq, b=_chunk_sums(m_fwd, jnp.log(f)))


def _hgrn_chunk_local(pre, r):
    C = A_CHUNK
    qf, k, b = pre["qf"][r], pre["k"][r], pre["b"][r]
    causal = lax.broadcasted_iota(jnp.int32, (C, C), 0) >= lax.broadcasted_iota(jnp.int32, (C, C), 1)
    bm = b[C // 2 - 1:C // 2, :]
    bl = b[C - 1:C, :]
    e_q, e_k = jnp.exp(b - bm), jnp.exp(bm - b)
    e_b, e_l = jnp.exp(b), jnp.exp(bl - b)
    qd, kd = qf * e_q, k * e_k
    qe, ke = qf * e_b, k * e_l
    att = jnp.where(causal, _nt(qd, kd), 0.0)
    return dict(causal=causal, e_q=e_q, e_k=e_k, e_b=e_b, e_l=e_l, qd=qd, kd=kd, qe=qe, ke=ke, att=att, dec=jnp.exp(bl))


def _hgrn_chunk_fwd(pre, r, v, st):
    c = _hgrn_chunk_local(pre, r)
    c["o"] = _nn(c["att"], v) + _nt(c["qe"], st)
    return c


def _lockstep(gens):
    out = [None] * len(gens)
    live = list(enumerate(gens))
    while live:
        still = []
        for i, g in live:
            try:
                next(g)
                still.append((i, g))
            except StopIteration as done:
                out[i] = done.value
        live = still
    return out


HGRN_HEADS_PER_STEP = 4


def _hgrn_fwd(proj, lb_logits, o_gain, plan=None):
    T = proj.shape[0]
    BR = _hgrn_rows(T)
    cps = BR // A_CHUNK
    K, NH = A_HEAD_DIM, HGRN_HEADS_PER_STEP
    W = NH * K

    def col(off):
        return pl.BlockSpec((BR, W), lambda h, cb: (cb, off // W + h))

    def body(q_ref, f_ref, i_ref, g_ref, lbl_ref, og_ref, o_ref, s_ref, st):
        @pl.when(pl.program_id(1) == 0)
        def _():
            st[...] = jnp.zeros_like(st)

        lb_all = _lower_bound(lbl_ref[...])
        m_fwd = _chunk_sum_matrix(BR, False)
        pre = [_hgrn_block_pre(q_ref[:, n * K:(n + 1) * K], f_ref[:, n * K:(n + 1) * K], lb_all[:, n * K:(n + 1) * K], m_fwd)
               for n in range(NH)]
        def local(n, ci):
            r, hs = slice(ci * A_CHUNK, (ci + 1) * A_CHUNK), slice(n * K, (n + 1) * K)
            v = i_ref[r, hs]
            c = _hgrn_chunk_local(pre[n], r)
            yield
            return dict(o=_nn(c["att"], v), ds=_tn(v, c["ke"]), qe=c["qe"], dec=c["dec"])

        def chain(n, loc):
            hs = slice(n * K, (n + 1) * K)
            state = st[n]
            for ci, p in enumerate(loc):
                r = slice(ci * A_CHUNK, (ci + 1) * A_CHUNK)
                s_ref[n, ci] = state
                o = p["o"] + _nt(p["qe"], state)
                state = state * p["dec"] + p["ds"]
                yield
                on = o * lax.rsqrt(jnp.mean(o * o, axis=-1, keepdims=True) + EPS)
                g = g_ref[r, hs]
                o_ref[r, hs] = (on * og_ref[:, hs] * (g * _sig(g))).astype(BF16)
            st[n] = state

        loc = _lockstep([local(n, ci) for n in range(NH) for ci in range(cps)])
        _lockstep([chain(n, loc[n * cps:(n + 1) * cps]) for n in range(NH)])

    return _pcall(
        body, plan=plan, name="hgrn_fwd", grid=(A_HEADS // NH, T // BR),
        in_specs=[col(OFF_QA), col(OFF_FA), col(OFF_IA), col(OFF_GA),
                  pl.BlockSpec((2, W), lambda h, cb: (0, h)), pl.BlockSpec((1, W), lambda h, cb: (0, h))],
        out_specs=[pl.BlockSpec((BR, W), lambda h, cb: (cb, h)),
                   pl.BlockSpec((NH, cps, K, K), lambda h, cb: (h, cb, 0, 0))],
        out_shape=[jax.ShapeDtypeStruct((T, A_WIDTH), BF16),
                   jax.ShapeDtypeStruct((A_HEADS, T // A_CHUNK, K, K), F32)],
        scratch_shapes=[pltpu.VMEM((NH, K, K), F32)],
        compiler_params=_params(("parallel", "arbitrary")),
    )(proj, proj, proj, proj, lb_logits, o_gain)


def _hgrn_bwd(proj, lb_logits, o_gain, states, do, plan=None):
    T = proj.shape[0]
    BR = _hgrn_rows(T)
    cps = BR // A_CHUNK
    ncb = T // BR
    K, C, NH = A_HEAD_DIM, A_CHUNK, HGRN_HEADS_PER_STEP
    W = NH * K

    def col(off):
        return pl.BlockSpec((BR, W), lambda h, cb: (ncb - 1 - cb, off // W + h))

    def body(q_ref, f_ref, i_ref, g_ref, lbl_ref, og_ref, s_ref, do_ref,
             dq_ref, df_ref, di_ref, dg_ref, dlb_ref, dog_ref, dst):
        @pl.when(pl.program_id(1) == 0)
        def _():
            dst[...] = jnp.zeros_like(dst)
            dlb_ref[...] = jnp.zeros_like(dlb_ref)
            dog_ref[...] = jnp.zeros_like(dog_ref)

        lb_all = _lower_bound(lbl_ref[...])
        row = lax.broadcasted_iota(jnp.int32, (C, K), 0)
        m_fwd, m_bwd = _chunk_sum_matrix(BR, False), _chunk_sum_matrix(BR, True)
        pre = [_hgrn_block_pre(q_ref[:, n * K:(n + 1) * K], f_ref[:, n * K:(n + 1) * K], lb_all[:, n * K:(n + 1) * K], m_fwd)
               for n in range(NH)]
        def local(n, ci):
            r, hs = slice(ci * C, (ci + 1) * C), slice(n * K, (n + 1) * K)
            gain = og_ref[:, hs]
            st = s_ref[n, ci]
            v = i_ref[r, hs]
            q = q_ref[r, hs]
            c = _hgrn_chunk_fwd(pre[n], r, v, st)
            yield
            o = c["o"]
            rn = lax.rsqrt(jnp.mean(o * o, axis=-1, keepdims=True) + EPS)
            on = o * rn
            g = g_ref[r, hs]
            sgg = _sig(g)
            dy = do_ref[r, hs]
            d_ong = dy * (g * sgg)
            dg_ref[r, hs] = (dy * (on * gain) * (sgg * (1.0 + g * (1.0 - sgg)))).astype(BF16)
            d_on = d_ong * gain
            d_o = rn * (d_on - on * jnp.mean(d_on * on, axis=-1, keepdims=True))
            datt = jnp.where(c["causal"], _nt(d_o, v), 0.0)
            dqe = _nn(d_o, st)
            yield
            dqd = _nn(datt, c["kd"])
            dkd = _tn(datt, c["qd"])
            dv = _tn(c["att"], d_o)
            ds = _tn(d_o, c["qe"])
            yield
            t_q, t_k = dqd * c["qd"], dkd * c["kd"]
            sq = pre[n]["sq"][r]
            dq_ref[r, hs] = ((dqd * c["e_q"] + dqe * c["e_b"]) * (sq * (1.0 + q * (1.0 - sq)))).astype(BF16)
            return dict(v=v, st=st, ke=c["ke"], e_l=c["e_l"], dec=c["dec"], dv=dv, ds=ds, dk=dkd * c["e_k"],
                        db=t_q - t_k + dqe * c["qe"], dbm=jnp.sum(t_k - t_q, axis=0, keepdims=True),
                        d_og=jnp.sum(d_ong * on, axis=0, keepdims=True))

        def chain(n, loc):
            hs = slice(n * K, (n + 1) * K)
            dst_next = dst[n]
            db_of, dk_of = [None] * cps, [None] * cps
            for ci in reversed(range(cps)):
                p = loc[ci]
                di_ref[ci * C:(ci + 1) * C, hs] = (p["dv"] + _nt(p["ke"], dst_next)).astype(BF16)
                dke = _nn(p["v"], dst_next)
                yield
                t_l = dke * p["ke"]
                dbl = jnp.sum(t_l, axis=0, keepdims=True) + jnp.sum(dst_next * p["st"], axis=0, keepdims=True) * p["dec"]
                db_of[ci] = p["db"] - t_l + jnp.where(row == C // 2 - 1, p["dbm"], 0.0) + jnp.where(row == C - 1, dbl, 0.0)
                dk_of[ci] = p["dk"] + dke * p["e_l"]
                dst_next = dst_next * p["dec"] + p["ds"]
            dst[n] = dst_next
            return db_of, dk_of

        loc = _lockstep([local(n, ci) for n in range(NH) for ci in range(cps)])
        loc = [loc[n * cps:(n + 1) * cps] for n in range(NH)]
        chains = _lockstep([chain(n, loc[n]) for n in range(NH)])
        for n in range(NH):
            hs = slice(n * K, (n + 1) * K)
            db_of, dk_of = chains[n]
            d_og = loc[n][0]["d_og"]
            for p in loc[n][1:]:
                d_og = d_og + p["d_og"]
            dog_ref[0:1, hs] += d_og
            lb, sg = lb_all[:, hs], pre[n]["sg"]
            dlf = _chunk_sums(m_bwd, jnp.concatenate(db_of, axis=0))
            df = dlf / pre[n]["f"] - jnp.concatenate(dk_of, axis=0)
            df_ref[:, hs] = (df * (1.0 - lb) * sg * (1.0 - sg)).astype(BF16)
            dlb_ref[0:1, hs] += jnp.sum(df * (1.0 - sg), axis=0, keepdims=True)

    ocol = pl.BlockSpec((BR, W), lambda h, cb: (ncb - 1 - cb, h))
    vec = pl.BlockSpec((8, W), lambda h, cb: (0, h))
    return _pcall(
        body, plan=plan, name="hgrn_bwd", grid=(A_HEADS // NH, ncb),
        in_specs=[col(OFF_QA), col(OFF_FA), col(OFF_IA), col(OFF_GA),
                  pl.BlockSpec((2, W), lambda h, cb: (0, h)), pl.BlockSpec((1, W), lambda h, cb: (0, h)),
                  pl.BlockSpec((NH, cps, K, K), lambda h, cb: (h, ncb - 1 - cb, 0, 0)),
                  pl.BlockSpec((BR, W), lambda h, cb: (ncb - 1 - cb, h))],
        out_specs=[ocol, ocol, ocol, ocol, vec, vec],
        out_shape=[jax.ShapeDtypeStruct((T, A_WIDTH), BF16)] * 4 + [jax.ShapeDtypeStruct((8, A_WIDTH), F32)] * 2,
        scratch_shapes=[pltpu.VMEM((NH, K, K), F32)],
        compiler_params=_params(("parallel", "arbitrary")),
    )(proj, proj, proj, proj, lb_logits, o_gain, states, do)


def _head_norm(x):
    r = lax.rsqrt(jnp.mean(x * x, axis=-1, keepdims=True) + EPS)
    return x * r, r


def _head_norm_bwd(dy, xn, r, gain):
    dxn = dy * gain
    return r * (dxn - xn * jnp.mean(dxn * xn, axis=-1, keepdims=True)), jnp.sum(dy * xn, axis=0, keepdims=True)


def _swa_mask(has_prev):
    rows = B_GROUP * BLOCK
    r = lax.broadcasted_iota(jnp.int32, (rows, 2 * BLOCK), 0) % BLOCK
    c = lax.broadcasted_iota(jnp.int32, (rows, 2 * BLOCK), 1)
    rel = r + BLOCK - c
    return (rel >= 0) & (rel < BLOCK) & ((c >= BLOCK) | has_prev)


def _swa_head_fwd(j, q_ref, kp_ref, kc_ref, vp_ref, vc_ref, qg, kg, sk_ref, mask):
    hs = slice(j * B_HEAD_DIM, (j + 1) * B_HEAD_DIM)
    kcat = jnp.concatenate([kp_ref[:, hs], kc_ref[:, hs]], axis=0)
    vcat = jnp.concatenate([vp_ref[:, hs], vc_ref[:, hs]], axis=0)
    qs = jnp.concatenate([q_ref[:, pl.ds((j * B_GROUP + g) * B_HEAD_DIM, B_HEAD_DIM)] for g in range(B_GROUP)], axis=0)
    kn, kr = _head_norm(kcat)
    qn, qr = _head_norm(qs)
    kh, qh = kn * kg, qn * qg
    yield
    s = jnp.where(mask, _nt(qh, kh) * (B_HEAD_DIM ** -0.5), NEG_BIG)
    yield
    sink = jnp.concatenate(
        [jnp.broadcast_to(sk_ref[0:1, pl.ds(j * B_GROUP + g, 1)], (BLOCK, 1)) for g in range(B_GROUP)], axis=0)
    m = jnp.maximum(jnp.max(s, axis=-1, keepdims=True), sink)
    p = jnp.exp(s - m)
    e_sink = jnp.exp(sink - m)
    inv = 1.0 / (jnp.sum(p, axis=-1, keepdims=True) + e_sink)
    prob = p * inv
    return dict(vcat=vcat, kn=kn, kr=kr, qn=qn, qr=qr, kh=kh, qh=qh, prob=prob, p_sink=e_sink * inv)


def _swa_in_specs(nb, last):
    def qi(n):
        return jnp.minimum(n, last)

    q = pl.BlockSpec((BLOCK, B_WIDTH), lambda n: (qi(n), OFF_QB // B_WIDTH))
    kc = pl.BlockSpec((BLOCK, B_KV_WIDTH), lambda n: (qi(n), OFF_KB // B_KV_WIDTH))
    kp = pl.BlockSpec((BLOCK, B_KV_WIDTH), lambda n: (jnp.maximum(qi(n) - 1, 0), OFF_KB // B_KV_WIDTH))
    vc = pl.BlockSpec((BLOCK, B_KV_WIDTH), lambda n: (qi(n), OFF_VB // B_KV_WIDTH))
    vp = pl.BlockSpec((BLOCK, B_KV_WIDTH), lambda n: (jnp.maximum(qi(n) - 1, 0), OFF_VB // B_KV_WIDTH))
    small = [pl.BlockSpec((1, B_HEAD_DIM), lambda n: (0, 0)), pl.BlockSpec((1, B_HEAD_DIM), lambda n: (0, 0)),
             pl.BlockSpec((1, B_GROUP * B_KV_HEADS), lambda n: (0, 0))]
    return [q, kp, kc, vp, vc] + small


def _swa_fwd(proj, q_gain, k_gain, sinks, plan=None):
    T = proj.shape[0]
    nb = T // BLOCK

    def body(q_ref, kp_ref, kc_ref, vp_ref, vc_ref, qg_ref, kg_ref, sk_ref, o_ref):
        mask = _swa_mask(pl.program_id(0) > 0)

        def head(j):
            c = yield from _swa_head_fwd(j, q_ref, kp_ref, kc_ref, vp_ref, vc_ref, qg_ref[...], kg_ref[...], sk_ref, mask)
            yield
            o = _nn(c["prob"], c["vcat"])
            yield
            for g in range(B_GROUP):
                o_ref[:, pl.ds((j * B_GROUP + g) * B_HEAD_DIM, B_HEAD_DIM)] = o[g * BLOCK:(g + 1) * BLOCK].astype(BF16)

        _lockstep([head(j) for j in range(B_KV_HEADS)])

    return _pcall(
        body, plan=plan, name="swa_fwd", grid=(nb,),
        in_specs=_swa_in_specs(nb, nb - 1),
        out_specs=pl.BlockSpec((BLOCK, B_WIDTH), lambda n: (n, 0)),
        out_shape=jax.ShapeDtypeStruct((T, B_WIDTH), BF16),
        compiler_params=_params(("parallel",)),
    )(proj, proj, proj, proj, proj, q_gain, k_gain, sinks)


def _swa_bwd(proj, q_gain, k_gain, sinks, do, plan=None):
    T = proj.shape[0]
    nb = T // BLOCK
    scale = B_HEAD_DIM ** -0.5

    def body(q_ref, kp_ref, kc_ref, vp_ref, vc_ref, qg_ref, kg_ref, sk_ref, do_ref,
             dq_ref, dkv_ref, sm_ref, ck, cv):
        n = pl.program_id(0)

        @pl.when(n == 0)
        def _():
            ck[...] = jnp.zeros_like(ck)
            cv[...] = jnp.zeros_like(cv)
            sm_ref[...] = jnp.zeros_like(sm_ref)

        @pl.when(n < nb)
        def _():
            mask = _swa_mask(n > 0)
            qg, kg = qg_ref[...], kg_ref[...]
            lane = lax.broadcasted_iota(jnp.int32, (1, BLOCK), 1)
            def head(j):
                hs = slice(j * B_HEAD_DIM, (j + 1) * B_HEAD_DIM)
                vs = slice(B_KV_WIDTH + j * B_HEAD_DIM, B_KV_WIDTH + (j + 1) * B_HEAD_DIM)
                c = yield from _swa_head_fwd(j, q_ref, kp_ref, kc_ref, vp_ref, vc_ref, qg, kg, sk_ref, mask)
                d_out = jnp.concatenate(
                    [do_ref[:, pl.ds((j * B_GROUP + g) * B_HEAD_DIM, B_HEAD_DIM)] for g in range(B_GROUP)], axis=0)
                prob = c["prob"]
                yield
                out = _nn(prob, c["vcat"])
                d_prob = _nt(d_out, c["vcat"])
                dv = _tn(prob, d_out)
                yield
                delta = jnp.sum(d_out * out, axis=-1, keepdims=True)
                ds = prob * (d_prob - delta)
                d_sink = -c["p_sink"] * delta
                yield
                dqh = _nn(ds, c["kh"]) * scale
                dkh = _tn(ds, c["qh"]) * scale
                yield
                dq, dqg = _head_norm_bwd(dqh, c["qn"], c["qr"], qg)
                dk, dkg = _head_norm_bwd(dkh, c["kn"], c["kr"], kg)
                d_sinks = jnp.zeros((1, BLOCK), F32)
                for g in range(B_GROUP):
                    dq_ref[:, pl.ds((j * B_GROUP + g) * B_HEAD_DIM, B_HEAD_DIM)] = dq[g * BLOCK:(g + 1) * BLOCK].astype(BF16)
                    tot = jnp.sum(d_sink[g * BLOCK:(g + 1) * BLOCK], axis=0, keepdims=True)
                    d_sinks = d_sinks + jnp.where(lane == j * B_GROUP + g, tot, 0.0)
                dkv_ref[:, hs] = (ck[:, hs] + dk[0:BLOCK]).astype(BF16)
                dkv_ref[:, vs] = (cv[:, hs] + dv[0:BLOCK]).astype(BF16)
                ck[:, hs] = dk[BLOCK:2 * BLOCK]
                cv[:, hs] = dv[BLOCK:2 * BLOCK]
                return dqg, dkg, d_sinks

            small = _lockstep([head(j) for j in range(B_KV_HEADS)])
            sm_ref[0:1, 0:B_HEAD_DIM] += small[0][0] + small[1][0] + small[2][0] + small[3][0]
            sm_ref[1:2, 0:B_HEAD_DIM] += small[0][1] + small[1][1] + small[2][1] + small[3][1]
            sm_ref[2:3, :] += small[0][2] + small[1][2] + small[2][2] + small[3][2]

        @pl.when(n == nb)
        def _():
            dkv_ref[:, 0:B_KV_WIDTH] = ck[...].astype(BF16)
            dkv_ref[:, B_KV_WIDTH:2 * B_KV_WIDTH] = cv[...].astype(BF16)

    return _pcall(
        body, plan=plan, name="swa_bwd", grid=(nb + 1,),
        in_specs=_swa_in_specs(nb, nb - 1) + [pl.BlockSpec((BLOCK, B_WIDTH), lambda n: (jnp.minimum(n, nb - 1), 0))],
        out_specs=[pl.BlockSpec((BLOCK, B_WIDTH), lambda n: (jnp.minimum(n, nb - 1), 0)),
                   pl.BlockSpec((BLOCK, 2 * B_KV_WIDTH), lambda n: (jnp.maximum(n - 1, 0), 0)),
                   pl.BlockSpec((8, BLOCK), lambda n: (0, 0))],
        out_shape=[jax.ShapeDtypeStruct((T, B_WIDTH), BF16), jax.ShapeDtypeStruct((T, 2 * B_KV_WIDTH), BF16),
                   jax.ShapeDtypeStruct((8, BLOCK), F32)],
        scratch_shapes=[pltpu.VMEM((BLOCK, B_KV_WIDTH), F32), pltpu.VMEM((BLOCK, B_KV_WIDTH), F32)],
        compiler_params=_params(("arbitrary",)),
    )(proj, proj, proj, proj, proj, q_gain, k_gain, sinks, do)


W_IN, W_A, W_B, W_OUT, W_MI, W_MO = range(6)


def _local_step(x, target, mod8, norm1_gain, norm2_gain, lb_logits, o_gain, q_gain, k_gain, sinks, shards, c_arr, chip_arr):
    relu2 = lambda u: (u, jnp.square(jnp.maximum(u, 0.0)))
    pair, half = {}, {}

    def exchange(ws, grads):
        return _sibling_exchange_plan([_grad_view(g, w) for w, g in zip(ws, grads)])

    def pair_sums(ws, grads, others):
        for w, g, o in zip(ws, grads, others):
            pair[w] = _pair_sum(_grad_view(g, w), o, c_arr, f"pair_sum{w}")

    def sum_slots(ws, slots):
        for w, s in zip(ws, slots):
            half[w] = _sum_slots(pair[w], s, w, chip_arr, f"sum_slots{w}")

    part_in = _cast_into_full({W_IN: shards[W_IN]}, "cast_w_in")[W_IN]
    h, (part_in,) = _norm1_fwd(x, norm1_gain, mod8, plan=_gather_plan({W_IN: part_in}, part="near"))
    parts, (w_in,) = _cast_into_full({w: shards[w] for w in range(1, N_W)}, "cast_rest",
                                     plan=_gather_plan({W_IN: part_in}, pass_at=(0.97,), part="far"))
    proj, (w_mi,) = _mm(h, w_in, name="mm_proj", bn=512, plan=_gather_plan({W_MI: parts[W_MI]}, pass_at=(0.47, 0.72)))
    (o_a, states), (w_a, w_b) = _hgrn_fwd(
        proj, lb_logits, o_gain, plan=_gather_plan({w: parts[w] for w in (W_A, W_B)}, pass_at=(0.4, 0.65)))
    o_b, (w_out,) = _swa_fwd(proj, q_gain, k_gain, sinks, plan=_gather_plan({W_OUT: parts[W_OUT]}, pass_at=(0.3, 0.5)))
    ya = _mm(o_a, w_a, name="mm_branch_a")
    gate_cols = (OFF_GATE_A // MERGE_BC, OFF_GATE_B // MERGE_BC)
    yb, merged = _mm(o_b, w_b, name="mm_branch_b", bn=MERGE_BC, out_dtypes=(F32, BF16),
                     extras=(proj, proj, ya), extra_cols=gate_cols + (0,),
                     epi=lambda acc, ga, gb, ya_: (acc, _sig(ga) * ya_ + _sig(gb) * acc))
    mo = _mm(merged, w_out, name="mm_out")
    x1, h2 = _res_norm2_fwd(x, mo, norm2_gain, mod8)
    (u, act), (w_mo,) = _mm(h2, w_mi, name="mm_mlp_in", out_dtypes=(F32, BF16), epi=relu2,
                            plan=_gather_plan({W_MO: parts[W_MO]}, pass_at=(0.6, 0.9)))
    dy, dmlp, st_loss = _mm(act, w_mo, name="mm_mlp_out", bm=512, out_dtypes=(F32, BF16, F32), n_stats=1,
                            extras=(x1, target), row_extras=(mod8,), epi=_loss_head)
    st_loss = st_loss.reshape(-1, 8, D_MODEL).sum(axis=0)
    def half_blocks(w, own):
        def block(i):
            return 2 * i + (lax.axis_index("c") if own else 1 - lax.axis_index("c"))
        return (1 if W_SHAPES[w][2] else N_CHIPS), block

    def pair_of(w, lhs, rhs, other, name):
        hr, cols = _half_shape(w)
        p = _mm(lhs, rhs, name=name, ta=True, bn=512, a_blocks=half_blocks(w, True), out_dtypes=(BF16,),
                extras=(other,), epi=lambda acc, o: (acc + o,))
        return p.reshape(-1, hr, W_SHAPES[w][1])

    g_send = _mm(act, dmlp, name="mm_g_mlp_out_send", ta=True, bn=512, a_blocks=half_blocks(W_MO, False))
    du, (g_other,) = _mm(dmlp, w_mo, name="mm_d_act", tb=True, out_dtypes=(BF16,), extras=(u,),
                         epi=lambda acc, uu: (acc * (2.0 * jnp.maximum(uu, 0.0)),), plan=_sibling_share_plan([g_send]))
    pair[W_MO] = pair_of(W_MO, act, dmlp, g_other, "mm_g_mlp_out_own")
    g_send = _mm(h2, du, name="mm_g_mlp_in_send", ta=True, bn=512, a_blocks=half_blocks(W_MI, False))
    dh2, res = _mm(du, w_mi, name="mm_d_h2", tb=True,
                   plan=_join(_chip_exchange_plan({W_MO: pair[W_MO]}), _sibling_share_plan([g_send])))
    sum_slots([W_MO], res[:1])
    pair[W_MI] = pair_of(W_MI, h2, du, res[1], "mm_g_mlp_in_own")
    dx1, dmo, st_n2 = _norm2_bwd(dh2, x1, dy, mo, norm2_gain, mod8)
    def merge_bwd(dm, ga, gb, ya_, yb_):
        sa, sb = _sig(ga), _sig(gb)
        return dm * sa, dm * sb, dm * ya_ * sa * (1.0 - sa), dm * yb_ * sb * (1.0 - sb)

    dya, dyb, dga, dgb = _mm(dmo, w_out, name="mm_d_merged", tb=True, bn=MERGE_BC, out_dtypes=(BF16,) * 4,
                             extras=(proj, proj, ya, yb), extra_cols=gate_cols + (0, 0), epi=merge_bwd)
    g_out = _mm(merged, dmo, name="mm_g_out", ta=True, bn=512)
    do_a = _mm(dya, w_a, name="mm_d_oa", tb=True)
    g_a = _mm(o_a, dya, name="mm_g_branch_a", ta=True, bn=512)
    do_b = _mm(dyb, w_b, name="mm_d_ob", tb=True)
    g_b = _mm(o_b, dyb, name="mm_g_branch_b", ta=True, bn=512)
    mid = [W_A, W_B, W_OUT]
    (dqb, dkvb, st_swa), res = _swa_bwd(
        proj, q_gain, k_gain, sinks, do_b,
        plan=_join(_chip_exchange_plan({W_MI: pair[W_MI]}), exchange(mid, [g_a, g_b, g_out])))
    sum_slots([W_MI], res[:1])
    pair_sums(mid, [g_a, g_b, g_out], res[1:])
    (dqa, dfa, dia, dgga, d_lb, d_og), slots_mid = _hgrn_bwd(
        proj, lb_logits, o_gain, states, do_a, plan=_chip_exchange_plan({w: pair[w] for w in mid}))
    sum_slots(mid, slots_mid)
    dproj = jnp.concatenate([dqa, dfa, dia, dgga, dqb, dkvb, dga, dgb], axis=1)
    done = [W_A, W_B, W_OUT, W_MI, W_MO]
    g_send, res = _mm(h, dproj, name="mm_g_in_send", ta=True, bn=512, a_blocks=half_blocks(W_IN, False),
                      plan=_sibling_share_plan([half[w] for w in done]))
    theirs = dict(zip(done, res))
    g_own, (g_other,) = _mm(h, dproj, name="mm_g_in_own", ta=True, bn=512, a_blocks=half_blocks(W_IN, True),
                            plan=_sibling_share_plan([g_send]))
    pair[W_IN] = _add_bf16(g_own, g_other, "pair_sum0")[None]
    dh, slots_in = _mm(dproj, w_in, name="mm_d_h", tb=True, bk=2432, plan=_chip_exchange_plan({W_IN: pair[W_IN]}))
    sum_slots([W_IN], slots_in)
    grad_x, st_n1 = _norm1_bwd(dh, x, dx1, norm1_gain, mod8)
    (theirs[W_IN],) = _run_plan(_sibling_share_plan([half[W_IN]]), "sibling_share_w_in")
    stats = dict(loss=st_loss, n2=st_n2, n1=st_n1, d_lb=d_lb, d_og=d_og, swa=st_swa)
    return grad_x, [half[w] for w in range(N_W)], [theirs[w] for w in range(N_W)], stats


EW_VMEM_BYTES = 40 << 20


def _ew_rows(rows, cols, streams):
    br = 8
    while br * 2 * 4 <= rows and br * 2 * cols * 4 * 2 * streams <= EW_VMEM_BYTES and rows % (br * 2) == 0:
        br *= 2
    return br


CAST_STEPS = 16


def _cast_into_full(shards, name, plan=None):
    ws = sorted(shards)
    in_specs, out_specs, out_shape = [], [], []
    for w in ws:
        sr, sc = shards[w].shape
        R, C, by_col = W_SHAPES[w]
        br = sr // CAST_STEPS
        assert br * CAST_STEPS == sr and br % 16 == 0, (w, sr)

        def out_map(i, by_col=by_col):
            chip = 2 * lax.axis_index("x") + lax.axis_index("y")
            return (i, chip) if by_col else (chip * CAST_STEPS + i, 0)

        in_specs.append(pl.BlockSpec((br, sc), lambda i: (i, 0)))
        out_specs.append(pl.BlockSpec((br, sc), out_map))
        out_shape.append(jax.ShapeDtypeStruct((R, C), BF16))

    def body(*refs):
        for w_ref, o_ref in zip(refs[:len(ws)], refs[len(ws):]):
            o_ref[...] = w_ref[...].astype(BF16)

    res = _pcall(body, plan=plan, name=name, grid=(CAST_STEPS,), in_specs=in_specs, out_specs=out_specs,
                 out_shape=out_shape, compiler_params=_params(("arbitrary",)))(*[shards[w] for w in ws])
    if plan is None:
        return dict(zip(ws, res))
    return dict(zip(ws, res[0])), res[1]


def _adamw_math(w, g, m, v):
    m = ADAM_B1 * m + (1.0 - ADAM_B1) * g
    v = ADAM_B2 * v + (1.0 - ADAM_B2) * (g * g)
    m_hat = m / (1.0 - ADAM_B1 ** ADAM_STEP)
    v_hat = v / (1.0 - ADAM_B2 ** ADAM_STEP)
    delta = -ADAM_LR * (m_hat / (jnp.sqrt(v_hat) + ADAM_EPS) + ADAM_WD * w)
    return delta, m, v


def _adamw(w, g, m, v, name):
    R, C = w.shape
    br = _ew_rows(R, C, 7)
    spec = pl.BlockSpec((br, C), lambda i: (i, 0))

    def body(w_ref, g_ref, m_ref, v_ref, d_ref, nm_ref, nv_ref):
        d_ref[...], nm_ref[...], nv_ref[...] = _adamw_math(w_ref[...], g_ref[...], m_ref[...], v_ref[...])

    sh = jax.ShapeDtypeStruct((R, C), F32)
    return _pcall(body, name=name, grid=(R // br,), in_specs=[spec] * 4, out_specs=[spec] * 3, out_shape=[sh] * 3,
                  compiler_params=_params(("parallel",)))(w, g, m, v)


def _add_bf16(a, b, name):
    R, C = a.shape
    br = _ew_rows(R, C, 2.5)
    spec = pl.BlockSpec((br, C), lambda i: (i, 0))

    def body(a_ref, b_ref, o_ref):
        o_ref[...] = (a_ref[...] + b_ref[...]).astype(BF16)

    return _pcall(body, name=name, grid=(R // br,), in_specs=[spec, spec], out_specs=spec,
                  out_shape=jax.ShapeDtypeStruct((R, C), BF16), compiler_params=_params(("parallel",)))(a, b)


def _adamw_halves(w, own, other, m, v, c_arr, name):
    R, C = w.shape
    hr = R // 2
    br = _ew_rows(hr, C, 9)
    nb = hr // br
    full = pl.BlockSpec((br, C), lambda h, i, c_ref: (h * nb + i, 0))

    def own_map(h, i, c_ref):
        return jnp.where(h == c_ref[0], i, jnp.where(c_ref[0] == 0, nb - 1, 0)), 0

    def other_map(h, i, c_ref):
        return jnp.where(h != c_ref[0], i, jnp.where(c_ref[0] == 0, 0, nb - 1)), 0

    def body(c_ref, w_ref, own_ref, oth_ref, m_ref, v_ref, g_ref, d_ref, nm_ref, nv_ref):
        g = jnp.where(pl.program_id(0) == c_ref[0], own_ref[...], oth_ref[...])
        g_ref[...] = g
        d_ref[...], nm_ref[...], nv_ref[...] = _adamw_math(w_ref[...], g, m_ref[...], v_ref[...])

    sh = jax.ShapeDtypeStruct((R, C), F32)
    return _pcall(
        body, name=name,
        grid_spec=pltpu.PrefetchScalarGridSpec(
            num_scalar_prefetch=1, grid=(2, nb),
            in_specs=[full, pl.BlockSpec((br, C), own_map), pl.BlockSpec((br, C), other_map), full, full],
            out_specs=[full] * 4),
        out_shape=[sh] * 4, compiler_params=_params(("arbitrary", "arbitrary")))(c_arr, w, own, other, m, v)


def _ada_grad_adamw(c_t, dmod, w, m, v):
    R, C = w.shape
    br = _ew_rows(R, C, 8)
    spec = pl.BlockSpec((br, C), lambda i: (i, 0))

    def body(c_ref, dm_ref, w_ref, m_ref, v_ref, g_ref, d_ref, nm_ref, nv_ref):
        cv = c_ref[...]
        sc = cv * _sig(cv)
        g = sc[:, 0:1] * dm_ref[0:1, :]
        for b in range(1, N_DEV):
            g = g + sc[:, b:b + 1] * dm_ref[b:b + 1, :]
        g_ref[...] = g
        d_ref[...], nm_ref[...], nv_ref[...] = _adamw_math(w_ref[...], g, m_ref[...], v_ref[...])

    sh = jax.ShapeDtypeStruct((R, C), F32)
    return _pcall(
        body, name="ada_grad_adamw", grid=(R // br,),
        in_specs=[pl.BlockSpec((br, N_DEV), lambda i: (i, 0)), pl.BlockSpec((N_DEV, C), lambda i: (0, 0)), spec, spec, spec],
        out_specs=[spec] * 4, out_shape=[sh] * 4, compiler_params=_params(("parallel",)))(c_t, dmod, w, m, v)


SMALL_ROWS = 16


def _small_sum(small_all, lb_logits):
    def body(s_ref, lbl_ref, o_ref):
        acc = s_ref[0:SMALL_ROWS, :]
        for d in range(1, N_DEV):
            acc = acc + s_ref[d * SMALL_ROWS:(d + 1) * SMALL_ROWS, :]
        o_ref[...] = acc
        z = lbl_ref[...]
        e = jnp.exp(z - jnp.max(z, axis=0, keepdims=True))
        p0 = e[0:1, :] / (e[0:1, :] + e[1:2, :])
        dz = acc[8:9, 0:A_WIDTH] * p0 * (1.0 - p0)
        o_ref[8:9, 0:A_WIDTH] = dz
        o_ref[10:11, 0:A_WIDTH] = -dz

    return _pcall(body, name="small_sum", out_shape=jax.ShapeDtypeStruct((SMALL_ROWS, D_MODEL), F32),
                  in_specs=[pl.BlockSpec(memory_space=pltpu.VMEM)] * 2, out_specs=pl.BlockSpec(memory_space=pltpu.VMEM),
                  compiler_params=_params())(small_all, lb_logits)


RELATIONS = ((1, 0), (0, 1), (1, 1))
ANY = pl.BlockSpec(memory_space=pl.ANY)


def _place():
    x, y, c = lax.axis_index("x"), lax.axis_index("y"), lax.axis_index("c")
    return x, y, c


def _allgather_small(x_shard, name):
    m_per, n = x_shard.shape

    def body(x_ref, out_ref, send_sems, recv_sems, local_sem):
        x, y, c = _place()
        me, sibling = (x, y, c), (x, y, 1 - c)
        chips = [(1 - x, y), (x, 1 - y), (1 - x, 1 - y)]

        def rows(px, py, pc):
            return out_ref.at[pl.ds((4 * px + 2 * py + pc) * m_per, m_per), :]

        def copy(k, block, to, src=None):
            return pltpu.make_async_remote_copy(
                src_ref=rows(*block) if src is None else src, dst_ref=rows(*block),
                send_sem=send_sems.at[k], recv_sem=recv_sems.at[k], device_id=to, device_id_type=MESH)

        mine = pltpu.make_async_copy(x_ref, rows(*me), local_sem)
        mine.start()
        first = [copy(0, me, sibling, src=x_ref)]
        first += [copy(1 + j, me, (*chip, c), src=x_ref) for j, chip in enumerate(chips)]
        for cp in first:
            cp.start()
        passed = [copy(4 + j, (*chip, c), sibling) for j, chip in enumerate(chips)]
        for j, chip in enumerate(chips):
            copy(1 + j, (*chip, c), me).wait_recv()
            passed[j].start()
        copy(0, sibling, me).wait_recv()
        for j, chip in enumerate(chips):
            copy(4 + j, (*chip, 1 - c), me).wait_recv()
        for cp in first + passed:
            cp.wait_send()
        mine.wait()

    return _pcall(
        body, name=name, out_shape=jax.ShapeDtypeStruct((N_DEV * m_per, n), x_shard.dtype),
        in_specs=[pl.BlockSpec(memory_space=pltpu.VMEM)], out_specs=pl.BlockSpec(memory_space=pltpu.VMEM),
        scratch_shapes=[pltpu.SemaphoreType.DMA((7,)), pltpu.SemaphoreType.DMA((7,)), pltpu.SemaphoreType.DMA],
        compiler_params=_params(),
    )(x_shard)


W_SHAPES = ((D_MODEL, IN_WIDTH, True), (A_WIDTH, D_MODEL, True), (B_WIDTH, D_MODEL, True),
            (D_MODEL, D_MODEL, False), (D_MODEL, MLP_HIDDEN, True), (MLP_HIDDEN, D_MODEL, False))
N_W = len(W_SHAPES)


def _shard_shape(w):
    R, C, by_col = W_SHAPES[w]
    return (R, C // N_CHIPS) if by_col else (R // N_CHIPS, C)


def _half_shape(w):
    sr, sc = _shard_shape(w)
    return sr // 2, sc


def _region(full_ref, w, chip, half, quarter=None):
    sr, sc = _shard_shape(w)
    by_col = W_SHAPES[w][2]
    r0, c0 = (0, chip * sc) if by_col else (chip * sr, 0)
    r0, rows = r0 + half * (sr // 2), sr // 2
    if quarter is not None:
        r0, rows = r0 + quarter * (rows // 2), rows // 2
    return full_ref.at[pl.ds(r0, rows), pl.ds(c0, sc)]


def _on_device(fn):
    x, y, c = _place()
    me = 4 * x + 2 * y + c
    for d in range(N_DEV):
        @pl.when(me == d)
        def _(d=d):
            fn(x, y, c, d)


GATHER_COPIES = (
    (0, 0, None, "x"), (0, 0, None, "y"),
    (1, 2, 0, "y"), (1, 1, 1, "x"),
    (1, 2, None, "s"), (1, 1, None, "s"),
    (2, 3, 0, "s"), (2, 3, 1, "s"),
)
PEER_FLIP = {"x": 2, "y": 1, "s": 0}


GATHER_STAGES = {
    None: (((), (0, 1), ()), ((0, 1), (2, 3, 4, 5), ()), ((2, 3), (6, 7), ()), ((4, 5, 6, 7), (), tuple(range(8)))),
    "near": (((), (0, 1), ()), ((0, 1), (), (0, 1))),
    "far": (((), (2, 3, 4, 5), ()), ((2, 3), (6, 7), ()), ((4, 5, 6, 7), (), (2, 3, 4, 5, 6, 7))),
}


def _gather_plan(partials, pass_at=(0.5, 0.75), part=None):
    ws = sorted(partials)
    n_t = len(GATHER_COPIES)
    jobs = [(i, w) for i, w in enumerate(ws)]

    def copy(pi, po, ps, x, y, c, d, i, w, t, landing):
        chip, dc = d >> 1, d & 1
        stage, flip, quarter, to = GATHER_COPIES[t]
        if landing:
            peer_chip = chip ^ PEER_FLIP[to]
            part = _region(po[i], w, peer_chip ^ flip, (1 - dc) if to == "s" else dc, quarter)
            src = part
        else:
            part = _region(po[i], w, chip ^ flip, dc, quarter)
            here = flip != 0 and (part_of is None or stage == 2)
            src = part if here else _region(pi[i], w, chip ^ flip, dc, quarter)
        target = {"x": (x ^ 1, y, c), "y": (x, y ^ 1, c), "s": (x, y, 1 - c)}[to]
        return pltpu.make_async_remote_copy(
            src_ref=src, dst_ref=part, send_sem=ps[0].at[i * n_t + t], recv_sem=ps[1].at[i * n_t + t],
            device_id=target, device_id_type=MESH)

    part_of = part

    def stage(landed, started, sent):
        def run(pi, po, ps):
            def on(x, y, c, d):
                for i, w in jobs:
                    for t in landed:
                        copy(pi, po, ps, x, y, c, d, i, w, t, True).wait_recv()
                for i, w in jobs:
                    for t in started:
                        copy(pi, po, ps, x, y, c, d, i, w, t, False).start()
                for i, w in jobs:
                    for t in sent:
                        copy(pi, po, ps, x, y, c, d, i, w, t, False).wait_send()
            _on_device(on)
        return run

    stages = [stage(*st) for st in GATHER_STAGES[part]]
    mid_at = tuple(pass_at) if part is None else tuple(pass_at)[:len(stages) - 2]
    return _Plan([partials[w] for w in ws], [jax.ShapeDtypeStruct(W_SHAPES[w][:2], BF16) for w in ws],
                 [pltpu.SemaphoreType.DMA((n_t * len(ws),)) for _ in range(2)], stages,
                 {i: i for i in range(len(ws))}, mid_at=mid_at)


def _grad_view(g, w):
    R, C, by_col = W_SHAPES[w]
    return g.reshape(1, 2, R // 2, C) if by_col else g.reshape(N_CHIPS, 2, R // N_CHIPS // 2, C)


def _start_wait_plan(ins, outs, n_copies, copies):
    def start(pi, po, ps):
        for cp in copies(pi, po, ps):
            cp.start()

    def finish(pi, po, ps):
        for cp in copies(pi, po, ps):
            cp.wait()

    return _Plan(ins, outs, [pltpu.SemaphoreType.DMA((n_copies,)), pltpu.SemaphoreType.DMA((n_copies,))], [start, finish])


def _sibling_exchange_plan(g4s):
    pieces = [(i, p) for i, g in enumerate(g4s) for p in range(g.shape[0])]

    def copies(pi, po, ps):
        x, y, c = _place()
        return [pltpu.make_async_remote_copy(
            src_ref=pi[i].at[p, 1 - c], dst_ref=po[i].at[p], send_sem=ps[0].at[n], recv_sem=ps[1].at[n],
            device_id=(x, y, 1 - c), device_id_type=MESH) for n, (i, p) in enumerate(pieces)]

    return _start_wait_plan(list(g4s), [jax.ShapeDtypeStruct((g.shape[0],) + g.shape[2:], F32) for g in g4s],
                            len(pieces), copies)


def _pair_sum(g4, other, c_arr, name):
    P, _, hr, C = g4.shape
    br = _ew_rows(hr, C, 2.5)

    def body(c_ref, g_ref, o_ref, p_ref):
        p_ref[...] = (g_ref[...] + o_ref[...]).astype(BF16)

    return _pcall(
        body, name=name,
        grid_spec=pltpu.PrefetchScalarGridSpec(
            num_scalar_prefetch=1, grid=(P, hr // br),
            in_specs=[pl.BlockSpec((None, None, br, C), lambda p, i, c_ref: (p, c_ref[0], i, 0)),
                      pl.BlockSpec((None, br, C), lambda p, i, c_ref: (p, i, 0))],
            out_specs=pl.BlockSpec((None, br, C), lambda p, i, c_ref: (p, i, 0))),
        out_shape=jax.ShapeDtypeStruct((P, hr, C), BF16),
        compiler_params=_params(("parallel", "parallel")),
    )(c_arr, g4, other)


def _pair_part(p_ref, w, chip):
    sr, sc = _shard_shape(w)
    return p_ref.at[0, :, pl.ds(chip * sc, sc)] if W_SHAPES[w][2] else p_ref.at[chip]


def _chip_exchange_plan(pairs, rels=(0, 1, 2), into=None):
    ws = sorted(pairs)
    n = len(ws)

    def stage(wait):
        def run(pi, po, ps):
            def on(x, y, c, d):
                for i, w in enumerate(ws):
                    for k, (rx, ry) in enumerate(RELATIONS):
                        if k not in rels:
                            continue
                        cp = pltpu.make_async_remote_copy(
                            src_ref=_pair_part(pi[i], w, (d >> 1) ^ (2 * rx + ry)), dst_ref=po[i].at[k],
                            send_sem=ps[0].at[i * 3 + k], recv_sem=ps[1].at[i * 3 + k],
                            device_id=(x ^ rx, y ^ ry, c), device_id_type=MESH)
                        if wait:
                            cp.wait()
                        else:
                            cp.start()
            _on_device(on)
        return run

    ins = [pairs[w] for w in ws] + ([into[w] for w in ws] if into else [])
    return _Plan(ins, [jax.ShapeDtypeStruct((3,) + _half_shape(w), BF16) for w in ws],
                 [pltpu.SemaphoreType.DMA((3 * n,)), pltpu.SemaphoreType.DMA((3 * n,))],
                 [stage(False), stage(True)], {n + i: i for i in range(n)} if into else None)


def _sum_slots(pair, slots, w, chip_arr, name):
    _, hr, C = slots.shape
    br = _ew_rows(hr, C, 3)
    own_map = (lambda i, chip: (0, i, chip[0])) if W_SHAPES[w][2] else (lambda i, chip: (chip[0], i, 0))

    def body(chip_ref, p_ref, s_ref, o_ref):
        acc = p_ref[...].astype(F32)
        for k in range(3):
            acc = acc + s_ref[k].astype(F32)
        o_ref[...] = acc

    return _pcall(
        body, name=name,
        grid_spec=pltpu.PrefetchScalarGridSpec(
            num_scalar_prefetch=1, grid=(hr // br,),
            in_specs=[pl.BlockSpec((None, br, C), own_map), pl.BlockSpec((3, br, C), lambda i, chip: (0, i, 0))],
            out_specs=pl.BlockSpec((br, C), lambda i, chip: (i, 0))),
        out_shape=jax.ShapeDtypeStruct((hr, C), F32), compiler_params=_params(("parallel",)),
    )(chip_arr, pair, slots)


def _sibling_share_plan(halves):
    def copies(pi, po, ps):
        x, y, c = _place()
        return [pltpu.make_async_remote_copy(
            src_ref=pi[i], dst_ref=po[i], send_sem=ps[0].at[i], recv_sem=ps[1].at[i],
            device_id=(x, y, 1 - c), device_id_type=MESH) for i in range(len(halves))]

    return _start_wait_plan(list(halves), [jax.ShapeDtypeStruct(h.shape, F32) for h in halves], len(halves), copies)


def _pad_lanes(v, width=D_MODEL):
    return jnp.pad(v, ((0, 0), (0, width - v.shape[1])))


def _pack_small(b_ada, norm1, norm2, lb, o_gain, q_gain, k_gain, sinks):
    rows = [b_ada.reshape(N_MOD, D_MODEL), norm1, norm2, jnp.concatenate([lb[0:1], o_gain], axis=1),
            _pad_lanes(jnp.concatenate([q_gain, k_gain, sinks], axis=1)), _pad_lanes(lb[1:2]),
            jnp.zeros((SMALL_ROWS - 11, D_MODEL), F32)]
    return jnp.concatenate(rows, axis=0)


def _unpack_small(p):
    return (p[0:6].reshape(1, N_MOD * D_MODEL), p[6:7], p[7:8],
            jnp.concatenate([p[8:9, 0:A_WIDTH], p[10:11, 0:A_WIDTH]], axis=0), p[8:9, A_WIDTH:],
            p[9:10, 0:64], p[9:10, 64:128], p[9:10, 128:144])


def kernel(x, c, w_ada, b_ada, norm1_gain, w_in, lb_logits, hgrn_o_gain, q_norm_gain, k_norm_gain, sinks, w_branch_a, w_branch_b, w_out, norm2_gain, w_mlp_in, w_mlp_out, loss_target, m_w_ada, m_b_ada, m_norm1_gain, m_w_in, m_lb_logits, m_hgrn_o_gain, m_q_norm_gain, m_k_norm_gain, m_sinks, m_w_branch_a, m_w_branch_b, m_w_out, m_norm2_gain, m_w_mlp_in, m_w_mlp_out, v_w_ada, v_b_ada, v_norm1_gain, v_w_in, v_lb_logits, v_hgrn_o_gain, v_q_norm_gain, v_k_norm_gain, v_sinks, v_w_branch_a, v_w_branch_b, v_w_out, v_norm2_gain, v_w_mlp_in, v_w_mlp_out):
    xi, yi, ci = _place()
    chip = 2 * xi + yi
    me = 4 * xi + 2 * yi + ci
    ada_cols = w_ada.shape[2]

    c_all = _allgather_small(jnp.broadcast_to(c, (8, D_MODEL)), "gather_c").reshape(N_DEV, 8, D_MODEL)[:, 0]
    b_cols = lax.dynamic_slice(b_ada, (0, chip * ada_cols), (1, ada_cols))
    mod_part = _ada_fwd(c_all, w_ada[0], b_cols)
    mod_all = _allgather_small(mod_part, "gather_mod").reshape(N_CHIPS, 2, N_DEV, ada_cols)[:, 0]
    mod_mine = lax.dynamic_index_in_dim(mod_all, me, axis=1, keepdims=False).reshape(N_MOD, D_MODEL)
    mod8 = jnp.concatenate([mod_mine, jnp.zeros((2, D_MODEL), F32)], axis=0)

    shards = (w_in[0], w_branch_a[0], w_branch_b[0], w_out[0], w_mlp_in[0], w_mlp_out[0])
    chip_arr = chip.astype(jnp.int32).reshape(1)
    c_arr = ci.astype(jnp.int32).reshape(1)

    grad_x, halves, theirs, st = _local_step(x[0], loss_target[0], mod8, norm1_gain, norm2_gain, lb_logits, hgrn_o_gain,
                                             q_norm_gain, k_norm_gain, sinks, shards, c_arr, chip_arr)
    loss = lax.psum(0.5 * jnp.sum(st["loss"][0]) / D_MODEL, ("x", "y", "c"))
    moments = ((m_w_in, v_w_in), (m_w_branch_a, v_w_branch_a), (m_w_branch_b, v_w_branch_b), (m_w_out, v_w_out),
               (m_w_mlp_in, v_w_mlp_in), (m_w_mlp_out, v_w_mlp_out))
    big = [_adamw_halves(shards[w], halves[w], theirs[w], moments[w][0][0], moments[w][1][0], c_arr, f"adamw{w}")
           for w in range(N_W)]

    swa = st["swa"]
    small = jnp.concatenate([
        st["n1"][1:2], st["n1"][0:1], st["n2"][3:4], st["n2"][1:2], st["n2"][0:1], st["loss"][1:2],
        st["n1"][2:3], st["n2"][2:3], jnp.concatenate([st["d_lb"][0:1], st["d_og"][0:1]], axis=1),
        _pad_lanes(jnp.concatenate([swa[0:1, 0:64], swa[1:2, 0:64], swa[2:3, 0:16]], axis=1)),
        jnp.zeros((SMALL_ROWS - 10, D_MODEL), F32)], axis=0)
    small_all = _allgather_small(small, "gather_small")
    g_small = _small_sum(small_all, lb_logits)
    small_w = (b_ada, norm1_gain, norm2_gain, lb_logits, hgrn_o_gain, q_norm_gain, k_norm_gain, sinks)
    small_m = (m_b_ada, m_norm1_gain, m_norm2_gain, m_lb_logits, m_hgrn_o_gain, m_q_norm_gain, m_k_norm_gain, m_sinks)
    small_v = (v_b_ada, v_norm1_gain, v_norm2_gain, v_lb_logits, v_hgrn_o_gain, v_q_norm_gain, v_k_norm_gain, v_sinks)
    sm = [_unpack_small(t) for t in
          (g_small,) + tuple(_adamw(_pack_small(*small_w), g_small, _pack_small(*small_m), _pack_small(*small_v),
                                    "adamw_small"))]
    g_b, g_n1, g_n2, g_lb, g_og, g_qg, g_kg, g_sk = ([t[i] for t in sm] for i in range(8))

    dmod_all = small_all.reshape(N_DEV, SMALL_ROWS, D_MODEL)[:, 0:N_MOD].reshape(N_DEV, N_MOD * D_MODEL)
    dmod_cols = lax.dynamic_slice(dmod_all, (0, chip * ada_cols), (N_DEV, ada_cols))
    ada = _ada_grad_adamw(c_all.T, dmod_cols, w_ada[0], m_w_ada[0], v_w_ada[0])

    def ordered(k):
        lead = lambda a: a[None]
        return (lead(ada[k]), g_b[k], g_n1[k], lead(big[0][k]), g_lb[k], g_og[k], g_qg[k], g_kg[k], g_sk[k],
                lead(big[1][k]), lead(big[2][k]), lead(big[3][k]), g_n2[k], lead(big[4][k]), lead(big[5][k]))

    return (loss, grad_x[None]) + ordered(0) + ordered(1) + ordered(2) + ordered(3)
```

```python
import jax
import jax.numpy as jnp
from jax import lax
from jax.experimental import pallas as pl
from jax.experimental.pallas import tpu as pltpu

F32 = jnp.float32
BF16 = jnp.bfloat16
HIGHEST = lax.Precision.HIGHEST
MESH = pl.DeviceIdType.MESH

D_MODEL = 2048
A_WIDTH = 1024
A_HEADS = 8
A_HEAD_DIM = 128
A_CHUNK = 64
B_WIDTH = 1024
B_HEAD_DIM = 64
B_GROUP = 4
B_KV_HEADS = 4
B_KV_WIDTH = 256
BLOCK = 128
MLP_HIDDEN = 8192
IN_WIDTH = 9728
N_MOD = 6
EPS = 1e-6
N_CHIPS = 4
N_DEV = 8

OFF_QA, OFF_FA, OFF_IA, OFF_GA = 0, 1024, 2048, 3072
OFF_QB, OFF_KB, OFF_VB = 4096, 5120, 5376
OFF_GATE_A, OFF_GATE_B = 5632, 7680

ADAM_LR = 0.001
ADAM_B1 = 0.9
ADAM_B2 = 0.999
ADAM_EPS = 1e-08
ADAM_WD = 0.01
ADAM_STEP = 10

VMEM_LIMIT_V7X = 48 * 1024 * 1024
NEG_BIG = -1e30


def _params(sem=None, vmem=VMEM_LIMIT_V7X):
    return pltpu.CompilerParams(dimension_semantics=sem, vmem_limit_bytes=vmem)


class _Plan:
    def __init__(self, ins, outs, sems, stages, aliases=None, mid_at=()):
        self.ins, self.outs, self.sems, self.stages, self.aliases = ins, outs, sems, stages, aliases or {}
        self.mid_at = tuple(mid_at)
        assert len(self.mid_at) == len(stages) - 2


def _join(a, b):
    assert len(a.stages) == 2 and len(b.stages) == 2
    ni, no, ns = len(a.ins), len(a.outs), len(a.sems)

    def stage(k):
        def run(pi, po, ps):
            a.stages[k](pi[:ni], po[:no], ps[:ns])
            b.stages[k](pi[ni:], po[no:], ps[ns:])
        return run

    aliases = dict(a.aliases)
    aliases.update({ni + i: no + o for i, o in b.aliases.items()})
    return _Plan(a.ins + b.ins, a.outs + b.outs, a.sems + b.sems, [stage(0), stage(1)], aliases)


def _pcall(body, plan=None, **kw):
    if plan is None:
        return pl.pallas_call(body, **kw)
    grid = kw["grid"]
    single = not isinstance(kw["out_specs"], (list, tuple))
    in_specs = list(kw["in_specs"])
    out_specs = [kw["out_specs"]] if single else list(kw["out_specs"])
    out_shape = [kw["out_shape"]] if single else list(kw["out_shape"])
    scratch = list(kw.get("scratch_shapes", ()))
    n_in, n_out, n_scr = len(in_specs), len(out_specs), len(scratch)
    n_pi, n_po = len(plan.ins), len(plan.outs)
    total = 1
    for g in grid:
        total *= g
    n_st = len(plan.stages)

    def wrapped(*refs):
        o0 = n_in + n_pi
        s0 = o0 + n_out + n_po
        pi, po, ps = refs[n_in:o0], refs[o0 + n_out:s0], refs[s0 + n_scr:]
        lin = 0
        for d, g in enumerate(grid):
            lin = lin * g + pl.program_id(d)
        for si, frac in enumerate((0.0,) + plan.mid_at):
            @pl.when(lin == int(frac * (total - 1)))
            def _(si=si):
                plan.stages[si](pi, po, ps)
        body(*refs[:n_in], *refs[o0:o0 + n_out], *refs[s0:s0 + n_scr])

        @pl.when(lin == total - 1)
        def _():
            plan.stages[-1](pi, po, ps)

    any_spec = pl.BlockSpec(memory_space=pl.ANY)
    call = pl.pallas_call(
        wrapped, name=kw["name"], grid=grid, in_specs=in_specs + [any_spec] * n_pi,
        out_specs=out_specs + [any_spec] * n_po, out_shape=out_shape + list(plan.outs),
        scratch_shapes=scratch + list(plan.sems),
        input_output_aliases={n_in + i: n_out + o for i, o in plan.aliases.items()},
        compiler_params=_params(("arbitrary",) * len(grid)))

    def run(*args):
        res = call(*args, *plan.ins)
        outs = list(res[:n_out])
        return (outs[0] if single else outs), list(res[n_out:])

    return run


def _run_plan(plan, name):
    return _pcall(lambda: None, plan=plan, name=name, grid=(1,), in_specs=[], out_specs=[], out_shape=[])()[1]


def _sig(x):
    return 1.0 / (1.0 + jnp.exp(-x))


def _nn(a, b):
    return lax.dot_general(a.astype(BF16), b.astype(BF16), (((1,), (0,)), ((), ())), preferred_element_type=F32)


def _nt(a, b):
    return lax.dot_general(a.astype(BF16), b.astype(BF16), (((1,), (1,)), ((), ())), preferred_element_type=F32)


def _tn(a, b):
    return lax.dot_general(a.astype(BF16), b.astype(BF16), (((0,), (0,)), ((), ())), preferred_element_type=F32)


def _mm(a, b, *, name, ta=False, tb=False, bm=1024, bn=1024, bk=2048, out_dtypes=(F32,), epi=None, extras=(),
        extra_cols=None, plan=None, a_blocks=None, row_extras=(), n_stats=0):
    if ta:
        K, M = a.shape
        bk = K
        if a_blocks is not None:
            M = a_blocks[0] * bm
    else:
        M, K = a.shape
    if tb:
        N, K2 = b.shape
    else:
        K2, N = b.shape
    bm, bn, bk = min(bm, M), min(bn, N), min(bk, K)
    assert K == K2 and M % bm == 0 and N % bn == 0 and K % bk == 0, (name, a.shape, b.shape)
    nk = K // bk
    a_col = a_blocks[1] if a_blocks is not None else (lambda i: i)
    a_spec = pl.BlockSpec((bk, bm), lambda i, j, k: (k, a_col(i))) if ta else pl.BlockSpec((bm, bk), lambda i, j, k: (i, k))
    b_spec = pl.BlockSpec((bn, bk), lambda i, j, k: (j, k)) if tb else pl.BlockSpec((bk, bn), lambda i, j, k: (k, j))
    t_spec = pl.BlockSpec((bm, bn), lambda i, j, k: (i, j))
    extra_cols = extra_cols or (0,) * len(extras)
    e_specs = [pl.BlockSpec((bm, bn), lambda i, j, k, off=off: (i, off + j)) for off in extra_cols]
    e_specs += [pl.BlockSpec((8, bn), lambda i, j, k: (0, j)) for _ in row_extras]
    dims = (((1,), (1 if tb else 0,)), ((), ()))
    n_e, n_o = len(extras) + len(row_extras), len(out_dtypes)
    stat_spec = pl.BlockSpec((8, bn), lambda i, j, k: (i, j))

    def body(*refs):
        a_ref, b_ref = refs[0], refs[1]
        e_refs = refs[2:2 + n_e]
        o_refs = refs[2 + n_e:2 + n_e + n_o]

        def finish(acc):
            outs = (acc,) if epi is None else epi(acc, *[e[...] for e in e_refs])
            for o_ref, o in zip(o_refs, outs):
                o_ref[...] = o.astype(o_ref.dtype)

        if ta:
            at_ref = refs[-1]

            @pl.when(pl.program_id(1) == 0)
            def _():
                at_ref[...] = a_ref[...].T

            lhs = at_ref[...]
        else:
            lhs = a_ref[...].astype(BF16)
        part = lax.dot_general(lhs, b_ref[...].astype(BF16), dims, preferred_element_type=F32)
        if nk == 1:
            finish(part)
        else:
            acc_ref = refs[-1]
            k = pl.program_id(2)

            @pl.when(k == 0)
            def _():
                acc_ref[...] = part

            @pl.when(k > 0)
            def _():
                acc_ref[...] += part

            @pl.when(k == nk - 1)
            def _():
                finish(acc_ref[...])

    if ta:
        assert a.dtype == BF16 and nk == 1
        scratch = [pltpu.VMEM((bm, bk), BF16)]
    else:
        scratch = [pltpu.VMEM((bm, bn), F32)] if nk > 1 else []
    out = _pcall(
        body, plan=plan, name=name, grid=(M // bm, N // bn, nk),
        in_specs=[a_spec, b_spec] + e_specs,
        out_specs=[t_spec] * (n_o - n_stats) + [stat_spec] * n_stats,
        out_shape=[jax.ShapeDtypeStruct((M, N), dt) for dt in out_dtypes[:n_o - n_stats]]
        + [jax.ShapeDtypeStruct((8 * (M // bm), N), F32)] * n_stats,
        scratch_shapes=scratch,
        compiler_params=_params(("parallel", "arbitrary", "arbitrary")),
    )(a, b, *extras, *row_extras)
    if plan is not None:
        return (out[0][0] if n_o == 1 else out[0]), out[1]
    return out[0] if n_o == 1 else out


def _ada_fwd(c_all, w_ada, b_cols):
    n = w_ada.shape[1]
    bn = 512

    def body(c_ref, w_ref, b_ref, o_ref):
        cv = c_ref[...]
        sc = cv * _sig(cv)
        o_ref[...] = jnp.dot(sc, w_ref[...], precision=HIGHEST, preferred_element_type=F32) + b_ref[...]

    return _pcall(
        body, name="ada_fwd", grid=(n // bn,),
        in_specs=[pl.BlockSpec((N_DEV, D_MODEL), lambda j: (0, 0)), pl.BlockSpec((D_MODEL, bn), lambda j: (0, j)),
                  pl.BlockSpec((1, bn), lambda j: (0, j))],
        out_specs=pl.BlockSpec((N_DEV, bn), lambda j: (0, j)),
        out_shape=jax.ShapeDtypeStruct((N_DEV, n), F32),
        compiler_params=_params(("parallel",)),
    )(c_all, w_ada, b_cols)


ROWS_EW = 256


def _rms_fwd_math(x, gain, scale, shift):
    rstd = lax.rsqrt(jnp.mean(x * x, axis=-1, keepdims=True) + EPS)
    xhat = x * rstd
    n = xhat * gain
    return n * (1.0 + scale) + shift, xhat, n, rstd


def _rms_bwd_math(dh, xhat, n, rstd, gain, scale):
    dn = dh * (1.0 + scale)
    dxhat = dn * gain
    dx = rstd * (dxhat - xhat * jnp.mean(dxhat * xhat, axis=-1, keepdims=True))
    d_scale = jnp.sum(dh * n, axis=0, keepdims=True)
    d_shift = jnp.sum(dh, axis=0, keepdims=True)
    d_gain = jnp.sum(dn * xhat, axis=0, keepdims=True)
    return dx, d_scale, d_shift, d_gain


def _row_spec(w=D_MODEL, br=ROWS_EW):
    return pl.BlockSpec((br, w), lambda i: (i, 0))


def _vec_spec(r=8, w=D_MODEL):
    return pl.BlockSpec((r, w), lambda i: (0, 0))


def _norm1_fwd(x, gain, mod8, plan=None):
    T = x.shape[0]

    def body(x_ref, g_ref, m_ref, h_ref):
        h, _, _, _ = _rms_fwd_math(x_ref[...], g_ref[...], m_ref[1:2, :], m_ref[0:1, :])
        h_ref[...] = h.astype(BF16)

    return _pcall(
        body, plan=plan, name="norm1_fwd", grid=(T // ROWS_EW,),
        in_specs=[_row_spec(), _vec_spec(1), _vec_spec()],
        out_specs=_row_spec(), out_shape=jax.ShapeDtypeStruct((T, D_MODEL), BF16),
        compiler_params=_params(("parallel",)),
    )(x, gain, mod8)


def _res_norm2_fwd(x, mo, gain, mod8):
    T = x.shape[0]
    br = ROWS_EW

    def body(x_ref, mo_ref, g_ref, m_ref, x1_ref, h_ref):
        x1 = x_ref[...] + m_ref[2:3, :] * mo_ref[...]
        x1_ref[...] = x1
        h, _, _, _ = _rms_fwd_math(x1, g_ref[...], m_ref[4:5, :], m_ref[3:4, :])
        h_ref[...] = h.astype(BF16)

    return _pcall(
        body, name="res_norm2_fwd", grid=(T // br,),
        in_specs=[_row_spec(br=br), _row_spec(br=br), _vec_spec(1), _vec_spec()],
        out_specs=[_row_spec(br=br), _row_spec(br=br)],
        out_shape=[jax.ShapeDtypeStruct((T, D_MODEL), F32), jax.ShapeDtypeStruct((T, D_MODEL), BF16)],
        compiler_params=_params(("parallel",)),
    )(x, mo, gain, mod8)


def _loss_head(mlp, x1, target, mod):
    gate = mod[5:6, :]
    err = x1 + gate * mlp - target
    dy = err * (1.0 / D_MODEL)
    row = lax.broadcasted_iota(jnp.int32, (8, mlp.shape[1]), 0)
    stats = jnp.where(row == 0, jnp.sum(err * err, axis=0, keepdims=True),
                      jnp.where(row == 1, jnp.sum(dy * mlp, axis=0, keepdims=True), 0.0))
    return dy, dy * gate, stats


def _norm2_bwd(dh2, x1, dy, mo, gain, mod8):
    T = x1.shape[0]

    def body(dh_ref, x1_ref, dy_ref, mo_ref, g_ref, m_ref, dx1_ref, dmo_ref, st_ref):
        i = pl.program_id(0)
        gain_v, scale = g_ref[...], m_ref[4:5, :]
        _, xhat, n, rstd = _rms_fwd_math(x1_ref[...], gain_v, scale, m_ref[3:4, :])
        dx, d_scale, d_shift, d_gain = _rms_bwd_math(dh_ref[...], xhat, n, rstd, gain_v, scale)
        dx1 = dy_ref[...] + dx
        dx1_ref[...] = dx1
        dmo_ref[...] = (dx1 * m_ref[2:3, :]).astype(BF16)

        @pl.when(i == 0)
        def _():
            st_ref[...] = jnp.zeros_like(st_ref)

        st_ref[0:1, :] += d_scale
        st_ref[1:2, :] += d_shift
        st_ref[2:3, :] += d_gain
        st_ref[3:4, :] += jnp.sum(dx1 * mo_ref[...], axis=0, keepdims=True)

    return _pcall(
        body, name="norm2_bwd", grid=(T // ROWS_EW,),
        in_specs=[_row_spec(), _row_spec(), _row_spec(), _row_spec(), _vec_spec(1), _vec_spec()],
        out_specs=[_row_spec(), _row_spec(), _vec_spec()],
        out_shape=[jax.ShapeDtypeStruct((T, D_MODEL), F32), jax.ShapeDtypeStruct((T, D_MODEL), BF16),
                   jax.ShapeDtypeStruct((8, D_MODEL), F32)],
        compiler_params=_params(("arbitrary",)),
    )(dh2, x1, dy, mo, gain, mod8)


def _norm1_bwd(dh, x, dx1, gain, mod8):
    T = x.shape[0]
    br = ROWS_EW

    def body(dh_ref, x_ref, dx1_ref, g_ref, m_ref, dx_ref, st_ref):
        i = pl.program_id(0)
        gain_v, scale = g_ref[...], m_ref[1:2, :]
        _, xhat, n, rstd = _rms_fwd_math(x_ref[...], gain_v, scale, m_ref[0:1, :])
        dx, d_scale, d_shift, d_gain = _rms_bwd_math(dh_ref[...], xhat, n, rstd, gain_v, scale)
        dx_ref[...] = dx1_ref[...] + dx

        @pl.when(i == 0)
        def _():
            st_ref[...] = jnp.zeros_like(st_ref)

        st_ref[0:1, :] += d_scale
        st_ref[1:2, :] += d_shift
        st_ref[2:3, :] += d_gain

    return _pcall(
        body, name="norm1_bwd", grid=(T // br,),
        in_specs=[_row_spec(br=br), _row_spec(br=br), _row_spec(br=br), _vec_spec(1), _vec_spec()],
        out_specs=[_row_spec(br=br), _vec_spec()],
        out_shape=[jax.ShapeDtypeStruct((T, D_MODEL), F32), jax.ShapeDtypeStruct((8, D_MODEL), F32)],
        compiler_params=_params(("arbitrary",)),
    )(dh, x, dx1, gain, mod8)


MERGE_BC = 512


def _hgrn_rows(T):
    return 512 if T >= 1024 else 128


def _lower_bound(lbl):
    e = jnp.exp(lbl - jnp.max(lbl, axis=0, keepdims=True))
    return e[0:1, :] / (e[0:1, :] + e[1:2, :])


def _chunk_sum_matrix(rows, backward):
    shift = A_CHUNK.bit_length() - 1
    r = lax.broadcasted_iota(jnp.int32, (rows, rows), 0)
    c = lax.broadcasted_iota(jnp.int32, (rows, rows), 1)
    same = jnp.right_shift(r, shift) == jnp.right_shift(c, shift)
    return (same & ((r <= c) if backward else (r >= c))).astype(BF16)


def _chunk_sums(m, x):
    n = x.shape[1]
    hi = x.astype(BF16)
    rest = x - hi.astype(F32)
    mid = rest.astype(BF16)
    lo = (rest - mid.astype(F32)).astype(BF16)
    y = jnp.dot(m, jnp.concatenate([hi, mid, lo], axis=1), preferred_element_type=F32)
    return y[:, 0:n] + y[:, n:2 * n] + y[:, 2 * n:3 * n]


def _hgrn_block_pre(q, fl, lb, m_fwd):
    sg = _sig(fl)
    f = lb + (1.0 - lb) * sg
    sq = _sig(q)
    return dict(sg=sg, f=f, k=1.0 - f, sq=sq, qf=q * sq, b=_chunk_sums(m_fwd, jnp.log(f)))


def _hgrn_chunk_local(pre, r):
    C = A_CHUNK
    qf, k, b = pre["qf"][r], pre["k"][r], pre["b"][r]
    causal = lax.broadcasted_iota(jnp.int32, (C, C), 0) >= lax.broadcasted_iota(jnp.int32, (C, C), 1)
    bm = b[C // 2 - 1:C // 2, :]
    bl = b[C - 1:C, :]
    e_q, e_k = jnp.exp(b - bm), jnp.exp(bm - b)
    e_b, e_l = jnp.exp(b), jnp.exp(bl - b)
    qd, kd = qf * e_q, k * e_k
    qe, ke = qf * e_b, k * e_l
    att = jnp.where(causal, _nt(qd, kd), 0.0)
    return dict(causal=causal, e_q=e_q, e_k=e_k, e_b=e_b, e_l=e_l, qd=qd, kd=kd, qe=qe, ke=ke, att=att, dec=jnp.exp(bl))


def _hgrn_chunk_fwd(pre, r, v, st):
    c = _hgrn_chunk_local(pre, r)
    c["o"] = _nn(c["att"], v) + _nt(c["qe"], st)
    return c


def _lockstep(gens):
    out = [None] * len(gens)
    live = list(enumerate(gens))
    while live:
        still = []
        for i, g in live:
            try:
                next(g)
                still.append((i, g))
            except StopIteration as done:
                out[i] = done.value
        live = still
    return out


HGRN_HEADS_PER_STEP = 4


def _hgrn_fwd(proj, lb_logits, o_gain, plan=None):
    T = proj.shape[0]
    BR = _hgrn_rows(T)
    cps = BR // A_CHUNK
    K, NH = A_HEAD_DIM, HGRN_HEADS_PER_STEP
    W = NH * K

    def col(off):
        return pl.BlockSpec((BR, W), lambda h, cb: (cb, off // W + h))

    def body(q_ref, f_ref, i_ref, g_ref, lbl_ref, og_ref, o_ref, s_ref, st):
        @pl.when(pl.program_id(1) == 0)
        def _():
            st[...] = jnp.zeros_like(st)

        lb_all = _lower_bound(lbl_ref[...])
        m_fwd = _chunk_sum_matrix(BR, False)
        pre = [_hgrn_block_pre(q_ref[:, n * K:(n + 1) * K], f_ref[:, n * K:(n + 1) * K], lb_all[:, n * K:(n + 1) * K], m_fwd)
               for n in range(NH)]
        def local(n, ci):
            r, hs = slice(ci * A_CHUNK, (ci + 1) * A_CHUNK), slice(n * K, (n + 1) * K)
            v = i_ref[r, hs]
            c = _hgrn_chunk_local(pre[n], r)
            yield
            return dict(o=_nn(c["att"], v), ds=_tn(v, c["ke"]), qe=c["qe"], dec=c["dec"])

        def chain(n, loc):
            hs = slice(n * K, (n + 1) * K)
            state = st[n]
            for ci, p in enumerate(loc):
                r = slice(ci * A_CHUNK, (ci + 1) * A_CHUNK)
                s_ref[n, ci] = state
                o = p["o"] + _nt(p["qe"], state)
                state = state * p["dec"] + p["ds"]
                yield
                on = o * lax.rsqrt(jnp.mean(o * o, axis=-1, keepdims=True) + EPS)
                g = g_ref[r, hs]
                o_ref[r, hs] = (on * og_ref[:, hs] * (g * _sig(g))).astype(BF16)
            st[n] = state

        loc = _lockstep([local(n, ci) for n in range(NH) for ci in range(cps)])
        _lockstep([chain(n, loc[n * cps:(n + 1) * cps]) for n in range(NH)])

    return _pcall(
        body, plan=plan, name="hgrn_fwd", grid=(A_HEADS // NH, T // BR),
        in_specs=[col(OFF_QA), col(OFF_FA), col(OFF_IA), col(OFF_GA),
                  pl.BlockSpec((2, W), lambda h, cb: (0, h)), pl.BlockSpec((1, W), lambda h, cb: (0, h))],
        out_specs=[pl.BlockSpec((BR, W), lambda h, cb: (cb, h)),
                   pl.BlockSpec((NH, cps, K, K), lambda h, cb: (h, cb, 0, 0))],
        out_shape=[jax.ShapeDtypeStruct((T, A_WIDTH), BF16),
                   jax.ShapeDtypeStruct((A_HEADS, T // A_CHUNK, K, K), F32)],
        scratch_shapes=[pltpu.VMEM((NH, K, K), F32)],
        compiler_params=_params(("parallel", "arbitrary")),
    )(proj, proj, proj, proj, lb_logits, o_gain)


def _hgrn_bwd(proj, lb_logits, o_gain, states, do, plan=None):
    T = proj.shape[0]
    BR = _hgrn_rows(T)
    cps = BR // A_CHUNK
    ncb = T // BR
    K, C, NH = A_HEAD_DIM, A_CHUNK, HGRN_HEADS_PER_STEP
    W = NH * K

    def col(off):
        return pl.BlockSpec((BR, W), lambda h, cb: (ncb - 1 - cb, off // W + h))

    def body(q_ref, f_ref, i_ref, g_ref, lbl_ref, og_ref, s_ref, do_ref,
             dq_ref, df_ref, di_ref, dg_ref, dlb_ref, dog_ref, dst):
        @pl.when(pl.program_id(1) == 0)
        def _():
            dst[...] = jnp.zeros_like(dst)
            dlb_ref[...] = jnp.zeros_like(dlb_ref)
            dog_ref[...] = jnp.zeros_like(dog_ref)

        lb_all = _lower_bound(lbl_ref[...])
        row = lax.broadcasted_iota(jnp.int32, (C, K), 0)
        m_fwd, m_bwd = _chunk_sum_matrix(BR, False), _chunk_sum_matrix(BR, True)
        pre = [_hgrn_block_pre(q_ref[:, n * K:(n + 1) * K], f_ref[:, n * K:(n + 1) * K], lb_all[:, n * K:(n + 1) * K], m_fwd)
               for n in range(NH)]
        def local(n, ci):
            r, hs = slice(ci * C, (ci + 1) * C), slice(n * K, (n + 1) * K)
            gain = og_ref[:, hs]
            st = s_ref[n, ci]
            v = i_ref[r, hs]
            q = q_ref[r, hs]
            c = _hgrn_chunk_fwd(pre[n], r, v, st)
            yield
            o = c["o"]
            rn = lax.rsqrt(jnp.mean(o * o, axis=-1, keepdims=True) + EPS)
            on = o * rn
            g = g_ref[r, hs]
            sgg = _sig(g)
            dy = do_ref[r, hs]
            d_ong = dy * (g * sgg)
            dg_ref[r, hs] = (dy * (on * gain) * (sgg * (1.0 + g * (1.0 - sgg)))).astype(BF16)
            d_on = d_ong * gain
            d_o = rn * (d_on - on * jnp.mean(d_on * on, axis=-1, keepdims=True))
            datt = jnp.where(c["causal"], _nt(d_o, v), 0.0)
            dqe = _nn(d_o, st)
            yield
            dqd = _nn(datt, c["kd"])
            dkd = _tn(datt, c["qd"])
            dv = _tn(c["att"], d_o)
            ds = _tn(d_o, c["qe"])
            yield
            t_q, t_k = dqd * c["qd"], dkd * c["kd"]
            sq = pre[n]["sq"][r]
            dq_ref[r, hs] = ((dqd * c["e_q"] + dqe * c["e_b"]) * (sq * (1.0 + q * (1.0 - sq)))).astype(BF16)
            return dict(v=v, st=st, ke=c["ke"], e_l=c["e_l"], dec=c["dec"], dv=dv, ds=ds, dk=dkd * c["e_k"],
                        db=t_q - t_k + dqe * c["qe"], dbm=jnp.sum(t_k - t_q, axis=0, keepdims=True),
                        d_og=jnp.sum(d_ong * on, axis=0, keepdims=True))

        def chain(n, loc):
            hs = slice(n * K, (n + 1) * K)
            dst_next = dst[n]
            db_of, dk_of = [None] * cps, [None] * cps
            for ci in reversed(range(cps)):
                p = loc[ci]
                di_ref[ci * C:(ci + 1) * C, hs] = (p["dv"] + _nt(p["ke"], dst_next)).astype(BF16)
                dke = _nn(p["v"], dst_next)
                yield
                t_l = dke * p["ke"]
                dbl = jnp.sum(t_l, axis=0, keepdims=True) + jnp.sum(dst_next * p["st"], axis=0, keepdims=True) * p["dec"]
                db_of[ci] = p["db"] - t_l + jnp.where(row == C // 2 - 1, p["dbm"], 0.0) + jnp.where(row == C - 1, dbl, 0.0)
                dk_of[ci] = p["dk"] + dke * p["e_l"]
                dst_next = dst_next * p["dec"] + p["ds"]
            dst[n] = dst_next
            return db_of, dk_of

        loc = _lockstep([local(n, ci) for n in range(NH) for ci in range(cps)])
        loc = [loc[n * cps:(n + 1) * cps] for n in range(NH)]
        chains = _lockstep([chain(n, loc[n]) for n in range(NH)])
        for n in range(NH):
            hs = slice(n * K, (n + 1) * K)
            db_of, dk_of = chains[n]
            d_og = loc[n][0]["d_og"]
            for p in loc[n][1:]:
                d_og = d_og + p["d_og"]
            dog_ref[0:1, hs] += d_og
            lb, sg = lb_all[:, hs], pre[n]["sg"]
            dlf = _chunk_sums(m_bwd, jnp.concatenate(db_of, axis=0))
            df = dlf / pre[n]["f"] - jnp.concatenate(dk_of, axis=0)
            df_ref[:, hs] = (df * (1.0 - lb) * sg * (1.0 - sg)).astype(BF16)
            dlb_ref[0:1, hs] += jnp.sum(df * (1.0 - sg), axis=0, keepdims=True)

    ocol = pl.BlockSpec((BR, W), lambda h, cb: (ncb - 1 - cb, h))
    vec = pl.BlockSpec((8, W), lambda h, cb: (0, h))
    return _pcall(
        body, plan=plan, name="hgrn_bwd", grid=(A_HEADS // NH, ncb),
        in_specs=[col(OFF_QA), col(OFF_FA), col(OFF_IA), col(OFF_GA),
                  pl.BlockSpec((2, W), lambda h, cb: (0, h)), pl.BlockSpec((1, W), lambda h, cb: (0, h)),
                  pl.BlockSpec((NH, cps, K, K), lambda h, cb: (h, ncb - 1 - cb, 0, 0)),
                  pl.BlockSpec((BR, W), lambda h, cb: (ncb - 1 - cb, h))],
        out_specs=[ocol, ocol, ocol, ocol, vec, vec],
        out_shape=[jax.ShapeDtypeStruct((T, A_WIDTH), BF16)] * 4 + [jax.ShapeDtypeStruct((8, A_WIDTH), F32)] * 2,
        scratch_shapes=[pltpu.VMEM((NH, K, K), F32)],
        compiler_params=_params(("parallel", "arbitrary")),
    )(proj, proj, proj, proj, lb_logits, o_gain, states, do)


def _head_norm(x):
    r = lax.rsqrt(jnp.mean(x * x, axis=-1, keepdims=True) + EPS)
    return x * r, r


def _head_norm_bwd(dy, xn, r, gain):
    dxn = dy * gain
    return r * (dxn - xn * jnp.mean(dxn * xn, axis=-1, keepdims=True)), jnp.sum(dy * xn, axis=0, keepdims=True)


def _swa_mask(has_prev):
    rows = B_GROUP * BLOCK
    r = lax.broadcasted_iota(jnp.int32, (rows, 2 * BLOCK), 0) % BLOCK
    c = lax.broadcasted_iota(jnp.int32, (rows, 2 * BLOCK), 1)
    rel = r + BLOCK - c
    return (rel >= 0) & (rel < BLOCK) & ((c >= BLOCK) | has_prev)


def _swa_head_fwd(j, q_ref, kp_ref, kc_ref, vp_ref, vc_ref, qg, kg, sk_ref, mask):
    hs = slice(j * B_HEAD_DIM, (j + 1) * B_HEAD_DIM)
    kcat = jnp.concatenate([kp_ref[:, hs], kc_ref[:, hs]], axis=0)
    vcat = jnp.concatenate([vp_ref[:, hs], vc_ref[:, hs]], axis=0)
    qs = jnp.concatenate([q_ref[:, pl.ds((j * B_GROUP + g) * B_HEAD_DIM, B_HEAD_DIM)] for g in range(B_GROUP)], axis=0)
    kn, kr = _head_norm(kcat)
    qn, qr = _head_norm(qs)
    kh, qh = kn * kg, qn * qg
    yield
    s = jnp.where(mask, _nt(qh, kh) * (B_HEAD_DIM ** -0.5), NEG_BIG)
    yield
    sink = jnp.concatenate(
        [jnp.broadcast_to(sk_ref[0:1, pl.ds(j * B_GROUP + g, 1)], (BLOCK, 1)) for g in range(B_GROUP)], axis=0)
    m = jnp.maximum(jnp.max(s, axis=-1, keepdims=True), sink)
    p = jnp.exp(s - m)
    e_sink = jnp.exp(sink - m)
    inv = 1.0 / (jnp.sum(p, axis=-1, keepdims=True) + e_sink)
    prob = p * inv
    return dict(vcat=vcat, kn=kn, kr=kr, qn=qn, qr=qr, kh=kh, qh=qh, prob=prob, p_sink=e_sink * inv)


def _swa_in_specs(nb, last):
    def qi(n):
        return jnp.minimum(n, last)

    q = pl.BlockSpec((BLOCK, B_WIDTH), lambda n: (qi(n), OFF_QB // B_WIDTH))
    kc = pl.BlockSpec((BLOCK, B_KV_WIDTH), lambda n: (qi(n), OFF_KB // B_KV_WIDTH))
    kp = pl.BlockSpec((BLOCK, B_KV_WIDTH), lambda n: (jnp.maximum(qi(n) - 1, 0), OFF_KB // B_KV_WIDTH))
    vc = pl.BlockSpec((BLOCK, B_KV_WIDTH), lambda n: (qi(n), OFF_VB // B_KV_WIDTH))
    vp = pl.BlockSpec((BLOCK, B_KV_WIDTH), lambda n: (jnp.maximum(qi(n) - 1, 0), OFF_VB // B_KV_WIDTH))
    small = [pl.BlockSpec((1, B_HEAD_DIM), lambda n: (0, 0)), pl.BlockSpec((1, B_HEAD_DIM), lambda n: (0, 0)),
             pl.BlockSpec((1, B_GROUP * B_KV_HEADS), lambda n: (0, 0))]
    return [q, kp, kc, vp, vc] + small


def _swa_fwd(proj, q_gain, k_gain, sinks, plan=None):
    T = proj.shape[0]
    nb = T // BLOCK

    def body(q_ref, kp_ref, kc_ref, vp_ref, vc_ref, qg_ref, kg_ref, sk_ref, o_ref):
        mask = _swa_mask(pl.program_id(0) > 0)

        def head(j):
            c = yield from _swa_head_fwd(j, q_ref, kp_ref, kc_ref, vp_ref, vc_ref, qg_ref[...], kg_ref[...], sk_ref, mask)
            yield
            o = _nn(c["prob"], c["vcat"])
            yield
            for g in range(B_GROUP):
                o_ref[:, pl.ds((j * B_GROUP + g) * B_HEAD_DIM, B_HEAD_DIM)] = o[g * BLOCK:(g + 1) * BLOCK].astype(BF16)

        _lockstep([head(j) for j in range(B_KV_HEADS)])

    return _pcall(
        body, plan=plan, name="swa_fwd", grid=(nb,),
        in_specs=_swa_in_specs(nb, nb - 1),
        out_specs=pl.BlockSpec((BLOCK, B_WIDTH), lambda n: (n, 0)),
        out_shape=jax.ShapeDtypeStruct((T, B_WIDTH), BF16),
        compiler_params=_params(("parallel",)),
    )(proj, proj, proj, proj, proj, q_gain, k_gain, sinks)


def _swa_bwd(proj, q_gain, k_gain, sinks, do, plan=None):
    T = proj.shape[0]
    nb = T // BLOCK
    scale = B_HEAD_DIM ** -0.5

    def body(q_ref, kp_ref, kc_ref, vp_ref, vc_ref, qg_ref, kg_ref, sk_ref, do_ref,
             dq_ref, dkv_ref, sm_ref, ck, cv):
        n = pl.program_id(0)

        @pl.when(n == 0)
        def _():
            ck[...] = jnp.zeros_like(ck)
            cv[...] = jnp.zeros_like(cv)
            sm_ref[...] = jnp.zeros_like(sm_ref)

        @pl.when(n < nb)
        def _():
            mask = _swa_mask(n > 0)
            qg, kg = qg_ref[...], kg_ref[...]
            lane = lax.broadcasted_iota(jnp.int32, (1, BLOCK), 1)
            def head(j):
                hs = slice(j * B_HEAD_DIM, (j + 1) * B_HEAD_DIM)
                vs = slice(B_KV_WIDTH + j * B_HEAD_DIM, B_KV_WIDTH + (j + 1) * B_HEAD_DIM)
                c = yield from _swa_head_fwd(j, q_ref, kp_ref, kc_ref, vp_ref, vc_ref, qg, kg, sk_ref, mask)
                d_out = jnp.concatenate(
                    [do_ref[:, pl.ds((j * B_GROUP + g) * B_HEAD_DIM, B_HEAD_DIM)] for g in range(B_GROUP)], axis=0)
                prob = c["prob"]
                yield
                out = _nn(prob, c["vcat"])
                d_prob = _nt(d_out, c["vcat"])
                dv = _tn(prob, d_out)
                yield
                delta = jnp.sum(d_out * out, axis=-1, keepdims=True)
                ds = prob * (d_prob - delta)
                d_sink = -c["p_sink"] * delta
                yield
                dqh = _nn(ds, c["kh"]) * scale
                dkh = _tn(ds, c["qh"]) * scale
                yield
                dq, dqg = _head_norm_bwd(dqh, c["qn"], c["qr"], qg)
                dk, dkg = _head_norm_bwd(dkh, c["kn"], c["kr"], kg)
                d_sinks = jnp.zeros((1, BLOCK), F32)
                for g in range(B_GROUP):
                    dq_ref[:, pl.ds((j * B_GROUP + g) * B_HEAD_DIM, B_HEAD_DIM)] = dq[g * BLOCK:(g + 1) * BLOCK].astype(BF16)
                    tot = jnp.sum(d_sink[g * BLOCK:(g + 1) * BLOCK], axis=0, keepdims=True)
                    d_sinks = d_sinks + jnp.where(lane == j * B_GROUP + g, tot, 0.0)
                dkv_ref[:, hs] = (ck[:, hs] + dk[0:BLOCK]).astype(BF16)
                dkv_ref[:, vs] = (cv[:, hs] + dv[0:BLOCK]).astype(BF16)
                ck[:, hs] = dk[BLOCK:2 * BLOCK]
                cv[:, hs] = dv[BLOCK:2 * BLOCK]
                return dqg, dkg, d_sinks

            small = _lockstep([head(j) for j in range(B_KV_HEADS)])
            sm_ref[0:1, 0:B_HEAD_DIM] += small[0][0] + small[1][0] + small[2][0] + small[3][0]
            sm_ref[1:2, 0:B_HEAD_DIM] += small[0][1] + small[1][1] + small[2][1] + small[3][1]
            sm_ref[2:3, :] += small[0][2] + small[1][2] + small[2][2] + small[3][2]

        @pl.when(n == nb)
        def _():
            dkv_ref[:, 0:B_KV_WIDTH] = ck[...].astype(BF16)
            dkv_ref[:, B_KV_WIDTH:2 * B_KV_WIDTH] = cv[...].astype(BF16)

    return _pcall(
        body, plan=plan, name="swa_bwd", grid=(nb + 1,),
        in_specs=_swa_in_specs(nb, nb - 1) + [pl.BlockSpec((BLOCK, B_WIDTH), lambda n: (jnp.minimum(n, nb - 1), 0))],
        out_specs=[pl.BlockSpec((BLOCK, B_WIDTH), lambda n: (jnp.minimum(n, nb - 1), 0)),
                   pl.BlockSpec((BLOCK, 2 * B_KV_WIDTH), lambda n: (jnp.maximum(n - 1, 0), 0)),
                   pl.BlockSpec((8, BLOCK), lambda n: (0, 0))],
        out_shape=[jax.ShapeDtypeStruct((T, B_WIDTH), BF16), jax.ShapeDtypeStruct((T, 2 * B_KV_WIDTH), BF16),
                   jax.ShapeDtypeStruct((8, BLOCK), F32)],
        scratch_shapes=[pltpu.VMEM((BLOCK, B_KV_WIDTH), F32), pltpu.VMEM((BLOCK, B_KV_WIDTH), F32)],
        compiler_params=_params(("arbitrary",)),
    )(proj, proj, proj, proj, proj, q_gain, k_gain, sinks, do)


W_IN, W_A, W_B, W_OUT, W_MI, W_MO = range(6)


def _local_step(x, target, mod8, norm1_gain, norm2_gain, lb_logits, o_gain, q_gain, k_gain, sinks, shards, c_arr, chip_arr):
    relu2 = lambda u: (u, jnp.square(jnp.maximum(u, 0.0)))
    pair, half = {}, {}

    def exchange(ws, grads):
        return _sibling_exchange_plan([_grad_view(g, w) for w, g in zip(ws, grads)])

    def pair_sums(ws, grads, others):
        for w, g, o in zip(ws, grads, others):
            pair[w] = _pair_sum(_grad_view(g, w), o, c_arr, f"pair_sum{w}")

    def sum_slots(ws, slots):
        for w, s in zip(ws, slots):
            half[w] = _sum_slots(pair[w], s, w, chip_arr, f"sum_slots{w}")

    part_in = _cast_into_full({W_IN: shards[W_IN]}, "cast_w_in")[W_IN]
    h, (part_in,) = _norm1_fwd(x, norm1_gain, mod8, plan=_gather_plan({W_IN: part_in}, part="near"))
    parts, (w_in,) = _cast_into_full({w: shards[w] for w in range(1, N_W)}, "cast_rest",
                                     plan=_gather_plan({W_IN: part_in}, pass_at=(0.97,), part="far"))
    proj, (w_mi,) = _mm(h, w_in, name="mm_proj", bn=512, plan=_gather_plan({W_MI: parts[W_MI]}, pass_at=(0.47, 0.72)))
    (o_a, states), (w_a, w_b) = _hgrn_fwd(
        proj, lb_logits, o_gain, plan=_gather_plan({w: parts[w] for w in (W_A, W_B)}, pass_at=(0.4, 0.65)))
    o_b, (w_out,) = _swa_fwd(proj, q_gain, k_gain, sinks, plan=_gather_plan({W_OUT: parts[W_OUT]}, pass_at=(0.3, 0.5)))
    ya = _mm(o_a, w_a, name="mm_branch_a")
    gate_cols = (OFF_GATE_A // MERGE_BC, OFF_GATE_B // MERGE_BC)
    yb, merged = _mm(o_b, w_b, name="mm_branch_b", bn=MERGE_BC, out_dtypes=(F32, BF16),
                     extras=(proj, proj, ya), extra_cols=gate_cols + (0,),
                     epi=lambda acc, ga, gb, ya_: (acc, _sig(ga) * ya_ + _sig(gb) * acc))
    mo = _mm(merged, w_out, name="mm_out")
    x1, h2 = _res_norm2_fwd(x, mo, norm2_gain, mod8)
    (u, act), (w_mo,) = _mm(h2, w_mi, name="mm_mlp_in", out_dtypes=(F32, BF16), epi=relu2,
                            plan=_gather_plan({W_MO: parts[W_MO]}, pass_at=(0.6, 0.9)))
    dy, dmlp, st_loss = _mm(act, w_mo, name="mm_mlp_out", bm=512, out_dtypes=(F32, BF16, F32), n_stats=1,
                            extras=(x1, target), row_extras=(mod8,), epi=_loss_head)
    st_loss = st_loss.reshape(-1, 8, D_MODEL).sum(axis=0)
    def half_blocks(w, own):
        def block(i):
            return 2 * i + (lax.axis_index("c") if own else 1 - lax.axis_index("c"))
        return (1 if W_SHAPES[w][2] else N_CHIPS), block

    def pair_of(w, lhs, rhs, other, name):
        hr, cols = _half_shape(w)
        p = _mm(lhs, rhs, name=name, ta=True, bn=512, a_blocks=half_blocks(w, True), out_dtypes=(BF16,),
                extras=(other,), epi=lambda acc, o: (acc + o,))
        return p.reshape(-1, hr, W_SHAPES[w][1])

    g_send = _mm(act, dmlp, name="mm_g_mlp_out_send", ta=True, bn=512, a_blocks=half_blocks(W_MO, False))
    du, (g_other,) = _mm(dmlp, w_mo, name="mm_d_act", tb=True, out_dtypes=(BF16,), extras=(u,),
                         epi=lambda acc, uu: (acc * (2.0 * jnp.maximum(uu, 0.0)),), plan=_sibling_share_plan([g_send]))
    pair[W_MO] = pair_of(W_MO, act, dmlp, g_other, "mm_g_mlp_out_own")
    g_send = _mm(h2, du, name="mm_g_mlp_in_send", ta=True, bn=512, a_blocks=half_blocks(W_MI, False))
    dh2, res = _mm(du, w_mi, name="mm_d_h2", tb=True,
                   plan=_join(_chip_exchange_plan({W_MO: pair[W_MO]}), _sibling_share_plan([g_send])))
    sum_slots([W_MO], res[:1])
    pair[W_MI] = pair_of(W_MI, h2, du, res[1], "mm_g_mlp_in_own")
    dx1, dmo, st_n2 = _norm2_bwd(dh2, x1, dy, mo, norm2_gain, mod8)
    def merge_bwd(dm, ga, gb, ya_, yb_):
        sa, sb = _sig(ga), _sig(gb)
        return dm * sa, dm * sb, dm * ya_ * sa * (1.0 - sa), dm * yb_ * sb * (1.0 - sb)

    dya, dyb, dga, dgb = _mm(dmo, w_out, name="mm_d_merged", tb=True, bn=MERGE_BC, out_dtypes=(BF16,) * 4,
                             extras=(proj, proj, ya, yb), extra_cols=gate_cols + (0, 0), epi=merge_bwd)
    g_out = _mm(merged, dmo, name="mm_g_out", ta=True, bn=512)
    do_a = _mm(dya, w_a, name="mm_d_oa", tb=True)
    g_a = _mm(o_a, dya, name="mm_g_branch_a", ta=True, bn=512)
    do_b = _mm(dyb, w_b, name="mm_d_ob", tb=True)
    g_b = _mm(o_b, dyb, name="mm_g_branch_b", ta=True, bn=512)
    mid = [W_A, W_B, W_OUT]
    (dqb, dkvb, st_swa), res = _swa_bwd(
        proj, q_gain, k_gain, sinks, do_b,
        plan=_join(_chip_exchange_plan({W_MI: pair[W_MI]}), exchange(mid, [g_a, g_b, g_out])))
    sum_slots([W_MI], res[:1])
    pair_sums(mid, [g_a, g_b, g_out], res[1:])
    (dqa, dfa, dia, dgga, d_lb, d_og), slots_mid = _hgrn_bwd(
        proj, lb_logits, o_gain, states, do_a, plan=_chip_exchange_plan({w: pair[w] for w in mid}))
    sum_slots(mid, slots_mid)
    dproj = jnp.concatenate([dqa, dfa, dia, dgga, dqb, dkvb, dga, dgb], axis=1)
    done = [W_A, W_B, W_OUT, W_MI, W_MO]
    g_send, res = _mm(h, dproj, name="mm_g_in_send", ta=True, bn=512, a_blocks=half_blocks(W_IN, False),
                      plan=_sibling_share_plan([half[w] for w in done]))
    theirs = dict(zip(done, res))
    g_own, (g_other,) = _mm(h, dproj, name="mm_g_in_own", ta=True, bn=512, a_blocks=half_blocks(W_IN, True),
                            plan=_sibling_share_plan([g_send]))
    pair[W_IN] = _add_bf16(g_own, g_other, "pair_sum0")[None]
    dh, slots_in = _mm(dproj, w_in, name="mm_d_h", tb=True, bk=2432, plan=_chip_exchange_plan({W_IN: pair[W_IN]}))
    sum_slots([W_IN], slots_in)
    grad_x, st_n1 = _norm1_bwd(dh, x, dx1, norm1_gain, mod8)
    (theirs[W_IN],) = _run_plan(_sibling_share_plan([half[W_IN]]), "sibling_share_w_in")
    stats = dict(loss=st_loss, n2=st_n2, n1=st_n1, d_lb=d_lb, d_og=d_og, swa=st_swa)
    return grad_x, [half[w] for w in range(N_W)], [theirs[w] for w in range(N_W)], stats


EW_VMEM_BYTES = 40 << 20


def _ew_rows(rows, cols, streams):
    br = 8
    while br * 2 * 4 <= rows and br * 2 * cols * 4 * 2 * streams <= EW_VMEM_BYTES and rows % (br * 2) == 0:
        br *= 2
    return br


CAST_STEPS = 16


def _cast_into_full(shards, name, plan=None):
    ws = sorted(shards)
    in_specs, out_specs, out_shape = [], [], []
    for w in ws:
        sr, sc = shards[w].shape
        R, C, by_col = W_SHAPES[w]
        br = sr // CAST_STEPS
        assert br * CAST_STEPS == sr and br % 16 == 0, (w, sr)

        def out_map(i, by_col=by_col):
            chip = 2 * lax.axis_index("x") + lax.axis_index("y")
            return (i, chip) if by_col else (chip * CAST_STEPS + i, 0)

        in_specs.append(pl.BlockSpec((br, sc), lambda i: (i, 0)))
        out_specs.append(pl.BlockSpec((br, sc), out_map))
        out_shape.append(jax.ShapeDtypeStruct((R, C), BF16))

    def body(*refs):
        for w_ref, o_ref in zip(refs[:len(ws)], refs[len(ws):]):
            o_ref[...] = w_ref[...].astype(BF16)

    res = _pcall(body, plan=plan, name=name, grid=(CAST_STEPS,), in_specs=in_specs, out_specs=out_specs,
                 out_shape=out_shape, compiler_params=_params(("arbitrary",)))(*[shards[w] for w in ws])
    if plan is None:
        return dict(zip(ws, res))
    return dict(zip(ws, res[0])), res[1]


def _adamw_math(w, g, m, v):
    m = ADAM_B1 * m + (1.0 - ADAM_B1) * g
    v = ADAM_B2 * v + (1.0 - ADAM_B2) * (g * g)
    m_hat = m / (1.0 - ADAM_B1 ** ADAM_STEP)
    v_hat = v / (1.0 - ADAM_B2 ** ADAM_STEP)
    delta = -ADAM_LR * (m_hat / (jnp.sqrt(v_hat) + ADAM_EPS) + ADAM_WD * w)
    return delta, m, v


def _adamw(w, g, m, v, name):
    R, C = w.shape
    br = _ew_rows(R, C, 7)
    spec = pl.BlockSpec((br, C), lambda i: (i, 0))

    def body(w_ref, g_ref, m_ref, v_ref, d_ref, nm_ref, nv_ref):
        d_ref[...], nm_ref[...], nv_ref[...] = _adamw_math(w_ref[...], g_ref[...], m_ref[...], v_ref[...])

    sh = jax.ShapeDtypeStruct((R, C), F32)
    return _pcall(body, name=name, grid=(R // br,), in_specs=[spec] * 4, out_specs=[spec] * 3, out_shape=[sh] * 3,
                  compiler_params=_params(("parallel",)))(w, g, m, v)


def _add_bf16(a, b, name):
    R, C = a.shape
    br, n_in = 64, 3
    n = R // br
    assert n * br == R and n >= 2

    def body(a_hbm, b_hbm, o_hbm, a_buf, b_buf, o_buf, a_sem, b_sem, o_sem):
        def rows(i):
            return pl.ds(i * br if isinstance(i, int) else pl.multiple_of(i * br, br), br)

        def reads(i):
            s = i % n_in
            return (pltpu.make_async_copy(a_hbm.at[rows(i)], a_buf.at[s], a_sem.at[s]),
                    pltpu.make_async_copy(b_hbm.at[rows(i)], b_buf.at[s], b_sem.at[s]))

        def write(i):
            return pltpu.make_async_copy(o_buf.at[i % 2], o_hbm.at[rows(i)], o_sem.at[i % 2])

        for i in range(n_in - 1):
            for cp in reads(i):
                cp.start()

        def step(i, carry):
            @pl.when(i + n_in - 1 < n)
            def _():
                for cp in reads(i + n_in - 1):
                    cp.start()

            for cp in reads(i):
                cp.wait()

            @pl.when(i >= 2)
            def _():
                write(i - 2).wait()

            o_buf[i % 2] = (a_buf[i % n_in] + b_buf[i % n_in]).astype(BF16)
            write(i).start()
            return carry

        lax.fori_loop(0, n, step, 0)
        write(n - 2).wait()
        write(n - 1).wait()

    any_spec = pl.BlockSpec(memory_space=pl.ANY)
    return _pcall(
        body, name=name, in_specs=[any_spec, any_spec], out_specs=any_spec,
        out_shape=jax.ShapeDtypeStruct((R, C), BF16),
        scratch_shapes=[pltpu.VMEM((n_in, br, C), F32), pltpu.VMEM((n_in, br, C), F32), pltpu.VMEM((2, br, C), BF16),
                        pltpu.SemaphoreType.DMA((n_in,)), pltpu.SemaphoreType.DMA((n_in,)), pltpu.SemaphoreType.DMA((2,))],
        compiler_params=_params())(a, b)


def _adamw_halves(w, own, other, m, v, c_arr, name):
    R, C = w.shape
    hr = R // 2
    br = _ew_rows(hr, C, 9)
    nb = hr // br
    full = pl.BlockSpec((br, C), lambda h, i, c_ref: (h * nb + i, 0))

    def own_map(h, i, c_ref):
        return jnp.where(h == c_ref[0], i, jnp.where(c_ref[0] == 0, nb - 1, 0)), 0

    def other_map(h, i, c_ref):
        return jnp.where(h != c_ref[0], i, jnp.where(c_ref[0] == 0, 0, nb - 1)), 0

    def body(c_ref, w_ref, own_ref, oth_ref, m_ref, v_ref, g_ref, d_ref, nm_ref, nv_ref):
        g = jnp.where(pl.program_id(0) == c_ref[0], own_ref[...], oth_ref[...])
        g_ref[...] = g
        d_ref[...], nm_ref[...], nv_ref[...] = _adamw_math(w_ref[...], g, m_ref[...], v_ref[...])

    sh = jax.ShapeDtypeStruct((R, C), F32)
    return _pcall(
        body, name=name,
        grid_spec=pltpu.PrefetchScalarGridSpec(
            num_scalar_prefetch=1, grid=(2, nb),
            in_specs=[full, pl.BlockSpec((br, C), own_map), pl.BlockSpec((br, C), other_map), full, full],
            out_specs=[full] * 4),
        out_shape=[sh] * 4, compiler_params=_params(("arbitrary", "arbitrary")))(c_arr, w, own, other, m, v)


def _ada_grad_adamw(c_t, dmod, w, m, v):
    R, C = w.shape
    br = _ew_rows(R, C, 8)
    spec = pl.BlockSpec((br, C), lambda i: (i, 0))

    def body(c_ref, dm_ref, w_ref, m_ref, v_ref, g_ref, d_ref, nm_ref, nv_ref):
        cv = c_ref[...]
        sc = cv * _sig(cv)
        g = sc[:, 0:1] * dm_ref[0:1, :]
        for b in range(1, N_DEV):
            g = g + sc[:, b:b + 1] * dm_ref[b:b + 1, :]
        g_ref[...] = g
        d_ref[...], nm_ref[...], nv_ref[...] = _adamw_math(w_ref[...], g, m_ref[...], v_ref[...])

    sh = jax.ShapeDtypeStruct((R, C), F32)
    return _pcall(
        body, name="ada_grad_adamw", grid=(R // br,),
        in_specs=[pl.BlockSpec((br, N_DEV), lambda i: (i, 0)), pl.BlockSpec((N_DEV, C), lambda i: (0, 0)), spec, spec, spec],
        out_specs=[spec] * 4, out_shape=[sh] * 4, compiler_params=_params(("parallel",)))(c_t, dmod, w, m, v)


SMALL_ROWS = 16


def _small_sum(small_all, lb_logits):
    def body(s_ref, lbl_ref, o_ref):
        acc = s_ref[0:SMALL_ROWS, :]
        for d in range(1, N_DEV):
            acc = acc + s_ref[d * SMALL_ROWS:(d + 1) * SMALL_ROWS, :]
        o_ref[...] = acc
        z = lbl_ref[...]
        e = jnp.exp(z - jnp.max(z, axis=0, keepdims=True))
        p0 = e[0:1, :] / (e[0:1, :] + e[1:2, :])
        dz = acc[8:9, 0:A_WIDTH] * p0 * (1.0 - p0)
        o_ref[8:9, 0:A_WIDTH] = dz
        o_ref[10:11, 0:A_WIDTH] = -dz

    return _pcall(body, name="small_sum", out_shape=jax.ShapeDtypeStruct((SMALL_ROWS, D_MODEL), F32),
                  in_specs=[pl.BlockSpec(memory_space=pltpu.VMEM)] * 2, out_specs=pl.BlockSpec(memory_space=pltpu.VMEM),
                  compiler_params=_params())(small_all, lb_logits)


RELATIONS = ((1, 0), (0, 1), (1, 1))
ANY = pl.BlockSpec(memory_space=pl.ANY)


def _place():
    x, y, c = lax.axis_index("x"), lax.axis_index("y"), lax.axis_index("c")
    return x, y, c


def _allgather_small(x_shard, name):
    m_per, n = x_shard.shape

    def body(x_ref, out_ref, send_sems, recv_sems, local_sem):
        x, y, c = _place()
        me, sibling = (x, y, c), (x, y, 1 - c)
        chips = [(1 - x, y), (x, 1 - y), (1 - x, 1 - y)]

        def rows(px, py, pc):
            return out_ref.at[pl.ds((4 * px + 2 * py + pc) * m_per, m_per), :]

        def copy(k, block, to, src=None):
            return pltpu.make_async_remote_copy(
                src_ref=rows(*block) if src is None else src, dst_ref=rows(*block),
                send_sem=send_sems.at[k], recv_sem=recv_sems.at[k], device_id=to, device_id_type=MESH)

        mine = pltpu.make_async_copy(x_ref, rows(*me), local_sem)
        mine.start()
        first = [copy(0, me, sibling, src=x_ref)]
        first += [copy(1 + j, me, (*chip, c), src=x_ref) for j, chip in enumerate(chips)]
        for cp in first:
            cp.start()
        passed = [copy(4 + j, (*chip, c), sibling) for j, chip in enumerate(chips)]
        for j, chip in enumerate(chips):
            copy(1 + j, (*chip, c), me).wait_recv()
            passed[j].start()
        copy(0, sibling, me).wait_recv()
        for j, chip in enumerate(chips):
            copy(4 + j, (*chip, 1 - c), me).wait_recv()
        for cp in first + passed:
            cp.wait_send()
        mine.wait()

    return _pcall(
        body, name=name, out_shape=jax.ShapeDtypeStruct((N_DEV * m_per, n), x_shard.dtype),
        in_specs=[pl.BlockSpec(memory_space=pltpu.VMEM)], out_specs=pl.BlockSpec(memory_space=pltpu.VMEM),
        scratch_shapes=[pltpu.SemaphoreType.DMA((7,)), pltpu.SemaphoreType.DMA((7,)), pltpu.SemaphoreType.DMA],
        compiler_params=_params(),
    )(x_shard)


W_SHAPES = ((D_MODEL, IN_WIDTH, True), (A_WIDTH, D_MODEL, True), (B_WIDTH, D_MODEL, True),
            (D_MODEL, D_MODEL, False), (D_MODEL, MLP_HIDDEN, True), (MLP_HIDDEN, D_MODEL, False))
N_W = len(W_SHAPES)


def _shard_shape(w):
    R, C, by_col = W_SHAPES[w]
    return (R, C // N_CHIPS) if by_col else (R // N_CHIPS, C)


def _half_shape(w):
    sr, sc = _shard_shape(w)
    return sr // 2, sc


def _region(full_ref, w, chip, half, quarter=None):
    sr, sc = _shard_shape(w)
    by_col = W_SHAPES[w][2]
    r0, c0 = (0, chip * sc) if by_col else (chip * sr, 0)
    r0, rows = r0 + half * (sr // 2), sr // 2
    if quarter is not None:
        r0, rows = r0 + quarter * (rows // 2), rows // 2
    return full_ref.at[pl.ds(r0, rows), pl.ds(c0, sc)]


def _on_device(fn):
    x, y, c = _place()
    me = 4 * x + 2 * y + c
    for d in range(N_DEV):
        @pl.when(me == d)
        def _(d=d):
            fn(x, y, c, d)


GATHER_COPIES = (
    (0, 0, None, "x"), (0, 0, None, "y"),
    (1, 2, 0, "y"), (1, 1, 1, "x"),
    (1, 2, None, "s"), (1, 1, None, "s"),
    (2, 3, 0, "s"), (2, 3, 1, "s"),
)
PEER_FLIP = {"x": 2, "y": 1, "s": 0}


GATHER_STAGES = {
    None: (((), (0, 1), ()), ((0, 1), (2, 3, 4, 5), ()), ((2, 3), (6, 7), ()), ((4, 5, 6, 7), (), tuple(range(8)))),
    "near": (((), (0, 1), ()), ((0, 1), (), (0, 1))),
    "far": (((), (2, 3, 4, 5), ()), ((2, 3), (6, 7), ()), ((4, 5, 6, 7), (), (2, 3, 4, 5, 6, 7))),
}


def _gather_plan(partials, pass_at=(0.5, 0.75), part=None):
    ws = sorted(partials)
    n_t = len(GATHER_COPIES)
    jobs = [(i, w) for i, w in enumerate(ws)]

    def copy(pi, po, ps, x, y, c, d, i, w, t, landing):
        chip, dc = d >> 1, d & 1
        stage, flip, quarter, to = GATHER_COPIES[t]
        if landing:
            peer_chip = chip ^ PEER_FLIP[to]
            part = _region(po[i], w, peer_chip ^ flip, (1 - dc) if to == "s" else dc, quarter)
            src = part
        else:
            part = _region(po[i], w, chip ^ flip, dc, quarter)
            here = flip != 0 and (part_of is None or stage == 2)
            src = part if here else _region(pi[i], w, chip ^ flip, dc, quarter)
        target = {"x": (x ^ 1, y, c), "y": (x, y ^ 1, c), "s": (x, y, 1 - c)}[to]
        return pltpu.make_async_remote_copy(
            src_ref=src, dst_ref=part, send_sem=ps[0].at[i * n_t + t], recv_sem=ps[1].at[i * n_t + t],
            device_id=target, device_id_type=MESH)

    part_of = part

    def stage(landed, started, sent):
        def run(pi, po, ps):
            def on(x, y, c, d):
                for i, w in jobs:
                    for t in landed:
                        copy(pi, po, ps, x, y, c, d, i, w, t, True).wait_recv()
                for i, w in jobs:
                    for t in started:
                        copy(pi, po, ps, x, y, c, d, i, w, t, False).start()
                for i, w in jobs:
                    for t in sent:
                        copy(pi, po, ps, x, y, c, d, i, w, t, False).wait_send()
            _on_device(on)
        return run

    stages = [stage(*st) for st in GATHER_STAGES[part]]
    mid_at = tuple(pass_at) if part is None else tuple(pass_at)[:len(stages) - 2]
    return _Plan([partials[w] for w in ws], [jax.ShapeDtypeStruct(W_SHAPES[w][:2], BF16) for w in ws],
                 [pltpu.SemaphoreType.DMA((n_t * len(ws),)) for _ in range(2)], stages,
                 {i: i for i in range(len(ws))}, mid_at=mid_at)


def _grad_view(g, w):
    R, C, by_col = W_SHAPES[w]
    return g.reshape(1, 2, R // 2, C) if by_col else g.reshape(N_CHIPS, 2, R // N_CHIPS // 2, C)


def _start_wait_plan(ins, outs, n_copies, copies):
    def start(pi, po, ps):
        for cp in copies(pi, po, ps):
            cp.start()

    def finish(pi, po, ps):
        for cp in copies(pi, po, ps):
            cp.wait()

    return _Plan(ins, outs, [pltpu.SemaphoreType.DMA((n_copies,)), pltpu.SemaphoreType.DMA((n_copies,))], [start, finish])


def _sibling_exchange_plan(g4s):
    pieces = [(i, p) for i, g in enumerate(g4s) for p in range(g.shape[0])]

    def copies(pi, po, ps):
        x, y, c = _place()
        return [pltpu.make_async_remote_copy(
            src_ref=pi[i].at[p, 1 - c], dst_ref=po[i].at[p], send_sem=ps[0].at[n], recv_sem=ps[1].at[n],
            device_id=(x, y, 1 - c), device_id_type=MESH) for n, (i, p) in enumerate(pieces)]

    return _start_wait_plan(list(g4s), [jax.ShapeDtypeStruct((g.shape[0],) + g.shape[2:], F32) for g in g4s],
                            len(pieces), copies)


def _pair_sum(g4, other, c_arr, name):
    P, _, hr, C = g4.shape
    br = _ew_rows(hr, C, 2.5)

    def body(c_ref, g_ref, o_ref, p_ref):
        p_ref[...] = (g_ref[...] + o_ref[...]).astype(BF16)

    return _pcall(
        body, name=name,
        grid_spec=pltpu.PrefetchScalarGridSpec(
            num_scalar_prefetch=1, grid=(P, hr // br),
            in_specs=[pl.BlockSpec((None, None, br, C), lambda p, i, c_ref: (p, c_ref[0], i, 0)),
                      pl.BlockSpec((None, br, C), lambda p, i, c_ref: (p, i, 0))],
            out_specs=pl.BlockSpec((None, br, C), lambda p, i, c_ref: (p, i, 0))),
        out_shape=jax.ShapeDtypeStruct((P, hr, C), BF16),
        compiler_params=_params(("parallel", "parallel")),
    )(c_arr, g4, other)


def _pair_part(p_ref, w, chip):
    sr, sc = _shard_shape(w)
    return p_ref.at[0, :, pl.ds(chip * sc, sc)] if W_SHAPES[w][2] else p_ref.at[chip]


def _chip_exchange_plan(pairs):
    ws = sorted(pairs)
    n = len(ws)

    def stage(wait):
        def run(pi, po, ps):
            def on(x, y, c, d):
                for i, w in enumerate(ws):
                    for k, (rx, ry) in enumerate(RELATIONS):
                        cp = pltpu.make_async_remote_copy(
                            src_ref=_pair_part(pi[i], w, (d >> 1) ^ (2 * rx + ry)), dst_ref=po[i].at[k],
                            send_sem=ps[0].at[i * 3 + k], recv_sem=ps[1].at[i * 3 + k],
                            device_id=(x ^ rx, y ^ ry, c), device_id_type=MESH)
                        if wait:
                            cp.wait()
                        else:
                            cp.start()
            _on_device(on)
        return run

    return _Plan([pairs[w] for w in ws], [jax.ShapeDtypeStruct((3,) + _half_shape(w), BF16) for w in ws],
                 [pltpu.SemaphoreType.DMA((3 * n,)), pltpu.SemaphoreType.DMA((3 * n,))], [stage(False), stage(True)])


def _sum_slots(pair, slots, w, chip_arr, name):
    _, hr, C = slots.shape
    br = _ew_rows(hr, C, 3)
    own_map = (lambda i, chip: (0, i, chip[0])) if W_SHAPES[w][2] else (lambda i, chip: (chip[0], i, 0))

    def body(chip_ref, p_ref, s_ref, o_ref):
        acc = p_ref[...].astype(F32)
        for k in range(3):
            acc = acc + s_ref[k].astype(F32)
        o_ref[...] = acc

    return _pcall(
        body, name=name,
        grid_spec=pltpu.PrefetchScalarGridSpec(
            num_scalar_prefetch=1, grid=(hr // br,),
            in_specs=[pl.BlockSpec((None, br, C), own_map), pl.BlockSpec((3, br, C), lambda i, chip: (0, i, 0))],
            out_specs=pl.BlockSpec((br, C), lambda i, chip: (i, 0))),
        out_shape=jax.ShapeDtypeStruct((hr, C), F32), compiler_params=_params(("parallel",)),
    )(chip_arr, pair, slots)


def _sibling_share_plan(halves):
    def copies(pi, po, ps):
        x, y, c = _place()
        return [pltpu.make_async_remote_copy(
            src_ref=pi[i], dst_ref=po[i], send_sem=ps[0].at[i], recv_sem=ps[1].at[i],
            device_id=(x, y, 1 - c), device_id_type=MESH) for i in range(len(halves))]

    return _start_wait_plan(list(halves), [jax.ShapeDtypeStruct(h.shape, F32) for h in halves], len(halves), copies)


def _pad_lanes(v, width=D_MODEL):
    return jnp.pad(v, ((0, 0), (0, width - v.shape[1])))


def _pack_small(b_ada, norm1, norm2, lb, o_gain, q_gain, k_gain, sinks):
    rows = [b_ada.reshape(N_MOD, D_MODEL), norm1, norm2, jnp.concatenate([lb[0:1], o_gain], axis=1),
            _pad_lanes(jnp.concatenate([q_gain, k_gain, sinks], axis=1)), _pad_lanes(lb[1:2]),
            jnp.zeros((SMALL_ROWS - 11, D_MODEL), F32)]
    return jnp.concatenate(rows, axis=0)


def _unpack_small(p):
    return (p[0:6].reshape(1, N_MOD * D_MODEL), p[6:7], p[7:8],
            jnp.concatenate([p[8:9, 0:A_WIDTH], p[10:11, 0:A_WIDTH]], axis=0), p[8:9, A_WIDTH:],
            p[9:10, 0:64], p[9:10, 64:128], p[9:10, 128:144])


def kernel(x, c, w_ada, b_ada, norm1_gain, w_in, lb_logits, hgrn_o_gain, q_norm_gain, k_norm_gain, sinks, w_branch_a, w_branch_b, w_out, norm2_gain, w_mlp_in, w_mlp_out, loss_target, m_w_ada, m_b_ada, m_norm1_gain, m_w_in, m_lb_logits, m_hgrn_o_gain, m_q_norm_gain, m_k_norm_gain, m_sinks, m_w_branch_a, m_w_branch_b, m_w_out, m_norm2_gain, m_w_mlp_in, m_w_mlp_out, v_w_ada, v_b_ada, v_norm1_gain, v_w_in, v_lb_logits, v_hgrn_o_gain, v_q_norm_gain, v_k_norm_gain, v_sinks, v_w_branch_a, v_w_branch_b, v_w_out, v_norm2_gain, v_w_mlp_in, v_w_mlp_out):
    xi, yi, ci = _place()
    chip = 2 * xi + yi
    me = 4 * xi + 2 * yi + ci
    ada_cols = w_ada.shape[2]

    c_all = _allgather_small(jnp.broadcast_to(c, (8, D_MODEL)), "gather_c").reshape(N_DEV, 8, D_MODEL)[:, 0]
    b_cols = lax.dynamic_slice(b_ada, (0, chip * ada_cols), (1, ada_cols))
    mod_part = _ada_fwd(c_all, w_ada[0], b_cols)
    mod_all = _allgather_small(mod_part, "gather_mod").reshape(N_CHIPS, 2, N_DEV, ada_cols)[:, 0]
    mod_mine = lax.dynamic_index_in_dim(mod_all, me, axis=1, keepdims=False).reshape(N_MOD, D_MODEL)
    mod8 = jnp.concatenate([mod_mine, jnp.zeros((2, D_MODEL), F32)], axis=0)

    shards = (w_in[0], w_branch_a[0], w_branch_b[0], w_out[0], w_mlp_in[0], w_mlp_out[0])
    chip_arr = chip.astype(jnp.int32).reshape(1)
    c_arr = ci.astype(jnp.int32).reshape(1)

    grad_x, halves, theirs, st = _local_step(x[0], loss_target[0], mod8, norm1_gain, norm2_gain, lb_logits, hgrn_o_gain,
                                             q_norm_gain, k_norm_gain, sinks, shards, c_arr, chip_arr)
    loss = lax.psum(0.5 * jnp.sum(st["loss"][0]) / D_MODEL, ("x", "y", "c"))
    moments = ((m_w_in, v_w_in), (m_w_branch_a, v_w_branch_a), (m_w_branch_b, v_w_branch_b), (m_w_out, v_w_out),
               (m_w_mlp_in, v_w_mlp_in), (m_w_mlp_out, v_w_mlp_out))
    big = [_adamw_halves(shards[w], halves[w], theirs[w], moments[w][0][0], moments[w][1][0], c_arr, f"adamw{w}")
           for w in range(N_W)]

    swa = st["swa"]
    small = jnp.concatenate([
        st["n1"][1:2], st["n1"][0:1], st["n2"][3:4], st["n2"][1:2], st["n2"][0:1], st["loss"][1:2],
        st["n1"][2:3], st["n2"][2:3], jnp.concatenate([st["d_lb"][0:1], st["d_og"][0:1]], axis=1),
        _pad_lanes(jnp.concatenate([swa[0:1, 0:64], swa[1:2, 0:64], swa[2:3, 0:16]], axis=1)),
        jnp.zeros((SMALL_ROWS - 10, D_MODEL), F32)], axis=0)
    small_all = _allgather_small(small, "gather_small")
    g_small = _small_sum(small_all, lb_logits)
    small_w = (b_ada, norm1_gain, norm2_gain, lb_logits, hgrn_o_gain, q_norm_gain, k_norm_gain, sinks)
    small_m = (m_b_ada, m_norm1_gain, m_norm2_gain, m_lb_logits, m_hgrn_o_gain, m_q_norm_gain, m_k_norm_gain, m_sinks)
    small_v = (v_b_ada, v_norm1_gain, v_norm2_gain, v_lb_logits, v_hgrn_o_gain, v_q_norm_gain, v_k_norm_gain, v_sinks)
    sm = [_unpack_small(t) for t in
          (g_small,) + tuple(_adamw(_pack_small(*small_w), g_small, _pack_small(*small_m), _pack_small(*small_v),
                                    "adamw_small"))]
    g_b, g_n1, g_n2, g_lb, g_og, g_qg, g_kg, g_sk = ([t[i] for t in sm] for i in range(8))

    dmod_all = small_all.reshape(N_DEV, SMALL_ROWS, D_MODEL)[:, 0:N_MOD].reshape(N_DEV, N_MOD * D_MODEL)
    dmod_cols = lax.dynamic_slice(dmod_all, (0, chip * ada_cols), (N_DEV, ada_cols))
    ada = _ada_grad_adamw(c_all.T, dmod_cols, w_ada[0], m_w_ada[0], v_w_ada[0])

    def ordered(k):
        lead = lambda a: a[None]
        return (lead(ada[k]), g_b[k], g_n1[k], lead(big[0][k]), g_lb[k], g_og[k], g_qg[k], g_kg[k], g_sk[k],
                lead(big[1][k]), lead(big[2][k]), lead(big[3][k]), g_n2[k], lead(big[4][k]), lead(big[5][k]))

    return (loss, grad_x[None]) + ordered(0) + ordered(1) + ordered(2) + ordered(3)
```
